```python
import math
import jax, jax.numpy as jnp
from jax import lax
import numpy as np

D_MODEL = 1024
BATCH = 1
SEQ = 16384
DEPTH = 4

GRID_W = 64
CTX_LEN = 256
MIXERS = ('chunk_mlp', 'mlstm', 'rglru', 'fourier')
N_MIXERS = 4
PREFIX_READERS = ('mlstm', 'rglru')
N_MOD = 6
EPS = 1e-6
POS_BASE = 10000.0
CONV_WIDTH = 4
CM_CHUNK = 128
CM_WIDTH = 1024
CM_GROUPS = 4
ML_HEADS = 4
ML_DK = 128
ML_DV = 256
ML_QK = ML_HEADS * ML_DK
ML_V = ML_HEADS * ML_DV
ML_CHUNK = 128
ML_IN = 2 * ML_QK + 2 * ML_V + 4 * ML_HEADS
LRU_WIDTH = 1280
LRU_HEADS = 10
LRU_BLOCK = LRU_WIDTH // LRU_HEADS
LRU_C = 8.0
FN_WIDTH = 1024
FN_GROUPS = 4
MOE_GROUPS = 4
MOE_EXPERTS_PER_GROUP = 8
MOE_EXPERTS = MOE_GROUPS * MOE_EXPERTS_PER_GROUP
MOE_TOPK = 2
MOE_HIDDEN = 512
MOE_BLOCK = 128

kernel_name = 'hybrid_interleaved_diffusion_block'


def _rms_norm(x, g):
    xf = x.astype(jnp.float32)
    y = xf * lax.rsqrt(jnp.mean(xf * xf, axis=-1, keepdims=True) + EPS)
    return (y * g.astype(jnp.float32)).astype(x.dtype)


def _modulate(h, shift, scale):
    return h * (1 + scale) + shift


def _dwconv_centred(x, w, b):
    k_w = w.shape[0]
    left = k_w // 2
    right = k_w - 1 - left
    t = x.shape[1]
    xp = jnp.pad(x, ((0, 0), (left, right), (0, 0)))
    y = xp[:, 0:t] * w[0]
    for j in range(1, k_w):
        y = y + xp[:, j:j + t] * w[j]
    return y + b


def _grid_pos_embedding(n_tok, d):
    rows = n_tok // GRID_W
    r = jnp.repeat(jnp.arange(rows, dtype=jnp.float32), GRID_W)
    col = jnp.tile(jnp.arange(GRID_W, dtype=jnp.float32), rows)
    q = d // 4
    freq = jnp.exp(-math.log(POS_BASE) * jnp.arange(q, dtype=jnp.float32) / q)
    ar = r[:, None] * freq
    ac = col[:, None] * freq
    return jnp.concatenate([jnp.sin(ar), jnp.cos(ar), jnp.sin(ac), jnp.cos(ac)], axis=-1)


def _chunk_mlp(h, w_in, v_norm_g, w_s, b_s, w_out):
    bsz, t, _ = h.shape
    z = jax.nn.gelu(h @ w_in)
    u = z[..., :CM_WIDTH]
    v = _rms_norm(z[..., CM_WIDTH:], v_norm_g)
    vb = v.reshape(bsz, t // CM_CHUNK, CM_CHUNK, CM_GROUPS, CM_WIDTH // CM_GROUPS)
    s = jnp.einsum('gts,bnsgc->bntgc', w_s, vb) + b_s.T[None, None, :, :, None]
    return (u * s.reshape(bsz, t, CM_WIDTH)) @ w_out


def _fourier_mix(h, w_in, w_out):
    bsz, t, _ = h.shape
    z = (h @ w_in).astype(jnp.float32).reshape(bsz, t, FN_GROUPS, FN_WIDTH // FN_GROUPS)
    f = jnp.fft.fftn(z, axes=(1, 3), norm='ortho').real
    return f.reshape(bsz, t, FN_WIDTH).astype(h.dtype) @ w_out


def _heads_to_chunks(a):
    bsz, t, nh, d = a.shape
    return a.reshape(bsz, t // ML_CHUNK, ML_CHUNK, nh, d).transpose(0, 3, 1, 2, 4)


def _gates_to_chunks(a):
    bsz, t, nh = a.shape
    return a.reshape(bsz, t // ML_CHUNK, ML_CHUNK, nh).transpose(0, 3, 1, 2)


def _chunks_to_heads(a):
    bsz, nh, nc, cl, d = a.shape
    return a.transpose(0, 2, 3, 1, 4).reshape(bsz, nc * cl, nh, d)


def _mlstm_chunk_states(k, v, li, lf, state0):
    b = jnp.cumsum(lf, axis=-1)
    g = b[..., -1]
    a = g[..., None] - b + li
    m_loc = jnp.max(a, axis=-1)
    kw = k * jnp.exp(a - m_loc[..., None])[..., None]
    c_loc = jnp.einsum('bhnlk,bhnlv->bhnkv', kw, v)
    n_loc = jnp.sum(kw, axis=-2)

    def step(carry, inp):
        c_prev, n_prev, m_prev = carry
        cl, nl, ml, gl = inp
        m_new = jnp.maximum(gl + m_prev, ml)
        dec = jnp.exp(gl + m_prev - m_new)
        sc = jnp.exp(ml - m_new)
        c_new = dec[..., None, None] * c_prev + sc[..., None, None] * cl
        n_new = dec[..., None] * n_prev + sc[..., None] * nl
        return (c_new, n_new, m_new), (c_prev, n_prev, m_prev)

    xs = (jnp.moveaxis(c_loc, 2, 0), jnp.moveaxis(n_loc, 2, 0),
          jnp.moveaxis(m_loc, 2, 0), jnp.moveaxis(g, 2, 0))
    final, starts = lax.scan(step, state0, xs)
    starts = tuple(jnp.moveaxis(s, 0, 2) for s in starts)
    return starts, final


def _mlstm_chunk_outputs(q, k, v, li, lf, starts):
    c0, n0, m0 = starts
    cl = q.shape[-2]
    b = jnp.cumsum(lf, axis=-1)
    inter = b + m0[..., None]
    past_in_chunk = jnp.tril(jnp.ones((cl, cl), dtype=bool))
    dlog = jnp.where(past_in_chunk, b[..., :, None] - b[..., None, :] + li[..., None, :], -jnp.inf)
    m = jnp.maximum(inter, jnp.max(dlog, axis=-1))
    s = jnp.einsum('bhntk,bhnsk->bhnts', q, k) * jnp.exp(dlog - m[..., None])
    w_inter = jnp.exp(inter - m)
    num = jnp.einsum('bhnts,bhnsv->bhntv', s, v) + w_inter[..., None] * jnp.einsum('bhntk,bhnkv->bhntv', q, c0)
    den = jnp.sum(s, axis=-1) + w_inter * jnp.einsum('bhntk,bhnk->bhnt', q, n0)
    return num / jnp.maximum(jnp.abs(den), jnp.exp(-m))[..., None]


def _mlstm_direction(q, k, v, li, lf, state0, want_out):
    kc, vc = _heads_to_chunks(k), _heads_to_chunks(v)
    lic, lfc = _gates_to_chunks(li), _gates_to_chunks(lf)
    starts, final = _mlstm_chunk_states(kc, vc, lic, lfc, state0)
    if not want_out:
        return None, final
    h = _mlstm_chunk_outputs(_heads_to_chunks(q), kc, vc, lic, lfc, starts)
    return _chunks_to_heads(h), final


def _mlstm_mixer(h_ctx, h_lat, want_ctx_out, w_in, conv_w, conv_b, gate_b, norm_g, w_out):
    f32 = jnp.float32

    def project(h):
        bsz, t, _ = h.shape
        z = h @ w_in
        qk = jax.nn.silu(_dwconv_centred(z[..., :2 * ML_QK], conv_w, conv_b)).astype(f32)
        q = qk[..., :ML_QK].reshape(bsz, t, ML_HEADS, ML_DK) * (ML_DK ** -0.5)
        k = qk[..., ML_QK:].reshape(bsz, t, ML_HEADS, ML_DK)
        v = z[..., 2 * ML_QK:2 * ML_QK + ML_V].astype(f32).reshape(bsz, t, ML_HEADS, ML_DV)
        o = z[..., 2 * ML_QK + ML_V:2 * ML_QK + 2 * ML_V]
        pre = z[..., 2 * ML_QK + 2 * ML_V:].astype(f32).reshape(bsz, t, 2, 2, ML_HEADS) + gate_b.astype(f32)
        li = pre[:, :, :, 0]
        lf = jax.nn.log_sigmoid(pre[:, :, :, 1])
        return q, k, v, o, li, lf

    def readout(h_sum, o):
        bsz, t = o.shape[:2]
        hn = h_sum * lax.rsqrt(jnp.mean(h_sum * h_sum, axis=-1, keepdims=True) + EPS)
        hn = hn.reshape(bsz, t, ML_V) * norm_g.astype(f32)
        return (hn.astype(o.dtype) * jax.nn.sigmoid(o)) @ w_out

    qc, kc, vc, oc, lic, lfc = project(h_ctx)
    ql, kl, vl, ol, lil, lfl = project(h_lat)
    bsz = h_lat.shape[0]
    zero = (jnp.zeros((bsz, ML_HEADS, ML_DK, ML_DV), f32),
            jnp.zeros((bsz, ML_HEADS, ML_DK), f32),
            jnp.zeros((bsz, ML_HEADS), f32))
    ident = lambda a: a
    rev = lambda a: jnp.flip(a, axis=1)
    outs_ctx, outs_lat = [], []
    for d, order in enumerate((ident, rev)):
        hc, state = _mlstm_direction(order(qc), order(kc), order(vc), order(lic[:, :, d]),
                                     order(lfc[:, :, d]), zero, want_ctx_out)
        hl, _ = _mlstm_direction(order(ql), order(kl), order(vl), order(lil[:, :, d]),
                                 order(lfl[:, :, d]), state, True)
        outs_lat.append(order(hl))
        if want_ctx_out:
            outs_ctx.append(order(hc))
    y_lat = readout(outs_lat[0] + outs_lat[1], ol)
    y_ctx = readout(outs_ctx[0] + outs_ctx[1], oc) if want_ctx_out else None
    return y_ctx, y_lat


def _linrec_combine(left, right):
    a_l, b_l = left
    a_r, b_r = right
    return a_l * a_r, a_r * b_l + b_r


def _rglru_mixer(h_ctx, h_lat, want_ctx_out, w_in, conv_w, conv_b, w_a, b_a, w_x, b_x, lam, w_out):
    f32 = jnp.float32

    def branches(h):
        z = h @ w_in
        return (jax.nn.gelu(z[..., :LRU_WIDTH]),
                _dwconv_centred(z[..., LRU_WIDTH:], conv_w, conv_b).astype(f32))

    def block_diag_gate(xr, w, b):
        bsz, t, _ = xr.shape
        y = jnp.einsum('btnc,ncd->btnd', xr.reshape(bsz, t, LRU_HEADS, LRU_BLOCK), w.astype(f32))
        return jax.nn.sigmoid(y.reshape(bsz, t, LRU_WIDTH) + b.astype(f32))

    def recur(xr, d, h0, reverse):
        r = block_diag_gate(xr, w_a[d], b_a[d])
        i = block_diag_gate(xr, w_x[d], b_x[d])
        log_a = -LRU_C * r * jax.nn.softplus(-lam[d].astype(f32))
        a = jnp.exp(log_a)
        u = jnp.sqrt(-jnp.expm1(2.0 * log_a)) * (i * xr)
        a_cum, u_cum = lax.associative_scan(_linrec_combine, (a, u), reverse=reverse, axis=1)
        return a_cum * h0[:, None, :] + u_cum

    g_c, x_c = branches(h_ctx)
    g_l, x_l = branches(h_lat)
    h0 = jnp.zeros((h_lat.shape[0], LRU_WIDTH), f32)
    hl_sum = 0.0
    hc_sum = 0.0
    for d, reverse in ((0, False), (1, True)):
        hc = recur(x_c, d, h0, reverse)
        state = hc[:, 0] if reverse else hc[:, -1]
        hl_sum = hl_sum + recur(x_l, d, state, reverse)
        if want_ctx_out:
            hc_sum = hc_sum + hc
    y_lat = (g_l * hl_sum.astype(g_l.dtype)) @ w_out
    y_ctx = (g_c * hc_sum.astype(g_c.dtype)) @ w_out if want_ctx_out else None
    return y_ctx, y_lat


def _hier_moe(h, rg_w, rg_b, re_w, re_b, w_gate, w_up, w_down):
    n_tok, d = h.shape
    g_logits = (h @ rg_w + rg_b).astype(jnp.float32)
    grp = jnp.argmax(g_logits, axis=-1)
    p_grp = jnp.take_along_axis(jax.nn.softmax(g_logits, axis=-1), grp[:, None], axis=-1)
    e_logits = (h @ re_w + re_b).astype(jnp.float32).reshape(n_tok, MOE_GROUPS, MOE_EXPERTS_PER_GROUP)
    e_in_grp = jnp.take_along_axis(e_logits, grp[:, None, None], axis=1)[:, 0]
    top_val, top_idx = lax.top_k(e_in_grp, MOE_TOPK)
    weight = (jax.nn.softmax(top_val, axis=-1) * p_grp).astype(h.dtype)
    expert = grp[:, None] * MOE_EXPERTS_PER_GROUP + top_idx
    n_asg = n_tok * MOE_TOPK
    flat_e = expert.reshape(n_asg)
    order = jnp.argsort(flat_e)
    se = flat_e[order]
    stok = order // MOE_TOPK
    sw = weight.reshape(n_asg)[order]
    counts = jnp.bincount(flat_e, length=MOE_EXPERTS)
    seg_start = jnp.cumsum(counts) - counts
    padded = (counts + MOE_BLOCK - 1) // MOE_BLOCK * MOE_BLOCK
    pad_end = jnp.cumsum(padded)
    pad_start = pad_end - padded
    dest = pad_start[se] + jnp.arange(n_asg) - seg_start[se]
    n_blocks = -(-(n_asg + MOE_EXPERTS * (MOE_BLOCK - 1)) // MOE_BLOCK)
    buf = jnp.zeros((n_blocks * MOE_BLOCK, d), h.dtype).at[dest].set(h[stok])
    blk_expert = jnp.minimum(jnp.searchsorted(pad_end, jnp.arange(n_blocks) * MOE_BLOCK, side='right'),
                             MOE_EXPERTS - 1)

    def expert_block(args):
        xb, e = args
        return (jax.nn.silu(xb @ w_gate[e]) * (xb @ w_up[e])) @ w_down[e]

    ybuf = lax.map(expert_block, (buf.reshape(n_blocks, MOE_BLOCK, d), blk_expert)).reshape(-1, d)
    return jnp.zeros_like(h).at[stok].add(ybuf[dest] * sw[:, None])


def setup_inputs(seed: int = 0) -> dict:
    key = jax.random.key(seed)
    keys = iter(jax.random.split(key, 64))

    def nrm(shape, scale):
        return jax.random.normal(next(keys), shape, jnp.float32) * scale

    d = D_MODEL
    n_a, n_b, n_c, n_d = (len(range(m, DEPTH, N_MIXERS)) for m in range(N_MIXERS))
    ml_gate_b = nrm((n_b, 2, 2, ML_HEADS), 0.1).at[:, :, 1].add(jnp.linspace(3.0, 6.0, ML_HEADS))
    u = jax.random.uniform(next(keys), (n_c, 2, LRU_WIDTH), jnp.float32, 0.9, 0.999)
    p = u ** (1.0 / LRU_C)
    lru_lambda = jnp.log(p) - jnp.log1p(-p)
    return {
        'x': nrm((BATCH, SEQ, d), 1.0),
        'c': nrm((BATCH, d), 1.0),
        'ctx': nrm((BATCH, CTX_LEN, d), 1.0),
        'c_ctx': nrm((d,), 1.0),
        'ada_w': nrm((DEPTH, d, N_MOD * d), 0.5 * d ** -0.5),
        'ada_b': nrm((DEPTH, N_MOD * d), 0.02),
        'norm_mix_g': 1.0 + nrm((DEPTH, d), 0.1),
        'norm_ffn_g': 1.0 + nrm((DEPTH, d), 0.1),
        'final_norm_g': 1.0 + nrm((d,), 0.1),
        'router_group_w': nrm((DEPTH, d, MOE_GROUPS), d ** -0.5),
        'router_group_b': nrm((DEPTH, MOE_GROUPS), 0.01),
        'router_expert_w': nrm((DEPTH, d, MOE_EXPERTS), d ** -0.5),
        'router_expert_b': nrm((DEPTH, MOE_EXPERTS), 0.01),
        'expert_w_gate': nrm((DEPTH, MOE_EXPERTS, d, MOE_HIDDEN), d ** -0.5),
        'expert_w_up': nrm((DEPTH, MOE_EXPERTS, d, MOE_HIDDEN), d ** -0.5),
        'expert_w_down': nrm((DEPTH, MOE_EXPERTS, MOE_HIDDEN, d), MOE_HIDDEN ** -0.5),
        'cm_w_in': nrm((n_a, d, 2 * CM_WIDTH), d ** -0.5),
        'cm_v_norm_g': 1.0 + nrm((n_a, CM_WIDTH), 0.1),
        'cm_w_s': nrm((n_a, CM_GROUPS, CM_CHUNK, CM_CHUNK), CM_CHUNK ** -0.5),
        'cm_b_s': nrm((n_a, CM_GROUPS, CM_CHUNK), 0.02),
        'cm_w_out': nrm((n_a, CM_WIDTH, d), CM_WIDTH ** -0.5),
        'ml_w_in': nrm((n_b, d, ML_IN), d ** -0.5),
        'ml_conv_w': nrm((n_b, CONV_WIDTH, 2 * ML_QK), CONV_WIDTH ** -0.5),
        'ml_conv_b': nrm((n_b, 2 * ML_QK), 0.02),
        'ml_gate_b': ml_gate_b,
        'ml_norm_g': 1.0 + nrm((n_b, ML_V), 0.1),
        'ml_w_out': nrm((n_b, ML_V, d), ML_V ** -0.5),
        'lru_w_in': nrm((n_c, d, 2 * LRU_WIDTH), d ** -0.5),
        'lru_conv_w': nrm((n_c, CONV_WIDTH, LRU_WIDTH), CONV_WIDTH ** -0.5),
        'lru_conv_b': nrm((n_c, LRU_WIDTH), 0.02),
        'lru_w_a': nrm((n_c, 2, LRU_HEADS, LRU_BLOCK, LRU_BLOCK), LRU_BLOCK ** -0.5),
        'lru_b_a': nrm((n_c, 2, LRU_WIDTH), 0.02),
        'lru_w_x': nrm((n_c, 2, LRU_HEADS, LRU_BLOCK, LRU_BLOCK), LRU_BLOCK ** -0.5),
        'lru_b_x': nrm((n_c, 2, LRU_WIDTH), 0.02),
        'lru_lambda': lru_lambda,
        'lru_w_out': nrm((n_c, LRU_WIDTH, d), LRU_WIDTH ** -0.5),
        'fn_w_in': nrm((n_d, d, FN_WIDTH), d ** -0.5),
        'fn_w_out': nrm((n_d, FN_WIDTH, d), FN_WIDTH ** -0.5),
    }


def reference(x, c, ctx, c_ctx, ada_w, ada_b, norm_mix_g, norm_ffn_g, final_norm_g,
              router_group_w, router_group_b, router_expert_w, router_expert_b,
              expert_w_gate, expert_w_up, expert_w_down,
              cm_w_in, cm_v_norm_g, cm_w_s, cm_b_s, cm_w_out,
              ml_w_in, ml_conv_w, ml_conv_b, ml_gate_b, ml_norm_g, ml_w_out,
              lru_w_in, lru_conv_w, lru_conv_b, lru_w_a, lru_b_a, lru_w_x, lru_b_x, lru_lambda, lru_w_out,
              fn_w_in, fn_w_out):
    bsz, seq, d = x.shape
    n_ctx = ctx.shape[1]
    lat = x + _grid_pos_embedding(seq, d).astype(x.dtype)[None]
    cx = ctx
    readers = [i for i in range(DEPTH) if MIXERS[i % N_MIXERS] in PREFIX_READERS]
    last_reader = readers[-1] if readers else -1
    silu_c = jax.nn.silu(c)
    silu_cc = jax.nn.silu(c_ctx)
    for i in range(DEPTH):
        kind = MIXERS[i % N_MIXERS]
        j = i // N_MIXERS
        ctx_in = i <= last_reader
        ctx_upd = i < last_reader
        mod_l = jnp.split((silu_c @ ada_w[i] + ada_b[i])[:, None, :], N_MOD, axis=-1)
        hl = _modulate(_rms_norm(lat, norm_mix_g[i]), mod_l[0], mod_l[1])
        hc = None
        mod_c = None
        if ctx_in:
            mod_c = jnp.split((silu_cc @ ada_w[i] + ada_b[i])[None, None, :], N_MOD, axis=-1)
            hc = _modulate(_rms_norm(cx, norm_mix_g[i]), mod_c[0], mod_c[1])
        if kind == 'chunk_mlp':
            prm = (cm_w_in[j], cm_v_norm_g[j], cm_w_s[j], cm_b_s[j], cm_w_out[j])
            yl = _chunk_mlp(hl, *prm)
            yc = _chunk_mlp(hc, *prm) if ctx_upd else None
        elif kind == 'fourier':
            yl = _fourier_mix(hl, fn_w_in[j], fn_w_out[j])
            yc = _fourier_mix(hc, fn_w_in[j], fn_w_out[j]) if ctx_upd else None
        elif kind == 'mlstm':
            yc, yl = _mlstm_mixer(hc, hl, ctx_upd, ml_w_in[j], ml_conv_w[j], ml_conv_b[j],
                                  ml_gate_b[j], ml_norm_g[j], ml_w_out[j])
        else:
            yc, yl = _rglru_mixer(hc, hl, ctx_upd, lru_w_in[j], lru_conv_w[j], lru_conv_b[j],
                                  lru_w_a[j], lru_b_a[j], lru_w_x[j], lru_b_x[j], lru_lambda[j],
                                  lru_w_out[j])
        lat = lat + mod_l[2] * yl
        if ctx_upd:
            cx = cx + mod_c[2] * yc
        moe_p = (router_group_w[i], router_group_b[i], router_expert_w[i], router_expert_b[i],
                 expert_w_gate[i], expert_w_up[i], expert_w_down[i])
        hl2 = _modulate(_rms_norm(lat, norm_ffn_g[i]), mod_l[3], mod_l[4])
        if ctx_upd:
            hc2 = _modulate(_rms_norm(cx, norm_ffn_g[i]), mod_c[3], mod_c[4])
            tok = jnp.concatenate([hc2, hl2], axis=1)
            y = _hier_moe(tok.reshape(-1, d), *moe_p).reshape(tok.shape)
            cx = cx + mod_c[5] * y[:, :n_ctx]
            lat = lat + mod_l[5] * y[:, n_ctx:]
        else:
            y = _hier_moe(hl2.reshape(-1, d), *moe_p).reshape(hl2.shape)
            lat = lat + mod_l[5] * y
    return _rms_norm(lat, final_norm_g)
```

```python
import functools
import math

import jax
import jax.numpy as jnp
import numpy as np
from jax import lax
from jax.experimental import pallas as pl
from jax.experimental.pallas import tpu as pltpu

F32 = jnp.float32
BF16 = jnp.bfloat16

EPS = 1e-6
POS_BASE = 10000.0
GRID_W = 64
N_MOD = 6
TM = 256
LANES = 128
SUBLANES = 8
VMEM_LIMIT = 56 * 1024 * 1024

CM_CHUNK = 128
CM_GROUPS = 4
ML_HEADS = 4
ML_DK = 128
ML_DV = 256
ML_CHUNK = 128
LRU_HEADS = 10
LRU_BLOCK = 128
LRU_C = 8.0
FN_GROUPS = 4
FFT_N2 = 128
MOE_GROUPS = 4
MOE_EPG = 8
MOE_EXPERTS = MOE_GROUPS * MOE_EPG
MOE_ROWS = 256
CONV_LEFT = 2

HI = lax.Precision.HIGHEST


def _cparams(*sem):
    return pltpu.CompilerParams(dimension_semantics=sem, vmem_limit_bytes=VMEM_LIMIT)


def _full(shape):
    nd = len(shape)
    return pl.BlockSpec(shape, lambda *_: (0,) * nd)


def _rms(x, g):
    return x * lax.rsqrt(jnp.mean(x * x, axis=-1, keepdims=True) + EPS) * g


def _gelu(x):
    c = math.sqrt(2.0 / math.pi)
    return 0.5 * x * (1.0 + jnp.tanh(c * (x + 0.044715 * (x * x * x))))


def _sigmoid(x):
    return 1.0 / (1.0 + jnp.exp(-x))


def _silu(x):
    return x * _sigmoid(x)


def _softplus(x):
    return jnp.maximum(x, 0.0) + jnp.log(1.0 + jnp.exp(-jnp.abs(x)))


def _mod_rows(mod_ref, tile, ctx_tiles, first):
    row = jnp.where(tile < ctx_tiles, 0, 1)
    m = mod_ref[pl.ds(row, 1), :]
    d = m.shape[1] // N_MOD
    return tuple(m[:, (first + j) * d:(first + j + 1) * d] for j in range(3))


def _bdot(a, b):
    return jnp.dot(a.astype(BF16), b.astype(BF16), preferred_element_type=F32)


def _ada_kernel(c_ref, w_ref, b_ref, o_ref):
    c = c_ref[...]
    o_ref[...] = jnp.dot(_silu(c), w_ref[...], precision=HI,
                         preferred_element_type=F32) + b_ref[...]


def _ada_table(c_rows, ada_w, ada_b):
    depth, d, n = ada_w.shape
    tn = 1024
    return pl.pallas_call(
        _ada_kernel,
        out_shape=jax.ShapeDtypeStruct((depth, SUBLANES, n), F32),
        grid=(depth, n // tn),
        in_specs=[_full((SUBLANES, d)),
                  pl.BlockSpec((None, d, tn), lambda i, j: (i, 0, j)),
                  pl.BlockSpec((None, 1, tn), lambda i, j: (i, 0, j))],
        out_specs=pl.BlockSpec((None, SUBLANES, tn), lambda i, j: (i, 0, j)),
        compiler_params=_cparams("arbitrary", "arbitrary"),
        name="ada_table",
    )(c_rows, ada_w, ada_b.reshape(depth, 1, n))


def _prep_kernel(x_ref, ctx_ref, rt_ref, ct_ref, o_ref):
    i = pl.program_id(0)

    @pl.when(i == 0)
    def _():
        o_ref[...] = ctx_ref[...]

    @pl.when(i > 0)
    def _():
        rows_per_tile = TM // GRID_W
        q2 = rt_ref.shape[1]
        r0 = (i - 1) * rows_per_tile
        rt = jnp.concatenate(
            [jnp.broadcast_to(rt_ref[pl.ds(r0 + j, 1), :], (GRID_W, q2))
             for j in range(rows_per_tile)], axis=0)
        ct = jnp.concatenate([ct_ref[...]] * rows_per_tile, axis=0)
        o_ref[...] = x_ref[...] + jnp.concatenate([rt, ct], axis=1)


def _prep_stream(x2, ctx2):
    seq, d = x2.shape
    n_ctx = ctx2.shape[0]
    assert n_ctx == TM and seq % TM == 0 and TM % GRID_W == 0
    q = d // 4
    freq = jnp.exp(-math.log(POS_BASE) * jnp.arange(q, dtype=F32) / q)
    ar = jnp.arange(seq // GRID_W, dtype=F32)[:, None] * freq
    ac = jnp.arange(GRID_W, dtype=F32)[:, None] * freq
    rt = jnp.concatenate([jnp.sin(ar), jnp.cos(ar)], axis=-1)
    ct = jnp.concatenate([jnp.sin(ac), jnp.cos(ac)], axis=-1)
    nt = 1 + seq // TM
    return pl.pallas_call(
        _prep_kernel,
        out_shape=jax.ShapeDtypeStruct((n_ctx + seq, d), F32),
        grid=(nt,),
        in_specs=[pl.BlockSpec((TM, d), lambda i: (jnp.maximum(i - 1, 0), 0)),
                  _full((TM, d)), _full(rt.shape), _full(ct.shape)],
        out_specs=pl.BlockSpec((TM, d), lambda i: (i, 0)),
        compiler_params=_cparams("arbitrary"),
        name="prep_stream",
    )(x2, ctx2, rt, ct)


def _cm_kernel(s_ref, mod_ref, g_ref, win_ref, vg_ref, ws_ref, bs_ref, wout_ref, o_ref, p_scr):
    i = pl.program_id(0)
    shift, scale, gate = _mod_rows(mod_ref, i, 1, 0)
    x = s_ref[...]
    h = _rms(x, g_ref[...]) * (1.0 + scale) + shift
    z = _gelu(_bdot(h, win_ref[...]))
    w = z.shape[1] // 2
    u = z[:, :w]
    v = _rms(z[:, w:], vg_ref[...]).astype(BF16)
    gw = w // CM_GROUPS
    for c in range(TM // CM_CHUNK):
        r = slice(c * CM_CHUNK, (c + 1) * CM_CHUNK)
        for g in range(CM_GROUPS):
            cs = slice(g * gw, (g + 1) * gw)
            s = jnp.dot(ws_ref[g], v[r, cs], preferred_element_type=F32) + bs_ref[:, g:g + 1]
            p_scr[r, cs] = (u[r, cs] * s).astype(BF16)
    y = jnp.dot(p_scr[...], wout_ref[...], preferred_element_type=F32)
    o_ref[...] = x + gate * y


def _chunk_mlp_layer(s, mods, norm_g, w_in, v_g, w_s, b_s, w_out):
    n, d = s.shape
    w = w_out.shape[0]
    return pl.pallas_call(
        _cm_kernel,
        out_shape=jax.ShapeDtypeStruct((n, d), F32),
        grid=(n // TM,),
        in_specs=[pl.BlockSpec((TM, d), lambda i: (i, 0)),
                  _full(mods.shape), _full((1, d)), _full(w_in.shape), _full((1, w)),
                  _full(w_s.shape), _full((CM_CHUNK, CM_GROUPS)), _full(w_out.shape)],
        out_specs=pl.BlockSpec((TM, d), lambda i: (i, 0)),
        scratch_shapes=[pltpu.VMEM((TM, w), BF16)],
        compiler_params=_cparams("arbitrary"),
        name="chunk_mlp",
    )(s, mods, norm_g.reshape(1, d), w_in.astype(BF16), v_g.reshape(1, w),
      w_s.astype(BF16), b_s.T, w_out.astype(BF16))


def _router_kernel(s_ref, mod_ref, g_ref, rw_ref, rb_ref, h_ref, info_ref, wt_ref, cnt_ref,
                   carry, *, ctx_tiles):
    i = pl.program_id(0)

    @pl.when(i == 0)
    def _():
        carry[...] = jnp.zeros_like(carry)

    shift, scale, _ = _mod_rows(mod_ref, i, ctx_tiles, 3)
    h = _rms(s_ref[...], g_ref[...]) * (1.0 + scale) + shift
    h_ref[...] = h
    logits = jnp.dot(h, rw_ref[...], precision=HI, preferred_element_type=F32) + rb_ref[...]
    lane = lax.broadcasted_iota(jnp.int32, logits.shape, 1)
    neg = jnp.float32(-jnp.inf)
    big = jnp.int32(1 << 20)
    is_g = lane < MOE_GROUPS
    gl = jnp.where(is_g, logits, neg)
    gmax = jnp.max(gl, axis=-1, keepdims=True)
    grp = jnp.min(jnp.where(is_g & (gl == gmax), lane, big), axis=-1, keepdims=True)
    p_grp = 1.0 / jnp.sum(jnp.exp(gl - gmax), axis=-1, keepdims=True)
    e_lane = lane - MOE_GROUPS
    in_grp = (e_lane >= 0) & (e_lane < MOE_EXPERTS) & ((e_lane >> 3) == grp)
    l1 = jnp.where(in_grp, logits, neg)
    v1 = jnp.max(l1, axis=-1, keepdims=True)
    i1 = jnp.min(jnp.where(in_grp & (l1 == v1), lane, big), axis=-1, keepdims=True)
    rest = in_grp & (lane != i1)
    l2 = jnp.where(rest, logits, neg)
    v2 = jnp.max(l2, axis=-1, keepdims=True)
    i2 = jnp.min(jnp.where(rest & (l2 == v2), lane, big), axis=-1, keepdims=True)
    e21 = jnp.exp(v2 - v1)
    w1 = p_grp / (1.0 + e21)
    w2 = p_grp * e21 / (1.0 + e21)
    oh1 = (lane == i1).astype(F32)
    oh2 = (lane == i2).astype(F32)
    oh = oh1 + oh2
    ri = lax.broadcasted_iota(jnp.int32, (TM, TM), 0)
    ci = lax.broadcasted_iota(jnp.int32, (TM, TM), 1)
    tri = (ci < ri).astype(BF16)
    before = jnp.dot(tri, oh.astype(BF16), preferred_element_type=F32) + carry[0:1, :]
    r1 = jnp.sum(oh1 * before, axis=-1, keepdims=True).astype(jnp.int32)
    r2 = jnp.sum(oh2 * before, axis=-1, keepdims=True).astype(jnp.int32)
    carry[0:1, :] = carry[0:1, :] + jnp.sum(oh, axis=0, keepdims=True)
    info_ref[...] = jnp.where(lane == 0, i1 - MOE_GROUPS,
                              jnp.where(lane == 1, i2 - MOE_GROUPS,
                                        jnp.where(lane == 2, r1, jnp.where(lane == 3, r2, 0))))
    wt_ref[...] = jnp.where(lane == 0, w1, jnp.where(lane == 1, w2, 0.0))
    cnt_ref[...] = jnp.broadcast_to(carry[0:1, :], cnt_ref.shape)


def _router(s, mods, norm_g, rg_w, rg_b, re_w, re_b, ctx_tiles):
    n, d = s.shape
    pad = LANES - MOE_GROUPS - MOE_EXPERTS
    rw = jnp.concatenate([rg_w, re_w, jnp.zeros((d, pad), F32)], axis=1)
    rb = jnp.concatenate([rg_b, re_b, jnp.zeros((pad,), F32)]).reshape(1, LANES)
    tile = pl.BlockSpec((TM, d), lambda i: (i, 0))
    small = pl.BlockSpec((TM, LANES), lambda i: (i, 0))
    return pl.pallas_call(
        functools.partial(_router_kernel, ctx_tiles=ctx_tiles),
        out_shape=(jax.ShapeDtypeStruct((n, d), F32),
                   jax.ShapeDtypeStruct((n, LANES), jnp.int32),
                   jax.ShapeDtypeStruct((n, LANES), F32),
                   jax.ShapeDtypeStruct((SUBLANES, LANES), F32)),
        grid=(n // TM,),
        in_specs=[tile, _full(mods.shape), _full((1, d)), _full((d, LANES)), _full((1, LANES))],
        out_specs=(tile, small, small, _full((SUBLANES, LANES))),
        scratch_shapes=[pltpu.VMEM((SUBLANES, LANES), F32)],
        compiler_params=_cparams("arbitrary"),
        name="moe_router",
    )(s, mods, norm_g.reshape(1, d), rw, rb)


def _row_copy(src, r, dst, d, sem):
    return pltpu.make_async_copy(src.at[pl.ds(r, 1), :], dst.at[pl.ds(d, 1), :], sem)


def _dispatch_kernel(dest_ref, h_ref, xs_in, xs_out, sem):
    del xs_in
    base = pl.program_id(0) * (2 * TM)

    def start(r, c):
        _row_copy(h_ref, r, xs_out, dest_ref[base + 2 * r], sem).start()
        _row_copy(h_ref, r, xs_out, dest_ref[base + 2 * r + 1], sem).start()
        return c

    lax.fori_loop(0, TM, start, 0)

    def wait(r, c):
        _row_copy(h_ref, r, xs_out, dest_ref[base + 2 * r], sem).wait()
        _row_copy(h_ref, r, xs_out, dest_ref[base + 2 * r + 1], sem).wait()
        return c

    lax.fori_loop(0, TM, wait, 0)


def _dispatch(dest, h, n_rows):
    n, d = h.shape
    xs0 = jnp.zeros((n_rows, d), F32)
    return pl.pallas_call(
        _dispatch_kernel,
        out_shape=jax.ShapeDtypeStruct((n_rows, d), F32),
        grid_spec=pltpu.PrefetchScalarGridSpec(
            num_scalar_prefetch=1,
            grid=(n // TM,),
            in_specs=[pl.BlockSpec((TM, d), lambda i, dst: (i, 0)),
                      pl.BlockSpec(memory_space=pl.ANY)],
            out_specs=pl.BlockSpec(memory_space=pl.ANY),
            scratch_shapes=[pltpu.SemaphoreType.DMA]),
        input_output_aliases={2: 0},
        compiler_params=_cparams("arbitrary"),
        name="moe_dispatch",
    )(dest, h, xs0)


def _expert_kernel(be_ref, nu_ref, x_ref, wg_ref, wu_ref, wd_ref, y_ref, wg_s, wu_s, wd_s):
    b = pl.program_id(0)
    used = b < nu_ref[0]
    prev = be_ref[jnp.maximum(b - 1, 0)]
    fresh = (b == 0) | (be_ref[b] != prev)

    @pl.when(used & fresh)
    def _():
        wg_s[...] = wg_ref[...].astype(BF16)
        wu_s[...] = wu_ref[...].astype(BF16)
        wd_s[...] = wd_ref[...].astype(BF16)

    @pl.when(used)
    def _():
        x = x_ref[...].astype(BF16)
        a = jnp.dot(x, wg_s[...], preferred_element_type=F32)
        u = jnp.dot(x, wu_s[...], preferred_element_type=F32)
        y_ref[...] = jnp.dot((_silu(a) * u).astype(BF16), wd_s[...], preferred_element_type=F32)

    @pl.when(jnp.logical_not(used))
    def _():
        y_ref[...] = jnp.zeros_like(y_ref)


def _experts(blk_expert, n_used, xs, w_gate, w_up, w_down):
    n_rows, d = xs.shape
    hid = w_gate.shape[2]
    nb = n_rows // MOE_ROWS
    return pl.pallas_call(
        _expert_kernel,
        out_shape=jax.ShapeDtypeStruct((n_rows, d), F32),
        grid_spec=pltpu.PrefetchScalarGridSpec(
            num_scalar_prefetch=2,
            grid=(nb,),
            in_specs=[pl.BlockSpec((MOE_ROWS, d), lambda b, be, nu: (b, 0)),
                      pl.BlockSpec((None, d, hid), lambda b, be, nu: (be[b], 0, 0)),
                      pl.BlockSpec((None, d, hid), lambda b, be, nu: (be[b], 0, 0)),
                      pl.BlockSpec((None, hid, d), lambda b, be, nu: (be[b], 0, 0))],
            out_specs=pl.BlockSpec((MOE_ROWS, d), lambda b, be, nu: (b, 0)),
            scratch_shapes=[pltpu.VMEM((d, hid), BF16), pltpu.VMEM((d, hid), BF16),
                            pltpu.VMEM((hid, d), BF16)]),
        compiler_params=_cparams("arbitrary"),
        name="moe_experts",
    )(blk_expert, n_used, xs, w_gate, w_up, w_down)


def _combine_kernel(dest_ref, s_ref, wt_ref, mod_ref, fg_ref, ys_ref, o_ref, y0, y1, sem,
                    *, ctx_tiles, final_norm):
    i = pl.program_id(0)
    base = i * (2 * TM)

    def start(r, c):
        _row_copy(ys_ref, dest_ref[base + 2 * r], y0, r, sem).start()
        _row_copy(ys_ref, dest_ref[base + 2 * r + 1], y1, r, sem).start()
        return c

    lax.fori_loop(0, TM, start, 0)

    def wait(r, c):
        _row_copy(ys_ref, dest_ref[base + 2 * r], y0, r, sem).wait()
        _row_copy(ys_ref, dest_ref[base + 2 * r + 1], y1, r, sem).wait()
        return c

    lax.fori_loop(0, TM, wait, 0)
    gate = _mod_rows(mod_ref, i, ctx_tiles, 3)[2]
    wt = wt_ref[...]
    y = wt[:, 0:1] * y0[...] + wt[:, 1:2] * y1[...]
    out = s_ref[...] + gate * y
    if final_norm:
        out = _rms(out, fg_ref[...])
    o_ref[...] = out


def _combine(dest, s, wts, mods, final_g, ys, ctx_tiles, final_norm):
    n, d = s.shape
    return pl.pallas_call(
        functools.partial(_combine_kernel, ctx_tiles=ctx_tiles, final_norm=final_norm),
        out_shape=jax.ShapeDtypeStruct((n, d), F32),
        grid_spec=pltpu.PrefetchScalarGridSpec(
            num_scalar_prefetch=1,
            grid=(n // TM,),
            in_specs=[pl.BlockSpec((TM, d), lambda i, dst: (i, 0)),
                      pl.BlockSpec((TM, LANES), lambda i, dst: (i, 0)),
                      pl.BlockSpec(mods.shape, lambda i, dst: (0, 0)),
                      pl.BlockSpec((1, d), lambda i, dst: (0, 0)),
                      pl.BlockSpec(memory_space=pl.ANY)],
            out_specs=pl.BlockSpec((TM, d), lambda i, dst: (i, 0)),
            scratch_shapes=[pltpu.VMEM((TM, d), F32), pltpu.VMEM((TM, d), F32),
                            pltpu.SemaphoreType.DMA]),
        compiler_params=_cparams("arbitrary"),
        name="moe_combine",
    )(dest, s, wts, mods, final_g.reshape(1, d), ys)


def _moe_layer(s, mods, norm_g, rg_w, rg_b, re_w, re_b, w_gate, w_up, w_down, ctx_tiles,
               final_g, final_norm):
    n, d = s.shape
    h, info, wts, cnt = _router(s, mods, norm_g, rg_w, rg_b, re_w, re_b, ctx_tiles)
    counts = cnt[0, MOE_GROUPS:MOE_GROUPS + MOE_EXPERTS].astype(jnp.int32)
    padded = (counts + MOE_ROWS - 1) // MOE_ROWS * MOE_ROWS
    pad_end = jnp.cumsum(padded)
    pad_start = pad_end - padded
    dest = (pad_start[info[:, 0:2]] + info[:, 2:4]).reshape(2 * n)
    nb = (2 * n + MOE_EXPERTS * (MOE_ROWS - 1)) // MOE_ROWS + 1
    blk_expert = jnp.minimum(
        jnp.searchsorted(pad_end, jnp.arange(nb, dtype=jnp.int32) * MOE_ROWS, side='right'),
        MOE_EXPERTS - 1).astype(jnp.int32)
    n_used = (pad_end[-1:] // MOE_ROWS).astype(jnp.int32)
    xs = _dispatch(dest, h, nb * MOE_ROWS)
    ys = _experts(blk_expert, n_used, xs, w_gate, w_up, w_down)
    return _combine(dest, s, wts, mods, final_g, ys, ctx_tiles, final_norm)


def _conv_tile(x, prev_ref, next_ref, has_prev, has_next, w_ref, b_ref):
    rows = x.shape[0]
    ridx = lax.broadcasted_iota(jnp.int32, x.shape, 0)
    pm = jnp.where(has_prev, 1.0, 0.0)
    nm = jnp.where(has_next, 1.0, 0.0)
    p2 = prev_ref[SUBLANES - 2:SUBLANES - 1, :] * pm
    p1 = prev_ref[SUBLANES - 1:SUBLANES, :] * pm
    n1 = next_ref[0:1, :] * nm
    xm1 = jnp.where(ridx == 0, p1, pltpu.roll(x, 1, axis=0))
    xm2 = jnp.where(ridx == 0, p2, jnp.where(ridx == 1, p1, pltpu.roll(x, 2, axis=0)))
    xp1 = jnp.where(ridx == rows - 1, n1, pltpu.roll(x, rows - 1, axis=0))
    return (xm2 * w_ref[0:1, :] + xm1 * w_ref[1:2, :] + x * w_ref[2:3, :]
            + xp1 * w_ref[3:4, :] + b_ref[...])


def _ml_proj_kernel(s_ref, mod_ref, g_ref, w_ref, wg_ref, gb_ref, qk_ref, v_ref, o_ref, gt_ref):
    i = pl.program_id(0)
    shift, scale, _ = _mod_rows(mod_ref, i, 1, 0)
    h = _rms(s_ref[...], g_ref[...]) * (1.0 + scale) + shift
    z = _bdot(h, w_ref[...])
    nqk = qk_ref.shape[1]
    nv = v_ref.shape[1]
    qk_ref[...] = z[:, :nqk]
    v_ref[...] = z[:, nqk:nqk + nv].astype(BF16)
    o_ref[...] = z[:, nqk + nv:]
    pre = jnp.dot(h, wg_ref[...], precision=HI, preferred_element_type=F32) + gb_ref[...]
    lane = lax.broadcasted_iota(jnp.int32, pre.shape, 1)
    is_forget = ((lane >> 2) & 1) == 1
    gt_ref[...] = jnp.where(is_forget, -_softplus(-pre), pre)


def _ml_proj(s, mods, norm_g, w_in, gate_b):
    n, d = s.shape
    nqk = 2 * ML_HEADS * ML_DK
    nv = ML_HEADS * ML_DV
    n_main = nqk + 2 * nv
    n_gate = w_in.shape[1] - n_main
    w_main = w_in[:, :n_main].astype(BF16)
    w_gate = jnp.concatenate([w_in[:, n_main:], jnp.zeros((d, LANES - n_gate), F32)], axis=1)
    gb = jnp.concatenate([gate_b.reshape(n_gate), jnp.zeros((LANES - n_gate,), F32)]).reshape(1, LANES)
    tile = lambda w: pl.BlockSpec((TM, w), lambda i: (i, 0))
    return pl.pallas_call(
        _ml_proj_kernel,
        out_shape=(jax.ShapeDtypeStruct((n, nqk), F32), jax.ShapeDtypeStruct((n, nv), BF16),
                   jax.ShapeDtypeStruct((n, nv), F32), jax.ShapeDtypeStruct((n, LANES), F32)),
        grid=(n // TM,),
        in_specs=[tile(d), _full(mods.shape), _full((1, d)), _full(w_main.shape),
                  _full((d, LANES)), _full((1, LANES))],
        out_specs=(tile(nqk), tile(nv), tile(nv), tile(LANES)),
        compiler_params=_cparams("arbitrary"),
        name="mlstm_proj",
    )(s, mods, norm_g.reshape(1, d), w_main, w_gate, gb)


def _ml_chunk_index(j, n_chunks, ctx_chunks, reverse):
    if not reverse:
        return j
    return jnp.where(j < ctx_chunks, ctx_chunks - 1 - j, n_chunks - 1 + ctx_chunks - j)


def _ml_rec_kernel(qk_ref, qkp_ref, qkn_ref, v_ref, gt_ref, gtt_ref, cw_ref, cb_ref, o_ref,
                   c_scr, n_scr, m_scr, *, reverse, n_chunks, ctx_chunks):
    j = pl.program_id(0)
    c = _ml_chunk_index(j, n_chunks, ctx_chunks, reverse)

    @pl.when(j == 0)
    def _():
        c_scr[...] = jnp.zeros_like(c_scr)
        n_scr[...] = jnp.zeros_like(n_scr)
        m_scr[...] = jnp.zeros_like(m_scr)

    has_prev = (c != 0) & (c != ctx_chunks)
    has_next = (c != ctx_chunks - 1) & (c != n_chunks - 1)
    qk = _silu(_conv_tile(qk_ref[...], qkp_ref, qkn_ref, has_prev, has_next, cw_ref, cb_ref))
    L = ML_CHUNK
    ri = lax.broadcasted_iota(jnp.int32, (L, L), 0)
    ci = lax.broadcasted_iota(jnp.int32, (L, L), 1)
    past = (ci >= ri) if reverse else (ci <= ri)
    pastf = past.astype(F32)
    gt = gt_ref[...]
    gtt = gtt_ref[...]
    b_col = jnp.dot(pastf, gt, precision=HI, preferred_element_type=F32)
    b_row = jnp.dot(gtt, pastf.T, precision=HI, preferred_element_type=F32)
    last = 0 if reverse else L - 1
    dbase = 8 if reverse else 0
    nq = ML_HEADS * ML_DK
    for hd in range(ML_HEADS):
        cl = dbase + hd
        cf = dbase + 4 + hd
        q = qk[:, hd * ML_DK:(hd + 1) * ML_DK] * (ML_DK ** -0.5)
        k = qk[:, nq + hd * ML_DK:nq + (hd + 1) * ML_DK]
        v = v_ref[:, hd * ML_DV:(hd + 1) * ML_DV]
        li_c = gt[:, cl:cl + 1]
        li_r = gtt[cl:cl + 1, :]
        b_c = b_col[:, cf:cf + 1]
        b_r = b_row[cf:cf + 1, :]
        g = b_r[:, last:last + 1]
        m0 = m_scr[hd:hd + 1, 0:1]
        c0 = c_scr[hd]
        n0 = n_scr[hd:hd + 1, :]
        a_c = g - b_c + li_c
        a_r = g - b_r + li_r
        m_loc = jnp.max(a_r, axis=-1, keepdims=True)
        inter = b_c + m0
        dlog = jnp.where(past, b_c - b_r + li_r, -jnp.inf)
        m = jnp.maximum(inter, jnp.max(dlog, axis=-1, keepdims=True))
        qb = q.astype(BF16)
        sc = lax.dot_general(qb, k.astype(BF16), (((1,), (1,)), ((), ())),
                             preferred_element_type=F32) * jnp.exp(dlog - m)
        w_inter = jnp.exp(inter - m)
        num = (jnp.dot(sc.astype(BF16), v, preferred_element_type=F32)
               + w_inter * jnp.dot(qb, c0.astype(BF16), preferred_element_type=F32))
        den = (jnp.sum(sc, axis=-1, keepdims=True)
               + w_inter * jnp.sum(q * n0, axis=-1, keepdims=True))
        o_ref[:, hd * ML_DV:(hd + 1) * ML_DV] = num / jnp.maximum(jnp.abs(den), jnp.exp(-m))
        m_new = jnp.maximum(g + m0, m_loc)
        dec = jnp.exp(g + m0 - m_new)
        scl = jnp.exp(m_loc - m_new)
        kw = k * jnp.exp(a_c - m_loc)
        c_scr[hd] = dec * c0 + scl * jnp.dot(kw.T.astype(BF16), v, preferred_element_type=F32)
        n_scr[hd:hd + 1, :] = dec * n0 + scl * jnp.sum(kw, axis=0, keepdims=True)
        m_scr[hd:hd + 1, :] = jnp.broadcast_to(m_new, (1, LANES))


def _ml_rec(qk, v, gt, gtt, conv_w, conv_b, reverse):
    n, nqk = qk.shape
    nv = v.shape[1]
    L = ML_CHUNK
    nc = n // L
    cc = TM // L
    hb = L // SUBLANES
    idx = lambda j: _ml_chunk_index(j, nc, cc, reverse)
    last8 = n // SUBLANES - 1
    return pl.pallas_call(
        functools.partial(_ml_rec_kernel, reverse=reverse, n_chunks=nc, ctx_chunks=cc),
        out_shape=jax.ShapeDtypeStruct((n, nv), F32),
        grid=(nc,),
        in_specs=[pl.BlockSpec((L, nqk), lambda j: (idx(j), 0)),
                  pl.BlockSpec((SUBLANES, nqk), lambda j: (jnp.maximum(idx(j) * hb - 1, 0), 0)),
                  pl.BlockSpec((SUBLANES, nqk), lambda j: (jnp.minimum((idx(j) + 1) * hb, last8), 0)),
                  pl.BlockSpec((L, nv), lambda j: (idx(j), 0)),
                  pl.BlockSpec((L, LANES), lambda j: (idx(j), 0)),
                  pl.BlockSpec((2 * SUBLANES, L), lambda j: (0, idx(j))),
                  _full((4, nqk)), _full((1, nqk))],
        out_specs=pl.BlockSpec((L, nv), lambda j: (idx(j), 0)),
        scratch_shapes=[pltpu.VMEM((ML_HEADS, ML_DK, ML_DV), F32),
                        pltpu.VMEM((SUBLANES, ML_DK), F32),
                        pltpu.VMEM((SUBLANES, LANES), F32)],
        compiler_params=_cparams("arbitrary"),
        name="mlstm_rev" if reverse else "mlstm_fwd",
    )(qk, qk, qk, v, gt, gtt, conv_w, conv_b.reshape(1, nqk))


def _ml_out_kernel(hf_ref, hr_ref, o_ref, s_ref, mod_ref, ng_ref, w_ref, out_ref, p_scr):
    i = pl.program_id(0)
    gate = _mod_rows(mod_ref, i, 1, 0)[2]
    hs = hf_ref[...] + hr_ref[...]
    sig = _sigmoid(o_ref[...])
    ng = ng_ref[...]
    for hd in range(ML_HEADS):
        cs = slice(hd * ML_DV, (hd + 1) * ML_DV)
        seg = hs[:, cs]
        hn = seg * lax.rsqrt(jnp.mean(seg * seg, axis=-1, keepdims=True) + EPS) * ng[:, cs]
        p_scr[:, cs] = (hn * sig[:, cs]).astype(BF16)
    y = jnp.dot(p_scr[...], w_ref[...], preferred_element_type=F32)
    out_ref[...] = s_ref[...] + gate * y


def _ml_out(hf, hr, o, s, mods, norm_g, w_out):
    n, d = s.shape
    nv = hf.shape[1]
    tile = lambda w: pl.BlockSpec((TM, w), lambda i: (i, 0))
    return pl.pallas_call(
        _ml_out_kernel,
        out_shape=jax.ShapeDtypeStruct((n, d), F32),
        grid=(n // TM,),
        in_specs=[tile(nv), tile(nv), tile(nv), tile(d), _full(mods.shape), _full((1, nv)),
                  _full(w_out.shape)],
        out_specs=tile(d),
        scratch_shapes=[pltpu.VMEM((TM, nv), BF16)],
        compiler_params=_cparams("arbitrary"),
        name="mlstm_out",
    )(hf, hr, o, s, mods, norm_g.reshape(1, nv), w_out.astype(BF16))


def _mlstm_layer(s, mods, norm_g, w_in, conv_w, conv_b, gate_b, ml_norm_g, w_out):
    qk, v, o, gt = _ml_proj(s, mods, norm_g, w_in, gate_b)
    gtt = gt[:, :2 * SUBLANES].T
    hf = _ml_rec(qk, v, gt, gtt, conv_w, conv_b, False)
    hr = _ml_rec(qk, v, gt, gtt, conv_w, conv_b, True)
    return _ml_out(hf, hr, o, s, mods, ml_norm_g, w_out)


def _lru_proj_kernel(s_ref, mod_ref, g_ref, w_ref, gl_ref, xr_ref):
    i = pl.program_id(0)
    shift, scale, _ = _mod_rows(mod_ref, i, 1, 0)
    h = _rms(s_ref[...], g_ref[...]) * (1.0 + scale) + shift
    z = _bdot(h, w_ref[...])
    w = gl_ref.shape[1]
    gl_ref[...] = _gelu(z[:, :w])
    xr_ref[...] = z[:, w:]


def _lru_proj(s, mods, norm_g, w_in):
    n, d = s.shape
    w = w_in.shape[1] // 2
    tile = lambda c: pl.BlockSpec((TM, c), lambda i: (i, 0))
    return pl.pallas_call(
        _lru_proj_kernel,
        out_shape=(jax.ShapeDtypeStruct((n, w), F32), jax.ShapeDtypeStruct((n, w), F32)),
        grid=(n // TM,),
        in_specs=[tile(d), _full(mods.shape), _full((1, d)), _full(w_in.shape)],
        out_specs=(tile(w), tile(w)),
        compiler_params=_cparams("arbitrary"),
        name="rglru_proj",
    )(s, mods, norm_g.reshape(1, d), w_in.astype(BF16))


def _lru_tile_index(j, n_tiles, reverse):
    if not reverse:
        return j
    return jnp.where(j == 0, 0, n_tiles - j)


def _lru_scan_kernel(x_ref, xp_ref, xn_ref, cw_ref, cb_ref, wg_ref, ba_ref, bx_ref, lam_ref,
                     o_ref, a_scr, u_scr, carry, *, reverse, n_tiles):
    j = pl.program_id(0)
    t = _lru_tile_index(j, n_tiles, reverse)

    @pl.when(j == 0)
    def _():
        carry[...] = jnp.zeros_like(carry)

    has_prev = t > 1
    has_next = (t != 0) & (t != n_tiles - 1)
    xr = _conv_tile(x_ref[...], xp_ref, xn_ref, has_prev, has_next, cw_ref, cb_ref)
    sp = _softplus(-lam_ref[...])
    B = LRU_BLOCK
    for hd in range(LRU_HEADS):
        cs = slice(hd * B, (hd + 1) * B)
        xh = xr[:, cs]
        y = jnp.dot(xh.astype(BF16), wg_ref[hd], preferred_element_type=F32)
        r = _sigmoid(y[:, :B] + ba_ref[:, cs])
        ig = _sigmoid(y[:, B:] + bx_ref[:, cs])
        log_a = -LRU_C * r * sp[:, cs]
        a_scr[:, cs] = jnp.exp(log_a)
        u_scr[:, cs] = jnp.sqrt(1.0 - jnp.exp(2.0 * log_a)) * (ig * xh)

    S = SUBLANES
    w = a_scr.shape[1]
    sidx = lax.broadcasted_iota(jnp.int32, (S, w), 0)

    def group(gi, c):
        g = (TM // S - 1 - gi) if reverse else gi
        r0 = pl.multiple_of(g * S, S)
        a = a_scr[pl.ds(r0, S), :]
        u = u_scr[pl.ds(r0, S), :]
        for sft in (1, 2, 4):
            if reverse:
                ok = sidx < S - sft
                a_e = pltpu.roll(a, S - sft, axis=0)
                u_e = pltpu.roll(u, S - sft, axis=0)
            else:
                ok = sidx >= sft
                a_e = pltpu.roll(a, sft, axis=0)
                u_e = pltpu.roll(u, sft, axis=0)
            u = jnp.where(ok, a * u_e + u, u)
            a = jnp.where(ok, a * a_e, a)
        hcur = a * carry[...] + u
        o_ref[pl.ds(r0, S), :] = hcur
        edge = 0 if reverse else S - 1
        carry[...] = jnp.broadcast_to(hcur[edge:edge + 1, :], (S, w))
        return c

    lax.fori_loop(0, TM // S, group, 0)


def _lru_scan(xraw, conv_w, conv_b, w_a, b_a, w_x, b_x, lam, reverse):
    n, w = xraw.shape
    nt = n // TM
    hb = TM // SUBLANES
    idx = lambda j: _lru_tile_index(j, nt, reverse)
    last8 = n // SUBLANES - 1
    wg = jnp.concatenate([w_a, w_x], axis=-1).astype(BF16)
    return pl.pallas_call(
        functools.partial(_lru_scan_kernel, reverse=reverse, n_tiles=nt),
        out_shape=jax.ShapeDtypeStruct((n, w), F32),
        grid=(nt,),
        in_specs=[pl.BlockSpec((TM, w), lambda j: (idx(j), 0)),
                  pl.BlockSpec((SUBLANES, w), lambda j: (jnp.maximum(idx(j) * hb - 1, 0), 0)),
                  pl.BlockSpec((SUBLANES, w), lambda j: (jnp.minimum((idx(j) + 1) * hb, last8), 0)),
                  _full((4, w)), _full((1, w)), _full(wg.shape), _full((1, w)), _full((1, w)),
                  _full((1, w))],
        out_specs=pl.BlockSpec((TM, w), lambda j: (idx(j), 0)),
        scratch_shapes=[pltpu.VMEM((TM, w), F32), pltpu.VMEM((TM, w), F32),
                        pltpu.VMEM((SUBLANES, w), F32)],
        compiler_params=_cparams("arbitrary"),
        name="rglru_rev" if reverse else "rglru_fwd",
    )(xraw, xraw, xraw, conv_w, conv_b.reshape(1, w), wg, b_a.reshape(1, w), b_x.reshape(1, w),
      lam.reshape(1, w))


def _lru_out_kernel(gl_ref, hf_ref, hr_ref, s_ref, mod_ref, w_ref, out_ref):
    gate = _mod_rows(mod_ref, 1, 0, 0)[2]
    p = gl_ref[...] * (hf_ref[...] + hr_ref[...])
    out_ref[...] = s_ref[...] + gate * _bdot(p, w_ref[...])


def _lru_out(gl, hf, hr, s, mods, w_out):
    n, d = s.shape
    w = gl.shape[1]
    lat = lambda c: pl.BlockSpec((TM, c), lambda i: (i + 1, 0))
    return pl.pallas_call(
        _lru_out_kernel,
        out_shape=jax.ShapeDtypeStruct((n - TM, d), F32),
        grid=(n // TM - 1,),
        in_specs=[lat(w), lat(w), lat(w), lat(d), _full(mods.shape), _full(w_out.shape)],
        out_specs=pl.BlockSpec((TM, d), lambda i: (i, 0)),
        compiler_params=_cparams("arbitrary"),
        name="rglru_out",
    )(gl, hf, hr, s, mods, w_out.astype(BF16))


def _rglru_layer(s, mods, norm_g, w_in, conv_w, conv_b, w_a, b_a, w_x, b_x, lam, w_out):
    gl, xraw = _lru_proj(s, mods, norm_g, w_in)
    hf = _lru_scan(xraw, conv_w, conv_b, w_a[0], b_a[0], w_x[0], b_x[0], lam[0], False)
    hr = _lru_scan(xraw, conv_w, conv_b, w_a[1], b_a[1], w_x[1], b_x[1], lam[1], True)
    return _lru_out(gl, hf, hr, s, mods, w_out)


def _fn_proj_kernel(s_ref, mod_ref, g_ref, w_ref, cs_ref, yr_ref, yi_ref):
    shift, scale, _ = _mod_rows(mod_ref, 1, 0, 0)
    h = _rms(s_ref[...], g_ref[...]) * (1.0 + scale) + shift
    z = _bdot(h, w_ref[...]).astype(BF16)
    gw = z.shape[1] // FN_GROUPS
    for g in range(FN_GROUPS):
        cs = slice(g * gw, (g + 1) * gw)
        y = jnp.dot(z[:, cs], cs_ref[...], preferred_element_type=F32)
        yr_ref[:, cs] = y[:, :gw].astype(BF16)
        yi_ref[:, cs] = y[:, gw:].astype(BF16)


def _dft_cos_sin(n, scale):
    k = np.arange(n, dtype=np.int64)
    ang = 2.0 * np.pi * ((k[:, None] * k[None, :]) % n).astype(np.float64) / n
    return np.cos(ang) * scale, np.sin(ang) * scale


def _fn_proj(s, mods, norm_g, w_in):
    n, d = s.shape
    w = w_in.shape[1]
    gw = w // FN_GROUPS
    c, sn = _dft_cos_sin(gw, gw ** -0.5)
    cs = jnp.asarray(np.concatenate([c, -sn], axis=1), BF16)
    tile = lambda k: pl.BlockSpec((TM, k), lambda i: (i, 0))
    return pl.pallas_call(
        _fn_proj_kernel,
        out_shape=(jax.ShapeDtypeStruct((n, w), BF16), jax.ShapeDtypeStruct((n, w), BF16)),
        grid=(n // TM,),
        in_specs=[tile(d), _full(mods.shape), _full((1, d)), _full(w_in.shape), _full(cs.shape)],
        out_specs=(tile(w), tile(w)),
        compiler_params=_cparams("arbitrary"),
        name="fourier_proj",
    )(s, mods, norm_g.reshape(1, d), w_in.astype(BF16), cs)


def _fn_stage1_kernel(xr_ref, xi_ref, m_ref, tc_ref, ts_ref, br_ref, bi_ref):
    n1 = xr_ref.shape[0]
    x = jnp.concatenate([xr_ref[...], xi_ref[...]], axis=0)
    a = jnp.dot(m_ref[...], x, preferred_element_type=F32)
    ar = a[:n1]
    ai = a[n1:]
    reps = ar.shape[1] // LANES
    per = reps // (tc_ref.shape[1] // LANES)
    tc = jnp.concatenate([tc_ref[:, (q // per) * LANES:(q // per + 1) * LANES] for q in range(reps)], axis=1)
    ts = jnp.concatenate([ts_ref[:, (q // per) * LANES:(q // per + 1) * LANES] for q in range(reps)], axis=1)
    br_ref[...] = (ar * tc + ai * ts).astype(BF16)
    bi_ref[...] = (ai * tc - ar * ts).astype(BF16)


def _fn_stage1(yr, yi):
    t, w = yr.shape
    n2 = FFT_N2
    n1 = t // n2
    c, sn = _dft_cos_sin(n1, n1 ** -0.5)
    m = jnp.asarray(np.block([[c, sn], [-sn, c]]), BF16)
    k1 = np.arange(n1, dtype=np.int64)[:, None]
    t2 = np.arange(n2, dtype=np.int64)[None, :]
    ang = 2.0 * np.pi * ((k1 * t2) % t).astype(np.float64) / t
    tc = jnp.asarray(np.repeat(np.cos(ang), LANES, axis=1), F32)
    ts = jnp.asarray(np.repeat(np.sin(ang), LANES, axis=1), F32)
    per = 2
    col = lambda i: (0, i)
    xspec = pl.BlockSpec((n1, per * w), col)
    tspec = pl.BlockSpec((n1, per * LANES), col)
    return pl.pallas_call(
        _fn_stage1_kernel,
        out_shape=(jax.ShapeDtypeStruct((n1, n2 * w), BF16), jax.ShapeDtypeStruct((n1, n2 * w), BF16)),
        grid=(n2 // per,),
        in_specs=[xspec, xspec, _full(m.shape), tspec, tspec],
        out_specs=(xspec, xspec),
        compiler_params=_cparams("arbitrary"),
        name="fourier_stage1",
    )(yr.reshape(n1, n2 * w), yi.reshape(n1, n2 * w), m, tc, ts)


def _fn_stage2_kernel(br_ref, bi_ref, d_ref, w_ref, s_ref, mod_ref, o_ref):
    gate = _mod_rows(mod_ref, 1, 0, 0)[2]
    b = jnp.concatenate([br_ref[...], bi_ref[...]], axis=0)
    f = jnp.dot(d_ref[...], b, preferred_element_type=F32)
    y = _bdot(f, w_ref[...])
    o_ref[...] = s_ref[...] + gate * y


def _fn_stage2(br, bi, s, mods, w_out):
    t, d = s.shape
    n2 = FFT_N2
    n1 = t // n2
    w = w_out.shape[0]
    c, sn = _dft_cos_sin(n2, n2 ** -0.5)
    dm = jnp.asarray(np.concatenate([c, sn], axis=1), BF16)
    bspec = pl.BlockSpec((None, n2, w), lambda i: (i, 0, 0))
    sspec = pl.BlockSpec((n2, d), lambda i: (0, i))
    out = pl.pallas_call(
        _fn_stage2_kernel,
        out_shape=jax.ShapeDtypeStruct((n2, n1 * d), F32),
        grid=(n1,),
        in_specs=[bspec, bspec, _full(dm.shape), _full(w_out.shape), sspec, _full(mods.shape)],
        out_specs=sspec,
        compiler_params=_cparams("arbitrary"),
        name="fourier_stage2",
    )(br.reshape(n1, n2, w), bi.reshape(n1, n2, w), dm, w_out.astype(BF16),
      s.reshape(n2, n1 * d), mods)
    return out.reshape(t, d)


def _fourier_layer(s, mods, norm_g, w_in, w_out):
    yr, yi = _fn_proj(s, mods, norm_g, w_in)
    br, bi = _fn_stage1(yr, yi)
    return _fn_stage2(br, bi, s, mods, w_out)


def kernel(x, c, ctx, c_ctx, ada_w, ada_b, norm_mix_g, norm_ffn_g, final_norm_g, router_group_w, router_group_b, router_expert_w, router_expert_b, expert_w_gate, expert_w_up, expert_w_down, cm_w_in, cm_v_norm_g, cm_w_s, cm_b_s, cm_w_out, ml_w_in, ml_conv_w, ml_conv_b, ml_gate_b, ml_norm_g, ml_w_out, lru_w_in, lru_conv_w, lru_conv_b, lru_w_a, lru_b_a, lru_w_x, lru_b_x, lru_lambda, lru_w_out, fn_w_in, fn_w_out):
    bsz, seq, d = x.shape
    assert bsz == 1 and ada_w.shape[0] == 4 and ctx.shape[1] == TM
    c_rows = jnp.concatenate([c_ctx[None, :], c, jnp.zeros((SUBLANES - 2, d), F32)], axis=0)
    mods = _ada_table(c_rows, ada_w, ada_b)
    s = _prep_stream(x[0], ctx[0])

    def moe(s, i, ctx_tiles, final_norm=False):
        return _moe_layer(s, mods[i], norm_ffn_g[i], router_group_w[i], router_group_b[i],
                          router_expert_w[i], router_expert_b[i], expert_w_gate[i],
                          expert_w_up[i], expert_w_down[i], ctx_tiles, final_norm_g, final_norm)

    s = _chunk_mlp_layer(s, mods[0], norm_mix_g[0], cm_w_in[0], cm_v_norm_g[0], cm_w_s[0],
                         cm_b_s[0], cm_w_out[0])
    s = moe(s, 0, 1)
    s = _mlstm_layer(s, mods[1], norm_mix_g[1], ml_w_in[0], ml_conv_w[0], ml_conv_b[0],
                     ml_gate_b[0], ml_norm_g[0], ml_w_out[0])
    s = moe(s, 1, 1)
    s = _rglru_layer(s, mods[2], norm_mix_g[2], lru_w_in[0], lru_conv_w[0], lru_conv_b[0],
                     lru_w_a[0], lru_b_a[0], lru_w_x[0], lru_b_x[0], lru_lambda[0], lru_w_out[0])
    s = moe(s, 2, 0)
    s = _fourier_layer(s, mods[3], norm_mix_g[3], fn_w_in[0], fn_w_out[0])
    s = moe(s, 3, 0, final_norm=True)
    return s[None]
```

```python
import functools
import math

import jax
import jax.numpy as jnp
import numpy as np
from jax import lax
from jax.experimental import pallas as pl
from jax.experimental.pallas import tpu as pltpu

F32 = jnp.float32
BF16 = jnp.bfloat16

EPS = 1e-6
POS_BASE = 10000.0
GRID_W = 64
N_MOD = 6
TM = 256
LANES = 128
SUBLANES = 8
VMEM_LIMIT = 56 * 1024 * 1024

CM_CHUNK = 128
CM_GROUPS = 4
ML_HEADS = 4
ML_DK = 128
ML_DV = 256
ML_CHUNK = 128
LRU_HEADS = 10
LRU_BLOCK = 128
LRU_C = 8.0
FN_GROUPS = 4
FFT_N2 = 128
MOE_GROUPS = 4
MOE_EPG = 8
MOE_EXPERTS = MOE_GROUPS * MOE_EPG
MOE_ROWS_LOG2 = 8
MOE_ROWS = 1 << MOE_ROWS_LOG2
ROUTE_ROWS = 40
FN_TB = 1024
FN_CB = 8
CONV_LEFT = 2

HI = lax.Precision.HIGHEST


def _cparams(*sem):
    return pltpu.CompilerParams(dimension_semantics=sem, vmem_limit_bytes=VMEM_LIMIT)


def _full(shape):
    nd = len(shape)
    return pl.BlockSpec(shape, lambda *_: (0,) * nd)


def _rms(x, g):
    return x * lax.rsqrt(jnp.mean(x * x, axis=-1, keepdims=True) + EPS) * g


def _gelu(x):
    c = math.sqrt(2.0 / math.pi)
    return 0.5 * x * (1.0 + jnp.tanh(c * (x + 0.044715 * (x * x * x))))


def _sigmoid(x):
    return 1.0 / (1.0 + jnp.exp(-x))


def _silu(x):
    return x * _sigmoid(x)


def _softplus(x):
    return jnp.maximum(x, 0.0) + jnp.log(1.0 + jnp.exp(-jnp.abs(x)))


def _mod_rows(mod_ref, tile, ctx_tiles, first):
    row = jnp.where(tile < ctx_tiles, 0, 1)
    m = mod_ref[pl.ds(row, 1), :]
    d = m.shape[1] // N_MOD
    return tuple(m[:, (first + j) * d:(first + j + 1) * d] for j in range(3))


def _bdot(a, b):
    return jnp.dot(a.astype(BF16), b.astype(BF16), preferred_element_type=F32)


def _ada_kernel(c_ref, w_ref, b_ref, o_ref):
    c = c_ref[...]
    o_ref[...] = jnp.dot(_silu(c), w_ref[...], precision=HI,
                         preferred_element_type=F32) + b_ref[...]


def _ada_table(c_rows, ada_w, ada_b):
    depth, d, n = ada_w.shape
    tn = 1024
    return pl.pallas_call(
        _ada_kernel,
        out_shape=jax.ShapeDtypeStruct((depth, SUBLANES, n), F32),
        grid=(depth, n // tn),
        in_specs=[_full((SUBLANES, d)),
                  pl.BlockSpec((None, d, tn), lambda i, j: (i, 0, j)),
                  pl.BlockSpec((None, 1, tn), lambda i, j: (i, 0, j))],
        out_specs=pl.BlockSpec((None, SUBLANES, tn), lambda i, j: (i, 0, j)),
        compiler_params=_cparams("arbitrary", "arbitrary"),
        name="ada_table",
    )(c_rows, ada_w, ada_b.reshape(depth, 1, n))


def _prep_kernel(x_ref, ctx_ref, rt_ref, ct_ref, o_ref):
    i = pl.program_id(0)

    @pl.when(i == 0)
    def _():
        o_ref[...] = ctx_ref[...]

    @pl.when(i > 0)
    def _():
        rows_per_tile = TM // GRID_W
        q2 = rt_ref.shape[1]
        r0 = (i - 1) * rows_per_tile
        rt = jnp.concatenate(
            [jnp.broadcast_to(rt_ref[pl.ds(r0 + j, 1), :], (GRID_W, q2))
             for j in range(rows_per_tile)], axis=0)
        ct = jnp.concatenate([ct_ref[...]] * rows_per_tile, axis=0)
        o_ref[...] = x_ref[...] + jnp.concatenate([rt, ct], axis=1)


def _prep_stream(x2, ctx2):
    seq, d = x2.shape
    n_ctx = ctx2.shape[0]
    assert n_ctx == TM and seq % TM == 0 and TM % GRID_W == 0
    q = d // 4
    freq = jnp.exp(-math.log(POS_BASE) * jnp.arange(q, dtype=F32) / q)
    ar = jnp.arange(seq // GRID_W, dtype=F32)[:, None] * freq
    ac = jnp.arange(GRID_W, dtype=F32)[:, None] * freq
    rt = jnp.concatenate([jnp.sin(ar), jnp.cos(ar)], axis=-1)
    ct = jnp.concatenate([jnp.sin(ac), jnp.cos(ac)], axis=-1)
    nt = 1 + seq // TM
    return pl.pallas_call(
        _prep_kernel,
        out_shape=jax.ShapeDtypeStruct((n_ctx + seq, d), F32),
        grid=(nt,),
        in_specs=[pl.BlockSpec((TM, d), lambda i: (jnp.maximum(i - 1, 0), 0)),
                  _full((TM, d)), _full(rt.shape), _full(ct.shape)],
        out_specs=pl.BlockSpec((TM, d), lambda i: (i, 0)),
        compiler_params=_cparams("arbitrary"),
        name="prep_stream",
    )(x2, ctx2, rt, ct)


def _cm_kernel(s_ref, mod_ref, g_ref, win_ref, vg_ref, ws_ref, bs_ref, wout_ref, o_ref, p_scr):
    i = pl.program_id(0)
    shift, scale, gate = _mod_rows(mod_ref, i, 1, 0)
    x = s_ref[...]
    h = _rms(x, g_ref[...]) * (1.0 + scale) + shift
    z = _gelu(_bdot(h, win_ref[...]))
    w = z.shape[1] // 2
    u = z[:, :w]
    v = _rms(z[:, w:], vg_ref[...]).astype(BF16)
    gw = w // CM_GROUPS
    for c in range(TM // CM_CHUNK):
        r = slice(c * CM_CHUNK, (c + 1) * CM_CHUNK)
        for g in range(CM_GROUPS):
            cs = slice(g * gw, (g + 1) * gw)
            s = jnp.dot(ws_ref[g], v[r, cs], preferred_element_type=F32) + bs_ref[:, g:g + 1]
            p_scr[r, cs] = (u[r, cs] * s).astype(BF16)
    y = jnp.dot(p_scr[...], wout_ref[...], preferred_element_type=F32)
    o_ref[...] = x + gate * y


def _chunk_mlp_layer(s, mods, norm_g, w_in, v_g, w_s, b_s, w_out):
    n, d = s.shape
    w = w_out.shape[0]
    return pl.pallas_call(
        _cm_kernel,
        out_shape=jax.ShapeDtypeStruct((n, d), F32),
        grid=(n // TM,),
        in_specs=[pl.BlockSpec((TM, d), lambda i: (i, 0)),
                  _full(mods.shape), _full((1, d)), _full(w_in.shape), _full((1, w)),
                  _full(w_s.shape), _full((CM_CHUNK, CM_GROUPS)), _full(w_out.shape)],
        out_specs=pl.BlockSpec((TM, d), lambda i: (i, 0)),
        scratch_shapes=[pltpu.VMEM((TM, w), BF16)],
        compiler_params=_cparams("arbitrary"),
        name="chunk_mlp",
    )(s, mods, norm_g.reshape(1, d), w_in.astype(BF16), v_g.reshape(1, w),
      w_s.astype(BF16), b_s.T, w_out.astype(BF16))


def _router_kernel(s_ref, mod_ref, g_ref, rwt_ref, rbt_ref, h_ref, e1_ref, e2_ref, r1_ref, r2_ref,
                   wt_ref, cnt_ref, carry, *, ctx_tiles):
    i = pl.program_id(0)

    @pl.when(i == 0)
    def _():
        carry[...] = jnp.zeros_like(carry)

    shift, scale, _ = _mod_rows(mod_ref, i, ctx_tiles, 3)
    h = _rms(s_ref[...], g_ref[...]) * (1.0 + scale) + shift
    h_ref[...] = h
    logits = lax.dot_general(rwt_ref[...], h, (((1,), (1,)), ((), ())), precision=HI,
                             preferred_element_type=F32) + rbt_ref[...]
    row = lax.broadcasted_iota(jnp.int32, logits.shape, 0)
    neg = jnp.float32(-jnp.inf)
    big = jnp.int32(1 << 20)
    is_g = row < MOE_GROUPS
    gl = jnp.where(is_g, logits, neg)
    gmax = jnp.max(gl, axis=0, keepdims=True)
    grp = jnp.min(jnp.where(is_g & (gl == gmax), row, big), axis=0, keepdims=True)
    p_grp = 1.0 / jnp.sum(jnp.exp(gl - gmax), axis=0, keepdims=True)
    e_row = row - MOE_GROUPS
    in_grp = (e_row >= 0) & (e_row < MOE_EXPERTS) & ((e_row >> 3) == grp)
    l1 = jnp.where(in_grp, logits, neg)
    v1 = jnp.max(l1, axis=0, keepdims=True)
    i1 = jnp.min(jnp.where(in_grp & (l1 == v1), row, big), axis=0, keepdims=True)
    rest = in_grp & (row != i1)
    l2 = jnp.where(rest, logits, neg)
    v2 = jnp.max(l2, axis=0, keepdims=True)
    i2 = jnp.min(jnp.where(rest & (l2 == v2), row, big), axis=0, keepdims=True)
    e21 = jnp.exp(v2 - v1)
    w1 = p_grp / (1.0 + e21)
    w2 = p_grp * e21 / (1.0 + e21)
    oh1 = (row == i1).astype(F32)
    oh2 = (row == i2).astype(F32)
    oh = oh1 + oh2
    ri = lax.broadcasted_iota(jnp.int32, (TM, TM), 0)
    ci = lax.broadcasted_iota(jnp.int32, (TM, TM), 1)
    earlier = (ri < ci).astype(BF16)
    before = jnp.dot(oh.astype(BF16), earlier, preferred_element_type=F32) + carry[:, 0:1]
    r1_ref[...] = jnp.sum(oh1 * before, axis=0, keepdims=True).astype(jnp.int32)
    r2_ref[...] = jnp.sum(oh2 * before, axis=0, keepdims=True).astype(jnp.int32)
    e1_ref[...] = i1 - MOE_GROUPS
    e2_ref[...] = i2 - MOE_GROUPS
    carry[...] = carry[...] + jnp.sum(oh, axis=1, keepdims=True)
    cnt_ref[...] = carry[...]
    wrows = jnp.concatenate([w1, w2, jnp.zeros((LANES - 2, TM), F32)], axis=0)
    eye = (ri == ci).astype(F32)
    wt_ref[...] = lax.dot_general(eye, wrows, (((1,), (1,)), ((), ())), precision=HI,
                                  preferred_element_type=F32)


def _router(s, mods, norm_g, rg_w, rg_b, re_w, re_b, ctx_tiles):
    n, d = s.shape
    nt = n // TM
    pad = ROUTE_ROWS - MOE_GROUPS - MOE_EXPERTS
    rwt = jnp.concatenate([rg_w, re_w, jnp.zeros((d, pad), F32)], axis=1).T
    rbt = jnp.broadcast_to(jnp.concatenate([rg_b, re_b, jnp.zeros((pad,), F32)])[:, None],
                           (ROUTE_ROWS, TM))
    tile = pl.BlockSpec((TM, d), lambda i: (i, 0))
    irow = pl.BlockSpec((None, 1, TM), lambda i: (i, 0, 0))
    ishape = jax.ShapeDtypeStruct((nt, 1, TM), jnp.int32)
    return pl.pallas_call(
        functools.partial(_router_kernel, ctx_tiles=ctx_tiles),
        out_shape=(jax.ShapeDtypeStruct((n, d), F32), ishape, ishape, ishape, ishape,
                   jax.ShapeDtypeStruct((n, LANES), F32),
                   jax.ShapeDtypeStruct((ROUTE_ROWS, LANES), F32)),
        grid=(nt,),
        in_specs=[tile, _full(mods.shape), _full((1, d)), _full((ROUTE_ROWS, d)),
                  _full((ROUTE_ROWS, TM))],
        out_specs=(tile, irow, irow, irow, irow, pl.BlockSpec((TM, LANES), lambda i: (i, 0)),
                   _full((ROUTE_ROWS, LANES))),
        scratch_shapes=[pltpu.VMEM((ROUTE_ROWS, LANES), F32)],
        compiler_params=_cparams("arbitrary"),
        name="moe_router",
    )(s, mods, norm_g.reshape(1, d), rwt, rbt)


def _finalize_kernel(cnt_ref, e1_ref, e2_ref, r1_ref, r2_ref, d1_ref, d2_ref, blk_ref):
    e1 = e1_ref[...]
    e2 = e2_ref[...]
    r1 = r1_ref[...]
    r2 = r2_ref[...]
    d1 = jnp.zeros_like(e1)
    d2 = jnp.zeros_like(e2)
    lane = lax.broadcasted_iota(jnp.int32, blk_ref.shape, 1)
    brow = lane * MOE_ROWS
    sub = lax.broadcasted_iota(jnp.int32, blk_ref.shape, 0)
    be = jnp.zeros(blk_ref.shape, jnp.int32)
    pend = jnp.zeros(blk_ref.shape, jnp.int32)
    ps = jnp.int32(0)
    for e in range(MOE_EXPERTS):
        c = cnt_ref[e]
        pe = ps + lax.shift_left(lax.shift_right_logical(c + (MOE_ROWS - 1), MOE_ROWS_LOG2),
                                 MOE_ROWS_LOG2)
        d1 = jnp.where(e1 == e, ps + r1, d1)
        d2 = jnp.where(e2 == e, ps + r2, d2)
        be = be + (brow >= pe).astype(jnp.int32)
        pend = jnp.where(lane == e, pe, pend)
        ps = pe
    d1_ref[...] = d1
    d2_ref[...] = d2
    n_used = lax.shift_right_logical(ps, MOE_ROWS_LOG2)
    blk_ref[...] = jnp.where(sub == 0, jnp.minimum(be, MOE_EXPERTS - 1),
                             jnp.where(sub == 1, pend, n_used))


def _finalize(counts, e1, e2, r1, r2, nb):
    nbp = (nb + LANES - 1) // LANES * LANES
    whole = pl.BlockSpec(e1.shape, lambda i, c: (0, 0, 0))
    ishape = jax.ShapeDtypeStruct(e1.shape, jnp.int32)
    return pl.pallas_call(
        _finalize_kernel,
        out_shape=(ishape, ishape, jax.ShapeDtypeStruct((SUBLANES, nbp), jnp.int32)),
        grid_spec=pltpu.PrefetchScalarGridSpec(
            num_scalar_prefetch=1,
            grid=(1,),
            in_specs=[whole, whole, whole, whole],
            out_specs=(whole, whole, pl.BlockSpec((SUBLANES, nbp), lambda i, c: (0, 0)))),
        compiler_params=_cparams("arbitrary"),
        name="moe_finalize",
    )(counts, e1, e2, r1, r2)


def _row_copy(src, r, dst, d, sem):
    return pltpu.make_async_copy(src.at[pl.ds(r, 1), :], dst.at[pl.ds(d, 1), :], sem)


def _zero_fill_padding(pend_ref, nu_ref, xs_out, zbuf, zsem):
    nb = xs_out.shape[0] // MOE_ROWS
    zbuf[...] = jnp.zeros_like(zbuf)

    def block_copy(b):
        r0 = pl.multiple_of(b * MOE_ROWS, MOE_ROWS)
        return pltpu.make_async_copy(zbuf, xs_out.at[pl.ds(r0, MOE_ROWS), :], zsem)

    def seg_last_block(e):
        pe = pend_ref[e]
        prev = pend_ref[e - 1] if e > 0 else 0
        return pe > prev, lax.shift_right_logical(pe, MOE_ROWS_LOG2) - 1

    for e in range(MOE_EXPERTS):
        nonempty, b = seg_last_block(e)

        @pl.when(nonempty)
        def _():
            block_copy(b).start()

    def tail_start(b, c):
        block_copy(b).start()
        return c

    lax.fori_loop(nu_ref[0], nb, tail_start, 0)
    for e in range(MOE_EXPERTS):
        nonempty, b = seg_last_block(e)

        @pl.when(nonempty)
        def _():
            block_copy(b).wait()

    def tail_wait(b, c):
        block_copy(b).wait()
        return c

    lax.fori_loop(nu_ref[0], nb, tail_wait, 0)


def _dispatch_kernel(dest_ref, pend_ref, nu_ref, h_ref, xs_out, sem, zbuf, zsem):
    @pl.when(pl.program_id(0) == 0)
    def _():
        _zero_fill_padding(pend_ref, nu_ref, xs_out, zbuf, zsem)

    base = pl.program_id(0) * (2 * TM)

    def start(r, c):
        _row_copy(h_ref, r, xs_out, dest_ref[base + r], sem).start()
        _row_copy(h_ref, r, xs_out, dest_ref[base + TM + r], sem).start()
        return c

    lax.fori_loop(0, TM, start, 0, unroll=8)
    for _ in range(2):
        pltpu.make_async_copy(h_ref, xs_out.at[pl.ds(0, TM), :], sem).wait()


def _dispatch(dest, pad_end, n_used, h, n_rows):
    n, d = h.shape
    return pl.pallas_call(
        _dispatch_kernel,
        out_shape=jax.ShapeDtypeStruct((n_rows, d), F32),
        grid_spec=pltpu.PrefetchScalarGridSpec(
            num_scalar_prefetch=3,
            grid=(n // TM,),
            in_specs=[pl.BlockSpec((TM, d), lambda i, *_: (i, 0))],
            out_specs=pl.BlockSpec(memory_space=pl.ANY),
            scratch_shapes=[pltpu.SemaphoreType.DMA, pltpu.VMEM((MOE_ROWS, d), F32),
                            pltpu.SemaphoreType.DMA]),
        compiler_params=_cparams("arbitrary"),
        name="moe_dispatch",
    )(dest, pad_end, n_used, h)


def _expert_kernel(be_ref, nu_ref, x_ref, wg_ref, wu_ref, wd_ref, y_ref, wg_s, wu_s, wd_s):
    b = pl.program_id(0)
    used = b < nu_ref[0]
    prev = be_ref[jnp.maximum(b - 1, 0)]
    fresh = (b == 0) | (be_ref[b] != prev)

    @pl.when(used & fresh)
    def _():
        wg_s[...] = wg_ref[...].astype(BF16)
        wu_s[...] = wu_ref[...].astype(BF16)
        wd_s[...] = wd_ref[...].astype(BF16)

    @pl.when(used)
    def _():
        x = x_ref[...].astype(BF16)
        a = jnp.dot(x, wg_s[...], preferred_element_type=F32)
        u = jnp.dot(x, wu_s[...], preferred_element_type=F32)
        y_ref[...] = jnp.dot((_silu(a) * u).astype(BF16), wd_s[...], preferred_element_type=F32)

    @pl.when(jnp.logical_not(used))
    def _():
        y_ref[...] = jnp.zeros_like(y_ref)


def _experts(blk_expert, n_used, xs, w_gate, w_up, w_down, layer):
    n_rows, d = xs.shape
    hid = w_gate.shape[3]
    nb = n_rows // MOE_ROWS
    wmap = lambda b, be, nu: (layer, be[b], 0, 0)
    return pl.pallas_call(
        _expert_kernel,
        out_shape=jax.ShapeDtypeStruct((n_rows, d), F32),
        grid_spec=pltpu.PrefetchScalarGridSpec(
            num_scalar_prefetch=2,
            grid=(nb,),
            in_specs=[pl.BlockSpec((MOE_ROWS, d), lambda b, be, nu: (b, 0)),
                      pl.BlockSpec((None, None, d, hid), wmap),
                      pl.BlockSpec((None, None, d, hid), wmap),
                      pl.BlockSpec((None, None, hid, d), wmap)],
            out_specs=pl.BlockSpec((MOE_ROWS, d), lambda b, be, nu: (b, 0)),
            scratch_shapes=[pltpu.VMEM((d, hid), BF16), pltpu.VMEM((d, hid), BF16),
                            pltpu.VMEM((hid, d), BF16)]),
        compiler_params=_cparams("arbitrary"),
        name="moe_experts",
    )(blk_expert, n_used, xs, w_gate, w_up, w_down)


def _combine_kernel(dest_ref, s_ref, wt_ref, mod_ref, fg_ref, ys_ref, o_ref, y0, y1, sem,
                    *, ctx_tiles, final_norm):
    i = pl.program_id(0)
    base = i * (2 * TM)

    def start(r, c):
        _row_copy(ys_ref, dest_ref[base + r], y0, r, sem).start()
        _row_copy(ys_ref, dest_ref[base + TM + r], y1, r, sem).start()
        return c

    lax.fori_loop(0, TM, start, 0, unroll=8)
    pltpu.make_async_copy(ys_ref.at[pl.ds(0, TM), :], y0, sem).wait()
    pltpu.make_async_copy(ys_ref.at[pl.ds(0, TM), :], y1, sem).wait()
    gate = _mod_rows(mod_ref, i, ctx_tiles, 3)[2]
    wt = wt_ref[...]
    y = wt[:, 0:1] * y0[...] + wt[:, 1:2] * y1[...]
    out = s_ref[...] + gate * y
    if final_norm:
        out = _rms(out, fg_ref[...])
    o_ref[...] = out


def _combine(dest, s, wts, mods, final_g, ys, ctx_tiles, final_norm):
    n, d = s.shape
    return pl.pallas_call(
        functools.partial(_combine_kernel, ctx_tiles=ctx_tiles, final_norm=final_norm),
        out_shape=jax.ShapeDtypeStruct((n, d), F32),
        grid_spec=pltpu.PrefetchScalarGridSpec(
            num_scalar_prefetch=1,
            grid=(n // TM,),
            in_specs=[pl.BlockSpec((TM, d), lambda i, dst: (i, 0)),
                      pl.BlockSpec((TM, LANES), lambda i, dst: (i, 0)),
                      pl.BlockSpec(mods.shape, lambda i, dst: (0, 0)),
                      pl.BlockSpec((1, d), lambda i, dst: (0, 0)),
                      pl.BlockSpec(memory_space=pl.ANY)],
            out_specs=pl.BlockSpec((TM, d), lambda i, dst: (i, 0)),
            scratch_shapes=[pltpu.VMEM((TM, d), F32), pltpu.VMEM((TM, d), F32),
                            pltpu.SemaphoreType.DMA]),
        compiler_params=_cparams("arbitrary"),
        name="moe_combine",
    )(dest, s, wts, mods, final_g.reshape(1, d), ys)


def _moe_layer(s, mods, norm_g, rg_w, rg_b, re_w, re_b, w_gate, w_up, w_down, layer, ctx_tiles,
               final_g, final_norm):
    n, d = s.shape
    h, e1, e2, r1, r2, wts, cnt = _router(s, mods, norm_g, rg_w, rg_b, re_w, re_b, ctx_tiles)
    counts = cnt[MOE_GROUPS:MOE_GROUPS + MOE_EXPERTS, 0].astype(jnp.int32)
    nb = (2 * n + MOE_EXPERTS * (MOE_ROWS - 1)) // MOE_ROWS + 1
    d1, d2, blk = _finalize(counts, e1, e2, r1, r2, nb)
    dest = jnp.concatenate([d1, d2], axis=1).reshape(2 * n)
    n_used = blk[2, :1]
    xs = _dispatch(dest, blk[1, :MOE_EXPERTS], n_used, h, nb * MOE_ROWS)
    ys = _experts(blk[0, :nb], n_used, xs, w_gate, w_up, w_down, layer)
    return _combine(dest, s, wts, mods, final_g, ys, ctx_tiles, final_norm)


def _conv_tile(x, prev_ref, next_ref, has_prev, has_next, w_ref, b_ref):
    rows = x.shape[0]
    ridx = lax.broadcasted_iota(jnp.int32, x.shape, 0)
    pm = jnp.where(has_prev, 1.0, 0.0)
    nm = jnp.where(has_next, 1.0, 0.0)
    p2 = prev_ref[SUBLANES - 2:SUBLANES - 1, :] * pm
    p1 = prev_ref[SUBLANES - 1:SUBLANES, :] * pm
    n1 = next_ref[0:1, :] * nm
    xm1 = jnp.where(ridx == 0, p1, pltpu.roll(x, 1, axis=0))
    xm2 = jnp.where(ridx == 0, p2, jnp.where(ridx == 1, p1, pltpu.roll(x, 2, axis=0)))
    xp1 = jnp.where(ridx == rows - 1, n1, pltpu.roll(x, rows - 1, axis=0))
    return (xm2 * w_ref[0:1, :] + xm1 * w_ref[1:2, :] + x * w_ref[2:3, :]
            + xp1 * w_ref[3:4, :] + b_ref[...])


def _ml_proj_kernel(s_ref, mod_ref, g_ref, w_ref, wg_ref, gb_ref, qk_ref, v_ref, o_ref, gt_ref):
    i = pl.program_id(0)
    shift, scale, _ = _mod_rows(mod_ref, i, 1, 0)
    h = _rms(s_ref[...], g_ref[...]) * (1.0 + scale) + shift
    z = _bdot(h, w_ref[...])
    nqk = qk_ref.shape[1]
    nv = v_ref.shape[1]
    qk_ref[...] = z[:, :nqk]
    v_ref[...] = z[:, nqk:nqk + nv].astype(BF16)
    o_ref[...] = z[:, nqk + nv:]
    pre = jnp.dot(h, wg_ref[...], precision=HI, preferred_element_type=F32) + gb_ref[...]
    lane = lax.broadcasted_iota(jnp.int32, pre.shape, 1)
    is_forget = ((lane >> 2) & 1) == 1
    gt_ref[...] = jnp.where(is_forget, -_softplus(-pre), pre)


def _ml_proj(s, mods, norm_g, w_in, gate_b):
    n, d = s.shape
    nqk = 2 * ML_HEADS * ML_DK
    nv = ML_HEADS * ML_DV
    n_main = nqk + 2 * nv
    n_gate = w_in.shape[1] - n_main
    w_main = w_in[:, :n_main].astype(BF16)
    w_gate = jnp.concatenate([w_in[:, n_main:], jnp.zeros((d, LANES - n_gate), F32)], axis=1)
    gb = jnp.concatenate([gate_b.reshape(n_gate), jnp.zeros((LANES - n_gate,), F32)]).reshape(1, LANES)
    tile = lambda w: pl.BlockSpec((TM, w), lambda i: (i, 0))
    return pl.pallas_call(
        _ml_proj_kernel,
        out_shape=(jax.ShapeDtypeStruct((n, nqk), F32), jax.ShapeDtypeStruct((n, nv), BF16),
                   jax.ShapeDtypeStruct((n, nv), F32), jax.ShapeDtypeStruct((n, LANES), F32)),
        grid=(n // TM,),
        in_specs=[tile(d), _full(mods.shape), _full((1, d)), _full(w_main.shape),
                  _full((d, LANES)), _full((1, LANES))],
        out_specs=(tile(nqk), tile(nv), tile(nv), tile(LANES)),
        compiler_params=_cparams("arbitrary"),
        name="mlstm_proj",
    )(s, mods, norm_g.reshape(1, d), w_main, w_gate, gb)


def _ml_chunk_index(j, n_chunks, ctx_chunks, reverse):
    if not reverse:
        return j
    return jnp.where(j < ctx_chunks, ctx_chunks - 1 - j, n_chunks - 1 + ctx_chunks - j)


def _ml_rec_kernel(qk_ref, qkp_ref, qkn_ref, v_ref, gt_ref, gtt_ref, cw_ref, cb_ref, o_ref,
                   c_scr, n_scr, m_scr, *, reverse, n_chunks, ctx_chunks):
    j = pl.program_id(0)
    c = _ml_chunk_index(j, n_chunks, ctx_chunks, reverse)

    @pl.when(j == 0)
    def _():
        c_scr[...] = jnp.zeros_like(c_scr)
        n_scr[...] = jnp.zeros_like(n_scr)
        m_scr[...] = jnp.zeros_like(m_scr)

    has_prev = (c != 0) & (c != ctx_chunks)
    has_next = (c != ctx_chunks - 1) & (c != n_chunks - 1)
    qk = _silu(_conv_tile(qk_ref[...], qkp_ref, qkn_ref, has_prev, has_next, cw_ref, cb_ref))
    L = ML_CHUNK
    ri = lax.broadcasted_iota(jnp.int32, (L, L), 0)
    ci = lax.broadcasted_iota(jnp.int32, (L, L), 1)
    past = (ci >= ri) if reverse else (ci <= ri)
    pastf = past.astype(F32)
    gt = gt_ref[...]
    gtt = gtt_ref[...]
    b_col = jnp.dot(pastf, gt, precision=HI, preferred_element_type=F32)
    b_row = jnp.dot(gtt, pastf.T, precision=HI, preferred_element_type=F32)
    last = 0 if reverse else L - 1
    dbase = 8 if reverse else 0
    nq = ML_HEADS * ML_DK
    for hd in range(ML_HEADS):
        cl = dbase + hd
        cf = dbase + 4 + hd
        q = qk[:, hd * ML_DK:(hd + 1) * ML_DK] * (ML_DK ** -0.5)
        k = qk[:, nq + hd * ML_DK:nq + (hd + 1) * ML_DK]
        v = v_ref[:, hd * ML_DV:(hd + 1) * ML_DV]
        li_c = gt[:, cl:cl + 1]
        li_r = gtt[cl:cl + 1, :]
        b_c = b_col[:, cf:cf + 1]
        b_r = b_row[cf:cf + 1, :]
        g = b_r[:, last:last + 1]
        m0 = m_scr[hd:hd + 1, 0:1]
        c0 = c_scr[hd]
        n0 = n_scr[hd:hd + 1, :]
        a_c = g - b_c + li_c
        a_r = g - b_r + li_r
        m_loc = jnp.max(a_r, axis=-1, keepdims=True)
        inter = b_c + m0
        dlog = jnp.where(past, b_c - b_r + li_r, -jnp.inf)
        m = jnp.maximum(inter, jnp.max(dlog, axis=-1, keepdims=True))
        qb = q.astype(BF16)
        sc = lax.dot_general(qb, k.astype(BF16), (((1,), (1,)), ((), ())),
                             preferred_element_type=F32) * jnp.exp(dlog - m)
        w_inter = jnp.exp(inter - m)
        num = (jnp.dot(sc.astype(BF16), v, preferred_element_type=F32)
               + w_inter * jnp.dot(qb, c0.astype(BF16), preferred_element_type=F32))
        den = (jnp.sum(sc, axis=-1, keepdims=True)
               + w_inter * jnp.sum(q * n0, axis=-1, keepdims=True))
        o_ref[:, hd * ML_DV:(hd + 1) * ML_DV] = num / jnp.maximum(jnp.abs(den), jnp.exp(-m))
        m_new = jnp.maximum(g + m0, m_loc)
        dec = jnp.exp(g + m0 - m_new)
        scl = jnp.exp(m_loc - m_new)
        kw = k * jnp.exp(a_c - m_loc)
        c_scr[hd] = dec * c0 + scl * jnp.dot(kw.T.astype(BF16), v, preferred_element_type=F32)
        n_scr[hd:hd + 1, :] = dec * n0 + scl * jnp.sum(kw, axis=0, keepdims=True)
        m_scr[hd:hd + 1, :] = jnp.broadcast_to(m_new, (1, LANES))


def _ml_rec(qk, v, gt, gtt, conv_w, conv_b, reverse):
    n, nqk = qk.shape
    nv = v.shape[1]
    L = ML_CHUNK
    nc = n // L
    cc = TM // L
    hb = L // SUBLANES
    idx = lambda j: _ml_chunk_index(j, nc, cc, reverse)
    last8 = n // SUBLANES - 1
    return pl.pallas_call(
        functools.partial(_ml_rec_kernel, reverse=reverse, n_chunks=nc, ctx_chunks=cc),
        out_shape=jax.ShapeDtypeStruct((n, nv), F32),
        grid=(nc,),
        in_specs=[pl.BlockSpec((L, nqk), lambda j: (idx(j), 0)),
                  pl.BlockSpec((SUBLANES, nqk), lambda j: (jnp.maximum(idx(j) * hb - 1, 0), 0)),
                  pl.BlockSpec((SUBLANES, nqk), lambda j: (jnp.minimum((idx(j) + 1) * hb, last8), 0)),
                  pl.BlockSpec((L, nv), lambda j: (idx(j), 0)),
                  pl.BlockSpec((L, LANES), lambda j: (idx(j), 0)),
                  pl.BlockSpec((2 * SUBLANES, L), lambda j: (0, idx(j))),
                  _full((4, nqk)), _full((1, nqk))],
        out_specs=pl.BlockSpec((L, nv), lambda j: (idx(j), 0)),
        scratch_shapes=[pltpu.VMEM((ML_HEADS, ML_DK, ML_DV), F32),
                        pltpu.VMEM((SUBLANES, ML_DK), F32),
                        pltpu.VMEM((SUBLANES, LANES), F32)],
        compiler_params=_cparams("arbitrary"),
        name="mlstm_rev" if reverse else "mlstm_fwd",
    )(qk, qk, qk, v, gt, gtt, conv_w, conv_b.reshape(1, nqk))


def _ml_out_kernel(hf_ref, hr_ref, o_ref, s_ref, mod_ref, ng_ref, w_ref, out_ref, p_scr):
    i = pl.program_id(0)
    gate = _mod_rows(mod_ref, i, 1, 0)[2]
    hs = hf_ref[...] + hr_ref[...]
    sig = _sigmoid(o_ref[...])
    ng = ng_ref[...]
    for hd in range(ML_HEADS):
        cs = slice(hd * ML_DV, (hd + 1) * ML_DV)
        seg = hs[:, cs]
        hn = seg * lax.rsqrt(jnp.mean(seg * seg, axis=-1, keepdims=True) + EPS) * ng[:, cs]
        p_scr[:, cs] = (hn * sig[:, cs]).astype(BF16)
    y = jnp.dot(p_scr[...], w_ref[...], preferred_element_type=F32)
    out_ref[...] = s_ref[...] + gate * y


def _ml_out(hf, hr, o, s, mods, norm_g, w_out):
    n, d = s.shape
    nv = hf.shape[1]
    tile = lambda w: pl.BlockSpec((TM, w), lambda i: (i, 0))
    return pl.pallas_call(
        _ml_out_kernel,
        out_shape=jax.ShapeDtypeStruct((n, d), F32),
        grid=(n // TM,),
        in_specs=[tile(nv), tile(nv), tile(nv), tile(d), _full(mods.shape), _full((1, nv)),
                  _full(w_out.shape)],
        out_specs=tile(d),
        scratch_shapes=[pltpu.VMEM((TM, nv), BF16)],
        compiler_params=_cparams("arbitrary"),
        name="mlstm_out",
    )(hf, hr, o, s, mods, norm_g.reshape(1, nv), w_out.astype(BF16))


def _mlstm_layer(s, mods, norm_g, w_in, conv_w, conv_b, gate_b, ml_norm_g, w_out):
    qk, v, o, gt = _ml_proj(s, mods, norm_g, w_in, gate_b)
    gtt = gt[:, :2 * SUBLANES].T
    hf = _ml_rec(qk, v, gt, gtt, conv_w, conv_b, False)
    hr = _ml_rec(qk, v, gt, gtt, conv_w, conv_b, True)
    return _ml_out(hf, hr, o, s, mods, ml_norm_g, w_out)


def _lru_proj_kernel(s_ref, mod_ref, g_ref, w_ref, gl_ref, xr_ref):
    i = pl.program_id(0)
    shift, scale, _ = _mod_rows(mod_ref, i, 1, 0)
    h = _rms(s_ref[...], g_ref[...]) * (1.0 + scale) + shift
    z = _bdot(h, w_ref[...])
    w = gl_ref.shape[1]
    gl_ref[...] = _gelu(z[:, :w])
    xr_ref[...] = z[:, w:]


def _lru_proj(s, mods, norm_g, w_in):
    n, d = s.shape
    w = w_in.shape[1] // 2
    tile = lambda c: pl.BlockSpec((TM, c), lambda i: (i, 0))
    return pl.pallas_call(
        _lru_proj_kernel,
        out_shape=(jax.ShapeDtypeStruct((n, w), F32), jax.ShapeDtypeStruct((n, w), F32)),
        grid=(n // TM,),
        in_specs=[tile(d), _full(mods.shape), _full((1, d)), _full(w_in.shape)],
        out_specs=(tile(w), tile(w)),
        compiler_params=_cparams("arbitrary"),
        name="rglru_proj",
    )(s, mods, norm_g.reshape(1, d), w_in.astype(BF16))


def _lru_tile_index(j, n_tiles, reverse):
    if not reverse:
        return j
    return jnp.where(j == 0, 0, n_tiles - j)


def _lru_scan_kernel(x_ref, xp_ref, xn_ref, cw_ref, cb_ref, wg_ref, ba_ref, bx_ref, lam_ref,
                     o_ref, a_scr, u_scr, carry, *, reverse, n_tiles):
    j = pl.program_id(0)
    t = _lru_tile_index(j, n_tiles, reverse)

    @pl.when(j == 0)
    def _():
        carry[...] = jnp.zeros_like(carry)

    has_prev = t > 1
    has_next = (t != 0) & (t != n_tiles - 1)
    xr = _conv_tile(x_ref[...], xp_ref, xn_ref, has_prev, has_next, cw_ref, cb_ref)
    sp = _softplus(-lam_ref[...])
    B = LRU_BLOCK
    for hd in range(LRU_HEADS):
        cs = slice(hd * B, (hd + 1) * B)
        xh = xr[:, cs]
        y = jnp.dot(xh.astype(BF16), wg_ref[hd], preferred_element_type=F32)
        r = _sigmoid(y[:, :B] + ba_ref[:, cs])
        ig = _sigmoid(y[:, B:] + bx_ref[:, cs])
        log_a = -LRU_C * r * sp[:, cs]
        a_scr[:, cs] = jnp.exp(log_a)
        u_scr[:, cs] = jnp.sqrt(1.0 - jnp.exp(2.0 * log_a)) * (ig * xh)

    S = SUBLANES
    w = a_scr.shape[1]
    sidx = lax.broadcasted_iota(jnp.int32, (S, w), 0)

    def group(gi, c):
        g = (TM // S - 1 - gi) if reverse else gi
        r0 = pl.multiple_of(g * S, S)
        a = a_scr[pl.ds(r0, S), :]
        u = u_scr[pl.ds(r0, S), :]
        for sft in (1, 2, 4):
            if reverse:
                ok = sidx < S - sft
                a_e = pltpu.roll(a, S - sft, axis=0)
                u_e = pltpu.roll(u, S - sft, axis=0)
            else:
                ok = sidx >= sft
                a_e = pltpu.roll(a, sft, axis=0)
                u_e = pltpu.roll(u, sft, axis=0)
            u = jnp.where(ok, a * u_e + u, u)
            a = jnp.where(ok, a * a_e, a)
        hcur = a * carry[...] + u
        o_ref[pl.ds(r0, S), :] = hcur
        edge = 0 if reverse else S - 1
        carry[...] = jnp.broadcast_to(hcur[edge:edge + 1, :], (S, w))
        return c

    lax.fori_loop(0, TM // S, group, 0)


def _lru_scan(xraw, conv_w, conv_b, w_a, b_a, w_x, b_x, lam, reverse):
    n, w = xraw.shape
    nt = n // TM
    hb = TM // SUBLANES
    idx = lambda j: _lru_tile_index(j, nt, reverse)
    last8 = n // SUBLANES - 1
    wg = jnp.concatenate([w_a, w_x], axis=-1).astype(BF16)
    return pl.pallas_call(
        functools.partial(_lru_scan_kernel, reverse=reverse, n_tiles=nt),
        out_shape=jax.ShapeDtypeStruct((n, w), F32),
        grid=(nt,),
        in_specs=[pl.BlockSpec((TM, w), lambda j: (idx(j), 0)),
                  pl.BlockSpec((SUBLANES, w), lambda j: (jnp.maximum(idx(j) * hb - 1, 0), 0)),
                  pl.BlockSpec((SUBLANES, w), lambda j: (jnp.minimum((idx(j) + 1) * hb, last8), 0)),
                  _full((4, w)), _full((1, w)), _full(wg.shape), _full((1, w)), _full((1, w)),
                  _full((1, w))],
        out_specs=pl.BlockSpec((TM, w), lambda j: (idx(j), 0)),
        scratch_shapes=[pltpu.VMEM((TM, w), F32), pltpu.VMEM((TM, w), F32),
                        pltpu.VMEM((SUBLANES, w), F32)],
        compiler_params=_cparams("arbitrary"),
        name="rglru_rev" if reverse else "rglru_fwd",
    )(xraw, xraw, xraw, conv_w, conv_b.reshape(1, w), wg, b_a.reshape(1, w), b_x.reshape(1, w),
      lam.reshape(1, w))


def _lru_out_kernel(gl_ref, hf_ref, hr_ref, s_ref, mod_ref, w_ref, out_ref):
    gate = _mod_rows(mod_ref, 1, 0, 0)[2]
    p = gl_ref[...] * (hf_ref[...] + hr_ref[...])
    out_ref[...] = s_ref[...] + gate * _bdot(p, w_ref[...])


def _lru_out(gl, hf, hr, s, mods, w_out):
    n, d = s.shape
    w = gl.shape[1]
    lat = lambda c: pl.BlockSpec((TM, c), lambda i: (i + 1, 0))
    return pl.pallas_call(
        _lru_out_kernel,
        out_shape=jax.ShapeDtypeStruct((n - TM, d), F32),
        grid=(n // TM - 1,),
        in_specs=[lat(w), lat(w), lat(w), lat(d), _full(mods.shape), _full(w_out.shape)],
        out_specs=pl.BlockSpec((TM, d), lambda i: (i, 0)),
        compiler_params=_cparams("arbitrary"),
        name="rglru_out",
    )(gl, hf, hr, s, mods, w_out.astype(BF16))


def _rglru_layer(s, mods, norm_g, w_in, conv_w, conv_b, w_a, b_a, w_x, b_x, lam, w_out):
    gl, xraw = _lru_proj(s, mods, norm_g, w_in)
    hf = _lru_scan(xraw, conv_w, conv_b, w_a[0], b_a[0], w_x[0], b_x[0], lam[0], False)
    hr = _lru_scan(xraw, conv_w, conv_b, w_a[1], b_a[1], w_x[1], b_x[1], lam[1], True)
    return _lru_out(gl, hf, hr, s, mods, w_out)


def _fn_proj_kernel(s_ref, mod_ref, g_ref, wt_ref, cs_ref, yr_ref, yi_ref, ar_scr, ai_scr):
    shift, scale, _ = _mod_rows(mod_ref, 1, 0, 0)
    nm = wt_ref.shape[0]
    gw = nm // FN_GROUPS
    per = TM // FFT_N2
    nj = FN_TB // FFT_N2
    csb = cs_ref[...].astype(BF16)

    def sub(tc, c):
        r0 = pl.multiple_of(tc * TM, TM)
        h = _rms(s_ref[pl.ds(r0, TM), :], g_ref[...]) * (1.0 + scale) + shift
        zt = lax.dot_general(wt_ref[...], h.astype(BF16), (((1,), (1,)), ((), ())),
                             preferred_element_type=F32).astype(BF16)
        for g in range(FN_GROUPS):
            y = jnp.dot(csb, zt[g * gw:(g + 1) * gw, :], preferred_element_type=F32)
            for q in range(per):
                row0 = pl.multiple_of((tc * per + q) * nm + g * gw, gw)
                ar_scr[pl.ds(row0, gw), :] = y[:gw, q * FFT_N2:(q + 1) * FFT_N2]
                ai_scr[pl.ds(row0, gw), :] = y[gw:, q * FFT_N2:(q + 1) * FFT_N2]
        return c

    lax.fori_loop(0, FN_TB // TM, sub, 0)

    def relayout(m, c):
        yr_ref[m] = ar_scr[pl.ds(m, nj, stride=nm), :]
        yi_ref[m] = ai_scr[pl.ds(m, nj, stride=nm), :]
        return c

    lax.fori_loop(0, nm, relayout, 0, unroll=8)


def _dft_cos_sin(n, scale):
    k = np.arange(n, dtype=np.int64)
    ang = 2.0 * np.pi * ((k[:, None] * k[None, :]) % n).astype(np.float64) / n
    return np.cos(ang) * scale, np.sin(ang) * scale


def _fn_proj(s, mods, norm_g, w_in):
    t, d = s.shape
    nm = w_in.shape[1]
    gw = nm // FN_GROUPS
    n1 = t // FFT_N2
    nj = FN_TB // FFT_N2
    c, sn = _dft_cos_sin(gw, gw ** -0.5)
    cs = jnp.asarray(np.concatenate([c, -sn], axis=0), F32)
    yspec = pl.BlockSpec((nm, nj, FFT_N2), lambda i: (0, i, 0))
    yshape = jax.ShapeDtypeStruct((nm, n1, FFT_N2), F32)
    return pl.pallas_call(
        _fn_proj_kernel,
        out_shape=(yshape, yshape),
        grid=(t // FN_TB,),
        in_specs=[pl.BlockSpec((FN_TB, d), lambda i: (i, 0)), _full(mods.shape), _full((1, d)),
                  _full((nm, d)), _full(cs.shape)],
        out_specs=(yspec, yspec),
        scratch_shapes=[pltpu.VMEM((nj * nm, FFT_N2), F32), pltpu.VMEM((nj * nm, FFT_N2), F32)],
        compiler_params=_cparams("arbitrary"),
        name="fourier_proj",
    )(s, mods, norm_g.reshape(1, d), w_in.T.astype(BF16), cs)


def _fn_fft_kernel(yr_ref, yi_ref, m_ref, tc_ref, ts_ref, d_ref, o_ref):
    n1 = yr_ref.shape[1]
    n2 = FFT_N2
    xr = jnp.concatenate([yr_ref[m].astype(BF16) for m in range(FN_CB)], axis=1)
    xi = jnp.concatenate([yi_ref[m].astype(BF16) for m in range(FN_CB)], axis=1)
    a = jnp.dot(m_ref[...].astype(BF16), jnp.concatenate([xr, xi], axis=0),
                preferred_element_type=F32)
    ar = a[:n1]
    ai = a[n1:]
    tc = jnp.concatenate([tc_ref[...]] * FN_CB, axis=1)
    ts = jnp.concatenate([ts_ref[...]] * FN_CB, axis=1)
    br = ar * tc + ai * ts
    bi = ai * tc - ar * ts
    bst = jnp.concatenate(
        [jnp.concatenate([br[:, m * n2:(m + 1) * n2], bi[:, m * n2:(m + 1) * n2]], axis=1)
         for m in range(FN_CB)], axis=0).astype(BF16)
    res = lax.dot_general(d_ref[...].astype(BF16), bst, (((1,), (1,)), ((), ())),
                          preferred_element_type=F32)
    for m in range(FN_CB):
        o_ref[m] = res[:, m * n1:(m + 1) * n1]


def _fn_fft(yr, yi):
    nm, n1, n2 = yr.shape
    t = n1 * n2
    c, sn = _dft_cos_sin(n1, n1 ** -0.5)
    m = jnp.asarray(np.block([[c, sn], [-sn, c]]), F32)
    k1 = np.arange(n1, dtype=np.int64)[:, None]
    t2 = np.arange(n2, dtype=np.int64)[None, :]
    ang = 2.0 * np.pi * ((k1 * t2) % t).astype(np.float64) / t
    tc = jnp.asarray(np.cos(ang), F32)
    ts = jnp.asarray(np.sin(ang), F32)
    c2, s2 = _dft_cos_sin(n2, n2 ** -0.5)
    dm = jnp.asarray(np.concatenate([c2, s2], axis=1), F32)
    yspec = pl.BlockSpec((FN_CB, n1, n2), lambda i: (i, 0, 0))
    return pl.pallas_call(
        _fn_fft_kernel,
        out_shape=jax.ShapeDtypeStruct((nm, n2, n1), F32),
        grid=(nm // FN_CB,),
        in_specs=[yspec, yspec, _full(m.shape), _full(tc.shape), _full(ts.shape), _full(dm.shape)],
        out_specs=pl.BlockSpec((FN_CB, n2, n1), lambda i: (i, 0, 0)),
        compiler_params=_cparams("arbitrary"),
        name="fourier_fft",
    )(yr, yi, m, tc, ts, dm)


def _fn_out_kernel(ft_ref, w_ref, s_ref, mod_ref, o_ref, a_scr):
    gate = _mod_rows(mod_ref, 1, 0, 0)[2]
    nm, nj, n1 = ft_ref.shape

    def relayout(m, c):
        a_scr[pl.ds(m, nj, stride=nm), :] = ft_ref[m]
        return c

    lax.fori_loop(0, nm, relayout, 0, unroll=8)
    for j in range(nj):
        slab = a_scr[j * nm:(j + 1) * nm, :].astype(BF16)
        y = lax.dot_general(slab, w_ref[...], (((0,), (0,)), ((), ())), preferred_element_type=F32)
        rows = slice(j * n1, (j + 1) * n1)
        o_ref[rows, :] = s_ref[rows, :] + gate * y


def _fn_out(ft, s, mods, w_out):
    t, d = s.shape
    nm, n2, n1 = ft.shape
    nj = FN_TB // n1
    tok = pl.BlockSpec((FN_TB, d), lambda i: (i, 0))
    return pl.pallas_call(
        _fn_out_kernel,
        out_shape=jax.ShapeDtypeStruct((t, d), F32),
        grid=(t // FN_TB,),
        in_specs=[pl.BlockSpec((nm, nj, n1), lambda i: (0, i, 0)), _full(w_out.shape), tok,
                  _full(mods.shape)],
        out_specs=tok,
        scratch_shapes=[pltpu.VMEM((nj * nm, n1), F32)],
        compiler_params=_cparams("arbitrary"),
        name="fourier_out",
    )(ft, w_out.astype(BF16), s, mods)


def _fourier_layer(s, mods, norm_g, w_in, w_out):
    yr, yi = _fn_proj(s, mods, norm_g, w_in)
    return _fn_out(_fn_fft(yr, yi), s, mods, w_out)


def kernel(x, c, ctx, c_ctx, ada_w, ada_b, norm_mix_g, norm_ffn_g, final_norm_g, router_group_w, router_group_b, router_expert_w, router_expert_b, expert_w_gate, expert_w_up, expert_w_down, cm_w_in, cm_v_norm_g, cm_w_s, cm_b_s, cm_w_out, ml_w_in, ml_conv_w, ml_conv_b, ml_gate_b, ml_norm_g, ml_w_out, lru_w_in, lru_conv_w, lru_conv_b, lru_w_a, lru_b_a, lru_w_x, lru_b_x, lru_lambda, lru_w_out, fn_w_in, fn_w_out):
    bsz, seq, d = x.shape
    assert bsz == 1 and ada_w.shape[0] == 4 and ctx.shape[1] == TM
    c_rows = jnp.concatenate([c_ctx[None, :], c, jnp.zeros((SUBLANES - 2, d), F32)], axis=0)
    mods = _ada_table(c_rows, ada_w, ada_b)
    s = _prep_stream(x[0], ctx[0])

    def moe(s, i, ctx_tiles, final_norm=False):
        return _moe_layer(s, mods[i], norm_ffn_g[i], router_group_w[i], router_group_b[i],
                          router_expert_w[i], router_expert_b[i], expert_w_gate, expert_w_up,
                          expert_w_down, i, ctx_tiles, final_norm_g, final_norm)

    s = _chunk_mlp_layer(s, mods[0], norm_mix_g[0], cm_w_in[0], cm_v_norm_g[0], cm_w_s[0],
                         cm_b_s[0], cm_w_out[0])
    s = moe(s, 0, 1)
    s = _mlstm_layer(s, mods[1], norm_mix_g[1], ml_w_in[0], ml_conv_w[0], ml_conv_b[0],
                     ml_gate_b[0], ml_norm_g[0], ml_w_out[0])
    s = moe(s, 1, 1)
    s = _rglru_layer(s, mods[2], norm_mix_g[2], lru_w_in[0], lru_conv_w[0], lru_conv_b[0],
                     lru_w_a[0], lru_b_a[0], lru_w_x[0], lru_b_x[0], lru_lambda[0], lru_w_out[0])
    s = moe(s, 2, 0)
    s = _fourier_layer(s, mods[3], norm_mix_g[3], fn_w_in[0], fn_w_out[0])
    s = moe(s, 3, 0, final_norm=True)
    return s[None]
```

```python
import functools
import math

import jax
import jax.numpy as jnp
import numpy as np
from jax import lax
from jax.experimental import pallas as pl
from jax.experimental.pallas import tpu as pltpu

F32 = jnp.float32
BF16 = jnp.bfloat16

EPS = 1e-6
POS_BASE = 10000.0
GRID_W = 64
N_MOD = 6
TM = 256
LANES = 128
SUBLANES = 8
VMEM_LIMIT = 56 * 1024 * 1024

CM_CHUNK = 128
CM_GROUPS = 4
ML_HEADS = 4
ML_DK = 128
ML_DV = 256
ML_CHUNK = 128
LRU_HEADS = 10
LRU_BLOCK = 128
LRU_C = 8.0
FN_GROUPS = 4
FFT_N2 = 128
MOE_GROUPS = 4
MOE_EPG = 8
MOE_EXPERTS = MOE_GROUPS * MOE_EPG
MOE_ROWS_LOG2 = 8
MOE_ROWS = 1 << MOE_ROWS_LOG2
ROUTE_ROWS = 40
FN_TB = 1024
FN_CB = 8
CONV_LEFT = 2

HI = lax.Precision.HIGHEST


def _cparams(*sem):
    return pltpu.CompilerParams(dimension_semantics=sem, vmem_limit_bytes=VMEM_LIMIT)


def _full(shape):
    nd = len(shape)
    return pl.BlockSpec(shape, lambda *_: (0,) * nd)


def _rms(x, g):
    return x * lax.rsqrt(jnp.mean(x * x, axis=-1, keepdims=True) + EPS) * g


def _gelu(x):
    c = math.sqrt(2.0 / math.pi)
    return 0.5 * x * (1.0 + jnp.tanh(c * (x + 0.044715 * (x * x * x))))


def _sigmoid(x):
    return 1.0 / (1.0 + jnp.exp(-x))


def _silu(x):
    return x * _sigmoid(x)


def _softplus(x):
    return jnp.maximum(x, 0.0) + jnp.log(1.0 + jnp.exp(-jnp.abs(x)))


def _mod_rows(mod_ref, tile, ctx_tiles, first):
    row = jnp.where(tile < ctx_tiles, 0, 1)
    m = mod_ref[pl.ds(row, 1), :]
    d = m.shape[1] // N_MOD
    return tuple(m[:, (first + j) * d:(first + j + 1) * d] for j in range(3))


def _bdot(a, b):
    return jnp.dot(a.astype(BF16), b.astype(BF16), preferred_element_type=F32)


def _ada_kernel(c_ref, w_ref, b_ref, o_ref):
    c = c_ref[...]
    o_ref[...] = jnp.dot(_silu(c), w_ref[...], precision=HI,
                         preferred_element_type=F32) + b_ref[...]


def _ada_table(c_rows, ada_w, ada_b):
    depth, d, n = ada_w.shape
    tn = 1024
    return pl.pallas_call(
        _ada_kernel,
        out_shape=jax.ShapeDtypeStruct((depth, SUBLANES, n), F32),
        grid=(depth, n // tn),
        in_specs=[_full((SUBLANES, d)),
                  pl.BlockSpec((None, d, tn), lambda i, j: (i, 0, j)),
                  pl.BlockSpec((None, 1, tn), lambda i, j: (i, 0, j))],
        out_specs=pl.BlockSpec((None, SUBLANES, tn), lambda i, j: (i, 0, j)),
        compiler_params=_cparams("arbitrary", "arbitrary"),
        name="ada_table",
    )(c_rows, ada_w, ada_b.reshape(depth, 1, n))


def _prep_kernel(x_ref, ctx_ref, rt_ref, ct_ref, o_ref):
    i = pl.program_id(0)

    @pl.when(i == 0)
    def _():
        o_ref[...] = ctx_ref[...]

    @pl.when(i > 0)
    def _():
        rows_per_tile = TM // GRID_W
        q2 = rt_ref.shape[1]
        r0 = (i - 1) * rows_per_tile
        rt = jnp.concatenate(
            [jnp.broadcast_to(rt_ref[pl.ds(r0 + j, 1), :], (GRID_W, q2))
             for j in range(rows_per_tile)], axis=0)
        ct = jnp.concatenate([ct_ref[...]] * rows_per_tile, axis=0)
        o_ref[...] = x_ref[...] + jnp.concatenate([rt, ct], axis=1)


def _prep_stream(x2, ctx2):
    seq, d = x2.shape
    n_ctx = ctx2.shape[0]
    assert n_ctx == TM and seq % TM == 0 and TM % GRID_W == 0
    q = d // 4
    freq = jnp.exp(-math.log(POS_BASE) * jnp.arange(q, dtype=F32) / q)
    ar = jnp.arange(seq // GRID_W, dtype=F32)[:, None] * freq
    ac = jnp.arange(GRID_W, dtype=F32)[:, None] * freq
    rt = jnp.concatenate([jnp.sin(ar), jnp.cos(ar)], axis=-1)
    ct = jnp.concatenate([jnp.sin(ac), jnp.cos(ac)], axis=-1)
    nt = 1 + seq // TM
    return pl.pallas_call(
        _prep_kernel,
        out_shape=jax.ShapeDtypeStruct((n_ctx + seq, d), F32),
        grid=(nt,),
        in_specs=[pl.BlockSpec((TM, d), lambda i: (jnp.maximum(i - 1, 0), 0)),
                  _full((TM, d)), _full(rt.shape), _full(ct.shape)],
        out_specs=pl.BlockSpec((TM, d), lambda i: (i, 0)),
        compiler_params=_cparams("arbitrary"),
        name="prep_stream",
    )(x2, ctx2, rt, ct)


def _cm_kernel(s_ref, mod_ref, g_ref, win_ref, vg_ref, ws_ref, bs_ref, wout_ref, o_ref, p_scr):
    i = pl.program_id(0)
    shift, scale, gate = _mod_rows(mod_ref, i, 1, 0)
    x = s_ref[...]
    h = _rms(x, g_ref[...]) * (1.0 + scale) + shift
    z = _gelu(_bdot(h, win_ref[...]))
    w = z.shape[1] // 2
    u = z[:, :w]
    v = _rms(z[:, w:], vg_ref[...]).astype(BF16)
    gw = w // CM_GROUPS
    for c in range(TM // CM_CHUNK):
        r = slice(c * CM_CHUNK, (c + 1) * CM_CHUNK)
        for g in range(CM_GROUPS):
            cs = slice(g * gw, (g + 1) * gw)
            s = jnp.dot(ws_ref[g], v[r, cs], preferred_element_type=F32) + bs_ref[:, g:g + 1]
            p_scr[r, cs] = (u[r, cs] * s).astype(BF16)
    y = jnp.dot(p_scr[...], wout_ref[...], preferred_element_type=F32)
    o_ref[...] = x + gate * y


def _chunk_mlp_layer(s, mods, norm_g, w_in, v_g, w_s, b_s, w_out):
    n, d = s.shape
    w = w_out.shape[0]
    return pl.pallas_call(
        _cm_kernel,
        out_shape=jax.ShapeDtypeStruct((n, d), F32),
        grid=(n // TM,),
        in_specs=[pl.BlockSpec((TM, d), lambda i: (i, 0)),
                  _full(mods.shape), _full((1, d)), _full(w_in.shape), _full((1, w)),
                  _full(w_s.shape), _full((CM_CHUNK, CM_GROUPS)), _full(w_out.shape)],
        out_specs=pl.BlockSpec((TM, d), lambda i: (i, 0)),
        scratch_shapes=[pltpu.VMEM((TM, w), BF16)],
        compiler_params=_cparams("arbitrary"),
        name="chunk_mlp",
    )(s, mods, norm_g.reshape(1, d), w_in.astype(BF16), v_g.reshape(1, w),
      w_s.astype(BF16), b_s.T, w_out.astype(BF16))


def _store_token_tiles(ref, x):
    rows, d = x.shape
    for j in range(d // LANES):
        ref[pl.ds(j, rows, stride=d // LANES), :] = x[:, j * LANES:(j + 1) * LANES]


def _load_token_tiles(ref):
    chunks = SUBLANES
    rows = ref.shape[0] // chunks
    return jnp.concatenate([ref[pl.ds(j, rows, stride=chunks), :] for j in range(chunks)], axis=1)


def _router_kernel(s_ref, mod_ref, g_ref, rwt_ref, rbt_ref, h_ref, e1_ref, e2_ref, r1_ref, r2_ref,
                   wt_ref, cnt_ref, carry, *, ctx_tiles):
    i = pl.program_id(0)

    @pl.when(i == 0)
    def _():
        carry[...] = jnp.zeros_like(carry)

    shift, scale, _ = _mod_rows(mod_ref, i, ctx_tiles, 3)
    h = _rms(s_ref[...], g_ref[...]) * (1.0 + scale) + shift
    _store_token_tiles(h_ref, h)
    logits = lax.dot_general(rwt_ref[...], h, (((1,), (1,)), ((), ())), precision=HI,
                             preferred_element_type=F32) + rbt_ref[...]
    row = lax.broadcasted_iota(jnp.int32, logits.shape, 0)
    neg = jnp.float32(-jnp.inf)
    big = jnp.int32(1 << 20)
    is_g = row < MOE_GROUPS
    gl = jnp.where(is_g, logits, neg)
    gmax = jnp.max(gl, axis=0, keepdims=True)
    grp = jnp.min(jnp.where(is_g & (gl == gmax), row, big), axis=0, keepdims=True)
    p_grp = 1.0 / jnp.sum(jnp.exp(gl - gmax), axis=0, keepdims=True)
    e_row = row - MOE_GROUPS
    in_grp = (e_row >= 0) & (e_row < MOE_EXPERTS) & ((e_row >> 3) == grp)
    l1 = jnp.where(in_grp, logits, neg)
    v1 = jnp.max(l1, axis=0, keepdims=True)
    i1 = jnp.min(jnp.where(in_grp & (l1 == v1), row, big), axis=0, keepdims=True)
    rest = in_grp & (row != i1)
    l2 = jnp.where(rest, logits, neg)
    v2 = jnp.max(l2, axis=0, keepdims=True)
    i2 = jnp.min(jnp.where(rest & (l2 == v2), row, big), axis=0, keepdims=True)
    e21 = jnp.exp(v2 - v1)
    w1 = p_grp / (1.0 + e21)
    w2 = p_grp * e21 / (1.0 + e21)
    oh1 = (row == i1).astype(F32)
    oh2 = (row == i2).astype(F32)
    oh = oh1 + oh2
    ri = lax.broadcasted_iota(jnp.int32, (TM, TM), 0)
    ci = lax.broadcasted_iota(jnp.int32, (TM, TM), 1)
    earlier = (ri < ci).astype(BF16)
    before = jnp.dot(oh.astype(BF16), earlier, preferred_element_type=F32) + carry[:, 0:1]
    r1_ref[...] = jnp.sum(oh1 * before, axis=0, keepdims=True).astype(jnp.int32)
    r2_ref[...] = jnp.sum(oh2 * before, axis=0, keepdims=True).astype(jnp.int32)
    e1_ref[...] = i1 - MOE_GROUPS
    e2_ref[...] = i2 - MOE_GROUPS
    carry[...] = carry[...] + jnp.sum(oh, axis=1, keepdims=True)
    cnt_ref[...] = carry[...]
    wt_ref[...] = jnp.concatenate([w1, w2, jnp.zeros((LANES - 2, TM), F32)], axis=0).T


def _router(s, mods, norm_g, rg_w, rg_b, re_w, re_b, ctx_tiles):
    n, d = s.shape
    nt = n // TM
    pad = ROUTE_ROWS - MOE_GROUPS - MOE_EXPERTS
    rwt = jnp.concatenate([rg_w, re_w, jnp.zeros((d, pad), F32)], axis=1).T
    rbt = jnp.broadcast_to(jnp.concatenate([rg_b, re_b, jnp.zeros((pad,), F32)])[:, None],
                           (ROUTE_ROWS, TM))
    assert d == SUBLANES * LANES
    tile = pl.BlockSpec((TM, d), lambda i: (i, 0))
    irow = pl.BlockSpec((None, 1, TM), lambda i: (i, 0, 0))
    ishape = jax.ShapeDtypeStruct((nt, 1, TM), jnp.int32)
    return pl.pallas_call(
        functools.partial(_router_kernel, ctx_tiles=ctx_tiles),
        out_shape=(jax.ShapeDtypeStruct((n * SUBLANES, LANES), F32), ishape, ishape, ishape, ishape,
                   jax.ShapeDtypeStruct((n, LANES), F32),
                   jax.ShapeDtypeStruct((ROUTE_ROWS, LANES), F32)),
        grid=(nt,),
        in_specs=[tile, _full(mods.shape), _full((1, d)), _full((ROUTE_ROWS, d)),
                  _full((ROUTE_ROWS, TM))],
        out_specs=(pl.BlockSpec((TM * SUBLANES, LANES), lambda i: (i, 0)), irow, irow, irow, irow,
                   pl.BlockSpec((TM, LANES), lambda i: (i, 0)), _full((ROUTE_ROWS, LANES))),
        scratch_shapes=[pltpu.VMEM((ROUTE_ROWS, LANES), F32)],
        compiler_params=_cparams("arbitrary"),
        name="moe_router",
    )(s, mods, norm_g.reshape(1, d), rwt, rbt)


def _finalize_kernel(cnt_ref, e1_ref, e2_ref, r1_ref, r2_ref, d1_ref, d2_ref, blk_ref):
    e1 = e1_ref[...]
    e2 = e2_ref[...]
    r1 = r1_ref[...]
    r2 = r2_ref[...]
    d1 = jnp.zeros_like(e1)
    d2 = jnp.zeros_like(e2)
    lane = lax.broadcasted_iota(jnp.int32, blk_ref.shape, 1)
    brow = lane * MOE_ROWS
    sub = lax.broadcasted_iota(jnp.int32, blk_ref.shape, 0)
    be = jnp.zeros(blk_ref.shape, jnp.int32)
    pend = jnp.zeros(blk_ref.shape, jnp.int32)
    ps = jnp.int32(0)
    for e in range(MOE_EXPERTS):
        c = cnt_ref[e]
        pe = ps + lax.shift_left(lax.shift_right_logical(c + (MOE_ROWS - 1), MOE_ROWS_LOG2),
                                 MOE_ROWS_LOG2)
        d1 = jnp.where(e1 == e, ps + r1, d1)
        d2 = jnp.where(e2 == e, ps + r2, d2)
        be = be + (brow >= pe).astype(jnp.int32)
        pend = jnp.where(lane == e, pe, pend)
        ps = pe
    d1_ref[...] = d1
    d2_ref[...] = d2
    n_used = lax.shift_right_logical(ps, MOE_ROWS_LOG2)
    blk_ref[...] = jnp.where(sub == 0, jnp.minimum(be, MOE_EXPERTS - 1),
                             jnp.where(sub == 1, pend, n_used))


def _finalize(counts, e1, e2, r1, r2, nb):
    nbp = (nb + LANES - 1) // LANES * LANES
    whole = pl.BlockSpec(e1.shape, lambda i, c: (0, 0, 0))
    ishape = jax.ShapeDtypeStruct(e1.shape, jnp.int32)
    return pl.pallas_call(
        _finalize_kernel,
        out_shape=(ishape, ishape, jax.ShapeDtypeStruct((SUBLANES, nbp), jnp.int32)),
        grid_spec=pltpu.PrefetchScalarGridSpec(
            num_scalar_prefetch=1,
            grid=(1,),
            in_specs=[whole, whole, whole, whole],
            out_specs=(whole, whole, pl.BlockSpec((SUBLANES, nbp), lambda i, c: (0, 0)))),
        compiler_params=_cparams("arbitrary"),
        name="moe_finalize",
    )(counts, e1, e2, r1, r2)


def _token_copy(src, r, dst, d, sem):
    return pltpu.make_async_copy(src.at[pl.ds(pl.multiple_of(r * SUBLANES, SUBLANES), SUBLANES), :],
                                 dst.at[pl.ds(pl.multiple_of(d * SUBLANES, SUBLANES), SUBLANES), :],
                                 sem)


def _zero_fill_padding(pend_ref, nu_ref, xs_out, zbuf, zsem):
    blk_rows = MOE_ROWS * SUBLANES
    nb = xs_out.shape[0] // blk_rows
    zbuf[...] = jnp.zeros_like(zbuf)

    def block_copy(b):
        r0 = pl.multiple_of(b * blk_rows, blk_rows)
        return pltpu.make_async_copy(zbuf, xs_out.at[pl.ds(r0, blk_rows), :], zsem)

    def seg_last_block(e):
        pe = pend_ref[e]
        prev = pend_ref[e - 1] if e > 0 else 0
        return pe > prev, lax.shift_right_logical(pe, MOE_ROWS_LOG2) - 1

    for e in range(MOE_EXPERTS):
        nonempty, b = seg_last_block(e)

        @pl.when(nonempty)
        def _():
            block_copy(b).start()

    def tail_start(b, c):
        block_copy(b).start()
        return c

    lax.fori_loop(nu_ref[0], nb, tail_start, 0)
    for e in range(MOE_EXPERTS):
        nonempty, b = seg_last_block(e)

        @pl.when(nonempty)
        def _():
            block_copy(b).wait()

    def tail_wait(b, c):
        block_copy(b).wait()
        return c

    lax.fori_loop(nu_ref[0], nb, tail_wait, 0)


def _dispatch_kernel(dest_ref, pend_ref, nu_ref, h_ref, xs_out, sem, zbuf, zsem):
    @pl.when(pl.program_id(0) == 0)
    def _():
        _zero_fill_padding(pend_ref, nu_ref, xs_out, zbuf, zsem)

    base = pl.program_id(0) * (2 * TM)

    def start(r, c):
        _token_copy(h_ref, r, xs_out, dest_ref[base + r], sem).start(priority=0)
        _token_copy(h_ref, r, xs_out, dest_ref[base + TM + r], sem).start(priority=1)
        return c

    lax.fori_loop(0, TM, start, 0, unroll=8)
    for _ in range(2):
        pltpu.make_async_copy(h_ref, xs_out.at[pl.ds(0, TM * SUBLANES), :], sem).wait()


def _dispatch(dest, pad_end, n_used, h, n_rows):
    n = h.shape[0] // SUBLANES
    return pl.pallas_call(
        _dispatch_kernel,
        out_shape=jax.ShapeDtypeStruct((n_rows * SUBLANES, LANES), F32),
        grid_spec=pltpu.PrefetchScalarGridSpec(
            num_scalar_prefetch=3,
            grid=(n // TM,),
            in_specs=[pl.BlockSpec((TM * SUBLANES, LANES), lambda i, *_: (i, 0))],
            out_specs=pl.BlockSpec(memory_space=pl.ANY),
            scratch_shapes=[pltpu.SemaphoreType.DMA, pltpu.VMEM((MOE_ROWS * SUBLANES, LANES), F32),
                            pltpu.SemaphoreType.DMA]),
        compiler_params=_cparams("arbitrary"),
        name="moe_dispatch",
    )(dest, pad_end, n_used, h)


def _expert_kernel(be_ref, nu_ref, x_ref, wg_ref, wu_ref, wd_ref, y_ref, wg_s, wu_s, wd_s):
    b = pl.program_id(0)
    used = b < nu_ref[0]
    prev = be_ref[jnp.maximum(b - 1, 0)]
    fresh = (b == 0) | (be_ref[b] != prev)

    @pl.when(used & fresh)
    def _():
        wg_s[...] = wg_ref[...].astype(BF16)
        wu_s[...] = wu_ref[...].astype(BF16)
        wd_s[...] = wd_ref[...].astype(BF16)

    @pl.when(used)
    def _():
        x = _load_token_tiles(x_ref).astype(BF16)
        a = jnp.dot(x, wg_s[...], preferred_element_type=F32)
        u = jnp.dot(x, wu_s[...], preferred_element_type=F32)
        _store_token_tiles(y_ref, jnp.dot((_silu(a) * u).astype(BF16), wd_s[...],
                                          preferred_element_type=F32))

    @pl.when(jnp.logical_not(used))
    def _():
        y_ref[...] = jnp.zeros_like(y_ref)


def _experts(blk_expert, n_used, xs, w_gate, w_up, w_down, layer):
    d, hid = w_gate.shape[2:]
    nb = xs.shape[0] // (MOE_ROWS * SUBLANES)
    wmap = lambda b, be, nu: (layer, be[b], 0, 0)
    rows = pl.BlockSpec((MOE_ROWS * SUBLANES, LANES), lambda b, be, nu: (b, 0))
    return pl.pallas_call(
        _expert_kernel,
        out_shape=jax.ShapeDtypeStruct(xs.shape, F32),
        grid_spec=pltpu.PrefetchScalarGridSpec(
            num_scalar_prefetch=2,
            grid=(nb,),
            in_specs=[rows,
                      pl.BlockSpec((None, None, d, hid), wmap),
                      pl.BlockSpec((None, None, d, hid), wmap),
                      pl.BlockSpec((None, None, hid, d), wmap)],
            out_specs=rows,
            scratch_shapes=[pltpu.VMEM((d, hid), BF16), pltpu.VMEM((d, hid), BF16),
                            pltpu.VMEM((hid, d), BF16)]),
        compiler_params=_cparams("arbitrary"),
        name="moe_experts",
    )(blk_expert, n_used, xs, w_gate, w_up, w_down)


def _combine_kernel(dest_ref, s_ref, wt_ref, mod_ref, fg_ref, ys_ref, o_ref, y0, y1, sem,
                    *, ctx_tiles, final_norm):
    i = pl.program_id(0)
    base = i * (2 * TM)

    def start(r, c):
        _token_copy(ys_ref, dest_ref[base + r], y0, r, sem).start(priority=0)
        _token_copy(ys_ref, dest_ref[base + TM + r], y1, r, sem).start(priority=1)
        return c

    lax.fori_loop(0, TM, start, 0, unroll=8)
    pltpu.make_async_copy(ys_ref.at[pl.ds(0, TM * SUBLANES), :], y0, sem).wait()
    pltpu.make_async_copy(ys_ref.at[pl.ds(0, TM * SUBLANES), :], y1, sem).wait()
    gate = _mod_rows(mod_ref, i, ctx_tiles, 3)[2]
    wt = wt_ref[...]
    y = wt[:, 0:1] * _load_token_tiles(y0) + wt[:, 1:2] * _load_token_tiles(y1)
    out = s_ref[...] + gate * y
    if final_norm:
        out = _rms(out, fg_ref[...])
    o_ref[...] = out


def _combine(dest, s, wts, mods, final_g, ys, ctx_tiles, final_norm):
    n, d = s.shape
    return pl.pallas_call(
        functools.partial(_combine_kernel, ctx_tiles=ctx_tiles, final_norm=final_norm),
        out_shape=jax.ShapeDtypeStruct((n, d), F32),
        grid_spec=pltpu.PrefetchScalarGridSpec(
            num_scalar_prefetch=1,
            grid=(n // TM,),
            in_specs=[pl.BlockSpec((TM, d), lambda i, dst: (i, 0)),
                      pl.BlockSpec((TM, LANES), lambda i, dst: (i, 0)),
                      pl.BlockSpec(mods.shape, lambda i, dst: (0, 0)),
                      pl.BlockSpec((1, d), lambda i, dst: (0, 0)),
                      pl.BlockSpec(memory_space=pl.ANY)],
            out_specs=pl.BlockSpec((TM, d), lambda i, dst: (i, 0)),
            scratch_shapes=[pltpu.VMEM((TM * SUBLANES, LANES), F32),
                            pltpu.VMEM((TM * SUBLANES, LANES), F32), pltpu.SemaphoreType.DMA]),
        compiler_params=_cparams("arbitrary"),
        name="moe_combine",
    )(dest, s, wts, mods, final_g.reshape(1, d), ys)


def _moe_layer(s, mods, norm_g, rg_w, rg_b, re_w, re_b, w_gate, w_up, w_down, layer, ctx_tiles,
               final_g, final_norm):
    n, d = s.shape
    h, e1, e2, r1, r2, wts, cnt = _router(s, mods, norm_g, rg_w, rg_b, re_w, re_b, ctx_tiles)
    counts = cnt[MOE_GROUPS:MOE_GROUPS + MOE_EXPERTS, 0].astype(jnp.int32)
    nb = (2 * n + MOE_EXPERTS * (MOE_ROWS - 1)) // MOE_ROWS + 1
    d1, d2, blk = _finalize(counts, e1, e2, r1, r2, nb)
    dest = jnp.concatenate([d1, d2], axis=1).reshape(2 * n)
    n_used = blk[2, :1]
    xs = _dispatch(dest, blk[1, :MOE_EXPERTS], n_used, h, nb * MOE_ROWS)
    ys = _experts(blk[0, :nb], n_used, xs, w_gate, w_up, w_down, layer)
    return _combine(dest, s, wts, mods, final_g, ys, ctx_tiles, final_norm)


def _conv_tile(x, prev_ref, next_ref, has_prev, has_next, w_ref, b_ref):
    rows = x.shape[0]
    ridx = lax.broadcasted_iota(jnp.int32, x.shape, 0)
    pm = jnp.where(has_prev, 1.0, 0.0)
    nm = jnp.where(has_next, 1.0, 0.0)
    p2 = prev_ref[SUBLANES - 2:SUBLANES - 1, :] * pm
    p1 = prev_ref[SUBLANES - 1:SUBLANES, :] * pm
    n1 = next_ref[0:1, :] * nm
    xm1 = jnp.where(ridx == 0, p1, pltpu.roll(x, 1, axis=0))
    xm2 = jnp.where(ridx == 0, p2, jnp.where(ridx == 1, p1, pltpu.roll(x, 2, axis=0)))
    xp1 = jnp.where(ridx == rows - 1, n1, pltpu.roll(x, rows - 1, axis=0))
    return (xm2 * w_ref[0:1, :] + xm1 * w_ref[1:2, :] + x * w_ref[2:3, :]
            + xp1 * w_ref[3:4, :] + b_ref[...])


def _ml_proj_kernel(s_ref, mod_ref, g_ref, w_ref, wg_ref, gb_ref, qk_ref, v_ref, o_ref, gt_ref):
    i = pl.program_id(0)
    shift, scale, _ = _mod_rows(mod_ref, i, 1, 0)
    h = _rms(s_ref[...], g_ref[...]) * (1.0 + scale) + shift
    z = _bdot(h, w_ref[...])
    nqk = qk_ref.shape[1]
    nv = v_ref.shape[1]
    qk_ref[...] = z[:, :nqk]
    v_ref[...] = z[:, nqk:nqk + nv].astype(BF16)
    o_ref[...] = z[:, nqk + nv:]
    pre = jnp.dot(h, wg_ref[...], precision=HI, preferred_element_type=F32) + gb_ref[...]
    lane = lax.broadcasted_iota(jnp.int32, pre.shape, 1)
    is_forget = ((lane >> 2) & 1) == 1
    gt_ref[...] = jnp.where(is_forget, -_softplus(-pre), pre)


def _ml_proj(s, mods, norm_g, w_in, gate_b):
    n, d = s.shape
    nqk = 2 * ML_HEADS * ML_DK
    nv = ML_HEADS * ML_DV
    n_main = nqk + 2 * nv
    n_gate = w_in.shape[1] - n_main
    w_main = w_in[:, :n_main].astype(BF16)
    w_gate = jnp.concatenate([w_in[:, n_main:], jnp.zeros((d, LANES - n_gate), F32)], axis=1)
    gb = jnp.concatenate([gate_b.reshape(n_gate), jnp.zeros((LANES - n_gate,), F32)]).reshape(1, LANES)
    tile = lambda w: pl.BlockSpec((TM, w), lambda i: (i, 0))
    return pl.pallas_call(
        _ml_proj_kernel,
        out_shape=(jax.ShapeDtypeStruct((n, nqk), F32), jax.ShapeDtypeStruct((n, nv), BF16),
                   jax.ShapeDtypeStruct((n, nv), F32), jax.ShapeDtypeStruct((n, LANES), F32)),
        grid=(n // TM,),
        in_specs=[tile(d), _full(mods.shape), _full((1, d)), _full(w_main.shape),
                  _full((d, LANES)), _full((1, LANES))],
        out_specs=(tile(nqk), tile(nv), tile(nv), tile(LANES)),
        compiler_params=_cparams("arbitrary"),
        name="mlstm_proj",
    )(s, mods, norm_g.reshape(1, d), w_main, w_gate, gb)


def _ml_chunk_index(j, n_chunks, ctx_chunks, reverse):
    if not reverse:
        return j
    return jnp.where(j < ctx_chunks, ctx_chunks - 1 - j, n_chunks - 1 + ctx_chunks - j)


def _ml_rec_kernel(qk_ref, qkp_ref, qkn_ref, v_ref, gt_ref, gtt_ref, cw_ref, cb_ref, o_ref,
                   c_scr, n_scr, m_scr, *, reverse, n_chunks, ctx_chunks):
    j = pl.program_id(0)
    c = _ml_chunk_index(j, n_chunks, ctx_chunks, reverse)

    @pl.when(j == 0)
    def _():
        c_scr[...] = jnp.zeros_like(c_scr)
        n_scr[...] = jnp.zeros_like(n_scr)
        m_scr[...] = jnp.zeros_like(m_scr)

    has_prev = (c != 0) & (c != ctx_chunks)
    has_next = (c != ctx_chunks - 1) & (c != n_chunks - 1)
    qk = _silu(_conv_tile(qk_ref[...], qkp_ref, qkn_ref, has_prev, has_next, cw_ref, cb_ref))
    L = ML_CHUNK
    ri = lax.broadcasted_iota(jnp.int32, (L, L), 0)
    ci = lax.broadcasted_iota(jnp.int32, (L, L), 1)
    past = (ci >= ri) if reverse else (ci <= ri)
    pastf = past.astype(F32)
    gt = gt_ref[...]
    gtt = gtt_ref[...]
    b_col = jnp.dot(pastf, gt, precision=HI, preferred_element_type=F32)
    b_row = jnp.dot(gtt, pastf.T, precision=HI, preferred_element_type=F32)
    last = 0 if reverse else L - 1
    dbase = 8 if reverse else 0
    nq = ML_HEADS * ML_DK
    for hd in range(ML_HEADS):
        cl = dbase + hd
        cf = dbase + 4 + hd
        q = qk[:, hd * ML_DK:(hd + 1) * ML_DK] * (ML_DK ** -0.5)
        k = qk[:, nq + hd * ML_DK:nq + (hd + 1) * ML_DK]
        v = v_ref[:, hd * ML_DV:(hd + 1) * ML_DV]
        li_c = gt[:, cl:cl + 1]
        li_r = gtt[cl:cl + 1, :]
        b_c = b_col[:, cf:cf + 1]
        b_r = b_row[cf:cf + 1, :]
        g = b_r[:, last:last + 1]
        m0 = m_scr[hd:hd + 1, 0:1]
        c0 = c_scr[hd]
        n0 = n_scr[hd:hd + 1, :]
        a_c = g - b_c + li_c
        a_r = g - b_r + li_r
        m_loc = jnp.max(a_r, axis=-1, keepdims=True)
        inter = b_c + m0
        dlog = jnp.where(past, b_c - b_r + li_r, -jnp.inf)
        m = jnp.maximum(inter, jnp.max(dlog, axis=-1, keepdims=True))
        qb = q.astype(BF16)
        sc = lax.dot_general(qb, k.astype(BF16), (((1,), (1,)), ((), ())),
                             preferred_element_type=F32) * jnp.exp(dlog - m)
        w_inter = jnp.exp(inter - m)
        num = (jnp.dot(sc.astype(BF16), v, preferred_element_type=F32)
               + w_inter * jnp.dot(qb, c0.astype(BF16), preferred_element_type=F32))
        den = (jnp.sum(sc, axis=-1, keepdims=True)
               + w_inter * jnp.sum(q * n0, axis=-1, keepdims=True))
        o_ref[:, hd * ML_DV:(hd + 1) * ML_DV] = num / jnp.maximum(jnp.abs(den), jnp.exp(-m))
        m_new = jnp.maximum(g + m0, m_loc)
        dec = jnp.exp(g + m0 - m_new)
        scl = jnp.exp(m_loc - m_new)
        kw = k * jnp.exp(a_c - m_loc)
        c_scr[hd] = dec * c0 + scl * jnp.dot(kw.T.astype(BF16), v, preferred_element_type=F32)
        n_scr[hd:hd + 1, :] = dec * n0 + scl * jnp.sum(kw, axis=0, keepdims=True)
        m_scr[hd:hd + 1, :] = jnp.broadcast_to(m_new, (1, LANES))


def _ml_rec(qk, v, gt, gtt, conv_w, conv_b, reverse):
    n, nqk = qk.shape
    nv = v.shape[1]
    L = ML_CHUNK
    nc = n // L
    cc = TM // L
    hb = L // SUBLANES
    idx = lambda j: _ml_chunk_index(j, nc, cc, reverse)
    last8 = n // SUBLANES - 1
    return pl.pallas_call(
        functools.partial(_ml_rec_kernel, reverse=reverse, n_chunks=nc, ctx_chunks=cc),
        out_shape=jax.ShapeDtypeStruct((n, nv), F32),
        grid=(nc,),
        in_specs=[pl.BlockSpec((L, nqk), lambda j: (idx(j), 0)),
                  pl.BlockSpec((SUBLANES, nqk), lambda j: (jnp.maximum(idx(j) * hb - 1, 0), 0)),
                  pl.BlockSpec((SUBLANES, nqk), lambda j: (jnp.minimum((idx(j) + 1) * hb, last8), 0)),
                  pl.BlockSpec((L, nv), lambda j: (idx(j), 0)),
                  pl.BlockSpec((L, LANES), lambda j: (idx(j), 0)),
                  pl.BlockSpec((2 * SUBLANES, L), lambda j: (0, idx(j))),
                  _full((4, nqk)), _full((1, nqk))],
        out_specs=pl.BlockSpec((L, nv), lambda j: (idx(j), 0)),
        scratch_shapes=[pltpu.VMEM((ML_HEADS, ML_DK, ML_DV), F32),
                        pltpu.VMEM((SUBLANES, ML_DK), F32),
                        pltpu.VMEM((SUBLANES, LANES), F32)],
        compiler_params=_cparams("arbitrary"),
        name="mlstm_rev" if reverse else "mlstm_fwd",
    )(qk, qk, qk, v, gt, gtt, conv_w, conv_b.reshape(1, nqk))


def _ml_out_kernel(hf_ref, hr_ref, o_ref, s_ref, mod_ref, ng_ref, w_ref, out_ref, p_scr):
    i = pl.program_id(0)
    gate = _mod_rows(mod_ref, i, 1, 0)[2]
    hs = hf_ref[...] + hr_ref[...]
    sig = _sigmoid(o_ref[...])
    ng = ng_ref[...]
    for hd in range(ML_HEADS):
        cs = slice(hd * ML_DV, (hd + 1) * ML_DV)
        seg = hs[:, cs]
        hn = seg * lax.rsqrt(jnp.mean(seg * seg, axis=-1, keepdims=True) + EPS) * ng[:, cs]
        p_scr[:, cs] = (hn * sig[:, cs]).astype(BF16)
    y = jnp.dot(p_scr[...], w_ref[...], preferred_element_type=F32)
    out_ref[...] = s_ref[...] + gate * y


def _ml_out(hf, hr, o, s, mods, norm_g, w_out):
    n, d = s.shape
    nv = hf.shape[1]
    tile = lambda w: pl.BlockSpec((TM, w), lambda i: (i, 0))
    return pl.pallas_call(
        _ml_out_kernel,
        out_shape=jax.ShapeDtypeStruct((n, d), F32),
        grid=(n // TM,),
        in_specs=[tile(nv), tile(nv), tile(nv), tile(d), _full(mods.shape), _full((1, nv)),
                  _full(w_out.shape)],
        out_specs=tile(d),
        scratch_shapes=[pltpu.VMEM((TM, nv), BF16)],
        compiler_params=_cparams("arbitrary"),
        name="mlstm_out",
    )(hf, hr, o, s, mods, norm_g.reshape(1, nv), w_out.astype(BF16))


def _mlstm_layer(s, mods, norm_g, w_in, conv_w, conv_b, gate_b, ml_norm_g, w_out):
    qk, v, o, gt = _ml_proj(s, mods, norm_g, w_in, gate_b)
    gtt = gt[:, :2 * SUBLANES].T
    hf = _ml_rec(qk, v, gt, gtt, conv_w, conv_b, False)
    hr = _ml_rec(qk, v, gt, gtt, conv_w, conv_b, True)
    return _ml_out(hf, hr, o, s, mods, ml_norm_g, w_out)


def _lru_proj_kernel(s_ref, mod_ref, g_ref, w_ref, gl_ref, xr_ref):
    i = pl.program_id(0)
    shift, scale, _ = _mod_rows(mod_ref, i, 1, 0)
    h = _rms(s_ref[...], g_ref[...]) * (1.0 + scale) + shift
    z = _bdot(h, w_ref[...])
    w = gl_ref.shape[1]
    gl_ref[...] = _gelu(z[:, :w])
    xr_ref[...] = z[:, w:]


def _lru_proj(s, mods, norm_g, w_in):
    n, d = s.shape
    w = w_in.shape[1] // 2
    tile = lambda c: pl.BlockSpec((TM, c), lambda i: (i, 0))
    return pl.pallas_call(
        _lru_proj_kernel,
        out_shape=(jax.ShapeDtypeStruct((n, w), F32), jax.ShapeDtypeStruct((n, w), F32)),
        grid=(n // TM,),
        in_specs=[tile(d), _full(mods.shape), _full((1, d)), _full(w_in.shape)],
        out_specs=(tile(w), tile(w)),
        compiler_params=_cparams("arbitrary"),
        name="rglru_proj",
    )(s, mods, norm_g.reshape(1, d), w_in.astype(BF16))


def _lru_tile_index(j, n_tiles, reverse):
    if not reverse:
        return j
    return jnp.where(j == 0, 0, n_tiles - j)


def _lru_scan_kernel(x_ref, xp_ref, xn_ref, cw_ref, cb_ref, wg_ref, ba_ref, bx_ref, lam_ref,
                     o_ref, a_scr, u_scr, carry, *, reverse, n_tiles):
    j = pl.program_id(0)
    t = _lru_tile_index(j, n_tiles, reverse)

    @pl.when(j == 0)
    def _():
        carry[...] = jnp.zeros_like(carry)

    has_prev = t > 1
    has_next = (t != 0) & (t != n_tiles - 1)
    xr = _conv_tile(x_ref[...], xp_ref, xn_ref, has_prev, has_next, cw_ref, cb_ref)
    sp = _softplus(-lam_ref[...])
    B = LRU_BLOCK
    for hd in range(LRU_HEADS):
        cs = slice(hd * B, (hd + 1) * B)
        xh = xr[:, cs]
        y = jnp.dot(xh.astype(BF16), wg_ref[hd], preferred_element_type=F32)
        r = _sigmoid(y[:, :B] + ba_ref[:, cs])
        ig = _sigmoid(y[:, B:] + bx_ref[:, cs])
        log_a = -LRU_C * r * sp[:, cs]
        a_scr[:, cs] = jnp.exp(log_a)
        u_scr[:, cs] = jnp.sqrt(1.0 - jnp.exp(2.0 * log_a)) * (ig * xh)

    S = SUBLANES
    w = a_scr.shape[1]
    sidx = lax.broadcasted_iota(jnp.int32, (S, w), 0)

    def group(gi, c):
        g = (TM // S - 1 - gi) if reverse else gi
        r0 = pl.multiple_of(g * S, S)
        a = a_scr[pl.ds(r0, S), :]
        u = u_scr[pl.ds(r0, S), :]
        for sft in (1, 2, 4):
            if reverse:
                ok = sidx < S - sft
                a_e = pltpu.roll(a, S - sft, axis=0)
                u_e = pltpu.roll(u, S - sft, axis=0)
            else:
                ok = sidx >= sft
                a_e = pltpu.roll(a, sft, axis=0)
                u_e = pltpu.roll(u, sft, axis=0)
            u = jnp.where(ok, a * u_e + u, u)
            a = jnp.where(ok, a * a_e, a)
        hcur = a * carry[...] + u
        o_ref[pl.ds(r0, S), :] = hcur
        edge = 0 if reverse else S - 1
        carry[...] = jnp.broadcast_to(hcur[edge:edge + 1, :], (S, w))
        return c

    lax.fori_loop(0, TM // S, group, 0)


def _lru_scan(xraw, conv_w, conv_b, w_a, b_a, w_x, b_x, lam, reverse):
    n, w = xraw.shape
    nt = n // TM
    hb = TM // SUBLANES
    idx = lambda j: _lru_tile_index(j, nt, reverse)
    last8 = n // SUBLANES - 1
    wg = jnp.concatenate([w_a, w_x], axis=-1).astype(BF16)
    return pl.pallas_call(
        functools.partial(_lru_scan_kernel, reverse=reverse, n_tiles=nt),
        out_shape=jax.ShapeDtypeStruct((n, w), F32),
        grid=(nt,),
        in_specs=[pl.BlockSpec((TM, w), lambda j: (idx(j), 0)),
                  pl.BlockSpec((SUBLANES, w), lambda j: (jnp.maximum(idx(j) * hb - 1, 0), 0)),
                  pl.BlockSpec((SUBLANES, w), lambda j: (jnp.minimum((idx(j) + 1) * hb, last8), 0)),
                  _full((4, w)), _full((1, w)), _full(wg.shape), _full((1, w)), _full((1, w)),
                  _full((1, w))],
        out_specs=pl.BlockSpec((TM, w), lambda j: (idx(j), 0)),
        scratch_shapes=[pltpu.VMEM((TM, w), F32), pltpu.VMEM((TM, w), F32),
                        pltpu.VMEM((SUBLANES, w), F32)],
        compiler_params=_cparams("arbitrary"),
        name="rglru_rev" if reverse else "rglru_fwd",
    )(xraw, xraw, xraw, conv_w, conv_b.reshape(1, w), wg, b_a.reshape(1, w), b_x.reshape(1, w),
      lam.reshape(1, w))


def _lru_out_kernel(gl_ref, hf_ref, hr_ref, s_ref, mod_ref, w_ref, out_ref):
    gate = _mod_rows(mod_ref, 1, 0, 0)[2]
    p = gl_ref[...] * (hf_ref[...] + hr_ref[...])
    out_ref[...] = s_ref[...] + gate * _bdot(p, w_ref[...])


def _lru_out(gl, hf, hr, s, mods, w_out):
    n, d = s.shape
    w = gl.shape[1]
    lat = lambda c: pl.BlockSpec((TM, c), lambda i: (i + 1, 0))
    return pl.pallas_call(
        _lru_out_kernel,
        out_shape=jax.ShapeDtypeStruct((n - TM, d), F32),
        grid=(n // TM - 1,),
        in_specs=[lat(w), lat(w), lat(w), lat(d), _full(mods.shape), _full(w_out.shape)],
        out_specs=pl.BlockSpec((TM, d), lambda i: (i, 0)),
        compiler_params=_cparams("arbitrary"),
        name="rglru_out",
    )(gl, hf, hr, s, mods, w_out.astype(BF16))


def _rglru_layer(s, mods, norm_g, w_in, conv_w, conv_b, w_a, b_a, w_x, b_x, lam, w_out):
    gl, xraw = _lru_proj(s, mods, norm_g, w_in)
    hf = _lru_scan(xraw, conv_w, conv_b, w_a[0], b_a[0], w_x[0], b_x[0], lam[0], False)
    hr = _lru_scan(xraw, conv_w, conv_b, w_a[1], b_a[1], w_x[1], b_x[1], lam[1], True)
    return _lru_out(gl, hf, hr, s, mods, w_out)


def _fn_proj_kernel(s_ref, mod_ref, g_ref, wt_ref, cs_ref, yr_ref, yi_ref, ar_scr, ai_scr):
    shift, scale, _ = _mod_rows(mod_ref, 1, 0, 0)
    nm = wt_ref.shape[0]
    gw = nm // FN_GROUPS
    per = TM // FFT_N2
    nj = FN_TB // FFT_N2
    csb = cs_ref[...].astype(BF16)

    def sub(tc, c):
        r0 = pl.multiple_of(tc * TM, TM)
        h = _rms(s_ref[pl.ds(r0, TM), :], g_ref[...]) * (1.0 + scale) + shift
        zt = lax.dot_general(wt_ref[...], h.astype(BF16), (((1,), (1,)), ((), ())),
                             preferred_element_type=F32).astype(BF16)
        for g in range(FN_GROUPS):
            y = jnp.dot(csb, zt[g * gw:(g + 1) * gw, :], preferred_element_type=F32)
            for q in range(per):
                row0 = pl.multiple_of((tc * per + q) * nm + g * gw, gw)
                ar_scr[pl.ds(row0, gw), :] = y[:gw, q * FFT_N2:(q + 1) * FFT_N2]
                ai_scr[pl.ds(row0, gw), :] = y[gw:, q * FFT_N2:(q + 1) * FFT_N2]
        return c

    lax.fori_loop(0, FN_TB // TM, sub, 0)

    def relayout(m, c):
        yr_ref[m] = ar_scr[pl.ds(m, nj, stride=nm), :]
        yi_ref[m] = ai_scr[pl.ds(m, nj, stride=nm), :]
        return c

    lax.fori_loop(0, nm, relayout, 0, unroll=8)


def _dft_cos_sin(n, scale):
    k = np.arange(n, dtype=np.int64)
    ang = 2.0 * np.pi * ((k[:, None] * k[None, :]) % n).astype(np.float64) / n
    return np.cos(ang) * scale, np.sin(ang) * scale


def _fn_proj(s, mods, norm_g, w_in):
    t, d = s.shape
    nm = w_in.shape[1]
    gw = nm // FN_GROUPS
    n1 = t // FFT_N2
    nj = FN_TB // FFT_N2
    c, sn = _dft_cos_sin(gw, gw ** -0.5)
    cs = jnp.asarray(np.concatenate([c, -sn], axis=0), F32)
    yspec = pl.BlockSpec((nm, nj, FFT_N2), lambda i: (0, i, 0))
    yshape = jax.ShapeDtypeStruct((nm, n1, FFT_N2), F32)
    return pl.pallas_call(
        _fn_proj_kernel,
        out_shape=(yshape, yshape),
        grid=(t // FN_TB,),
        in_specs=[pl.BlockSpec((FN_TB, d), lambda i: (i, 0)), _full(mods.shape), _full((1, d)),
                  _full((nm, d)), _full(cs.shape)],
        out_specs=(yspec, yspec),
        scratch_shapes=[pltpu.VMEM((nj * nm, FFT_N2), F32), pltpu.VMEM((nj * nm, FFT_N2), F32)],
        compiler_params=_cparams("arbitrary"),
        name="fourier_proj",
    )(s, mods, norm_g.reshape(1, d), w_in.T.astype(BF16), cs)


def _fn_fft_kernel(yr_ref, yi_ref, m_ref, tc_ref, ts_ref, d_ref, o_ref):
    n1 = yr_ref.shape[1]
    n2 = FFT_N2
    xr = jnp.concatenate([yr_ref[m].astype(BF16) for m in range(FN_CB)], axis=1)
    xi = jnp.concatenate([yi_ref[m].astype(BF16) for m in range(FN_CB)], axis=1)
    a = jnp.dot(m_ref[...].astype(BF16), jnp.concatenate([xr, xi], axis=0),
                preferred_element_type=F32)
    ar = a[:n1]
    ai = a[n1:]
    tc = jnp.concatenate([tc_ref[...]] * FN_CB, axis=1)
    ts = jnp.concatenate([ts_ref[...]] * FN_CB, axis=1)
    br = ar * tc + ai * ts
    bi = ai * tc - ar * ts
    bst = jnp.concatenate(
        [jnp.concatenate([br[:, m * n2:(m + 1) * n2], bi[:, m * n2:(m + 1) * n2]], axis=1)
         for m in range(FN_CB)], axis=0).astype(BF16)
    res = lax.dot_general(d_ref[...].astype(BF16), bst, (((1,), (1,)), ((), ())),
                          preferred_element_type=F32)
    for m in range(FN_CB):
        o_ref[m] = res[:, m * n1:(m + 1) * n1]


def _fn_fft(yr, yi):
    nm, n1, n2 = yr.shape
    t = n1 * n2
    c, sn = _dft_cos_sin(n1, n1 ** -0.5)
    m = jnp.asarray(np.block([[c, sn], [-sn, c]]), F32)
    k1 = np.arange(n1, dtype=np.int64)[:, None]
    t2 = np.arange(n2, dtype=np.int64)[None, :]
    ang = 2.0 * np.pi * ((k1 * t2) % t).astype(np.float64) / t
    tc = jnp.asarray(np.cos(ang), F32)
    ts = jnp.asarray(np.sin(ang), F32)
    c2, s2 = _dft_cos_sin(n2, n2 ** -0.5)
    dm = jnp.asarray(np.concatenate([c2, s2], axis=1), F32)
    yspec = pl.BlockSpec((FN_CB, n1, n2), lambda i: (i, 0, 0))
    return pl.pallas_call(
        _fn_fft_kernel,
        out_shape=jax.ShapeDtypeStruct((nm, n2, n1), F32),
        grid=(nm // FN_CB,),
        in_specs=[yspec, yspec, _full(m.shape), _full(tc.shape), _full(ts.shape), _full(dm.shape)],
        out_specs=pl.BlockSpec((FN_CB, n2, n1), lambda i: (i, 0, 0)),
        compiler_params=_cparams("arbitrary"),
        name="fourier_fft",
    )(yr, yi, m, tc, ts, dm)


def _fn_out_kernel(ft_ref, w_ref, s_ref, mod_ref, o_ref, a_scr):
    gate = _mod_rows(mod_ref, 1, 0, 0)[2]
    nm, nj, n1 = ft_ref.shape

    def relayout(m, c):
        a_scr[pl.ds(m, nj, stride=nm), :] = ft_ref[m]
        return c

    lax.fori_loop(0, nm, relayout, 0, unroll=8)
    for j in range(nj):
        slab = a_scr[j * nm:(j + 1) * nm, :].astype(BF16)
        y = lax.dot_general(slab, w_ref[...], (((0,), (0,)), ((), ())), preferred_element_type=F32)
        rows = slice(j * n1, (j + 1) * n1)
        o_ref[rows, :] = s_ref[rows, :] + gate * y


def _fn_out(ft, s, mods, w_out):
    t, d = s.shape
    nm, n2, n1 = ft.shape
    nj = FN_TB // n1
    tok = pl.BlockSpec((FN_TB, d), lambda i: (i, 0))
    return pl.pallas_call(
        _fn_out_kernel,
        out_shape=jax.ShapeDtypeStruct((t, d), F32),
        grid=(t // FN_TB,),
        in_specs=[pl.BlockSpec((nm, nj, n1), lambda i: (0, i, 0)), _full(w_out.shape), tok,
                  _full(mods.shape)],
        out_specs=tok,
        scratch_shapes=[pltpu.VMEM((nj * nm, n1), F32)],
        compiler_params=_cparams("arbitrary"),
        name="fourier_out",
    )(ft, w_out.astype(BF16), s, mods)


def _fourier_layer(s, mods, norm_g, w_in, w_out):
    yr, yi = _fn_proj(s, mods, norm_g, w_in)
    return _fn_out(_fn_fft(yr, yi), s, mods, w_out)


def kernel(x, c, ctx, c_ctx, ada_w, ada_b, norm_mix_g, norm_ffn_g, final_norm_g, router_group_w, router_group_b, router_expert_w, router_expert_b, expert_w_gate, expert_w_up, expert_w_down, cm_w_in, cm_v_norm_g, cm_w_s, cm_b_s, cm_w_out, ml_w_in, ml_conv_w, ml_conv_b, ml_gate_b, ml_norm_g, ml_w_out, lru_w_in, lru_conv_w, lru_conv_b, lru_w_a, lru_b_a, lru_w_x, lru_b_x, lru_lambda, lru_w_out, fn_w_in, fn_w_out):
    bsz, seq, d = x.shape
    assert bsz == 1 and ada_w.shape[0] == 4 and ctx.shape[1] == TM
    c_rows = jnp.concatenate([c_ctx[None, :], c, jnp.zeros((SUBLANES - 2, d), F32)], axis=0)
    mods = _ada_table(c_rows, ada_w, ada_b)
    s = _prep_stream(x[0], ctx[0])

    def moe(s, i, ctx_tiles, final_norm=False):
        return _moe_layer(s, mods[i], norm_ffn_g[i], router_group_w[i], router_group_b[i],
                          router_expert_w[i], router_expert_b[i], expert_w_gate, expert_w_up,
                          expert_w_down, i, ctx_tiles, final_norm_g, final_norm)

    s = _chunk_mlp_layer(s, mods[0], norm_mix_g[0], cm_w_in[0], cm_v_norm_g[0], cm_w_s[0],
                         cm_b_s[0], cm_w_out[0])
    s = moe(s, 0, 1)
    s = _mlstm_layer(s, mods[1], norm_mix_g[1], ml_w_in[0], ml_conv_w[0], ml_conv_b[0],
                     ml_gate_b[0], ml_norm_g[0], ml_w_out[0])
    s = moe(s, 1, 1)
    s = _rglru_layer(s, mods[2], norm_mix_g[2], lru_w_in[0], lru_conv_w[0], lru_conv_b[0],
                     lru_w_a[0], lru_b_a[0], lru_w_x[0], lru_b_x[0], lru_lambda[0], lru_w_out[0])
    s = moe(s, 2, 0)
    s = _fourier_layer(s, mods[3], norm_mix_g[3], fn_w_in[0], fn_w_out[0])
    s = moe(s, 3, 0, final_norm=True)
    return s[None]
```

```python
import functools
import math

import jax
import jax.numpy as jnp
import numpy as np
from jax import lax
from jax.experimental import pallas as pl
from jax.experimental.pallas import tpu as pltpu

F32 = jnp.float32
BF16 = jnp.bfloat16

EPS = 1e-6
POS_BASE = 10000.0
GRID_W = 64
N_MOD = 6
TM = 256
LANES = 128
SUBLANES = 8
VMEM_LIMIT = 56 * 1024 * 1024

CM_CHUNK = 128
CM_GROUPS = 4
ML_HEADS = 4
ML_DK = 128
ML_DV = 256
ML_CHUNK = 128
LRU_HEADS = 10
LRU_BLOCK = 128
LRU_C = 8.0
FN_GROUPS = 4
FFT_N2 = 128
MOE_GROUPS = 4
MOE_EPG = 8
MOE_EXPERTS = MOE_GROUPS * MOE_EPG
MOE_ROWS_LOG2 = 8
MOE_ROWS = 1 << MOE_ROWS_LOG2
ROUTE_ROWS = 40
FN_TB = 1024
FN_CB = 8
CONV_LEFT = 2

HI = lax.Precision.HIGHEST


def _cparams(*sem):
    return pltpu.CompilerParams(dimension_semantics=sem, vmem_limit_bytes=VMEM_LIMIT)


def _full(shape):
    nd = len(shape)
    return pl.BlockSpec(shape, lambda *_: (0,) * nd)


def _rms(x, g):
    return x * lax.rsqrt(jnp.mean(x * x, axis=-1, keepdims=True) + EPS) * g


def _gelu(x):
    c = math.sqrt(2.0 / math.pi)
    return 0.5 * x * (1.0 + jnp.tanh(c * (x + 0.044715 * (x * x * x))))


def _sigmoid(x):
    return 1.0 / (1.0 + jnp.exp(-x))


def _silu(x):
    return x * _sigmoid(x)


def _softplus(x):
    return jnp.maximum(x, 0.0) + jnp.log(1.0 + jnp.exp(-jnp.abs(x)))


def _mod_rows(mod_ref, tile, ctx_tiles, first):
    row = jnp.where(tile < ctx_tiles, 0, 1)
    m = mod_ref[pl.ds(row, 1), :]
    d = m.shape[1] // N_MOD
    return tuple(m[:, (first + j) * d:(first + j + 1) * d] for j in range(3))


def _bdot(a, b):
    return jnp.dot(a.astype(BF16), b.astype(BF16), preferred_element_type=F32)


def _ada_kernel(c_ref, w_ref, b_ref, o_ref):
    c = c_ref[...]
    o_ref[...] = jnp.dot(_silu(c), w_ref[...], precision=HI,
                         preferred_element_type=F32) + b_ref[...]


def _ada_table(c_rows, ada_w, ada_b):
    depth, d, n = ada_w.shape
    tn = 1024
    return pl.pallas_call(
        _ada_kernel,
        out_shape=jax.ShapeDtypeStruct((depth, SUBLANES, n), F32),
        grid=(depth, n // tn),
        in_specs=[_full((SUBLANES, d)),
                  pl.BlockSpec((None, d, tn), lambda i, j: (i, 0, j)),
                  pl.BlockSpec((None, 1, tn), lambda i, j: (i, 0, j))],
        out_specs=pl.BlockSpec((None, SUBLANES, tn), lambda i, j: (i, 0, j)),
        compiler_params=_cparams("arbitrary", "arbitrary"),
        name="ada_table",
    )(c_rows, ada_w, ada_b.reshape(depth, 1, n))


def _prep_kernel(x_ref, ctx_ref, rt_ref, ct_ref, o_ref):
    i = pl.program_id(0)

    @pl.when(i == 0)
    def _():
        o_ref[...] = ctx_ref[...]

    @pl.when(i > 0)
    def _():
        rows_per_tile = TM // GRID_W
        q2 = rt_ref.shape[1]
        r0 = (i - 1) * rows_per_tile
        rt = jnp.concatenate(
            [jnp.broadcast_to(rt_ref[pl.ds(r0 + j, 1), :], (GRID_W, q2))
             for j in range(rows_per_tile)], axis=0)
        ct = jnp.concatenate([ct_ref[...]] * rows_per_tile, axis=0)
        o_ref[...] = x_ref[...] + jnp.concatenate([rt, ct], axis=1)


def _prep_stream(x2, ctx2):
    seq, d = x2.shape
    n_ctx = ctx2.shape[0]
    assert n_ctx == TM and seq % TM == 0 and TM % GRID_W == 0
    q = d // 4
    freq = jnp.exp(-math.log(POS_BASE) * jnp.arange(q, dtype=F32) / q)
    ar = jnp.arange(seq // GRID_W, dtype=F32)[:, None] * freq
    ac = jnp.arange(GRID_W, dtype=F32)[:, None] * freq
    rt = jnp.concatenate([jnp.sin(ar), jnp.cos(ar)], axis=-1)
    ct = jnp.concatenate([jnp.sin(ac), jnp.cos(ac)], axis=-1)
    nt = 1 + seq // TM
    return pl.pallas_call(
        _prep_kernel,
        out_shape=jax.ShapeDtypeStruct((n_ctx + seq, d), F32),
        grid=(nt,),
        in_specs=[pl.BlockSpec((TM, d), lambda i: (jnp.maximum(i - 1, 0), 0)),
                  _full((TM, d)), _full(rt.shape), _full(ct.shape)],
        out_specs=pl.BlockSpec((TM, d), lambda i: (i, 0)),
        compiler_params=_cparams("arbitrary"),
        name="prep_stream",
    )(x2, ctx2, rt, ct)


def _cm_kernel(s_ref, mod_ref, g_ref, win_ref, vg_ref, ws_ref, bs_ref, wout_ref, o_ref, p_scr):
    i = pl.program_id(0)
    shift, scale, gate = _mod_rows(mod_ref, i, 1, 0)
    x = s_ref[...]
    h = _rms(x, g_ref[...]) * (1.0 + scale) + shift
    z = _gelu(_bdot(h, win_ref[...]))
    w = z.shape[1] // 2
    u = z[:, :w]
    v = _rms(z[:, w:], vg_ref[...]).astype(BF16)
    gw = w // CM_GROUPS
    for c in range(TM // CM_CHUNK):
        r = slice(c * CM_CHUNK, (c + 1) * CM_CHUNK)
        for g in range(CM_GROUPS):
            cs = slice(g * gw, (g + 1) * gw)
            s = jnp.dot(ws_ref[g], v[r, cs], preferred_element_type=F32) + bs_ref[:, g:g + 1]
            p_scr[r, cs] = (u[r, cs] * s).astype(BF16)
    y = jnp.dot(p_scr[...], wout_ref[...], preferred_element_type=F32)
    o_ref[...] = x + gate * y


def _chunk_mlp_layer(s, mods, norm_g, w_in, v_g, w_s, b_s, w_out):
    n, d = s.shape
    w = w_out.shape[0]
    return pl.pallas_call(
        _cm_kernel,
        out_shape=jax.ShapeDtypeStruct((n, d), F32),
        grid=(n // TM,),
        in_specs=[pl.BlockSpec((TM, d), lambda i: (i, 0)),
                  _full(mods.shape), _full((1, d)), _full(w_in.shape), _full((1, w)),
                  _full(w_s.shape), _full((CM_CHUNK, CM_GROUPS)), _full(w_out.shape)],
        out_specs=pl.BlockSpec((TM, d), lambda i: (i, 0)),
        scratch_shapes=[pltpu.VMEM((TM, w), BF16)],
        compiler_params=_cparams("arbitrary"),
        name="chunk_mlp",
    )(s, mods, norm_g.reshape(1, d), w_in.astype(BF16), v_g.reshape(1, w),
      w_s.astype(BF16), b_s.T, w_out.astype(BF16))


def _store_token_tiles(ref, x):
    rows, d = x.shape
    for j in range(d // LANES):
        ref[pl.ds(j, rows, stride=d // LANES), :] = x[:, j * LANES:(j + 1) * LANES]


def _load_token_tiles(ref):
    chunks = SUBLANES
    rows = ref.shape[0] // chunks
    return jnp.concatenate([ref[pl.ds(j, rows, stride=chunks), :] for j in range(chunks)], axis=1)


def _router_kernel(s_ref, mod_ref, g_ref, rwt_ref, rbt_ref, h_ref, e1_ref, e2_ref, r1_ref, r2_ref,
                   wt_ref, cnt_ref, carry, *, ctx_tiles):
    i = pl.program_id(0)

    @pl.when(i == 0)
    def _():
        carry[...] = jnp.zeros_like(carry)

    shift, scale, _ = _mod_rows(mod_ref, i, ctx_tiles, 3)
    h = _rms(s_ref[...], g_ref[...]) * (1.0 + scale) + shift
    _store_token_tiles(h_ref, h)
    logits = lax.dot_general(rwt_ref[...], h, (((1,), (1,)), ((), ())), precision=HI,
                             preferred_element_type=F32) + rbt_ref[...]
    row = lax.broadcasted_iota(jnp.int32, logits.shape, 0)
    neg = jnp.float32(-jnp.inf)
    big = jnp.int32(1 << 20)
    is_g = row < MOE_GROUPS
    gl = jnp.where(is_g, logits, neg)
    gmax = jnp.max(gl, axis=0, keepdims=True)
    grp = jnp.min(jnp.where(is_g & (gl == gmax), row, big), axis=0, keepdims=True)
    p_grp = 1.0 / jnp.sum(jnp.exp(gl - gmax), axis=0, keepdims=True)
    e_row = row - MOE_GROUPS
    in_grp = (e_row >= 0) & (e_row < MOE_EXPERTS) & ((e_row >> 3) == grp)
    l1 = jnp.where(in_grp, logits, neg)
    v1 = jnp.max(l1, axis=0, keepdims=True)
    i1 = jnp.min(jnp.where(in_grp & (l1 == v1), row, big), axis=0, keepdims=True)
    rest = in_grp & (row != i1)
    l2 = jnp.where(rest, logits, neg)
    v2 = jnp.max(l2, axis=0, keepdims=True)
    i2 = jnp.min(jnp.where(rest & (l2 == v2), row, big), axis=0, keepdims=True)
    e21 = jnp.exp(v2 - v1)
    w1 = p_grp / (1.0 + e21)
    w2 = p_grp * e21 / (1.0 + e21)
    oh1 = (row == i1).astype(F32)
    oh2 = (row == i2).astype(F32)
    oh = oh1 + oh2
    ri = lax.broadcasted_iota(jnp.int32, (TM, TM), 0)
    ci = lax.broadcasted_iota(jnp.int32, (TM, TM), 1)
    earlier = (ri < ci).astype(BF16)
    before = jnp.dot(oh.astype(BF16), earlier, preferred_element_type=F32) + carry[:, 0:1]
    r1_ref[...] = jnp.sum(oh1 * before, axis=0, keepdims=True).astype(jnp.int32)
    r2_ref[...] = jnp.sum(oh2 * before, axis=0, keepdims=True).astype(jnp.int32)
    e1_ref[...] = i1 - MOE_GROUPS
    e2_ref[...] = i2 - MOE_GROUPS
    carry[...] = carry[...] + jnp.sum(oh, axis=1, keepdims=True)
    cnt_ref[...] = carry[...]
    wt_ref[...] = jnp.concatenate([w1, w2, jnp.zeros((LANES - 2, TM), F32)], axis=0).T


def _router(s, mods, norm_g, rg_w, rg_b, re_w, re_b, ctx_tiles):
    n, d = s.shape
    nt = n // TM
    pad = ROUTE_ROWS - MOE_GROUPS - MOE_EXPERTS
    rwt = jnp.concatenate([rg_w, re_w, jnp.zeros((d, pad), F32)], axis=1).T
    rbt = jnp.broadcast_to(jnp.concatenate([rg_b, re_b, jnp.zeros((pad,), F32)])[:, None],
                           (ROUTE_ROWS, TM))
    assert d == SUBLANES * LANES
    tile = pl.BlockSpec((TM, d), lambda i: (i, 0))
    irow = pl.BlockSpec((None, 1, TM), lambda i: (i, 0, 0))
    ishape = jax.ShapeDtypeStruct((nt, 1, TM), jnp.int32)
    return pl.pallas_call(
        functools.partial(_router_kernel, ctx_tiles=ctx_tiles),
        out_shape=(jax.ShapeDtypeStruct((n * SUBLANES, LANES), F32), ishape, ishape, ishape, ishape,
                   jax.ShapeDtypeStruct((n, LANES), F32),
                   jax.ShapeDtypeStruct((ROUTE_ROWS, LANES), F32)),
        grid=(nt,),
        in_specs=[tile, _full(mods.shape), _full((1, d)), _full((ROUTE_ROWS, d)),
                  _full((ROUTE_ROWS, TM))],
        out_specs=(pl.BlockSpec((TM * SUBLANES, LANES), lambda i: (i, 0)), irow, irow, irow, irow,
                   pl.BlockSpec((TM, LANES), lambda i: (i, 0)), _full((ROUTE_ROWS, LANES))),
        scratch_shapes=[pltpu.VMEM((ROUTE_ROWS, LANES), F32)],
        compiler_params=_cparams("arbitrary"),
        name="moe_router",
    )(s, mods, norm_g.reshape(1, d), rwt, rbt)


def _finalize_kernel(cnt_ref, e1_ref, e2_ref, r1_ref, r2_ref, d1_ref, d2_ref, blk_ref):
    e1 = e1_ref[...]
    e2 = e2_ref[...]
    r1 = r1_ref[...]
    r2 = r2_ref[...]
    d1 = jnp.zeros_like(e1)
    d2 = jnp.zeros_like(e2)
    lane = lax.broadcasted_iota(jnp.int32, blk_ref.shape, 1)
    brow = lane * MOE_ROWS
    sub = lax.broadcasted_iota(jnp.int32, blk_ref.shape, 0)
    be = jnp.zeros(blk_ref.shape, jnp.int32)
    pend = jnp.zeros(blk_ref.shape, jnp.int32)
    ps = jnp.int32(0)
    for e in range(MOE_EXPERTS):
        c = cnt_ref[e]
        pe = ps + lax.shift_left(lax.shift_right_logical(c + (MOE_ROWS - 1), MOE_ROWS_LOG2),
                                 MOE_ROWS_LOG2)
        d1 = jnp.where(e1 == e, ps + r1, d1)
        d2 = jnp.where(e2 == e, ps + r2, d2)
        be = be + (brow >= pe).astype(jnp.int32)
        pend = jnp.where(lane == e, pe, pend)
        ps = pe
    d1_ref[...] = d1
    d2_ref[...] = d2
    n_used = lax.shift_right_logical(ps, MOE_ROWS_LOG2)
    blk_ref[...] = jnp.where(sub == 0, jnp.minimum(be, MOE_EXPERTS - 1),
                             jnp.where(sub == 1, pend, n_used))


def _finalize(counts, e1, e2, r1, r2, nb):
    nbp = (nb + LANES - 1) // LANES * LANES
    whole = pl.BlockSpec(e1.shape, lambda i, c: (0, 0, 0))
    ishape = jax.ShapeDtypeStruct(e1.shape, jnp.int32)
    return pl.pallas_call(
        _finalize_kernel,
        out_shape=(ishape, ishape, jax.ShapeDtypeStruct((SUBLANES, nbp), jnp.int32)),
        grid_spec=pltpu.PrefetchScalarGridSpec(
            num_scalar_prefetch=1,
            grid=(1,),
            in_specs=[whole, whole, whole, whole],
            out_specs=(whole, whole, pl.BlockSpec((SUBLANES, nbp), lambda i, c: (0, 0)))),
        compiler_params=_cparams("arbitrary"),
        name="moe_finalize",
    )(counts, e1, e2, r1, r2)


def _token_copy(src, r, dst, d, sem):
    return pltpu.make_async_copy(src.at[pl.ds(pl.multiple_of(r * SUBLANES, SUBLANES), SUBLANES), :],
                                 dst.at[pl.ds(pl.multiple_of(d * SUBLANES, SUBLANES), SUBLANES), :],
                                 sem)


def _zero_fill_padding(pend_ref, nu_ref, xs_out, zbuf, zsem):
    blk_rows = MOE_ROWS * SUBLANES
    nb = xs_out.shape[0] // blk_rows
    zbuf[...] = jnp.zeros_like(zbuf)

    def block_copy(b):
        r0 = pl.multiple_of(b * blk_rows, blk_rows)
        return pltpu.make_async_copy(zbuf, xs_out.at[pl.ds(r0, blk_rows), :], zsem)

    def seg_last_block(e):
        pe = pend_ref[e]
        prev = pend_ref[e - 1] if e > 0 else 0
        return pe > prev, lax.shift_right_logical(pe, MOE_ROWS_LOG2) - 1

    for e in range(MOE_EXPERTS):
        nonempty, b = seg_last_block(e)

        @pl.when(nonempty)
        def _():
            block_copy(b).start()

    def tail_start(b, c):
        block_copy(b).start()
        return c

    lax.fori_loop(nu_ref[0], nb, tail_start, 0)
    for e in range(MOE_EXPERTS):
        nonempty, b = seg_last_block(e)

        @pl.when(nonempty)
        def _():
            block_copy(b).wait()

    def tail_wait(b, c):
        block_copy(b).wait()
        return c

    lax.fori_loop(nu_ref[0], nb, tail_wait, 0)


def _dispatch_kernel(dest_ref, pend_ref, nu_ref, h_hbm, xs_out, sem, zbuf, zsem):
    i = pl.program_id(0)

    @pl.when(i == 0)
    def _():
        _zero_fill_padding(pend_ref, nu_ref, xs_out, zbuf, zsem)

    base = i * (2 * TM)
    tok0 = i * TM

    def start(r, c):
        _token_copy(h_hbm, tok0 + r, xs_out, dest_ref[base + r], sem).start(priority=0)
        _token_copy(h_hbm, tok0 + r, xs_out, dest_ref[base + TM + r], sem).start(priority=1)
        return c

    lax.fori_loop(0, TM, start, 0, unroll=8)

    def retire_one_step():
        for _ in range(2):
            pltpu.make_async_copy(h_hbm.at[pl.ds(0, TM * SUBLANES), :],
                                  xs_out.at[pl.ds(0, TM * SUBLANES), :], sem).wait()

    @pl.when(i > 0)
    def _():
        retire_one_step()

    @pl.when(i == pl.num_programs(0) - 1)
    def _():
        retire_one_step()


def _dispatch(dest, pad_end, n_used, h, n_rows):
    n = h.shape[0] // SUBLANES
    return pl.pallas_call(
        _dispatch_kernel,
        out_shape=jax.ShapeDtypeStruct((n_rows * SUBLANES, LANES), F32),
        grid_spec=pltpu.PrefetchScalarGridSpec(
            num_scalar_prefetch=3,
            grid=(n // TM,),
            in_specs=[pl.BlockSpec(memory_space=pl.ANY)],
            out_specs=pl.BlockSpec(memory_space=pl.ANY),
            scratch_shapes=[pltpu.SemaphoreType.DMA, pltpu.VMEM((MOE_ROWS * SUBLANES, LANES), F32),
                            pltpu.SemaphoreType.DMA]),
        compiler_params=_cparams("arbitrary"),
        name="moe_dispatch",
    )(dest, pad_end, n_used, h)


def _expert_kernel(be_ref, nu_ref, x_ref, wg_ref, wu_ref, wd_ref, y_ref, wg_s, wu_s, wd_s):
    b = pl.program_id(0)
    used = b < nu_ref[0]
    prev = be_ref[jnp.maximum(b - 1, 0)]
    fresh = (b == 0) | (be_ref[b] != prev)

    @pl.when(used & fresh)
    def _():
        wg_s[...] = wg_ref[...].astype(BF16)
        wu_s[...] = wu_ref[...].astype(BF16)
        wd_s[...] = wd_ref[...].astype(BF16)

    @pl.when(used)
    def _():
        x = _load_token_tiles(x_ref).astype(BF16)
        a = jnp.dot(x, wg_s[...], preferred_element_type=F32)
        u = jnp.dot(x, wu_s[...], preferred_element_type=F32)
        _store_token_tiles(y_ref, jnp.dot((_silu(a) * u).astype(BF16), wd_s[...],
                                          preferred_element_type=F32))

    @pl.when(jnp.logical_not(used))
    def _():
        y_ref[...] = jnp.zeros_like(y_ref)


def _experts(blk_expert, n_used, xs, w_gate, w_up, w_down, layer):
    d, hid = w_gate.shape[2:]
    nb = xs.shape[0] // (MOE_ROWS * SUBLANES)
    wmap = lambda b, be, nu: (layer, be[b], 0, 0)
    rows = pl.BlockSpec((MOE_ROWS * SUBLANES, LANES), lambda b, be, nu: (b, 0))
    return pl.pallas_call(
        _expert_kernel,
        out_shape=jax.ShapeDtypeStruct(xs.shape, F32),
        grid_spec=pltpu.PrefetchScalarGridSpec(
            num_scalar_prefetch=2,
            grid=(nb,),
            in_specs=[rows,
                      pl.BlockSpec((None, None, d, hid), wmap),
                      pl.BlockSpec((None, None, d, hid), wmap),
                      pl.BlockSpec((None, None, hid, d), wmap)],
            out_specs=rows,
            scratch_shapes=[pltpu.VMEM((d, hid), BF16), pltpu.VMEM((d, hid), BF16),
                            pltpu.VMEM((hid, d), BF16)]),
        compiler_params=_cparams("arbitrary"),
        name="moe_experts",
    )(blk_expert, n_used, xs, w_gate, w_up, w_down)


def _combine_kernel(dest_ref, s_ref, wt_ref, mod_ref, fg_ref, ys_ref, o_ref, ybuf, sem,
                    *, ctx_tiles, final_norm):
    i = pl.program_id(0)
    slot = i % 2

    def gather(tile, slot):
        base = tile * (2 * TM)

        def start(r, c):
            _token_copy(ys_ref, dest_ref[base + r], ybuf.at[slot, 0], r,
                        sem.at[slot]).start(priority=0)
            _token_copy(ys_ref, dest_ref[base + TM + r], ybuf.at[slot, 1], r,
                        sem.at[slot]).start(priority=1)
            return c

        lax.fori_loop(0, TM, start, 0, unroll=8)

    @pl.when(i == 0)
    def _():
        gather(0, 0)

    @pl.when(i + 1 < pl.num_programs(0))
    def _():
        gather(i + 1, 1 - slot)

    for k in range(2):
        pltpu.make_async_copy(ys_ref.at[pl.ds(0, TM * SUBLANES), :], ybuf.at[slot, k],
                              sem.at[slot]).wait()
    gate = _mod_rows(mod_ref, i, ctx_tiles, 3)[2]
    wt = wt_ref[...]
    y = (wt[:, 0:1] * _load_token_tiles(ybuf.at[slot, 0])
         + wt[:, 1:2] * _load_token_tiles(ybuf.at[slot, 1]))
    out = s_ref[...] + gate * y
    if final_norm:
        out = _rms(out, fg_ref[...])
    o_ref[...] = out


def _combine(dest, s, wts, mods, final_g, ys, ctx_tiles, final_norm):
    n, d = s.shape
    return pl.pallas_call(
        functools.partial(_combine_kernel, ctx_tiles=ctx_tiles, final_norm=final_norm),
        out_shape=jax.ShapeDtypeStruct((n, d), F32),
        grid_spec=pltpu.PrefetchScalarGridSpec(
            num_scalar_prefetch=1,
            grid=(n // TM,),
            in_specs=[pl.BlockSpec((TM, d), lambda i, dst: (i, 0)),
                      pl.BlockSpec((TM, LANES), lambda i, dst: (i, 0)),
                      pl.BlockSpec(mods.shape, lambda i, dst: (0, 0)),
                      pl.BlockSpec((1, d), lambda i, dst: (0, 0)),
                      pl.BlockSpec(memory_space=pl.ANY)],
            out_specs=pl.BlockSpec((TM, d), lambda i, dst: (i, 0)),
            scratch_shapes=[pltpu.VMEM((2, 2, TM * SUBLANES, LANES), F32),
                            pltpu.SemaphoreType.DMA((2,))]),
        compiler_params=_cparams("arbitrary"),
        name="moe_combine",
    )(dest, s, wts, mods, final_g.reshape(1, d), ys)


def _moe_layer(s, mods, norm_g, rg_w, rg_b, re_w, re_b, w_gate, w_up, w_down, layer, ctx_tiles,
               final_g, final_norm):
    n, d = s.shape
    h, e1, e2, r1, r2, wts, cnt = _router(s, mods, norm_g, rg_w, rg_b, re_w, re_b, ctx_tiles)
    counts = cnt[MOE_GROUPS:MOE_GROUPS + MOE_EXPERTS, 0].astype(jnp.int32)
    nb = (2 * n + MOE_EXPERTS * (MOE_ROWS - 1)) // MOE_ROWS + 1
    d1, d2, blk = _finalize(counts, e1, e2, r1, r2, nb)
    dest = jnp.concatenate([d1, d2], axis=1).reshape(2 * n)
    n_used = blk[2, :1]
    xs = _dispatch(dest, blk[1, :MOE_EXPERTS], n_used, h, nb * MOE_ROWS)
    ys = _experts(blk[0, :nb], n_used, xs, w_gate, w_up, w_down, layer)
    return _combine(dest, s, wts, mods, final_g, ys, ctx_tiles, final_norm)


def _conv_tile(x, prev_ref, next_ref, has_prev, has_next, w_ref, b_ref):
    rows = x.shape[0]
    ridx = lax.broadcasted_iota(jnp.int32, x.shape, 0)
    pm = jnp.where(has_prev, 1.0, 0.0)
    nm = jnp.where(has_next, 1.0, 0.0)
    p2 = prev_ref[SUBLANES - 2:SUBLANES - 1, :] * pm
    p1 = prev_ref[SUBLANES - 1:SUBLANES, :] * pm
    n1 = next_ref[0:1, :] * nm
    xm1 = jnp.where(ridx == 0, p1, pltpu.roll(x, 1, axis=0))
    xm2 = jnp.where(ridx == 0, p2, jnp.where(ridx == 1, p1, pltpu.roll(x, 2, axis=0)))
    xp1 = jnp.where(ridx == rows - 1, n1, pltpu.roll(x, rows - 1, axis=0))
    return (xm2 * w_ref[0:1, :] + xm1 * w_ref[1:2, :] + x * w_ref[2:3, :]
            + xp1 * w_ref[3:4, :] + b_ref[...])


def _ml_proj_kernel(s_ref, mod_ref, g_ref, w_ref, wg_ref, gb_ref, qk_ref, v_ref, o_ref, gt_ref):
    i = pl.program_id(0)
    shift, scale, _ = _mod_rows(mod_ref, i, 1, 0)
    h = _rms(s_ref[...], g_ref[...]) * (1.0 + scale) + shift
    z = _bdot(h, w_ref[...])
    nqk = qk_ref.shape[1]
    nv = v_ref.shape[1]
    qk_ref[...] = z[:, :nqk]
    v_ref[...] = z[:, nqk:nqk + nv].astype(BF16)
    o_ref[...] = z[:, nqk + nv:].astype(BF16)
    pre = jnp.dot(h, wg_ref[...], precision=HI, preferred_element_type=F32) + gb_ref[...]
    lane = lax.broadcasted_iota(jnp.int32, pre.shape, 1)
    is_forget = ((lane >> 2) & 1) == 1
    gt_ref[...] = jnp.where(is_forget, -_softplus(-pre), pre)


def _ml_proj(s, mods, norm_g, w_in, gate_b):
    n, d = s.shape
    nqk = 2 * ML_HEADS * ML_DK
    nv = ML_HEADS * ML_DV
    n_main = nqk + 2 * nv
    n_gate = w_in.shape[1] - n_main
    w_main = w_in[:, :n_main].astype(BF16)
    w_gate = jnp.concatenate([w_in[:, n_main:], jnp.zeros((d, LANES - n_gate), F32)], axis=1)
    gb = jnp.concatenate([gate_b.reshape(n_gate), jnp.zeros((LANES - n_gate,), F32)]).reshape(1, LANES)
    tile = lambda w: pl.BlockSpec((TM, w), lambda i: (i, 0))
    return pl.pallas_call(
        _ml_proj_kernel,
        out_shape=(jax.ShapeDtypeStruct((n, nqk), F32), jax.ShapeDtypeStruct((n, nv), BF16),
                   jax.ShapeDtypeStruct((n, nv), BF16), jax.ShapeDtypeStruct((n, LANES), F32)),
        grid=(n // TM,),
        in_specs=[tile(d), _full(mods.shape), _full((1, d)), _full(w_main.shape),
                  _full((d, LANES)), _full((1, LANES))],
        out_specs=(tile(nqk), tile(nv), tile(nv), tile(LANES)),
        compiler_params=_cparams("arbitrary"),
        name="mlstm_proj",
    )(s, mods, norm_g.reshape(1, d), w_main, w_gate, gb)


def _ml_chunk_index(j, n_chunks, ctx_chunks, reverse):
    if not reverse:
        return j
    return jnp.where(j < ctx_chunks, ctx_chunks - 1 - j, n_chunks - 1 + ctx_chunks - j)


def _ml_rec_kernel(qk_ref, qkp_ref, qkn_ref, v_ref, gt_ref, gtt_ref, cw_ref, cb_ref, o_ref,
                   c_scr, n_scr, m_scr, *, reverse, n_chunks, ctx_chunks):
    j = pl.program_id(0)
    c = _ml_chunk_index(j, n_chunks, ctx_chunks, reverse)

    @pl.when(j == 0)
    def _():
        c_scr[...] = jnp.zeros_like(c_scr)
        n_scr[...] = jnp.zeros_like(n_scr)
        m_scr[...] = jnp.zeros_like(m_scr)

    has_prev = (c != 0) & (c != ctx_chunks)
    has_next = (c != ctx_chunks - 1) & (c != n_chunks - 1)
    qk = _silu(_conv_tile(qk_ref[...], qkp_ref, qkn_ref, has_prev, has_next, cw_ref, cb_ref))
    L = ML_CHUNK
    ri = lax.broadcasted_iota(jnp.int32, (L, L), 0)
    ci = lax.broadcasted_iota(jnp.int32, (L, L), 1)
    past = (ci >= ri) if reverse else (ci <= ri)
    pastf = past.astype(F32)
    gt = gt_ref[...]
    gtt = gtt_ref[...]
    b_col = jnp.dot(pastf, gt, precision=HI, preferred_element_type=F32)
    b_row = jnp.dot(gtt, pastf.T, precision=HI, preferred_element_type=F32)
    last = 0 if reverse else L - 1
    dbase = 8 if reverse else 0
    nq = ML_HEADS * ML_DK
    for hd in range(ML_HEADS):
        cl = dbase + hd
        cf = dbase + 4 + hd
        q = qk[:, hd * ML_DK:(hd + 1) * ML_DK] * (ML_DK ** -0.5)
        k = qk[:, nq + hd * ML_DK:nq + (hd + 1) * ML_DK]
        v = v_ref[:, hd * ML_DV:(hd + 1) * ML_DV]
        li_c = gt[:, cl:cl + 1]
        li_r = gtt[cl:cl + 1, :]
        b_c = b_col[:, cf:cf + 1]
        b_r = b_row[cf:cf + 1, :]
        g = b_r[:, last:last + 1]
        m0 = m_scr[hd:hd + 1, 0:1]
        c0 = c_scr[hd]
        n0 = n_scr[hd:hd + 1, :]
        a_c = g - b_c + li_c
        a_r = g - b_r + li_r
        m_loc = jnp.max(a_r, axis=-1, keepdims=True)
        inter = b_c + m0
        dlog = jnp.where(past, b_c - b_r + li_r, -jnp.inf)
        m = jnp.maximum(inter, jnp.max(dlog, axis=-1, keepdims=True))
        qb = q.astype(BF16)
        sc = lax.dot_general(qb, k.astype(BF16), (((1,), (1,)), ((), ())),
                             preferred_element_type=F32) * jnp.exp(dlog - m)
        w_inter = jnp.exp(inter - m)
        num = (jnp.dot(sc.astype(BF16), v, preferred_element_type=F32)
               + w_inter * jnp.dot(qb, c0.astype(BF16), preferred_element_type=F32))
        den = (jnp.sum(sc, axis=-1, keepdims=True)
               + w_inter * jnp.sum(q * n0, axis=-1, keepdims=True))
        o_ref[:, hd * ML_DV:(hd + 1) * ML_DV] = (
            num / jnp.maximum(jnp.abs(den), jnp.exp(-m))).astype(BF16)
        m_new = jnp.maximum(g + m0, m_loc)
        dec = jnp.exp(g + m0 - m_new)
        scl = jnp.exp(m_loc - m_new)
        kw = k * jnp.exp(a_c - m_loc)
        c_scr[hd] = dec * c0 + scl * jnp.dot(kw.T.astype(BF16), v, preferred_element_type=F32)
        n_scr[hd:hd + 1, :] = dec * n0 + scl * jnp.sum(kw, axis=0, keepdims=True)
        m_scr[hd:hd + 1, :] = jnp.broadcast_to(m_new, (1, LANES))


def _ml_rec(qk, v, gt, gtt, conv_w, conv_b, reverse):
    n, nqk = qk.shape
    nv = v.shape[1]
    L = ML_CHUNK
    nc = n // L
    cc = TM // L
    hb = L // SUBLANES
    idx = lambda j: _ml_chunk_index(j, nc, cc, reverse)
    last8 = n // SUBLANES - 1
    return pl.pallas_call(
        functools.partial(_ml_rec_kernel, reverse=reverse, n_chunks=nc, ctx_chunks=cc),
        out_shape=jax.ShapeDtypeStruct((n, nv), BF16),
        grid=(nc,),
        in_specs=[pl.BlockSpec((L, nqk), lambda j: (idx(j), 0)),
                  pl.BlockSpec((SUBLANES, nqk), lambda j: (jnp.maximum(idx(j) * hb - 1, 0), 0)),
                  pl.BlockSpec((SUBLANES, nqk), lambda j: (jnp.minimum((idx(j) + 1) * hb, last8), 0)),
                  pl.BlockSpec((L, nv), lambda j: (idx(j), 0)),
                  pl.BlockSpec((L, LANES), lambda j: (idx(j), 0)),
                  pl.BlockSpec((2 * SUBLANES, L), lambda j: (0, idx(j))),
                  _full((4, nqk)), _full((1, nqk))],
        out_specs=pl.BlockSpec((L, nv), lambda j: (idx(j), 0)),
        scratch_shapes=[pltpu.VMEM((ML_HEADS, ML_DK, ML_DV), F32),
                        pltpu.VMEM((SUBLANES, ML_DK), F32),
                        pltpu.VMEM((SUBLANES, LANES), F32)],
        compiler_params=_cparams("arbitrary"),
        name="mlstm_rev" if reverse else "mlstm_fwd",
    )(qk, qk, qk, v, gt, gtt, conv_w, conv_b.reshape(1, nqk))


def _ml_out_kernel(hf_ref, hr_ref, o_ref, s_ref, mod_ref, ng_ref, w_ref, out_ref, p_scr):
    i = pl.program_id(0)
    gate = _mod_rows(mod_ref, i, 1, 0)[2]
    hs = hf_ref[...].astype(F32) + hr_ref[...].astype(F32)
    sig = _sigmoid(o_ref[...].astype(F32))
    ng = ng_ref[...]
    for hd in range(ML_HEADS):
        cs = slice(hd * ML_DV, (hd + 1) * ML_DV)
        seg = hs[:, cs]
        hn = seg * lax.rsqrt(jnp.mean(seg * seg, axis=-1, keepdims=True) + EPS) * ng[:, cs]
        p_scr[:, cs] = (hn * sig[:, cs]).astype(BF16)
    y = jnp.dot(p_scr[...], w_ref[...], preferred_element_type=F32)
    out_ref[...] = s_ref[...] + gate * y


def _ml_out(hf, hr, o, s, mods, norm_g, w_out):
    n, d = s.shape
    nv = hf.shape[1]
    tile = lambda w: pl.BlockSpec((TM, w), lambda i: (i, 0))
    return pl.pallas_call(
        _ml_out_kernel,
        out_shape=jax.ShapeDtypeStruct((n, d), F32),
        grid=(n // TM,),
        in_specs=[tile(nv), tile(nv), tile(nv), tile(d), _full(mods.shape), _full((1, nv)),
                  _full(w_out.shape)],
        out_specs=tile(d),
        scratch_shapes=[pltpu.VMEM((TM, nv), BF16)],
        compiler_params=_cparams("arbitrary"),
        name="mlstm_out",
    )(hf, hr, o, s, mods, norm_g.reshape(1, nv), w_out.astype(BF16))


def _mlstm_layer(s, mods, norm_g, w_in, conv_w, conv_b, gate_b, ml_norm_g, w_out):
    qk, v, o, gt = _ml_proj(s, mods, norm_g, w_in, gate_b)
    gtt = gt[:, :2 * SUBLANES].T
    hf = _ml_rec(qk, v, gt, gtt, conv_w, conv_b, False)
    hr = _ml_rec(qk, v, gt, gtt, conv_w, conv_b, True)
    return _ml_out(hf, hr, o, s, mods, ml_norm_g, w_out)


def _lru_proj_kernel(s_ref, mod_ref, g_ref, w_ref, gl_ref, xr_ref):
    i = pl.program_id(0)
    shift, scale, _ = _mod_rows(mod_ref, i, 1, 0)
    h = _rms(s_ref[...], g_ref[...]) * (1.0 + scale) + shift
    z = _bdot(h, w_ref[...])
    w = gl_ref.shape[1]
    gl_ref[...] = _gelu(z[:, :w]).astype(BF16)
    xr_ref[...] = z[:, w:]


def _lru_proj(s, mods, norm_g, w_in):
    n, d = s.shape
    w = w_in.shape[1] // 2
    tile = lambda c: pl.BlockSpec((TM, c), lambda i: (i, 0))
    return pl.pallas_call(
        _lru_proj_kernel,
        out_shape=(jax.ShapeDtypeStruct((n, w), BF16), jax.ShapeDtypeStruct((n, w), F32)),
        grid=(n // TM,),
        in_specs=[tile(d), _full(mods.shape), _full((1, d)), _full(w_in.shape)],
        out_specs=(tile(w), tile(w)),
        compiler_params=_cparams("arbitrary"),
        name="rglru_proj",
    )(s, mods, norm_g.reshape(1, d), w_in.astype(BF16))


def _lru_tile_index(j, n_tiles, reverse):
    if not reverse:
        return j
    return jnp.where(j == 0, 0, n_tiles - j)


def _lru_scan_kernel(x_ref, xp_ref, xn_ref, cw_ref, cb_ref, wg_ref, ba_ref, bx_ref, lam_ref,
                     o_ref, a_scr, u_scr, carry, *, reverse, n_tiles):
    j = pl.program_id(0)
    t = _lru_tile_index(j, n_tiles, reverse)

    @pl.when(j == 0)
    def _():
        carry[...] = jnp.zeros_like(carry)

    has_prev = t > 1
    has_next = (t != 0) & (t != n_tiles - 1)
    xr = _conv_tile(x_ref[...], xp_ref, xn_ref, has_prev, has_next, cw_ref, cb_ref)
    sp = _softplus(-lam_ref[...])
    B = LRU_BLOCK
    for hd in range(LRU_HEADS):
        cs = slice(hd * B, (hd + 1) * B)
        xh = xr[:, cs]
        y = jnp.dot(xh.astype(BF16), wg_ref[hd], preferred_element_type=F32)
        r = _sigmoid(y[:, :B] + ba_ref[:, cs])
        ig = _sigmoid(y[:, B:] + bx_ref[:, cs])
        log_a = -LRU_C * r * sp[:, cs]
        a = jnp.exp(log_a)
        a_scr[:, cs] = a
        u_scr[:, cs] = jnp.sqrt(1.0 - a * a) * (ig * xh)

    S = SUBLANES
    w = a_scr.shape[1]
    sidx = lax.broadcasted_iota(jnp.int32, (S, w), 0)

    def group(gi, c):
        g = (TM // S - 1 - gi) if reverse else gi
        r0 = pl.multiple_of(g * S, S)
        a = a_scr[pl.ds(r0, S), :]
        u = u_scr[pl.ds(r0, S), :]
        for sft in (1, 2, 4):
            if reverse:
                ok = sidx < S - sft
                a_e = pltpu.roll(a, S - sft, axis=0)
                u_e = pltpu.roll(u, S - sft, axis=0)
            else:
                ok = sidx >= sft
                a_e = pltpu.roll(a, sft, axis=0)
                u_e = pltpu.roll(u, sft, axis=0)
            u = jnp.where(ok, a * u_e + u, u)
            a = jnp.where(ok, a * a_e, a)
        hcur = a * carry[...] + u
        u_scr[pl.ds(r0, S), :] = hcur
        edge = 0 if reverse else S - 1
        carry[...] = jnp.broadcast_to(hcur[edge:edge + 1, :], (S, w))
        return c

    lax.fori_loop(0, TM // S, group, 0)
    o_ref[...] = u_scr[...].astype(BF16)


def _lru_scan(xraw, conv_w, conv_b, w_a, b_a, w_x, b_x, lam, reverse):
    n, w = xraw.shape
    nt = n // TM
    hb = TM // SUBLANES
    idx = lambda j: _lru_tile_index(j, nt, reverse)
    last8 = n // SUBLANES - 1
    wg = jnp.concatenate([w_a, w_x], axis=-1).astype(BF16)
    return pl.pallas_call(
        functools.partial(_lru_scan_kernel, reverse=reverse, n_tiles=nt),
        out_shape=jax.ShapeDtypeStruct((n, w), BF16),
        grid=(nt,),
        in_specs=[pl.BlockSpec((TM, w), lambda j: (idx(j), 0)),
                  pl.BlockSpec((SUBLANES, w), lambda j: (jnp.maximum(idx(j) * hb - 1, 0), 0)),
                  pl.BlockSpec((SUBLANES, w), lambda j: (jnp.minimum((idx(j) + 1) * hb, last8), 0)),
                  _full((4, w)), _full((1, w)), _full(wg.shape), _full((1, w)), _full((1, w)),
                  _full((1, w))],
        out_specs=pl.BlockSpec((TM, w), lambda j: (idx(j), 0)),
        scratch_shapes=[pltpu.VMEM((TM, w), F32), pltpu.VMEM((TM, w), F32),
                        pltpu.VMEM((SUBLANES, w), F32)],
        compiler_params=_cparams("arbitrary"),
        name="rglru_rev" if reverse else "rglru_fwd",
    )(xraw, xraw, xraw, conv_w, conv_b.reshape(1, w), wg, b_a.reshape(1, w), b_x.reshape(1, w),
      lam.reshape(1, w))


def _lru_out_kernel(gl_ref, hf_ref, hr_ref, s_ref, mod_ref, w_ref, out_ref):
    gate = _mod_rows(mod_ref, 1, 0, 0)[2]
    p = gl_ref[...].astype(F32) * (hf_ref[...].astype(F32) + hr_ref[...].astype(F32))
    out_ref[...] = s_ref[...] + gate * _bdot(p, w_ref[...])


def _lru_out(gl, hf, hr, s, mods, w_out):
    n, d = s.shape
    w = gl.shape[1]
    lat = lambda c: pl.BlockSpec((TM, c), lambda i: (i + 1, 0))
    return pl.pallas_call(
        _lru_out_kernel,
        out_shape=jax.ShapeDtypeStruct((n - TM, d), F32),
        grid=(n // TM - 1,),
        in_specs=[lat(w), lat(w), lat(w), lat(d), _full(mods.shape), _full(w_out.shape)],
        out_specs=pl.BlockSpec((TM, d), lambda i: (i, 0)),
        compiler_params=_cparams("arbitrary"),
        name="rglru_out",
    )(gl, hf, hr, s, mods, w_out.astype(BF16))


def _rglru_layer(s, mods, norm_g, w_in, conv_w, conv_b, w_a, b_a, w_x, b_x, lam, w_out):
    gl, xraw = _lru_proj(s, mods, norm_g, w_in)
    hf = _lru_scan(xraw, conv_w, conv_b, w_a[0], b_a[0], w_x[0], b_x[0], lam[0], False)
    hr = _lru_scan(xraw, conv_w, conv_b, w_a[1], b_a[1], w_x[1], b_x[1], lam[1], True)
    return _lru_out(gl, hf, hr, s, mods, w_out)


def _fn_proj_kernel(s_ref, mod_ref, g_ref, wt_ref, cs_ref, yr_ref, yi_ref, ar_scr, ai_scr):
    shift, scale, _ = _mod_rows(mod_ref, 1, 0, 0)
    nm = wt_ref.shape[0]
    gw = nm // FN_GROUPS
    per = TM // FFT_N2
    nj = FN_TB // FFT_N2
    csb = cs_ref[...].astype(BF16)

    def sub(tc, c):
        r0 = pl.multiple_of(tc * TM, TM)
        h = _rms(s_ref[pl.ds(r0, TM), :], g_ref[...]) * (1.0 + scale) + shift
        zt = lax.dot_general(wt_ref[...], h.astype(BF16), (((1,), (1,)), ((), ())),
                             preferred_element_type=F32).astype(BF16)
        for g in range(FN_GROUPS):
            y = jnp.dot(csb, zt[g * gw:(g + 1) * gw, :], preferred_element_type=F32)
            for q in range(per):
                row0 = pl.multiple_of((tc * per + q) * nm + g * gw, gw)
                ar_scr[pl.ds(row0, gw), :] = y[:gw, q * FFT_N2:(q + 1) * FFT_N2]
                ai_scr[pl.ds(row0, gw), :] = y[gw:, q * FFT_N2:(q + 1) * FFT_N2]
        return c

    lax.fori_loop(0, FN_TB // TM, sub, 0)

    def relayout(m, c):
        yr_ref[m] = ar_scr[pl.ds(m, nj, stride=nm), :]
        yi_ref[m] = ai_scr[pl.ds(m, nj, stride=nm), :]
        return c

    lax.fori_loop(0, nm, relayout, 0, unroll=8)


def _dft_cos_sin(n, scale):
    k = np.arange(n, dtype=np.int64)
    ang = 2.0 * np.pi * ((k[:, None] * k[None, :]) % n).astype(np.float64) / n
    return np.cos(ang) * scale, np.sin(ang) * scale


def _fn_proj(s, mods, norm_g, w_in):
    t, d = s.shape
    nm = w_in.shape[1]
    gw = nm // FN_GROUPS
    n1 = t // FFT_N2
    nj = FN_TB // FFT_N2
    c, sn = _dft_cos_sin(gw, gw ** -0.5)
    cs = jnp.asarray(np.concatenate([c, -sn], axis=0), F32)
    yspec = pl.BlockSpec((nm, nj, FFT_N2), lambda i: (0, i, 0))
    yshape = jax.ShapeDtypeStruct((nm, n1, FFT_N2), F32)
    return pl.pallas_call(
        _fn_proj_kernel,
        out_shape=(yshape, yshape),
        grid=(t // FN_TB,),
        in_specs=[pl.BlockSpec((FN_TB, d), lambda i: (i, 0)), _full(mods.shape), _full((1, d)),
                  _full((nm, d)), _full(cs.shape)],
        out_specs=(yspec, yspec),
        scratch_shapes=[pltpu.VMEM((nj * nm, FFT_N2), F32), pltpu.VMEM((nj * nm, FFT_N2), F32)],
        compiler_params=_cparams("arbitrary"),
        name="fourier_proj",
    )(s, mods, norm_g.reshape(1, d), w_in.T.astype(BF16), cs)


def _fn_fft_kernel(yr_ref, yi_ref, m_ref, tc_ref, ts_ref, d_ref, o_ref):
    n1 = yr_ref.shape[1]
    n2 = FFT_N2
    xr = jnp.concatenate([yr_ref[m].astype(BF16) for m in range(FN_CB)], axis=1)
    xi = jnp.concatenate([yi_ref[m].astype(BF16) for m in range(FN_CB)], axis=1)
    a = jnp.dot(m_ref[...].astype(BF16), jnp.concatenate([xr, xi], axis=0),
                preferred_element_type=F32)
    ar = a[:n1]
    ai = a[n1:]
    tc = jnp.concatenate([tc_ref[...]] * FN_CB, axis=1)
    ts = jnp.concatenate([ts_ref[...]] * FN_CB, axis=1)
    br = ar * tc + ai * ts
    bi = ai * tc - ar * ts
    bst = jnp.concatenate(
        [jnp.concatenate([br[:, m * n2:(m + 1) * n2], bi[:, m * n2:(m + 1) * n2]], axis=1)
         for m in range(FN_CB)], axis=0).astype(BF16)
    res = lax.dot_general(d_ref[...].astype(BF16), bst, (((1,), (1,)), ((), ())),
                          preferred_element_type=F32)
    for m in range(FN_CB):
        o_ref[m] = res[:, m * n1:(m + 1) * n1]


def _fn_fft(yr, yi):
    nm, n1, n2 = yr.shape
    t = n1 * n2
    c, sn = _dft_cos_sin(n1, n1 ** -0.5)
    m = jnp.asarray(np.block([[c, sn], [-sn, c]]), F32)
    k1 = np.arange(n1, dtype=np.int64)[:, None]
    t2 = np.arange(n2, dtype=np.int64)[None, :]
    ang = 2.0 * np.pi * ((k1 * t2) % t).astype(np.float64) / t
    tc = jnp.asarray(np.cos(ang), F32)
    ts = jnp.asarray(np.sin(ang), F32)
    c2, s2 = _dft_cos_sin(n2, n2 ** -0.5)
    dm = jnp.asarray(np.concatenate([c2, s2], axis=1), F32)
    yspec = pl.BlockSpec((FN_CB, n1, n2), lambda i: (i, 0, 0))
    return pl.pallas_call(
        _fn_fft_kernel,
        out_shape=jax.ShapeDtypeStruct((nm, n2, n1), F32),
        grid=(nm // FN_CB,),
        in_specs=[yspec, yspec, _full(m.shape), _full(tc.shape), _full(ts.shape), _full(dm.shape)],
        out_specs=pl.BlockSpec((FN_CB, n2, n1), lambda i: (i, 0, 0)),
        compiler_params=_cparams("arbitrary"),
        name="fourier_fft",
    )(yr, yi, m, tc, ts, dm)


def _fn_out_kernel(ft_ref, w_ref, s_ref, mod_ref, o_ref, a_scr):
    gate = _mod_rows(mod_ref, 1, 0, 0)[2]
    nm, nj, n1 = ft_ref.shape

    def relayout(m, c):
        a_scr[pl.ds(m, nj, stride=nm), :] = ft_ref[m]
        return c

    lax.fori_loop(0, nm, relayout, 0, unroll=8)
    for j in range(nj):
        slab = a_scr[j * nm:(j + 1) * nm, :].astype(BF16)
        y = lax.dot_general(slab, w_ref[...], (((0,), (0,)), ((), ())), preferred_element_type=F32)
        rows = slice(j * n1, (j + 1) * n1)
        o_ref[rows, :] = s_ref[rows, :] + gate * y


def _fn_out(ft, s, mods, w_out):
    t, d = s.shape
    nm, n2, n1 = ft.shape
    nj = FN_TB // n1
    tok = pl.BlockSpec((FN_TB, d), lambda i: (i, 0))
    return pl.pallas_call(
        _fn_out_kernel,
        out_shape=jax.ShapeDtypeStruct((t, d), F32),
        grid=(t // FN_TB,),
        in_specs=[pl.BlockSpec((nm, nj, n1), lambda i: (0, i, 0)), _full(w_out.shape), tok,
                  _full(mods.shape)],
        out_specs=tok,
        scratch_shapes=[pltpu.VMEM((nj * nm, n1), F32)],
        compiler_params=_cparams("arbitrary"),
        name="fourier_out",
    )(ft, w_out.astype(BF16), s, mods)


def _fourier_layer(s, mods, norm_g, w_in, w_out):
    yr, yi = _fn_proj(s, mods, norm_g, w_in)
    return _fn_out(_fn_fft(yr, yi), s, mods, w_out)


def kernel(x, c, ctx, c_ctx, ada_w, ada_b, norm_mix_g, norm_ffn_g, final_norm_g, router_group_w, router_group_b, router_expert_w, router_expert_b, expert_w_gate, expert_w_up, expert_w_down, cm_w_in, cm_v_norm_g, cm_w_s, cm_b_s, cm_w_out, ml_w_in, ml_conv_w, ml_conv_b, ml_gate_b, ml_norm_g, ml_w_out, lru_w_in, lru_conv_w, lru_conv_b, lru_w_a, lru_b_a, lru_w_x, lru_b_x, lru_lambda, lru_w_out, fn_w_in, fn_w_out):
    bsz, seq, d = x.shape
    assert bsz == 1 and ada_w.shape[0] == 4 and ctx.shape[1] == TM
    c_rows = jnp.concatenate([c_ctx[None, :], c, jnp.zeros((SUBLANES - 2, d), F32)], axis=0)
    mods = _ada_table(c_rows, ada_w, ada_b)
    s = _prep_stream(x[0], ctx[0])

    def moe(s, i, ctx_tiles, final_norm=False):
        return _moe_layer(s, mods[i], norm_ffn_g[i], router_group_w[i], router_group_b[i],
                          router_expert_w[i], router_expert_b[i], expert_w_gate, expert_w_up,
                          expert_w_down, i, ctx_tiles, final_norm_g, final_norm)

    s = _chunk_mlp_layer(s, mods[0], norm_mix_g[0], cm_w_in[0], cm_v_norm_g[0], cm_w_s[0],
                         cm_b_s[0], cm_w_out[0])
    s = moe(s, 0, 1)
    s = _mlstm_layer(s, mods[1], norm_mix_g[1], ml_w_in[0], ml_conv_w[0], ml_conv_b[0],
                     ml_gate_b[0], ml_norm_g[0], ml_w_out[0])
    s = moe(s, 1, 1)
    s = _rglru_layer(s, mods[2], norm_mix_g[2], lru_w_in[0], lru_conv_w[0], lru_conv_b[0],
                     lru_w_a[0], lru_b_a[0], lru_w_x[0], lru_b_x[0], lru_lambda[0], lru_w_out[0])
    s = moe(s, 2, 0)
    s = _fourier_layer(s, mods[3], norm_mix_g[3], fn_w_in[0], fn_w_out[0])
    s = moe(s, 3, 0, final_norm=True)
    return s[None]
```

```python
import functools
import math

import jax
import jax.numpy as jnp
import numpy as np
from jax import lax
from jax.experimental import pallas as pl
from jax.experimental.pallas import tpu as pltpu

F32 = jnp.float32
BF16 = jnp.bfloat16

EPS = 1e-6
POS_BASE = 10000.0
GRID_W = 64
N_MOD = 6
TM = 256
LANES = 128
SUBLANES = 8
VMEM_LIMIT = 56 * 1024 * 1024

CM_CHUNK = 128
CM_GROUPS = 4
ML_HEADS = 4
ML_DK = 128
ML_DV = 256
ML_CHUNK = 128
LRU_HEADS = 10
LRU_BLOCK = 128
LRU_C = 8.0
FN_GROUPS = 4
FFT_N2 = 128
MOE_GROUPS = 4
MOE_EPG = 8
MOE_EXPERTS = MOE_GROUPS * MOE_EPG
MOE_ROWS_LOG2 = 8
MOE_ROWS = 1 << MOE_ROWS_LOG2
ROUTE_ROWS = 40
FN_TB = 1024
FN_CB = 8
CONV_LEFT = 2

HI = lax.Precision.HIGHEST


def _cparams(*sem):
    return pltpu.CompilerParams(dimension_semantics=sem, vmem_limit_bytes=VMEM_LIMIT)


def _full(shape):
    nd = len(shape)
    return pl.BlockSpec(shape, lambda *_: (0,) * nd)


def _rms(x, g):
    return x * lax.rsqrt(jnp.mean(x * x, axis=-1, keepdims=True) + EPS) * g


def _gelu(x):
    c = math.sqrt(2.0 / math.pi)
    return 0.5 * x * (1.0 + jnp.tanh(c * (x + 0.044715 * (x * x * x))))


def _sigmoid(x):
    return 1.0 / (1.0 + jnp.exp(-x))


def _silu(x):
    return x * _sigmoid(x)


def _softplus(x):
    return jnp.maximum(x, 0.0) + jnp.log(1.0 + jnp.exp(-jnp.abs(x)))


def _mod_rows(mod_ref, tile, ctx_tiles, first):
    row = jnp.where(tile < ctx_tiles, 0, 1)
    m = mod_ref[pl.ds(row, 1), :]
    d = m.shape[1] // N_MOD
    return tuple(m[:, (first + j) * d:(first + j + 1) * d] for j in range(3))


def _bdot(a, b):
    return jnp.dot(a.astype(BF16), b.astype(BF16), preferred_element_type=F32)


def _ada_kernel(c_ref, w_ref, b_ref, o_ref):
    c = c_ref[...]
    o_ref[...] = jnp.dot(_silu(c), w_ref[...], precision=HI,
                         preferred_element_type=F32) + b_ref[...]


def _ada_table(c_rows, ada_w, ada_b):
    depth, d, n = ada_w.shape
    tn = 1024
    return pl.pallas_call(
        _ada_kernel,
        out_shape=jax.ShapeDtypeStruct((depth, SUBLANES, n), F32),
        grid=(depth, n // tn),
        in_specs=[_full((SUBLANES, d)),
                  pl.BlockSpec((None, d, tn), lambda i, j: (i, 0, j)),
                  pl.BlockSpec((None, 1, tn), lambda i, j: (i, 0, j))],
        out_specs=pl.BlockSpec((None, SUBLANES, tn), lambda i, j: (i, 0, j)),
        compiler_params=_cparams("arbitrary", "arbitrary"),
        name="ada_table",
    )(c_rows, ada_w, ada_b.reshape(depth, 1, n))


def _prep_kernel(x_ref, ctx_ref, rt_ref, ct_ref, o_ref):
    i = pl.program_id(0)

    @pl.when(i == 0)
    def _():
        o_ref[...] = ctx_ref[...]

    @pl.when(i > 0)
    def _():
        rows_per_tile = TM // GRID_W
        q2 = rt_ref.shape[1]
        r0 = (i - 1) * rows_per_tile
        rt = jnp.concatenate(
            [jnp.broadcast_to(rt_ref[pl.ds(r0 + j, 1), :], (GRID_W, q2))
             for j in range(rows_per_tile)], axis=0)
        ct = jnp.concatenate([ct_ref[...]] * rows_per_tile, axis=0)
        o_ref[...] = x_ref[...] + jnp.concatenate([rt, ct], axis=1)


def _prep_stream(x2, ctx2):
    seq, d = x2.shape
    n_ctx = ctx2.shape[0]
    assert n_ctx == TM and seq % TM == 0 and TM % GRID_W == 0
    q = d // 4
    freq = jnp.exp(-math.log(POS_BASE) * jnp.arange(q, dtype=F32) / q)
    ar = jnp.arange(seq // GRID_W, dtype=F32)[:, None] * freq
    ac = jnp.arange(GRID_W, dtype=F32)[:, None] * freq
    rt = jnp.concatenate([jnp.sin(ar), jnp.cos(ar)], axis=-1)
    ct = jnp.concatenate([jnp.sin(ac), jnp.cos(ac)], axis=-1)
    nt = 1 + seq // TM
    return pl.pallas_call(
        _prep_kernel,
        out_shape=jax.ShapeDtypeStruct((n_ctx + seq, d), F32),
        grid=(nt,),
        in_specs=[pl.BlockSpec((TM, d), lambda i: (jnp.maximum(i - 1, 0), 0)),
                  _full((TM, d)), _full(rt.shape), _full(ct.shape)],
        out_specs=pl.BlockSpec((TM, d), lambda i: (i, 0)),
        compiler_params=_cparams("arbitrary"),
        name="prep_stream",
    )(x2, ctx2, rt, ct)


def _cm_kernel(s_ref, mod_ref, g_ref, win_ref, vg_ref, ws_ref, bs_ref, wout_ref, o_ref, p_scr):
    i = pl.program_id(0)
    shift, scale, gate = _mod_rows(mod_ref, i, 1, 0)
    x = s_ref[...]
    h = _rms(x, g_ref[...]) * (1.0 + scale) + shift
    z = _gelu(_bdot(h, win_ref[...]))
    w = z.shape[1] // 2
    u = z[:, :w]
    v = _rms(z[:, w:], vg_ref[...]).astype(BF16)
    gw = w // CM_GROUPS
    for c in range(TM // CM_CHUNK):
        r = slice(c * CM_CHUNK, (c + 1) * CM_CHUNK)
        for g in range(CM_GROUPS):
            cs = slice(g * gw, (g + 1) * gw)
            s = jnp.dot(ws_ref[g], v[r, cs], preferred_element_type=F32) + bs_ref[:, g:g + 1]
            p_scr[r, cs] = (u[r, cs] * s).astype(BF16)
    y = jnp.dot(p_scr[...], wout_ref[...], preferred_element_type=F32)
    o_ref[...] = x + gate * y


def _chunk_mlp_layer(s, mods, norm_g, w_in, v_g, w_s, b_s, w_out):
    n, d = s.shape
    w = w_out.shape[0]
    return pl.pallas_call(
        _cm_kernel,
        out_shape=jax.ShapeDtypeStruct((n, d), F32),
        grid=(n // TM,),
        in_specs=[pl.BlockSpec((TM, d), lambda i: (i, 0)),
                  _full(mods.shape), _full((1, d)), _full(w_in.shape), _full((1, w)),
                  _full(w_s.shape), _full((CM_CHUNK, CM_GROUPS)), _full(w_out.shape)],
        out_specs=pl.BlockSpec((TM, d), lambda i: (i, 0)),
        scratch_shapes=[pltpu.VMEM((TM, w), BF16)],
        compiler_params=_cparams("arbitrary"),
        name="chunk_mlp",
    )(s, mods, norm_g.reshape(1, d), w_in.astype(BF16), v_g.reshape(1, w),
      w_s.astype(BF16), b_s.T, w_out.astype(BF16))


def _store_token_tiles(ref, x):
    rows, d = x.shape
    for j in range(d // LANES):
        ref[pl.ds(j, rows, stride=d // LANES), :] = x[:, j * LANES:(j + 1) * LANES]


def _load_token_tiles(ref):
    chunks = SUBLANES
    rows = ref.shape[0] // chunks
    return jnp.concatenate([ref[pl.ds(j, rows, stride=chunks), :] for j in range(chunks)], axis=1)


def _router_kernel(s_ref, mod_ref, g_ref, rwt_ref, rbt_ref, h_ref, e1_ref, e2_ref, r1_ref, r2_ref,
                   wt_ref, cnt_ref, carry, *, ctx_tiles):
    i = pl.program_id(0)

    @pl.when(i == 0)
    def _():
        carry[...] = jnp.zeros_like(carry)

    shift, scale, _ = _mod_rows(mod_ref, i, ctx_tiles, 3)
    h = _rms(s_ref[...], g_ref[...]) * (1.0 + scale) + shift
    _store_token_tiles(h_ref, h)
    logits = lax.dot_general(rwt_ref[...], h, (((1,), (1,)), ((), ())), precision=HI,
                             preferred_element_type=F32) + rbt_ref[...]
    row = lax.broadcasted_iota(jnp.int32, logits.shape, 0)
    neg = jnp.float32(-jnp.inf)
    big = jnp.int32(1 << 20)
    is_g = row < MOE_GROUPS
    gl = jnp.where(is_g, logits, neg)
    gmax = jnp.max(gl, axis=0, keepdims=True)
    grp = jnp.min(jnp.where(is_g & (gl == gmax), row, big), axis=0, keepdims=True)
    p_grp = 1.0 / jnp.sum(jnp.exp(gl - gmax), axis=0, keepdims=True)
    e_row = row - MOE_GROUPS
    in_grp = (e_row >= 0) & (e_row < MOE_EXPERTS) & ((e_row >> 3) == grp)
    l1 = jnp.where(in_grp, logits, neg)
    v1 = jnp.max(l1, axis=0, keepdims=True)
    i1 = jnp.min(jnp.where(in_grp & (l1 == v1), row, big), axis=0, keepdims=True)
    rest = in_grp & (row != i1)
    l2 = jnp.where(rest, logits, neg)
    v2 = jnp.max(l2, axis=0, keepdims=True)
    i2 = jnp.min(jnp.where(rest & (l2 == v2), row, big), axis=0, keepdims=True)
    e21 = jnp.exp(v2 - v1)
    w1 = p_grp / (1.0 + e21)
    w2 = p_grp * e21 / (1.0 + e21)
    oh1 = (row == i1).astype(F32)
    oh2 = (row == i2).astype(F32)
    oh = oh1 + oh2
    ri = lax.broadcasted_iota(jnp.int32, (TM, TM), 0)
    ci = lax.broadcasted_iota(jnp.int32, (TM, TM), 1)
    earlier = (ri < ci).astype(BF16)
    before = jnp.dot(oh.astype(BF16), earlier, preferred_element_type=F32) + carry[:, 0:1]
    r1_ref[...] = jnp.sum(oh1 * before, axis=0, keepdims=True).astype(jnp.int32)
    r2_ref[...] = jnp.sum(oh2 * before, axis=0, keepdims=True).astype(jnp.int32)
    e1_ref[...] = i1 - MOE_GROUPS
    e2_ref[...] = i2 - MOE_GROUPS
    carry[...] = carry[...] + jnp.sum(oh, axis=1, keepdims=True)
    cnt_ref[...] = carry[...]
    wt_ref[...] = jnp.concatenate([w1, w2, jnp.zeros((LANES - 2, TM), F32)], axis=0).T


def _router(s, mods, norm_g, rg_w, rg_b, re_w, re_b, ctx_tiles):
    n, d = s.shape
    nt = n // TM
    pad = ROUTE_ROWS - MOE_GROUPS - MOE_EXPERTS
    rwt = jnp.concatenate([rg_w, re_w, jnp.zeros((d, pad), F32)], axis=1).T
    rbt = jnp.broadcast_to(jnp.concatenate([rg_b, re_b, jnp.zeros((pad,), F32)])[:, None],
                           (ROUTE_ROWS, TM))
    assert d == SUBLANES * LANES
    tile = pl.BlockSpec((TM, d), lambda i: (i, 0))
    irow = pl.BlockSpec((None, 1, TM), lambda i: (i, 0, 0))
    ishape = jax.ShapeDtypeStruct((nt, 1, TM), jnp.int32)
    return pl.pallas_call(
        functools.partial(_router_kernel, ctx_tiles=ctx_tiles),
        out_shape=(jax.ShapeDtypeStruct((n * SUBLANES, LANES), F32), ishape, ishape, ishape, ishape,
                   jax.ShapeDtypeStruct((n, LANES), F32),
                   jax.ShapeDtypeStruct((ROUTE_ROWS, LANES), F32)),
        grid=(nt,),
        in_specs=[tile, _full(mods.shape), _full((1, d)), _full((ROUTE_ROWS, d)),
                  _full((ROUTE_ROWS, TM))],
        out_specs=(pl.BlockSpec((TM * SUBLANES, LANES), lambda i: (i, 0)), irow, irow, irow, irow,
                   pl.BlockSpec((TM, LANES), lambda i: (i, 0)), _full((ROUTE_ROWS, LANES))),
        scratch_shapes=[pltpu.VMEM((ROUTE_ROWS, LANES), F32)],
        compiler_params=_cparams("arbitrary"),
        name="moe_router",
    )(s, mods, norm_g.reshape(1, d), rwt, rbt)


def _finalize_kernel(cnt_ref, e1_ref, e2_ref, r1_ref, r2_ref, d1_ref, d2_ref, blk_ref):
    e1 = e1_ref[...]
    e2 = e2_ref[...]
    r1 = r1_ref[...]
    r2 = r2_ref[...]
    d1 = jnp.zeros_like(e1)
    d2 = jnp.zeros_like(e2)
    lane = lax.broadcasted_iota(jnp.int32, blk_ref.shape, 1)
    brow = lane * MOE_ROWS
    sub = lax.broadcasted_iota(jnp.int32, blk_ref.shape, 0)
    be = jnp.zeros(blk_ref.shape, jnp.int32)
    pend = jnp.zeros(blk_ref.shape, jnp.int32)
    ps = jnp.int32(0)
    for e in range(MOE_EXPERTS):
        c = cnt_ref[e]
        pe = ps + lax.shift_left(lax.shift_right_logical(c + (MOE_ROWS - 1), MOE_ROWS_LOG2),
                                 MOE_ROWS_LOG2)
        d1 = jnp.where(e1 == e, ps + r1, d1)
        d2 = jnp.where(e2 == e, ps + r2, d2)
        be = be + (brow >= pe).astype(jnp.int32)
        pend = jnp.where(lane == e, pe, pend)
        ps = pe
    d1_ref[...] = d1
    d2_ref[...] = d2
    n_used = lax.shift_right_logical(ps, MOE_ROWS_LOG2)
    blk_ref[...] = jnp.where(sub == 0, jnp.minimum(be, MOE_EXPERTS - 1),
                             jnp.where(sub == 1, pend, n_used))


def _finalize(counts, e1, e2, r1, r2, nb):
    nbp = (nb + LANES - 1) // LANES * LANES
    whole = pl.BlockSpec(e1.shape, lambda i, c: (0, 0, 0))
    ishape = jax.ShapeDtypeStruct(e1.shape, jnp.int32)
    return pl.pallas_call(
        _finalize_kernel,
        out_shape=(ishape, ishape, jax.ShapeDtypeStruct((SUBLANES, nbp), jnp.int32)),
        grid_spec=pltpu.PrefetchScalarGridSpec(
            num_scalar_prefetch=1,
            grid=(1,),
            in_specs=[whole, whole, whole, whole],
            out_specs=(whole, whole, pl.BlockSpec((SUBLANES, nbp), lambda i, c: (0, 0)))),
        compiler_params=_cparams("arbitrary"),
        name="moe_finalize",
    )(counts, e1, e2, r1, r2)


def _token_copy(src, r, dst, d, sem):
    return pltpu.make_async_copy(src.at[pl.ds(pl.multiple_of(r * SUBLANES, SUBLANES), SUBLANES), :],
                                 dst.at[pl.ds(pl.multiple_of(d * SUBLANES, SUBLANES), SUBLANES), :],
                                 sem)


def _zero_fill_padding(pend_ref, nu_ref, xs_out, zbuf, zsem):
    blk_rows = MOE_ROWS * SUBLANES
    nb = xs_out.shape[0] // blk_rows
    zbuf[...] = jnp.zeros_like(zbuf)

    def block_copy(b):
        r0 = pl.multiple_of(b * blk_rows, blk_rows)
        return pltpu.make_async_copy(zbuf, xs_out.at[pl.ds(r0, blk_rows), :], zsem)

    def seg_last_block(e):
        pe = pend_ref[e]
        prev = pend_ref[e - 1] if e > 0 else 0
        return pe > prev, lax.shift_right_logical(pe, MOE_ROWS_LOG2) - 1

    for e in range(MOE_EXPERTS):
        nonempty, b = seg_last_block(e)

        @pl.when(nonempty)
        def _():
            block_copy(b).start()

    def tail_start(b, c):
        block_copy(b).start()
        return c

    lax.fori_loop(nu_ref[0], nb, tail_start, 0)
    for e in range(MOE_EXPERTS):
        nonempty, b = seg_last_block(e)

        @pl.when(nonempty)
        def _():
            block_copy(b).wait()

    def tail_wait(b, c):
        block_copy(b).wait()
        return c

    lax.fori_loop(nu_ref[0], nb, tail_wait, 0)


def _dispatch_kernel(dest_ref, pend_ref, nu_ref, h_ref, xs_out, sem, zbuf, zsem, *, tiles):
    i = pl.program_id(0)

    @pl.when(i == 0)
    def _():
        _zero_fill_padding(pend_ref, nu_ref, xs_out, zbuf, zsem)

    for q in range(tiles):
        base = (i * tiles + q) * (2 * TM)

        def start(r, c):
            _token_copy(h_ref, q * TM + r, xs_out, dest_ref[base + r], sem).start(priority=0)
            _token_copy(h_ref, q * TM + r, xs_out, dest_ref[base + TM + r], sem).start(priority=1)
            return c

        lax.fori_loop(0, TM, start, 0, unroll=8)
    for _ in range(2):
        pltpu.make_async_copy(h_ref, xs_out.at[pl.ds(0, tiles * TM * SUBLANES), :], sem).wait()


def _dispatch(dest, pad_end, n_used, h, n_rows):
    n = h.shape[0] // SUBLANES
    nt = n // TM
    tiles = next(k for k in (5, 4, 2, 1) if nt % k == 0)
    return pl.pallas_call(
        functools.partial(_dispatch_kernel, tiles=tiles),
        out_shape=jax.ShapeDtypeStruct((n_rows * SUBLANES, LANES), F32),
        grid_spec=pltpu.PrefetchScalarGridSpec(
            num_scalar_prefetch=3,
            grid=(nt // tiles,),
            in_specs=[pl.BlockSpec((tiles * TM * SUBLANES, LANES), lambda i, *_: (i, 0))],
            out_specs=pl.BlockSpec(memory_space=pl.ANY),
            scratch_shapes=[pltpu.SemaphoreType.DMA, pltpu.VMEM((MOE_ROWS * SUBLANES, LANES), F32),
                            pltpu.SemaphoreType.DMA]),
        compiler_params=_cparams("arbitrary"),
        name="moe_dispatch",
    )(dest, pad_end, n_used, h)


def _expert_kernel(be_ref, nu_ref, x_ref, wg_ref, wu_ref, wd_ref, y_ref, wg_s, wu_s, wd_s):
    b = pl.program_id(0)
    used = b < nu_ref[0]
    prev = be_ref[jnp.maximum(b - 1, 0)]
    fresh = (b == 0) | (be_ref[b] != prev)

    @pl.when(used & fresh)
    def _():
        wg_s[...] = wg_ref[...].astype(BF16)
        wu_s[...] = wu_ref[...].astype(BF16)
        wd_s[...] = wd_ref[...].astype(BF16)

    @pl.when(used)
    def _():
        x = _load_token_tiles(x_ref).astype(BF16)
        a = jnp.dot(x, wg_s[...], preferred_element_type=F32)
        u = jnp.dot(x, wu_s[...], preferred_element_type=F32)
        _store_token_tiles(y_ref, jnp.dot((_silu(a) * u).astype(BF16), wd_s[...],
                                          preferred_element_type=F32))

    @pl.when(jnp.logical_not(used))
    def _():
        y_ref[...] = jnp.zeros_like(y_ref)


def _experts(blk_expert, n_used, xs, w_gate, w_up, w_down, layer):
    d, hid = w_gate.shape[2:]
    nb = xs.shape[0] // (MOE_ROWS * SUBLANES)
    wmap = lambda b, be, nu: (layer, be[b], 0, 0)
    rows = pl.BlockSpec((MOE_ROWS * SUBLANES, LANES), lambda b, be, nu: (b, 0))
    return pl.pallas_call(
        _expert_kernel,
        out_shape=jax.ShapeDtypeStruct(xs.shape, F32),
        grid_spec=pltpu.PrefetchScalarGridSpec(
            num_scalar_prefetch=2,
            grid=(nb,),
            in_specs=[rows,
                      pl.BlockSpec((None, None, d, hid), wmap),
                      pl.BlockSpec((None, None, d, hid), wmap),
                      pl.BlockSpec((None, None, hid, d), wmap)],
            out_specs=rows,
            scratch_shapes=[pltpu.VMEM((d, hid), BF16), pltpu.VMEM((d, hid), BF16),
                            pltpu.VMEM((hid, d), BF16)]),
        compiler_params=_cparams("arbitrary"),
        name="moe_experts",
    )(blk_expert, n_used, xs, w_gate, w_up, w_down)


def _combine_kernel(dest_ref, s_ref, wt_ref, mod_ref, fg_ref, ys_ref, o_ref, ybuf, sem,
                    *, ctx_tiles, final_norm):
    i = pl.program_id(0)
    slot = i % 2

    def gather(tile, slot):
        base = tile * (2 * TM)

        def start(r, c):
            _token_copy(ys_ref, dest_ref[base + r], ybuf.at[slot, 0], r,
                        sem.at[slot]).start(priority=0)
            _token_copy(ys_ref, dest_ref[base + TM + r], ybuf.at[slot, 1], r,
                        sem.at[slot]).start(priority=1)
            return c

        lax.fori_loop(0, TM, start, 0, unroll=8)

    @pl.when(i == 0)
    def _():
        gather(0, 0)

    @pl.when(i + 1 < pl.num_programs(0))
    def _():
        gather(i + 1, 1 - slot)

    for k in range(2):
        pltpu.make_async_copy(ys_ref.at[pl.ds(0, TM * SUBLANES), :], ybuf.at[slot, k],
                              sem.at[slot]).wait()
    gate = _mod_rows(mod_ref, i, ctx_tiles, 3)[2]
    wt = wt_ref[...]
    y = (wt[:, 0:1] * _load_token_tiles(ybuf.at[slot, 0])
         + wt[:, 1:2] * _load_token_tiles(ybuf.at[slot, 1]))
    out = s_ref[...] + gate * y
    if final_norm:
        out = _rms(out, fg_ref[...])
    o_ref[...] = out


def _combine(dest, s, wts, mods, final_g, ys, ctx_tiles, final_norm):
    n, d = s.shape
    return pl.pallas_call(
        functools.partial(_combine_kernel, ctx_tiles=ctx_tiles, final_norm=final_norm),
        out_shape=jax.ShapeDtypeStruct((n, d), F32),
        grid_spec=pltpu.PrefetchScalarGridSpec(
            num_scalar_prefetch=1,
            grid=(n // TM,),
            in_specs=[pl.BlockSpec((TM, d), lambda i, dst: (i, 0)),
                      pl.BlockSpec((TM, LANES), lambda i, dst: (i, 0)),
                      pl.BlockSpec(mods.shape, lambda i, dst: (0, 0)),
                      pl.BlockSpec((1, d), lambda i, dst: (0, 0)),
                      pl.BlockSpec(memory_space=pl.ANY)],
            out_specs=pl.BlockSpec((TM, d), lambda i, dst: (i, 0)),
            scratch_shapes=[pltpu.VMEM((2, 2, TM * SUBLANES, LANES), F32),
                            pltpu.SemaphoreType.DMA((2,))]),
        compiler_params=_cparams("arbitrary"),
        name="moe_combine",
    )(dest, s, wts, mods, final_g.reshape(1, d), ys)


def _moe_layer(s, mods, norm_g, rg_w, rg_b, re_w, re_b, w_gate, w_up, w_down, layer, ctx_tiles,
               final_g, final_norm):
    n, d = s.shape
    h, e1, e2, r1, r2, wts, cnt = _router(s, mods, norm_g, rg_w, rg_b, re_w, re_b, ctx_tiles)
    counts = cnt[MOE_GROUPS:MOE_GROUPS + MOE_EXPERTS, 0].astype(jnp.int32)
    nb = (2 * n + MOE_EXPERTS * (MOE_ROWS - 1)) // MOE_ROWS + 1
    d1, d2, blk = _finalize(counts, e1, e2, r1, r2, nb)
    dest = jnp.concatenate([d1, d2], axis=1).reshape(2 * n)
    n_used = blk[2, :1]
    xs = _dispatch(dest, blk[1, :MOE_EXPERTS], n_used, h, nb * MOE_ROWS)
    ys = _experts(blk[0, :nb], n_used, xs, w_gate, w_up, w_down, layer)
    return _combine(dest, s, wts, mods, final_g, ys, ctx_tiles, final_norm)


def _conv_tile(x, prev_ref, next_ref, has_prev, has_next, w_ref, b_ref):
    rows = x.shape[0]
    ridx = lax.broadcasted_iota(jnp.int32, x.shape, 0)
    pm = jnp.where(has_prev, 1.0, 0.0)
    nm = jnp.where(has_next, 1.0, 0.0)
    p2 = prev_ref[SUBLANES - 2:SUBLANES - 1, :] * pm
    p1 = prev_ref[SUBLANES - 1:SUBLANES, :] * pm
    n1 = next_ref[0:1, :] * nm
    xm1 = jnp.where(ridx == 0, p1, pltpu.roll(x, 1, axis=0))
    xm2 = jnp.where(ridx == 0, p2, jnp.where(ridx == 1, p1, pltpu.roll(x, 2, axis=0)))
    xp1 = jnp.where(ridx == rows - 1, n1, pltpu.roll(x, rows - 1, axis=0))
    return (xm2 * w_ref[0:1, :] + xm1 * w_ref[1:2, :] + x * w_ref[2:3, :]
            + xp1 * w_ref[3:4, :] + b_ref[...])


def _ml_proj_kernel(s_ref, mod_ref, g_ref, w_ref, wg_ref, gb_ref, qk_ref, v_ref, o_ref, gt_ref):
    i = pl.program_id(0)
    shift, scale, _ = _mod_rows(mod_ref, i, 1, 0)
    h = _rms(s_ref[...], g_ref[...]) * (1.0 + scale) + shift
    z = _bdot(h, w_ref[...])
    nqk = qk_ref.shape[1]
    nv = v_ref.shape[1]
    qk_ref[...] = z[:, :nqk]
    v_ref[...] = z[:, nqk:nqk + nv].astype(BF16)
    o_ref[...] = z[:, nqk + nv:].astype(BF16)
    pre = jnp.dot(h, wg_ref[...], precision=HI, preferred_element_type=F32) + gb_ref[...]
    lane = lax.broadcasted_iota(jnp.int32, pre.shape, 1)
    is_forget = ((lane >> 2) & 1) == 1
    gt_ref[...] = jnp.where(is_forget, -_softplus(-pre), pre)


def _ml_proj(s, mods, norm_g, w_in, gate_b):
    n, d = s.shape
    nqk = 2 * ML_HEADS * ML_DK
    nv = ML_HEADS * ML_DV
    n_main = nqk + 2 * nv
    n_gate = w_in.shape[1] - n_main
    w_main = w_in[:, :n_main].astype(BF16)
    w_gate = jnp.concatenate([w_in[:, n_main:], jnp.zeros((d, LANES - n_gate), F32)], axis=1)
    gb = jnp.concatenate([gate_b.reshape(n_gate), jnp.zeros((LANES - n_gate,), F32)]).reshape(1, LANES)
    tile = lambda w: pl.BlockSpec((TM, w), lambda i: (i, 0))
    return pl.pallas_call(
        _ml_proj_kernel,
        out_shape=(jax.ShapeDtypeStruct((n, nqk), F32), jax.ShapeDtypeStruct((n, nv), BF16),
                   jax.ShapeDtypeStruct((n, nv), BF16), jax.ShapeDtypeStruct((n, LANES), F32)),
        grid=(n // TM,),
        in_specs=[tile(d), _full(mods.shape), _full((1, d)), _full(w_main.shape),
                  _full((d, LANES)), _full((1, LANES))],
        out_specs=(tile(nqk), tile(nv), tile(nv), tile(LANES)),
        compiler_params=_cparams("arbitrary"),
        name="mlstm_proj",
    )(s, mods, norm_g.reshape(1, d), w_main, w_gate, gb)


def _ml_chunk_index(j, n_chunks, ctx_chunks, reverse):
    if not reverse:
        return j
    return jnp.where(j < ctx_chunks, ctx_chunks - 1 - j, n_chunks - 1 + ctx_chunks - j)


def _ml_rec_kernel(qk_ref, qkp_ref, qkn_ref, v_ref, gt_ref, gtt_ref, cw_ref, cb_ref, o_ref,
                   c_scr, n_scr, m_scr, *, reverse, n_chunks, ctx_chunks):
    j = pl.program_id(0)
    c = _ml_chunk_index(j, n_chunks, ctx_chunks, reverse)

    @pl.when(j == 0)
    def _():
        c_scr[...] = jnp.zeros_like(c_scr)
        n_scr[...] = jnp.zeros_like(n_scr)
        m_scr[...] = jnp.zeros_like(m_scr)

    has_prev = (c != 0) & (c != ctx_chunks)
    has_next = (c != ctx_chunks - 1) & (c != n_chunks - 1)
    qk = _silu(_conv_tile(qk_ref[...], qkp_ref, qkn_ref, has_prev, has_next, cw_ref, cb_ref))
    L = ML_CHUNK
    ri = lax.broadcasted_iota(jnp.int32, (L, L), 0)
    ci = lax.broadcasted_iota(jnp.int32, (L, L), 1)
    past = (ci >= ri) if reverse else (ci <= ri)
    pastf = past.astype(F32)
    gt = gt_ref[...]
    gtt = gtt_ref[...]
    b_col = jnp.dot(pastf, gt, precision=HI, preferred_element_type=F32)
    b_row = jnp.dot(gtt, pastf.T, precision=HI, preferred_element_type=F32)
    last = 0 if reverse else L - 1
    dbase = 8 if reverse else 0
    nq = ML_HEADS * ML_DK
    for hd in range(ML_HEADS):
        cl = dbase + hd
        cf = dbase + 4 + hd
        q = qk[:, hd * ML_DK:(hd + 1) * ML_DK] * (ML_DK ** -0.5)
        k = qk[:, nq + hd * ML_DK:nq + (hd + 1) * ML_DK]
        v = v_ref[:, hd * ML_DV:(hd + 1) * ML_DV]
        li_c = gt[:, cl:cl + 1]
        li_r = gtt[cl:cl + 1, :]
        b_c = b_col[:, cf:cf + 1]
        b_r = b_row[cf:cf + 1, :]
        g = b_r[:, last:last + 1]
        m0 = m_scr[hd:hd + 1, 0:1]
        c0 = c_scr[hd]
        n0 = n_scr[hd:hd + 1, :]
        a_c = g - b_c + li_c
        a_r = g - b_r + li_r
        m_loc = jnp.max(a_r, axis=-1, keepdims=True)
        inter = b_c + m0
        dlog = jnp.where(past, b_c - b_r + li_r, -jnp.inf)
        m = jnp.maximum(inter, jnp.max(dlog, axis=-1, keepdims=True))
        qb = q.astype(BF16)
        sc = lax.dot_general(qb, k.astype(BF16), (((1,), (1,)), ((), ())),
                             preferred_element_type=F32) * jnp.exp(dlog - m)
        w_inter = jnp.exp(inter - m)
        num = (jnp.dot(sc.astype(BF16), v, preferred_element_type=F32)
               + w_inter * jnp.dot(qb, c0.astype(BF16), preferred_element_type=F32))
        den = (jnp.sum(sc, axis=-1, keepdims=True)
               + w_inter * jnp.sum(q * n0, axis=-1, keepdims=True))
        o_ref[:, hd * ML_DV:(hd + 1) * ML_DV] = (
            num / jnp.maximum(jnp.abs(den), jnp.exp(-m))).astype(BF16)
        m_new = jnp.maximum(g + m0, m_loc)
        dec = jnp.exp(g + m0 - m_new)
        scl = jnp.exp(m_loc - m_new)
        kw = k * jnp.exp(a_c - m_loc)
        c_scr[hd] = dec * c0 + scl * jnp.dot(kw.T.astype(BF16), v, preferred_element_type=F32)
        n_scr[hd:hd + 1, :] = dec * n0 + scl * jnp.sum(kw, axis=0, keepdims=True)
        m_scr[hd:hd + 1, :] = jnp.broadcast_to(m_new, (1, LANES))


def _ml_rec(qk, v, gt, gtt, conv_w, conv_b, reverse):
    n, nqk = qk.shape
    nv = v.shape[1]
    L = ML_CHUNK
    nc = n // L
    cc = TM // L
    hb = L // SUBLANES
    idx = lambda j: _ml_chunk_index(j, nc, cc, reverse)
    last8 = n // SUBLANES - 1
    return pl.pallas_call(
        functools.partial(_ml_rec_kernel, reverse=reverse, n_chunks=nc, ctx_chunks=cc),
        out_shape=jax.ShapeDtypeStruct((n, nv), BF16),
        grid=(nc,),
        in_specs=[pl.BlockSpec((L, nqk), lambda j: (idx(j), 0)),
                  pl.BlockSpec((SUBLANES, nqk), lambda j: (jnp.maximum(idx(j) * hb - 1, 0), 0)),
                  pl.BlockSpec((SUBLANES, nqk), lambda j: (jnp.minimum((idx(j) + 1) * hb, last8), 0)),
                  pl.BlockSpec((L, nv), lambda j: (idx(j), 0)),
                  pl.BlockSpec((L, LANES), lambda j: (idx(j), 0)),
                  pl.BlockSpec((2 * SUBLANES, L), lambda j: (0, idx(j))),
                  _full((4, nqk)), _full((1, nqk))],
        out_specs=pl.BlockSpec((L, nv), lambda j: (idx(j), 0)),
        scratch_shapes=[pltpu.VMEM((ML_HEADS, ML_DK, ML_DV), F32),
                        pltpu.VMEM((SUBLANES, ML_DK), F32),
                        pltpu.VMEM((SUBLANES, LANES), F32)],
        compiler_params=_cparams("arbitrary"),
        name="mlstm_rev" if reverse else "mlstm_fwd",
    )(qk, qk, qk, v, gt, gtt, conv_w, conv_b.reshape(1, nqk))


def _ml_out_kernel(hf_ref, hr_ref, o_ref, s_ref, mod_ref, ng_ref, w_ref, out_ref, p_scr):
    i = pl.program_id(0)
    gate = _mod_rows(mod_ref, i, 1, 0)[2]
    hs = hf_ref[...].astype(F32) + hr_ref[...].astype(F32)
    sig = _sigmoid(o_ref[...].astype(F32))
    ng = ng_ref[...]
    for hd in range(ML_HEADS):
        cs = slice(hd * ML_DV, (hd + 1) * ML_DV)
        seg = hs[:, cs]
        hn = seg * lax.rsqrt(jnp.mean(seg * seg, axis=-1, keepdims=True) + EPS) * ng[:, cs]
        p_scr[:, cs] = (hn * sig[:, cs]).astype(BF16)
    y = jnp.dot(p_scr[...], w_ref[...], preferred_element_type=F32)
    out_ref[...] = s_ref[...] + gate * y


def _ml_out(hf, hr, o, s, mods, norm_g, w_out):
    n, d = s.shape
    nv = hf.shape[1]
    tile = lambda w: pl.BlockSpec((TM, w), lambda i: (i, 0))
    return pl.pallas_call(
        _ml_out_kernel,
        out_shape=jax.ShapeDtypeStruct((n, d), F32),
        grid=(n // TM,),
        in_specs=[tile(nv), tile(nv), tile(nv), tile(d), _full(mods.shape), _full((1, nv)),
                  _full(w_out.shape)],
        out_specs=tile(d),
        scratch_shapes=[pltpu.VMEM((TM, nv), BF16)],
        compiler_params=_cparams("arbitrary"),
        name="mlstm_out",
    )(hf, hr, o, s, mods, norm_g.reshape(1, nv), w_out.astype(BF16))


def _mlstm_layer(s, mods, norm_g, w_in, conv_w, conv_b, gate_b, ml_norm_g, w_out):
    qk, v, o, gt = _ml_proj(s, mods, norm_g, w_in, gate_b)
    gtt = gt[:, :2 * SUBLANES].T
    hf = _ml_rec(qk, v, gt, gtt, conv_w, conv_b, False)
    hr = _ml_rec(qk, v, gt, gtt, conv_w, conv_b, True)
    return _ml_out(hf, hr, o, s, mods, ml_norm_g, w_out)


def _lru_proj_kernel(s_ref, mod_ref, g_ref, w_ref, gl_ref, xr_ref):
    i = pl.program_id(0)
    shift, scale, _ = _mod_rows(mod_ref, i, 1, 0)
    h = _rms(s_ref[...], g_ref[...]) * (1.0 + scale) + shift
    z = _bdot(h, w_ref[...])
    w = gl_ref.shape[1]
    gl_ref[...] = _gelu(z[:, :w]).astype(BF16)
    xr_ref[...] = z[:, w:]


def _lru_proj(s, mods, norm_g, w_in):
    n, d = s.shape
    w = w_in.shape[1] // 2
    tile = lambda c: pl.BlockSpec((TM, c), lambda i: (i, 0))
    return pl.pallas_call(
        _lru_proj_kernel,
        out_shape=(jax.ShapeDtypeStruct((n, w), BF16), jax.ShapeDtypeStruct((n, w), F32)),
        grid=(n // TM,),
        in_specs=[tile(d), _full(mods.shape), _full((1, d)), _full(w_in.shape)],
        out_specs=(tile(w), tile(w)),
        compiler_params=_cparams("arbitrary"),
        name="rglru_proj",
    )(s, mods, norm_g.reshape(1, d), w_in.astype(BF16))


def _lru_tile_index(j, n_tiles, reverse):
    if not reverse:
        return j
    return jnp.where(j == 0, 0, n_tiles - j)


def _lru_scan_kernel(x_ref, xp_ref, xn_ref, cw_ref, cb_ref, wg_ref, ba_ref, bx_ref, lam_ref,
                     o_ref, a_scr, u_scr, carry, *, reverse, n_tiles):
    j = pl.program_id(0)
    t = _lru_tile_index(j, n_tiles, reverse)

    @pl.when(j == 0)
    def _():
        carry[...] = jnp.zeros_like(carry)

    has_prev = t > 1
    has_next = (t != 0) & (t != n_tiles - 1)
    xr = _conv_tile(x_ref[...], xp_ref, xn_ref, has_prev, has_next, cw_ref, cb_ref)
    sp = _softplus(-lam_ref[...])
    B = LRU_BLOCK
    for hd in range(LRU_HEADS):
        cs = slice(hd * B, (hd + 1) * B)
        xh = xr[:, cs]
        y = jnp.dot(xh.astype(BF16), wg_ref[hd], preferred_element_type=F32)
        r = _sigmoid(y[:, :B] + ba_ref[:, cs])
        ig = _sigmoid(y[:, B:] + bx_ref[:, cs])
        log_a = -LRU_C * r * sp[:, cs]
        a = jnp.exp(log_a)
        a_scr[:, cs] = a
        u_scr[:, cs] = jnp.sqrt(1.0 - a * a) * (ig * xh)

    S = SUBLANES
    w = a_scr.shape[1]
    sidx = lax.broadcasted_iota(jnp.int32, (S, w), 0)

    def group(gi, c):
        g = (TM // S - 1 - gi) if reverse else gi
        r0 = pl.multiple_of(g * S, S)
        a = a_scr[pl.ds(r0, S), :]
        u = u_scr[pl.ds(r0, S), :]
        for sft in (1, 2, 4):
            if reverse:
                ok = sidx < S - sft
                a_e = pltpu.roll(a, S - sft, axis=0)
                u_e = pltpu.roll(u, S - sft, axis=0)
            else:
                ok = sidx >= sft
                a_e = pltpu.roll(a, sft, axis=0)
                u_e = pltpu.roll(u, sft, axis=0)
            u = jnp.where(ok, a * u_e + u, u)
            a = jnp.where(ok, a * a_e, a)
        hcur = a * carry[...] + u
        u_scr[pl.ds(r0, S), :] = hcur
        edge = 0 if reverse else S - 1
        carry[...] = jnp.broadcast_to(hcur[edge:edge + 1, :], (S, w))
        return c

    lax.fori_loop(0, TM // S, group, 0)
    o_ref[...] = u_scr[...].astype(BF16)


def _lru_scan(xraw, conv_w, conv_b, w_a, b_a, w_x, b_x, lam, reverse):
    n, w = xraw.shape
    nt = n // TM
    hb = TM // SUBLANES
    idx = lambda j: _lru_tile_index(j, nt, reverse)
    last8 = n // SUBLANES - 1
    wg = jnp.concatenate([w_a, w_x], axis=-1).astype(BF16)
    return pl.pallas_call(
        functools.partial(_lru_scan_kernel, reverse=reverse, n_tiles=nt),
        out_shape=jax.ShapeDtypeStruct((n, w), BF16),
        grid=(nt,),
        in_specs=[pl.BlockSpec((TM, w), lambda j: (idx(j), 0)),
                  pl.BlockSpec((SUBLANES, w), lambda j: (jnp.maximum(idx(j) * hb - 1, 0), 0)),
                  pl.BlockSpec((SUBLANES, w), lambda j: (jnp.minimum((idx(j) + 1) * hb, last8), 0)),
                  _full((4, w)), _full((1, w)), _full(wg.shape), _full((1, w)), _full((1, w)),
                  _full((1, w))],
        out_specs=pl.BlockSpec((TM, w), lambda j: (idx(j), 0)),
        scratch_shapes=[pltpu.VMEM((TM, w), F32), pltpu.VMEM((TM, w), F32),
                        pltpu.VMEM((SUBLANES, w), F32)],
        compiler_params=_cparams("arbitrary"),
        name="rglru_rev" if reverse else "rglru_fwd",
    )(xraw, xraw, xraw, conv_w, conv_b.reshape(1, w), wg, b_a.reshape(1, w), b_x.reshape(1, w),
      lam.reshape(1, w))


def _lru_out_kernel(gl_ref, hf_ref, hr_ref, s_ref, mod_ref, w_ref, out_ref):
    gate = _mod_rows(mod_ref, 1, 0, 0)[2]
    p = gl_ref[...].astype(F32) * (hf_ref[...].astype(F32) + hr_ref[...].astype(F32))
    out_ref[...] = s_ref[...] + gate * _bdot(p, w_ref[...])


def _lru_out(gl, hf, hr, s, mods, w_out):
    n, d = s.shape
    w = gl.shape[1]
    lat = lambda c: pl.BlockSpec((TM, c), lambda i: (i + 1, 0))
    return pl.pallas_call(
        _lru_out_kernel,
        out_shape=jax.ShapeDtypeStruct((n - TM, d), F32),
        grid=(n // TM - 1,),
        in_specs=[lat(w), lat(w), lat(w), lat(d), _full(mods.shape), _full(w_out.shape)],
        out_specs=pl.BlockSpec((TM, d), lambda i: (i, 0)),
        compiler_params=_cparams("arbitrary"),
        name="rglru_out",
    )(gl, hf, hr, s, mods, w_out.astype(BF16))


def _rglru_layer(s, mods, norm_g, w_in, conv_w, conv_b, w_a, b_a, w_x, b_x, lam, w_out):
    gl, xraw = _lru_proj(s, mods, norm_g, w_in)
    hf = _lru_scan(xraw, conv_w, conv_b, w_a[0], b_a[0], w_x[0], b_x[0], lam[0], False)
    hr = _lru_scan(xraw, conv_w, conv_b, w_a[1], b_a[1], w_x[1], b_x[1], lam[1], True)
    return _lru_out(gl, hf, hr, s, mods, w_out)


def _fn_proj_kernel(s_ref, mod_ref, g_ref, wt_ref, cs_ref, yr_ref, yi_ref, ar_scr, ai_scr):
    shift, scale, _ = _mod_rows(mod_ref, 1, 0, 0)
    nm = wt_ref.shape[0]
    gw = nm // FN_GROUPS
    per = TM // FFT_N2
    nj = FN_TB // FFT_N2
    csb = cs_ref[...].astype(BF16)

    def sub(tc, c):
        r0 = pl.multiple_of(tc * TM, TM)
        h = _rms(s_ref[pl.ds(r0, TM), :], g_ref[...]) * (1.0 + scale) + shift
        zt = lax.dot_general(wt_ref[...], h.astype(BF16), (((1,), (1,)), ((), ())),
                             preferred_element_type=F32).astype(BF16)
        for g in range(FN_GROUPS):
            y = jnp.dot(csb, zt[g * gw:(g + 1) * gw, :], preferred_element_type=F32)
            for q in range(per):
                row0 = pl.multiple_of((tc * per + q) * _slab_pitch(nm) + g * gw, SUBLANES)
                ar_scr[pl.ds(row0, gw), :] = y[:gw, q * FFT_N2:(q + 1) * FFT_N2]
                ai_scr[pl.ds(row0, gw), :] = y[gw:, q * FFT_N2:(q + 1) * FFT_N2]
        return c

    lax.fori_loop(0, FN_TB // TM, sub, 0)

    def relayout(m, c):
        yr_ref[m] = ar_scr[pl.ds(m, nj, stride=_slab_pitch(nm)), :]
        yi_ref[m] = ai_scr[pl.ds(m, nj, stride=_slab_pitch(nm)), :]
        return c

    lax.fori_loop(0, nm, relayout, 0, unroll=8)


def _slab_pitch(rows):
    return rows + SUBLANES


def _dft_cos_sin(n, scale):
    k = np.arange(n, dtype=np.int64)
    ang = 2.0 * np.pi * ((k[:, None] * k[None, :]) % n).astype(np.float64) / n
    return np.cos(ang) * scale, np.sin(ang) * scale


def _fn_proj(s, mods, norm_g, w_in):
    t, d = s.shape
    nm = w_in.shape[1]
    gw = nm // FN_GROUPS
    n1 = t // FFT_N2
    nj = FN_TB // FFT_N2
    c, sn = _dft_cos_sin(gw, gw ** -0.5)
    cs = jnp.asarray(np.concatenate([c, -sn], axis=0), F32)
    yspec = pl.BlockSpec((nm, nj, FFT_N2), lambda i: (0, i, 0))
    yshape = jax.ShapeDtypeStruct((nm, n1, FFT_N2), F32)
    return pl.pallas_call(
        _fn_proj_kernel,
        out_shape=(yshape, yshape),
        grid=(t // FN_TB,),
        in_specs=[pl.BlockSpec((FN_TB, d), lambda i: (i, 0)), _full(mods.shape), _full((1, d)),
                  _full((nm, d)), _full(cs.shape)],
        out_specs=(yspec, yspec),
        scratch_shapes=[pltpu.VMEM((nj * _slab_pitch(nm), FFT_N2), F32),
                        pltpu.VMEM((nj * _slab_pitch(nm), FFT_N2), F32)],
        compiler_params=_cparams("arbitrary"),
        name="fourier_proj",
    )(s, mods, norm_g.reshape(1, d), w_in.T.astype(BF16), cs)


def _fn_fft_kernel(yr_ref, yi_ref, m_ref, tc_ref, ts_ref, d_ref, o_ref):
    n1 = yr_ref.shape[1]
    n2 = FFT_N2
    xr = jnp.concatenate([yr_ref[m].astype(BF16) for m in range(FN_CB)], axis=1)
    xi = jnp.concatenate([yi_ref[m].astype(BF16) for m in range(FN_CB)], axis=1)
    a = jnp.dot(m_ref[...].astype(BF16), jnp.concatenate([xr, xi], axis=0),
                preferred_element_type=F32)
    ar = a[:n1]
    ai = a[n1:]
    tc = jnp.concatenate([tc_ref[...]] * FN_CB, axis=1)
    ts = jnp.concatenate([ts_ref[...]] * FN_CB, axis=1)
    br = ar * tc + ai * ts
    bi = ai * tc - ar * ts
    bst = jnp.concatenate(
        [jnp.concatenate([br[:, m * n2:(m + 1) * n2], bi[:, m * n2:(m + 1) * n2]], axis=1)
         for m in range(FN_CB)], axis=0).astype(BF16)
    res = lax.dot_general(d_ref[...].astype(BF16), bst, (((1,), (1,)), ((), ())),
                          preferred_element_type=F32)
    for m in range(FN_CB):
        o_ref[m] = res[:, m * n1:(m + 1) * n1]


def _fn_fft(yr, yi):
    nm, n1, n2 = yr.shape
    t = n1 * n2
    c, sn = _dft_cos_sin(n1, n1 ** -0.5)
    m = jnp.asarray(np.block([[c, sn], [-sn, c]]), F32)
    k1 = np.arange(n1, dtype=np.int64)[:, None]
    t2 = np.arange(n2, dtype=np.int64)[None, :]
    ang = 2.0 * np.pi * ((k1 * t2) % t).astype(np.float64) / t
    tc = jnp.asarray(np.cos(ang), F32)
    ts = jnp.asarray(np.sin(ang), F32)
    c2, s2 = _dft_cos_sin(n2, n2 ** -0.5)
    dm = jnp.asarray(np.concatenate([c2, s2], axis=1), F32)
    yspec = pl.BlockSpec((FN_CB, n1, n2), lambda i: (i, 0, 0))
    return pl.pallas_call(
        _fn_fft_kernel,
        out_shape=jax.ShapeDtypeStruct((nm, n2, n1), F32),
        grid=(nm // FN_CB,),
        in_specs=[yspec, yspec, _full(m.shape), _full(tc.shape), _full(ts.shape), _full(dm.shape)],
        out_specs=pl.BlockSpec((FN_CB, n2, n1), lambda i: (i, 0, 0)),
        compiler_params=_cparams("arbitrary"),
        name="fourier_fft",
    )(yr, yi, m, tc, ts, dm)


def _fn_out_kernel(ft_ref, w_ref, s_ref, mod_ref, o_ref, a_scr):
    gate = _mod_rows(mod_ref, 1, 0, 0)[2]
    nm, nj, n1 = ft_ref.shape

    def relayout(m, c):
        a_scr[pl.ds(m, nj, stride=_slab_pitch(nm)), :] = ft_ref[m]
        return c

    lax.fori_loop(0, nm, relayout, 0, unroll=8)
    for j in range(nj):
        p0 = j * _slab_pitch(nm)
        slab = a_scr[p0:p0 + nm, :].astype(BF16)
        y = lax.dot_general(slab, w_ref[...], (((0,), (0,)), ((), ())), preferred_element_type=F32)
        rows = slice(j * n1, (j + 1) * n1)
        o_ref[rows, :] = s_ref[rows, :] + gate * y


def _fn_out(ft, s, mods, w_out):
    t, d = s.shape
    nm, n2, n1 = ft.shape
    nj = FN_TB // n1
    tok = pl.BlockSpec((FN_TB, d), lambda i: (i, 0))
    return pl.pallas_call(
        _fn_out_kernel,
        out_shape=jax.ShapeDtypeStruct((t, d), F32),
        grid=(t // FN_TB,),
        in_specs=[pl.BlockSpec((nm, nj, n1), lambda i: (0, i, 0)), _full(w_out.shape), tok,
                  _full(mods.shape)],
        out_specs=tok,
        scratch_shapes=[pltpu.VMEM((nj * _slab_pitch(nm), n1), F32)],
        compiler_params=_cparams("arbitrary"),
        name="fourier_out",
    )(ft, w_out.astype(BF16), s, mods)


def _fourier_layer(s, mods, norm_g, w_in, w_out):
    yr, yi = _fn_proj(s, mods, norm_g, w_in)
    return _fn_out(_fn_fft(yr, yi), s, mods, w_out)


def kernel(x, c, ctx, c_ctx, ada_w, ada_b, norm_mix_g, norm_ffn_g, final_norm_g, router_group_w, router_group_b, router_expert_w, router_expert_b, expert_w_gate, expert_w_up, expert_w_down, cm_w_in, cm_v_norm_g, cm_w_s, cm_b_s, cm_w_out, ml_w_in, ml_conv_w, ml_conv_b, ml_gate_b, ml_norm_g, ml_w_out, lru_w_in, lru_conv_w, lru_conv_b, lru_w_a, lru_b_a, lru_w_x, lru_b_x, lru_lambda, lru_w_out, fn_w_in, fn_w_out):
    bsz, seq, d = x.shape
    assert bsz == 1 and ada_w.shape[0] == 4 and ctx.shape[1] == TM
    c_rows = jnp.concatenate([c_ctx[None, :], c, jnp.zeros((SUBLANES - 2, d), F32)], axis=0)
    mods = _ada_table(c_rows, ada_w, ada_b)
    s = _prep_stream(x[0], ctx[0])

    def moe(s, i, ctx_tiles, final_norm=False):
        return _moe_layer(s, mods[i], norm_ffn_g[i], router_group_w[i], router_group_b[i],
                          router_expert_w[i], router_expert_b[i], expert_w_gate, expert_w_up,
                          expert_w_down, i, ctx_tiles, final_norm_g, final_norm)

    s = _chunk_mlp_layer(s, mods[0], norm_mix_g[0], cm_w_in[0], cm_v_norm_g[0], cm_w_s[0],
                         cm_b_s[0], cm_w_out[0])
    s = moe(s, 0, 1)
    s = _mlstm_layer(s, mods[1], norm_mix_g[1], ml_w_in[0], ml_conv_w[0], ml_conv_b[0],
                     ml_gate_b[0], ml_norm_g[0], ml_w_out[0])
    s = moe(s, 1, 1)
    s = _rglru_layer(s, mods[2], norm_mix_g[2], lru_w_in[0], lru_conv_w[0], lru_conv_b[0],
                     lru_w_a[0], lru_b_a[0], lru_w_x[0], lru_b_x[0], lru_lambda[0], lru_w_out[0])
    s = moe(s, 2, 0)
    s = _fourier_layer(s, mods[3], norm_mix_g[3], fn_w_in[0], fn_w_out[0])
    s = moe(s, 3, 0, final_norm=True)
    return s[None]
```

```python
import functools
import math

import jax
import jax.numpy as jnp
import numpy as np
from jax import lax
from jax.experimental import pallas as pl
from jax.experimental.pallas import tpu as pltpu

F32 = jnp.float32
BF16 = jnp.bfloat16

EPS = 1e-6
POS_BASE = 10000.0
GRID_W = 64
N_MOD = 6
TM = 256
LANES = 128
SUBLANES = 8
VMEM_LIMIT = 56 * 1024 * 1024

CM_CHUNK = 128
CM_GROUPS = 4
ML_HEADS = 4
ML_DK = 128
ML_DV = 256
ML_CHUNK = 128
LRU_HEADS = 10
LRU_BLOCK = 128
LRU_C = 8.0
FN_GROUPS = 4
FFT_N2 = 128
MOE_GROUPS = 4
MOE_EPG = 8
MOE_EXPERTS = MOE_GROUPS * MOE_EPG
MOE_ROWS_LOG2 = 8
MOE_ROWS = 1 << MOE_ROWS_LOG2
ROUTE_ROWS = 40
FN_TB = 1024
FN_CB = 8
CONV_LEFT = 2

HI = lax.Precision.HIGHEST


def _cparams(*sem):
    return pltpu.CompilerParams(dimension_semantics=sem, vmem_limit_bytes=VMEM_LIMIT)


def _full(shape):
    nd = len(shape)
    return pl.BlockSpec(shape, lambda *_: (0,) * nd)


def _rms(x, g):
    return x * lax.rsqrt(jnp.mean(x * x, axis=-1, keepdims=True) + EPS) * g


def _gelu(x):
    c = math.sqrt(2.0 / math.pi)
    return 0.5 * x * (1.0 + jnp.tanh(c * (x + 0.044715 * (x * x * x))))


def _sigmoid(x):
    return 1.0 / (1.0 + jnp.exp(-x))


def _silu(x):
    return x * _sigmoid(x)


def _softplus(x):
    return jnp.maximum(x, 0.0) + jnp.log(1.0 + jnp.exp(-jnp.abs(x)))


def _mod_rows(mod_ref, tile, ctx_tiles, first):
    row = jnp.where(tile < ctx_tiles, 0, 1)
    m = mod_ref[pl.ds(row, 1), :]
    d = m.shape[1] // N_MOD
    return tuple(m[:, (first + j) * d:(first + j + 1) * d] for j in range(3))


def _bdot(a, b):
    return jnp.dot(a.astype(BF16), b.astype(BF16), preferred_element_type=F32)


def _ada_kernel(c_ref, w_ref, b_ref, o_ref):
    c = c_ref[...]
    o_ref[...] = jnp.dot(_silu(c), w_ref[...], precision=HI,
                         preferred_element_type=F32) + b_ref[...]


def _ada_table(c_rows, ada_w, ada_b):
    depth, d, n = ada_w.shape
    tn = 1024
    return pl.pallas_call(
        _ada_kernel,
        out_shape=jax.ShapeDtypeStruct((depth, SUBLANES, n), F32),
        grid=(depth, n // tn),
        in_specs=[_full((SUBLANES, d)),
                  pl.BlockSpec((None, d, tn), lambda i, j: (i, 0, j)),
                  pl.BlockSpec((None, 1, tn), lambda i, j: (i, 0, j))],
        out_specs=pl.BlockSpec((None, SUBLANES, tn), lambda i, j: (i, 0, j)),
        compiler_params=_cparams("arbitrary", "arbitrary"),
        name="ada_table",
    )(c_rows, ada_w, ada_b.reshape(depth, 1, n))


def _prep_kernel(x_ref, ctx_ref, rt_ref, ct_ref, o_ref):
    i = pl.program_id(0)

    @pl.when(i == 0)
    def _():
        o_ref[...] = ctx_ref[...]

    @pl.when(i > 0)
    def _():
        rows_per_tile = TM // GRID_W
        q2 = rt_ref.shape[1]
        r0 = (i - 1) * rows_per_tile
        rt = jnp.concatenate(
            [jnp.broadcast_to(rt_ref[pl.ds(r0 + j, 1), :], (GRID_W, q2))
             for j in range(rows_per_tile)], axis=0)
        ct = jnp.concatenate([ct_ref[...]] * rows_per_tile, axis=0)
        o_ref[...] = x_ref[...] + jnp.concatenate([rt, ct], axis=1)


def _prep_stream(x2, ctx2):
    seq, d = x2.shape
    n_ctx = ctx2.shape[0]
    assert n_ctx == TM and seq % TM == 0 and TM % GRID_W == 0
    q = d // 4
    freq = jnp.exp(-math.log(POS_BASE) * jnp.arange(q, dtype=F32) / q)
    ar = jnp.arange(seq // GRID_W, dtype=F32)[:, None] * freq
    ac = jnp.arange(GRID_W, dtype=F32)[:, None] * freq
    rt = jnp.concatenate([jnp.sin(ar), jnp.cos(ar)], axis=-1)
    ct = jnp.concatenate([jnp.sin(ac), jnp.cos(ac)], axis=-1)
    nt = 1 + seq // TM
    return pl.pallas_call(
        _prep_kernel,
        out_shape=jax.ShapeDtypeStruct((n_ctx + seq, d), F32),
        grid=(nt,),
        in_specs=[pl.BlockSpec((TM, d), lambda i: (jnp.maximum(i - 1, 0), 0)),
                  _full((TM, d)), _full(rt.shape), _full(ct.shape)],
        out_specs=pl.BlockSpec((TM, d), lambda i: (i, 0)),
        compiler_params=_cparams("arbitrary"),
        name="prep_stream",
    )(x2, ctx2, rt, ct)


def _cm_kernel(s_ref, mod_ref, g_ref, win_ref, vg_ref, ws_ref, bs_ref, wout_ref, o_ref, p_scr):
    i = pl.program_id(0)
    shift, scale, gate = _mod_rows(mod_ref, i, 1, 0)
    x = s_ref[...]
    h = _rms(x, g_ref[...]) * (1.0 + scale) + shift
    z = _gelu(_bdot(h, win_ref[...]))
    w = z.shape[1] // 2
    u = z[:, :w]
    v = _rms(z[:, w:], vg_ref[...]).astype(BF16)
    gw = w // CM_GROUPS
    for c in range(TM // CM_CHUNK):
        r = slice(c * CM_CHUNK, (c + 1) * CM_CHUNK)
        for g in range(CM_GROUPS):
            cs = slice(g * gw, (g + 1) * gw)
            s = jnp.dot(ws_ref[g], v[r, cs], preferred_element_type=F32) + bs_ref[:, g:g + 1]
            p_scr[r, cs] = (u[r, cs] * s).astype(BF16)
    y = jnp.dot(p_scr[...], wout_ref[...], preferred_element_type=F32)
    o_ref[...] = x + gate * y


def _chunk_mlp_layer(s, mods, norm_g, w_in, v_g, w_s, b_s, w_out):
    n, d = s.shape
    w = w_out.shape[0]
    return pl.pallas_call(
        _cm_kernel,
        out_shape=jax.ShapeDtypeStruct((n, d), F32),
        grid=(n // TM,),
        in_specs=[pl.BlockSpec((TM, d), lambda i: (i, 0)),
                  _full(mods.shape), _full((1, d)), _full(w_in.shape), _full((1, w)),
                  _full(w_s.shape), _full((CM_CHUNK, CM_GROUPS)), _full(w_out.shape)],
        out_specs=pl.BlockSpec((TM, d), lambda i: (i, 0)),
        scratch_shapes=[pltpu.VMEM((TM, w), BF16)],
        compiler_params=_cparams("arbitrary"),
        name="chunk_mlp",
    )(s, mods, norm_g.reshape(1, d), w_in.astype(BF16), v_g.reshape(1, w),
      w_s.astype(BF16), b_s.T, w_out.astype(BF16))


def _store_token_tiles(ref, x):
    rows, d = x.shape
    for j in range(d // LANES):
        ref[pl.ds(j, rows, stride=d // LANES), :] = x[:, j * LANES:(j + 1) * LANES]


def _load_token_tiles(ref):
    chunks = SUBLANES
    rows = ref.shape[0] // chunks
    return jnp.concatenate([ref[pl.ds(j, rows, stride=chunks), :] for j in range(chunks)], axis=1)


def _router_kernel(s_ref, mod_ref, g_ref, rwt_ref, rbt_ref, h_ref, e1_ref, e2_ref, r1_ref, r2_ref,
                   wt_ref, cnt_ref, carry, *, ctx_tiles):
    i = pl.program_id(0)

    @pl.when(i == 0)
    def _():
        carry[...] = jnp.zeros_like(carry)

    shift, scale, _ = _mod_rows(mod_ref, i, ctx_tiles, 3)
    h = _rms(s_ref[...], g_ref[...]) * (1.0 + scale) + shift
    _store_token_tiles(h_ref, h)
    logits = lax.dot_general(rwt_ref[...], h, (((1,), (1,)), ((), ())), precision=HI,
                             preferred_element_type=F32) + rbt_ref[...]
    row = lax.broadcasted_iota(jnp.int32, logits.shape, 0)
    neg = jnp.float32(-jnp.inf)
    big = jnp.int32(1 << 20)
    is_g = row < MOE_GROUPS
    gl = jnp.where(is_g, logits, neg)
    gmax = jnp.max(gl, axis=0, keepdims=True)
    grp = jnp.min(jnp.where(is_g & (gl == gmax), row, big), axis=0, keepdims=True)
    p_grp = 1.0 / jnp.sum(jnp.exp(gl - gmax), axis=0, keepdims=True)
    e_row = row - MOE_GROUPS
    in_grp = (e_row >= 0) & (e_row < MOE_EXPERTS) & ((e_row >> 3) == grp)
    l1 = jnp.where(in_grp, logits, neg)
    v1 = jnp.max(l1, axis=0, keepdims=True)
    i1 = jnp.min(jnp.where(in_grp & (l1 == v1), row, big), axis=0, keepdims=True)
    rest = in_grp & (row != i1)
    l2 = jnp.where(rest, logits, neg)
    v2 = jnp.max(l2, axis=0, keepdims=True)
    i2 = jnp.min(jnp.where(rest & (l2 == v2), row, big), axis=0, keepdims=True)
    e21 = jnp.exp(v2 - v1)
    w1 = p_grp / (1.0 + e21)
    w2 = p_grp * e21 / (1.0 + e21)
    oh1 = (row == i1).astype(F32)
    oh2 = (row == i2).astype(F32)
    oh = oh1 + oh2
    ri = lax.broadcasted_iota(jnp.int32, (TM, TM), 0)
    ci = lax.broadcasted_iota(jnp.int32, (TM, TM), 1)
    earlier = (ri < ci).astype(BF16)
    before = jnp.dot(oh.astype(BF16), earlier, preferred_element_type=F32) + carry[:, 0:1]
    r1_ref[...] = jnp.sum(oh1 * before, axis=0, keepdims=True).astype(jnp.int32)
    r2_ref[...] = jnp.sum(oh2 * before, axis=0, keepdims=True).astype(jnp.int32)
    e1_ref[...] = i1 - MOE_GROUPS
    e2_ref[...] = i2 - MOE_GROUPS
    carry[...] = carry[...] + jnp.sum(oh, axis=1, keepdims=True)
    cnt_ref[...] = carry[...]
    wt_ref[...] = jnp.concatenate([w1, w2, jnp.zeros((LANES - 2, TM), F32)], axis=0).T


def _router(s, mods, norm_g, rg_w, rg_b, re_w, re_b, ctx_tiles):
    n, d = s.shape
    nt = n // TM
    pad = ROUTE_ROWS - MOE_GROUPS - MOE_EXPERTS
    rwt = jnp.concatenate([rg_w, re_w, jnp.zeros((d, pad), F32)], axis=1).T
    rbt = jnp.broadcast_to(jnp.concatenate([rg_b, re_b, jnp.zeros((pad,), F32)])[:, None],
                           (ROUTE_ROWS, TM))
    assert d == SUBLANES * LANES
    tile = pl.BlockSpec((TM, d), lambda i: (i, 0))
    irow = pl.BlockSpec((None, 1, TM), lambda i: (i, 0, 0))
    ishape = jax.ShapeDtypeStruct((nt, 1, TM), jnp.int32)
    return pl.pallas_call(
        functools.partial(_router_kernel, ctx_tiles=ctx_tiles),
        out_shape=(jax.ShapeDtypeStruct((n * SUBLANES, LANES), F32), ishape, ishape, ishape, ishape,
                   jax.ShapeDtypeStruct((n, LANES), F32),
                   jax.ShapeDtypeStruct((ROUTE_ROWS, LANES), F32)),
        grid=(nt,),
        in_specs=[tile, _full(mods.shape), _full((1, d)), _full((ROUTE_ROWS, d)),
                  _full((ROUTE_ROWS, TM))],
        out_specs=(pl.BlockSpec((TM * SUBLANES, LANES), lambda i: (i, 0)), irow, irow, irow, irow,
                   pl.BlockSpec((TM, LANES), lambda i: (i, 0)), _full((ROUTE_ROWS, LANES))),
        scratch_shapes=[pltpu.VMEM((ROUTE_ROWS, LANES), F32)],
        compiler_params=_cparams("arbitrary"),
        name="moe_router",
    )(s, mods, norm_g.reshape(1, d), rwt, rbt)


def _finalize_kernel(cnt_ref, e1_ref, e2_ref, r1_ref, r2_ref, d1_ref, d2_ref, blk_ref):
    e1 = e1_ref[...]
    e2 = e2_ref[...]
    r1 = r1_ref[...]
    r2 = r2_ref[...]
    d1 = jnp.zeros_like(e1)
    d2 = jnp.zeros_like(e2)
    lane = lax.broadcasted_iota(jnp.int32, blk_ref.shape, 1)
    brow = lane * MOE_ROWS
    sub = lax.broadcasted_iota(jnp.int32, blk_ref.shape, 0)
    be = jnp.zeros(blk_ref.shape, jnp.int32)
    pend = jnp.zeros(blk_ref.shape, jnp.int32)
    ps = jnp.int32(0)
    for e in range(MOE_EXPERTS):
        c = cnt_ref[e]
        pe = ps + lax.shift_left(lax.shift_right_logical(c + (MOE_ROWS - 1), MOE_ROWS_LOG2),
                                 MOE_ROWS_LOG2)
        d1 = jnp.where(e1 == e, ps + r1, d1)
        d2 = jnp.where(e2 == e, ps + r2, d2)
        be = be + (brow >= pe).astype(jnp.int32)
        pend = jnp.where(lane == e, pe, pend)
        ps = pe
    d1_ref[...] = d1
    d2_ref[...] = d2
    n_used = lax.shift_right_logical(ps, MOE_ROWS_LOG2)
    blk_ref[...] = jnp.where(sub == 0, jnp.minimum(be, MOE_EXPERTS - 1),
                             jnp.where(sub == 1, pend, n_used))


def _finalize(counts, e1, e2, r1, r2, nb):
    nbp = (nb + LANES - 1) // LANES * LANES
    whole = pl.BlockSpec(e1.shape, lambda i, c: (0, 0, 0))
    ishape = jax.ShapeDtypeStruct(e1.shape, jnp.int32)
    return pl.pallas_call(
        _finalize_kernel,
        out_shape=(ishape, ishape, jax.ShapeDtypeStruct((SUBLANES, nbp), jnp.int32)),
        grid_spec=pltpu.PrefetchScalarGridSpec(
            num_scalar_prefetch=1,
            grid=(1,),
            in_specs=[whole, whole, whole, whole],
            out_specs=(whole, whole, pl.BlockSpec((SUBLANES, nbp), lambda i, c: (0, 0)))),
        compiler_params=_cparams("arbitrary"),
        name="moe_finalize",
    )(counts, e1, e2, r1, r2)


def _token_copy(src, r, dst, d, sem):
    return pltpu.make_async_copy(src.at[pl.ds(pl.multiple_of(r * SUBLANES, SUBLANES), SUBLANES), :],
                                 dst.at[pl.ds(pl.multiple_of(d * SUBLANES, SUBLANES), SUBLANES), :],
                                 sem)


def _zero_fill_padding(pend_ref, nu_ref, xs_out, zbuf, zsem):
    blk_rows = MOE_ROWS * SUBLANES
    nb = xs_out.shape[0] // blk_rows
    zbuf[...] = jnp.zeros_like(zbuf)

    def block_copy(b):
        r0 = pl.multiple_of(b * blk_rows, blk_rows)
        return pltpu.make_async_copy(zbuf, xs_out.at[pl.ds(r0, blk_rows), :], zsem)

    def seg_last_block(e):
        pe = pend_ref[e]
        prev = pend_ref[e - 1] if e > 0 else 0
        return pe > prev, lax.shift_right_logical(pe, MOE_ROWS_LOG2) - 1

    for e in range(MOE_EXPERTS):
        nonempty, b = seg_last_block(e)

        @pl.when(nonempty)
        def _():
            block_copy(b).start()

    def tail_start(b, c):
        block_copy(b).start()
        return c

    lax.fori_loop(nu_ref[0], nb, tail_start, 0)
    for e in range(MOE_EXPERTS):
        nonempty, b = seg_last_block(e)

        @pl.when(nonempty)
        def _():
            block_copy(b).wait()

    def tail_wait(b, c):
        block_copy(b).wait()
        return c

    lax.fori_loop(nu_ref[0], nb, tail_wait, 0)


def _dispatch_kernel(dest_ref, pend_ref, nu_ref, h_ref, xs_out, sem, zbuf, zsem, *, tiles):
    i = pl.program_id(0)

    @pl.when(i == 0)
    def _():
        _zero_fill_padding(pend_ref, nu_ref, xs_out, zbuf, zsem)

    for q in range(tiles):
        base = (i * tiles + q) * (2 * TM)

        def start(r, c):
            _token_copy(h_ref, q * TM + r, xs_out, dest_ref[base + r], sem).start(priority=0)
            _token_copy(h_ref, q * TM + r, xs_out, dest_ref[base + TM + r], sem).start(priority=1)
            return c

        lax.fori_loop(0, TM, start, 0, unroll=8)
    for _ in range(2):
        pltpu.make_async_copy(h_ref, xs_out.at[pl.ds(0, tiles * TM * SUBLANES), :], sem).wait()


def _dispatch(dest, pad_end, n_used, h, n_rows):
    n = h.shape[0] // SUBLANES
    nt = n // TM
    tiles = next(k for k in (5, 4, 2, 1) if nt % k == 0)
    return pl.pallas_call(
        functools.partial(_dispatch_kernel, tiles=tiles),
        out_shape=jax.ShapeDtypeStruct((n_rows * SUBLANES, LANES), F32),
        grid_spec=pltpu.PrefetchScalarGridSpec(
            num_scalar_prefetch=3,
            grid=(nt // tiles,),
            in_specs=[pl.BlockSpec((tiles * TM * SUBLANES, LANES), lambda i, *_: (i, 0))],
            out_specs=pl.BlockSpec(memory_space=pl.ANY),
            scratch_shapes=[pltpu.SemaphoreType.DMA, pltpu.VMEM((MOE_ROWS * SUBLANES, LANES), F32),
                            pltpu.SemaphoreType.DMA]),
        compiler_params=_cparams("arbitrary"),
        name="moe_dispatch",
    )(dest, pad_end, n_used, h)


X_SLOTS = 3


def _expert_kernel(be_ref, pend_ref, nu_ref, xs_hbm, wg_hbm, wu_hbm, wd_hbm, y_ref,
                   xbuf, wg_f, wu_f, wd_f, wg_s, wu_s, wd_s, xsem, wsem, ord_ref, *, layer):
    b = pl.program_id(0)
    nu = nu_ref[0]
    blk_rows = MOE_ROWS * SUBLANES

    def x_copy(blk, slot):
        r0 = pl.multiple_of(blk * blk_rows, blk_rows)
        return pltpu.make_async_copy(xs_hbm.at[pl.ds(r0, blk_rows), :], xbuf.at[slot],
                                     xsem.at[slot])

    def w_copies(e, slot):
        return (pltpu.make_async_copy(wg_hbm.at[layer, e], wg_f.at[slot], wsem.at[slot]),
                pltpu.make_async_copy(wu_hbm.at[layer, e], wu_f.at[slot], wsem.at[slot]),
                pltpu.make_async_copy(wd_hbm.at[layer, e], wd_f.at[slot], wsem.at[slot]))

    @pl.when(b == 0)
    def _():
        ord_ref[0] = 0
        for j in range(X_SLOTS - 1):
            @pl.when(j < nu)
            def _():
                x_copy(j, j).start()

        @pl.when(nu > 0)
        def _():
            for c in w_copies(be_ref[0], 0):
                c.start()

    ahead = b + (X_SLOTS - 1)

    @pl.when(ahead < nu)
    def _():
        x_copy(ahead, lax.rem(ahead, X_SLOTS)).start()

    used = b < nu
    e = be_ref[b]
    fresh = used & ((b == 0) | (e != be_ref[jnp.maximum(b - 1, 0)]))

    @pl.when(fresh)
    def _():
        k = ord_ref[0]
        slot = lax.rem(k, 2)
        for c in w_copies(e, slot):
            c.wait()
        wg_s[...] = wg_f[slot].astype(BF16)
        wu_s[...] = wu_f[slot].astype(BF16)
        wd_s[...] = wd_f[slot].astype(BF16)
        nxt = lax.shift_right_logical(pend_ref[e], MOE_ROWS_LOG2)

        @pl.when(nxt < nu)
        def _():
            for c in w_copies(be_ref[nxt], 1 - slot):
                c.start()

        ord_ref[0] = k + 1

    @pl.when(used)
    def _():
        slot = lax.rem(b, X_SLOTS)
        x_copy(b, slot).wait()
        x = _load_token_tiles(xbuf.at[slot]).astype(BF16)
        a = jnp.dot(x, wg_s[...], preferred_element_type=F32)
        u = jnp.dot(x, wu_s[...], preferred_element_type=F32)
        _store_token_tiles(y_ref, jnp.dot((_silu(a) * u).astype(BF16), wd_s[...],
                                          preferred_element_type=F32))

    @pl.when(jnp.logical_not(used))
    def _():
        y_ref[...] = jnp.zeros_like(y_ref)


def _experts(blk_expert, pad_end, n_used, xs, w_gate, w_up, w_down, layer):
    d, hid = w_gate.shape[2:]
    blk_rows = MOE_ROWS * SUBLANES
    nb = xs.shape[0] // blk_rows
    hbm = pl.BlockSpec(memory_space=pl.ANY)
    return pl.pallas_call(
        functools.partial(_expert_kernel, layer=layer),
        out_shape=jax.ShapeDtypeStruct(xs.shape, F32),
        grid_spec=pltpu.PrefetchScalarGridSpec(
            num_scalar_prefetch=3,
            grid=(nb,),
            in_specs=[hbm, hbm, hbm, hbm],
            out_specs=pl.BlockSpec((blk_rows, LANES), lambda b, *_: (b, 0)),
            scratch_shapes=[pltpu.VMEM((X_SLOTS, blk_rows, LANES), F32),
                            pltpu.VMEM((2, d, hid), F32), pltpu.VMEM((2, d, hid), F32),
                            pltpu.VMEM((2, hid, d), F32),
                            pltpu.VMEM((d, hid), BF16), pltpu.VMEM((d, hid), BF16),
                            pltpu.VMEM((hid, d), BF16),
                            pltpu.SemaphoreType.DMA((X_SLOTS,)), pltpu.SemaphoreType.DMA((2,)),
                            pltpu.SMEM((1,), jnp.int32)]),
        compiler_params=_cparams("arbitrary"),
        name="moe_experts",
    )(blk_expert, pad_end, n_used, xs, w_gate, w_up, w_down)


def _combine_kernel(dest_ref, s_ref, wt_ref, mod_ref, fg_ref, ys_ref, o_ref, ybuf, sem,
                    *, ctx_tiles, final_norm):
    i = pl.program_id(0)
    slot = i % 2

    def gather(tile, slot):
        base = tile * (2 * TM)

        def start(r, c):
            _token_copy(ys_ref, dest_ref[base + r], ybuf.at[slot, 0], r,
                        sem.at[slot]).start(priority=0)
            _token_copy(ys_ref, dest_ref[base + TM + r], ybuf.at[slot, 1], r,
                        sem.at[slot]).start(priority=1)
            return c

        lax.fori_loop(0, TM, start, 0, unroll=8)

    @pl.when(i == 0)
    def _():
        gather(0, 0)

    @pl.when(i + 1 < pl.num_programs(0))
    def _():
        gather(i + 1, 1 - slot)

    for k in range(2):
        pltpu.make_async_copy(ys_ref.at[pl.ds(0, TM * SUBLANES), :], ybuf.at[slot, k],
                              sem.at[slot]).wait()
    gate = _mod_rows(mod_ref, i, ctx_tiles, 3)[2]
    wt = wt_ref[...]
    y = (wt[:, 0:1] * _load_token_tiles(ybuf.at[slot, 0])
         + wt[:, 1:2] * _load_token_tiles(ybuf.at[slot, 1]))
    out = s_ref[...] + gate * y
    if final_norm:
        out = _rms(out, fg_ref[...])
    o_ref[...] = out


def _combine(dest, s, wts, mods, final_g, ys, ctx_tiles, final_norm):
    n, d = s.shape
    return pl.pallas_call(
        functools.partial(_combine_kernel, ctx_tiles=ctx_tiles, final_norm=final_norm),
        out_shape=jax.ShapeDtypeStruct((n, d), F32),
        grid_spec=pltpu.PrefetchScalarGridSpec(
            num_scalar_prefetch=1,
            grid=(n // TM,),
            in_specs=[pl.BlockSpec((TM, d), lambda i, dst: (i, 0)),
                      pl.BlockSpec((TM, LANES), lambda i, dst: (i, 0)),
                      pl.BlockSpec(mods.shape, lambda i, dst: (0, 0)),
                      pl.BlockSpec((1, d), lambda i, dst: (0, 0)),
                      pl.BlockSpec(memory_space=pl.ANY)],
            out_specs=pl.BlockSpec((TM, d), lambda i, dst: (i, 0)),
            scratch_shapes=[pltpu.VMEM((2, 2, TM * SUBLANES, LANES), F32),
                            pltpu.SemaphoreType.DMA((2,))]),
        compiler_params=_cparams("arbitrary"),
        name="moe_combine",
    )(dest, s, wts, mods, final_g.reshape(1, d), ys)


def _moe_layer(s, mods, norm_g, rg_w, rg_b, re_w, re_b, w_gate, w_up, w_down, layer, ctx_tiles,
               final_g, final_norm):
    n, d = s.shape
    h, e1, e2, r1, r2, wts, cnt = _router(s, mods, norm_g, rg_w, rg_b, re_w, re_b, ctx_tiles)
    counts = cnt[MOE_GROUPS:MOE_GROUPS + MOE_EXPERTS, 0].astype(jnp.int32)
    nb = (2 * n + MOE_EXPERTS * (MOE_ROWS - 1)) // MOE_ROWS + 1
    d1, d2, blk = _finalize(counts, e1, e2, r1, r2, nb)
    dest = jnp.concatenate([d1, d2], axis=1).reshape(2 * n)
    n_used = blk[2, :1]
    pad_end = blk[1, :MOE_EXPERTS]
    xs = _dispatch(dest, pad_end, n_used, h, nb * MOE_ROWS)
    ys = _experts(blk[0, :nb], pad_end, n_used, xs, w_gate, w_up, w_down, layer)
    return _combine(dest, s, wts, mods, final_g, ys, ctx_tiles, final_norm)


def _conv_tile(x, prev_ref, next_ref, has_prev, has_next, w_ref, b_ref):
    rows = x.shape[0]
    ridx = lax.broadcasted_iota(jnp.int32, x.shape, 0)
    pm = jnp.where(has_prev, 1.0, 0.0)
    nm = jnp.where(has_next, 1.0, 0.0)
    p2 = prev_ref[SUBLANES - 2:SUBLANES - 1, :] * pm
    p1 = prev_ref[SUBLANES - 1:SUBLANES, :] * pm
    n1 = next_ref[0:1, :] * nm
    xm1 = jnp.where(ridx == 0, p1, pltpu.roll(x, 1, axis=0))
    xm2 = jnp.where(ridx == 0, p2, jnp.where(ridx == 1, p1, pltpu.roll(x, 2, axis=0)))
    xp1 = jnp.where(ridx == rows - 1, n1, pltpu.roll(x, rows - 1, axis=0))
    return (xm2 * w_ref[0:1, :] + xm1 * w_ref[1:2, :] + x * w_ref[2:3, :]
            + xp1 * w_ref[3:4, :] + b_ref[...])


def _ml_proj_kernel(s_ref, mod_ref, g_ref, w_ref, wg_ref, gb_ref, qk_ref, v_ref, o_ref, gt_ref):
    i = pl.program_id(0)
    shift, scale, _ = _mod_rows(mod_ref, i, 1, 0)
    h = _rms(s_ref[...], g_ref[...]) * (1.0 + scale) + shift
    z = _bdot(h, w_ref[...])
    nqk = qk_ref.shape[1]
    nv = v_ref.shape[1]
    qk_ref[...] = z[:, :nqk]
    v_ref[...] = z[:, nqk:nqk + nv].astype(BF16)
    o_ref[...] = z[:, nqk + nv:].astype(BF16)
    pre = jnp.dot(h, wg_ref[...], precision=HI, preferred_element_type=F32) + gb_ref[...]
    lane = lax.broadcasted_iota(jnp.int32, pre.shape, 1)
    is_forget = ((lane >> 2) & 1) == 1
    gt_ref[...] = jnp.where(is_forget, -_softplus(-pre), pre)


def _ml_proj(s, mods, norm_g, w_in, gate_b):
    n, d = s.shape
    nqk = 2 * ML_HEADS * ML_DK
    nv = ML_HEADS * ML_DV
    n_main = nqk + 2 * nv
    n_gate = w_in.shape[1] - n_main
    w_main = w_in[:, :n_main].astype(BF16)
    w_gate = jnp.concatenate([w_in[:, n_main:], jnp.zeros((d, LANES - n_gate), F32)], axis=1)
    gb = jnp.concatenate([gate_b.reshape(n_gate), jnp.zeros((LANES - n_gate,), F32)]).reshape(1, LANES)
    tile = lambda w: pl.BlockSpec((TM, w), lambda i: (i, 0))
    return pl.pallas_call(
        _ml_proj_kernel,
        out_shape=(jax.ShapeDtypeStruct((n, nqk), F32), jax.ShapeDtypeStruct((n, nv), BF16),
                   jax.ShapeDtypeStruct((n, nv), BF16), jax.ShapeDtypeStruct((n, LANES), F32)),
        grid=(n // TM,),
        in_specs=[tile(d), _full(mods.shape), _full((1, d)), _full(w_main.shape),
                  _full((d, LANES)), _full((1, LANES))],
        out_specs=(tile(nqk), tile(nv), tile(nv), tile(LANES)),
        compiler_params=_cparams("arbitrary"),
        name="mlstm_proj",
    )(s, mods, norm_g.reshape(1, d), w_main, w_gate, gb)


def _ml_chunk_index(j, n_chunks, ctx_chunks, reverse):
    if not reverse:
        return j
    return jnp.where(j < ctx_chunks, ctx_chunks - 1 - j, n_chunks - 1 + ctx_chunks - j)


def _ml_rec_kernel(qk_ref, qkp_ref, qkn_ref, v_ref, gt_ref, gtt_ref, cw_ref, cb_ref, o_ref,
                   c_scr, n_scr, m_scr, *, reverse, n_chunks, ctx_chunks):
    j = pl.program_id(0)
    c = _ml_chunk_index(j, n_chunks, ctx_chunks, reverse)

    @pl.when(j == 0)
    def _():
        c_scr[...] = jnp.zeros_like(c_scr)
        n_scr[...] = jnp.zeros_like(n_scr)
        m_scr[...] = jnp.zeros_like(m_scr)

    has_prev = (c != 0) & (c != ctx_chunks)
    has_next = (c != ctx_chunks - 1) & (c != n_chunks - 1)
    qk = _silu(_conv_tile(qk_ref[...], qkp_ref, qkn_ref, has_prev, has_next, cw_ref, cb_ref))
    L = ML_CHUNK
    ri = lax.broadcasted_iota(jnp.int32, (L, L), 0)
    ci = lax.broadcasted_iota(jnp.int32, (L, L), 1)
    past = (ci >= ri) if reverse else (ci <= ri)
    pastf = past.astype(F32)
    gt = gt_ref[...]
    gtt = gtt_ref[...]
    b_col = jnp.dot(pastf, gt, precision=HI, preferred_element_type=F32)
    b_row = jnp.dot(gtt, pastf.T, precision=HI, preferred_element_type=F32)
    last = 0 if reverse else L - 1
    dbase = 8 if reverse else 0
    nq = ML_HEADS * ML_DK
    for hd in range(ML_HEADS):
        cl = dbase + hd
        cf = dbase + 4 + hd
        q = qk[:, hd * ML_DK:(hd + 1) * ML_DK] * (ML_DK ** -0.5)
        k = qk[:, nq + hd * ML_DK:nq + (hd + 1) * ML_DK]
        v = v_ref[:, hd * ML_DV:(hd + 1) * ML_DV]
        li_c = gt[:, cl:cl + 1]
        li_r = gtt[cl:cl + 1, :]
        b_c = b_col[:, cf:cf + 1]
        b_r = b_row[cf:cf + 1, :]
        g = b_r[:, last:last + 1]
        m0 = m_scr[hd:hd + 1, 0:1]
        c0 = c_scr[hd]
        n0 = n_scr[hd:hd + 1, :]
        a_c = g - b_c + li_c
        a_r = g - b_r + li_r
        m_loc = jnp.max(a_r, axis=-1, keepdims=True)
        inter = b_c + m0
        dlog = jnp.where(past, b_c - b_r + li_r, -jnp.inf)
        m = jnp.maximum(inter, jnp.max(dlog, axis=-1, keepdims=True))
        qb = q.astype(BF16)
        sc = lax.dot_general(qb, k.astype(BF16), (((1,), (1,)), ((), ())),
                             preferred_element_type=F32) * jnp.exp(dlog - m)
        w_inter = jnp.exp(inter - m)
        num = (jnp.dot(sc.astype(BF16), v, preferred_element_type=F32)
               + w_inter * jnp.dot(qb, c0.astype(BF16), preferred_element_type=F32))
        den = (jnp.sum(sc, axis=-1, keepdims=True)
               + w_inter * jnp.sum(q * n0, axis=-1, keepdims=True))
        o_ref[:, hd * ML_DV:(hd + 1) * ML_DV] = (
            num / jnp.maximum(jnp.abs(den), jnp.exp(-m))).astype(BF16)
        m_new = jnp.maximum(g + m0, m_loc)
        dec = jnp.exp(g + m0 - m_new)
        scl = jnp.exp(m_loc - m_new)
        kw = k * jnp.exp(a_c - m_loc)
        c_scr[hd] = dec * c0 + scl * jnp.dot(kw.T.astype(BF16), v, preferred_element_type=F32)
        n_scr[hd:hd + 1, :] = dec * n0 + scl * jnp.sum(kw, axis=0, keepdims=True)
        m_scr[hd:hd + 1, :] = jnp.broadcast_to(m_new, (1, LANES))


def _ml_rec(qk, v, gt, gtt, conv_w, conv_b, reverse):
    n, nqk = qk.shape
    nv = v.shape[1]
    L = ML_CHUNK
    nc = n // L
    cc = TM // L
    hb = L // SUBLANES
    idx = lambda j: _ml_chunk_index(j, nc, cc, reverse)
    last8 = n // SUBLANES - 1
    return pl.pallas_call(
        functools.partial(_ml_rec_kernel, reverse=reverse, n_chunks=nc, ctx_chunks=cc),
        out_shape=jax.ShapeDtypeStruct((n, nv), BF16),
        grid=(nc,),
        in_specs=[pl.BlockSpec((L, nqk), lambda j: (idx(j), 0)),
                  pl.BlockSpec((SUBLANES, nqk), lambda j: (jnp.maximum(idx(j) * hb - 1, 0), 0)),
                  pl.BlockSpec((SUBLANES, nqk), lambda j: (jnp.minimum((idx(j) + 1) * hb, last8), 0)),
                  pl.BlockSpec((L, nv), lambda j: (idx(j), 0)),
                  pl.BlockSpec((L, LANES), lambda j: (idx(j), 0)),
                  pl.BlockSpec((2 * SUBLANES, L), lambda j: (0, idx(j))),
                  _full((4, nqk)), _full((1, nqk))],
        out_specs=pl.BlockSpec((L, nv), lambda j: (idx(j), 0)),
        scratch_shapes=[pltpu.VMEM((ML_HEADS, ML_DK, ML_DV), F32),
                        pltpu.VMEM((SUBLANES, ML_DK), F32),
                        pltpu.VMEM((SUBLANES, LANES), F32)],
        compiler_params=_cparams("arbitrary"),
        name="mlstm_rev" if reverse else "mlstm_fwd",
    )(qk, qk, qk, v, gt, gtt, conv_w, conv_b.reshape(1, nqk))


def _ml_out_kernel(hf_ref, hr_ref, o_ref, s_ref, mod_ref, ng_ref, w_ref, out_ref, p_scr):
    i = pl.program_id(0)
    gate = _mod_rows(mod_ref, i, 1, 0)[2]
    hs = hf_ref[...].astype(F32) + hr_ref[...].astype(F32)
    sig = _sigmoid(o_ref[...].astype(F32))
    ng = ng_ref[...]
    for hd in range(ML_HEADS):
        cs = slice(hd * ML_DV, (hd + 1) * ML_DV)
        seg = hs[:, cs]
        hn = seg * lax.rsqrt(jnp.mean(seg * seg, axis=-1, keepdims=True) + EPS) * ng[:, cs]
        p_scr[:, cs] = (hn * sig[:, cs]).astype(BF16)
    y = jnp.dot(p_scr[...], w_ref[...], preferred_element_type=F32)
    out_ref[...] = s_ref[...] + gate * y


def _ml_out(hf, hr, o, s, mods, norm_g, w_out):
    n, d = s.shape
    nv = hf.shape[1]
    tile = lambda w: pl.BlockSpec((TM, w), lambda i: (i, 0))
    return pl.pallas_call(
        _ml_out_kernel,
        out_shape=jax.ShapeDtypeStruct((n, d), F32),
        grid=(n // TM,),
        in_specs=[tile(nv), tile(nv), tile(nv), tile(d), _full(mods.shape), _full((1, nv)),
                  _full(w_out.shape)],
        out_specs=tile(d),
        scratch_shapes=[pltpu.VMEM((TM, nv), BF16)],
        compiler_params=_cparams("arbitrary"),
        name="mlstm_out",
    )(hf, hr, o, s, mods, norm_g.reshape(1, nv), w_out.astype(BF16))


def _mlstm_layer(s, mods, norm_g, w_in, conv_w, conv_b, gate_b, ml_norm_g, w_out):
    qk, v, o, gt = _ml_proj(s, mods, norm_g, w_in, gate_b)
    gtt = gt[:, :2 * SUBLANES].T
    hf = _ml_rec(qk, v, gt, gtt, conv_w, conv_b, False)
    hr = _ml_rec(qk, v, gt, gtt, conv_w, conv_b, True)
    return _ml_out(hf, hr, o, s, mods, ml_norm_g, w_out)


def _lru_proj_kernel(s_ref, mod_ref, g_ref, w_ref, gl_ref, xr_ref):
    i = pl.program_id(0)
    shift, scale, _ = _mod_rows(mod_ref, i, 1, 0)
    h = _rms(s_ref[...], g_ref[...]) * (1.0 + scale) + shift
    z = _bdot(h, w_ref[...])
    w = gl_ref.shape[1]
    gl_ref[...] = _gelu(z[:, :w]).astype(BF16)
    xr_ref[...] = z[:, w:]


def _lru_proj(s, mods, norm_g, w_in):
    n, d = s.shape
    w = w_in.shape[1] // 2
    tile = lambda c: pl.BlockSpec((TM, c), lambda i: (i, 0))
    return pl.pallas_call(
        _lru_proj_kernel,
        out_shape=(jax.ShapeDtypeStruct((n, w), BF16), jax.ShapeDtypeStruct((n, w), F32)),
        grid=(n // TM,),
        in_specs=[tile(d), _full(mods.shape), _full((1, d)), _full(w_in.shape)],
        out_specs=(tile(w), tile(w)),
        compiler_params=_cparams("arbitrary"),
        name="rglru_proj",
    )(s, mods, norm_g.reshape(1, d), w_in.astype(BF16))


def _lru_tile_index(j, n_tiles, reverse):
    if not reverse:
        return j
    return jnp.where(j == 0, 0, n_tiles - j)


def _lru_scan_kernel(x_ref, xp_ref, xn_ref, cw_ref, cb_ref, wg_ref, ba_ref, bx_ref, lam_ref,
                     o_ref, a_scr, u_scr, carry, *, reverse, n_tiles):
    j = pl.program_id(0)
    t = _lru_tile_index(j, n_tiles, reverse)

    @pl.when(j == 0)
    def _():
        carry[...] = jnp.zeros_like(carry)

    has_prev = t > 1
    has_next = (t != 0) & (t != n_tiles - 1)
    xr = _conv_tile(x_ref[...], xp_ref, xn_ref, has_prev, has_next, cw_ref, cb_ref)
    sp = _softplus(-lam_ref[...])
    B = LRU_BLOCK
    for hd in range(LRU_HEADS):
        cs = slice(hd * B, (hd + 1) * B)
        xh = xr[:, cs]
        y = jnp.dot(xh.astype(BF16), wg_ref[hd], preferred_element_type=F32)
        r = _sigmoid(y[:, :B] + ba_ref[:, cs])
        ig = _sigmoid(y[:, B:] + bx_ref[:, cs])
        log_a = -LRU_C * r * sp[:, cs]
        a = jnp.exp(log_a)
        a_scr[:, cs] = a
        u_scr[:, cs] = jnp.sqrt(1.0 - a * a) * (ig * xh)

    S = SUBLANES
    w = a_scr.shape[1]
    sidx = lax.broadcasted_iota(jnp.int32, (S, w), 0)

    def group(gi, c):
        g = (TM // S - 1 - gi) if reverse else gi
        r0 = pl.multiple_of(g * S, S)
        a = a_scr[pl.ds(r0, S), :]
        u = u_scr[pl.ds(r0, S), :]
        for sft in (1, 2, 4):
            if reverse:
                ok = sidx < S - sft
                a_e = pltpu.roll(a, S - sft, axis=0)
                u_e = pltpu.roll(u, S - sft, axis=0)
            else:
                ok = sidx >= sft
                a_e = pltpu.roll(a, sft, axis=0)
                u_e = pltpu.roll(u, sft, axis=0)
            u = jnp.where(ok, a * u_e + u, u)
            a = jnp.where(ok, a * a_e, a)
        hcur = a * carry[...] + u
        u_scr[pl.ds(r0, S), :] = hcur
        edge = 0 if reverse else S - 1
        carry[...] = jnp.broadcast_to(hcur[edge:edge + 1, :], (S, w))
        return c

    lax.fori_loop(0, TM // S, group, 0)
    o_ref[...] = u_scr[...].astype(BF16)


def _lru_scan(xraw, conv_w, conv_b, w_a, b_a, w_x, b_x, lam, reverse):
    n, w = xraw.shape
    nt = n // TM
    hb = TM // SUBLANES
    idx = lambda j: _lru_tile_index(j, nt, reverse)
    last8 = n // SUBLANES - 1
    wg = jnp.concatenate([w_a, w_x], axis=-1).astype(BF16)
    return pl.pallas_call(
        functools.partial(_lru_scan_kernel, reverse=reverse, n_tiles=nt),
        out_shape=jax.ShapeDtypeStruct((n, w), BF16),
        grid=(nt,),
        in_specs=[pl.BlockSpec((TM, w), lambda j: (idx(j), 0)),
                  pl.BlockSpec((SUBLANES, w), lambda j: (jnp.maximum(idx(j) * hb - 1, 0), 0)),
                  pl.BlockSpec((SUBLANES, w), lambda j: (jnp.minimum((idx(j) + 1) * hb, last8), 0)),
                  _full((4, w)), _full((1, w)), _full(wg.shape), _full((1, w)), _full((1, w)),
                  _full((1, w))],
        out_specs=pl.BlockSpec((TM, w), lambda j: (idx(j), 0)),
        scratch_shapes=[pltpu.VMEM((TM, w), F32), pltpu.VMEM((TM, w), F32),
                        pltpu.VMEM((SUBLANES, w), F32)],
        compiler_params=_cparams("arbitrary"),
        name="rglru_rev" if reverse else "rglru_fwd",
    )(xraw, xraw, xraw, conv_w, conv_b.reshape(1, w), wg, b_a.reshape(1, w), b_x.reshape(1, w),
      lam.reshape(1, w))


def _lru_out_kernel(gl_ref, hf_ref, hr_ref, s_ref, mod_ref, w_ref, out_ref):
    gate = _mod_rows(mod_ref, 1, 0, 0)[2]
    p = gl_ref[...].astype(F32) * (hf_ref[...].astype(F32) + hr_ref[...].astype(F32))
    out_ref[...] = s_ref[...] + gate * _bdot(p, w_ref[...])


def _lru_out(gl, hf, hr, s, mods, w_out):
    n, d = s.shape
    w = gl.shape[1]
    lat = lambda c: pl.BlockSpec((TM, c), lambda i: (i + 1, 0))
    return pl.pallas_call(
        _lru_out_kernel,
        out_shape=jax.ShapeDtypeStruct((n - TM, d), F32),
        grid=(n // TM - 1,),
        in_specs=[lat(w), lat(w), lat(w), lat(d), _full(mods.shape), _full(w_out.shape)],
        out_specs=pl.BlockSpec((TM, d), lambda i: (i, 0)),
        compiler_params=_cparams("arbitrary"),
        name="rglru_out",
    )(gl, hf, hr, s, mods, w_out.astype(BF16))


def _rglru_layer(s, mods, norm_g, w_in, conv_w, conv_b, w_a, b_a, w_x, b_x, lam, w_out):
    gl, xraw = _lru_proj(s, mods, norm_g, w_in)
    hf = _lru_scan(xraw, conv_w, conv_b, w_a[0], b_a[0], w_x[0], b_x[0], lam[0], False)
    hr = _lru_scan(xraw, conv_w, conv_b, w_a[1], b_a[1], w_x[1], b_x[1], lam[1], True)
    return _lru_out(gl, hf, hr, s, mods, w_out)


def _fn_proj_kernel(s_ref, mod_ref, g_ref, wt_ref, cs_ref, yr_ref, yi_ref, ar_scr, ai_scr):
    shift, scale, _ = _mod_rows(mod_ref, 1, 0, 0)
    nm = wt_ref.shape[0]
    gw = nm // FN_GROUPS
    per = TM // FFT_N2
    nj = FN_TB // FFT_N2
    csb = cs_ref[...].astype(BF16)

    def sub(tc, c):
        r0 = pl.multiple_of(tc * TM, TM)
        h = _rms(s_ref[pl.ds(r0, TM), :], g_ref[...]) * (1.0 + scale) + shift
        zt = lax.dot_general(wt_ref[...], h.astype(BF16), (((1,), (1,)), ((), ())),
                             preferred_element_type=F32).astype(BF16)
        for g in range(FN_GROUPS):
            y = jnp.dot(csb, zt[g * gw:(g + 1) * gw, :], preferred_element_type=F32)
            for q in range(per):
                row0 = pl.multiple_of((tc * per + q) * _slab_pitch(nm) + g * gw, SUBLANES)
                ar_scr[pl.ds(row0, gw), :] = y[:gw, q * FFT_N2:(q + 1) * FFT_N2]
                ai_scr[pl.ds(row0, gw), :] = y[gw:, q * FFT_N2:(q + 1) * FFT_N2]
        return c

    lax.fori_loop(0, FN_TB // TM, sub, 0)

    def relayout(m, c):
        yr_ref[m] = ar_scr[pl.ds(m, nj, stride=_slab_pitch(nm)), :]
        yi_ref[m] = ai_scr[pl.ds(m, nj, stride=_slab_pitch(nm)), :]
        return c

    lax.fori_loop(0, nm, relayout, 0, unroll=8)


def _slab_pitch(rows):
    return rows + SUBLANES


def _dft_cos_sin(n, scale):
    k = np.arange(n, dtype=np.int64)
    ang = 2.0 * np.pi * ((k[:, None] * k[None, :]) % n).astype(np.float64) / n
    return np.cos(ang) * scale, np.sin(ang) * scale


def _fn_proj(s, mods, norm_g, w_in):
    t, d = s.shape
    nm = w_in.shape[1]
    gw = nm // FN_GROUPS
    n1 = t // FFT_N2
    nj = FN_TB // FFT_N2
    c, sn = _dft_cos_sin(gw, gw ** -0.5)
    cs = jnp.asarray(np.concatenate([c, -sn], axis=0), F32)
    yspec = pl.BlockSpec((nm, nj, FFT_N2), lambda i: (0, i, 0))
    yshape = jax.ShapeDtypeStruct((nm, n1, FFT_N2), F32)
    return pl.pallas_call(
        _fn_proj_kernel,
        out_shape=(yshape, yshape),
        grid=(t // FN_TB,),
        in_specs=[pl.BlockSpec((FN_TB, d), lambda i: (i, 0)), _full(mods.shape), _full((1, d)),
                  _full((nm, d)), _full(cs.shape)],
        out_specs=(yspec, yspec),
        scratch_shapes=[pltpu.VMEM((nj * _slab_pitch(nm), FFT_N2), F32),
                        pltpu.VMEM((nj * _slab_pitch(nm), FFT_N2), F32)],
        compiler_params=_cparams("arbitrary"),
        name="fourier_proj",
    )(s, mods, norm_g.reshape(1, d), w_in.T.astype(BF16), cs)


def _fn_fft_kernel(yr_ref, yi_ref, m_ref, tc_ref, ts_ref, d_ref, o_ref):
    n1 = yr_ref.shape[1]
    n2 = FFT_N2
    xr = jnp.concatenate([yr_ref[m].astype(BF16) for m in range(FN_CB)], axis=1)
    xi = jnp.concatenate([yi_ref[m].astype(BF16) for m in range(FN_CB)], axis=1)
    a = jnp.dot(m_ref[...].astype(BF16), jnp.concatenate([xr, xi], axis=0),
                preferred_element_type=F32)
    ar = a[:n1]
    ai = a[n1:]
    tc = jnp.concatenate([tc_ref[...]] * FN_CB, axis=1)
    ts = jnp.concatenate([ts_ref[...]] * FN_CB, axis=1)
    br = ar * tc + ai * ts
    bi = ai * tc - ar * ts
    bst = jnp.concatenate(
        [jnp.concatenate([br[:, m * n2:(m + 1) * n2], bi[:, m * n2:(m + 1) * n2]], axis=1)
         for m in range(FN_CB)], axis=0).astype(BF16)
    res = lax.dot_general(d_ref[...].astype(BF16), bst, (((1,), (1,)), ((), ())),
                          preferred_element_type=F32)
    for m in range(FN_CB):
        o_ref[m] = res[:, m * n1:(m + 1) * n1]


def _fn_fft(yr, yi):
    nm, n1, n2 = yr.shape
    t = n1 * n2
    c, sn = _dft_cos_sin(n1, n1 ** -0.5)
    m = jnp.asarray(np.block([[c, sn], [-sn, c]]), F32)
    k1 = np.arange(n1, dtype=np.int64)[:, None]
    t2 = np.arange(n2, dtype=np.int64)[None, :]
    ang = 2.0 * np.pi * ((k1 * t2) % t).astype(np.float64) / t
    tc = jnp.asarray(np.cos(ang), F32)
    ts = jnp.asarray(np.sin(ang), F32)
    c2, s2 = _dft_cos_sin(n2, n2 ** -0.5)
    dm = jnp.asarray(np.concatenate([c2, s2], axis=1), F32)
    yspec = pl.BlockSpec((FN_CB, n1, n2), lambda i: (i, 0, 0))
    return pl.pallas_call(
        _fn_fft_kernel,
        out_shape=jax.ShapeDtypeStruct((nm, n2, n1), F32),
        grid=(nm // FN_CB,),
        in_specs=[yspec, yspec, _full(m.shape), _full(tc.shape), _full(ts.shape), _full(dm.shape)],
        out_specs=pl.BlockSpec((FN_CB, n2, n1), lambda i: (i, 0, 0)),
        compiler_params=_cparams("arbitrary"),
        name="fourier_fft",
    )(yr, yi, m, tc, ts, dm)


def _fn_out_kernel(ft_ref, w_ref, s_ref, mod_ref, o_ref, a_scr):
    gate = _mod_rows(mod_ref, 1, 0, 0)[2]
    nm, nj, n1 = ft_ref.shape

    def relayout(m, c):
        a_scr[pl.ds(m, nj, stride=_slab_pitch(nm)), :] = ft_ref[m]
        return c

    lax.fori_loop(0, nm, relayout, 0, unroll=8)
    for j in range(nj):
        p0 = j * _slab_pitch(nm)
        slab = a_scr[p0:p0 + nm, :].astype(BF16)
        y = lax.dot_general(slab, w_ref[...], (((0,), (0,)), ((), ())), preferred_element_type=F32)
        rows = slice(j * n1, (j + 1) * n1)
        o_ref[rows, :] = s_ref[rows, :] + gate * y


def _fn_out(ft, s, mods, w_out):
    t, d = s.shape
    nm, n2, n1 = ft.shape
    nj = FN_TB // n1
    tok = pl.BlockSpec((FN_TB, d), lambda i: (i, 0))
    return pl.pallas_call(
        _fn_out_kernel,
        out_shape=jax.ShapeDtypeStruct((t, d), F32),
        grid=(t // FN_TB,),
        in_specs=[pl.BlockSpec((nm, nj, n1), lambda i: (0, i, 0)), _full(w_out.shape), tok,
                  _full(mods.shape)],
        out_specs=tok,
        scratch_shapes=[pltpu.VMEM((nj * _slab_pitch(nm), n1), F32)],
        compiler_params=_cparams("arbitrary"),
        name="fourier_out",
    )(ft, w_out.astype(BF16), s, mods)


def _fourier_layer(s, mods, norm_g, w_in, w_out):
    yr, yi = _fn_proj(s, mods, norm_g, w_in)
    return _fn_out(_fn_fft(yr, yi), s, mods, w_out)


def kernel(x, c, ctx, c_ctx, ada_w, ada_b, norm_mix_g, norm_ffn_g, final_norm_g, router_group_w, router_group_b, router_expert_w, router_expert_b, expert_w_gate, expert_w_up, expert_w_down, cm_w_in, cm_v_norm_g, cm_w_s, cm_b_s, cm_w_out, ml_w_in, ml_conv_w, ml_conv_b, ml_gate_b, ml_norm_g, ml_w_out, lru_w_in, lru_conv_w, lru_conv_b, lru_w_a, lru_b_a, lru_w_x, lru_b_x, lru_lambda, lru_w_out, fn_w_in, fn_w_out):
    bsz, seq, d = x.shape
    assert bsz == 1 and ada_w.shape[0] == 4 and ctx.shape[1] == TM
    c_rows = jnp.concatenate([c_ctx[None, :], c, jnp.zeros((SUBLANES - 2, d), F32)], axis=0)
    mods = _ada_table(c_rows, ada_w, ada_b)
    s = _prep_stream(x[0], ctx[0])

    def moe(s, i, ctx_tiles, final_norm=False):
        return _moe_layer(s, mods[i], norm_ffn_g[i], router_group_w[i], router_group_b[i],
                          router_expert_w[i], router_expert_b[i], expert_w_gate, expert_w_up,
                          expert_w_down, i, ctx_tiles, final_norm_g, final_norm)

    s = _chunk_mlp_layer(s, mods[0], norm_mix_g[0], cm_w_in[0], cm_v_norm_g[0], cm_w_s[0],
                         cm_b_s[0], cm_w_out[0])
    s = moe(s, 0, 1)
    s = _mlstm_layer(s, mods[1], norm_mix_g[1], ml_w_in[0], ml_conv_w[0], ml_conv_b[0],
                     ml_gate_b[0], ml_norm_g[0], ml_w_out[0])
    s = moe(s, 1, 1)
    s = _rglru_layer(s, mods[2], norm_mix_g[2], lru_w_in[0], lru_conv_w[0], lru_conv_b[0],
                     lru_w_a[0], lru_b_a[0], lru_w_x[0], lru_b_x[0], lru_lambda[0], lru_w_out[0])
    s = moe(s, 2, 0)
    s = _fourier_layer(s, mods[3], norm_mix_g[3], fn_w_in[0], fn_w_out[0])
    s = moe(s, 3, 0, final_norm=True)
    return s[None]
```

```python
import functools
import math

import jax
import jax.numpy as jnp
import numpy as np
from jax import lax
from jax.experimental import pallas as pl
from jax.experimental.pallas import tpu as pltpu

F32 = jnp.float32
BF16 = jnp.bfloat16

EPS = 1e-6
POS_BASE = 10000.0
GRID_W = 64
N_MOD = 6
TM = 256
LANES = 128
SUBLANES = 8
VMEM_LIMIT = 56 * 1024 * 1024

CM_CHUNK = 128
CM_GROUPS = 4
ML_HEADS = 4
ML_DK = 128
ML_DV = 256
ML_CHUNK = 128
LRU_HEADS = 10
LRU_BLOCK = 128
LRU_C = 8.0
FN_GROUPS = 4
FFT_N2 = 128
MOE_GROUPS = 4
MOE_EPG = 8
MOE_EXPERTS = MOE_GROUPS * MOE_EPG
MOE_ROWS_LOG2 = 8
MOE_ROWS = 1 << MOE_ROWS_LOG2
ROUTE_ROWS = 40
FN_TB = 1024
FN_CB = 8
CONV_LEFT = 2

HI = lax.Precision.HIGHEST


def _cparams(*sem):
    return pltpu.CompilerParams(dimension_semantics=sem, vmem_limit_bytes=VMEM_LIMIT)


def _full(shape):
    nd = len(shape)
    return pl.BlockSpec(shape, lambda *_: (0,) * nd)


def _rms(x, g):
    return x * lax.rsqrt(jnp.mean(x * x, axis=-1, keepdims=True) + EPS) * g


def _gelu(x):
    c = math.sqrt(2.0 / math.pi)
    return 0.5 * x * (1.0 + jnp.tanh(c * (x + 0.044715 * (x * x * x))))


def _sigmoid(x):
    return 0.5 * jnp.tanh(0.5 * x) + 0.5


def _silu(x):
    return x * _sigmoid(x)


def _softplus(x):
    return jnp.maximum(x, 0.0) + jnp.log(1.0 + jnp.exp(-jnp.abs(x)))


def _mod_rows(mod_ref, tile, ctx_tiles, first):
    row = jnp.where(tile < ctx_tiles, 0, 1)
    m = mod_ref[pl.ds(row, 1), :]
    d = m.shape[1] // N_MOD
    return tuple(m[:, (first + j) * d:(first + j + 1) * d] for j in range(3))


def _bdot(a, b):
    return jnp.dot(a.astype(BF16), b.astype(BF16), preferred_element_type=F32)


def _ada_kernel(c_ref, w_ref, b_ref, o_ref):
    c = c_ref[...]
    o_ref[...] = jnp.dot(_silu(c), w_ref[...], precision=HI,
                         preferred_element_type=F32) + b_ref[...]


def _ada_table(c_rows, ada_w, ada_b):
    depth, d, n = ada_w.shape
    tn = 1024
    return pl.pallas_call(
        _ada_kernel,
        out_shape=jax.ShapeDtypeStruct((depth, SUBLANES, n), F32),
        grid=(depth, n // tn),
        in_specs=[_full((SUBLANES, d)),
                  pl.BlockSpec((None, d, tn), lambda i, j: (i, 0, j)),
                  pl.BlockSpec((None, 1, tn), lambda i, j: (i, 0, j))],
        out_specs=pl.BlockSpec((None, SUBLANES, tn), lambda i, j: (i, 0, j)),
        compiler_params=_cparams("arbitrary", "arbitrary"),
        name="ada_table",
    )(c_rows, ada_w, ada_b.reshape(depth, 1, n))


def _pos_tables(seq, d):
    q = d // 4
    freq = jnp.exp(-math.log(POS_BASE) * jnp.arange(q, dtype=F32) / q)
    ar = jnp.arange(seq // GRID_W, dtype=F32)[:, None] * freq
    ac = jnp.arange(GRID_W, dtype=F32)[:, None] * freq
    return (jnp.concatenate([jnp.sin(ar), jnp.cos(ar)], axis=-1),
            jnp.concatenate([jnp.sin(ac), jnp.cos(ac)], axis=-1))


def _cm_kernel(x_ref, ctx_ref, rt_ref, ct_ref, mod_ref, g_ref, win_ref, vg_ref, ws_ref, bs_ref,
               wout_ref, o_ref, p_scr, x_scr):
    i = pl.program_id(0)

    @pl.when(i == 0)
    def _():
        x_scr[...] = ctx_ref[...]

    @pl.when(i > 0)
    def _():
        rows_per_tile = TM // GRID_W
        q2 = rt_ref.shape[1]
        r0 = (i - 1) * rows_per_tile
        rt = jnp.concatenate(
            [jnp.broadcast_to(rt_ref[pl.ds(r0 + j, 1), :], (GRID_W, q2))
             for j in range(rows_per_tile)], axis=0)
        ct = jnp.concatenate([ct_ref[...]] * rows_per_tile, axis=0)
        x_scr[...] = x_ref[...] + jnp.concatenate([rt, ct], axis=1)

    shift, scale, gate = _mod_rows(mod_ref, i, 1, 0)
    x = x_scr[...]
    h = _rms(x, g_ref[...]) * (1.0 + scale) + shift
    z = _gelu(_bdot(h, win_ref[...]))
    w = z.shape[1] // 2
    u = z[:, :w]
    v = _rms(z[:, w:], vg_ref[...]).astype(BF16)
    gw = w // CM_GROUPS
    for c in range(TM // CM_CHUNK):
        r = slice(c * CM_CHUNK, (c + 1) * CM_CHUNK)
        for g in range(CM_GROUPS):
            cs = slice(g * gw, (g + 1) * gw)
            s = jnp.dot(ws_ref[g], v[r, cs], preferred_element_type=F32) + bs_ref[:, g:g + 1]
            p_scr[r, cs] = (u[r, cs] * s).astype(BF16)
    y = jnp.dot(p_scr[...], wout_ref[...], preferred_element_type=F32)
    o_ref[...] = x + gate * y


def _chunk_mlp_layer(x2, ctx2, mods, norm_g, w_in, v_g, w_s, b_s, w_out):
    seq, d = x2.shape
    n_ctx = ctx2.shape[0]
    assert n_ctx == TM and seq % TM == 0 and TM % GRID_W == 0
    n = n_ctx + seq
    w = w_out.shape[0]
    rt, ct = _pos_tables(seq, d)
    return pl.pallas_call(
        _cm_kernel,
        out_shape=jax.ShapeDtypeStruct((n, d), F32),
        grid=(n // TM,),
        in_specs=[pl.BlockSpec((TM, d), lambda i: (jnp.maximum(i - 1, 0), 0)),
                  _full((TM, d)), _full(rt.shape), _full(ct.shape),
                  _full(mods.shape), _full((1, d)), _full(w_in.shape), _full((1, w)),
                  _full(w_s.shape), _full((CM_CHUNK, CM_GROUPS)), _full(w_out.shape)],
        out_specs=pl.BlockSpec((TM, d), lambda i: (i, 0)),
        scratch_shapes=[pltpu.VMEM((TM, w), BF16), pltpu.VMEM((TM, d), F32)],
        compiler_params=_cparams("arbitrary"),
        name="chunk_mlp",
    )(x2, ctx2, rt, ct, mods, norm_g.reshape(1, d), w_in.astype(BF16), v_g.reshape(1, w),
      w_s.astype(BF16), b_s.T, w_out.astype(BF16))


def _store_token_tiles(ref, x):
    rows, d = x.shape
    for j in range(d // LANES):
        ref[pl.ds(j, rows, stride=d // LANES), :] = x[:, j * LANES:(j + 1) * LANES]


def _load_token_tiles(ref):
    chunks = SUBLANES
    rows = ref.shape[0] // chunks
    return jnp.concatenate([ref[pl.ds(j, rows, stride=chunks), :] for j in range(chunks)], axis=1)


def _route_tiles(nt):
    return next(k for k in (5, 4, 2, 1) if nt % k == 0)


def _router_kernel(s_ref, mod_ref, g_ref, rwt_ref, rbt_ref, tri_ref, h_ref, e1_ref, e2_ref, r1_ref,
                   r2_ref, wt_ref, cnt_ref, carry, *, ctx_rows):
    i = pl.program_id(0)
    rows, d = s_ref.shape

    @pl.when(i == 0)
    def _():
        carry[...] = jnp.zeros_like(carry)

    lat = mod_ref[1:2, :]
    shift, scale = lat[:, 3 * d:4 * d], lat[:, 4 * d:5 * d]
    if ctx_rows:
        ctx = mod_ref[0:1, :]
        is_ctx = (i == 0) & (lax.broadcasted_iota(jnp.int32, (rows, 1), 0) < ctx_rows)
        shift = jnp.where(is_ctx, ctx[:, 3 * d:4 * d], shift)
        scale = jnp.where(is_ctx, ctx[:, 4 * d:5 * d], scale)
    h = _rms(s_ref[...], g_ref[...]) * (1.0 + scale) + shift
    _store_token_tiles(h_ref, h)
    logits = lax.dot_general(rwt_ref[...], h, (((1,), (1,)), ((), ())), precision=HI,
                             preferred_element_type=F32) + rbt_ref[...]
    row = lax.broadcasted_iota(jnp.int32, logits.shape, 0)
    neg = jnp.float32(-jnp.inf)
    big = jnp.int32(1 << 20)
    is_g = row < MOE_GROUPS
    gl = jnp.where(is_g, logits, neg)
    gmax = jnp.max(gl, axis=0, keepdims=True)
    grp = jnp.min(jnp.where(is_g & (gl == gmax), row, big), axis=0, keepdims=True)
    p_grp = 1.0 / jnp.sum(jnp.exp(gl - gmax), axis=0, keepdims=True)
    e_row = row - MOE_GROUPS
    in_grp = (e_row >= 0) & (e_row < MOE_EXPERTS) & ((e_row >> 3) == grp)
    l1 = jnp.where(in_grp, logits, neg)
    v1 = jnp.max(l1, axis=0, keepdims=True)
    i1 = jnp.min(jnp.where(in_grp & (l1 == v1), row, big), axis=0, keepdims=True)
    rest = in_grp & (row != i1)
    l2 = jnp.where(rest, logits, neg)
    v2 = jnp.max(l2, axis=0, keepdims=True)
    i2 = jnp.min(jnp.where(rest & (l2 == v2), row, big), axis=0, keepdims=True)
    e21 = jnp.exp(v2 - v1)
    w1 = p_grp / (1.0 + e21)
    w2 = p_grp * e21 / (1.0 + e21)
    oh1 = (row == i1).astype(F32)
    oh2 = (row == i2).astype(F32)
    oh = oh1 + oh2
    before = jnp.dot(oh.astype(BF16), tri_ref[...], preferred_element_type=F32) + carry[:, 0:1]
    r1_ref[...] = jnp.sum(oh1 * before, axis=0, keepdims=True).astype(jnp.int32)
    r2_ref[...] = jnp.sum(oh2 * before, axis=0, keepdims=True).astype(jnp.int32)
    e1_ref[...] = i1 - MOE_GROUPS
    e2_ref[...] = i2 - MOE_GROUPS
    carry[...] = carry[...] + jnp.sum(oh, axis=1, keepdims=True)
    cnt_ref[...] = carry[...]
    wt_ref[...] = jnp.concatenate([w1, w2, jnp.zeros((LANES - 2, rows), F32)], axis=0).T


def _router(s, mods, norm_g, rg_w, rg_b, re_w, re_b, ctx_tiles):
    n, d = s.shape
    rows = TM * _route_tiles(n // TM)
    steps = n // rows
    pad = ROUTE_ROWS - MOE_GROUPS - MOE_EXPERTS
    rwt = jnp.concatenate([rg_w, re_w, jnp.zeros((d, pad), F32)], axis=1).T
    rbt = jnp.broadcast_to(jnp.concatenate([rg_b, re_b, jnp.zeros((pad,), F32)])[:, None],
                           (ROUTE_ROWS, rows))
    tri = jnp.asarray(np.triu(np.ones((rows, rows), np.float32), 1), BF16)
    assert d == SUBLANES * LANES
    tile = pl.BlockSpec((rows, d), lambda i: (i, 0))
    irow = pl.BlockSpec((None, 1, rows), lambda i: (i, 0, 0))
    ishape = jax.ShapeDtypeStruct((steps, 1, rows), jnp.int32)
    return pl.pallas_call(
        functools.partial(_router_kernel, ctx_rows=ctx_tiles * TM),
        out_shape=(jax.ShapeDtypeStruct((n * SUBLANES, LANES), F32), ishape, ishape, ishape, ishape,
                   jax.ShapeDtypeStruct((n, LANES), F32),
                   jax.ShapeDtypeStruct((ROUTE_ROWS, LANES), F32)),
        grid=(steps,),
        in_specs=[tile, _full(mods.shape), _full((1, d)), _full((ROUTE_ROWS, d)),
                  _full((ROUTE_ROWS, rows)), _full((rows, rows))],
        out_specs=(pl.BlockSpec((rows * SUBLANES, LANES), lambda i: (i, 0)), irow, irow, irow, irow,
                   pl.BlockSpec((rows, LANES), lambda i: (i, 0)), _full((ROUTE_ROWS, LANES))),
        scratch_shapes=[pltpu.VMEM((ROUTE_ROWS, LANES), F32)],
        compiler_params=_cparams("arbitrary"),
        name="moe_router",
    )(s, mods, norm_g.reshape(1, d), rwt, rbt, tri)


def _finalize_kernel(cnt_ref, e1_ref, e2_ref, r1_ref, r2_ref, d1_ref, d2_ref, blk_ref):
    e1 = e1_ref[...]
    e2 = e2_ref[...]
    r1 = r1_ref[...]
    r2 = r2_ref[...]
    d1 = jnp.zeros_like(e1)
    d2 = jnp.zeros_like(e2)
    lane = lax.broadcasted_iota(jnp.int32, blk_ref.shape, 1)
    brow = lane * MOE_ROWS
    sub = lax.broadcasted_iota(jnp.int32, blk_ref.shape, 0)
    be = jnp.zeros(blk_ref.shape, jnp.int32)
    pend = jnp.zeros(blk_ref.shape, jnp.int32)
    ps = jnp.int32(0)
    for e in range(MOE_EXPERTS):
        c = cnt_ref[e]
        pe = ps + lax.shift_left(lax.shift_right_logical(c + (MOE_ROWS - 1), MOE_ROWS_LOG2),
                                 MOE_ROWS_LOG2)
        d1 = jnp.where(e1 == e, ps + r1, d1)
        d2 = jnp.where(e2 == e, ps + r2, d2)
        be = be + (brow >= pe).astype(jnp.int32)
        pend = jnp.where(lane == e, pe, pend)
        ps = pe
    d1_ref[...] = d1
    d2_ref[...] = d2
    n_used = lax.shift_right_logical(ps, MOE_ROWS_LOG2)
    blk_ref[...] = jnp.where(sub == 0, jnp.minimum(be, MOE_EXPERTS - 1),
                             jnp.where(sub == 1, pend, n_used))


def _finalize(counts, e1, e2, r1, r2, nb):
    nbp = (nb + LANES - 1) // LANES * LANES
    whole = pl.BlockSpec(e1.shape, lambda i, c: (0, 0, 0))
    ishape = jax.ShapeDtypeStruct(e1.shape, jnp.int32)
    return pl.pallas_call(
        _finalize_kernel,
        out_shape=(ishape, ishape, jax.ShapeDtypeStruct((SUBLANES, nbp), jnp.int32)),
        grid_spec=pltpu.PrefetchScalarGridSpec(
            num_scalar_prefetch=1,
            grid=(1,),
            in_specs=[whole, whole, whole, whole],
            out_specs=(whole, whole, pl.BlockSpec((SUBLANES, nbp), lambda i, c: (0, 0)))),
        compiler_params=_cparams("arbitrary"),
        name="moe_finalize",
    )(counts, e1, e2, r1, r2)


def _token_copy(src, r, dst, d, sem):
    return pltpu.make_async_copy(src.at[pl.ds(pl.multiple_of(r * SUBLANES, SUBLANES), SUBLANES), :],
                                 dst.at[pl.ds(pl.multiple_of(d * SUBLANES, SUBLANES), SUBLANES), :],
                                 sem)


def _zero_fill_padding(pend_ref, nu_ref, xs_out, zbuf, zsem):
    blk_rows = MOE_ROWS * SUBLANES
    nb = xs_out.shape[0] // blk_rows
    zbuf[...] = jnp.zeros_like(zbuf)

    def block_copy(b):
        r0 = pl.multiple_of(b * blk_rows, blk_rows)
        return pltpu.make_async_copy(zbuf, xs_out.at[pl.ds(r0, blk_rows), :], zsem)

    def seg_last_block(e):
        pe = pend_ref[e]
        prev = pend_ref[e - 1] if e > 0 else 0
        return pe > prev, lax.shift_right_logical(pe, MOE_ROWS_LOG2) - 1

    for e in range(MOE_EXPERTS):
        nonempty, b = seg_last_block(e)

        @pl.when(nonempty)
        def _():
            block_copy(b).start()

    def tail_start(b, c):
        block_copy(b).start()
        return c

    lax.fori_loop(nu_ref[0], nb, tail_start, 0)
    for e in range(MOE_EXPERTS):
        nonempty, b = seg_last_block(e)

        @pl.when(nonempty)
        def _():
            block_copy(b).wait()

    def tail_wait(b, c):
        block_copy(b).wait()
        return c

    lax.fori_loop(nu_ref[0], nb, tail_wait, 0)


def _dispatch_kernel(dest_ref, pend_ref, nu_ref, h_ref, xs_out, sem, zbuf, zsem, *, tiles):
    i = pl.program_id(0)

    @pl.when(i == 0)
    def _():
        _zero_fill_padding(pend_ref, nu_ref, xs_out, zbuf, zsem)

    rows = tiles * TM
    for q in range(tiles):
        base = i * (2 * rows) + q * TM

        def start(r, c):
            _token_copy(h_ref, q * TM + r, xs_out, dest_ref[base + r], sem).start(priority=0)
            _token_copy(h_ref, q * TM + r, xs_out, dest_ref[base + rows + r],
                        sem).start(priority=1)
            return c

        lax.fori_loop(0, TM, start, 0, unroll=8)
    for _ in range(2):
        pltpu.make_async_copy(h_ref, xs_out.at[pl.ds(0, tiles * TM * SUBLANES), :], sem).wait()


def _dispatch(dest, pad_end, n_used, h, n_rows):
    n = h.shape[0] // SUBLANES
    nt = n // TM
    tiles = _route_tiles(nt)
    return pl.pallas_call(
        functools.partial(_dispatch_kernel, tiles=tiles),
        out_shape=jax.ShapeDtypeStruct((n_rows * SUBLANES, LANES), F32),
        grid_spec=pltpu.PrefetchScalarGridSpec(
            num_scalar_prefetch=3,
            grid=(nt // tiles,),
            in_specs=[pl.BlockSpec((tiles * TM * SUBLANES, LANES), lambda i, *_: (i, 0))],
            out_specs=pl.BlockSpec(memory_space=pl.ANY),
            scratch_shapes=[pltpu.SemaphoreType.DMA, pltpu.VMEM((MOE_ROWS * SUBLANES, LANES), F32),
                            pltpu.SemaphoreType.DMA]),
        compiler_params=_cparams("arbitrary"),
        name="moe_dispatch",
    )(dest, pad_end, n_used, h)


X_SLOTS = 3


def _expert_kernel(be_ref, pend_ref, nu_ref, xs_hbm, wg_hbm, wu_hbm, wd_hbm, y_ref,
                   xbuf, wg_f, wu_f, wd_f, wg_s, wu_s, wd_s, xsem, wsem, ord_ref, *, layer):
    b = pl.program_id(0)
    nu = nu_ref[0]
    blk_rows = MOE_ROWS * SUBLANES

    def x_copy(blk, slot):
        r0 = pl.multiple_of(blk * blk_rows, blk_rows)
        return pltpu.make_async_copy(xs_hbm.at[pl.ds(r0, blk_rows), :], xbuf.at[slot],
                                     xsem.at[slot])

    def w_copies(e, slot):
        return (pltpu.make_async_copy(wg_hbm.at[layer, e], wg_f.at[slot], wsem.at[slot]),
                pltpu.make_async_copy(wu_hbm.at[layer, e], wu_f.at[slot], wsem.at[slot]),
                pltpu.make_async_copy(wd_hbm.at[layer, e], wd_f.at[slot], wsem.at[slot]))

    @pl.when(b == 0)
    def _():
        ord_ref[0] = 0
        for j in range(X_SLOTS - 1):
            @pl.when(j < nu)
            def _():
                x_copy(j, j).start()

        @pl.when(nu > 0)
        def _():
            for c in w_copies(be_ref[0], 0):
                c.start()

    ahead = b + (X_SLOTS - 1)

    @pl.when(ahead < nu)
    def _():
        x_copy(ahead, lax.rem(ahead, X_SLOTS)).start()

    used = b < nu
    e = be_ref[b]
    fresh = used & ((b == 0) | (e != be_ref[jnp.maximum(b - 1, 0)]))

    @pl.when(fresh)
    def _():
        k = ord_ref[0]
        slot = lax.rem(k, 2)
        for c in w_copies(e, slot):
            c.wait()
        wg_s[...] = wg_f[slot].astype(BF16)
        wu_s[...] = wu_f[slot].astype(BF16)
        wd_s[...] = wd_f[slot].astype(BF16)
        nxt = lax.shift_right_logical(pend_ref[e], MOE_ROWS_LOG2)

        @pl.when(nxt < nu)
        def _():
            for c in w_copies(be_ref[nxt], 1 - slot):
                c.start()

        ord_ref[0] = k + 1

    @pl.when(used)
    def _():
        slot = lax.rem(b, X_SLOTS)
        x_copy(b, slot).wait()
        x = _load_token_tiles(xbuf.at[slot]).astype(BF16)
        a = jnp.dot(x, wg_s[...], preferred_element_type=F32)
        u = jnp.dot(x, wu_s[...], preferred_element_type=F32)
        _store_token_tiles(y_ref, jnp.dot((_silu(a) * u).astype(BF16), wd_s[...],
                                          preferred_element_type=F32))

    @pl.when(jnp.logical_not(used))
    def _():
        y_ref[...] = jnp.zeros_like(y_ref)


def _experts(blk_expert, pad_end, n_used, xs, w_gate, w_up, w_down, layer):
    d, hid = w_gate.shape[2:]
    blk_rows = MOE_ROWS * SUBLANES
    nb = xs.shape[0] // blk_rows
    hbm = pl.BlockSpec(memory_space=pl.ANY)
    return pl.pallas_call(
        functools.partial(_expert_kernel, layer=layer),
        out_shape=jax.ShapeDtypeStruct(xs.shape, F32),
        grid_spec=pltpu.PrefetchScalarGridSpec(
            num_scalar_prefetch=3,
            grid=(nb,),
            in_specs=[hbm, hbm, hbm, hbm],
            out_specs=pl.BlockSpec((blk_rows, LANES), lambda b, *_: (b, 0)),
            scratch_shapes=[pltpu.VMEM((X_SLOTS, blk_rows, LANES), F32),
                            pltpu.VMEM((2, d, hid), F32), pltpu.VMEM((2, d, hid), F32),
                            pltpu.VMEM((2, hid, d), F32),
                            pltpu.VMEM((d, hid), BF16), pltpu.VMEM((d, hid), BF16),
                            pltpu.VMEM((hid, d), BF16),
                            pltpu.SemaphoreType.DMA((X_SLOTS,)), pltpu.SemaphoreType.DMA((2,)),
                            pltpu.SMEM((1,), jnp.int32)]),
        compiler_params=_cparams("arbitrary"),
        name="moe_experts",
    )(blk_expert, pad_end, n_used, xs, w_gate, w_up, w_down)


def _combine_kernel(dest_ref, s_ref, wt_ref, mod_ref, fg_ref, ys_ref, o_ref, ybuf, sem,
                    *, ctx_tiles, final_norm, tiles):
    i = pl.program_id(0)
    slot = i % 2
    rows = tiles * TM

    def gather(tile, slot):
        tile = jnp.asarray(tile, jnp.int32)
        base = lax.div(tile, tiles) * (2 * rows) + lax.rem(tile, tiles) * TM

        def start(r, c):
            _token_copy(ys_ref, dest_ref[base + r], ybuf.at[slot, 0], r,
                        sem.at[slot]).start(priority=0)
            _token_copy(ys_ref, dest_ref[base + rows + r], ybuf.at[slot, 1], r,
                        sem.at[slot]).start(priority=1)
            return c

        lax.fori_loop(0, TM, start, 0, unroll=8)

    @pl.when(i == 0)
    def _():
        gather(0, 0)

    @pl.when(i + 1 < pl.num_programs(0))
    def _():
        gather(i + 1, 1 - slot)

    for k in range(2):
        pltpu.make_async_copy(ys_ref.at[pl.ds(0, TM * SUBLANES), :], ybuf.at[slot, k],
                              sem.at[slot]).wait()
    gate = _mod_rows(mod_ref, i, ctx_tiles, 3)[2]
    wt = wt_ref[...]
    y = (wt[:, 0:1] * _load_token_tiles(ybuf.at[slot, 0])
         + wt[:, 1:2] * _load_token_tiles(ybuf.at[slot, 1]))
    out = s_ref[...] + gate * y
    if final_norm:
        out = _rms(out, fg_ref[...])
    o_ref[...] = out


def _combine(dest, s, wts, mods, final_g, ys, ctx_tiles, final_norm):
    n, d = s.shape
    return pl.pallas_call(
        functools.partial(_combine_kernel, ctx_tiles=ctx_tiles, final_norm=final_norm,
                          tiles=_route_tiles(n // TM)),
        out_shape=jax.ShapeDtypeStruct((n, d), F32),
        grid_spec=pltpu.PrefetchScalarGridSpec(
            num_scalar_prefetch=1,
            grid=(n // TM,),
            in_specs=[pl.BlockSpec((TM, d), lambda i, dst: (i, 0)),
                      pl.BlockSpec((TM, LANES), lambda i, dst: (i, 0)),
                      pl.BlockSpec(mods.shape, lambda i, dst: (0, 0)),
                      pl.BlockSpec((1, d), lambda i, dst: (0, 0)),
                      pl.BlockSpec(memory_space=pl.ANY)],
            out_specs=pl.BlockSpec((TM, d), lambda i, dst: (i, 0)),
            scratch_shapes=[pltpu.VMEM((2, 2, TM * SUBLANES, LANES), F32),
                            pltpu.SemaphoreType.DMA((2,))]),
        compiler_params=_cparams("arbitrary"),
        name="moe_combine",
    )(dest, s, wts, mods, final_g.reshape(1, d), ys)


def _moe_layer(s, mods, norm_g, rg_w, rg_b, re_w, re_b, w_gate, w_up, w_down, layer, ctx_tiles,
               final_g, final_norm):
    n, d = s.shape
    h, e1, e2, r1, r2, wts, cnt = _router(s, mods, norm_g, rg_w, rg_b, re_w, re_b, ctx_tiles)
    counts = cnt[MOE_GROUPS:MOE_GROUPS + MOE_EXPERTS, 0].astype(jnp.int32)
    nb = (2 * n + MOE_EXPERTS * (MOE_ROWS - 1)) // MOE_ROWS + 1
    d1, d2, blk = _finalize(counts, e1, e2, r1, r2, nb)
    dest = jnp.concatenate([d1, d2], axis=1).reshape(2 * n)
    n_used = blk[2, :1]
    pad_end = blk[1, :MOE_EXPERTS]
    xs = _dispatch(dest, pad_end, n_used, h, nb * MOE_ROWS)
    ys = _experts(blk[0, :nb], pad_end, n_used, xs, w_gate, w_up, w_down, layer)
    return _combine(dest, s, wts, mods, final_g, ys, ctx_tiles, final_norm)


def _conv_tile(x, prev_ref, next_ref, has_prev, has_next, w_ref, b_ref):
    rows = x.shape[0]
    S = SUBLANES
    ridx = lax.broadcasted_iota(jnp.int32, (S, x.shape[1]), 0)
    pm = jnp.where(has_prev, 1.0, 0.0)
    nm = jnp.where(has_next, 1.0, 0.0)
    p2 = prev_ref[S - 2:S - 1, :] * pm
    p1 = prev_ref[S - 1:S, :] * pm
    n1 = next_ref[0:1, :] * nm

    def fix_head(rolled, head):
        return jnp.concatenate([head(rolled[:S]), rolled[S:]], axis=0)

    xm1 = fix_head(pltpu.roll(x, 1, axis=0), lambda g: jnp.where(ridx == 0, p1, g))
    xm2 = fix_head(pltpu.roll(x, 2, axis=0),
                   lambda g: jnp.where(ridx == 0, p2, jnp.where(ridx == 1, p1, g)))
    xp1 = pltpu.roll(x, rows - 1, axis=0)
    xp1 = jnp.concatenate([xp1[:rows - S], jnp.where(ridx == S - 1, n1, xp1[rows - S:])], axis=0)
    return (xm2 * w_ref[0:1, :] + xm1 * w_ref[1:2, :] + x * w_ref[2:3, :]
            + xp1 * w_ref[3:4, :] + b_ref[...])


def _ml_proj_kernel(s_ref, mod_ref, g_ref, w_ref, wg_ref, gb_ref, qk_ref, v_ref, o_ref, gt_ref):
    i = pl.program_id(0)
    shift, scale, _ = _mod_rows(mod_ref, i, 1, 0)
    h = _rms(s_ref[...], g_ref[...]) * (1.0 + scale) + shift
    z = _bdot(h, w_ref[...])
    nqk = qk_ref.shape[1]
    nv = v_ref.shape[1]
    qk_ref[...] = z[:, :nqk]
    v_ref[...] = z[:, nqk:nqk + nv].astype(BF16)
    o_ref[...] = z[:, nqk + nv:].astype(BF16)
    pre = jnp.dot(h, wg_ref[...], precision=HI, preferred_element_type=F32) + gb_ref[...]
    lane = lax.broadcasted_iota(jnp.int32, pre.shape, 1)
    is_forget = ((lane >> 2) & 1) == 1
    gt_ref[...] = jnp.where(is_forget, -_softplus(-pre), pre)


def _ml_proj(s, mods, norm_g, w_in, gate_b):
    n, d = s.shape
    nqk = 2 * ML_HEADS * ML_DK
    nv = ML_HEADS * ML_DV
    n_main = nqk + 2 * nv
    n_gate = w_in.shape[1] - n_main
    w_main = w_in[:, :n_main].astype(BF16)
    w_gate = jnp.concatenate([w_in[:, n_main:], jnp.zeros((d, LANES - n_gate), F32)], axis=1)
    gb = jnp.concatenate([gate_b.reshape(n_gate), jnp.zeros((LANES - n_gate,), F32)]).reshape(1, LANES)
    tile = lambda w: pl.BlockSpec((TM, w), lambda i: (i, 0))
    return pl.pallas_call(
        _ml_proj_kernel,
        out_shape=(jax.ShapeDtypeStruct((n, nqk), F32), jax.ShapeDtypeStruct((n, nv), BF16),
                   jax.ShapeDtypeStruct((n, nv), BF16), jax.ShapeDtypeStruct((n, LANES), F32)),
        grid=(n // TM,),
        in_specs=[tile(d), _full(mods.shape), _full((1, d)), _full(w_main.shape),
                  _full((d, LANES)), _full((1, LANES))],
        out_specs=(tile(nqk), tile(nv), tile(nv), tile(LANES)),
        compiler_params=_cparams("arbitrary"),
        name="mlstm_proj",
    )(s, mods, norm_g.reshape(1, d), w_main, w_gate, gb)


def _ml_chunk_index(j, n_chunks, ctx_chunks, reverse):
    if not reverse:
        return j
    return jnp.where(j < ctx_chunks, ctx_chunks - 1 - j, n_chunks - 1 + ctx_chunks - j)


def _ml_rec_kernel(qk_ref, qkp_ref, qkn_ref, v_ref, gt_ref, gtt_ref, cw_ref, cb_ref, o_ref,
                   c_scr, n_scr, m_scr, *, reverse, n_chunks, ctx_chunks):
    j = pl.program_id(0)
    c = _ml_chunk_index(j, n_chunks, ctx_chunks, reverse)

    @pl.when(j == 0)
    def _():
        c_scr[...] = jnp.zeros_like(c_scr)
        n_scr[...] = jnp.zeros_like(n_scr)
        m_scr[...] = jnp.zeros_like(m_scr)

    has_prev = (c != 0) & (c != ctx_chunks)
    has_next = (c != ctx_chunks - 1) & (c != n_chunks - 1)
    qk = _silu(_conv_tile(qk_ref[...], qkp_ref, qkn_ref, has_prev, has_next, cw_ref, cb_ref))
    L = ML_CHUNK
    ri = lax.broadcasted_iota(jnp.int32, (L, L), 0)
    ci = lax.broadcasted_iota(jnp.int32, (L, L), 1)
    past = (ci >= ri) if reverse else (ci <= ri)
    pastf = past.astype(F32)
    gt = gt_ref[...]
    gtt = gtt_ref[...]
    b_col = jnp.dot(pastf, gt, precision=HI, preferred_element_type=F32)
    b_row = jnp.dot(gtt, pastf.T, precision=HI, preferred_element_type=F32)
    last = 0 if reverse else L - 1
    dbase = 8 if reverse else 0
    nq = ML_HEADS * ML_DK
    for hd in range(ML_HEADS):
        cl = dbase + hd
        cf = dbase + 4 + hd
        q = qk[:, hd * ML_DK:(hd + 1) * ML_DK] * (ML_DK ** -0.5)
        k = qk[:, nq + hd * ML_DK:nq + (hd + 1) * ML_DK]
        v = v_ref[:, hd * ML_DV:(hd + 1) * ML_DV]
        li_c = gt[:, cl:cl + 1]
        li_r = gtt[cl:cl + 1, :]
        b_c = b_col[:, cf:cf + 1]
        b_r = b_row[cf:cf + 1, :]
        g = b_r[:, last:last + 1]
        m0 = m_scr[hd:hd + 1, 0:1]
        c0 = c_scr[hd]
        n0 = n_scr[hd:hd + 1, :]
        a_c = g - b_c + li_c
        a_r = g - b_r + li_r
        m_loc = jnp.max(a_r, axis=-1, keepdims=True)
        inter = b_c + m0
        dlog = jnp.where(past, b_c - b_r + li_r, -jnp.inf)
        m = jnp.maximum(inter, jnp.max(dlog, axis=-1, keepdims=True))
        qb = q.astype(BF16)
        sc = lax.dot_general(qb, k.astype(BF16), (((1,), (1,)), ((), ())),
                             preferred_element_type=F32) * jnp.exp(dlog - m)
        w_inter = jnp.exp(inter - m)
        num = (jnp.dot(sc.astype(BF16), v, preferred_element_type=F32)
               + w_inter * jnp.dot(qb, c0.astype(BF16), preferred_element_type=F32))
        den = (jnp.sum(sc, axis=-1, keepdims=True)
               + w_inter * jnp.sum(q * n0, axis=-1, keepdims=True))
        o_ref[:, hd * ML_DV:(hd + 1) * ML_DV] = (
            num / jnp.maximum(jnp.abs(den), jnp.exp(-m))).astype(BF16)
        m_new = jnp.maximum(g + m0, m_loc)
        dec = jnp.exp(g + m0 - m_new)
        scl = jnp.exp(m_loc - m_new)
        kw = k * jnp.exp(a_c - m_loc)
        c_scr[hd] = dec * c0 + scl * jnp.dot(kw.T.astype(BF16), v, preferred_element_type=F32)
        n_scr[hd:hd + 1, :] = dec * n0 + scl * jnp.sum(kw, axis=0, keepdims=True)
        m_scr[hd:hd + 1, :] = jnp.broadcast_to(m_new, (1, LANES))


def _ml_rec(qk, v, gt, gtt, conv_w, conv_b, reverse):
    n, nqk = qk.shape
    nv = v.shape[1]
    L = ML_CHUNK
    nc = n // L
    cc = TM // L
    hb = L // SUBLANES
    idx = lambda j: _ml_chunk_index(j, nc, cc, reverse)
    last8 = n // SUBLANES - 1
    return pl.pallas_call(
        functools.partial(_ml_rec_kernel, reverse=reverse, n_chunks=nc, ctx_chunks=cc),
        out_shape=jax.ShapeDtypeStruct((n, nv), BF16),
        grid=(nc,),
        in_specs=[pl.BlockSpec((L, nqk), lambda j: (idx(j), 0)),
                  pl.BlockSpec((SUBLANES, nqk), lambda j: (jnp.maximum(idx(j) * hb - 1, 0), 0)),
                  pl.BlockSpec((SUBLANES, nqk), lambda j: (jnp.minimum((idx(j) + 1) * hb, last8), 0)),
                  pl.BlockSpec((L, nv), lambda j: (idx(j), 0)),
                  pl.BlockSpec((L, LANES), lambda j: (idx(j), 0)),
                  pl.BlockSpec((2 * SUBLANES, L), lambda j: (0, idx(j))),
                  _full((4, nqk)), _full((1, nqk))],
        out_specs=pl.BlockSpec((L, nv), lambda j: (idx(j), 0)),
        scratch_shapes=[pltpu.VMEM((ML_HEADS, ML_DK, ML_DV), F32),
                        pltpu.VMEM((SUBLANES, ML_DK), F32),
                        pltpu.VMEM((SUBLANES, LANES), F32)],
        compiler_params=_cparams("arbitrary"),
        name="mlstm_rev" if reverse else "mlstm_fwd",
    )(qk, qk, qk, v, gt, gtt, conv_w, conv_b.reshape(1, nqk))


def _ml_out_kernel(hf_ref, hr_ref, o_ref, s_ref, mod_ref, ng_ref, w_ref, out_ref, p_scr):
    i = pl.program_id(0)
    gate = _mod_rows(mod_ref, i, 1, 0)[2]
    hs = hf_ref[...].astype(F32) + hr_ref[...].astype(F32)
    sig = _sigmoid(o_ref[...].astype(F32))
    ng = ng_ref[...]
    for hd in range(ML_HEADS):
        cs = slice(hd * ML_DV, (hd + 1) * ML_DV)
        seg = hs[:, cs]
        hn = seg * lax.rsqrt(jnp.mean(seg * seg, axis=-1, keepdims=True) + EPS) * ng[:, cs]
        p_scr[:, cs] = (hn * sig[:, cs]).astype(BF16)
    y = jnp.dot(p_scr[...], w_ref[...], preferred_element_type=F32)
    out_ref[...] = s_ref[...] + gate * y


def _ml_out(hf, hr, o, s, mods, norm_g, w_out):
    n, d = s.shape
    nv = hf.shape[1]
    tile = lambda w: pl.BlockSpec((TM, w), lambda i: (i, 0))
    return pl.pallas_call(
        _ml_out_kernel,
        out_shape=jax.ShapeDtypeStruct((n, d), F32),
        grid=(n // TM,),
        in_specs=[tile(nv), tile(nv), tile(nv), tile(d), _full(mods.shape), _full((1, nv)),
                  _full(w_out.shape)],
        out_specs=tile(d),
        scratch_shapes=[pltpu.VMEM((TM, nv), BF16)],
        compiler_params=_cparams("arbitrary"),
        name="mlstm_out",
    )(hf, hr, o, s, mods, norm_g.reshape(1, nv), w_out.astype(BF16))


def _mlstm_layer(s, mods, norm_g, w_in, conv_w, conv_b, gate_b, ml_norm_g, w_out):
    qk, v, o, gt = _ml_proj(s, mods, norm_g, w_in, gate_b)
    gtt = gt[:, :2 * SUBLANES].T
    hf = _ml_rec(qk, v, gt, gtt, conv_w, conv_b, False)
    hr = _ml_rec(qk, v, gt, gtt, conv_w, conv_b, True)
    return _ml_out(hf, hr, o, s, mods, ml_norm_g, w_out)


def _lru_proj_kernel(s_ref, mod_ref, g_ref, w_ref, gl_ref, xr_ref):
    i = pl.program_id(0)
    shift, scale, _ = _mod_rows(mod_ref, i, 1, 0)
    h = _rms(s_ref[...], g_ref[...]) * (1.0 + scale) + shift
    z = _bdot(h, w_ref[...])
    w = gl_ref.shape[1]
    gl_ref[...] = _gelu(z[:, :w]).astype(BF16)
    xr_ref[...] = z[:, w:]


def _lru_proj(s, mods, norm_g, w_in):
    n, d = s.shape
    w = w_in.shape[1] // 2
    tile = lambda c: pl.BlockSpec((TM, c), lambda i: (i, 0))
    return pl.pallas_call(
        _lru_proj_kernel,
        out_shape=(jax.ShapeDtypeStruct((n, w), BF16), jax.ShapeDtypeStruct((n, w), F32)),
        grid=(n // TM,),
        in_specs=[tile(d), _full(mods.shape), _full((1, d)), _full(w_in.shape)],
        out_specs=(tile(w), tile(w)),
        compiler_params=_cparams("arbitrary"),
        name="rglru_proj",
    )(s, mods, norm_g.reshape(1, d), w_in.astype(BF16))


def _lru_tile_index(j, n_tiles, reverse):
    if not reverse:
        return j
    return jnp.where(j == 0, 0, n_tiles - j)


def _lru_scan_kernel(x_ref, xp_ref, xn_ref, cw_ref, cb_ref, wg_ref, ba_ref, bx_ref, lam_ref,
                     o_ref, a_scr, u_scr, carry, *, reverse, n_tiles):
    j = pl.program_id(0)
    t = _lru_tile_index(j, n_tiles, reverse)

    @pl.when(j == 0)
    def _():
        carry[...] = jnp.zeros_like(carry)

    has_prev = t > 1
    has_next = (t != 0) & (t != n_tiles - 1)
    xr = _conv_tile(x_ref[...], xp_ref, xn_ref, has_prev, has_next, cw_ref, cb_ref)
    sp = _softplus(-lam_ref[...])
    B = LRU_BLOCK
    for hd in range(LRU_HEADS):
        cs = slice(hd * B, (hd + 1) * B)
        xh = xr[:, cs]
        y = jnp.dot(xh.astype(BF16), wg_ref[hd], preferred_element_type=F32)
        r = _sigmoid(y[:, :B] + ba_ref[:, cs])
        ig = _sigmoid(y[:, B:] + bx_ref[:, cs])
        log_a = -LRU_C * r * sp[:, cs]
        a = jnp.exp(log_a)
        a_scr[:, cs] = a
        u_scr[:, cs] = jnp.sqrt(1.0 - a * a) * (ig * xh)

    S = SUBLANES
    w = a_scr.shape[1]
    sidx = lax.broadcasted_iota(jnp.int32, (S, w), 0)

    def group(gi, c):
        g = (TM // S - 1 - gi) if reverse else gi
        r0 = pl.multiple_of(g * S, S)
        a = a_scr[pl.ds(r0, S), :]
        u = u_scr[pl.ds(r0, S), :]
        for sft in (1, 2, 4):
            if reverse:
                ok = sidx < S - sft
                a_e = pltpu.roll(a, S - sft, axis=0)
                u_e = pltpu.roll(u, S - sft, axis=0)
            else:
                ok = sidx >= sft
                a_e = pltpu.roll(a, sft, axis=0)
                u_e = pltpu.roll(u, sft, axis=0)
            u = jnp.where(ok, a * u_e + u, u)
            a = jnp.where(ok, a * a_e, a)
        hcur = a * carry[...] + u
        u_scr[pl.ds(r0, S), :] = hcur
        edge = 0 if reverse else S - 1
        carry[...] = jnp.broadcast_to(hcur[edge:edge + 1, :], (S, w))
        return c

    lax.fori_loop(0, TM // S, group, 0)
    o_ref[...] = u_scr[...].astype(BF16)


def _lru_scan(xraw, conv_w, conv_b, w_a, b_a, w_x, b_x, lam, reverse):
    n, w = xraw.shape
    nt = n // TM
    hb = TM // SUBLANES
    idx = lambda j: _lru_tile_index(j, nt, reverse)
    last8 = n // SUBLANES - 1
    wg = jnp.concatenate([w_a, w_x], axis=-1).astype(BF16)
    return pl.pallas_call(
        functools.partial(_lru_scan_kernel, reverse=reverse, n_tiles=nt),
        out_shape=jax.ShapeDtypeStruct((n, w), BF16),
        grid=(nt,),
        in_specs=[pl.BlockSpec((TM, w), lambda j: (idx(j), 0)),
                  pl.BlockSpec((SUBLANES, w), lambda j: (jnp.maximum(idx(j) * hb - 1, 0), 0)),
                  pl.BlockSpec((SUBLANES, w), lambda j: (jnp.minimum((idx(j) + 1) * hb, last8), 0)),
                  _full((4, w)), _full((1, w)), _full(wg.shape), _full((1, w)), _full((1, w)),
                  _full((1, w))],
        out_specs=pl.BlockSpec((TM, w), lambda j: (idx(j), 0)),
        scratch_shapes=[pltpu.VMEM((TM, w), F32), pltpu.VMEM((TM, w), F32),
                        pltpu.VMEM((SUBLANES, w), F32)],
        compiler_params=_cparams("arbitrary"),
        name="rglru_rev" if reverse else "rglru_fwd",
    )(xraw, xraw, xraw, conv_w, conv_b.reshape(1, w), wg, b_a.reshape(1, w), b_x.reshape(1, w),
      lam.reshape(1, w))


def _lru_out_kernel(gl_ref, hf_ref, hr_ref, s_ref, mod_ref, w_ref, out_ref):
    gate = _mod_rows(mod_ref, 1, 0, 0)[2]
    p = gl_ref[...].astype(F32) * (hf_ref[...].astype(F32) + hr_ref[...].astype(F32))
    out_ref[...] = s_ref[...] + gate * _bdot(p, w_ref[...])


def _lru_out(gl, hf, hr, s, mods, w_out):
    n, d = s.shape
    w = gl.shape[1]
    lat = lambda c: pl.BlockSpec((TM, c), lambda i: (i + 1, 0))
    return pl.pallas_call(
        _lru_out_kernel,
        out_shape=jax.ShapeDtypeStruct((n - TM, d), F32),
        grid=(n // TM - 1,),
        in_specs=[lat(w), lat(w), lat(w), lat(d), _full(mods.shape), _full(w_out.shape)],
        out_specs=pl.BlockSpec((TM, d), lambda i: (i, 0)),
        compiler_params=_cparams("arbitrary"),
        name="rglru_out",
    )(gl, hf, hr, s, mods, w_out.astype(BF16))


def _rglru_layer(s, mods, norm_g, w_in, conv_w, conv_b, w_a, b_a, w_x, b_x, lam, w_out):
    gl, xraw = _lru_proj(s, mods, norm_g, w_in)
    hf = _lru_scan(xraw, conv_w, conv_b, w_a[0], b_a[0], w_x[0], b_x[0], lam[0], False)
    hr = _lru_scan(xraw, conv_w, conv_b, w_a[1], b_a[1], w_x[1], b_x[1], lam[1], True)
    return _lru_out(gl, hf, hr, s, mods, w_out)


def _fn_proj_kernel(s_ref, mod_ref, g_ref, wt_ref, cs_ref, yr_ref, yi_ref, ar_scr, ai_scr):
    shift, scale, _ = _mod_rows(mod_ref, 1, 0, 0)
    nm = wt_ref.shape[0]
    gw = nm // FN_GROUPS
    per = TM // FFT_N2
    nj = FN_TB // FFT_N2
    csb = cs_ref[...].astype(BF16)

    def sub(tc, c):
        r0 = pl.multiple_of(tc * TM, TM)
        h = _rms(s_ref[pl.ds(r0, TM), :], g_ref[...]) * (1.0 + scale) + shift
        zt = lax.dot_general(wt_ref[...], h.astype(BF16), (((1,), (1,)), ((), ())),
                             preferred_element_type=F32).astype(BF16)
        for g in range(FN_GROUPS):
            y = jnp.dot(csb, zt[g * gw:(g + 1) * gw, :], preferred_element_type=F32)
            for q in range(per):
                row0 = pl.multiple_of((tc * per + q) * _slab_pitch(nm) + g * gw, SUBLANES)
                ar_scr[pl.ds(row0, gw), :] = y[:gw, q * FFT_N2:(q + 1) * FFT_N2]
                ai_scr[pl.ds(row0, gw), :] = y[gw:, q * FFT_N2:(q + 1) * FFT_N2]
        return c

    lax.fori_loop(0, FN_TB // TM, sub, 0)

    def relayout(m, c):
        yr_ref[m] = ar_scr[pl.ds(m, nj, stride=_slab_pitch(nm)), :]
        yi_ref[m] = ai_scr[pl.ds(m, nj, stride=_slab_pitch(nm)), :]
        return c

    lax.fori_loop(0, nm, relayout, 0, unroll=8)


def _slab_pitch(rows):
    return rows + SUBLANES


def _dft_cos_sin(n, scale):
    k = np.arange(n, dtype=np.int64)
    ang = 2.0 * np.pi * ((k[:, None] * k[None, :]) % n).astype(np.float64) / n
    return np.cos(ang) * scale, np.sin(ang) * scale


def _fn_proj(s, mods, norm_g, w_in):
    t, d = s.shape
    nm = w_in.shape[1]
    gw = nm // FN_GROUPS
    n1 = t // FFT_N2
    nj = FN_TB // FFT_N2
    c, sn = _dft_cos_sin(gw, gw ** -0.5)
    cs = jnp.asarray(np.concatenate([c, -sn], axis=0), F32)
    yspec = pl.BlockSpec((nm, nj, FFT_N2), lambda i: (0, i, 0))
    yshape = jax.ShapeDtypeStruct((nm, n1, FFT_N2), F32)
    return pl.pallas_call(
        _fn_proj_kernel,
        out_shape=(yshape, yshape),
        grid=(t // FN_TB,),
        in_specs=[pl.BlockSpec((FN_TB, d), lambda i: (i, 0)), _full(mods.shape), _full((1, d)),
                  _full((nm, d)), _full(cs.shape)],
        out_specs=(yspec, yspec),
        scratch_shapes=[pltpu.VMEM((nj * _slab_pitch(nm), FFT_N2), F32),
                        pltpu.VMEM((nj * _slab_pitch(nm), FFT_N2), F32)],
        compiler_params=_cparams("arbitrary"),
        name="fourier_proj",
    )(s, mods, norm_g.reshape(1, d), w_in.T.astype(BF16), cs)


def _fn_fft_kernel(yr_ref, yi_ref, m_ref, tc_ref, ts_ref, d_ref, o_ref):
    n1 = yr_ref.shape[1]
    n2 = FFT_N2
    xr = jnp.concatenate([yr_ref[m].astype(BF16) for m in range(FN_CB)], axis=1)
    xi = jnp.concatenate([yi_ref[m].astype(BF16) for m in range(FN_CB)], axis=1)
    a = jnp.dot(m_ref[...].astype(BF16), jnp.concatenate([xr, xi], axis=0),
                preferred_element_type=F32)
    ar = a[:n1]
    ai = a[n1:]
    tc = jnp.concatenate([tc_ref[...]] * FN_CB, axis=1)
    ts = jnp.concatenate([ts_ref[...]] * FN_CB, axis=1)
    br = ar * tc + ai * ts
    bi = ai * tc - ar * ts
    bst = jnp.concatenate(
        [jnp.concatenate([br[:, m * n2:(m + 1) * n2], bi[:, m * n2:(m + 1) * n2]], axis=1)
         for m in range(FN_CB)], axis=0).astype(BF16)
    res = lax.dot_general(d_ref[...].astype(BF16), bst, (((1,), (1,)), ((), ())),
                          preferred_element_type=F32)
    for m in range(FN_CB):
        o_ref[m] = res[:, m * n1:(m + 1) * n1]


def _fn_fft(yr, yi):
    nm, n1, n2 = yr.shape
    t = n1 * n2
    c, sn = _dft_cos_sin(n1, n1 ** -0.5)
    m = jnp.asarray(np.block([[c, sn], [-sn, c]]), F32)
    k1 = np.arange(n1, dtype=np.int64)[:, None]
    t2 = np.arange(n2, dtype=np.int64)[None, :]
    ang = 2.0 * np.pi * ((k1 * t2) % t).astype(np.float64) / t
    tc = jnp.asarray(np.cos(ang), F32)
    ts = jnp.asarray(np.sin(ang), F32)
    c2, s2 = _dft_cos_sin(n2, n2 ** -0.5)
    dm = jnp.asarray(np.concatenate([c2, s2], axis=1), F32)
    yspec = pl.BlockSpec((FN_CB, n1, n2), lambda i: (i, 0, 0))
    return pl.pallas_call(
        _fn_fft_kernel,
        out_shape=jax.ShapeDtypeStruct((nm, n2, n1), F32),
        grid=(nm // FN_CB,),
        in_specs=[yspec, yspec, _full(m.shape), _full(tc.shape), _full(ts.shape), _full(dm.shape)],
        out_specs=pl.BlockSpec((FN_CB, n2, n1), lambda i: (i, 0, 0)),
        compiler_params=_cparams("arbitrary"),
        name="fourier_fft",
    )(yr, yi, m, tc, ts, dm)


def _fn_out_kernel(ft_ref, w_ref, s_ref, mod_ref, o_ref, a_scr):
    gate = _mod_rows(mod_ref, 1, 0, 0)[2]
    nm, nj, n1 = ft_ref.shape

    def relayout(m, c):
        a_scr[pl.ds(m, nj, stride=_slab_pitch(nm)), :] = ft_ref[m]
        return c

    lax.fori_loop(0, nm, relayout, 0, unroll=8)
    for j in range(nj):
        p0 = j * _slab_pitch(nm)
        slab = a_scr[p0:p0 + nm, :].astype(BF16)
        y = lax.dot_general(slab, w_ref[...], (((0,), (0,)), ((), ())), preferred_element_type=F32)
        rows = slice(j * n1, (j + 1) * n1)
        o_ref[rows, :] = s_ref[rows, :] + gate * y


def _fn_out(ft, s, mods, w_out):
    t, d = s.shape
    nm, n2, n1 = ft.shape
    nj = FN_TB // n1
    tok = pl.BlockSpec((FN_TB, d), lambda i: (i, 0))
    return pl.pallas_call(
        _fn_out_kernel,
        out_shape=jax.ShapeDtypeStruct((t, d), F32),
        grid=(t // FN_TB,),
        in_specs=[pl.BlockSpec((nm, nj, n1), lambda i: (0, i, 0)), _full(w_out.shape), tok,
                  _full(mods.shape)],
        out_specs=tok,
        scratch_shapes=[pltpu.VMEM((nj * _slab_pitch(nm), n1), F32)],
        compiler_params=_cparams("arbitrary"),
        name="fourier_out",
    )(ft, w_out.astype(BF16), s, mods)


def _fourier_layer(s, mods, norm_g, w_in, w_out):
    yr, yi = _fn_proj(s, mods, norm_g, w_in)
    return _fn_out(_fn_fft(yr, yi), s, mods, w_out)


def kernel(x, c, ctx, c_ctx, ada_w, ada_b, norm_mix_g, norm_ffn_g, final_norm_g, router_group_w, router_group_b, router_expert_w, router_expert_b, expert_w_gate, expert_w_up, expert_w_down, cm_w_in, cm_v_norm_g, cm_w_s, cm_b_s, cm_w_out, ml_w_in, ml_conv_w, ml_conv_b, ml_gate_b, ml_norm_g, ml_w_out, lru_w_in, lru_conv_w, lru_conv_b, lru_w_a, lru_b_a, lru_w_x, lru_b_x, lru_lambda, lru_w_out, fn_w_in, fn_w_out):
    bsz, seq, d = x.shape
    assert bsz == 1 and ada_w.shape[0] == 4 and ctx.shape[1] == TM
    c_rows = jnp.concatenate([c_ctx[None, :], c, jnp.zeros((SUBLANES - 2, d), F32)], axis=0)
    mods = _ada_table(c_rows, ada_w, ada_b)

    def moe(s, i, ctx_tiles, final_norm=False):
        return _moe_layer(s, mods[i], norm_ffn_g[i], router_group_w[i], router_group_b[i],
                          router_expert_w[i], router_expert_b[i], expert_w_gate, expert_w_up,
                          expert_w_down, i, ctx_tiles, final_norm_g, final_norm)

    s = _chunk_mlp_layer(x[0], ctx[0], mods[0], norm_mix_g[0], cm_w_in[0], cm_v_norm_g[0],
                         cm_w_s[0], cm_b_s[0], cm_w_out[0])
    s = moe(s, 0, 1)
    s = _mlstm_layer(s, mods[1], norm_mix_g[1], ml_w_in[0], ml_conv_w[0], ml_conv_b[0],
                     ml_gate_b[0], ml_norm_g[0], ml_w_out[0])
    s = moe(s, 1, 1)
    s = _rglru_layer(s, mods[2], norm_mix_g[2], lru_w_in[0], lru_conv_w[0], lru_conv_b[0],
                     lru_w_a[0], lru_b_a[0], lru_w_x[0], lru_b_x[0], lru_lambda[0], lru_w_out[0])
    s = moe(s, 2, 0)
    s = _fourier_layer(s, mods[3], norm_mix_g[3], fn_w_in[0], fn_w_out[0])
    s = moe(s, 3, 0, final_norm=True)
    return s[None]
```

```python
import functools
import math

import jax
import jax.numpy as jnp
import numpy as np
from jax import lax
from jax.experimental import pallas as pl
from jax.experimental.pallas import tpu as pltpu

F32 = jnp.float32
BF16 = jnp.bfloat16

EPS = 1e-6
POS_BASE = 10000.0
GRID_W = 64
N_MOD = 6
TM = 256
LANES = 128
SUBLANES = 8
VMEM_LIMIT = 56 * 1024 * 1024

CM_CHUNK = 128
CM_GROUPS = 4
ML_HEADS = 4
ML_DK = 128
ML_DV = 256
ML_CHUNK = 128
LRU_HEADS = 10
LRU_BLOCK = 128
LRU_C = 8.0
FN_GROUPS = 4
FFT_N2 = 128
MOE_GROUPS = 4
MOE_EPG = 8
MOE_EXPERTS = MOE_GROUPS * MOE_EPG
MOE_ROWS_LOG2 = 8
MOE_ROWS = 1 << MOE_ROWS_LOG2
ROUTE_ROWS = 40
FN_TB = 1024
FN_CB = 8
CONV_LEFT = 2

HI = lax.Precision.HIGHEST


def _cparams(*sem):
    return pltpu.CompilerParams(dimension_semantics=sem, vmem_limit_bytes=VMEM_LIMIT)


def _full(shape):
    nd = len(shape)
    return pl.BlockSpec(shape, lambda *_: (0,) * nd)


def _rms(x, g):
    return x * lax.rsqrt(jnp.mean(x * x, axis=-1, keepdims=True) + EPS) * g


def _gelu(x):
    c = math.sqrt(2.0 / math.pi)
    return 0.5 * x * (1.0 + jnp.tanh(c * (x + 0.044715 * (x * x * x))))


def _sigmoid(x):
    return 0.5 * jnp.tanh(0.5 * x) + 0.5


def _silu(x):
    return x * _sigmoid(x)


def _softplus(x):
    return jnp.maximum(x, 0.0) + jnp.log(1.0 + jnp.exp(-jnp.abs(x)))


def _mod_rows(mod_ref, tile, ctx_tiles, first):
    row = jnp.where(tile < ctx_tiles, 0, 1)
    m = mod_ref[pl.ds(row, 1), :]
    d = m.shape[1] // N_MOD
    return tuple(m[:, (first + j) * d:(first + j + 1) * d] for j in range(3))


def _bdot(a, b):
    return jnp.dot(a.astype(BF16), b.astype(BF16), preferred_element_type=F32)


def _split_bf16(x):
    hi = x.astype(BF16)
    return hi, (x - hi.astype(F32)).astype(BF16)


def _dot3(a, b, dims):
    a_hi, a_lo = _split_bf16(a)
    b_hi, b_lo = _split_bf16(b)
    dg = functools.partial(lax.dot_general, dimension_numbers=(dims, ((), ())),
                           preferred_element_type=F32)
    return dg(a_hi, b_hi) + dg(a_hi, b_lo) + dg(a_lo, b_hi)


def _ada_kernel(c_ref, w_ref, b_ref, o_ref):
    c = c_ref[...]
    o_ref[...] = jnp.dot(_silu(c), w_ref[...], precision=HI,
                         preferred_element_type=F32) + b_ref[...]


def _ada_table(c_rows, ada_w, ada_b):
    depth, d, n = ada_w.shape
    tn = 1024
    return pl.pallas_call(
        _ada_kernel,
        out_shape=jax.ShapeDtypeStruct((depth, SUBLANES, n), F32),
        grid=(depth, n // tn),
        in_specs=[_full((SUBLANES, d)),
                  pl.BlockSpec((None, d, tn), lambda i, j: (i, 0, j)),
                  pl.BlockSpec((None, 1, tn), lambda i, j: (i, 0, j))],
        out_specs=pl.BlockSpec((None, SUBLANES, tn), lambda i, j: (i, 0, j)),
        compiler_params=_cparams("arbitrary", "arbitrary"),
        name="ada_table",
    )(c_rows, ada_w, ada_b.reshape(depth, 1, n))


def _pos_tables(seq, d):
    q = d // 4
    freq = jnp.exp(-math.log(POS_BASE) * jnp.arange(q, dtype=F32) / q)
    ar = jnp.arange(seq // GRID_W, dtype=F32)[:, None] * freq
    ac = jnp.arange(GRID_W, dtype=F32)[:, None] * freq
    return (jnp.concatenate([jnp.sin(ar), jnp.cos(ar)], axis=-1),
            jnp.concatenate([jnp.sin(ac), jnp.cos(ac)], axis=-1))


def _cm_kernel(x_ref, ctx_ref, rt_ref, ct_ref, mod_ref, g_ref, win_ref, vg_ref, ws_ref, bs_ref,
               wout_ref, o_ref, p_scr, x_scr):
    i = pl.program_id(0)

    @pl.when(i == 0)
    def _():
        x_scr[...] = ctx_ref[...]

    @pl.when(i > 0)
    def _():
        rows_per_tile = TM // GRID_W
        q2 = rt_ref.shape[1]
        r0 = (i - 1) * rows_per_tile
        rt = jnp.concatenate(
            [jnp.broadcast_to(rt_ref[pl.ds(r0 + j, 1), :], (GRID_W, q2))
             for j in range(rows_per_tile)], axis=0)
        ct = jnp.concatenate([ct_ref[...]] * rows_per_tile, axis=0)
        x_scr[...] = x_ref[...] + jnp.concatenate([rt, ct], axis=1)

    shift, scale, gate = _mod_rows(mod_ref, i, 1, 0)
    x = x_scr[...]
    h = _rms(x, g_ref[...]) * (1.0 + scale) + shift
    z = _gelu(_bdot(h, win_ref[...]))
    w = z.shape[1] // 2
    u = z[:, :w]
    v = _rms(z[:, w:], vg_ref[...]).astype(BF16)
    gw = w // CM_GROUPS
    for c in range(TM // CM_CHUNK):
        r = slice(c * CM_CHUNK, (c + 1) * CM_CHUNK)
        for g in range(CM_GROUPS):
            cs = slice(g * gw, (g + 1) * gw)
            s = jnp.dot(ws_ref[g], v[r, cs], preferred_element_type=F32) + bs_ref[:, g:g + 1]
            p_scr[r, cs] = (u[r, cs] * s).astype(BF16)
    y = jnp.dot(p_scr[...], wout_ref[...], preferred_element_type=F32)
    o_ref[...] = x + gate * y


def _chunk_mlp_layer(x2, ctx2, mods, norm_g, w_in, v_g, w_s, b_s, w_out):
    seq, d = x2.shape
    n_ctx = ctx2.shape[0]
    assert n_ctx == TM and seq % TM == 0 and TM % GRID_W == 0
    n = n_ctx + seq
    w = w_out.shape[0]
    rt, ct = _pos_tables(seq, d)
    return pl.pallas_call(
        _cm_kernel,
        out_shape=jax.ShapeDtypeStruct((n, d), F32),
        grid=(n // TM,),
        in_specs=[pl.BlockSpec((TM, d), lambda i: (jnp.maximum(i - 1, 0), 0)),
                  _full((TM, d)), _full(rt.shape), _full(ct.shape),
                  _full(mods.shape), _full((1, d)), _full(w_in.shape), _full((1, w)),
                  _full(w_s.shape), _full((CM_CHUNK, CM_GROUPS)), _full(w_out.shape)],
        out_specs=pl.BlockSpec((TM, d), lambda i: (i, 0)),
        scratch_shapes=[pltpu.VMEM((TM, w), BF16), pltpu.VMEM((TM, d), F32)],
        compiler_params=_cparams("arbitrary"),
        name="chunk_mlp",
    )(x2, ctx2, rt, ct, mods, norm_g.reshape(1, d), w_in.astype(BF16), v_g.reshape(1, w),
      w_s.astype(BF16), b_s.T, w_out.astype(BF16))


def _store_token_tiles(ref, x):
    rows, d = x.shape
    for j in range(d // LANES):
        ref[pl.ds(j, rows, stride=d // LANES), :] = x[:, j * LANES:(j + 1) * LANES]


def _load_token_tiles(ref):
    chunks = SUBLANES
    rows = ref.shape[0] // chunks
    return jnp.concatenate([ref[pl.ds(j, rows, stride=chunks), :] for j in range(chunks)], axis=1)


def _route_tiles(nt):
    return next(k for k in (5, 4, 2, 1) if nt % k == 0)


def _router_kernel(s_ref, mod_ref, g_ref, rwt_ref, rbt_ref, tri_ref, h_ref, e1_ref, e2_ref, r1_ref,
                   r2_ref, wt_ref, cnt_ref, carry, *, ctx_rows):
    i = pl.program_id(0)
    rows, d = s_ref.shape

    @pl.when(i == 0)
    def _():
        carry[...] = jnp.zeros_like(carry)

    lat = mod_ref[1:2, :]
    shift, scale = lat[:, 3 * d:4 * d], lat[:, 4 * d:5 * d]
    if ctx_rows:
        ctx = mod_ref[0:1, :]
        is_ctx = (i == 0) & (lax.broadcasted_iota(jnp.int32, (rows, 1), 0) < ctx_rows)
        shift = jnp.where(is_ctx, ctx[:, 3 * d:4 * d], shift)
        scale = jnp.where(is_ctx, ctx[:, 4 * d:5 * d], scale)
    h = _rms(s_ref[...], g_ref[...]) * (1.0 + scale) + shift
    _store_token_tiles(h_ref, h)
    logits = _dot3(rwt_ref[...], h, ((1,), (1,))) + rbt_ref[...]
    row = lax.broadcasted_iota(jnp.int32, logits.shape, 0)
    neg = jnp.float32(-jnp.inf)
    big = jnp.int32(1 << 20)
    is_g = row < MOE_GROUPS
    gl = jnp.where(is_g, logits, neg)
    gmax = jnp.max(gl, axis=0, keepdims=True)
    grp = jnp.min(jnp.where(is_g & (gl == gmax), row, big), axis=0, keepdims=True)
    p_grp = 1.0 / jnp.sum(jnp.exp(gl - gmax), axis=0, keepdims=True)
    e_row = row - MOE_GROUPS
    in_grp = (e_row >= 0) & (e_row < MOE_EXPERTS) & ((e_row >> 3) == grp)
    l1 = jnp.where(in_grp, logits, neg)
    v1 = jnp.max(l1, axis=0, keepdims=True)
    i1 = jnp.min(jnp.where(in_grp & (l1 == v1), row, big), axis=0, keepdims=True)
    rest = in_grp & (row != i1)
    l2 = jnp.where(rest, logits, neg)
    v2 = jnp.max(l2, axis=0, keepdims=True)
    i2 = jnp.min(jnp.where(rest & (l2 == v2), row, big), axis=0, keepdims=True)
    e21 = jnp.exp(v2 - v1)
    w1 = p_grp / (1.0 + e21)
    w2 = p_grp * e21 / (1.0 + e21)
    oh1 = (row == i1).astype(F32)
    oh2 = (row == i2).astype(F32)
    oh = oh1 + oh2
    before = jnp.dot(oh.astype(BF16), tri_ref[...], preferred_element_type=F32) + carry[:, 0:1]
    r1_ref[...] = jnp.sum(oh1 * before, axis=0, keepdims=True).astype(jnp.int32)
    r2_ref[...] = jnp.sum(oh2 * before, axis=0, keepdims=True).astype(jnp.int32)
    e1_ref[...] = i1 - MOE_GROUPS
    e2_ref[...] = i2 - MOE_GROUPS
    carry[...] = carry[...] + jnp.sum(oh, axis=1, keepdims=True)
    cnt_ref[...] = carry[...]
    wt_ref[...] = jnp.concatenate([w1, w2, jnp.zeros((LANES - 2, rows), F32)], axis=0).T


def _router(s, mods, norm_g, rg_w, rg_b, re_w, re_b, ctx_tiles):
    n, d = s.shape
    rows = TM * _route_tiles(n // TM)
    steps = n // rows
    pad = ROUTE_ROWS - MOE_GROUPS - MOE_EXPERTS
    rwt = jnp.concatenate([rg_w, re_w, jnp.zeros((d, pad), F32)], axis=1).T
    rbt = jnp.broadcast_to(jnp.concatenate([rg_b, re_b, jnp.zeros((pad,), F32)])[:, None],
                           (ROUTE_ROWS, rows))
    tri = jnp.asarray(np.triu(np.ones((rows, rows), np.float32), 1), BF16)
    assert d == SUBLANES * LANES
    tile = pl.BlockSpec((rows, d), lambda i: (i, 0))
    irow = pl.BlockSpec((None, 1, rows), lambda i: (i, 0, 0))
    ishape = jax.ShapeDtypeStruct((steps, 1, rows), jnp.int32)
    return pl.pallas_call(
        functools.partial(_router_kernel, ctx_rows=ctx_tiles * TM),
        out_shape=(jax.ShapeDtypeStruct((n * SUBLANES, LANES), F32), ishape, ishape, ishape, ishape,
                   jax.ShapeDtypeStruct((n, LANES), F32),
                   jax.ShapeDtypeStruct((ROUTE_ROWS, LANES), F32)),
        grid=(steps,),
        in_specs=[tile, _full(mods.shape), _full((1, d)), _full((ROUTE_ROWS, d)),
                  _full((ROUTE_ROWS, rows)), _full((rows, rows))],
        out_specs=(pl.BlockSpec((rows * SUBLANES, LANES), lambda i: (i, 0)), irow, irow, irow, irow,
                   pl.BlockSpec((rows, LANES), lambda i: (i, 0)), _full((ROUTE_ROWS, LANES))),
        scratch_shapes=[pltpu.VMEM((ROUTE_ROWS, LANES), F32)],
        compiler_params=_cparams("arbitrary"),
        name="moe_router",
    )(s, mods, norm_g.reshape(1, d), rwt, rbt, tri)


def _finalize_kernel(cnt_ref, e1_ref, e2_ref, r1_ref, r2_ref, d1_ref, d2_ref, blk_ref):
    e1 = e1_ref[...]
    e2 = e2_ref[...]
    r1 = r1_ref[...]
    r2 = r2_ref[...]
    d1 = jnp.zeros_like(e1)
    d2 = jnp.zeros_like(e2)
    lane = lax.broadcasted_iota(jnp.int32, blk_ref.shape, 1)
    brow = lane * MOE_ROWS
    sub = lax.broadcasted_iota(jnp.int32, blk_ref.shape, 0)
    be = jnp.zeros(blk_ref.shape, jnp.int32)
    pend = jnp.zeros(blk_ref.shape, jnp.int32)
    ps = jnp.int32(0)
    for e in range(MOE_EXPERTS):
        c = cnt_ref[e]
        pe = ps + lax.shift_left(lax.shift_right_logical(c + (MOE_ROWS - 1), MOE_ROWS_LOG2),
                                 MOE_ROWS_LOG2)
        d1 = jnp.where(e1 == e, ps + r1, d1)
        d2 = jnp.where(e2 == e, ps + r2, d2)
        be = be + (brow >= pe).astype(jnp.int32)
        pend = jnp.where(lane == e, pe, pend)
        ps = pe
    d1_ref[...] = d1
    d2_ref[...] = d2
    n_used = lax.shift_right_logical(ps, MOE_ROWS_LOG2)
    blk_ref[...] = jnp.where(sub == 0, jnp.minimum(be, MOE_EXPERTS - 1),
                             jnp.where(sub == 1, pend, n_used))


def _finalize(counts, e1, e2, r1, r2, nb):
    nbp = (nb + LANES - 1) // LANES * LANES
    whole = pl.BlockSpec(e1.shape, lambda i, c: (0, 0, 0))
    ishape = jax.ShapeDtypeStruct(e1.shape, jnp.int32)
    return pl.pallas_call(
        _finalize_kernel,
        out_shape=(ishape, ishape, jax.ShapeDtypeStruct((SUBLANES, nbp), jnp.int32)),
        grid_spec=pltpu.PrefetchScalarGridSpec(
            num_scalar_prefetch=1,
            grid=(1,),
            in_specs=[whole, whole, whole, whole],
            out_specs=(whole, whole, pl.BlockSpec((SUBLANES, nbp), lambda i, c: (0, 0)))),
        compiler_params=_cparams("arbitrary"),
        name="moe_finalize",
    )(counts, e1, e2, r1, r2)


def _token_copy(src, r, dst, d, sem):
    return pltpu.make_async_copy(src.at[pl.ds(pl.multiple_of(r * SUBLANES, SUBLANES), SUBLANES), :],
                                 dst.at[pl.ds(pl.multiple_of(d * SUBLANES, SUBLANES), SUBLANES), :],
                                 sem)


def _zero_fill_padding(pend_ref, nu_ref, xs_out, zbuf, zsem):
    blk_rows = MOE_ROWS * SUBLANES
    nb = xs_out.shape[0] // blk_rows
    zbuf[...] = jnp.zeros_like(zbuf)

    def block_copy(b):
        r0 = pl.multiple_of(b * blk_rows, blk_rows)
        return pltpu.make_async_copy(zbuf, xs_out.at[pl.ds(r0, blk_rows), :], zsem)

    def seg_last_block(e):
        pe = pend_ref[e]
        prev = pend_ref[e - 1] if e > 0 else 0
        return pe > prev, lax.shift_right_logical(pe, MOE_ROWS_LOG2) - 1

    for e in range(MOE_EXPERTS):
        nonempty, b = seg_last_block(e)

        @pl.when(nonempty)
        def _():
            block_copy(b).start()

    def tail_start(b, c):
        block_copy(b).start()
        return c

    lax.fori_loop(nu_ref[0], nb, tail_start, 0)
    for e in range(MOE_EXPERTS):
        nonempty, b = seg_last_block(e)

        @pl.when(nonempty)
        def _():
            block_copy(b).wait()

    def tail_wait(b, c):
        block_copy(b).wait()
        return c

    lax.fori_loop(nu_ref[0], nb, tail_wait, 0)


def _dispatch_kernel(dest_ref, pend_ref, nu_ref, h_ref, xs_out, sem, zbuf, zsem, *, tiles):
    i = pl.program_id(0)

    @pl.when(i == 0)
    def _():
        _zero_fill_padding(pend_ref, nu_ref, xs_out, zbuf, zsem)

    rows = tiles * TM
    for q in range(tiles):
        base = i * (2 * rows) + q * TM

        def start(r, c):
            _token_copy(h_ref, q * TM + r, xs_out, dest_ref[base + r], sem).start(priority=0)
            _token_copy(h_ref, q * TM + r, xs_out, dest_ref[base + rows + r],
                        sem).start(priority=1)
            return c

        lax.fori_loop(0, TM, start, 0, unroll=8)
    for _ in range(2):
        pltpu.make_async_copy(h_ref, xs_out.at[pl.ds(0, tiles * TM * SUBLANES), :], sem).wait()


def _dispatch(dest, pad_end, n_used, h, n_rows):
    n = h.shape[0] // SUBLANES
    nt = n // TM
    tiles = _route_tiles(nt)
    return pl.pallas_call(
        functools.partial(_dispatch_kernel, tiles=tiles),
        out_shape=jax.ShapeDtypeStruct((n_rows * SUBLANES, LANES), F32),
        grid_spec=pltpu.PrefetchScalarGridSpec(
            num_scalar_prefetch=3,
            grid=(nt // tiles,),
            in_specs=[pl.BlockSpec((tiles * TM * SUBLANES, LANES), lambda i, *_: (i, 0))],
            out_specs=pl.BlockSpec(memory_space=pl.ANY),
            scratch_shapes=[pltpu.SemaphoreType.DMA, pltpu.VMEM((MOE_ROWS * SUBLANES, LANES), F32),
                            pltpu.SemaphoreType.DMA]),
        compiler_params=_cparams("arbitrary"),
        name="moe_dispatch",
    )(dest, pad_end, n_used, h)


X_SLOTS = 3


def _expert_kernel(be_ref, pend_ref, nu_ref, xs_hbm, wg_hbm, wu_hbm, wd_hbm, y_ref,
                   xbuf, wg_f, wu_f, wd_f, wg_s, wu_s, wd_s, xsem, wsem, ord_ref, *, layer):
    b = pl.program_id(0)
    nu = nu_ref[0]
    blk_rows = MOE_ROWS * SUBLANES

    def x_copy(blk, slot):
        r0 = pl.multiple_of(blk * blk_rows, blk_rows)
        return pltpu.make_async_copy(xs_hbm.at[pl.ds(r0, blk_rows), :], xbuf.at[slot],
                                     xsem.at[slot])

    def w_copies(e, slot):
        return (pltpu.make_async_copy(wg_hbm.at[layer, e], wg_f.at[slot], wsem.at[slot]),
                pltpu.make_async_copy(wu_hbm.at[layer, e], wu_f.at[slot], wsem.at[slot]),
                pltpu.make_async_copy(wd_hbm.at[layer, e], wd_f.at[slot], wsem.at[slot]))

    @pl.when(b == 0)
    def _():
        ord_ref[0] = 0
        for j in range(X_SLOTS - 1):
            @pl.when(j < nu)
            def _():
                x_copy(j, j).start()

        @pl.when(nu > 0)
        def _():
            for c in w_copies(be_ref[0], 0):
                c.start()

    ahead = b + (X_SLOTS - 1)

    @pl.when(ahead < nu)
    def _():
        x_copy(ahead, lax.rem(ahead, X_SLOTS)).start()

    used = b < nu
    e = be_ref[b]
    fresh = used & ((b == 0) | (e != be_ref[jnp.maximum(b - 1, 0)]))

    @pl.when(fresh)
    def _():
        k = ord_ref[0]
        slot = lax.rem(k, 2)
        for c in w_copies(e, slot):
            c.wait()
        wg_s[...] = wg_f[slot].astype(BF16)
        wu_s[...] = wu_f[slot].astype(BF16)
        wd_s[...] = wd_f[slot].astype(BF16)
        nxt = lax.shift_right_logical(pend_ref[e], MOE_ROWS_LOG2)

        @pl.when(nxt < nu)
        def _():
            for c in w_copies(be_ref[nxt], 1 - slot):
                c.start()

        ord_ref[0] = k + 1

    @pl.when(used)
    def _():
        slot = lax.rem(b, X_SLOTS)
        x_copy(b, slot).wait()
        x = _load_token_tiles(xbuf.at[slot]).astype(BF16)
        a = jnp.dot(x, wg_s[...], preferred_element_type=F32)
        u = jnp.dot(x, wu_s[...], preferred_element_type=F32)
        _store_token_tiles(y_ref, jnp.dot((_silu(a) * u).astype(BF16), wd_s[...],
                                          preferred_element_type=F32))

    @pl.when(jnp.logical_not(used))
    def _():
        y_ref[...] = jnp.zeros_like(y_ref)


def _experts(blk_expert, pad_end, n_used, xs, w_gate, w_up, w_down, layer):
    d, hid = w_gate.shape[2:]
    blk_rows = MOE_ROWS * SUBLANES
    nb = xs.shape[0] // blk_rows
    hbm = pl.BlockSpec(memory_space=pl.ANY)
    return pl.pallas_call(
        functools.partial(_expert_kernel, layer=layer),
        out_shape=jax.ShapeDtypeStruct(xs.shape, F32),
        grid_spec=pltpu.PrefetchScalarGridSpec(
            num_scalar_prefetch=3,
            grid=(nb,),
            in_specs=[hbm, hbm, hbm, hbm],
            out_specs=pl.BlockSpec((blk_rows, LANES), lambda b, *_: (b, 0)),
            scratch_shapes=[pltpu.VMEM((X_SLOTS, blk_rows, LANES), F32),
                            pltpu.VMEM((2, d, hid), F32), pltpu.VMEM((2, d, hid), F32),
                            pltpu.VMEM((2, hid, d), F32),
                            pltpu.VMEM((d, hid), BF16), pltpu.VMEM((d, hid), BF16),
                            pltpu.VMEM((hid, d), BF16),
                            pltpu.SemaphoreType.DMA((X_SLOTS,)), pltpu.SemaphoreType.DMA((2,)),
                            pltpu.SMEM((1,), jnp.int32)]),
        compiler_params=_cparams("arbitrary"),
        name="moe_experts",
    )(blk_expert, pad_end, n_used, xs, w_gate, w_up, w_down)


def _combine_kernel(dest_ref, s_ref, wt_ref, mod_ref, fg_ref, ys_ref, o_ref, ybuf, sem,
                    *, ctx_tiles, final_norm, tiles):
    i = pl.program_id(0)
    slot = i % 2
    rows = tiles * TM

    def gather(tile, slot):
        tile = jnp.asarray(tile, jnp.int32)
        base = lax.div(tile, tiles) * (2 * rows) + lax.rem(tile, tiles) * TM

        def start(r, c):
            _token_copy(ys_ref, dest_ref[base + r], ybuf.at[slot, 0], r,
                        sem.at[slot]).start(priority=0)
            _token_copy(ys_ref, dest_ref[base + rows + r], ybuf.at[slot, 1], r,
                        sem.at[slot]).start(priority=1)
            return c

        lax.fori_loop(0, TM, start, 0, unroll=8)

    @pl.when(i == 0)
    def _():
        gather(0, 0)

    @pl.when(i + 1 < pl.num_programs(0))
    def _():
        gather(i + 1, 1 - slot)

    for k in range(2):
        pltpu.make_async_copy(ys_ref.at[pl.ds(0, TM * SUBLANES), :], ybuf.at[slot, k],
                              sem.at[slot]).wait()
    gate = _mod_rows(mod_ref, i, ctx_tiles, 3)[2]
    wt = wt_ref[...]
    y = (wt[:, 0:1] * _load_token_tiles(ybuf.at[slot, 0])
         + wt[:, 1:2] * _load_token_tiles(ybuf.at[slot, 1]))
    out = s_ref[...] + gate * y
    if final_norm:
        out = _rms(out, fg_ref[...])
    o_ref[...] = out


def _combine(dest, s, wts, mods, final_g, ys, ctx_tiles, final_norm):
    n, d = s.shape
    return pl.pallas_call(
        functools.partial(_combine_kernel, ctx_tiles=ctx_tiles, final_norm=final_norm,
                          tiles=_route_tiles(n // TM)),
        out_shape=jax.ShapeDtypeStruct((n, d), F32),
        grid_spec=pltpu.PrefetchScalarGridSpec(
            num_scalar_prefetch=1,
            grid=(n // TM,),
            in_specs=[pl.BlockSpec((TM, d), lambda i, dst: (i, 0)),
                      pl.BlockSpec((TM, LANES), lambda i, dst: (i, 0)),
                      pl.BlockSpec(mods.shape, lambda i, dst: (0, 0)),
                      pl.BlockSpec((1, d), lambda i, dst: (0, 0)),
                      pl.BlockSpec(memory_space=pl.ANY)],
            out_specs=pl.BlockSpec((TM, d), lambda i, dst: (i, 0)),
            scratch_shapes=[pltpu.VMEM((2, 2, TM * SUBLANES, LANES), F32),
                            pltpu.SemaphoreType.DMA((2,))]),
        compiler_params=_cparams("arbitrary"),
        name="moe_combine",
    )(dest, s, wts, mods, final_g.reshape(1, d), ys)


def _moe_layer(s, mods, norm_g, rg_w, rg_b, re_w, re_b, w_gate, w_up, w_down, layer, ctx_tiles,
               final_g, final_norm):
    n, d = s.shape
    h, e1, e2, r1, r2, wts, cnt = _router(s, mods, norm_g, rg_w, rg_b, re_w, re_b, ctx_tiles)
    counts = cnt[MOE_GROUPS:MOE_GROUPS + MOE_EXPERTS, 0].astype(jnp.int32)
    nb = (2 * n + MOE_EXPERTS * (MOE_ROWS - 1)) // MOE_ROWS + 1
    d1, d2, blk = _finalize(counts, e1, e2, r1, r2, nb)
    dest = jnp.concatenate([d1, d2], axis=1).reshape(2 * n)
    n_used = blk[2, :1]
    pad_end = blk[1, :MOE_EXPERTS]
    xs = _dispatch(dest, pad_end, n_used, h, nb * MOE_ROWS)
    ys = _experts(blk[0, :nb], pad_end, n_used, xs, w_gate, w_up, w_down, layer)
    return _combine(dest, s, wts, mods, final_g, ys, ctx_tiles, final_norm)


def _conv_tile(x, prev_ref, next_ref, has_prev, has_next, w_ref, b_ref):
    rows = x.shape[0]
    S = SUBLANES
    ridx = lax.broadcasted_iota(jnp.int32, (S, x.shape[1]), 0)
    pm = jnp.where(has_prev, 1.0, 0.0)
    nm = jnp.where(has_next, 1.0, 0.0)
    p2 = prev_ref[S - 2:S - 1, :] * pm
    p1 = prev_ref[S - 1:S, :] * pm
    n1 = next_ref[0:1, :] * nm

    def fix_head(rolled, head):
        return jnp.concatenate([head(rolled[:S]), rolled[S:]], axis=0)

    xm1 = fix_head(pltpu.roll(x, 1, axis=0), lambda g: jnp.where(ridx == 0, p1, g))
    xm2 = fix_head(pltpu.roll(x, 2, axis=0),
                   lambda g: jnp.where(ridx == 0, p2, jnp.where(ridx == 1, p1, g)))
    xp1 = pltpu.roll(x, rows - 1, axis=0)
    xp1 = jnp.concatenate([xp1[:rows - S], jnp.where(ridx == S - 1, n1, xp1[rows - S:])], axis=0)
    return (xm2 * w_ref[0:1, :] + xm1 * w_ref[1:2, :] + x * w_ref[2:3, :]
            + xp1 * w_ref[3:4, :] + b_ref[...])


def _ml_proj_kernel(s_ref, mod_ref, g_ref, w_ref, wg_ref, gb_ref, qk_ref, v_ref, o_ref, gt_ref):
    i = pl.program_id(0)
    shift, scale, _ = _mod_rows(mod_ref, i, 1, 0)
    h = _rms(s_ref[...], g_ref[...]) * (1.0 + scale) + shift
    z = _bdot(h, w_ref[...])
    nqk = qk_ref.shape[1]
    nv = v_ref.shape[1]
    qk_ref[...] = z[:, :nqk]
    v_ref[...] = z[:, nqk:nqk + nv].astype(BF16)
    o_ref[...] = z[:, nqk + nv:].astype(BF16)
    pre = _dot3(h, wg_ref[...], ((1,), (0,))) + gb_ref[...]
    lane = lax.broadcasted_iota(jnp.int32, pre.shape, 1)
    is_forget = ((lane >> 2) & 1) == 1
    gt_ref[...] = jnp.where(is_forget, -_softplus(-pre), pre)


def _ml_proj(s, mods, norm_g, w_in, gate_b):
    n, d = s.shape
    nqk = 2 * ML_HEADS * ML_DK
    nv = ML_HEADS * ML_DV
    n_main = nqk + 2 * nv
    n_gate = w_in.shape[1] - n_main
    w_main = w_in[:, :n_main].astype(BF16)
    w_gate = jnp.concatenate([w_in[:, n_main:], jnp.zeros((d, LANES - n_gate), F32)], axis=1)
    gb = jnp.concatenate([gate_b.reshape(n_gate), jnp.zeros((LANES - n_gate,), F32)]).reshape(1, LANES)
    tile = lambda w: pl.BlockSpec((TM, w), lambda i: (i, 0))
    return pl.pallas_call(
        _ml_proj_kernel,
        out_shape=(jax.ShapeDtypeStruct((n, nqk), F32), jax.ShapeDtypeStruct((n, nv), BF16),
                   jax.ShapeDtypeStruct((n, nv), BF16), jax.ShapeDtypeStruct((n, LANES), F32)),
        grid=(n // TM,),
        in_specs=[tile(d), _full(mods.shape), _full((1, d)), _full(w_main.shape),
                  _full((d, LANES)), _full((1, LANES))],
        out_specs=(tile(nqk), tile(nv), tile(nv), tile(LANES)),
        compiler_params=_cparams("arbitrary"),
        name="mlstm_proj",
    )(s, mods, norm_g.reshape(1, d), w_main, w_gate, gb)


def _ml_chunk_index(j, n_chunks, ctx_chunks, reverse):
    if not reverse:
        return j
    return jnp.where(j < ctx_chunks, ctx_chunks - 1 - j, n_chunks - 1 + ctx_chunks - j)


def _ml_rec_kernel(qk_ref, qkp_ref, qkn_ref, v_ref, gt_ref, gtt_ref, cw_ref, cb_ref, o_ref,
                   c_scr, n_scr, m_scr, *, reverse, n_chunks, ctx_chunks):
    j = pl.program_id(0)
    c = _ml_chunk_index(j, n_chunks, ctx_chunks, reverse)

    @pl.when(j == 0)
    def _():
        c_scr[...] = jnp.zeros_like(c_scr)
        n_scr[...] = jnp.zeros_like(n_scr)
        m_scr[...] = jnp.zeros_like(m_scr)

    has_prev = (c != 0) & (c != ctx_chunks)
    has_next = (c != ctx_chunks - 1) & (c != n_chunks - 1)
    qk = _silu(_conv_tile(qk_ref[...], qkp_ref, qkn_ref, has_prev, has_next, cw_ref, cb_ref))
    L = ML_CHUNK
    ri = lax.broadcasted_iota(jnp.int32, (L, L), 0)
    ci = lax.broadcasted_iota(jnp.int32, (L, L), 1)
    past = (ci >= ri) if reverse else (ci <= ri)
    pastf = past.astype(F32)
    gt = gt_ref[...]
    gtt = gtt_ref[...]
    b_col = jnp.dot(pastf, gt, precision=HI, preferred_element_type=F32)
    b_row = jnp.dot(gtt, pastf.T, precision=HI, preferred_element_type=F32)
    last = 0 if reverse else L - 1
    dbase = 8 if reverse else 0
    nq = ML_HEADS * ML_DK
    for hd in range(ML_HEADS):
        cl = dbase + hd
        cf = dbase + 4 + hd
        q = qk[:, hd * ML_DK:(hd + 1) * ML_DK] * (ML_DK ** -0.5)
        k = qk[:, nq + hd * ML_DK:nq + (hd + 1) * ML_DK]
        v = v_ref[:, hd * ML_DV:(hd + 1) * ML_DV]
        li_c = gt[:, cl:cl + 1]
        li_r = gtt[cl:cl + 1, :]
        b_c = b_col[:, cf:cf + 1]
        b_r = b_row[cf:cf + 1, :]
        g = b_r[:, last:last + 1]
        m0 = m_scr[hd:hd + 1, 0:1]
        c0 = c_scr[hd]
        n0 = n_scr[hd:hd + 1, :]
        a_c = g - b_c + li_c
        a_r = g - b_r + li_r
        m_loc = jnp.max(a_r, axis=-1, keepdims=True)
        inter = b_c + m0
        dlog = jnp.where(past, b_c - b_r + li_r, -jnp.inf)
        m = jnp.maximum(inter, jnp.max(dlog, axis=-1, keepdims=True))
        qb = q.astype(BF16)
        sc = lax.dot_general(qb, k.astype(BF16), (((1,), (1,)), ((), ())),
                             preferred_element_type=F32) * jnp.exp(dlog - m)
        w_inter = jnp.exp(inter - m)
        num = (jnp.dot(sc.astype(BF16), v, preferred_element_type=F32)
               + w_inter * jnp.dot(qb, c0.astype(BF16), preferred_element_type=F32))
        den = (jnp.sum(sc, axis=-1, keepdims=True)
               + w_inter * jnp.sum(q * n0, axis=-1, keepdims=True))
        o_ref[:, hd * ML_DV:(hd + 1) * ML_DV] = (
            num / jnp.maximum(jnp.abs(den), jnp.exp(-m))).astype(BF16)
        m_new = jnp.maximum(g + m0, m_loc)
        dec = jnp.exp(g + m0 - m_new)
        scl = jnp.exp(m_loc - m_new)
        kw = k * jnp.exp(a_c - m_loc)
        c_scr[hd] = dec * c0 + scl * jnp.dot(kw.T.astype(BF16), v, preferred_element_type=F32)
        n_scr[hd:hd + 1, :] = dec * n0 + scl * jnp.sum(kw, axis=0, keepdims=True)
        m_scr[hd:hd + 1, :] = jnp.broadcast_to(m_new, (1, LANES))


def _ml_rec(qk, v, gt, gtt, conv_w, conv_b, reverse):
    n, nqk = qk.shape
    nv = v.shape[1]
    L = ML_CHUNK
    nc = n // L
    cc = TM // L
    hb = L // SUBLANES
    idx = lambda j: _ml_chunk_index(j, nc, cc, reverse)
    last8 = n // SUBLANES - 1
    return pl.pallas_call(
        functools.partial(_ml_rec_kernel, reverse=reverse, n_chunks=nc, ctx_chunks=cc),
        out_shape=jax.ShapeDtypeStruct((n, nv), BF16),
        grid=(nc,),
        in_specs=[pl.BlockSpec((L, nqk), lambda j: (idx(j), 0)),
                  pl.BlockSpec((SUBLANES, nqk), lambda j: (jnp.maximum(idx(j) * hb - 1, 0), 0)),
                  pl.BlockSpec((SUBLANES, nqk), lambda j: (jnp.minimum((idx(j) + 1) * hb, last8), 0)),
                  pl.BlockSpec((L, nv), lambda j: (idx(j), 0)),
                  pl.BlockSpec((L, LANES), lambda j: (idx(j), 0)),
                  pl.BlockSpec((2 * SUBLANES, L), lambda j: (0, idx(j))),
                  _full((4, nqk)), _full((1, nqk))],
        out_specs=pl.BlockSpec((L, nv), lambda j: (idx(j), 0)),
        scratch_shapes=[pltpu.VMEM((ML_HEADS, ML_DK, ML_DV), F32),
                        pltpu.VMEM((SUBLANES, ML_DK), F32),
                        pltpu.VMEM((SUBLANES, LANES), F32)],
        compiler_params=_cparams("arbitrary"),
        name="mlstm_rev" if reverse else "mlstm_fwd",
    )(qk, qk, qk, v, gt, gtt, conv_w, conv_b.reshape(1, nqk))


def _ml_out_kernel(hf_ref, hr_ref, o_ref, s_ref, mod_ref, ng_ref, w_ref, out_ref, p_scr):
    i = pl.program_id(0)
    gate = _mod_rows(mod_ref, i, 1, 0)[2]
    hs = hf_ref[...].astype(F32) + hr_ref[...].astype(F32)
    sig = _sigmoid(o_ref[...].astype(F32))
    ng = ng_ref[...]
    for hd in range(ML_HEADS):
        cs = slice(hd * ML_DV, (hd + 1) * ML_DV)
        seg = hs[:, cs]
        hn = seg * lax.rsqrt(jnp.mean(seg * seg, axis=-1, keepdims=True) + EPS) * ng[:, cs]
        p_scr[:, cs] = (hn * sig[:, cs]).astype(BF16)
    y = jnp.dot(p_scr[...], w_ref[...], preferred_element_type=F32)
    out_ref[...] = s_ref[...] + gate * y


def _ml_out(hf, hr, o, s, mods, norm_g, w_out):
    n, d = s.shape
    nv = hf.shape[1]
    tile = lambda w: pl.BlockSpec((TM, w), lambda i: (i, 0))
    return pl.pallas_call(
        _ml_out_kernel,
        out_shape=jax.ShapeDtypeStruct((n, d), F32),
        grid=(n // TM,),
        in_specs=[tile(nv), tile(nv), tile(nv), tile(d), _full(mods.shape), _full((1, nv)),
                  _full(w_out.shape)],
        out_specs=tile(d),
        scratch_shapes=[pltpu.VMEM((TM, nv), BF16)],
        compiler_params=_cparams("arbitrary"),
        name="mlstm_out",
    )(hf, hr, o, s, mods, norm_g.reshape(1, nv), w_out.astype(BF16))


def _mlstm_layer(s, mods, norm_g, w_in, conv_w, conv_b, gate_b, ml_norm_g, w_out):
    qk, v, o, gt = _ml_proj(s, mods, norm_g, w_in, gate_b)
    gtt = gt[:, :2 * SUBLANES].T
    hf = _ml_rec(qk, v, gt, gtt, conv_w, conv_b, False)
    hr = _ml_rec(qk, v, gt, gtt, conv_w, conv_b, True)
    return _ml_out(hf, hr, o, s, mods, ml_norm_g, w_out)


def _lru_proj_kernel(s_ref, mod_ref, g_ref, w_ref, gl_ref, xr_ref):
    i = pl.program_id(0)
    shift, scale, _ = _mod_rows(mod_ref, i, 1, 0)
    h = _rms(s_ref[...], g_ref[...]) * (1.0 + scale) + shift
    z = _bdot(h, w_ref[...])
    w = gl_ref.shape[1]
    gl_ref[...] = _gelu(z[:, :w]).astype(BF16)
    xr_ref[...] = z[:, w:]


def _lru_proj(s, mods, norm_g, w_in):
    n, d = s.shape
    w = w_in.shape[1] // 2
    tile = lambda c: pl.BlockSpec((TM, c), lambda i: (i, 0))
    return pl.pallas_call(
        _lru_proj_kernel,
        out_shape=(jax.ShapeDtypeStruct((n, w), BF16), jax.ShapeDtypeStruct((n, w), F32)),
        grid=(n // TM,),
        in_specs=[tile(d), _full(mods.shape), _full((1, d)), _full(w_in.shape)],
        out_specs=(tile(w), tile(w)),
        compiler_params=_cparams("arbitrary"),
        name="rglru_proj",
    )(s, mods, norm_g.reshape(1, d), w_in.astype(BF16))


def _lru_tile_index(j, n_tiles, reverse):
    if not reverse:
        return j
    return jnp.where(j == 0, 0, n_tiles - j)


def _lru_scan_kernel(x_ref, xp_ref, xn_ref, cw_ref, cb_ref, wg_ref, ba_ref, bx_ref, lam_ref,
                     o_ref, a_scr, u_scr, carry, *, reverse, n_tiles):
    j = pl.program_id(0)
    t = _lru_tile_index(j, n_tiles, reverse)

    @pl.when(j == 0)
    def _():
        carry[...] = jnp.zeros_like(carry)

    has_prev = t > 1
    has_next = (t != 0) & (t != n_tiles - 1)
    xr = _conv_tile(x_ref[...], xp_ref, xn_ref, has_prev, has_next, cw_ref, cb_ref)
    sp = _softplus(-lam_ref[...])
    B = LRU_BLOCK
    for hd in range(LRU_HEADS):
        cs = slice(hd * B, (hd + 1) * B)
        xh = xr[:, cs]
        y = jnp.dot(xh.astype(BF16), wg_ref[hd], preferred_element_type=F32)
        r = _sigmoid(y[:, :B] + ba_ref[:, cs])
        ig = _sigmoid(y[:, B:] + bx_ref[:, cs])
        log_a = -LRU_C * r * sp[:, cs]
        a = jnp.exp(log_a)
        a_scr[:, cs] = a
        u_scr[:, cs] = jnp.sqrt(1.0 - a * a) * (ig * xh)

    S = SUBLANES
    w = a_scr.shape[1]
    sidx = lax.broadcasted_iota(jnp.int32, (S, w), 0)

    def group(gi, c):
        g = (TM // S - 1 - gi) if reverse else gi
        r0 = pl.multiple_of(g * S, S)
        a = a_scr[pl.ds(r0, S), :]
        u = u_scr[pl.ds(r0, S), :]
        for sft in (1, 2, 4):
            if reverse:
                ok = sidx < S - sft
                a_e = pltpu.roll(a, S - sft, axis=0)
                u_e = pltpu.roll(u, S - sft, axis=0)
            else:
                ok = sidx >= sft
                a_e = pltpu.roll(a, sft, axis=0)
                u_e = pltpu.roll(u, sft, axis=0)
            u = jnp.where(ok, a * u_e + u, u)
            a = jnp.where(ok, a * a_e, a)
        hcur = a * carry[...] + u
        u_scr[pl.ds(r0, S), :] = hcur
        edge = 0 if reverse else S - 1
        carry[...] = jnp.broadcast_to(hcur[edge:edge + 1, :], (S, w))
        return c

    lax.fori_loop(0, TM // S, group, 0)
    o_ref[...] = u_scr[...].astype(BF16)


def _lru_scan(xraw, conv_w, conv_b, w_a, b_a, w_x, b_x, lam, reverse):
    n, w = xraw.shape
    nt = n // TM
    hb = TM // SUBLANES
    idx = lambda j: _lru_tile_index(j, nt, reverse)
    last8 = n // SUBLANES - 1
    wg = jnp.concatenate([w_a, w_x], axis=-1).astype(BF16)
    return pl.pallas_call(
        functools.partial(_lru_scan_kernel, reverse=reverse, n_tiles=nt),
        out_shape=jax.ShapeDtypeStruct((n, w), BF16),
        grid=(nt,),
        in_specs=[pl.BlockSpec((TM, w), lambda j: (idx(j), 0)),
                  pl.BlockSpec((SUBLANES, w), lambda j: (jnp.maximum(idx(j) * hb - 1, 0), 0)),
                  pl.BlockSpec((SUBLANES, w), lambda j: (jnp.minimum((idx(j) + 1) * hb, last8), 0)),
                  _full((4, w)), _full((1, w)), _full(wg.shape), _full((1, w)), _full((1, w)),
                  _full((1, w))],
        out_specs=pl.BlockSpec((TM, w), lambda j: (idx(j), 0)),
        scratch_shapes=[pltpu.VMEM((TM, w), F32), pltpu.VMEM((TM, w), F32),
                        pltpu.VMEM((SUBLANES, w), F32)],
        compiler_params=_cparams("arbitrary"),
        name="rglru_rev" if reverse else "rglru_fwd",
    )(xraw, xraw, xraw, conv_w, conv_b.reshape(1, w), wg, b_a.reshape(1, w), b_x.reshape(1, w),
      lam.reshape(1, w))


def _lru_out_kernel(gl_ref, hf_ref, hr_ref, s_ref, mod_ref, w_ref, out_ref):
    gate = _mod_rows(mod_ref, 1, 0, 0)[2]
    p = gl_ref[...].astype(F32) * (hf_ref[...].astype(F32) + hr_ref[...].astype(F32))
    out_ref[...] = s_ref[...] + gate * _bdot(p, w_ref[...])


def _lru_out(gl, hf, hr, s, mods, w_out):
    n, d = s.shape
    w = gl.shape[1]
    lat = lambda c: pl.BlockSpec((TM, c), lambda i: (i + 1, 0))
    return pl.pallas_call(
        _lru_out_kernel,
        out_shape=jax.ShapeDtypeStruct((n - TM, d), F32),
        grid=(n // TM - 1,),
        in_specs=[lat(w), lat(w), lat(w), lat(d), _full(mods.shape), _full(w_out.shape)],
        out_specs=pl.BlockSpec((TM, d), lambda i: (i, 0)),
        compiler_params=_cparams("arbitrary"),
        name="rglru_out",
    )(gl, hf, hr, s, mods, w_out.astype(BF16))


def _rglru_layer(s, mods, norm_g, w_in, conv_w, conv_b, w_a, b_a, w_x, b_x, lam, w_out):
    gl, xraw = _lru_proj(s, mods, norm_g, w_in)
    hf = _lru_scan(xraw, conv_w, conv_b, w_a[0], b_a[0], w_x[0], b_x[0], lam[0], False)
    hr = _lru_scan(xraw, conv_w, conv_b, w_a[1], b_a[1], w_x[1], b_x[1], lam[1], True)
    return _lru_out(gl, hf, hr, s, mods, w_out)


def _fn_proj_kernel(s_ref, mod_ref, g_ref, wt_ref, cs_ref, yr_ref, yi_ref, ar_scr, ai_scr):
    shift, scale, _ = _mod_rows(mod_ref, 1, 0, 0)
    nm = wt_ref.shape[0]
    gw = nm // FN_GROUPS
    per = TM // FFT_N2
    nj = FN_TB // FFT_N2
    csb = cs_ref[...].astype(BF16)

    def sub(tc, c):
        r0 = pl.multiple_of(tc * TM, TM)
        h = _rms(s_ref[pl.ds(r0, TM), :], g_ref[...]) * (1.0 + scale) + shift
        zt = lax.dot_general(wt_ref[...], h.astype(BF16), (((1,), (1,)), ((), ())),
                             preferred_element_type=F32).astype(BF16)
        for g in range(FN_GROUPS):
            y = jnp.dot(csb, zt[g * gw:(g + 1) * gw, :], preferred_element_type=F32)
            for q in range(per):
                row0 = pl.multiple_of((tc * per + q) * _slab_pitch(nm) + g * gw, SUBLANES)
                ar_scr[pl.ds(row0, gw), :] = y[:gw, q * FFT_N2:(q + 1) * FFT_N2]
                ai_scr[pl.ds(row0, gw), :] = y[gw:, q * FFT_N2:(q + 1) * FFT_N2]
        return c

    lax.fori_loop(0, FN_TB // TM, sub, 0)

    def relayout(m, c):
        yr_ref[m] = ar_scr[pl.ds(m, nj, stride=_slab_pitch(nm)), :]
        yi_ref[m] = ai_scr[pl.ds(m, nj, stride=_slab_pitch(nm)), :]
        return c

    lax.fori_loop(0, nm, relayout, 0, unroll=8)


def _slab_pitch(rows):
    return rows + SUBLANES


def _dft_cos_sin(n, scale):
    k = np.arange(n, dtype=np.int64)
    ang = 2.0 * np.pi * ((k[:, None] * k[None, :]) % n).astype(np.float64) / n
    return np.cos(ang) * scale, np.sin(ang) * scale


def _fn_proj(s, mods, norm_g, w_in):
    t, d = s.shape
    nm = w_in.shape[1]
    gw = nm // FN_GROUPS
    n1 = t // FFT_N2
    nj = FN_TB // FFT_N2
    c, sn = _dft_cos_sin(gw, gw ** -0.5)
    cs = jnp.asarray(np.concatenate([c, -sn], axis=0), F32)
    yspec = pl.BlockSpec((nm, nj, FFT_N2), lambda i: (0, i, 0))
    yshape = jax.ShapeDtypeStruct((nm, n1, FFT_N2), F32)
    return pl.pallas_call(
        _fn_proj_kernel,
        out_shape=(yshape, yshape),
        grid=(t // FN_TB,),
        in_specs=[pl.BlockSpec((FN_TB, d), lambda i: (i, 0)), _full(mods.shape), _full((1, d)),
                  _full((nm, d)), _full(cs.shape)],
        out_specs=(yspec, yspec),
        scratch_shapes=[pltpu.VMEM((nj * _slab_pitch(nm), FFT_N2), F32),
                        pltpu.VMEM((nj * _slab_pitch(nm), FFT_N2), F32)],
        compiler_params=_cparams("arbitrary"),
        name="fourier_proj",
    )(s, mods, norm_g.reshape(1, d), w_in.T.astype(BF16), cs)


def _fn_fft_kernel(yr_ref, yi_ref, m_ref, tc_ref, ts_ref, d_ref, o_ref):
    n1 = yr_ref.shape[1]
    n2 = FFT_N2
    xr = jnp.concatenate([yr_ref[m].astype(BF16) for m in range(FN_CB)], axis=1)
    xi = jnp.concatenate([yi_ref[m].astype(BF16) for m in range(FN_CB)], axis=1)
    a = jnp.dot(m_ref[...].astype(BF16), jnp.concatenate([xr, xi], axis=0),
                preferred_element_type=F32)
    ar = a[:n1]
    ai = a[n1:]
    tc = jnp.concatenate([tc_ref[...]] * FN_CB, axis=1)
    ts = jnp.concatenate([ts_ref[...]] * FN_CB, axis=1)
    br = ar * tc + ai * ts
    bi = ai * tc - ar * ts
    bst = jnp.concatenate(
        [jnp.concatenate([br[:, m * n2:(m + 1) * n2], bi[:, m * n2:(m + 1) * n2]], axis=1)
         for m in range(FN_CB)], axis=0).astype(BF16)
    res = lax.dot_general(d_ref[...].astype(BF16), bst, (((1,), (1,)), ((), ())),
                          preferred_element_type=F32)
    for m in range(FN_CB):
        o_ref[m] = res[:, m * n1:(m + 1) * n1]


def _fn_fft(yr, yi):
    nm, n1, n2 = yr.shape
    t = n1 * n2
    c, sn = _dft_cos_sin(n1, n1 ** -0.5)
    m = jnp.asarray(np.block([[c, sn], [-sn, c]]), F32)
    k1 = np.arange(n1, dtype=np.int64)[:, None]
    t2 = np.arange(n2, dtype=np.int64)[None, :]
    ang = 2.0 * np.pi * ((k1 * t2) % t).astype(np.float64) / t
    tc = jnp.asarray(np.cos(ang), F32)
    ts = jnp.asarray(np.sin(ang), F32)
    c2, s2 = _dft_cos_sin(n2, n2 ** -0.5)
    dm = jnp.asarray(np.concatenate([c2, s2], axis=1), F32)
    yspec = pl.BlockSpec((FN_CB, n1, n2), lambda i: (i, 0, 0))
    return pl.pallas_call(
        _fn_fft_kernel,
        out_shape=jax.ShapeDtypeStruct((nm, n2, n1), F32),
        grid=(nm // FN_CB,),
        in_specs=[yspec, yspec, _full(m.shape), _full(tc.shape), _full(ts.shape), _full(dm.shape)],
        out_specs=pl.BlockSpec((FN_CB, n2, n1), lambda i: (i, 0, 0)),
        compiler_params=_cparams("arbitrary"),
        name="fourier_fft",
    )(yr, yi, m, tc, ts, dm)


def _fn_out_kernel(ft_ref, w_ref, s_ref, mod_ref, o_ref, a_scr):
    gate = _mod_rows(mod_ref, 1, 0, 0)[2]
    nm, nj, n1 = ft_ref.shape

    def relayout(m, c):
        a_scr[pl.ds(m, nj, stride=_slab_pitch(nm)), :] = ft_ref[m]
        return c

    lax.fori_loop(0, nm, relayout, 0, unroll=8)
    for j in range(nj):
        p0 = j * _slab_pitch(nm)
        slab = a_scr[p0:p0 + nm, :].astype(BF16)
        y = lax.dot_general(slab, w_ref[...], (((0,), (0,)), ((), ())), preferred_element_type=F32)
        rows = slice(j * n1, (j + 1) * n1)
        o_ref[rows, :] = s_ref[rows, :] + gate * y


def _fn_out(ft, s, mods, w_out):
    t, d = s.shape
    nm, n2, n1 = ft.shape
    nj = FN_TB // n1
    tok = pl.BlockSpec((FN_TB, d), lambda i: (i, 0))
    return pl.pallas_call(
        _fn_out_kernel,
        out_shape=jax.ShapeDtypeStruct((t, d), F32),
        grid=(t // FN_TB,),
        in_specs=[pl.BlockSpec((nm, nj, n1), lambda i: (0, i, 0)), _full(w_out.shape), tok,
                  _full(mods.shape)],
        out_specs=tok,
        scratch_shapes=[pltpu.VMEM((nj * _slab_pitch(nm), n1), F32)],
        compiler_params=_cparams("arbitrary"),
        name="fourier_out",
    )(ft, w_out.astype(BF16), s, mods)


def _fourier_layer(s, mods, norm_g, w_in, w_out):
    yr, yi = _fn_proj(s, mods, norm_g, w_in)
    return _fn_out(_fn_fft(yr, yi), s, mods, w_out)


def kernel(x, c, ctx, c_ctx, ada_w, ada_b, norm_mix_g, norm_ffn_g, final_norm_g, router_group_w, router_group_b, router_expert_w, router_expert_b, expert_w_gate, expert_w_up, expert_w_down, cm_w_in, cm_v_norm_g, cm_w_s, cm_b_s, cm_w_out, ml_w_in, ml_conv_w, ml_conv_b, ml_gate_b, ml_norm_g, ml_w_out, lru_w_in, lru_conv_w, lru_conv_b, lru_w_a, lru_b_a, lru_w_x, lru_b_x, lru_lambda, lru_w_out, fn_w_in, fn_w_out):
    bsz, seq, d = x.shape
    assert bsz == 1 and ada_w.shape[0] == 4 and ctx.shape[1] == TM
    c_rows = jnp.concatenate([c_ctx[None, :], c, jnp.zeros((SUBLANES - 2, d), F32)], axis=0)
    mods = _ada_table(c_rows, ada_w, ada_b)

    def moe(s, i, ctx_tiles, final_norm=False):
        return _moe_layer(s, mods[i], norm_ffn_g[i], router_group_w[i], router_group_b[i],
                          router_expert_w[i], router_expert_b[i], expert_w_gate, expert_w_up,
                          expert_w_down, i, ctx_tiles, final_norm_g, final_norm)

    s = _chunk_mlp_layer(x[0], ctx[0], mods[0], norm_mix_g[0], cm_w_in[0], cm_v_norm_g[0],
                         cm_w_s[0], cm_b_s[0], cm_w_out[0])
    s = moe(s, 0, 1)
    s = _mlstm_layer(s, mods[1], norm_mix_g[1], ml_w_in[0], ml_conv_w[0], ml_conv_b[0],
                     ml_gate_b[0], ml_norm_g[0], ml_w_out[0])
    s = moe(s, 1, 1)
    s = _rglru_layer(s, mods[2], norm_mix_g[2], lru_w_in[0], lru_conv_w[0], lru_conv_b[0],
                     lru_w_a[0], lru_b_a[0], lru_w_x[0], lru_b_x[0], lru_lambda[0], lru_w_out[0])
    s = moe(s, 2, 0)
    s = _fourier_layer(s, mods[3], norm_mix_g[3], fn_w_in[0], fn_w_out[0])
    s = moe(s, 3, 0, final_norm=True)
    return s[None]
```

```python
import functools
import math

import jax
import jax.numpy as jnp
import numpy as np
from jax import lax
from jax.experimental import pallas as pl
from jax.experimental.pallas import tpu as pltpu

F32 = jnp.float32
BF16 = jnp.bfloat16

EPS = 1e-6
POS_BASE = 10000.0
GRID_W = 64
N_MOD = 6
TM = 256
LANES = 128
SUBLANES = 8
VMEM_LIMIT = 56 * 1024 * 1024

CM_CHUNK = 128
CM_GROUPS = 4
ML_HEADS = 4
ML_DK = 128
ML_DV = 256
ML_CHUNK = 128
LRU_HEADS = 10
LRU_BLOCK = 128
LRU_C = 8.0
FN_GROUPS = 4
FFT_N2 = 128
MOE_GROUPS = 4
MOE_EPG = 8
MOE_EXPERTS = MOE_GROUPS * MOE_EPG
MOE_ROWS_LOG2 = 8
MOE_ROWS = 1 << MOE_ROWS_LOG2
ROUTE_ROWS = 40
FN_TB = 1024
FN_CB = 8
CONV_LEFT = 2

HI = lax.Precision.HIGHEST


def _cparams(*sem):
    return pltpu.CompilerParams(dimension_semantics=sem, vmem_limit_bytes=VMEM_LIMIT)


def _full(shape):
    nd = len(shape)
    return pl.BlockSpec(shape, lambda *_: (0,) * nd)


def _rms(x, g):
    return x * lax.rsqrt(jnp.mean(x * x, axis=-1, keepdims=True) + EPS) * g


def _gelu(x):
    c = math.sqrt(2.0 / math.pi)
    return 0.5 * x * (1.0 + jnp.tanh(c * (x + 0.044715 * (x * x * x))))


def _sigmoid(x):
    return 0.5 * jnp.tanh(0.5 * x) + 0.5


def _silu(x):
    return x * _sigmoid(x)


def _softplus(x):
    return jnp.maximum(x, 0.0) + jnp.log(1.0 + jnp.exp(-jnp.abs(x)))


def _mod_rows(mod_ref, tile, ctx_tiles, first):
    row = jnp.where(tile < ctx_tiles, 0, 1)
    m = mod_ref[pl.ds(row, 1), :]
    d = m.shape[1] // N_MOD
    return tuple(m[:, (first + j) * d:(first + j + 1) * d] for j in range(3))


def _bdot(a, b):
    return jnp.dot(a.astype(BF16), b.astype(BF16), preferred_element_type=F32)


def _split_bf16(x):
    hi = x.astype(BF16)
    return hi, (x - hi.astype(F32)).astype(BF16)


def _dot3(a, b, dims):
    a_hi, a_lo = _split_bf16(a)
    b_hi, b_lo = _split_bf16(b)
    dg = functools.partial(lax.dot_general, dimension_numbers=(dims, ((), ())),
                           preferred_element_type=F32)
    return dg(a_hi, b_hi) + dg(a_hi, b_lo) + dg(a_lo, b_hi)


def _ada_kernel(c_ref, w_ref, b_ref, o_ref):
    c = c_ref[...]
    o_ref[...] = jnp.dot(_silu(c), w_ref[...], precision=HI,
                         preferred_element_type=F32) + b_ref[...]


def _ada_table(c_rows, ada_w, ada_b):
    depth, d, n = ada_w.shape
    tn = 1024
    return pl.pallas_call(
        _ada_kernel,
        out_shape=jax.ShapeDtypeStruct((depth, SUBLANES, n), F32),
        grid=(depth, n // tn),
        in_specs=[_full((SUBLANES, d)),
                  pl.BlockSpec((None, d, tn), lambda i, j: (i, 0, j)),
                  pl.BlockSpec((None, 1, tn), lambda i, j: (i, 0, j))],
        out_specs=pl.BlockSpec((None, SUBLANES, tn), lambda i, j: (i, 0, j)),
        compiler_params=_cparams("arbitrary", "arbitrary"),
        name="ada_table",
    )(c_rows, ada_w, ada_b.reshape(depth, 1, n))


def _pos_tables(seq, d):
    q = d // 4
    freq = jnp.exp(-math.log(POS_BASE) * jnp.arange(q, dtype=F32) / q)
    ar = jnp.arange(seq // GRID_W, dtype=F32)[:, None] * freq
    ac = jnp.arange(GRID_W, dtype=F32)[:, None] * freq
    return (jnp.concatenate([jnp.sin(ar), jnp.cos(ar)], axis=-1),
            jnp.concatenate([jnp.sin(ac), jnp.cos(ac)], axis=-1))


def _cm_kernel(x_ref, ctx_ref, rt_ref, ct_ref, mod_ref, g_ref, win_ref, vg_ref, ws_ref, bs_ref,
               wout_ref, o_ref, p_scr, x_scr):
    i = pl.program_id(0)

    @pl.when(i == 0)
    def _():
        x_scr[...] = ctx_ref[...]

    @pl.when(i > 0)
    def _():
        rows_per_tile = TM // GRID_W
        q2 = rt_ref.shape[1]
        r0 = (i - 1) * rows_per_tile
        rt = jnp.concatenate(
            [jnp.broadcast_to(rt_ref[pl.ds(r0 + j, 1), :], (GRID_W, q2))
             for j in range(rows_per_tile)], axis=0)
        ct = jnp.concatenate([ct_ref[...]] * rows_per_tile, axis=0)
        x_scr[...] = x_ref[...] + jnp.concatenate([rt, ct], axis=1)

    shift, scale, gate = _mod_rows(mod_ref, i, 1, 0)
    x = x_scr[...]
    h = _rms(x, g_ref[...]) * (1.0 + scale) + shift
    z = _gelu(_bdot(h, win_ref[...]))
    w = z.shape[1] // 2
    u = z[:, :w]
    v = _rms(z[:, w:], vg_ref[...]).astype(BF16)
    gw = w // CM_GROUPS
    for c in range(TM // CM_CHUNK):
        r = slice(c * CM_CHUNK, (c + 1) * CM_CHUNK)
        for g in range(CM_GROUPS):
            cs = slice(g * gw, (g + 1) * gw)
            s = jnp.dot(ws_ref[g], v[r, cs], preferred_element_type=F32) + bs_ref[:, g:g + 1]
            p_scr[r, cs] = (u[r, cs] * s).astype(BF16)
    y = jnp.dot(p_scr[...], wout_ref[...], preferred_element_type=F32)
    o_ref[...] = x + gate * y


def _chunk_mlp_layer(x2, ctx2, mods, norm_g, w_in, v_g, w_s, b_s, w_out):
    seq, d = x2.shape
    n_ctx = ctx2.shape[0]
    assert n_ctx == TM and seq % TM == 0 and TM % GRID_W == 0
    n = n_ctx + seq
    w = w_out.shape[0]
    rt, ct = _pos_tables(seq, d)
    return pl.pallas_call(
        _cm_kernel,
        out_shape=jax.ShapeDtypeStruct((n, d), F32),
        grid=(n // TM,),
        in_specs=[pl.BlockSpec((TM, d), lambda i: (jnp.maximum(i - 1, 0), 0)),
                  _full((TM, d)), _full(rt.shape), _full(ct.shape),
                  _full(mods.shape), _full((1, d)), _full(w_in.shape), _full((1, w)),
                  _full(w_s.shape), _full((CM_CHUNK, CM_GROUPS)), _full(w_out.shape)],
        out_specs=pl.BlockSpec((TM, d), lambda i: (i, 0)),
        scratch_shapes=[pltpu.VMEM((TM, w), BF16), pltpu.VMEM((TM, d), F32)],
        compiler_params=_cparams("arbitrary"),
        name="chunk_mlp",
    )(x2, ctx2, rt, ct, mods, norm_g.reshape(1, d), w_in.astype(BF16), v_g.reshape(1, w),
      w_s.astype(BF16), b_s.T, w_out.astype(BF16))


def _store_token_tiles(ref, x, f32_scr):
    rows, d = x.shape
    for j in range(d // LANES):
        f32_scr[pl.ds(j, rows, stride=d // LANES), :] = x[:, j * LANES:(j + 1) * LANES]
    ref[...] = f32_scr[...].astype(ref.dtype)


def _load_token_tiles(ref, f32_scr):
    chunks = SUBLANES
    rows = ref.shape[0] // chunks
    f32_scr[...] = ref[...].astype(F32)
    return jnp.concatenate([f32_scr[pl.ds(j, rows, stride=chunks), :] for j in range(chunks)],
                           axis=1)


def _route_tiles(nt):
    return next(k for k in (5, 4, 2, 1) if nt % k == 0)


def _router_kernel(s_ref, mod_ref, g_ref, rwt_ref, rbt_ref, tri_ref, h_ref, e1_ref, e2_ref, r1_ref,
                   r2_ref, wt_ref, cnt_ref, carry, tile_scr, *, ctx_rows):
    i = pl.program_id(0)
    rows, d = s_ref.shape

    @pl.when(i == 0)
    def _():
        carry[...] = jnp.zeros_like(carry)

    lat = mod_ref[1:2, :]
    shift, scale = lat[:, 3 * d:4 * d], lat[:, 4 * d:5 * d]
    if ctx_rows:
        ctx = mod_ref[0:1, :]
        is_ctx = (i == 0) & (lax.broadcasted_iota(jnp.int32, (rows, 1), 0) < ctx_rows)
        shift = jnp.where(is_ctx, ctx[:, 3 * d:4 * d], shift)
        scale = jnp.where(is_ctx, ctx[:, 4 * d:5 * d], scale)
    h = _rms(s_ref[...], g_ref[...]) * (1.0 + scale) + shift
    _store_token_tiles(h_ref, h, tile_scr)
    logits = _dot3(rwt_ref[...], h, ((1,), (1,))) + rbt_ref[...]
    row = lax.broadcasted_iota(jnp.int32, logits.shape, 0)
    neg = jnp.float32(-jnp.inf)
    big = jnp.int32(1 << 20)
    is_g = row < MOE_GROUPS
    gl = jnp.where(is_g, logits, neg)
    gmax = jnp.max(gl, axis=0, keepdims=True)
    grp = jnp.min(jnp.where(is_g & (gl == gmax), row, big), axis=0, keepdims=True)
    p_grp = 1.0 / jnp.sum(jnp.exp(gl - gmax), axis=0, keepdims=True)
    e_row = row - MOE_GROUPS
    in_grp = (e_row >= 0) & (e_row < MOE_EXPERTS) & ((e_row >> 3) == grp)
    l1 = jnp.where(in_grp, logits, neg)
    v1 = jnp.max(l1, axis=0, keepdims=True)
    i1 = jnp.min(jnp.where(in_grp & (l1 == v1), row, big), axis=0, keepdims=True)
    rest = in_grp & (row != i1)
    l2 = jnp.where(rest, logits, neg)
    v2 = jnp.max(l2, axis=0, keepdims=True)
    i2 = jnp.min(jnp.where(rest & (l2 == v2), row, big), axis=0, keepdims=True)
    e21 = jnp.exp(v2 - v1)
    w1 = p_grp / (1.0 + e21)
    w2 = p_grp * e21 / (1.0 + e21)
    oh1 = (row == i1).astype(F32)
    oh2 = (row == i2).astype(F32)
    oh = oh1 + oh2
    before = jnp.dot(oh.astype(BF16), tri_ref[...], preferred_element_type=F32) + carry[:, 0:1]
    r1_ref[...] = jnp.sum(oh1 * before, axis=0, keepdims=True).astype(jnp.int32)
    r2_ref[...] = jnp.sum(oh2 * before, axis=0, keepdims=True).astype(jnp.int32)
    e1_ref[...] = i1 - MOE_GROUPS
    e2_ref[...] = i2 - MOE_GROUPS
    carry[...] = carry[...] + jnp.sum(oh, axis=1, keepdims=True)
    cnt_ref[...] = carry[...]
    wt_ref[...] = jnp.concatenate([w1, w2, jnp.zeros((LANES - 2, rows), F32)], axis=0).T


def _router(s, mods, norm_g, rg_w, rg_b, re_w, re_b, ctx_tiles):
    n, d = s.shape
    rows = TM * _route_tiles(n // TM)
    steps = n // rows
    pad = ROUTE_ROWS - MOE_GROUPS - MOE_EXPERTS
    rwt = jnp.concatenate([rg_w, re_w, jnp.zeros((d, pad), F32)], axis=1).T
    rbt = jnp.broadcast_to(jnp.concatenate([rg_b, re_b, jnp.zeros((pad,), F32)])[:, None],
                           (ROUTE_ROWS, rows))
    tri = jnp.asarray(np.triu(np.ones((rows, rows), np.float32), 1), BF16)
    assert d == SUBLANES * LANES
    tile = pl.BlockSpec((rows, d), lambda i: (i, 0))
    irow = pl.BlockSpec((None, 1, rows), lambda i: (i, 0, 0))
    ishape = jax.ShapeDtypeStruct((steps, 1, rows), jnp.int32)
    return pl.pallas_call(
        functools.partial(_router_kernel, ctx_rows=ctx_tiles * TM),
        out_shape=(jax.ShapeDtypeStruct((n * SUBLANES, LANES), BF16), ishape, ishape, ishape, ishape,
                   jax.ShapeDtypeStruct((n, LANES), F32),
                   jax.ShapeDtypeStruct((ROUTE_ROWS, LANES), F32)),
        grid=(steps,),
        in_specs=[tile, _full(mods.shape), _full((1, d)), _full((ROUTE_ROWS, d)),
                  _full((ROUTE_ROWS, rows)), _full((rows, rows))],
        out_specs=(pl.BlockSpec((rows * SUBLANES, LANES), lambda i: (i, 0)), irow, irow, irow, irow,
                   pl.BlockSpec((rows, LANES), lambda i: (i, 0)), _full((ROUTE_ROWS, LANES))),
        scratch_shapes=[pltpu.VMEM((ROUTE_ROWS, LANES), F32),
                        pltpu.VMEM((rows * SUBLANES, LANES), F32)],
        compiler_params=_cparams("arbitrary"),
        name="moe_router",
    )(s, mods, norm_g.reshape(1, d), rwt, rbt, tri)


def _finalize_kernel(cnt_ref, e1_ref, e2_ref, r1_ref, r2_ref, d1_ref, d2_ref, blk_ref):
    e1 = e1_ref[...]
    e2 = e2_ref[...]
    r1 = r1_ref[...]
    r2 = r2_ref[...]
    d1 = jnp.zeros_like(e1)
    d2 = jnp.zeros_like(e2)
    lane = lax.broadcasted_iota(jnp.int32, blk_ref.shape, 1)
    brow = lane * MOE_ROWS
    sub = lax.broadcasted_iota(jnp.int32, blk_ref.shape, 0)
    be = jnp.zeros(blk_ref.shape, jnp.int32)
    pend = jnp.zeros(blk_ref.shape, jnp.int32)
    ps = jnp.int32(0)
    for e in range(MOE_EXPERTS):
        c = cnt_ref[e]
        pe = ps + lax.shift_left(lax.shift_right_logical(c + (MOE_ROWS - 1), MOE_ROWS_LOG2),
                                 MOE_ROWS_LOG2)
        d1 = jnp.where(e1 == e, ps + r1, d1)
        d2 = jnp.where(e2 == e, ps + r2, d2)
        be = be + (brow >= pe).astype(jnp.int32)
        pend = jnp.where(lane == e, pe, pend)
        ps = pe
    d1_ref[...] = d1
    d2_ref[...] = d2
    n_used = lax.shift_right_logical(ps, MOE_ROWS_LOG2)
    blk_ref[...] = jnp.where(sub == 0, jnp.minimum(be, MOE_EXPERTS - 1),
                             jnp.where(sub == 1, pend, n_used))


def _finalize(counts, e1, e2, r1, r2, nb):
    nbp = (nb + LANES - 1) // LANES * LANES
    whole = pl.BlockSpec(e1.shape, lambda i, c: (0, 0, 0))
    ishape = jax.ShapeDtypeStruct(e1.shape, jnp.int32)
    return pl.pallas_call(
        _finalize_kernel,
        out_shape=(ishape, ishape, jax.ShapeDtypeStruct((SUBLANES, nbp), jnp.int32)),
        grid_spec=pltpu.PrefetchScalarGridSpec(
            num_scalar_prefetch=1,
            grid=(1,),
            in_specs=[whole, whole, whole, whole],
            out_specs=(whole, whole, pl.BlockSpec((SUBLANES, nbp), lambda i, c: (0, 0)))),
        compiler_params=_cparams("arbitrary"),
        name="moe_finalize",
    )(counts, e1, e2, r1, r2)


def _token_copy(src, r, dst, d, sem):
    return pltpu.make_async_copy(src.at[pl.ds(pl.multiple_of(r * SUBLANES, SUBLANES), SUBLANES), :],
                                 dst.at[pl.ds(pl.multiple_of(d * SUBLANES, SUBLANES), SUBLANES), :],
                                 sem)


def _zero_fill_padding(pend_ref, nu_ref, xs_out, zbuf, zsem):
    blk_rows = MOE_ROWS * SUBLANES
    nb = xs_out.shape[0] // blk_rows
    zbuf[...] = jnp.zeros_like(zbuf)

    def block_copy(b):
        r0 = pl.multiple_of(b * blk_rows, blk_rows)
        return pltpu.make_async_copy(zbuf, xs_out.at[pl.ds(r0, blk_rows), :], zsem)

    def seg_last_block(e):
        pe = pend_ref[e]
        prev = pend_ref[e - 1] if e > 0 else 0
        return pe > prev, lax.shift_right_logical(pe, MOE_ROWS_LOG2) - 1

    for e in range(MOE_EXPERTS):
        nonempty, b = seg_last_block(e)

        @pl.when(nonempty)
        def _():
            block_copy(b).start()

    def tail_start(b, c):
        block_copy(b).start()
        return c

    lax.fori_loop(nu_ref[0], nb, tail_start, 0)
    for e in range(MOE_EXPERTS):
        nonempty, b = seg_last_block(e)

        @pl.when(nonempty)
        def _():
            block_copy(b).wait()

    def tail_wait(b, c):
        block_copy(b).wait()
        return c

    lax.fori_loop(nu_ref[0], nb, tail_wait, 0)


def _dispatch_kernel(dest_ref, pend_ref, nu_ref, h_ref, xs_out, sem, zbuf, zsem, *, tiles):
    i = pl.program_id(0)

    @pl.when(i == 0)
    def _():
        _zero_fill_padding(pend_ref, nu_ref, xs_out, zbuf, zsem)

    rows = tiles * TM
    for q in range(tiles):
        base = i * (2 * rows) + q * TM

        def start(r, c):
            _token_copy(h_ref, q * TM + r, xs_out, dest_ref[base + r], sem).start(priority=0)
            _token_copy(h_ref, q * TM + r, xs_out, dest_ref[base + rows + r],
                        sem).start(priority=1)
            return c

        lax.fori_loop(0, TM, start, 0, unroll=8)
    for _ in range(2):
        pltpu.make_async_copy(h_ref, xs_out.at[pl.ds(0, tiles * TM * SUBLANES), :], sem).wait()


def _dispatch(dest, pad_end, n_used, h, n_rows):
    n = h.shape[0] // SUBLANES
    nt = n // TM
    tiles = _route_tiles(nt)
    return pl.pallas_call(
        functools.partial(_dispatch_kernel, tiles=tiles),
        out_shape=jax.ShapeDtypeStruct((n_rows * SUBLANES, LANES), BF16),
        grid_spec=pltpu.PrefetchScalarGridSpec(
            num_scalar_prefetch=3,
            grid=(nt // tiles,),
            in_specs=[pl.BlockSpec((tiles * TM * SUBLANES, LANES), lambda i, *_: (i, 0))],
            out_specs=pl.BlockSpec(memory_space=pl.ANY),
            scratch_shapes=[pltpu.SemaphoreType.DMA, pltpu.VMEM((MOE_ROWS * SUBLANES, LANES), BF16),
                            pltpu.SemaphoreType.DMA]),
        compiler_params=_cparams("arbitrary"),
        name="moe_dispatch",
    )(dest, pad_end, n_used, h)


X_SLOTS = 3


def _expert_kernel(be_ref, pend_ref, nu_ref, xs_hbm, wg_hbm, wu_hbm, wd_hbm, y_ref,
                   xbuf, tile_scr, wg_f, wu_f, wd_f, wg_s, wu_s, wd_s, xsem, wsem, ord_ref,
                   *, layer):
    b = pl.program_id(0)
    nu = nu_ref[0]
    blk_rows = MOE_ROWS * SUBLANES

    def x_copy(blk, slot):
        r0 = pl.multiple_of(blk * blk_rows, blk_rows)
        return pltpu.make_async_copy(xs_hbm.at[pl.ds(r0, blk_rows), :], xbuf.at[slot],
                                     xsem.at[slot])

    def w_copies(e, slot):
        return (pltpu.make_async_copy(wg_hbm.at[layer, e], wg_f.at[slot], wsem.at[slot]),
                pltpu.make_async_copy(wu_hbm.at[layer, e], wu_f.at[slot], wsem.at[slot]),
                pltpu.make_async_copy(wd_hbm.at[layer, e], wd_f.at[slot], wsem.at[slot]))

    @pl.when(b == 0)
    def _():
        ord_ref[0] = 0
        for j in range(X_SLOTS - 1):
            @pl.when(j < nu)
            def _():
                x_copy(j, j).start()

        @pl.when(nu > 0)
        def _():
            for c in w_copies(be_ref[0], 0):
                c.start()

    ahead = b + (X_SLOTS - 1)

    @pl.when(ahead < nu)
    def _():
        x_copy(ahead, lax.rem(ahead, X_SLOTS)).start()

    used = b < nu
    e = be_ref[b]
    fresh = used & ((b == 0) | (e != be_ref[jnp.maximum(b - 1, 0)]))

    @pl.when(fresh)
    def _():
        k = ord_ref[0]
        slot = lax.rem(k, 2)
        for c in w_copies(e, slot):
            c.wait()
        wg_s[...] = wg_f[slot].astype(BF16)
        wu_s[...] = wu_f[slot].astype(BF16)
        wd_s[...] = wd_f[slot].astype(BF16)
        nxt = lax.shift_right_logical(pend_ref[e], MOE_ROWS_LOG2)

        @pl.when(nxt < nu)
        def _():
            for c in w_copies(be_ref[nxt], 1 - slot):
                c.start()

        ord_ref[0] = k + 1

    @pl.when(used)
    def _():
        slot = lax.rem(b, X_SLOTS)
        x_copy(b, slot).wait()
        x = _load_token_tiles(xbuf.at[slot], tile_scr).astype(BF16)
        a = jnp.dot(x, wg_s[...], preferred_element_type=F32)
        u = jnp.dot(x, wu_s[...], preferred_element_type=F32)
        y = jnp.dot((_silu(a) * u).astype(BF16), wd_s[...], preferred_element_type=F32)
        _store_token_tiles(y_ref, y, tile_scr)

    @pl.when(jnp.logical_not(used))
    def _():
        y_ref[...] = jnp.zeros_like(y_ref)


def _experts(blk_expert, pad_end, n_used, xs, w_gate, w_up, w_down, layer):
    d, hid = w_gate.shape[2:]
    blk_rows = MOE_ROWS * SUBLANES
    nb = xs.shape[0] // blk_rows
    hbm = pl.BlockSpec(memory_space=pl.ANY)
    return pl.pallas_call(
        functools.partial(_expert_kernel, layer=layer),
        out_shape=jax.ShapeDtypeStruct(xs.shape, BF16),
        grid_spec=pltpu.PrefetchScalarGridSpec(
            num_scalar_prefetch=3,
            grid=(nb,),
            in_specs=[hbm, hbm, hbm, hbm],
            out_specs=pl.BlockSpec((blk_rows, LANES), lambda b, *_: (b, 0)),
            scratch_shapes=[pltpu.VMEM((X_SLOTS, blk_rows, LANES), BF16),
                            pltpu.VMEM((blk_rows, LANES), F32),
                            pltpu.VMEM((2, d, hid), F32), pltpu.VMEM((2, d, hid), F32),
                            pltpu.VMEM((2, hid, d), F32),
                            pltpu.VMEM((d, hid), BF16), pltpu.VMEM((d, hid), BF16),
                            pltpu.VMEM((hid, d), BF16),
                            pltpu.SemaphoreType.DMA((X_SLOTS,)), pltpu.SemaphoreType.DMA((2,)),
                            pltpu.SMEM((1,), jnp.int32)]),
        compiler_params=_cparams("arbitrary"),
        name="moe_experts",
    )(blk_expert, pad_end, n_used, xs, w_gate, w_up, w_down)


def _combine_kernel(dest_ref, s_ref, wt_ref, mod_ref, fg_ref, ys_ref, o_ref, ybuf, tile_scr, sem,
                    *, ctx_tiles, final_norm, tiles):
    i = pl.program_id(0)
    slot = i % 2
    rows = tiles * TM

    def gather(tile, slot):
        tile = jnp.asarray(tile, jnp.int32)
        base = lax.div(tile, tiles) * (2 * rows) + lax.rem(tile, tiles) * TM

        def start(r, c):
            _token_copy(ys_ref, dest_ref[base + r], ybuf.at[slot, 0], r,
                        sem.at[slot]).start(priority=0)
            _token_copy(ys_ref, dest_ref[base + rows + r], ybuf.at[slot, 1], r,
                        sem.at[slot]).start(priority=1)
            return c

        lax.fori_loop(0, TM, start, 0, unroll=8)

    @pl.when(i == 0)
    def _():
        gather(0, 0)

    @pl.when(i + 1 < pl.num_programs(0))
    def _():
        gather(i + 1, 1 - slot)

    for k in range(2):
        pltpu.make_async_copy(ys_ref.at[pl.ds(0, TM * SUBLANES), :], ybuf.at[slot, k],
                              sem.at[slot]).wait()
    gate = _mod_rows(mod_ref, i, ctx_tiles, 3)[2]
    wt = wt_ref[...]
    y = wt[:, 0:1] * _load_token_tiles(ybuf.at[slot, 0], tile_scr)
    y = y + wt[:, 1:2] * _load_token_tiles(ybuf.at[slot, 1], tile_scr)
    out = s_ref[...] + gate * y
    if final_norm:
        out = _rms(out, fg_ref[...])
    o_ref[...] = out


def _combine(dest, s, wts, mods, final_g, ys, ctx_tiles, final_norm):
    n, d = s.shape
    return pl.pallas_call(
        functools.partial(_combine_kernel, ctx_tiles=ctx_tiles, final_norm=final_norm,
                          tiles=_route_tiles(n // TM)),
        out_shape=jax.ShapeDtypeStruct((n, d), F32),
        grid_spec=pltpu.PrefetchScalarGridSpec(
            num_scalar_prefetch=1,
            grid=(n // TM,),
            in_specs=[pl.BlockSpec((TM, d), lambda i, dst: (i, 0)),
                      pl.BlockSpec((TM, LANES), lambda i, dst: (i, 0)),
                      pl.BlockSpec(mods.shape, lambda i, dst: (0, 0)),
                      pl.BlockSpec((1, d), lambda i, dst: (0, 0)),
                      pl.BlockSpec(memory_space=pl.ANY)],
            out_specs=pl.BlockSpec((TM, d), lambda i, dst: (i, 0)),
            scratch_shapes=[pltpu.VMEM((2, 2, TM * SUBLANES, LANES), BF16),
                            pltpu.VMEM((TM * SUBLANES, LANES), F32),
                            pltpu.SemaphoreType.DMA((2,))]),
        compiler_params=_cparams("arbitrary"),
        name="moe_combine",
    )(dest, s, wts, mods, final_g.reshape(1, d), ys)


def _moe_layer(s, mods, norm_g, rg_w, rg_b, re_w, re_b, w_gate, w_up, w_down, layer, ctx_tiles,
               final_g, final_norm):
    n, d = s.shape
    h, e1, e2, r1, r2, wts, cnt = _router(s, mods, norm_g, rg_w, rg_b, re_w, re_b, ctx_tiles)
    counts = cnt[MOE_GROUPS:MOE_GROUPS + MOE_EXPERTS, 0].astype(jnp.int32)
    nb = (2 * n + MOE_EXPERTS * (MOE_ROWS - 1)) // MOE_ROWS + 1
    d1, d2, blk = _finalize(counts, e1, e2, r1, r2, nb)
    dest = jnp.concatenate([d1, d2], axis=1).reshape(2 * n)
    n_used = blk[2, :1]
    pad_end = blk[1, :MOE_EXPERTS]
    xs = _dispatch(dest, pad_end, n_used, h, nb * MOE_ROWS)
    ys = _experts(blk[0, :nb], pad_end, n_used, xs, w_gate, w_up, w_down, layer)
    return _combine(dest, s, wts, mods, final_g, ys, ctx_tiles, final_norm)


def _conv_tile(x, prev_ref, next_ref, has_prev, has_next, w_ref, b_ref):
    rows = x.shape[0]
    S = SUBLANES
    ridx = lax.broadcasted_iota(jnp.int32, (S, x.shape[1]), 0)
    pm = jnp.where(has_prev, 1.0, 0.0)
    nm = jnp.where(has_next, 1.0, 0.0)
    p2 = prev_ref[S - 2:S - 1, :] * pm
    p1 = prev_ref[S - 1:S, :] * pm
    n1 = next_ref[0:1, :] * nm

    def fix_head(rolled, head):
        return jnp.concatenate([head(rolled[:S]), rolled[S:]], axis=0)

    xm1 = fix_head(pltpu.roll(x, 1, axis=0), lambda g: jnp.where(ridx == 0, p1, g))
    xm2 = fix_head(pltpu.roll(x, 2, axis=0),
                   lambda g: jnp.where(ridx == 0, p2, jnp.where(ridx == 1, p1, g)))
    xp1 = pltpu.roll(x, rows - 1, axis=0)
    xp1 = jnp.concatenate([xp1[:rows - S], jnp.where(ridx == S - 1, n1, xp1[rows - S:])], axis=0)
    return (xm2 * w_ref[0:1, :] + xm1 * w_ref[1:2, :] + x * w_ref[2:3, :]
            + xp1 * w_ref[3:4, :] + b_ref[...])


def _ml_proj_kernel(s_ref, mod_ref, g_ref, w_ref, wg_ref, gb_ref, qk_ref, v_ref, o_ref, gt_ref):
    i = pl.program_id(0)
    shift, scale, _ = _mod_rows(mod_ref, i, 1, 0)
    h = _rms(s_ref[...], g_ref[...]) * (1.0 + scale) + shift
    z = _bdot(h, w_ref[...])
    nqk = qk_ref.shape[1]
    nv = v_ref.shape[1]
    qk_ref[...] = z[:, :nqk]
    v_ref[...] = z[:, nqk:nqk + nv].astype(BF16)
    o_ref[...] = z[:, nqk + nv:].astype(BF16)
    pre = _dot3(h, wg_ref[...], ((1,), (0,))) + gb_ref[...]
    lane = lax.broadcasted_iota(jnp.int32, pre.shape, 1)
    is_forget = ((lane >> 2) & 1) == 1
    gt_ref[...] = jnp.where(is_forget, -_softplus(-pre), pre)


def _ml_proj(s, mods, norm_g, w_in, gate_b):
    n, d = s.shape
    nqk = 2 * ML_HEADS * ML_DK
    nv = ML_HEADS * ML_DV
    n_main = nqk + 2 * nv
    n_gate = w_in.shape[1] - n_main
    w_main = w_in[:, :n_main].astype(BF16)
    w_gate = jnp.concatenate([w_in[:, n_main:], jnp.zeros((d, LANES - n_gate), F32)], axis=1)
    gb = jnp.concatenate([gate_b.reshape(n_gate), jnp.zeros((LANES - n_gate,), F32)]).reshape(1, LANES)
    tile = lambda w: pl.BlockSpec((TM, w), lambda i: (i, 0))
    return pl.pallas_call(
        _ml_proj_kernel,
        out_shape=(jax.ShapeDtypeStruct((n, nqk), F32), jax.ShapeDtypeStruct((n, nv), BF16),
                   jax.ShapeDtypeStruct((n, nv), BF16), jax.ShapeDtypeStruct((n, LANES), F32)),
        grid=(n // TM,),
        in_specs=[tile(d), _full(mods.shape), _full((1, d)), _full(w_main.shape),
                  _full((d, LANES)), _full((1, LANES))],
        out_specs=(tile(nqk), tile(nv), tile(nv), tile(LANES)),
        compiler_params=_cparams("arbitrary"),
        name="mlstm_proj",
    )(s, mods, norm_g.reshape(1, d), w_main, w_gate, gb)


def _ml_chunk_index(j, n_chunks, ctx_chunks, reverse):
    if not reverse:
        return j
    return jnp.where(j < ctx_chunks, ctx_chunks - 1 - j, n_chunks - 1 + ctx_chunks - j)


def _ml_rec_kernel(qk_ref, qkp_ref, qkn_ref, v_ref, gt_ref, gtt_ref, cw_ref, cb_ref, o_ref,
                   c_scr, n_scr, m_scr, *, reverse, n_chunks, ctx_chunks):
    j = pl.program_id(0)
    c = _ml_chunk_index(j, n_chunks, ctx_chunks, reverse)

    @pl.when(j == 0)
    def _():
        c_scr[...] = jnp.zeros_like(c_scr)
        n_scr[...] = jnp.zeros_like(n_scr)
        m_scr[...] = jnp.zeros_like(m_scr)

    has_prev = (c != 0) & (c != ctx_chunks)
    has_next = (c != ctx_chunks - 1) & (c != n_chunks - 1)
    qk = _silu(_conv_tile(qk_ref[...], qkp_ref, qkn_ref, has_prev, has_next, cw_ref, cb_ref))
    L = ML_CHUNK
    ri = lax.broadcasted_iota(jnp.int32, (L, L), 0)
    ci = lax.broadcasted_iota(jnp.int32, (L, L), 1)
    past = (ci >= ri) if reverse else (ci <= ri)
    pastf = past.astype(F32)
    gt = gt_ref[...]
    gtt = gtt_ref[...]
    b_col = jnp.dot(pastf, gt, precision=HI, preferred_element_type=F32)
    b_row = jnp.dot(gtt, pastf.T, precision=HI, preferred_element_type=F32)
    last = 0 if reverse else L - 1
    dbase = 8 if reverse else 0
    nq = ML_HEADS * ML_DK
    for hd in range(ML_HEADS):
        cl = dbase + hd
        cf = dbase + 4 + hd
        q = qk[:, hd * ML_DK:(hd + 1) * ML_DK] * (ML_DK ** -0.5)
        k = qk[:, nq + hd * ML_DK:nq + (hd + 1) * ML_DK]
        v = v_ref[:, hd * ML_DV:(hd + 1) * ML_DV]
        li_c = gt[:, cl:cl + 1]
        li_r = gtt[cl:cl + 1, :]
        b_c = b_col[:, cf:cf + 1]
        b_r = b_row[cf:cf + 1, :]
        g = b_r[:, last:last + 1]
        m0 = m_scr[hd:hd + 1, 0:1]
        c0 = c_scr[hd]
        n0 = n_scr[hd:hd + 1, :]
        a_c = g - b_c + li_c
        a_r = g - b_r + li_r
        m_loc = jnp.max(a_r, axis=-1, keepdims=True)
        inter = b_c + m0
        dlog = jnp.where(past, b_c - b_r + li_r, -jnp.inf)
        m = jnp.maximum(inter, jnp.max(dlog, axis=-1, keepdims=True))
        qb = q.astype(BF16)
        sc = lax.dot_general(qb, k.astype(BF16), (((1,), (1,)), ((), ())),
                             preferred_element_type=F32) * jnp.exp(dlog - m)
        w_inter = jnp.exp(inter - m)
        num = (jnp.dot(sc.astype(BF16), v, preferred_element_type=F32)
               + w_inter * jnp.dot(qb, c0.astype(BF16), preferred_element_type=F32))
        den = (jnp.sum(sc, axis=-1, keepdims=True)
               + w_inter * jnp.sum(q * n0, axis=-1, keepdims=True))
        o_ref[:, hd * ML_DV:(hd + 1) * ML_DV] = (
            num / jnp.maximum(jnp.abs(den), jnp.exp(-m))).astype(BF16)
        m_new = jnp.maximum(g + m0, m_loc)
        dec = jnp.exp(g + m0 - m_new)
        scl = jnp.exp(m_loc - m_new)
        kw = k * jnp.exp(a_c - m_loc)
        c_scr[hd] = dec * c0 + scl * jnp.dot(kw.T.astype(BF16), v, preferred_element_type=F32)
        n_scr[hd:hd + 1, :] = dec * n0 + scl * jnp.sum(kw, axis=0, keepdims=True)
        m_scr[hd:hd + 1, :] = jnp.broadcast_to(m_new, (1, LANES))


def _ml_rec(qk, v, gt, gtt, conv_w, conv_b, reverse):
    n, nqk = qk.shape
    nv = v.shape[1]
    L = ML_CHUNK
    nc = n // L
    cc = TM // L
    hb = L // SUBLANES
    idx = lambda j: _ml_chunk_index(j, nc, cc, reverse)
    last8 = n // SUBLANES - 1
    return pl.pallas_call(
        functools.partial(_ml_rec_kernel, reverse=reverse, n_chunks=nc, ctx_chunks=cc),
        out_shape=jax.ShapeDtypeStruct((n, nv), BF16),
        grid=(nc,),
        in_specs=[pl.BlockSpec((L, nqk), lambda j: (idx(j), 0)),
                  pl.BlockSpec((SUBLANES, nqk), lambda j: (jnp.maximum(idx(j) * hb - 1, 0), 0)),
                  pl.BlockSpec((SUBLANES, nqk), lambda j: (jnp.minimum((idx(j) + 1) * hb, last8), 0)),
                  pl.BlockSpec((L, nv), lambda j: (idx(j), 0)),
                  pl.BlockSpec((L, LANES), lambda j: (idx(j), 0)),
                  pl.BlockSpec((2 * SUBLANES, L), lambda j: (0, idx(j))),
                  _full((4, nqk)), _full((1, nqk))],
        out_specs=pl.BlockSpec((L, nv), lambda j: (idx(j), 0)),
        scratch_shapes=[pltpu.VMEM((ML_HEADS, ML_DK, ML_DV), F32),
                        pltpu.VMEM((SUBLANES, ML_DK), F32),
                        pltpu.VMEM((SUBLANES, LANES), F32)],
        compiler_params=_cparams("arbitrary"),
        name="mlstm_rev" if reverse else "mlstm_fwd",
    )(qk, qk, qk, v, gt, gtt, conv_w, conv_b.reshape(1, nqk))


def _ml_out_kernel(hf_ref, hr_ref, o_ref, s_ref, mod_ref, ng_ref, w_ref, out_ref, p_scr):
    i = pl.program_id(0)
    gate = _mod_rows(mod_ref, i, 1, 0)[2]
    hs = hf_ref[...].astype(F32) + hr_ref[...].astype(F32)
    sig = _sigmoid(o_ref[...].astype(F32))
    ng = ng_ref[...]
    for hd in range(ML_HEADS):
        cs = slice(hd * ML_DV, (hd + 1) * ML_DV)
        seg = hs[:, cs]
        hn = seg * lax.rsqrt(jnp.mean(seg * seg, axis=-1, keepdims=True) + EPS) * ng[:, cs]
        p_scr[:, cs] = (hn * sig[:, cs]).astype(BF16)
    y = jnp.dot(p_scr[...], w_ref[...], preferred_element_type=F32)
    out_ref[...] = s_ref[...] + gate * y


def _ml_out(hf, hr, o, s, mods, norm_g, w_out):
    n, d = s.shape
    nv = hf.shape[1]
    tile = lambda w: pl.BlockSpec((TM, w), lambda i: (i, 0))
    return pl.pallas_call(
        _ml_out_kernel,
        out_shape=jax.ShapeDtypeStruct((n, d), F32),
        grid=(n // TM,),
        in_specs=[tile(nv), tile(nv), tile(nv), tile(d), _full(mods.shape), _full((1, nv)),
                  _full(w_out.shape)],
        out_specs=tile(d),
        scratch_shapes=[pltpu.VMEM((TM, nv), BF16)],
        compiler_params=_cparams("arbitrary"),
        name="mlstm_out",
    )(hf, hr, o, s, mods, norm_g.reshape(1, nv), w_out.astype(BF16))


def _mlstm_layer(s, mods, norm_g, w_in, conv_w, conv_b, gate_b, ml_norm_g, w_out):
    qk, v, o, gt = _ml_proj(s, mods, norm_g, w_in, gate_b)
    gtt = gt[:, :2 * SUBLANES].T
    hf = _ml_rec(qk, v, gt, gtt, conv_w, conv_b, False)
    hr = _ml_rec(qk, v, gt, gtt, conv_w, conv_b, True)
    return _ml_out(hf, hr, o, s, mods, ml_norm_g, w_out)


def _lru_proj_kernel(s_ref, mod_ref, g_ref, w_ref, gl_ref, xr_ref):
    i = pl.program_id(0)
    shift, scale, _ = _mod_rows(mod_ref, i, 1, 0)
    h = _rms(s_ref[...], g_ref[...]) * (1.0 + scale) + shift
    z = _bdot(h, w_ref[...])
    w = gl_ref.shape[1]
    gl_ref[...] = _gelu(z[:, :w]).astype(BF16)
    xr_ref[...] = z[:, w:]


def _lru_proj(s, mods, norm_g, w_in):
    n, d = s.shape
    w = w_in.shape[1] // 2
    tile = lambda c: pl.BlockSpec((TM, c), lambda i: (i, 0))
    return pl.pallas_call(
        _lru_proj_kernel,
        out_shape=(jax.ShapeDtypeStruct((n, w), BF16), jax.ShapeDtypeStruct((n, w), F32)),
        grid=(n // TM,),
        in_specs=[tile(d), _full(mods.shape), _full((1, d)), _full(w_in.shape)],
        out_specs=(tile(w), tile(w)),
        compiler_params=_cparams("arbitrary"),
        name="rglru_proj",
    )(s, mods, norm_g.reshape(1, d), w_in.astype(BF16))


def _lru_tile_index(j, n_tiles, reverse):
    if not reverse:
        return j
    return jnp.where(j == 0, 0, n_tiles - j)


def _lru_scan_kernel(x_ref, xp_ref, xn_ref, cw_ref, cb_ref, wg_ref, ba_ref, bx_ref, lam_ref,
                     o_ref, a_scr, u_scr, carry, *, reverse, n_tiles):
    j = pl.program_id(0)
    t = _lru_tile_index(j, n_tiles, reverse)

    @pl.when(j == 0)
    def _():
        carry[...] = jnp.zeros_like(carry)

    has_prev = t > 1
    has_next = (t != 0) & (t != n_tiles - 1)
    xr = _conv_tile(x_ref[...], xp_ref, xn_ref, has_prev, has_next, cw_ref, cb_ref)
    sp = _softplus(-lam_ref[...])
    B = LRU_BLOCK
    for hd in range(LRU_HEADS):
        cs = slice(hd * B, (hd + 1) * B)
        xh = xr[:, cs]
        y = jnp.dot(xh.astype(BF16), wg_ref[hd], preferred_element_type=F32)
        r = _sigmoid(y[:, :B] + ba_ref[:, cs])
        ig = _sigmoid(y[:, B:] + bx_ref[:, cs])
        log_a = -LRU_C * r * sp[:, cs]
        a = jnp.exp(log_a)
        a_scr[:, cs] = a
        u_scr[:, cs] = jnp.sqrt(1.0 - a * a) * (ig * xh)

    S = SUBLANES
    w = a_scr.shape[1]
    sidx = lax.broadcasted_iota(jnp.int32, (S, w), 0)

    def group(gi, c):
        g = (TM // S - 1 - gi) if reverse else gi
        r0 = pl.multiple_of(g * S, S)
        a = a_scr[pl.ds(r0, S), :]
        u = u_scr[pl.ds(r0, S), :]
        for sft in (1, 2, 4):
            if reverse:
                ok = sidx < S - sft
                a_e = pltpu.roll(a, S - sft, axis=0)
                u_e = pltpu.roll(u, S - sft, axis=0)
            else:
                ok = sidx >= sft
                a_e = pltpu.roll(a, sft, axis=0)
                u_e = pltpu.roll(u, sft, axis=0)
            u = jnp.where(ok, a * u_e + u, u)
            a = jnp.where(ok, a * a_e, a)
        hcur = a * carry[...] + u
        u_scr[pl.ds(r0, S), :] = hcur
        edge = 0 if reverse else S - 1
        carry[...] = jnp.broadcast_to(hcur[edge:edge + 1, :], (S, w))
        return c

    lax.fori_loop(0, TM // S, group, 0)
    o_ref[...] = u_scr[...].astype(BF16)


def _lru_scan(xraw, conv_w, conv_b, w_a, b_a, w_x, b_x, lam, reverse):
    n, w = xraw.shape
    nt = n // TM
    hb = TM // SUBLANES
    idx = lambda j: _lru_tile_index(j, nt, reverse)
    last8 = n // SUBLANES - 1
    wg = jnp.concatenate([w_a, w_x], axis=-1).astype(BF16)
    return pl.pallas_call(
        functools.partial(_lru_scan_kernel, reverse=reverse, n_tiles=nt),
        out_shape=jax.ShapeDtypeStruct((n, w), BF16),
        grid=(nt,),
        in_specs=[pl.BlockSpec((TM, w), lambda j: (idx(j), 0)),
                  pl.BlockSpec((SUBLANES, w), lambda j: (jnp.maximum(idx(j) * hb - 1, 0), 0)),
                  pl.BlockSpec((SUBLANES, w), lambda j: (jnp.minimum((idx(j) + 1) * hb, last8), 0)),
                  _full((4, w)), _full((1, w)), _full(wg.shape), _full((1, w)), _full((1, w)),
                  _full((1, w))],
        out_specs=pl.BlockSpec((TM, w), lambda j: (idx(j), 0)),
        scratch_shapes=[pltpu.VMEM((TM, w), F32), pltpu.VMEM((TM, w), F32),
                        pltpu.VMEM((SUBLANES, w), F32)],
        compiler_params=_cparams("arbitrary"),
        name="rglru_rev" if reverse else "rglru_fwd",
    )(xraw, xraw, xraw, conv_w, conv_b.reshape(1, w), wg, b_a.reshape(1, w), b_x.reshape(1, w),
      lam.reshape(1, w))


def _lru_out_kernel(gl_ref, hf_ref, hr_ref, s_ref, mod_ref, w_ref, out_ref):
    gate = _mod_rows(mod_ref, 1, 0, 0)[2]
    p = gl_ref[...].astype(F32) * (hf_ref[...].astype(F32) + hr_ref[...].astype(F32))
    out_ref[...] = s_ref[...] + gate * _bdot(p, w_ref[...])


def _lru_out(gl, hf, hr, s, mods, w_out):
    n, d = s.shape
    w = gl.shape[1]
    lat = lambda c: pl.BlockSpec((TM, c), lambda i: (i + 1, 0))
    return pl.pallas_call(
        _lru_out_kernel,
        out_shape=jax.ShapeDtypeStruct((n - TM, d), F32),
        grid=(n // TM - 1,),
        in_specs=[lat(w), lat(w), lat(w), lat(d), _full(mods.shape), _full(w_out.shape)],
        out_specs=pl.BlockSpec((TM, d), lambda i: (i, 0)),
        compiler_params=_cparams("arbitrary"),
        name="rglru_out",
    )(gl, hf, hr, s, mods, w_out.astype(BF16))


def _rglru_layer(s, mods, norm_g, w_in, conv_w, conv_b, w_a, b_a, w_x, b_x, lam, w_out):
    gl, xraw = _lru_proj(s, mods, norm_g, w_in)
    hf = _lru_scan(xraw, conv_w, conv_b, w_a[0], b_a[0], w_x[0], b_x[0], lam[0], False)
    hr = _lru_scan(xraw, conv_w, conv_b, w_a[1], b_a[1], w_x[1], b_x[1], lam[1], True)
    return _lru_out(gl, hf, hr, s, mods, w_out)


def _fn_proj_kernel(s_ref, mod_ref, g_ref, wt_ref, cs_ref, yr_ref, yi_ref, ar_scr, ai_scr):
    shift, scale, _ = _mod_rows(mod_ref, 1, 0, 0)
    nm = wt_ref.shape[0]
    gw = nm // FN_GROUPS
    per = TM // FFT_N2
    nj = FN_TB // FFT_N2
    csb = cs_ref[...].astype(BF16)

    def sub(tc, c):
        r0 = pl.multiple_of(tc * TM, TM)
        h = _rms(s_ref[pl.ds(r0, TM), :], g_ref[...]) * (1.0 + scale) + shift
        zt = lax.dot_general(wt_ref[...], h.astype(BF16), (((1,), (1,)), ((), ())),
                             preferred_element_type=F32).astype(BF16)
        for g in range(FN_GROUPS):
            y = jnp.dot(csb, zt[g * gw:(g + 1) * gw, :], preferred_element_type=F32)
            for q in range(per):
                row0 = pl.multiple_of((tc * per + q) * _slab_pitch(nm) + g * gw, SUBLANES)
                ar_scr[pl.ds(row0, gw), :] = y[:gw, q * FFT_N2:(q + 1) * FFT_N2]
                ai_scr[pl.ds(row0, gw), :] = y[gw:, q * FFT_N2:(q + 1) * FFT_N2]
        return c

    lax.fori_loop(0, FN_TB // TM, sub, 0)

    def relayout(m, c):
        yr_ref[m] = ar_scr[pl.ds(m, nj, stride=_slab_pitch(nm)), :]
        yi_ref[m] = ai_scr[pl.ds(m, nj, stride=_slab_pitch(nm)), :]
        return c

    lax.fori_loop(0, nm, relayout, 0, unroll=8)


def _slab_pitch(rows):
    return rows + SUBLANES


def _dft_cos_sin(n, scale):
    k = np.arange(n, dtype=np.int64)
    ang = 2.0 * np.pi * ((k[:, None] * k[None, :]) % n).astype(np.float64) / n
    return np.cos(ang) * scale, np.sin(ang) * scale


def _fn_proj(s, mods, norm_g, w_in):
    t, d = s.shape
    nm = w_in.shape[1]
    gw = nm // FN_GROUPS
    n1 = t // FFT_N2
    nj = FN_TB // FFT_N2
    c, sn = _dft_cos_sin(gw, gw ** -0.5)
    cs = jnp.asarray(np.concatenate([c, -sn], axis=0), F32)
    yspec = pl.BlockSpec((nm, nj, FFT_N2), lambda i: (0, i, 0))
    yshape = jax.ShapeDtypeStruct((nm, n1, FFT_N2), F32)
    return pl.pallas_call(
        _fn_proj_kernel,
        out_shape=(yshape, yshape),
        grid=(t // FN_TB,),
        in_specs=[pl.BlockSpec((FN_TB, d), lambda i: (i, 0)), _full(mods.shape), _full((1, d)),
                  _full((nm, d)), _full(cs.shape)],
        out_specs=(yspec, yspec),
        scratch_shapes=[pltpu.VMEM((nj * _slab_pitch(nm), FFT_N2), F32),
                        pltpu.VMEM((nj * _slab_pitch(nm), FFT_N2), F32)],
        compiler_params=_cparams("arbitrary"),
        name="fourier_proj",
    )(s, mods, norm_g.reshape(1, d), w_in.T.astype(BF16), cs)


def _fn_fft_kernel(yr_ref, yi_ref, m_ref, tc_ref, ts_ref, d_ref, o_ref):
    n1 = yr_ref.shape[1]
    n2 = FFT_N2
    xr = jnp.concatenate([yr_ref[m].astype(BF16) for m in range(FN_CB)], axis=1)
    xi = jnp.concatenate([yi_ref[m].astype(BF16) for m in range(FN_CB)], axis=1)
    a = jnp.dot(m_ref[...].astype(BF16), jnp.concatenate([xr, xi], axis=0),
                preferred_element_type=F32)
    ar = a[:n1]
    ai = a[n1:]
    tc = jnp.concatenate([tc_ref[...]] * FN_CB, axis=1)
    ts = jnp.concatenate([ts_ref[...]] * FN_CB, axis=1)
    br = ar * tc + ai * ts
    bi = ai * tc - ar * ts
    bst = jnp.concatenate(
        [jnp.concatenate([br[:, m * n2:(m + 1) * n2], bi[:, m * n2:(m + 1) * n2]], axis=1)
         for m in range(FN_CB)], axis=0).astype(BF16)
    res = lax.dot_general(d_ref[...].astype(BF16), bst, (((1,), (1,)), ((), ())),
                          preferred_element_type=F32)
    for m in range(FN_CB):
        o_ref[m] = res[:, m * n1:(m + 1) * n1]


def _fn_fft(yr, yi):
    nm, n1, n2 = yr.shape
    t = n1 * n2
    c, sn = _dft_cos_sin(n1, n1 ** -0.5)
    m = jnp.asarray(np.block([[c, sn], [-sn, c]]), F32)
    k1 = np.arange(n1, dtype=np.int64)[:, None]
    t2 = np.arange(n2, dtype=np.int64)[None, :]
    ang = 2.0 * np.pi * ((k1 * t2) % t).astype(np.float64) / t
    tc = jnp.asarray(np.cos(ang), F32)
    ts = jnp.asarray(np.sin(ang), F32)
    c2, s2 = _dft_cos_sin(n2, n2 ** -0.5)
    dm = jnp.asarray(np.concatenate([c2, s2], axis=1), F32)
    yspec = pl.BlockSpec((FN_CB, n1, n2), lambda i: (i, 0, 0))
    return pl.pallas_call(
        _fn_fft_kernel,
        out_shape=jax.ShapeDtypeStruct((nm, n2, n1), F32),
        grid=(nm // FN_CB,),
        in_specs=[yspec, yspec, _full(m.shape), _full(tc.shape), _full(ts.shape), _full(dm.shape)],
        out_specs=pl.BlockSpec((FN_CB, n2, n1), lambda i: (i, 0, 0)),
        compiler_params=_cparams("arbitrary"),
        name="fourier_fft",
    )(yr, yi, m, tc, ts, dm)


def _fn_out_kernel(ft_ref, w_ref, s_ref, mod_ref, o_ref, a_scr):
    gate = _mod_rows(mod_ref, 1, 0, 0)[2]
    nm, nj, n1 = ft_ref.shape

    def relayout(m, c):
        a_scr[pl.ds(m, nj, stride=_slab_pitch(nm)), :] = ft_ref[m]
        return c

    lax.fori_loop(0, nm, relayout, 0, unroll=8)
    for j in range(nj):
        p0 = j * _slab_pitch(nm)
        slab = a_scr[p0:p0 + nm, :].astype(BF16)
        y = lax.dot_general(slab, w_ref[...], (((0,), (0,)), ((), ())), preferred_element_type=F32)
        rows = slice(j * n1, (j + 1) * n1)
        o_ref[rows, :] = s_ref[rows, :] + gate * y


def _fn_out(ft, s, mods, w_out):
    t, d = s.shape
    nm, n2, n1 = ft.shape
    nj = FN_TB // n1
    tok = pl.BlockSpec((FN_TB, d), lambda i: (i, 0))
    return pl.pallas_call(
        _fn_out_kernel,
        out_shape=jax.ShapeDtypeStruct((t, d), F32),
        grid=(t // FN_TB,),
        in_specs=[pl.BlockSpec((nm, nj, n1), lambda i: (0, i, 0)), _full(w_out.shape), tok,
                  _full(mods.shape)],
        out_specs=tok,
        scratch_shapes=[pltpu.VMEM((nj * _slab_pitch(nm), n1), F32)],
        compiler_params=_cparams("arbitrary"),
        name="fourier_out",
    )(ft, w_out.astype(BF16), s, mods)


def _fourier_layer(s, mods, norm_g, w_in, w_out):
    yr, yi = _fn_proj(s, mods, norm_g, w_in)
    return _fn_out(_fn_fft(yr, yi), s, mods, w_out)


def kernel(x, c, ctx, c_ctx, ada_w, ada_b, norm_mix_g, norm_ffn_g, final_norm_g, router_group_w, router_group_b, router_expert_w, router_expert_b, expert_w_gate, expert_w_up, expert_w_down, cm_w_in, cm_v_norm_g, cm_w_s, cm_b_s, cm_w_out, ml_w_in, ml_conv_w, ml_conv_b, ml_gate_b, ml_norm_g, ml_w_out, lru_w_in, lru_conv_w, lru_conv_b, lru_w_a, lru_b_a, lru_w_x, lru_b_x, lru_lambda, lru_w_out, fn_w_in, fn_w_out):
    bsz, seq, d = x.shape
    assert bsz == 1 and ada_w.shape[0] == 4 and ctx.shape[1] == TM
    c_rows = jnp.concatenate([c_ctx[None, :], c, jnp.zeros((SUBLANES - 2, d), F32)], axis=0)
    mods = _ada_table(c_rows, ada_w, ada_b)

    def moe(s, i, ctx_tiles, final_norm=False):
        return _moe_layer(s, mods[i], norm_ffn_g[i], router_group_w[i], router_group_b[i],
                          router_expert_w[i], router_expert_b[i], expert_w_gate, expert_w_up,
                          expert_w_down, i, ctx_tiles, final_norm_g, final_norm)

    s = _chunk_mlp_layer(x[0], ctx[0], mods[0], norm_mix_g[0], cm_w_in[0], cm_v_norm_g[0],
                         cm_w_s[0], cm_b_s[0], cm_w_out[0])
    s = moe(s, 0, 1)
    s = _mlstm_layer(s, mods[1], norm_mix_g[1], ml_w_in[0], ml_conv_w[0], ml_conv_b[0],
                     ml_gate_b[0], ml_norm_g[0], ml_w_out[0])
    s = moe(s, 1, 1)
    s = _rglru_layer(s, mods[2], norm_mix_g[2], lru_w_in[0], lru_conv_w[0], lru_conv_b[0],
                     lru_w_a[0], lru_b_a[0], lru_w_x[0], lru_b_x[0], lru_lambda[0], lru_w_out[0])
    s = moe(s, 2, 0)
    s = _fourier_layer(s, mods[3], norm_mix_g[3], fn_w_in[0], fn_w_out[0])
    s = moe(s, 3, 0, final_norm=True)
    return s[None]
```

```python
import functools
import math

import jax
import jax.numpy as jnp
import numpy as np
from jax import lax
from jax.experimental import pallas as pl
from jax.experimental.pallas import tpu as pltpu

F32 = jnp.float32
BF16 = jnp.bfloat16

EPS = 1e-6
POS_BASE = 10000.0
GRID_W = 64
N_MOD = 6
TM = 256
LANES = 128
SUBLANES = 8
VMEM_LIMIT = 56 * 1024 * 1024

CM_CHUNK = 128
CM_GROUPS = 4
ML_HEADS = 4
ML_DK = 128
ML_DV = 256
ML_CHUNK = 128
LRU_HEADS = 10
LRU_BLOCK = 128
LRU_C = 8.0
FN_GROUPS = 4
FFT_N2 = 128
MOE_GROUPS = 4
MOE_EPG = 8
MOE_EXPERTS = MOE_GROUPS * MOE_EPG
MOE_ROWS_LOG2 = 8
MOE_ROWS = 1 << MOE_ROWS_LOG2
ROUTE_ROWS = 40
FN_TB = 1024
FN_CB = 8
CONV_LEFT = 2

HI = lax.Precision.HIGHEST


def _cparams(*sem):
    return pltpu.CompilerParams(dimension_semantics=sem, vmem_limit_bytes=VMEM_LIMIT)


def _full(shape):
    nd = len(shape)
    return pl.BlockSpec(shape, lambda *_: (0,) * nd)


def _rms(x, g):
    return x * lax.rsqrt(jnp.mean(x * x, axis=-1, keepdims=True) + EPS) * g


def _gelu(x):
    c = math.sqrt(2.0 / math.pi)
    return 0.5 * x * (1.0 + jnp.tanh(c * (x + 0.044715 * (x * x * x))))


def _sigmoid(x):
    return 0.5 * jnp.tanh(0.5 * x) + 0.5


def _silu(x):
    return x * _sigmoid(x)


def _softplus(x):
    return jnp.maximum(x, 0.0) + jnp.log(1.0 + jnp.exp(-jnp.abs(x)))


def _mod_rows(mod_ref, tile, ctx_tiles, first):
    row = jnp.where(tile < ctx_tiles, 0, 1)
    m = mod_ref[pl.ds(row, 1), :]
    d = m.shape[1] // N_MOD
    return tuple(m[:, (first + j) * d:(first + j + 1) * d] for j in range(3))


def _bdot(a, b):
    return jnp.dot(a.astype(BF16), b.astype(BF16), preferred_element_type=F32)


def _split_bf16(x):
    hi = x.astype(BF16)
    return hi, (x - hi.astype(F32)).astype(BF16)


def _dot3(a, b, dims):
    a_hi, a_lo = _split_bf16(a)
    b_hi, b_lo = _split_bf16(b)
    dg = functools.partial(lax.dot_general, dimension_numbers=(dims, ((), ())),
                           preferred_element_type=F32)
    return dg(a_hi, b_hi) + dg(a_hi, b_lo) + dg(a_lo, b_hi)


def _ada_kernel(c_ref, w_ref, b_ref, o_ref):
    c = c_ref[...]
    o_ref[...] = _dot3(_silu(c), w_ref[...], ((1,), (0,))) + b_ref[...]


def _ada_table(c_rows, ada_w, ada_b):
    depth, d, n = ada_w.shape
    tn = 2048
    return pl.pallas_call(
        _ada_kernel,
        out_shape=jax.ShapeDtypeStruct((depth, SUBLANES, n), F32),
        grid=(depth, n // tn),
        in_specs=[_full((SUBLANES, d)),
                  pl.BlockSpec((None, d, tn), lambda i, j: (i, 0, j)),
                  pl.BlockSpec((None, 1, tn), lambda i, j: (i, 0, j))],
        out_specs=pl.BlockSpec((None, SUBLANES, tn), lambda i, j: (i, 0, j)),
        compiler_params=_cparams("arbitrary", "arbitrary"),
        name="ada_table",
    )(c_rows, ada_w, ada_b.reshape(depth, 1, n))


def _pos_tables(seq, d):
    q = d // 4
    freq = jnp.exp(-math.log(POS_BASE) * jnp.arange(q, dtype=F32) / q)
    ar = jnp.arange(seq // GRID_W, dtype=F32)[:, None] * freq
    ac = jnp.arange(GRID_W, dtype=F32)[:, None] * freq
    return (jnp.concatenate([jnp.sin(ar), jnp.cos(ar)], axis=-1),
            jnp.concatenate([jnp.sin(ac), jnp.cos(ac)], axis=-1))


def _cm_kernel(x_ref, ctx_ref, rt_ref, ct_ref, mod_ref, g_ref, win_ref, vg_ref, ws_ref, bs_ref,
               wout_ref, o_ref, p_scr, x_scr):
    i = pl.program_id(0)

    @pl.when(i == 0)
    def _():
        x_scr[...] = ctx_ref[...]

    @pl.when(i > 0)
    def _():
        rows_per_tile = TM // GRID_W
        q2 = rt_ref.shape[1]
        r0 = (i - 1) * rows_per_tile
        rt = jnp.concatenate(
            [jnp.broadcast_to(rt_ref[pl.ds(r0 + j, 1), :], (GRID_W, q2))
             for j in range(rows_per_tile)], axis=0)
        ct = jnp.concatenate([ct_ref[...]] * rows_per_tile, axis=0)
        x_scr[...] = x_ref[...] + jnp.concatenate([rt, ct], axis=1)

    shift, scale, gate = _mod_rows(mod_ref, i, 1, 0)
    x = x_scr[...]
    h = _rms(x, g_ref[...]) * (1.0 + scale) + shift
    z = _gelu(_bdot(h, win_ref[...]))
    w = z.shape[1] // 2
    u = z[:, :w]
    v = _rms(z[:, w:], vg_ref[...]).astype(BF16)
    gw = w // CM_GROUPS
    for c in range(TM // CM_CHUNK):
        r = slice(c * CM_CHUNK, (c + 1) * CM_CHUNK)
        for g in range(CM_GROUPS):
            cs = slice(g * gw, (g + 1) * gw)
            s = jnp.dot(ws_ref[g], v[r, cs], preferred_element_type=F32) + bs_ref[:, g:g + 1]
            p_scr[r, cs] = (u[r, cs] * s).astype(BF16)
    y = jnp.dot(p_scr[...], wout_ref[...], preferred_element_type=F32)
    o_ref[...] = x + gate * y


def _chunk_mlp_layer(x2, ctx2, mods, norm_g, w_in, v_g, w_s, b_s, w_out):
    seq, d = x2.shape
    n_ctx = ctx2.shape[0]
    assert n_ctx == TM and seq % TM == 0 and TM % GRID_W == 0
    n = n_ctx + seq
    w = w_out.shape[0]
    rt, ct = _pos_tables(seq, d)
    return pl.pallas_call(
        _cm_kernel,
        out_shape=jax.ShapeDtypeStruct((n, d), F32),
        grid=(n // TM,),
        in_specs=[pl.BlockSpec((TM, d), lambda i: (jnp.maximum(i - 1, 0), 0)),
                  _full((TM, d)), _full(rt.shape), _full(ct.shape),
                  _full(mods.shape), _full((1, d)), _full(w_in.shape), _full((1, w)),
                  _full(w_s.shape), _full((CM_CHUNK, CM_GROUPS)), _full(w_out.shape)],
        out_specs=pl.BlockSpec((TM, d), lambda i: (i, 0)),
        scratch_shapes=[pltpu.VMEM((TM, w), BF16), pltpu.VMEM((TM, d), F32)],
        compiler_params=_cparams("arbitrary"),
        name="chunk_mlp",
    )(x2, ctx2, rt, ct, mods, norm_g.reshape(1, d), w_in.astype(BF16), v_g.reshape(1, w),
      w_s.astype(BF16), b_s.T, w_out.astype(BF16))


def _store_token_tiles(ref, x):
    rows, d = x.shape
    for j in range(d // LANES):
        ref[pl.ds(j, rows, stride=d // LANES), :] = x[:, j * LANES:(j + 1) * LANES]


def _load_token_tiles(ref):
    chunks = SUBLANES
    rows = ref.shape[0] // chunks
    return jnp.concatenate([ref[pl.ds(j, rows, stride=chunks), :] for j in range(chunks)], axis=1)


def _route_tiles(nt):
    return next(k for k in (5, 4, 2, 1) if nt % k == 0)


def _router_kernel(s_ref, mod_ref, g_ref, rwt_ref, rbt_ref, tri_ref, h_ref, e1_ref, e2_ref, r1_ref,
                   r2_ref, wt_ref, cnt_ref, carry, *, ctx_rows):
    i = pl.program_id(0)
    rows, d = s_ref.shape

    @pl.when(i == 0)
    def _():
        carry[...] = jnp.zeros_like(carry)

    lat = mod_ref[1:2, :]
    shift, scale = lat[:, 3 * d:4 * d], lat[:, 4 * d:5 * d]
    if ctx_rows:
        ctx = mod_ref[0:1, :]
        is_ctx = (i == 0) & (lax.broadcasted_iota(jnp.int32, (rows, 1), 0) < ctx_rows)
        shift = jnp.where(is_ctx, ctx[:, 3 * d:4 * d], shift)
        scale = jnp.where(is_ctx, ctx[:, 4 * d:5 * d], scale)
    h = _rms(s_ref[...], g_ref[...]) * (1.0 + scale) + shift
    _store_token_tiles(h_ref, h)
    logits = _dot3(rwt_ref[...], h, ((1,), (1,))) + rbt_ref[...]
    row = lax.broadcasted_iota(jnp.int32, logits.shape, 0)
    neg = jnp.float32(-jnp.inf)
    big = jnp.int32(1 << 20)
    is_g = row < MOE_GROUPS
    gl = jnp.where(is_g, logits, neg)
    gmax = jnp.max(gl, axis=0, keepdims=True)
    grp = jnp.min(jnp.where(is_g & (gl == gmax), row, big), axis=0, keepdims=True)
    p_grp = 1.0 / jnp.sum(jnp.exp(gl - gmax), axis=0, keepdims=True)
    e_row = row - MOE_GROUPS
    in_grp = (e_row >= 0) & (e_row < MOE_EXPERTS) & ((e_row >> 3) == grp)
    l1 = jnp.where(in_grp, logits, neg)
    v1 = jnp.max(l1, axis=0, keepdims=True)
    i1 = jnp.min(jnp.where(in_grp & (l1 == v1), row, big), axis=0, keepdims=True)
    rest = in_grp & (row != i1)
    l2 = jnp.where(rest, logits, neg)
    v2 = jnp.max(l2, axis=0, keepdims=True)
    i2 = jnp.min(jnp.where(rest & (l2 == v2), row, big), axis=0, keepdims=True)
    e21 = jnp.exp(v2 - v1)
    w1 = p_grp / (1.0 + e21)
    w2 = p_grp * e21 / (1.0 + e21)
    oh1 = (row == i1).astype(F32)
    oh2 = (row == i2).astype(F32)
    oh = oh1 + oh2
    before = jnp.dot(oh.astype(BF16), tri_ref[...], preferred_element_type=F32) + carry[:, 0:1]
    r1_ref[...] = jnp.sum(oh1 * before, axis=0, keepdims=True).astype(jnp.int32)
    r2_ref[...] = jnp.sum(oh2 * before, axis=0, keepdims=True).astype(jnp.int32)
    e1_ref[...] = i1 - MOE_GROUPS
    e2_ref[...] = i2 - MOE_GROUPS
    carry[...] = carry[...] + jnp.sum(oh, axis=1, keepdims=True)
    cnt_ref[...] = carry[...]
    wt_ref[...] = jnp.concatenate([w1, w2, jnp.zeros((LANES - 2, rows), F32)], axis=0).T


def _router(s, mods, norm_g, rg_w, rg_b, re_w, re_b, ctx_tiles):
    n, d = s.shape
    rows = TM * _route_tiles(n // TM)
    steps = n // rows
    pad = ROUTE_ROWS - MOE_GROUPS - MOE_EXPERTS
    rwt = jnp.concatenate([rg_w, re_w, jnp.zeros((d, pad), F32)], axis=1).T
    rbt = jnp.broadcast_to(jnp.concatenate([rg_b, re_b, jnp.zeros((pad,), F32)])[:, None],
                           (ROUTE_ROWS, rows))
    tri = jnp.asarray(np.triu(np.ones((rows, rows), np.float32), 1), BF16)
    assert d == SUBLANES * LANES
    tile = pl.BlockSpec((rows, d), lambda i: (i, 0))
    irow = pl.BlockSpec((None, 1, rows), lambda i: (i, 0, 0))
    ishape = jax.ShapeDtypeStruct((steps, 1, rows), jnp.int32)
    return pl.pallas_call(
        functools.partial(_router_kernel, ctx_rows=ctx_tiles * TM),
        out_shape=(jax.ShapeDtypeStruct((n * SUBLANES, LANES), F32), ishape, ishape, ishape, ishape,
                   jax.ShapeDtypeStruct((n, LANES), F32),
                   jax.ShapeDtypeStruct((ROUTE_ROWS, LANES), F32)),
        grid=(steps,),
        in_specs=[tile, _full(mods.shape), _full((1, d)), _full((ROUTE_ROWS, d)),
                  _full((ROUTE_ROWS, rows)), _full((rows, rows))],
        out_specs=(pl.BlockSpec((rows * SUBLANES, LANES), lambda i: (i, 0)), irow, irow, irow, irow,
                   pl.BlockSpec((rows, LANES), lambda i: (i, 0)), _full((ROUTE_ROWS, LANES))),
        scratch_shapes=[pltpu.VMEM((ROUTE_ROWS, LANES), F32)],
        compiler_params=_cparams("arbitrary"),
        name="moe_router",
    )(s, mods, norm_g.reshape(1, d), rwt, rbt, tri)


def _finalize_kernel(cnt_ref, e1_ref, e2_ref, r1_ref, r2_ref, d1_ref, d2_ref, blk_ref):
    e1 = e1_ref[...]
    e2 = e2_ref[...]
    r1 = r1_ref[...]
    r2 = r2_ref[...]
    d1 = jnp.zeros_like(e1)
    d2 = jnp.zeros_like(e2)
    lane = lax.broadcasted_iota(jnp.int32, blk_ref.shape, 1)
    brow = lane * MOE_ROWS
    sub = lax.broadcasted_iota(jnp.int32, blk_ref.shape, 0)
    be = jnp.zeros(blk_ref.shape, jnp.int32)
    pend = jnp.zeros(blk_ref.shape, jnp.int32)
    ps = jnp.int32(0)
    for e in range(MOE_EXPERTS):
        c = cnt_ref[e]
        pe = ps + lax.shift_left(lax.shift_right_logical(c + (MOE_ROWS - 1), MOE_ROWS_LOG2),
                                 MOE_ROWS_LOG2)
        d1 = jnp.where(e1 == e, ps + r1, d1)
        d2 = jnp.where(e2 == e, ps + r2, d2)
        be = be + (brow >= pe).astype(jnp.int32)
        pend = jnp.where(lane == e, pe, pend)
        ps = pe
    d1_ref[...] = d1
    d2_ref[...] = d2
    n_used = lax.shift_right_logical(ps, MOE_ROWS_LOG2)
    blk_ref[...] = jnp.where(sub == 0, jnp.minimum(be, MOE_EXPERTS - 1),
                             jnp.where(sub == 1, pend, n_used))


def _finalize(counts, e1, e2, r1, r2, nb):
    nbp = (nb + LANES - 1) // LANES * LANES
    whole = pl.BlockSpec(e1.shape, lambda i, c: (0, 0, 0))
    ishape = jax.ShapeDtypeStruct(e1.shape, jnp.int32)
    return pl.pallas_call(
        _finalize_kernel,
        out_shape=(ishape, ishape, jax.ShapeDtypeStruct((SUBLANES, nbp), jnp.int32)),
        grid_spec=pltpu.PrefetchScalarGridSpec(
            num_scalar_prefetch=1,
            grid=(1,),
            in_specs=[whole, whole, whole, whole],
            out_specs=(whole, whole, pl.BlockSpec((SUBLANES, nbp), lambda i, c: (0, 0)))),
        compiler_params=_cparams("arbitrary"),
        name="moe_finalize",
    )(counts, e1, e2, r1, r2)


def _token_copy(src, r, dst, d, sem):
    return pltpu.make_async_copy(src.at[pl.ds(pl.multiple_of(r * SUBLANES, SUBLANES), SUBLANES), :],
                                 dst.at[pl.ds(pl.multiple_of(d * SUBLANES, SUBLANES), SUBLANES), :],
                                 sem)


def _zero_fill_padding(pend_ref, nu_ref, xs_out, zbuf, zsem):
    blk_rows = MOE_ROWS * SUBLANES
    nb = xs_out.shape[0] // blk_rows
    zbuf[...] = jnp.zeros_like(zbuf)

    def block_copy(b):
        r0 = pl.multiple_of(b * blk_rows, blk_rows)
        return pltpu.make_async_copy(zbuf, xs_out.at[pl.ds(r0, blk_rows), :], zsem)

    def seg_last_block(e):
        pe = pend_ref[e]
        prev = pend_ref[e - 1] if e > 0 else 0
        return pe > prev, lax.shift_right_logical(pe, MOE_ROWS_LOG2) - 1

    for e in range(MOE_EXPERTS):
        nonempty, b = seg_last_block(e)

        @pl.when(nonempty)
        def _():
            block_copy(b).start()

    def tail_start(b, c):
        block_copy(b).start()
        return c

    lax.fori_loop(nu_ref[0], nb, tail_start, 0)
    for e in range(MOE_EXPERTS):
        nonempty, b = seg_last_block(e)

        @pl.when(nonempty)
        def _():
            block_copy(b).wait()

    def tail_wait(b, c):
        block_copy(b).wait()
        return c

    lax.fori_loop(nu_ref[0], nb, tail_wait, 0)


def _dispatch_kernel(dest_ref, pend_ref, nu_ref, h_ref, xs_out, sem, zbuf, zsem, *, tiles):
    i = pl.program_id(0)

    @pl.when(i == 0)
    def _():
        _zero_fill_padding(pend_ref, nu_ref, xs_out, zbuf, zsem)

    rows = tiles * TM
    for q in range(tiles):
        base = i * (2 * rows) + q * TM

        def start(r, c):
            _token_copy(h_ref, q * TM + r, xs_out, dest_ref[base + r], sem).start(priority=0)
            _token_copy(h_ref, q * TM + r, xs_out, dest_ref[base + rows + r],
                        sem).start(priority=1)
            return c

        lax.fori_loop(0, TM, start, 0, unroll=8)
    for _ in range(2):
        pltpu.make_async_copy(h_ref, xs_out.at[pl.ds(0, tiles * TM * SUBLANES), :], sem).wait()


def _dispatch(dest, pad_end, n_used, h, n_rows):
    n = h.shape[0] // SUBLANES
    nt = n // TM
    tiles = _route_tiles(nt)
    return pl.pallas_call(
        functools.partial(_dispatch_kernel, tiles=tiles),
        out_shape=jax.ShapeDtypeStruct((n_rows * SUBLANES, LANES), F32),
        grid_spec=pltpu.PrefetchScalarGridSpec(
            num_scalar_prefetch=3,
            grid=(nt // tiles,),
            in_specs=[pl.BlockSpec((tiles * TM * SUBLANES, LANES), lambda i, *_: (i, 0))],
            out_specs=pl.BlockSpec(memory_space=pl.ANY),
            scratch_shapes=[pltpu.SemaphoreType.DMA, pltpu.VMEM((MOE_ROWS * SUBLANES, LANES), F32),
                            pltpu.SemaphoreType.DMA]),
        compiler_params=_cparams("arbitrary"),
        name="moe_dispatch",
    )(dest, pad_end, n_used, h)


X_SLOTS = 3


def _expert_kernel(be_ref, pend_ref, nu_ref, xs_hbm, wg_hbm, wu_hbm, wd_hbm, y_ref,
                   xbuf, wg_f, wu_f, wd_f, wg_s, wu_s, wd_s, xsem, wsem, ord_ref, *, layer):
    b = pl.program_id(0)
    nu = nu_ref[0]
    blk_rows = MOE_ROWS * SUBLANES

    def x_copy(blk, slot):
        r0 = pl.multiple_of(blk * blk_rows, blk_rows)
        return pltpu.make_async_copy(xs_hbm.at[pl.ds(r0, blk_rows), :], xbuf.at[slot],
                                     xsem.at[slot])

    def w_copies(e, slot):
        return (pltpu.make_async_copy(wg_hbm.at[layer, e], wg_f.at[slot], wsem.at[slot]),
                pltpu.make_async_copy(wu_hbm.at[layer, e], wu_f.at[slot], wsem.at[slot]),
                pltpu.make_async_copy(wd_hbm.at[layer, e], wd_f.at[slot], wsem.at[slot]))

    @pl.when(b == 0)
    def _():
        ord_ref[0] = 0
        for j in range(X_SLOTS - 1):
            @pl.when(j < nu)
            def _():
                x_copy(j, j).start()

        @pl.when(nu > 0)
        def _():
            for c in w_copies(be_ref[0], 0):
                c.start()

    ahead = b + (X_SLOTS - 1)

    @pl.when(ahead < nu)
    def _():
        x_copy(ahead, lax.rem(ahead, X_SLOTS)).start()

    used = b < nu
    e = be_ref[b]
    fresh = used & ((b == 0) | (e != be_ref[jnp.maximum(b - 1, 0)]))

    @pl.when(fresh)
    def _():
        k = ord_ref[0]
        slot = lax.rem(k, 2)
        for c in w_copies(e, slot):
            c.wait()
        wg_s[...] = wg_f[slot].astype(BF16)
        wu_s[...] = wu_f[slot].astype(BF16)
        wd_s[...] = wd_f[slot].astype(BF16)
        nxt = lax.shift_right_logical(pend_ref[e], MOE_ROWS_LOG2)

        @pl.when(nxt < nu)
        def _():
            for c in w_copies(be_ref[nxt], 1 - slot):
                c.start(priority=1)

        ord_ref[0] = k + 1

    @pl.when(used)
    def _():
        slot = lax.rem(b, X_SLOTS)
        x_copy(b, slot).wait()
        x = _load_token_tiles(xbuf.at[slot]).astype(BF16)
        a = jnp.dot(x, wg_s[...], preferred_element_type=F32)
        u = jnp.dot(x, wu_s[...], preferred_element_type=F32)
        _store_token_tiles(y_ref, jnp.dot((_silu(a) * u).astype(BF16), wd_s[...],
                                          preferred_element_type=F32))

    @pl.when(jnp.logical_not(used))
    def _():
        y_ref[...] = jnp.zeros_like(y_ref)


def _experts(blk_expert, pad_end, n_used, xs, w_gate, w_up, w_down, layer):
    d, hid = w_gate.shape[2:]
    blk_rows = MOE_ROWS * SUBLANES
    nb = xs.shape[0] // blk_rows
    hbm = pl.BlockSpec(memory_space=pl.ANY)
    return pl.pallas_call(
        functools.partial(_expert_kernel, layer=layer),
        out_shape=jax.ShapeDtypeStruct(xs.shape, F32),
        grid_spec=pltpu.PrefetchScalarGridSpec(
            num_scalar_prefetch=3,
            grid=(nb,),
            in_specs=[hbm, hbm, hbm, hbm],
            out_specs=pl.BlockSpec((blk_rows, LANES), lambda b, *_: (b, 0)),
            scratch_shapes=[pltpu.VMEM((X_SLOTS, blk_rows, LANES), F32),
                            pltpu.VMEM((2, d, hid), F32), pltpu.VMEM((2, d, hid), F32),
                            pltpu.VMEM((2, hid, d), F32),
                            pltpu.VMEM((d, hid), BF16), pltpu.VMEM((d, hid), BF16),
                            pltpu.VMEM((hid, d), BF16),
                            pltpu.SemaphoreType.DMA((X_SLOTS,)), pltpu.SemaphoreType.DMA((2,)),
                            pltpu.SMEM((1,), jnp.int32)]),
        compiler_params=_cparams("arbitrary"),
        name="moe_experts",
    )(blk_expert, pad_end, n_used, xs, w_gate, w_up, w_down)


def _combine_kernel(dest_ref, s_ref, wt_ref, mod_ref, fg_ref, ys_ref, o_ref, ybuf, sem,
                    *, ctx_tiles, final_norm, tiles):
    i = pl.program_id(0)
    slot = i % 2
    rows = tiles * TM

    def gather(tile, slot):
        tile = jnp.asarray(tile, jnp.int32)
        base = lax.div(tile, tiles) * (2 * rows) + lax.rem(tile, tiles) * TM

        def start(r, c):
            _token_copy(ys_ref, dest_ref[base + r], ybuf.at[slot, 0], r,
                        sem.at[slot]).start(priority=0)
            _token_copy(ys_ref, dest_ref[base + rows + r], ybuf.at[slot, 1], r,
                        sem.at[slot]).start(priority=1)
            return c

        lax.fori_loop(0, TM, start, 0, unroll=8)

    @pl.when(i == 0)
    def _():
        gather(0, 0)

    @pl.when(i + 1 < pl.num_programs(0))
    def _():
        gather(i + 1, 1 - slot)

    for k in range(2):
        pltpu.make_async_copy(ys_ref.at[pl.ds(0, TM * SUBLANES), :], ybuf.at[slot, k],
                              sem.at[slot]).wait()
    gate = _mod_rows(mod_ref, i, ctx_tiles, 3)[2]
    wt = wt_ref[...]
    y = (wt[:, 0:1] * _load_token_tiles(ybuf.at[slot, 0])
         + wt[:, 1:2] * _load_token_tiles(ybuf.at[slot, 1]))
    out = s_ref[...] + gate * y
    if final_norm:
        out = _rms(out, fg_ref[...])
    o_ref[...] = out


def _combine(dest, s, wts, mods, final_g, ys, ctx_tiles, final_norm):
    n, d = s.shape
    return pl.pallas_call(
        functools.partial(_combine_kernel, ctx_tiles=ctx_tiles, final_norm=final_norm,
                          tiles=_route_tiles(n // TM)),
        out_shape=jax.ShapeDtypeStruct((n, d), F32),
        grid_spec=pltpu.PrefetchScalarGridSpec(
            num_scalar_prefetch=1,
            grid=(n // TM,),
            in_specs=[pl.BlockSpec((TM, d), lambda i, dst: (i, 0)),
                      pl.BlockSpec((TM, LANES), lambda i, dst: (i, 0)),
                      pl.BlockSpec(mods.shape, lambda i, dst: (0, 0)),
                      pl.BlockSpec((1, d), lambda i, dst: (0, 0)),
                      pl.BlockSpec(memory_space=pl.ANY)],
            out_specs=pl.BlockSpec((TM, d), lambda i, dst: (i, 0)),
            scratch_shapes=[pltpu.VMEM((2, 2, TM * SUBLANES, LANES), F32),
                            pltpu.SemaphoreType.DMA((2,))]),
        compiler_params=_cparams("arbitrary"),
        name="moe_combine",
    )(dest, s, wts, mods, final_g.reshape(1, d), ys)


def _moe_layer(s, mods, norm_g, rg_w, rg_b, re_w, re_b, w_gate, w_up, w_down, layer, ctx_tiles,
               final_g, final_norm):
    n, d = s.shape
    h, e1, e2, r1, r2, wts, cnt = _router(s, mods, norm_g, rg_w, rg_b, re_w, re_b, ctx_tiles)
    counts = cnt[MOE_GROUPS:MOE_GROUPS + MOE_EXPERTS, 0].astype(jnp.int32)
    nb = (2 * n + MOE_EXPERTS * (MOE_ROWS - 1)) // MOE_ROWS + 1
    d1, d2, blk = _finalize(counts, e1, e2, r1, r2, nb)
    dest = jnp.concatenate([d1, d2], axis=1).reshape(2 * n)
    n_used = blk[2, :1]
    pad_end = blk[1, :MOE_EXPERTS]
    xs = _dispatch(dest, pad_end, n_used, h, nb * MOE_ROWS)
    ys = _experts(blk[0, :nb], pad_end, n_used, xs, w_gate, w_up, w_down, layer)
    return _combine(dest, s, wts, mods, final_g, ys, ctx_tiles, final_norm)


def _conv_tile(x, prev_ref, next_ref, has_prev, has_next, w_ref, b_ref):
    rows = x.shape[0]
    S = SUBLANES
    ridx = lax.broadcasted_iota(jnp.int32, (S, x.shape[1]), 0)
    pm = jnp.where(has_prev, 1.0, 0.0)
    nm = jnp.where(has_next, 1.0, 0.0)
    p2 = prev_ref[S - 2:S - 1, :] * pm
    p1 = prev_ref[S - 1:S, :] * pm
    n1 = next_ref[0:1, :] * nm

    def fix_head(rolled, head):
        return jnp.concatenate([head(rolled[:S]), rolled[S:]], axis=0)

    xm1 = fix_head(pltpu.roll(x, 1, axis=0), lambda g: jnp.where(ridx == 0, p1, g))
    xm2 = fix_head(pltpu.roll(x, 2, axis=0),
                   lambda g: jnp.where(ridx == 0, p2, jnp.where(ridx == 1, p1, g)))
    xp1 = pltpu.roll(x, rows - 1, axis=0)
    xp1 = jnp.concatenate([xp1[:rows - S], jnp.where(ridx == S - 1, n1, xp1[rows - S:])], axis=0)
    return (xm2 * w_ref[0:1, :] + xm1 * w_ref[1:2, :] + x * w_ref[2:3, :]
            + xp1 * w_ref[3:4, :] + b_ref[...])


def _ml_proj_kernel(s_ref, mod_ref, g_ref, w_ref, wg_ref, gb_ref, qk_ref, v_ref, o_ref, gt_ref):
    i = pl.program_id(0)
    shift, scale, _ = _mod_rows(mod_ref, i, 1, 0)
    h = _rms(s_ref[...], g_ref[...]) * (1.0 + scale) + shift
    z = _bdot(h, w_ref[...])
    nqk = qk_ref.shape[1]
    nv = v_ref.shape[1]
    qk_ref[...] = z[:, :nqk]
    v_ref[...] = z[:, nqk:nqk + nv].astype(BF16)
    o_ref[...] = z[:, nqk + nv:].astype(BF16)
    pre = _dot3(h, wg_ref[...], ((1,), (0,))) + gb_ref[...]
    lane = lax.broadcasted_iota(jnp.int32, pre.shape, 1)
    is_forget = ((lane >> 2) & 1) == 1
    gt_ref[...] = jnp.where(is_forget, -_softplus(-pre), pre)


def _ml_proj(s, mods, norm_g, w_in, gate_b):
    n, d = s.shape
    nqk = 2 * ML_HEADS * ML_DK
    nv = ML_HEADS * ML_DV
    n_main = nqk + 2 * nv
    n_gate = w_in.shape[1] - n_main
    w_main = w_in[:, :n_main].astype(BF16)
    w_gate = jnp.concatenate([w_in[:, n_main:], jnp.zeros((d, LANES - n_gate), F32)], axis=1)
    gb = jnp.concatenate([gate_b.reshape(n_gate), jnp.zeros((LANES - n_gate,), F32)]).reshape(1, LANES)
    tile = lambda w: pl.BlockSpec((TM, w), lambda i: (i, 0))
    return pl.pallas_call(
        _ml_proj_kernel,
        out_shape=(jax.ShapeDtypeStruct((n, nqk), F32), jax.ShapeDtypeStruct((n, nv), BF16),
                   jax.ShapeDtypeStruct((n, nv), BF16), jax.ShapeDtypeStruct((n, LANES), F32)),
        grid=(n // TM,),
        in_specs=[tile(d), _full(mods.shape), _full((1, d)), _full(w_main.shape),
                  _full((d, LANES)), _full((1, LANES))],
        out_specs=(tile(nqk), tile(nv), tile(nv), tile(LANES)),
        compiler_params=_cparams("arbitrary"),
        name="mlstm_proj",
    )(s, mods, norm_g.reshape(1, d), w_main, w_gate, gb)


def _ml_chunk_index(j, n_chunks, ctx_chunks, reverse):
    if not reverse:
        return j
    return jnp.where(j < ctx_chunks, ctx_chunks - 1 - j, n_chunks - 1 + ctx_chunks - j)


def _ml_rec_kernel(qk_ref, qkp_ref, qkn_ref, v_ref, gt_ref, gtt_ref, cw_ref, cb_ref, o_ref,
                   c_scr, n_scr, m_scr, *, reverse, n_chunks, ctx_chunks):
    j = pl.program_id(0)
    c = _ml_chunk_index(j, n_chunks, ctx_chunks, reverse)

    @pl.when(j == 0)
    def _():
        c_scr[...] = jnp.zeros_like(c_scr)
        n_scr[...] = jnp.zeros_like(n_scr)
        m_scr[...] = jnp.zeros_like(m_scr)

    has_prev = (c != 0) & (c != ctx_chunks)
    has_next = (c != ctx_chunks - 1) & (c != n_chunks - 1)
    qk = _silu(_conv_tile(qk_ref[...], qkp_ref, qkn_ref, has_prev, has_next, cw_ref, cb_ref))
    L = ML_CHUNK
    ri = lax.broadcasted_iota(jnp.int32, (L, L), 0)
    ci = lax.broadcasted_iota(jnp.int32, (L, L), 1)
    past = (ci >= ri) if reverse else (ci <= ri)
    pastf = past.astype(F32)
    gt = gt_ref[...]
    gtt = gtt_ref[...]
    b_col = jnp.dot(pastf, gt, precision=HI, preferred_element_type=F32)
    b_row = jnp.dot(gtt, pastf.T, precision=HI, preferred_element_type=F32)
    last = 0 if reverse else L - 1
    dbase = 8 if reverse else 0
    nq = ML_HEADS * ML_DK
    for hd in range(ML_HEADS):
        cl = dbase + hd
        cf = dbase + 4 + hd
        q = qk[:, hd * ML_DK:(hd + 1) * ML_DK] * (ML_DK ** -0.5)
        k = qk[:, nq + hd * ML_DK:nq + (hd + 1) * ML_DK]
        v = v_ref[:, hd * ML_DV:(hd + 1) * ML_DV]
        li_c = gt[:, cl:cl + 1]
        li_r = gtt[cl:cl + 1, :]
        b_c = b_col[:, cf:cf + 1]
        b_r = b_row[cf:cf + 1, :]
        g = b_r[:, last:last + 1]
        m0 = m_scr[hd:hd + 1, 0:1]
        c0 = c_scr[hd]
        n0 = n_scr[hd:hd + 1, :]
        a_c = g - b_c + li_c
        a_r = g - b_r + li_r
        m_loc = jnp.max(a_r, axis=-1, keepdims=True)
        inter = b_c + m0
        dlog = jnp.where(past, b_c - b_r + li_r, -jnp.inf)
        m = jnp.maximum(inter, jnp.max(dlog, axis=-1, keepdims=True))
        qb = q.astype(BF16)
        sc = lax.dot_general(qb, k.astype(BF16), (((1,), (1,)), ((), ())),
                             preferred_element_type=F32) * jnp.exp(dlog - m)
        w_inter = jnp.exp(inter - m)
        num = (jnp.dot(sc.astype(BF16), v, preferred_element_type=F32)
               + w_inter * jnp.dot(qb, c0.astype(BF16), preferred_element_type=F32))
        den = (jnp.sum(sc, axis=-1, keepdims=True)
               + w_inter * jnp.sum(q * n0, axis=-1, keepdims=True))
        o_ref[:, hd * ML_DV:(hd + 1) * ML_DV] = (
            num / jnp.maximum(jnp.abs(den), jnp.exp(-m))).astype(BF16)
        m_new = jnp.maximum(g + m0, m_loc)
        dec = jnp.exp(g + m0 - m_new)
        scl = jnp.exp(m_loc - m_new)
        kw = k * jnp.exp(a_c - m_loc)
        c_scr[hd] = dec * c0 + scl * jnp.dot(kw.T.astype(BF16), v, preferred_element_type=F32)
        n_scr[hd:hd + 1, :] = dec * n0 + scl * jnp.sum(kw, axis=0, keepdims=True)
        m_scr[hd:hd + 1, :] = jnp.broadcast_to(m_new, (1, LANES))


def _ml_rec(qk, v, gt, gtt, conv_w, conv_b, reverse):
    n, nqk = qk.shape
    nv = v.shape[1]
    L = ML_CHUNK
    nc = n // L
    cc = TM // L
    hb = L // SUBLANES
    idx = lambda j: _ml_chunk_index(j, nc, cc, reverse)
    last8 = n // SUBLANES - 1
    return pl.pallas_call(
        functools.partial(_ml_rec_kernel, reverse=reverse, n_chunks=nc, ctx_chunks=cc),
        out_shape=jax.ShapeDtypeStruct((n, nv), BF16),
        grid=(nc,),
        in_specs=[pl.BlockSpec((L, nqk), lambda j: (idx(j), 0)),
                  pl.BlockSpec((SUBLANES, nqk), lambda j: (jnp.maximum(idx(j) * hb - 1, 0), 0)),
                  pl.BlockSpec((SUBLANES, nqk), lambda j: (jnp.minimum((idx(j) + 1) * hb, last8), 0)),
                  pl.BlockSpec((L, nv), lambda j: (idx(j), 0)),
                  pl.BlockSpec((L, LANES), lambda j: (idx(j), 0)),
                  pl.BlockSpec((2 * SUBLANES, L), lambda j: (0, idx(j))),
                  _full((4, nqk)), _full((1, nqk))],
        out_specs=pl.BlockSpec((L, nv), lambda j: (idx(j), 0)),
        scratch_shapes=[pltpu.VMEM((ML_HEADS, ML_DK, ML_DV), F32),
                        pltpu.VMEM((SUBLANES, ML_DK), F32),
                        pltpu.VMEM((SUBLANES, LANES), F32)],
        compiler_params=_cparams("arbitrary"),
        name="mlstm_rev" if reverse else "mlstm_fwd",
    )(qk, qk, qk, v, gt, gtt, conv_w, conv_b.reshape(1, nqk))


def _ml_out_kernel(hf_ref, hr_ref, o_ref, s_ref, mod_ref, ng_ref, w_ref, out_ref, p_scr):
    i = pl.program_id(0)
    gate = _mod_rows(mod_ref, i, 1, 0)[2]
    hs = hf_ref[...].astype(F32) + hr_ref[...].astype(F32)
    sig = _sigmoid(o_ref[...].astype(F32))
    ng = ng_ref[...]
    for hd in range(ML_HEADS):
        cs = slice(hd * ML_DV, (hd + 1) * ML_DV)
        seg = hs[:, cs]
        hn = seg * lax.rsqrt(jnp.mean(seg * seg, axis=-1, keepdims=True) + EPS) * ng[:, cs]
        p_scr[:, cs] = (hn * sig[:, cs]).astype(BF16)
    y = jnp.dot(p_scr[...], w_ref[...], preferred_element_type=F32)
    out_ref[...] = s_ref[...] + gate * y


def _ml_out(hf, hr, o, s, mods, norm_g, w_out):
    n, d = s.shape
    nv = hf.shape[1]
    tile = lambda w: pl.BlockSpec((TM, w), lambda i: (i, 0))
    return pl.pallas_call(
        _ml_out_kernel,
        out_shape=jax.ShapeDtypeStruct((n, d), F32),
        grid=(n // TM,),
        in_specs=[tile(nv), tile(nv), tile(nv), tile(d), _full(mods.shape), _full((1, nv)),
                  _full(w_out.shape)],
        out_specs=tile(d),
        scratch_shapes=[pltpu.VMEM((TM, nv), BF16)],
        compiler_params=_cparams("arbitrary"),
        name="mlstm_out",
    )(hf, hr, o, s, mods, norm_g.reshape(1, nv), w_out.astype(BF16))


def _mlstm_layer(s, mods, norm_g, w_in, conv_w, conv_b, gate_b, ml_norm_g, w_out):
    qk, v, o, gt = _ml_proj(s, mods, norm_g, w_in, gate_b)
    gtt = gt[:, :2 * SUBLANES].T
    hf = _ml_rec(qk, v, gt, gtt, conv_w, conv_b, False)
    hr = _ml_rec(qk, v, gt, gtt, conv_w, conv_b, True)
    return _ml_out(hf, hr, o, s, mods, ml_norm_g, w_out)


def _lru_proj_kernel(s_ref, mod_ref, g_ref, w_ref, gl_ref, xr_ref):
    i = pl.program_id(0)
    shift, scale, _ = _mod_rows(mod_ref, i, 1, 0)
    h = _rms(s_ref[...], g_ref[...]) * (1.0 + scale) + shift
    z = _bdot(h, w_ref[...])
    w = gl_ref.shape[1]
    gl_ref[...] = _gelu(z[:, :w]).astype(BF16)
    xr_ref[...] = z[:, w:]


def _lru_proj(s, mods, norm_g, w_in):
    n, d = s.shape
    w = w_in.shape[1] // 2
    tile = lambda c: pl.BlockSpec((TM, c), lambda i: (i, 0))
    return pl.pallas_call(
        _lru_proj_kernel,
        out_shape=(jax.ShapeDtypeStruct((n, w), BF16), jax.ShapeDtypeStruct((n, w), F32)),
        grid=(n // TM,),
        in_specs=[tile(d), _full(mods.shape), _full((1, d)), _full(w_in.shape)],
        out_specs=(tile(w), tile(w)),
        compiler_params=_cparams("arbitrary"),
        name="rglru_proj",
    )(s, mods, norm_g.reshape(1, d), w_in.astype(BF16))


def _lru_tile_index(j, n_tiles, reverse):
    if not reverse:
        return j
    return jnp.where(j == 0, 0, n_tiles - j)


def _lru_scan_kernel(x_ref, xp_ref, xn_ref, cw_ref, cb_ref, wg_ref, ba_ref, bx_ref, lam_ref,
                     o_ref, a_scr, u_scr, carry, *, reverse, n_tiles):
    j = pl.program_id(0)
    t = _lru_tile_index(j, n_tiles, reverse)

    @pl.when(j == 0)
    def _():
        carry[...] = jnp.zeros_like(carry)

    has_prev = t > 1
    has_next = (t != 0) & (t != n_tiles - 1)
    xr = _conv_tile(x_ref[...], xp_ref, xn_ref, has_prev, has_next, cw_ref, cb_ref)
    sp = _softplus(-lam_ref[...])
    B = LRU_BLOCK
    for hd in range(LRU_HEADS):
        cs = slice(hd * B, (hd + 1) * B)
        xh = xr[:, cs]
        y = jnp.dot(xh.astype(BF16), wg_ref[hd], preferred_element_type=F32)
        r = _sigmoid(y[:, :B] + ba_ref[:, cs])
        ig = _sigmoid(y[:, B:] + bx_ref[:, cs])
        log_a = -LRU_C * r * sp[:, cs]
        a = jnp.exp(log_a)
        a_scr[:, cs] = a
        v = 1.0 - a * a
        u_scr[:, cs] = jnp.where(v > 0.0, v * lax.rsqrt(v), 0.0) * (ig * xh)

    S = SUBLANES
    w = a_scr.shape[1]
    sidx = lax.broadcasted_iota(jnp.int32, (S, w), 0)

    def group(gi, c):
        g = (TM // S - 1 - gi) if reverse else gi
        r0 = pl.multiple_of(g * S, S)
        a = a_scr[pl.ds(r0, S), :]
        u = u_scr[pl.ds(r0, S), :]
        for sft in (1, 2, 4):
            if reverse:
                ok = sidx < S - sft
                a_e = pltpu.roll(a, S - sft, axis=0)
                u_e = pltpu.roll(u, S - sft, axis=0)
            else:
                ok = sidx >= sft
                a_e = pltpu.roll(a, sft, axis=0)
                u_e = pltpu.roll(u, sft, axis=0)
            u = jnp.where(ok, a * u_e + u, u)
            a = jnp.where(ok, a * a_e, a)
        hcur = a * carry[...] + u
        u_scr[pl.ds(r0, S), :] = hcur
        edge = 0 if reverse else S - 1
        carry[...] = jnp.broadcast_to(hcur[edge:edge + 1, :], (S, w))
        return c

    lax.fori_loop(0, TM // S, group, 0)
    o_ref[...] = u_scr[...].astype(BF16)


def _lru_scan(xraw, conv_w, conv_b, w_a, b_a, w_x, b_x, lam, reverse):
    n, w = xraw.shape
    nt = n // TM
    hb = TM // SUBLANES
    idx = lambda j: _lru_tile_index(j, nt, reverse)
    last8 = n // SUBLANES - 1
    wg = jnp.concatenate([w_a, w_x], axis=-1).astype(BF16)
    return pl.pallas_call(
        functools.partial(_lru_scan_kernel, reverse=reverse, n_tiles=nt),
        out_shape=jax.ShapeDtypeStruct((n, w), BF16),
        grid=(nt,),
        in_specs=[pl.BlockSpec((TM, w), lambda j: (idx(j), 0)),
                  pl.BlockSpec((SUBLANES, w), lambda j: (jnp.maximum(idx(j) * hb - 1, 0), 0)),
                  pl.BlockSpec((SUBLANES, w), lambda j: (jnp.minimum((idx(j) + 1) * hb, last8), 0)),
                  _full((4, w)), _full((1, w)), _full(wg.shape), _full((1, w)), _full((1, w)),
                  _full((1, w))],
        out_specs=pl.BlockSpec((TM, w), lambda j: (idx(j), 0)),
        scratch_shapes=[pltpu.VMEM((TM, w), F32), pltpu.VMEM((TM, w), F32),
                        pltpu.VMEM((SUBLANES, w), F32)],
        compiler_params=_cparams("arbitrary"),
        name="rglru_rev" if reverse else "rglru_fwd",
    )(xraw, xraw, xraw, conv_w, conv_b.reshape(1, w), wg, b_a.reshape(1, w), b_x.reshape(1, w),
      lam.reshape(1, w))


def _lru_out_kernel(gl_ref, hf_ref, hr_ref, s_ref, mod_ref, w_ref, out_ref):
    gate = _mod_rows(mod_ref, 1, 0, 0)[2]
    p = gl_ref[...].astype(F32) * (hf_ref[...].astype(F32) + hr_ref[...].astype(F32))
    out_ref[...] = s_ref[...] + gate * _bdot(p, w_ref[...])


def _lru_out(gl, hf, hr, s, mods, w_out):
    n, d = s.shape
    w = gl.shape[1]
    lat = lambda c: pl.BlockSpec((TM, c), lambda i: (i + 1, 0))
    return pl.pallas_call(
        _lru_out_kernel,
        out_shape=jax.ShapeDtypeStruct((n - TM, d), F32),
        grid=(n // TM - 1,),
        in_specs=[lat(w), lat(w), lat(w), lat(d), _full(mods.shape), _full(w_out.shape)],
        out_specs=pl.BlockSpec((TM, d), lambda i: (i, 0)),
        compiler_params=_cparams("arbitrary"),
        name="rglru_out",
    )(gl, hf, hr, s, mods, w_out.astype(BF16))


def _rglru_layer(s, mods, norm_g, w_in, conv_w, conv_b, w_a, b_a, w_x, b_x, lam, w_out):
    gl, xraw = _lru_proj(s, mods, norm_g, w_in)
    hf = _lru_scan(xraw, conv_w, conv_b, w_a[0], b_a[0], w_x[0], b_x[0], lam[0], False)
    hr = _lru_scan(xraw, conv_w, conv_b, w_a[1], b_a[1], w_x[1], b_x[1], lam[1], True)
    return _lru_out(gl, hf, hr, s, mods, w_out)


def _fn_proj_kernel(s_ref, mod_ref, g_ref, wt_ref, cs_ref, yr_ref, yi_ref, ar_scr, ai_scr):
    shift, scale, _ = _mod_rows(mod_ref, 1, 0, 0)
    nm = wt_ref.shape[0]
    gw = nm // FN_GROUPS
    per = TM // FFT_N2
    nj = FN_TB // FFT_N2
    csb = cs_ref[...].astype(BF16)

    def sub(tc, c):
        r0 = pl.multiple_of(tc * TM, TM)
        h = _rms(s_ref[pl.ds(r0, TM), :], g_ref[...]) * (1.0 + scale) + shift
        zt = lax.dot_general(wt_ref[...], h.astype(BF16), (((1,), (1,)), ((), ())),
                             preferred_element_type=F32).astype(BF16)
        for g in range(FN_GROUPS):
            y = jnp.dot(csb, zt[g * gw:(g + 1) * gw, :], preferred_element_type=F32)
            for q in range(per):
                row0 = pl.multiple_of((tc * per + q) * _slab_pitch(nm) + g * gw, SUBLANES)
                ar_scr[pl.ds(row0, gw), :] = y[:gw, q * FFT_N2:(q + 1) * FFT_N2]
                ai_scr[pl.ds(row0, gw), :] = y[gw:, q * FFT_N2:(q + 1) * FFT_N2]
        return c

    lax.fori_loop(0, FN_TB // TM, sub, 0)

    def relayout(m, c):
        yr_ref[m] = ar_scr[pl.ds(m, nj, stride=_slab_pitch(nm)), :]
        yi_ref[m] = ai_scr[pl.ds(m, nj, stride=_slab_pitch(nm)), :]
        return c

    lax.fori_loop(0, nm, relayout, 0, unroll=8)


def _slab_pitch(rows):
    return rows + SUBLANES


def _dft_cos_sin(n, scale):
    k = np.arange(n, dtype=np.int64)
    ang = 2.0 * np.pi * ((k[:, None] * k[None, :]) % n).astype(np.float64) / n
    return np.cos(ang) * scale, np.sin(ang) * scale


def _fn_proj(s, mods, norm_g, w_in):
    t, d = s.shape
    nm = w_in.shape[1]
    gw = nm // FN_GROUPS
    n1 = t // FFT_N2
    nj = FN_TB // FFT_N2
    c, sn = _dft_cos_sin(gw, gw ** -0.5)
    cs = jnp.asarray(np.concatenate([c, -sn], axis=0), F32)
    yspec = pl.BlockSpec((nm, nj, FFT_N2), lambda i: (0, i, 0))
    yshape = jax.ShapeDtypeStruct((nm, n1, FFT_N2), F32)
    return pl.pallas_call(
        _fn_proj_kernel,
        out_shape=(yshape, yshape),
        grid=(t // FN_TB,),
        in_specs=[pl.BlockSpec((FN_TB, d), lambda i: (i, 0)), _full(mods.shape), _full((1, d)),
                  _full((nm, d)), _full(cs.shape)],
        out_specs=(yspec, yspec),
        scratch_shapes=[pltpu.VMEM((nj * _slab_pitch(nm), FFT_N2), F32),
                        pltpu.VMEM((nj * _slab_pitch(nm), FFT_N2), F32)],
        compiler_params=_cparams("arbitrary"),
        name="fourier_proj",
    )(s, mods, norm_g.reshape(1, d), w_in.T.astype(BF16), cs)


def _fn_fft_kernel(yr_ref, yi_ref, m_ref, tc_ref, ts_ref, d_ref, o_ref):
    n1 = yr_ref.shape[1]
    n2 = FFT_N2
    xr = jnp.concatenate([yr_ref[m].astype(BF16) for m in range(FN_CB)], axis=1)
    xi = jnp.concatenate([yi_ref[m].astype(BF16) for m in range(FN_CB)], axis=1)
    a = jnp.dot(m_ref[...].astype(BF16), jnp.concatenate([xr, xi], axis=0),
                preferred_element_type=F32)
    ar = a[:n1]
    ai = a[n1:]
    tc = jnp.concatenate([tc_ref[...]] * FN_CB, axis=1)
    ts = jnp.concatenate([ts_ref[...]] * FN_CB, axis=1)
    br = ar * tc + ai * ts
    bi = ai * tc - ar * ts
    bst = jnp.concatenate(
        [jnp.concatenate([br[:, m * n2:(m + 1) * n2], bi[:, m * n2:(m + 1) * n2]], axis=1)
         for m in range(FN_CB)], axis=0).astype(BF16)
    res = lax.dot_general(d_ref[...].astype(BF16), bst, (((1,), (1,)), ((), ())),
                          preferred_element_type=F32)
    for m in range(FN_CB):
        o_ref[m] = res[:, m * n1:(m + 1) * n1]


def _fn_fft(yr, yi):
    nm, n1, n2 = yr.shape
    t = n1 * n2
    c, sn = _dft_cos_sin(n1, n1 ** -0.5)
    m = jnp.asarray(np.block([[c, sn], [-sn, c]]), F32)
    k1 = np.arange(n1, dtype=np.int64)[:, None]
    t2 = np.arange(n2, dtype=np.int64)[None, :]
    ang = 2.0 * np.pi * ((k1 * t2) % t).astype(np.float64) / t
    tc = jnp.asarray(np.cos(ang), F32)
    ts = jnp.asarray(np.sin(ang), F32)
    c2, s2 = _dft_cos_sin(n2, n2 ** -0.5)
    dm = jnp.asarray(np.concatenate([c2, s2], axis=1), F32)
    yspec = pl.BlockSpec((FN_CB, n1, n2), lambda i: (i, 0, 0))
    return pl.pallas_call(
        _fn_fft_kernel,
        out_shape=jax.ShapeDtypeStruct((nm, n2, n1), F32),
        grid=(nm // FN_CB,),
        in_specs=[yspec, yspec, _full(m.shape), _full(tc.shape), _full(ts.shape), _full(dm.shape)],
        out_specs=pl.BlockSpec((FN_CB, n2, n1), lambda i: (i, 0, 0)),
        compiler_params=_cparams("arbitrary"),
        name="fourier_fft",
    )(yr, yi, m, tc, ts, dm)


def _fn_out_kernel(ft_ref, w_ref, s_ref, mod_ref, o_ref, a_scr):
    gate = _mod_rows(mod_ref, 1, 0, 0)[2]
    nm, nj, n1 = ft_ref.shape

    def relayout(m, c):
        a_scr[pl.ds(m, nj, stride=_slab_pitch(nm)), :] = ft_ref[m]
        return c

    lax.fori_loop(0, nm, relayout, 0, unroll=8)
    for j in range(nj):
        p0 = j * _slab_pitch(nm)
        slab = a_scr[p0:p0 + nm, :].astype(BF16)
        y = lax.dot_general(slab, w_ref[...], (((0,), (0,)), ((), ())), preferred_element_type=F32)
        rows = slice(j * n1, (j + 1) * n1)
        o_ref[rows, :] = s_ref[rows, :] + gate * y


def _fn_out(ft, s, mods, w_out):
    t, d = s.shape
    nm, n2, n1 = ft.shape
    nj = FN_TB // n1
    tok = pl.BlockSpec((FN_TB, d), lambda i: (i, 0))
    return pl.pallas_call(
        _fn_out_kernel,
        out_shape=jax.ShapeDtypeStruct((t, d), F32),
        grid=(t // FN_TB,),
        in_specs=[pl.BlockSpec((nm, nj, n1), lambda i: (0, i, 0)), _full(w_out.shape), tok,
                  _full(mods.shape)],
        out_specs=tok,
        scratch_shapes=[pltpu.VMEM((nj * _slab_pitch(nm), n1), F32)],
        compiler_params=_cparams("arbitrary"),
        name="fourier_out",
    )(ft, w_out.astype(BF16), s, mods)


def _fourier_layer(s, mods, norm_g, w_in, w_out):
    yr, yi = _fn_proj(s, mods, norm_g, w_in)
    return _fn_out(_fn_fft(yr, yi), s, mods, w_out)


def kernel(x, c, ctx, c_ctx, ada_w, ada_b, norm_mix_g, norm_ffn_g, final_norm_g, router_group_w, router_group_b, router_expert_w, router_expert_b, expert_w_gate, expert_w_up, expert_w_down, cm_w_in, cm_v_norm_g, cm_w_s, cm_b_s, cm_w_out, ml_w_in, ml_conv_w, ml_conv_b, ml_gate_b, ml_norm_g, ml_w_out, lru_w_in, lru_conv_w, lru_conv_b, lru_w_a, lru_b_a, lru_w_x, lru_b_x, lru_lambda, lru_w_out, fn_w_in, fn_w_out):
    bsz, seq, d = x.shape
    assert bsz == 1 and ada_w.shape[0] == 4 and ctx.shape[1] == TM
    c_rows = jnp.concatenate([c_ctx[None, :], c, jnp.zeros((SUBLANES - 2, d), F32)], axis=0)
    mods = _ada_table(c_rows, ada_w, ada_b)

    def moe(s, i, ctx_tiles, final_norm=False):
        return _moe_layer(s, mods[i], norm_ffn_g[i], router_group_w[i], router_group_b[i],
                          router_expert_w[i], router_expert_b[i], expert_w_gate, expert_w_up,
                          expert_w_down, i, ctx_tiles, final_norm_g, final_norm)

    s = _chunk_mlp_layer(x[0], ctx[0], mods[0], norm_mix_g[0], cm_w_in[0], cm_v_norm_g[0],
                         cm_w_s[0], cm_b_s[0], cm_w_out[0])
    s = moe(s, 0, 1)
    s = _mlstm_layer(s, mods[1], norm_mix_g[1], ml_w_in[0], ml_conv_w[0], ml_conv_b[0],
                     ml_gate_b[0], ml_norm_g[0], ml_w_out[0])
    s = moe(s, 1, 1)
    s = _rglru_layer(s, mods[2], norm_mix_g[2], lru_w_in[0], lru_conv_w[0], lru_conv_b[0],
                     lru_w_a[0], lru_b_a[0], lru_w_x[0], lru_b_x[0], lru_lambda[0], lru_w_out[0])
    s = moe(s, 2, 0)
    s = _fourier_layer(s, mods[3], norm_mix_g[3], fn_w_in[0], fn_w_out[0])
    s = moe(s, 3, 0, final_norm=True)
    return s[None]
```

```python
import functools
import math

import jax
import jax.numpy as jnp
import numpy as np
from jax import lax
from jax.experimental import pallas as pl
from jax.experimental.pallas import tpu as pltpu

F32 = jnp.float32
BF16 = jnp.bfloat16

EPS = 1e-6
POS_BASE = 10000.0
GRID_W = 64
N_MOD = 6
TM = 256
LANES = 128
SUBLANES = 8
VMEM_LIMIT = 56 * 1024 * 1024

CM_CHUNK = 128
CM_GROUPS = 4
ML_HEADS = 4
ML_DK = 128
ML_DV = 256
ML_CHUNK = 128
LRU_HEADS = 10
LRU_BLOCK = 128
LRU_C = 8.0
FN_GROUPS = 4
FFT_N2 = 128
MOE_GROUPS = 4
MOE_EPG = 8
MOE_EXPERTS = MOE_GROUPS * MOE_EPG
MOE_ROWS_LOG2 = 8
MOE_ROWS = 1 << MOE_ROWS_LOG2
ROUTE_ROWS = 40
FN_TB = 1024
FN_CB = 16
CONV_LEFT = 2

HI = lax.Precision.HIGHEST


def _cparams(*sem):
    return pltpu.CompilerParams(dimension_semantics=sem, vmem_limit_bytes=VMEM_LIMIT)


def _full(shape):
    nd = len(shape)
    return pl.BlockSpec(shape, lambda *_: (0,) * nd)


def _rms(x, g):
    return x * lax.rsqrt(jnp.mean(x * x, axis=-1, keepdims=True) + EPS) * g


def _gelu(x):
    c = math.sqrt(2.0 / math.pi)
    return 0.5 * x * (1.0 + jnp.tanh(c * (x + 0.044715 * (x * x * x))))


def _sigmoid(x):
    return 0.5 * jnp.tanh(0.5 * x) + 0.5


def _silu(x):
    return x * _sigmoid(x)


def _softplus(x):
    return jnp.maximum(x, 0.0) + jnp.log(1.0 + jnp.exp(-jnp.abs(x)))


def _mod_rows(mod_ref, tile, ctx_tiles, first):
    row = jnp.where(tile < ctx_tiles, 0, 1)
    m = mod_ref[pl.ds(row, 1), :]
    d = m.shape[1] // N_MOD
    return tuple(m[:, (first + j) * d:(first + j + 1) * d] for j in range(3))


def _bdot(a, b):
    return jnp.dot(a.astype(BF16), b.astype(BF16), preferred_element_type=F32)


def _split_bf16(x):
    hi = x.astype(BF16)
    return hi, (x - hi.astype(F32)).astype(BF16)


def _dot3(a, b, dims):
    a_hi, a_lo = _split_bf16(a)
    b_hi, b_lo = _split_bf16(b)
    dg = functools.partial(lax.dot_general, dimension_numbers=(dims, ((), ())),
                           preferred_element_type=F32)
    return dg(a_hi, b_hi) + dg(a_hi, b_lo) + dg(a_lo, b_hi)


def _ada_kernel(c_ref, w_ref, b_ref, o_ref):
    c = c_ref[...]
    o_ref[...] = _dot3(_silu(c), w_ref[...], ((1,), (0,))) + b_ref[...]


def _ada_table(c_rows, ada_w, ada_b):
    depth, d, n = ada_w.shape
    tn = 2048
    return pl.pallas_call(
        _ada_kernel,
        out_shape=jax.ShapeDtypeStruct((depth, SUBLANES, n), F32),
        grid=(depth, n // tn),
        in_specs=[_full((SUBLANES, d)),
                  pl.BlockSpec((None, d, tn), lambda i, j: (i, 0, j)),
                  pl.BlockSpec((None, 1, tn), lambda i, j: (i, 0, j))],
        out_specs=pl.BlockSpec((None, SUBLANES, tn), lambda i, j: (i, 0, j)),
        compiler_params=_cparams("arbitrary", "arbitrary"),
        name="ada_table",
    )(c_rows, ada_w, ada_b.reshape(depth, 1, n))


def _pos_tables(seq, d):
    q = d // 4
    freq = jnp.exp(-math.log(POS_BASE) * jnp.arange(q, dtype=F32) / q)
    ar = jnp.arange(seq // GRID_W, dtype=F32)[:, None] * freq
    ac = jnp.arange(GRID_W, dtype=F32)[:, None] * freq
    return (jnp.concatenate([jnp.sin(ar), jnp.cos(ar)], axis=-1),
            jnp.concatenate([jnp.sin(ac), jnp.cos(ac)], axis=-1))


def _cm_kernel(x_ref, ctx_ref, rt_ref, ct_ref, mod_ref, g_ref, win_ref, vg_ref, ws_ref, bs_ref,
               wout_ref, o_ref, p_scr, x_scr):
    i = pl.program_id(0)

    @pl.when(i == 0)
    def _():
        x_scr[...] = ctx_ref[...]

    @pl.when(i > 0)
    def _():
        rows_per_tile = TM // GRID_W
        q2 = rt_ref.shape[1]
        r0 = (i - 1) * rows_per_tile
        rt = jnp.concatenate(
            [jnp.broadcast_to(rt_ref[pl.ds(r0 + j, 1), :], (GRID_W, q2))
             for j in range(rows_per_tile)], axis=0)
        ct = jnp.concatenate([ct_ref[...]] * rows_per_tile, axis=0)
        x_scr[...] = x_ref[...] + jnp.concatenate([rt, ct], axis=1)

    shift, scale, gate = _mod_rows(mod_ref, i, 1, 0)
    x = x_scr[...]
    h = _rms(x, g_ref[...]) * (1.0 + scale) + shift
    z = _gelu(_bdot(h, win_ref[...]))
    w = z.shape[1] // 2
    u = z[:, :w]
    v = _rms(z[:, w:], vg_ref[...]).astype(BF16)
    gw = w // CM_GROUPS
    for c in range(TM // CM_CHUNK):
        r = slice(c * CM_CHUNK, (c + 1) * CM_CHUNK)
        for g in range(CM_GROUPS):
            cs = slice(g * gw, (g + 1) * gw)
            s = jnp.dot(ws_ref[g], v[r, cs], preferred_element_type=F32) + bs_ref[:, g:g + 1]
            p_scr[r, cs] = (u[r, cs] * s).astype(BF16)
    y = jnp.dot(p_scr[...], wout_ref[...], preferred_element_type=F32)
    o_ref[...] = x + gate * y


def _chunk_mlp_layer(x2, ctx2, mods, norm_g, w_in, v_g, w_s, b_s, w_out):
    seq, d = x2.shape
    n_ctx = ctx2.shape[0]
    assert n_ctx == TM and seq % TM == 0 and TM % GRID_W == 0
    n = n_ctx + seq
    w = w_out.shape[0]
    rt, ct = _pos_tables(seq, d)
    return pl.pallas_call(
        _cm_kernel,
        out_shape=jax.ShapeDtypeStruct((n, d), F32),
        grid=(n // TM,),
        in_specs=[pl.BlockSpec((TM, d), lambda i: (jnp.maximum(i - 1, 0), 0)),
                  _full((TM, d)), _full(rt.shape), _full(ct.shape),
                  _full(mods.shape), _full((1, d)), _full(w_in.shape), _full((1, w)),
                  _full(w_s.shape), _full((CM_CHUNK, CM_GROUPS)), _full(w_out.shape)],
        out_specs=pl.BlockSpec((TM, d), lambda i: (i, 0)),
        scratch_shapes=[pltpu.VMEM((TM, w), BF16), pltpu.VMEM((TM, d), F32)],
        compiler_params=_cparams("arbitrary"),
        name="chunk_mlp",
    )(x2, ctx2, rt, ct, mods, norm_g.reshape(1, d), w_in.astype(BF16), v_g.reshape(1, w),
      w_s.astype(BF16), b_s.T, w_out.astype(BF16))


def _store_token_tiles(ref, x):
    rows, d = x.shape
    for j in range(d // LANES):
        ref[pl.ds(j, rows, stride=d // LANES), :] = x[:, j * LANES:(j + 1) * LANES]


def _load_token_tiles(ref):
    chunks = SUBLANES
    rows = ref.shape[0] // chunks
    return jnp.concatenate([ref[pl.ds(j, rows, stride=chunks), :] for j in range(chunks)], axis=1)


def _route_tiles(nt):
    return next(k for k in (5, 4, 2, 1) if nt % k == 0)


def _router_kernel(s_ref, mod_ref, g_ref, rwt_ref, rbt_ref, tri_ref, h_ref, e1_ref, e2_ref, r1_ref,
                   r2_ref, wt_ref, cnt_ref, carry, *, ctx_rows):
    i = pl.program_id(0)
    rows, d = s_ref.shape

    @pl.when(i == 0)
    def _():
        carry[...] = jnp.zeros_like(carry)

    lat = mod_ref[1:2, :]
    shift, scale = lat[:, 3 * d:4 * d], lat[:, 4 * d:5 * d]
    if ctx_rows:
        ctx = mod_ref[0:1, :]
        is_ctx = (i == 0) & (lax.broadcasted_iota(jnp.int32, (rows, 1), 0) < ctx_rows)
        shift = jnp.where(is_ctx, ctx[:, 3 * d:4 * d], shift)
        scale = jnp.where(is_ctx, ctx[:, 4 * d:5 * d], scale)
    h = _rms(s_ref[...], g_ref[...]) * (1.0 + scale) + shift
    _store_token_tiles(h_ref, h)
    logits = _dot3(rwt_ref[...], h, ((1,), (1,))) + rbt_ref[...]
    row = lax.broadcasted_iota(jnp.int32, logits.shape, 0)
    neg = jnp.float32(-jnp.inf)
    big = jnp.int32(1 << 20)
    is_g = row < MOE_GROUPS
    gl = jnp.where(is_g, logits, neg)
    gmax = jnp.max(gl, axis=0, keepdims=True)
    grp = jnp.min(jnp.where(is_g & (gl == gmax), row, big), axis=0, keepdims=True)
    p_grp = 1.0 / jnp.sum(jnp.exp(gl - gmax), axis=0, keepdims=True)
    e_row = row - MOE_GROUPS
    in_grp = (e_row >= 0) & (e_row < MOE_EXPERTS) & ((e_row >> 3) == grp)
    l1 = jnp.where(in_grp, logits, neg)
    v1 = jnp.max(l1, axis=0, keepdims=True)
    i1 = jnp.min(jnp.where(in_grp & (l1 == v1), row, big), axis=0, keepdims=True)
    rest = in_grp & (row != i1)
    l2 = jnp.where(rest, logits, neg)
    v2 = jnp.max(l2, axis=0, keepdims=True)
    i2 = jnp.min(jnp.where(rest & (l2 == v2), row, big), axis=0, keepdims=True)
    e21 = jnp.exp(v2 - v1)
    w1 = p_grp / (1.0 + e21)
    w2 = p_grp * e21 / (1.0 + e21)
    oh1 = (row == i1).astype(F32)
    oh2 = (row == i2).astype(F32)
    oh = oh1 + oh2
    before = jnp.dot(oh.astype(BF16), tri_ref[...], preferred_element_type=F32) + carry[:, 0:1]
    r1_ref[...] = jnp.sum(oh1 * before, axis=0, keepdims=True).astype(jnp.int32)
    r2_ref[...] = jnp.sum(oh2 * before, axis=0, keepdims=True).astype(jnp.int32)
    e1_ref[...] = i1 - MOE_GROUPS
    e2_ref[...] = i2 - MOE_GROUPS
    carry[...] = carry[...] + jnp.sum(oh, axis=1, keepdims=True)
    cnt_ref[...] = carry[...]
    wt_ref[...] = jnp.concatenate([w1, w2, jnp.zeros((LANES - 2, rows), F32)], axis=0).T


def _router(s, mods, norm_g, rg_w, rg_b, re_w, re_b, ctx_tiles):
    n, d = s.shape
    rows = TM * _route_tiles(n // TM)
    steps = n // rows
    pad = ROUTE_ROWS - MOE_GROUPS - MOE_EXPERTS
    rwt = jnp.concatenate([rg_w, re_w, jnp.zeros((d, pad), F32)], axis=1).T
    rbt = jnp.broadcast_to(jnp.concatenate([rg_b, re_b, jnp.zeros((pad,), F32)])[:, None],
                           (ROUTE_ROWS, rows))
    tri = jnp.asarray(np.triu(np.ones((rows, rows), np.float32), 1), BF16)
    assert d == SUBLANES * LANES
    tile = pl.BlockSpec((rows, d), lambda i: (i, 0))
    irow = pl.BlockSpec((None, 1, rows), lambda i: (i, 0, 0))
    ishape = jax.ShapeDtypeStruct((steps, 1, rows), jnp.int32)
    return pl.pallas_call(
        functools.partial(_router_kernel, ctx_rows=ctx_tiles * TM),
        out_shape=(jax.ShapeDtypeStruct((n * SUBLANES, LANES), F32), ishape, ishape, ishape, ishape,
                   jax.ShapeDtypeStruct((n, LANES), F32),
                   jax.ShapeDtypeStruct((ROUTE_ROWS, LANES), F32)),
        grid=(steps,),
        in_specs=[tile, _full(mods.shape), _full((1, d)), _full((ROUTE_ROWS, d)),
                  _full((ROUTE_ROWS, rows)), _full((rows, rows))],
        out_specs=(pl.BlockSpec((rows * SUBLANES, LANES), lambda i: (i, 0)), irow, irow, irow, irow,
                   pl.BlockSpec((rows, LANES), lambda i: (i, 0)), _full((ROUTE_ROWS, LANES))),
        scratch_shapes=[pltpu.VMEM((ROUTE_ROWS, LANES), F32)],
        compiler_params=_cparams("arbitrary"),
        name="moe_router",
    )(s, mods, norm_g.reshape(1, d), rwt, rbt, tri)


def _finalize_kernel(cnt_ref, e1_ref, e2_ref, r1_ref, r2_ref, d1_ref, d2_ref, blk_ref):
    e1 = e1_ref[...]
    e2 = e2_ref[...]
    r1 = r1_ref[...]
    r2 = r2_ref[...]
    d1 = jnp.zeros_like(e1)
    d2 = jnp.zeros_like(e2)
    lane = lax.broadcasted_iota(jnp.int32, blk_ref.shape, 1)
    brow = lane * MOE_ROWS
    sub = lax.broadcasted_iota(jnp.int32, blk_ref.shape, 0)
    be = jnp.zeros(blk_ref.shape, jnp.int32)
    pend = jnp.zeros(blk_ref.shape, jnp.int32)
    ps = jnp.int32(0)
    for e in range(MOE_EXPERTS):
        c = cnt_ref[e]
        pe = ps + lax.shift_left(lax.shift_right_logical(c + (MOE_ROWS - 1), MOE_ROWS_LOG2),
                                 MOE_ROWS_LOG2)
        d1 = jnp.where(e1 == e, ps + r1, d1)
        d2 = jnp.where(e2 == e, ps + r2, d2)
        be = be + (brow >= pe).astype(jnp.int32)
        pend = jnp.where(lane == e, pe, pend)
        ps = pe
    d1_ref[...] = d1
    d2_ref[...] = d2
    n_used = lax.shift_right_logical(ps, MOE_ROWS_LOG2)
    blk_ref[...] = jnp.where(sub == 0, jnp.minimum(be, MOE_EXPERTS - 1),
                             jnp.where(sub == 1, pend, n_used))


def _finalize(counts, e1, e2, r1, r2, nb):
    nbp = (nb + LANES - 1) // LANES * LANES
    whole = pl.BlockSpec(e1.shape, lambda i, c: (0, 0, 0))
    ishape = jax.ShapeDtypeStruct(e1.shape, jnp.int32)
    return pl.pallas_call(
        _finalize_kernel,
        out_shape=(ishape, ishape, jax.ShapeDtypeStruct((SUBLANES, nbp), jnp.int32)),
        grid_spec=pltpu.PrefetchScalarGridSpec(
            num_scalar_prefetch=1,
            grid=(1,),
            in_specs=[whole, whole, whole, whole],
            out_specs=(whole, whole, pl.BlockSpec((SUBLANES, nbp), lambda i, c: (0, 0)))),
        compiler_params=_cparams("arbitrary"),
        name="moe_finalize",
    )(counts, e1, e2, r1, r2)


def _token_copy(src, r, dst, d, sem):
    return pltpu.make_async_copy(src.at[pl.ds(pl.multiple_of(r * SUBLANES, SUBLANES), SUBLANES), :],
                                 dst.at[pl.ds(pl.multiple_of(d * SUBLANES, SUBLANES), SUBLANES), :],
                                 sem)


def _zero_fill_padding(pend_ref, nu_ref, xs_out, zbuf, zsem):
    blk_rows = MOE_ROWS * SUBLANES
    nb = xs_out.shape[0] // blk_rows
    zbuf[...] = jnp.zeros_like(zbuf)

    def block_copy(b):
        r0 = pl.multiple_of(b * blk_rows, blk_rows)
        return pltpu.make_async_copy(zbuf, xs_out.at[pl.ds(r0, blk_rows), :], zsem)

    def seg_last_block(e):
        pe = pend_ref[e]
        prev = pend_ref[e - 1] if e > 0 else 0
        return pe > prev, lax.shift_right_logical(pe, MOE_ROWS_LOG2) - 1

    for e in range(MOE_EXPERTS):
        nonempty, b = seg_last_block(e)

        @pl.when(nonempty)
        def _():
            block_copy(b).start()

    def tail_start(b, c):
        block_copy(b).start()
        return c

    lax.fori_loop(nu_ref[0], nb, tail_start, 0)
    for e in range(MOE_EXPERTS):
        nonempty, b = seg_last_block(e)

        @pl.when(nonempty)
        def _():
            block_copy(b).wait()

    def tail_wait(b, c):
        block_copy(b).wait()
        return c

    lax.fori_loop(nu_ref[0], nb, tail_wait, 0)


def _dispatch_kernel(dest_ref, pend_ref, nu_ref, h_ref, xs_out, sem, zbuf, zsem, *, tiles):
    i = pl.program_id(0)

    @pl.when(i == 0)
    def _():
        _zero_fill_padding(pend_ref, nu_ref, xs_out, zbuf, zsem)

    rows = tiles * TM
    for q in range(tiles):
        base = i * (2 * rows) + q * TM

        def start(r, c):
            _token_copy(h_ref, q * TM + r, xs_out, dest_ref[base + r], sem).start(priority=0)
            _token_copy(h_ref, q * TM + r, xs_out, dest_ref[base + rows + r],
                        sem).start(priority=1)
            return c

        lax.fori_loop(0, TM, start, 0, unroll=8)
    for _ in range(2):
        pltpu.make_async_copy(h_ref, xs_out.at[pl.ds(0, tiles * TM * SUBLANES), :], sem).wait()


def _dispatch(dest, pad_end, n_used, h, n_rows):
    n = h.shape[0] // SUBLANES
    nt = n // TM
    tiles = _route_tiles(nt)
    return pl.pallas_call(
        functools.partial(_dispatch_kernel, tiles=tiles),
        out_shape=jax.ShapeDtypeStruct((n_rows * SUBLANES, LANES), F32),
        grid_spec=pltpu.PrefetchScalarGridSpec(
            num_scalar_prefetch=3,
            grid=(nt // tiles,),
            in_specs=[pl.BlockSpec((tiles * TM * SUBLANES, LANES), lambda i, *_: (i, 0))],
            out_specs=pl.BlockSpec(memory_space=pl.ANY),
            scratch_shapes=[pltpu.SemaphoreType.DMA, pltpu.VMEM((MOE_ROWS * SUBLANES, LANES), F32),
                            pltpu.SemaphoreType.DMA]),
        compiler_params=_cparams("arbitrary"),
        name="moe_dispatch",
    )(dest, pad_end, n_used, h)


X_SLOTS = 3


Y_SLOTS = 2


def _expert_kernel(be_ref, pend_ref, nu_ref, xs_hbm, wg_hbm, wu_hbm, wd_hbm, ys_hbm,
                   xbuf, ybuf, wg_f, wu_f, wd_f, wg_s, wu_s, wd_s, xsem, ysem, wsem, ord_ref,
                   *, layer):
    nu = nu_ref[0]
    blk_rows = MOE_ROWS * SUBLANES
    nb = ys_hbm.shape[0] // blk_rows

    def rows_of(ref, blk):
        return ref.at[pl.ds(pl.multiple_of(blk * blk_rows, blk_rows), blk_rows), :]

    def x_copy(blk, slot):
        return pltpu.make_async_copy(rows_of(xs_hbm, blk), xbuf.at[slot], xsem.at[slot])

    def y_copy(blk, slot):
        return pltpu.make_async_copy(ybuf.at[slot], rows_of(ys_hbm, blk), ysem.at[slot])

    def w_copies(e, slot):
        return (pltpu.make_async_copy(wg_hbm.at[layer, e], wg_f.at[slot], wsem.at[slot]),
                pltpu.make_async_copy(wu_hbm.at[layer, e], wu_f.at[slot], wsem.at[slot]),
                pltpu.make_async_copy(wd_hbm.at[layer, e], wd_f.at[slot], wsem.at[slot]))

    ord_ref[0] = 0
    for j in range(X_SLOTS - 1):
        @pl.when(j < nu)
        def _():
            x_copy(j, j).start()

    @pl.when(nu > 0)
    def _():
        for c in w_copies(be_ref[0], 0):
            c.start()

    def block(b, carry):
        ahead = b + (X_SLOTS - 1)

        @pl.when(ahead < nu)
        def _():
            x_copy(ahead, lax.rem(ahead, X_SLOTS)).start()

        e = be_ref[b]

        @pl.when((b == 0) | (e != be_ref[jnp.maximum(b - 1, 0)]))
        def _():
            k = ord_ref[0]
            slot = lax.rem(k, 2)
            for c in w_copies(e, slot):
                c.wait()
            wg_s[...] = wg_f[slot].astype(BF16)
            wu_s[...] = wu_f[slot].astype(BF16)
            wd_s[...] = wd_f[slot].astype(BF16)
            nxt = lax.shift_right_logical(pend_ref[e], MOE_ROWS_LOG2)

            @pl.when(nxt < nu)
            def _():
                for c in w_copies(be_ref[nxt], 1 - slot):
                    c.start(priority=1)

            ord_ref[0] = k + 1

        slot = lax.rem(b, X_SLOTS)
        yslot = lax.rem(b, Y_SLOTS)
        x_copy(b, slot).wait()

        @pl.when(b >= Y_SLOTS)
        def _():
            y_copy(b - Y_SLOTS, yslot).wait()

        x = _load_token_tiles(xbuf.at[slot]).astype(BF16)
        a = jnp.dot(x, wg_s[...], preferred_element_type=F32)
        u = jnp.dot(x, wu_s[...], preferred_element_type=F32)
        _store_token_tiles(ybuf.at[yslot], jnp.dot((_silu(a) * u).astype(BF16), wd_s[...],
                                                   preferred_element_type=F32))
        y_copy(b, yslot).start()
        return carry

    lax.fori_loop(0, nu, block, 0)
    for j in range(1, Y_SLOTS + 1):
        @pl.when(nu >= j)
        def _():
            y_copy(nu - j, lax.rem(nu - j, Y_SLOTS)).wait()

    ybuf[0] = jnp.zeros(ybuf.shape[1:], ybuf.dtype)

    def tail_start(b, c):
        y_copy(b, 0).start()
        return c

    def tail_wait(b, c):
        y_copy(b, 0).wait()
        return c

    lax.fori_loop(nu, nb, tail_start, 0)
    lax.fori_loop(nu, nb, tail_wait, 0)


def _experts(blk_expert, pad_end, n_used, xs, w_gate, w_up, w_down, layer):
    d, hid = w_gate.shape[2:]
    blk_rows = MOE_ROWS * SUBLANES
    hbm = pl.BlockSpec(memory_space=pl.ANY)
    return pl.pallas_call(
        functools.partial(_expert_kernel, layer=layer),
        out_shape=jax.ShapeDtypeStruct(xs.shape, F32),
        grid_spec=pltpu.PrefetchScalarGridSpec(
            num_scalar_prefetch=3,
            grid=(1,),
            in_specs=[hbm, hbm, hbm, hbm],
            out_specs=hbm,
            scratch_shapes=[pltpu.VMEM((X_SLOTS, blk_rows, LANES), F32),
                            pltpu.VMEM((Y_SLOTS, blk_rows, LANES), F32),
                            pltpu.VMEM((2, d, hid), F32), pltpu.VMEM((2, d, hid), F32),
                            pltpu.VMEM((2, hid, d), F32),
                            pltpu.VMEM((d, hid), BF16), pltpu.VMEM((d, hid), BF16),
                            pltpu.VMEM((hid, d), BF16),
                            pltpu.SemaphoreType.DMA((X_SLOTS,)), pltpu.SemaphoreType.DMA((Y_SLOTS,)),
                            pltpu.SemaphoreType.DMA((2,)), pltpu.SMEM((1,), jnp.int32)]),
        compiler_params=_cparams("arbitrary"),
        name="moe_experts",
    )(blk_expert, pad_end, n_used, xs, w_gate, w_up, w_down)


def _combine_kernel(dest_ref, s_ref, wt_ref, mod_ref, fg_ref, ys_ref, o_ref, ybuf, sem,
                    *, ctx_tiles, final_norm, tiles):
    i = pl.program_id(0)
    slot = i % 2
    rows = tiles * TM

    def gather(tile, slot):
        tile = jnp.asarray(tile, jnp.int32)
        base = lax.div(tile, tiles) * (2 * rows) + lax.rem(tile, tiles) * TM

        def start(r, c):
            _token_copy(ys_ref, dest_ref[base + r], ybuf.at[slot, 0], r,
                        sem.at[slot]).start(priority=0)
            _token_copy(ys_ref, dest_ref[base + rows + r], ybuf.at[slot, 1], r,
                        sem.at[slot]).start(priority=1)
            return c

        lax.fori_loop(0, TM, start, 0, unroll=8)

    @pl.when(i == 0)
    def _():
        gather(0, 0)

    @pl.when(i + 1 < pl.num_programs(0))
    def _():
        gather(i + 1, 1 - slot)

    for k in range(2):
        pltpu.make_async_copy(ys_ref.at[pl.ds(0, TM * SUBLANES), :], ybuf.at[slot, k],
                              sem.at[slot]).wait()
    gate = _mod_rows(mod_ref, i, ctx_tiles, 3)[2]
    wt = wt_ref[...]
    y = (wt[:, 0:1] * _load_token_tiles(ybuf.at[slot, 0])
         + wt[:, 1:2] * _load_token_tiles(ybuf.at[slot, 1]))
    out = s_ref[...] + gate * y
    if final_norm:
        out = _rms(out, fg_ref[...])
    o_ref[...] = out


def _combine(dest, s, wts, mods, final_g, ys, ctx_tiles, final_norm):
    n, d = s.shape
    return pl.pallas_call(
        functools.partial(_combine_kernel, ctx_tiles=ctx_tiles, final_norm=final_norm,
                          tiles=_route_tiles(n // TM)),
        out_shape=jax.ShapeDtypeStruct((n, d), F32),
        grid_spec=pltpu.PrefetchScalarGridSpec(
            num_scalar_prefetch=1,
            grid=(n // TM,),
            in_specs=[pl.BlockSpec((TM, d), lambda i, dst: (i, 0)),
                      pl.BlockSpec((TM, LANES), lambda i, dst: (i, 0)),
                      pl.BlockSpec(mods.shape, lambda i, dst: (0, 0)),
                      pl.BlockSpec((1, d), lambda i, dst: (0, 0)),
                      pl.BlockSpec(memory_space=pl.ANY)],
            out_specs=pl.BlockSpec((TM, d), lambda i, dst: (i, 0)),
            scratch_shapes=[pltpu.VMEM((2, 2, TM * SUBLANES, LANES), F32),
                            pltpu.SemaphoreType.DMA((2,))]),
        compiler_params=_cparams("arbitrary"),
        name="moe_combine",
    )(dest, s, wts, mods, final_g.reshape(1, d), ys)


def _moe_layer(s, mods, norm_g, rg_w, rg_b, re_w, re_b, w_gate, w_up, w_down, layer, ctx_tiles,
               final_g, final_norm):
    n, d = s.shape
    h, e1, e2, r1, r2, wts, cnt = _router(s, mods, norm_g, rg_w, rg_b, re_w, re_b, ctx_tiles)
    counts = cnt[MOE_GROUPS:MOE_GROUPS + MOE_EXPERTS, 0].astype(jnp.int32)
    nb = (2 * n + MOE_EXPERTS * (MOE_ROWS - 1)) // MOE_ROWS + 1
    d1, d2, blk = _finalize(counts, e1, e2, r1, r2, nb)
    dest = jnp.concatenate([d1, d2], axis=1).reshape(2 * n)
    n_used = blk[2, :1]
    pad_end = blk[1, :MOE_EXPERTS]
    xs = _dispatch(dest, pad_end, n_used, h, nb * MOE_ROWS)
    ys = _experts(blk[0, :nb], pad_end, n_used, xs, w_gate, w_up, w_down, layer)
    return _combine(dest, s, wts, mods, final_g, ys, ctx_tiles, final_norm)


def _conv_tile(x, prev_ref, next_ref, has_prev, has_next, w_ref, b_ref):
    rows = x.shape[0]
    S = SUBLANES
    ridx = lax.broadcasted_iota(jnp.int32, (S, x.shape[1]), 0)
    pm = jnp.where(has_prev, 1.0, 0.0)
    nm = jnp.where(has_next, 1.0, 0.0)
    p2 = prev_ref[S - 2:S - 1, :] * pm
    p1 = prev_ref[S - 1:S, :] * pm
    n1 = next_ref[0:1, :] * nm

    def fix_head(rolled, head):
        return jnp.concatenate([head(rolled[:S]), rolled[S:]], axis=0)

    xm1 = fix_head(pltpu.roll(x, 1, axis=0), lambda g: jnp.where(ridx == 0, p1, g))
    xm2 = fix_head(pltpu.roll(x, 2, axis=0),
                   lambda g: jnp.where(ridx == 0, p2, jnp.where(ridx == 1, p1, g)))
    xp1 = pltpu.roll(x, rows - 1, axis=0)
    xp1 = jnp.concatenate([xp1[:rows - S], jnp.where(ridx == S - 1, n1, xp1[rows - S:])], axis=0)
    return (xm2 * w_ref[0:1, :] + xm1 * w_ref[1:2, :] + x * w_ref[2:3, :]
            + xp1 * w_ref[3:4, :] + b_ref[...])


def _ml_proj_kernel(s_ref, mod_ref, g_ref, w_ref, wg_ref, gb_ref, qk_ref, v_ref, o_ref, gt_ref):
    i = pl.program_id(0)
    shift, scale, _ = _mod_rows(mod_ref, i, 1, 0)
    h = _rms(s_ref[...], g_ref[...]) * (1.0 + scale) + shift
    z = _bdot(h, w_ref[...])
    nqk = qk_ref.shape[1]
    nv = v_ref.shape[1]
    qk_ref[...] = z[:, :nqk]
    v_ref[...] = z[:, nqk:nqk + nv].astype(BF16)
    o_ref[...] = z[:, nqk + nv:].astype(BF16)
    pre = _dot3(h, wg_ref[...], ((1,), (0,))) + gb_ref[...]
    lane = lax.broadcasted_iota(jnp.int32, pre.shape, 1)
    is_forget = ((lane >> 2) & 1) == 1
    gt_ref[...] = jnp.where(is_forget, -_softplus(-pre), pre)


def _ml_proj(s, mods, norm_g, w_in, gate_b):
    n, d = s.shape
    nqk = 2 * ML_HEADS * ML_DK
    nv = ML_HEADS * ML_DV
    n_main = nqk + 2 * nv
    n_gate = w_in.shape[1] - n_main
    w_main = w_in[:, :n_main].astype(BF16)
    w_gate = jnp.concatenate([w_in[:, n_main:], jnp.zeros((d, LANES - n_gate), F32)], axis=1)
    gb = jnp.concatenate([gate_b.reshape(n_gate), jnp.zeros((LANES - n_gate,), F32)]).reshape(1, LANES)
    tile = lambda w: pl.BlockSpec((TM, w), lambda i: (i, 0))
    return pl.pallas_call(
        _ml_proj_kernel,
        out_shape=(jax.ShapeDtypeStruct((n, nqk), F32), jax.ShapeDtypeStruct((n, nv), BF16),
                   jax.ShapeDtypeStruct((n, nv), BF16), jax.ShapeDtypeStruct((n, LANES), F32)),
        grid=(n // TM,),
        in_specs=[tile(d), _full(mods.shape), _full((1, d)), _full(w_main.shape),
                  _full((d, LANES)), _full((1, LANES))],
        out_specs=(tile(nqk), tile(nv), tile(nv), tile(LANES)),
        compiler_params=_cparams("arbitrary"),
        name="mlstm_proj",
    )(s, mods, norm_g.reshape(1, d), w_main, w_gate, gb)


def _ml_chunk_index(j, n_chunks, ctx_chunks, reverse):
    if not reverse:
        return j
    return jnp.where(j < ctx_chunks, ctx_chunks - 1 - j, n_chunks - 1 + ctx_chunks - j)


def _ml_rec_kernel(qk_ref, qkp_ref, qkn_ref, v_ref, gt_ref, gtt_ref, cw_ref, cb_ref, o_ref,
                   c_scr, n_scr, m_scr, *, reverse, n_chunks, ctx_chunks):
    j = pl.program_id(0)
    c = _ml_chunk_index(j, n_chunks, ctx_chunks, reverse)

    @pl.when(j == 0)
    def _():
        c_scr[...] = jnp.zeros_like(c_scr)
        n_scr[...] = jnp.zeros_like(n_scr)
        m_scr[...] = jnp.zeros_like(m_scr)

    has_prev = (c != 0) & (c != ctx_chunks)
    has_next = (c != ctx_chunks - 1) & (c != n_chunks - 1)
    qk = _silu(_conv_tile(qk_ref[...], qkp_ref, qkn_ref, has_prev, has_next, cw_ref, cb_ref))
    L = ML_CHUNK
    ri = lax.broadcasted_iota(jnp.int32, (L, L), 0)
    ci = lax.broadcasted_iota(jnp.int32, (L, L), 1)
    past = (ci >= ri) if reverse else (ci <= ri)
    pastf = past.astype(F32)
    gt = gt_ref[...]
    gtt = gtt_ref[...]
    b_col = jnp.dot(pastf, gt, precision=HI, preferred_element_type=F32)
    b_row = jnp.dot(gtt, pastf.T, precision=HI, preferred_element_type=F32)
    last = 0 if reverse else L - 1
    dbase = 8 if reverse else 0
    nq = ML_HEADS * ML_DK
    for hd in range(ML_HEADS):
        cl = dbase + hd
        cf = dbase + 4 + hd
        q = qk[:, hd * ML_DK:(hd + 1) * ML_DK] * (ML_DK ** -0.5)
        k = qk[:, nq + hd * ML_DK:nq + (hd + 1) * ML_DK]
        v = v_ref[:, hd * ML_DV:(hd + 1) * ML_DV]
        li_c = gt[:, cl:cl + 1]
        li_r = gtt[cl:cl + 1, :]
        b_c = b_col[:, cf:cf + 1]
        b_r = b_row[cf:cf + 1, :]
        g = b_r[:, last:last + 1]
        m0 = m_scr[hd:hd + 1, 0:1]
        c0 = c_scr[hd]
        n0 = n_scr[hd:hd + 1, :]
        a_c = g - b_c + li_c
        a_r = g - b_r + li_r
        m_loc = jnp.max(a_r, axis=-1, keepdims=True)
        inter = b_c + m0
        dlog = jnp.where(past, b_c - b_r + li_r, -jnp.inf)
        m = jnp.maximum(inter, jnp.max(dlog, axis=-1, keepdims=True))
        qb = q.astype(BF16)
        sc = lax.dot_general(qb, k.astype(BF16), (((1,), (1,)), ((), ())),
                             preferred_element_type=F32) * jnp.exp(dlog - m)
        w_inter = jnp.exp(inter - m)
        num = (jnp.dot(sc.astype(BF16), v, preferred_element_type=F32)
               + w_inter * jnp.dot(qb, c0.astype(BF16), preferred_element_type=F32))
        den = (jnp.sum(sc, axis=-1, keepdims=True)
               + w_inter * jnp.sum(q * n0, axis=-1, keepdims=True))
        o_ref[:, hd * ML_DV:(hd + 1) * ML_DV] = (
            num / jnp.maximum(jnp.abs(den), jnp.exp(-m))).astype(BF16)
        m_new = jnp.maximum(g + m0, m_loc)
        dec = jnp.exp(g + m0 - m_new)
        scl = jnp.exp(m_loc - m_new)
        kw = k * jnp.exp(a_c - m_loc)
        c_scr[hd] = dec * c0 + scl * jnp.dot(kw.T.astype(BF16), v, preferred_element_type=F32)
        n_scr[hd:hd + 1, :] = dec * n0 + scl * jnp.sum(kw, axis=0, keepdims=True)
        m_scr[hd:hd + 1, :] = jnp.broadcast_to(m_new, (1, LANES))


def _ml_rec(qk, v, gt, gtt, conv_w, conv_b, reverse):
    n, nqk = qk.shape
    nv = v.shape[1]
    L = ML_CHUNK
    nc = n // L
    cc = TM // L
    hb = L // SUBLANES
    idx = lambda j: _ml_chunk_index(j, nc, cc, reverse)
    last8 = n // SUBLANES - 1
    return pl.pallas_call(
        functools.partial(_ml_rec_kernel, reverse=reverse, n_chunks=nc, ctx_chunks=cc),
        out_shape=jax.ShapeDtypeStruct((n, nv), BF16),
        grid=(nc,),
        in_specs=[pl.BlockSpec((L, nqk), lambda j: (idx(j), 0)),
                  pl.BlockSpec((SUBLANES, nqk), lambda j: (jnp.maximum(idx(j) * hb - 1, 0), 0)),
                  pl.BlockSpec((SUBLANES, nqk), lambda j: (jnp.minimum((idx(j) + 1) * hb, last8), 0)),
                  pl.BlockSpec((L, nv), lambda j: (idx(j), 0)),
                  pl.BlockSpec((L, LANES), lambda j: (idx(j), 0)),
                  pl.BlockSpec((2 * SUBLANES, L), lambda j: (0, idx(j))),
                  _full((4, nqk)), _full((1, nqk))],
        out_specs=pl.BlockSpec((L, nv), lambda j: (idx(j), 0)),
        scratch_shapes=[pltpu.VMEM((ML_HEADS, ML_DK, ML_DV), F32),
                        pltpu.VMEM((SUBLANES, ML_DK), F32),
                        pltpu.VMEM((SUBLANES, LANES), F32)],
        compiler_params=_cparams("arbitrary"),
        name="mlstm_rev" if reverse else "mlstm_fwd",
    )(qk, qk, qk, v, gt, gtt, conv_w, conv_b.reshape(1, nqk))


def _ml_out_kernel(hf_ref, hr_ref, o_ref, s_ref, mod_ref, ng_ref, w_ref, out_ref, p_scr):
    i = pl.program_id(0)
    gate = _mod_rows(mod_ref, i, 1, 0)[2]
    hs = hf_ref[...].astype(F32) + hr_ref[...].astype(F32)
    sig = _sigmoid(o_ref[...].astype(F32))
    ng = ng_ref[...]
    for hd in range(ML_HEADS):
        cs = slice(hd * ML_DV, (hd + 1) * ML_DV)
        seg = hs[:, cs]
        hn = seg * lax.rsqrt(jnp.mean(seg * seg, axis=-1, keepdims=True) + EPS) * ng[:, cs]
        p_scr[:, cs] = (hn * sig[:, cs]).astype(BF16)
    y = jnp.dot(p_scr[...], w_ref[...], preferred_element_type=F32)
    out_ref[...] = s_ref[...] + gate * y


def _ml_out(hf, hr, o, s, mods, norm_g, w_out):
    n, d = s.shape
    nv = hf.shape[1]
    tile = lambda w: pl.BlockSpec((TM, w), lambda i: (i, 0))
    return pl.pallas_call(
        _ml_out_kernel,
        out_shape=jax.ShapeDtypeStruct((n, d), F32),
        grid=(n // TM,),
        in_specs=[tile(nv), tile(nv), tile(nv), tile(d), _full(mods.shape), _full((1, nv)),
                  _full(w_out.shape)],
        out_specs=tile(d),
        scratch_shapes=[pltpu.VMEM((TM, nv), BF16)],
        compiler_params=_cparams("arbitrary"),
        name="mlstm_out",
    )(hf, hr, o, s, mods, norm_g.reshape(1, nv), w_out.astype(BF16))


def _mlstm_layer(s, mods, norm_g, w_in, conv_w, conv_b, gate_b, ml_norm_g, w_out):
    qk, v, o, gt = _ml_proj(s, mods, norm_g, w_in, gate_b)
    gtt = gt[:, :2 * SUBLANES].T
    hf = _ml_rec(qk, v, gt, gtt, conv_w, conv_b, False)
    hr = _ml_rec(qk, v, gt, gtt, conv_w, conv_b, True)
    return _ml_out(hf, hr, o, s, mods, ml_norm_g, w_out)


def _lru_proj_kernel(s_ref, mod_ref, g_ref, w_ref, gl_ref, xr_ref):
    i = pl.program_id(0)
    shift, scale, _ = _mod_rows(mod_ref, i, 1, 0)
    h = _rms(s_ref[...], g_ref[...]) * (1.0 + scale) + shift
    z = _bdot(h, w_ref[...])
    w = gl_ref.shape[1]
    gl_ref[...] = _gelu(z[:, :w]).astype(BF16)
    xr_ref[...] = z[:, w:]


def _lru_proj(s, mods, norm_g, w_in):
    n, d = s.shape
    w = w_in.shape[1] // 2
    tile = lambda c: pl.BlockSpec((TM, c), lambda i: (i, 0))
    return pl.pallas_call(
        _lru_proj_kernel,
        out_shape=(jax.ShapeDtypeStruct((n, w), BF16), jax.ShapeDtypeStruct((n, w), F32)),
        grid=(n // TM,),
        in_specs=[tile(d), _full(mods.shape), _full((1, d)), _full(w_in.shape)],
        out_specs=(tile(w), tile(w)),
        compiler_params=_cparams("arbitrary"),
        name="rglru_proj",
    )(s, mods, norm_g.reshape(1, d), w_in.astype(BF16))


def _lru_tile_index(j, n_tiles, reverse):
    if not reverse:
        return j
    return jnp.where(j == 0, 0, n_tiles - j)


def _lru_scan_kernel(x_ref, xp_ref, xn_ref, cw_ref, cb_ref, wg_ref, ba_ref, bx_ref, lam_ref,
                     o_ref, a_scr, u_scr, carry, *, reverse, n_tiles):
    j = pl.program_id(0)
    t = _lru_tile_index(j, n_tiles, reverse)

    @pl.when(j == 0)
    def _():
        carry[...] = jnp.zeros_like(carry)

    has_prev = t > 1
    has_next = (t != 0) & (t != n_tiles - 1)
    xr = _conv_tile(x_ref[...], xp_ref, xn_ref, has_prev, has_next, cw_ref, cb_ref)
    sp = _softplus(-lam_ref[...])
    B = LRU_BLOCK
    for hd in range(LRU_HEADS):
        cs = slice(hd * B, (hd + 1) * B)
        xh = xr[:, cs]
        y = jnp.dot(xh.astype(BF16), wg_ref[hd], preferred_element_type=F32)
        r = _sigmoid(y[:, :B] + ba_ref[:, cs])
        ig = _sigmoid(y[:, B:] + bx_ref[:, cs])
        log_a = -LRU_C * r * sp[:, cs]
        a = jnp.exp(log_a)
        a_scr[:, cs] = a
        v = 1.0 - a * a
        u_scr[:, cs] = jnp.where(v > 0.0, v * lax.rsqrt(v), 0.0) * (ig * xh)

    S = SUBLANES
    w = a_scr.shape[1]
    sidx = lax.broadcasted_iota(jnp.int32, (S, w), 0)

    def group(gi, c):
        g = (TM // S - 1 - gi) if reverse else gi
        r0 = pl.multiple_of(g * S, S)
        a = a_scr[pl.ds(r0, S), :]
        u = u_scr[pl.ds(r0, S), :]
        for sft in (1, 2, 4):
            if reverse:
                ok = sidx < S - sft
                a_e = pltpu.roll(a, S - sft, axis=0)
                u_e = pltpu.roll(u, S - sft, axis=0)
            else:
                ok = sidx >= sft
                a_e = pltpu.roll(a, sft, axis=0)
                u_e = pltpu.roll(u, sft, axis=0)
            u = jnp.where(ok, a * u_e + u, u)
            a = jnp.where(ok, a * a_e, a)
        hcur = a * carry[...] + u
        u_scr[pl.ds(r0, S), :] = hcur
        edge = 0 if reverse else S - 1
        carry[...] = jnp.broadcast_to(hcur[edge:edge + 1, :], (S, w))
        return c

    lax.fori_loop(0, TM // S, group, 0)
    o_ref[...] = u_scr[...].astype(BF16)


def _lru_scan(xraw, conv_w, conv_b, w_a, b_a, w_x, b_x, lam, reverse):
    n, w = xraw.shape
    nt = n // TM
    hb = TM // SUBLANES
    idx = lambda j: _lru_tile_index(j, nt, reverse)
    last8 = n // SUBLANES - 1
    wg = jnp.concatenate([w_a, w_x], axis=-1).astype(BF16)
    return pl.pallas_call(
        functools.partial(_lru_scan_kernel, reverse=reverse, n_tiles=nt),
        out_shape=jax.ShapeDtypeStruct((n, w), BF16),
        grid=(nt,),
        in_specs=[pl.BlockSpec((TM, w), lambda j: (idx(j), 0)),
                  pl.BlockSpec((SUBLANES, w), lambda j: (jnp.maximum(idx(j) * hb - 1, 0), 0)),
                  pl.BlockSpec((SUBLANES, w), lambda j: (jnp.minimum((idx(j) + 1) * hb, last8), 0)),
                  _full((4, w)), _full((1, w)), _full(wg.shape), _full((1, w)), _full((1, w)),
                  _full((1, w))],
        out_specs=pl.BlockSpec((TM, w), lambda j: (idx(j), 0)),
        scratch_shapes=[pltpu.VMEM((TM, w), F32), pltpu.VMEM((TM, w), F32),
                        pltpu.VMEM((SUBLANES, w), F32)],
        compiler_params=_cparams("arbitrary"),
        name="rglru_rev" if reverse else "rglru_fwd",
    )(xraw, xraw, xraw, conv_w, conv_b.reshape(1, w), wg, b_a.reshape(1, w), b_x.reshape(1, w),
      lam.reshape(1, w))


def _lru_out_kernel(gl_ref, hf_ref, hr_ref, s_ref, mod_ref, w_ref, out_ref):
    gate = _mod_rows(mod_ref, 1, 0, 0)[2]
    p = gl_ref[...].astype(F32) * (hf_ref[...].astype(F32) + hr_ref[...].astype(F32))
    out_ref[...] = s_ref[...] + gate * _bdot(p, w_ref[...])


def _lru_out(gl, hf, hr, s, mods, w_out):
    n, d = s.shape
    w = gl.shape[1]
    lat = lambda c: pl.BlockSpec((TM, c), lambda i: (i + 1, 0))
    return pl.pallas_call(
        _lru_out_kernel,
        out_shape=jax.ShapeDtypeStruct((n - TM, d), F32),
        grid=(n // TM - 1,),
        in_specs=[lat(w), lat(w), lat(w), lat(d), _full(mods.shape), _full(w_out.shape)],
        out_specs=pl.BlockSpec((TM, d), lambda i: (i, 0)),
        compiler_params=_cparams("arbitrary"),
        name="rglru_out",
    )(gl, hf, hr, s, mods, w_out.astype(BF16))


def _rglru_layer(s, mods, norm_g, w_in, conv_w, conv_b, w_a, b_a, w_x, b_x, lam, w_out):
    gl, xraw = _lru_proj(s, mods, norm_g, w_in)
    hf = _lru_scan(xraw, conv_w, conv_b, w_a[0], b_a[0], w_x[0], b_x[0], lam[0], False)
    hr = _lru_scan(xraw, conv_w, conv_b, w_a[1], b_a[1], w_x[1], b_x[1], lam[1], True)
    return _lru_out(gl, hf, hr, s, mods, w_out)


def _fn_proj_kernel(s_ref, mod_ref, g_ref, wt_ref, cs_ref, yr_ref, yi_ref, ar_scr, ai_scr):
    shift, scale, _ = _mod_rows(mod_ref, 1, 0, 0)
    nm = wt_ref.shape[0]
    gw = nm // FN_GROUPS
    per = TM // FFT_N2
    nj = FN_TB // FFT_N2
    csb = cs_ref[...].astype(BF16)

    def sub(tc, c):
        r0 = pl.multiple_of(tc * TM, TM)
        h = _rms(s_ref[pl.ds(r0, TM), :], g_ref[...]) * (1.0 + scale) + shift
        zt = lax.dot_general(wt_ref[...], h.astype(BF16), (((1,), (1,)), ((), ())),
                             preferred_element_type=F32).astype(BF16)
        for g in range(FN_GROUPS):
            y = jnp.dot(csb, zt[g * gw:(g + 1) * gw, :], preferred_element_type=F32)
            for q in range(per):
                row0 = pl.multiple_of((tc * per + q) * _slab_pitch(nm) + g * gw, SUBLANES)
                ar_scr[pl.ds(row0, gw), :] = y[:gw, q * FFT_N2:(q + 1) * FFT_N2]
                ai_scr[pl.ds(row0, gw), :] = y[gw:, q * FFT_N2:(q + 1) * FFT_N2]
        return c

    lax.fori_loop(0, FN_TB // TM, sub, 0)

    def relayout(m, c):
        yr_ref[m] = ar_scr[pl.ds(m, nj, stride=_slab_pitch(nm)), :]
        yi_ref[m] = ai_scr[pl.ds(m, nj, stride=_slab_pitch(nm)), :]
        return c

    lax.fori_loop(0, nm, relayout, 0, unroll=8)


def _slab_pitch(rows):
    return rows + SUBLANES


def _dft_cos_sin(n, scale):
    k = np.arange(n, dtype=np.int64)
    ang = 2.0 * np.pi * ((k[:, None] * k[None, :]) % n).astype(np.float64) / n
    return np.cos(ang) * scale, np.sin(ang) * scale


def _fn_proj(s, mods, norm_g, w_in):
    t, d = s.shape
    nm = w_in.shape[1]
    gw = nm // FN_GROUPS
    n1 = t // FFT_N2
    nj = FN_TB // FFT_N2
    c, sn = _dft_cos_sin(gw, gw ** -0.5)
    cs = jnp.asarray(np.concatenate([c, -sn], axis=0), F32)
    yspec = pl.BlockSpec((nm, nj, FFT_N2), lambda i: (0, i, 0))
    yshape = jax.ShapeDtypeStruct((nm, n1, FFT_N2), F32)
    return pl.pallas_call(
        _fn_proj_kernel,
        out_shape=(yshape, yshape),
        grid=(t // FN_TB,),
        in_specs=[pl.BlockSpec((FN_TB, d), lambda i: (i, 0)), _full(mods.shape), _full((1, d)),
                  _full((nm, d)), _full(cs.shape)],
        out_specs=(yspec, yspec),
        scratch_shapes=[pltpu.VMEM((nj * _slab_pitch(nm), FFT_N2), F32),
                        pltpu.VMEM((nj * _slab_pitch(nm), FFT_N2), F32)],
        compiler_params=_cparams("arbitrary"),
        name="fourier_proj",
    )(s, mods, norm_g.reshape(1, d), w_in.T.astype(BF16), cs)


def _fn_fft_kernel(yr_ref, yi_ref, m_ref, tc_ref, ts_ref, d_ref, o_ref):
    n1 = yr_ref.shape[1]
    n2 = FFT_N2
    xr = jnp.concatenate([yr_ref[m].astype(BF16) for m in range(FN_CB)], axis=1)
    xi = jnp.concatenate([yi_ref[m].astype(BF16) for m in range(FN_CB)], axis=1)
    a = jnp.dot(m_ref[...].astype(BF16), jnp.concatenate([xr, xi], axis=0),
                preferred_element_type=F32)
    ar = a[:n1]
    ai = a[n1:]
    tc = jnp.concatenate([tc_ref[...]] * FN_CB, axis=1)
    ts = jnp.concatenate([ts_ref[...]] * FN_CB, axis=1)
    br = ar * tc + ai * ts
    bi = ai * tc - ar * ts
    bst = jnp.concatenate(
        [jnp.concatenate([br[:, m * n2:(m + 1) * n2], bi[:, m * n2:(m + 1) * n2]], axis=1)
         for m in range(FN_CB)], axis=0).astype(BF16)
    res = lax.dot_general(d_ref[...].astype(BF16), bst, (((1,), (1,)), ((), ())),
                          preferred_element_type=F32)
    for m in range(FN_CB):
        o_ref[m] = res[:, m * n1:(m + 1) * n1]


def _fn_fft(yr, yi):
    nm, n1, n2 = yr.shape
    t = n1 * n2
    c, sn = _dft_cos_sin(n1, n1 ** -0.5)
    m = jnp.asarray(np.block([[c, sn], [-sn, c]]), F32)
    k1 = np.arange(n1, dtype=np.int64)[:, None]
    t2 = np.arange(n2, dtype=np.int64)[None, :]
    ang = 2.0 * np.pi * ((k1 * t2) % t).astype(np.float64) / t
    tc = jnp.asarray(np.cos(ang), F32)
    ts = jnp.asarray(np.sin(ang), F32)
    c2, s2 = _dft_cos_sin(n2, n2 ** -0.5)
    dm = jnp.asarray(np.concatenate([c2, s2], axis=1), F32)
    yspec = pl.BlockSpec((FN_CB, n1, n2), lambda i: (i, 0, 0))
    return pl.pallas_call(
        _fn_fft_kernel,
        out_shape=jax.ShapeDtypeStruct((nm, n2, n1), F32),
        grid=(nm // FN_CB,),
        in_specs=[yspec, yspec, _full(m.shape), _full(tc.shape), _full(ts.shape), _full(dm.shape)],
        out_specs=pl.BlockSpec((FN_CB, n2, n1), lambda i: (i, 0, 0)),
        compiler_params=_cparams("arbitrary"),
        name="fourier_fft",
    )(yr, yi, m, tc, ts, dm)


def _fn_out_kernel(ft_ref, w_ref, s_ref, mod_ref, o_ref, a_scr):
    gate = _mod_rows(mod_ref, 1, 0, 0)[2]
    nm, nj, n1 = ft_ref.shape

    def relayout(m, c):
        a_scr[pl.ds(m, nj, stride=_slab_pitch(nm)), :] = ft_ref[m]
        return c

    lax.fori_loop(0, nm, relayout, 0, unroll=8)
    for j in range(nj):
        p0 = j * _slab_pitch(nm)
        slab = a_scr[p0:p0 + nm, :].astype(BF16)
        y = lax.dot_general(slab, w_ref[...], (((0,), (0,)), ((), ())), preferred_element_type=F32)
        rows = slice(j * n1, (j + 1) * n1)
        o_ref[rows, :] = s_ref[rows, :] + gate * y


def _fn_out(ft, s, mods, w_out):
    t, d = s.shape
    nm, n2, n1 = ft.shape
    nj = FN_TB // n1
    tok = pl.BlockSpec((FN_TB, d), lambda i: (i, 0))
    return pl.pallas_call(
        _fn_out_kernel,
        out_shape=jax.ShapeDtypeStruct((t, d), F32),
        grid=(t // FN_TB,),
        in_specs=[pl.BlockSpec((nm, nj, n1), lambda i: (0, i, 0)), _full(w_out.shape), tok,
                  _full(mods.shape)],
        out_specs=tok,
        scratch_shapes=[pltpu.VMEM((nj * _slab_pitch(nm), n1), F32)],
        compiler_params=_cparams("arbitrary"),
        name="fourier_out",
    )(ft, w_out.astype(BF16), s, mods)


def _fourier_layer(s, mods, norm_g, w_in, w_out):
    yr, yi = _fn_proj(s, mods, norm_g, w_in)
    return _fn_out(_fn_fft(yr, yi), s, mods, w_out)


def kernel(x, c, ctx, c_ctx, ada_w, ada_b, norm_mix_g, norm_ffn_g, final_norm_g, router_group_w, router_group_b, router_expert_w, router_expert_b, expert_w_gate, expert_w_up, expert_w_down, cm_w_in, cm_v_norm_g, cm_w_s, cm_b_s, cm_w_out, ml_w_in, ml_conv_w, ml_conv_b, ml_gate_b, ml_norm_g, ml_w_out, lru_w_in, lru_conv_w, lru_conv_b, lru_w_a, lru_b_a, lru_w_x, lru_b_x, lru_lambda, lru_w_out, fn_w_in, fn_w_out):
    bsz, seq, d = x.shape
    assert bsz == 1 and ada_w.shape[0] == 4 and ctx.shape[1] == TM
    c_rows = jnp.concatenate([c_ctx[None, :], c, jnp.zeros((SUBLANES - 2, d), F32)], axis=0)
    mods = _ada_table(c_rows, ada_w, ada_b)

    def moe(s, i, ctx_tiles, final_norm=False):
        return _moe_layer(s, mods[i], norm_ffn_g[i], router_group_w[i], router_group_b[i],
                          router_expert_w[i], router_expert_b[i], expert_w_gate, expert_w_up,
                          expert_w_down, i, ctx_tiles, final_norm_g, final_norm)

    s = _chunk_mlp_layer(x[0], ctx[0], mods[0], norm_mix_g[0], cm_w_in[0], cm_v_norm_g[0],
                         cm_w_s[0], cm_b_s[0], cm_w_out[0])
    s = moe(s, 0, 1)
    s = _mlstm_layer(s, mods[1], norm_mix_g[1], ml_w_in[0], ml_conv_w[0], ml_conv_b[0],
                     ml_gate_b[0], ml_norm_g[0], ml_w_out[0])
    s = moe(s, 1, 1)
    s = _rglru_layer(s, mods[2], norm_mix_g[2], lru_w_in[0], lru_conv_w[0], lru_conv_b[0],
                     lru_w_a[0], lru_b_a[0], lru_w_x[0], lru_b_x[0], lru_lambda[0], lru_w_out[0])
    s = moe(s, 2, 0)
    s = _fourier_layer(s, mods[3], norm_mix_g[3], fn_w_in[0], fn_w_out[0])
    s = moe(s, 3, 0, final_norm=True)
    return s[None]
```

```python
import functools
import math
from typing import NamedTuple

import jax
import jax.numpy as jnp
import numpy as np
from jax import lax
from jax.experimental import pallas as pl
from jax.experimental.pallas import tpu as pltpu

F32 = jnp.float32
BF16 = jnp.bfloat16

EPS = 1e-6
POS_BASE = 10000.0
GRID_W = 64
N_MOD = 6
TM = 256
LANES = 128
SUBLANES = 8
VMEM_LIMIT = 56 * 1024 * 1024

CM_CHUNK = 128
CM_GROUPS = 4
ML_HEADS = 4
ML_DK = 128
ML_DV = 256
ML_CHUNK = 128
LRU_HEADS = 10
LRU_BLOCK = 128
LRU_C = 8.0
FN_GROUPS = 4
FFT_N2 = 128
MOE_GROUPS = 4
MOE_EPG = 8
MOE_EXPERTS = MOE_GROUPS * MOE_EPG
MOE_ROWS_LOG2 = 8
MOE_ROWS = 1 << MOE_ROWS_LOG2
ROUTE_ROWS = 40
FN_TB = 1024
FN_CB = 16
CONV_LEFT = 2

HI = lax.Precision.HIGHEST


def _cparams(*sem):
    return pltpu.CompilerParams(dimension_semantics=sem, vmem_limit_bytes=VMEM_LIMIT)


def _full(shape):
    nd = len(shape)
    return pl.BlockSpec(shape, lambda *_: (0,) * nd)


def _rms(x, g):
    return x * lax.rsqrt(jnp.mean(x * x, axis=-1, keepdims=True) + EPS) * g


def _gelu(x):
    c = math.sqrt(2.0 / math.pi)
    return 0.5 * x * (1.0 + jnp.tanh(c * (x + 0.044715 * (x * x * x))))


def _sigmoid(x):
    return 0.5 * jnp.tanh(0.5 * x) + 0.5


def _silu(x):
    return x * _sigmoid(x)


def _softplus(x):
    return jnp.maximum(x, 0.0) + jnp.log(1.0 + jnp.exp(-jnp.abs(x)))


def _mod_rows(mod_ref, tile, ctx_tiles, first):
    row = jnp.where(tile < ctx_tiles, 0, 1)
    m = mod_ref[pl.ds(row, 1), :]
    d = m.shape[1] // N_MOD
    return tuple(m[:, (first + j) * d:(first + j + 1) * d] for j in range(3))


def _bdot(a, b):
    return jnp.dot(a.astype(BF16), b.astype(BF16), preferred_element_type=F32)


def _split_bf16(x):
    hi = x.astype(BF16)
    return hi, (x - hi.astype(F32)).astype(BF16)


def _dot3(a, b, dims):
    a_hi, a_lo = _split_bf16(a)
    b_hi, b_lo = _split_bf16(b)
    dg = functools.partial(lax.dot_general, dimension_numbers=(dims, ((), ())),
                           preferred_element_type=F32)
    return dg(a_hi, b_hi) + dg(a_hi, b_lo) + dg(a_lo, b_hi)


def _ada_kernel(c_ref, w_ref, b_ref, o_ref):
    c = c_ref[...]
    o_ref[...] = _dot3(_silu(c), w_ref[...], ((1,), (0,))) + b_ref[...]


def _ada_table(c_rows, ada_w, ada_b):
    depth, d, n = ada_w.shape
    tn = 2048
    return pl.pallas_call(
        _ada_kernel,
        out_shape=jax.ShapeDtypeStruct((depth, SUBLANES, n), F32),
        grid=(depth, n // tn),
        in_specs=[_full((SUBLANES, d)),
                  pl.BlockSpec((None, d, tn), lambda i, j: (i, 0, j)),
                  pl.BlockSpec((None, 1, tn), lambda i, j: (i, 0, j))],
        out_specs=pl.BlockSpec((None, SUBLANES, tn), lambda i, j: (i, 0, j)),
        compiler_params=_cparams("arbitrary", "arbitrary"),
        name="ada_table",
    )(c_rows, ada_w, ada_b.reshape(depth, 1, n))


def _pos_tables(seq, d):
    q = d // 4
    freq = jnp.exp(-math.log(POS_BASE) * jnp.arange(q, dtype=F32) / q)
    ar = jnp.arange(seq // GRID_W, dtype=F32)[:, None] * freq
    ac = jnp.arange(GRID_W, dtype=F32)[:, None] * freq
    return (jnp.concatenate([jnp.sin(ar), jnp.cos(ar)], axis=-1),
            jnp.concatenate([jnp.sin(ac), jnp.cos(ac)], axis=-1))


def _cm_kernel(x_ref, ctx_ref, rt_ref, ct_ref, mod_ref, g_ref, win_ref, vg_ref, ws_ref, bs_ref,
               wout_ref, o_ref, p_scr, x_scr):
    i = pl.program_id(0)

    @pl.when(i == 0)
    def _():
        x_scr[...] = ctx_ref[...]

    @pl.when(i > 0)
    def _():
        rows_per_tile = TM // GRID_W
        q2 = rt_ref.shape[1]
        r0 = (i - 1) * rows_per_tile
        rt = jnp.concatenate(
            [jnp.broadcast_to(rt_ref[pl.ds(r0 + j, 1), :], (GRID_W, q2))
             for j in range(rows_per_tile)], axis=0)
        ct = jnp.concatenate([ct_ref[...]] * rows_per_tile, axis=0)
        x_scr[...] = x_ref[...] + jnp.concatenate([rt, ct], axis=1)

    shift, scale, gate = _mod_rows(mod_ref, i, 1, 0)
    x = x_scr[...]
    h = _rms(x, g_ref[...]) * (1.0 + scale) + shift
    z = _gelu(_bdot(h, win_ref[...]))
    w = z.shape[1] // 2
    u = z[:, :w]
    v = _rms(z[:, w:], vg_ref[...]).astype(BF16)
    gw = w // CM_GROUPS
    for c in range(TM // CM_CHUNK):
        r = slice(c * CM_CHUNK, (c + 1) * CM_CHUNK)
        for g in range(CM_GROUPS):
            cs = slice(g * gw, (g + 1) * gw)
            s = jnp.dot(ws_ref[g], v[r, cs], preferred_element_type=F32) + bs_ref[:, g:g + 1]
            p_scr[r, cs] = (u[r, cs] * s).astype(BF16)
    y = jnp.dot(p_scr[...], wout_ref[...], preferred_element_type=F32)
    o_ref[...] = x + gate * y


def _chunk_mlp_layer(x2, ctx2, mods, norm_g, w_in, v_g, w_s, b_s, w_out):
    seq, d = x2.shape
    n_ctx = ctx2.shape[0]
    assert n_ctx == TM and seq % TM == 0 and TM % GRID_W == 0
    n = n_ctx + seq
    w = w_out.shape[0]
    rt, ct = _pos_tables(seq, d)
    return pl.pallas_call(
        _cm_kernel,
        out_shape=jax.ShapeDtypeStruct((n, d), F32),
        grid=(n // TM,),
        in_specs=[pl.BlockSpec((TM, d), lambda i: (jnp.maximum(i - 1, 0), 0)),
                  _full((TM, d)), _full(rt.shape), _full(ct.shape),
                  _full(mods.shape), _full((1, d)), _full(w_in.shape), _full((1, w)),
                  _full(w_s.shape), _full((CM_CHUNK, CM_GROUPS)), _full(w_out.shape)],
        out_specs=pl.BlockSpec((TM, d), lambda i: (i, 0)),
        scratch_shapes=[pltpu.VMEM((TM, w), BF16), pltpu.VMEM((TM, d), F32)],
        compiler_params=_cparams("arbitrary"),
        name="chunk_mlp",
    )(x2, ctx2, rt, ct, mods, norm_g.reshape(1, d), w_in.astype(BF16), v_g.reshape(1, w),
      w_s.astype(BF16), b_s.T, w_out.astype(BF16))


def _store_token_tiles(ref, x):
    rows, d = x.shape
    for j in range(d // LANES):
        ref[pl.ds(j, rows, stride=d // LANES), :] = x[:, j * LANES:(j + 1) * LANES]


def _load_token_tiles(ref):
    chunks = SUBLANES
    rows = ref.shape[0] // chunks
    return jnp.concatenate([ref[pl.ds(j, rows, stride=chunks), :] for j in range(chunks)], axis=1)


def _route_tiles(nt):
    return next(k for k in (5, 4, 2, 1) if nt % k == 0)


def _router_kernel(s_ref, mod_ref, g_ref, rwt_ref, rbt_ref, tri_ref, h_ref, e1_ref, e2_ref, r1_ref,
                   r2_ref, wt_ref, cnt_ref, carry, *, ctx_rows):
    i = pl.program_id(0)
    rows, d = s_ref.shape

    @pl.when(i == 0)
    def _():
        carry[...] = jnp.zeros_like(carry)

    lat = mod_ref[1:2, :]
    shift, scale = lat[:, 3 * d:4 * d], lat[:, 4 * d:5 * d]
    if ctx_rows:
        ctx = mod_ref[0:1, :]
        is_ctx = (i == 0) & (lax.broadcasted_iota(jnp.int32, (rows, 1), 0) < ctx_rows)
        shift = jnp.where(is_ctx, ctx[:, 3 * d:4 * d], shift)
        scale = jnp.where(is_ctx, ctx[:, 4 * d:5 * d], scale)
    h = _rms(s_ref[...], g_ref[...]) * (1.0 + scale) + shift
    _store_token_tiles(h_ref, h)
    logits = _dot3(rwt_ref[...], h, ((1,), (1,))) + rbt_ref[...]
    row = lax.broadcasted_iota(jnp.int32, logits.shape, 0)
    neg = jnp.float32(-jnp.inf)
    big = jnp.int32(1 << 20)
    is_g = row < MOE_GROUPS
    gl = jnp.where(is_g, logits, neg)
    gmax = jnp.max(gl, axis=0, keepdims=True)
    grp = jnp.min(jnp.where(is_g & (gl == gmax), row, big), axis=0, keepdims=True)
    p_grp = 1.0 / jnp.sum(jnp.exp(gl - gmax), axis=0, keepdims=True)
    e_row = row - MOE_GROUPS
    in_grp = (e_row >= 0) & (e_row < MOE_EXPERTS) & ((e_row >> 3) == grp)
    l1 = jnp.where(in_grp, logits, neg)
    v1 = jnp.max(l1, axis=0, keepdims=True)
    i1 = jnp.min(jnp.where(in_grp & (l1 == v1), row, big), axis=0, keepdims=True)
    rest = in_grp & (row != i1)
    l2 = jnp.where(rest, logits, neg)
    v2 = jnp.max(l2, axis=0, keepdims=True)
    i2 = jnp.min(jnp.where(rest & (l2 == v2), row, big), axis=0, keepdims=True)
    e21 = jnp.exp(v2 - v1)
    w1 = p_grp / (1.0 + e21)
    w2 = p_grp * e21 / (1.0 + e21)
    oh1 = (row == i1).astype(F32)
    oh2 = (row == i2).astype(F32)
    oh = oh1 + oh2
    before = jnp.dot(oh.astype(BF16), tri_ref[...], preferred_element_type=F32) + carry[:, 0:1]
    r1_ref[...] = jnp.sum(oh1 * before, axis=0, keepdims=True).astype(jnp.int32)
    r2_ref[...] = jnp.sum(oh2 * before, axis=0, keepdims=True).astype(jnp.int32)
    e1_ref[...] = i1 - MOE_GROUPS
    e2_ref[...] = i2 - MOE_GROUPS
    carry[...] = carry[...] + jnp.sum(oh, axis=1, keepdims=True)
    cnt_ref[...] = carry[...]
    wt_ref[...] = jnp.concatenate([w1, w2, jnp.zeros((LANES - 2, rows), F32)], axis=0).T


def _router(s, mods, norm_g, rg_w, rg_b, re_w, re_b, ctx_tiles):
    n, d = s.shape
    rows = TM * _route_tiles(n // TM)
    steps = n // rows
    pad = ROUTE_ROWS - MOE_GROUPS - MOE_EXPERTS
    rwt = jnp.concatenate([rg_w, re_w, jnp.zeros((d, pad), F32)], axis=1).T
    rbt = jnp.broadcast_to(jnp.concatenate([rg_b, re_b, jnp.zeros((pad,), F32)])[:, None],
                           (ROUTE_ROWS, rows))
    tri = jnp.asarray(np.triu(np.ones((rows, rows), np.float32), 1), BF16)
    assert d == SUBLANES * LANES
    tile = pl.BlockSpec((rows, d), lambda i: (i, 0))
    irow = pl.BlockSpec((None, 1, rows), lambda i: (i, 0, 0))
    ishape = jax.ShapeDtypeStruct((steps, 1, rows), jnp.int32)
    return pl.pallas_call(
        functools.partial(_router_kernel, ctx_rows=ctx_tiles * TM),
        out_shape=(jax.ShapeDtypeStruct((n * SUBLANES, LANES), F32), ishape, ishape, ishape, ishape,
                   jax.ShapeDtypeStruct((n, LANES), F32),
                   jax.ShapeDtypeStruct((ROUTE_ROWS, LANES), F32)),
        grid=(steps,),
        in_specs=[tile, _full(mods.shape), _full((1, d)), _full((ROUTE_ROWS, d)),
                  _full((ROUTE_ROWS, rows)), _full((rows, rows))],
        out_specs=(pl.BlockSpec((rows * SUBLANES, LANES), lambda i: (i, 0)), irow, irow, irow, irow,
                   pl.BlockSpec((rows, LANES), lambda i: (i, 0)), _full((ROUTE_ROWS, LANES))),
        scratch_shapes=[pltpu.VMEM((ROUTE_ROWS, LANES), F32)],
        compiler_params=_cparams("arbitrary"),
        name="moe_router",
    )(s, mods, norm_g.reshape(1, d), rwt, rbt, tri)


def _finalize_kernel(cnt_ref, e1_ref, e2_ref, r1_ref, r2_ref, d1_ref, d2_ref, blk_ref):
    e1 = e1_ref[...]
    e2 = e2_ref[...]
    r1 = r1_ref[...]
    r2 = r2_ref[...]
    d1 = jnp.zeros_like(e1)
    d2 = jnp.zeros_like(e2)
    lane = lax.broadcasted_iota(jnp.int32, blk_ref.shape, 1)
    brow = lane * MOE_ROWS
    sub = lax.broadcasted_iota(jnp.int32, blk_ref.shape, 0)
    be = jnp.zeros(blk_ref.shape, jnp.int32)
    pend = jnp.zeros(blk_ref.shape, jnp.int32)
    ps = jnp.int32(0)
    for e in range(MOE_EXPERTS):
        c = cnt_ref[e]
        pe = ps + lax.shift_left(lax.shift_right_logical(c + (MOE_ROWS - 1), MOE_ROWS_LOG2),
                                 MOE_ROWS_LOG2)
        d1 = jnp.where(e1 == e, ps + r1, d1)
        d2 = jnp.where(e2 == e, ps + r2, d2)
        be = be + (brow >= pe).astype(jnp.int32)
        pend = jnp.where(lane == e, pe, pend)
        ps = pe
    d1_ref[...] = d1
    d2_ref[...] = d2
    n_used = lax.shift_right_logical(ps, MOE_ROWS_LOG2)
    blk_ref[...] = jnp.where(sub == 0, jnp.minimum(be, MOE_EXPERTS - 1),
                             jnp.where(sub == 1, pend, n_used))


def _finalize(counts, e1, e2, r1, r2, nb):
    nbp = (nb + LANES - 1) // LANES * LANES
    whole = pl.BlockSpec(e1.shape, lambda i, c: (0, 0, 0))
    ishape = jax.ShapeDtypeStruct(e1.shape, jnp.int32)
    return pl.pallas_call(
        _finalize_kernel,
        out_shape=(ishape, ishape, jax.ShapeDtypeStruct((SUBLANES, nbp), jnp.int32)),
        grid_spec=pltpu.PrefetchScalarGridSpec(
            num_scalar_prefetch=1,
            grid=(1,),
            in_specs=[whole, whole, whole, whole],
            out_specs=(whole, whole, pl.BlockSpec((SUBLANES, nbp), lambda i, c: (0, 0)))),
        compiler_params=_cparams("arbitrary"),
        name="moe_finalize",
    )(counts, e1, e2, r1, r2)


def _token_copy(src, r, dst, d, sem):
    return pltpu.make_async_copy(src.at[pl.ds(pl.multiple_of(r * SUBLANES, SUBLANES), SUBLANES), :],
                                 dst.at[pl.ds(pl.multiple_of(d * SUBLANES, SUBLANES), SUBLANES), :],
                                 sem)


def _zero_fill_padding(pend_ref, nu_ref, xs_out, zbuf, zsem):
    blk_rows = MOE_ROWS * SUBLANES
    nb = xs_out.shape[0] // blk_rows
    zbuf[...] = jnp.zeros_like(zbuf)

    def block_copy(b):
        r0 = pl.multiple_of(b * blk_rows, blk_rows)
        return pltpu.make_async_copy(zbuf, xs_out.at[pl.ds(r0, blk_rows), :], zsem)

    def seg_last_block(e):
        pe = pend_ref[e]
        prev = pend_ref[e - 1] if e > 0 else 0
        return pe > prev, lax.shift_right_logical(pe, MOE_ROWS_LOG2) - 1

    for e in range(MOE_EXPERTS):
        nonempty, b = seg_last_block(e)

        @pl.when(nonempty)
        def _():
            block_copy(b).start()

    def tail_start(b, c):
        block_copy(b).start()
        return c

    lax.fori_loop(nu_ref[0], nb, tail_start, 0)
    for e in range(MOE_EXPERTS):
        nonempty, b = seg_last_block(e)

        @pl.when(nonempty)
        def _():
            block_copy(b).wait()

    def tail_wait(b, c):
        block_copy(b).wait()
        return c

    lax.fori_loop(nu_ref[0], nb, tail_wait, 0)


def _dispatch_kernel(dest_ref, pend_ref, nu_ref, h_ref, xs_out, sem, zbuf, zsem, *, tiles):
    i = pl.program_id(0)

    @pl.when(i == 0)
    def _():
        _zero_fill_padding(pend_ref, nu_ref, xs_out, zbuf, zsem)

    rows = tiles * TM
    for q in range(tiles):
        base = i * (2 * rows) + q * TM

        def start(r, c):
            _token_copy(h_ref, q * TM + r, xs_out, dest_ref[base + r], sem).start(priority=0)
            _token_copy(h_ref, q * TM + r, xs_out, dest_ref[base + rows + r],
                        sem).start(priority=1)
            return c

        lax.fori_loop(0, TM, start, 0, unroll=8)
    for _ in range(2):
        pltpu.make_async_copy(h_ref, xs_out.at[pl.ds(0, tiles * TM * SUBLANES), :], sem).wait()


def _dispatch(dest, pad_end, n_used, h, n_rows):
    n = h.shape[0] // SUBLANES
    nt = n // TM
    tiles = _route_tiles(nt)
    return pl.pallas_call(
        functools.partial(_dispatch_kernel, tiles=tiles),
        out_shape=jax.ShapeDtypeStruct((n_rows * SUBLANES, LANES), F32),
        grid_spec=pltpu.PrefetchScalarGridSpec(
            num_scalar_prefetch=3,
            grid=(nt // tiles,),
            in_specs=[pl.BlockSpec((tiles * TM * SUBLANES, LANES), lambda i, *_: (i, 0))],
            out_specs=pl.BlockSpec(memory_space=pl.ANY),
            scratch_shapes=[pltpu.SemaphoreType.DMA, pltpu.VMEM((MOE_ROWS * SUBLANES, LANES), F32),
                            pltpu.SemaphoreType.DMA]),
        compiler_params=_cparams("arbitrary"),
        name="moe_dispatch",
    )(dest, pad_end, n_used, h)


X_SLOTS = 3


Y_SLOTS = 2


def _expert_kernel(be_ref, pend_ref, nu_ref, xs_hbm, wg_hbm, wu_hbm, wd_hbm, ys_hbm,
                   xbuf, ybuf, wg_f, wu_f, wd_f, wg_s, wu_s, wd_s, xsem, ysem, wsem, ord_ref,
                   *, layer):
    nu = nu_ref[0]
    blk_rows = MOE_ROWS * SUBLANES
    nb = ys_hbm.shape[0] // blk_rows

    def rows_of(ref, blk):
        return ref.at[pl.ds(pl.multiple_of(blk * blk_rows, blk_rows), blk_rows), :]

    def x_copy(blk, slot):
        return pltpu.make_async_copy(rows_of(xs_hbm, blk), xbuf.at[slot], xsem.at[slot])

    def y_copy(blk, slot):
        return pltpu.make_async_copy(ybuf.at[slot], rows_of(ys_hbm, blk), ysem.at[slot])

    def w_copies(e, slot):
        return (pltpu.make_async_copy(wg_hbm.at[layer, e], wg_f.at[slot], wsem.at[slot]),
                pltpu.make_async_copy(wu_hbm.at[layer, e], wu_f.at[slot], wsem.at[slot]),
                pltpu.make_async_copy(wd_hbm.at[layer, e], wd_f.at[slot], wsem.at[slot]))

    ord_ref[0] = 0
    for j in range(X_SLOTS - 1):
        @pl.when(j < nu)
        def _():
            x_copy(j, j).start()

    @pl.when(nu > 0)
    def _():
        for c in w_copies(be_ref[0], 0):
            c.start()

    def block(b, carry):
        ahead = b + (X_SLOTS - 1)

        @pl.when(ahead < nu)
        def _():
            x_copy(ahead, lax.rem(ahead, X_SLOTS)).start()

        e = be_ref[b]

        @pl.when((b == 0) | (e != be_ref[jnp.maximum(b - 1, 0)]))
        def _():
            k = ord_ref[0]
            slot = lax.rem(k, 2)
            for c in w_copies(e, slot):
                c.wait()
            wg_s[...] = wg_f[slot].astype(BF16)
            wu_s[...] = wu_f[slot].astype(BF16)
            wd_s[...] = wd_f[slot].astype(BF16)
            nxt = lax.shift_right_logical(pend_ref[e], MOE_ROWS_LOG2)

            @pl.when(nxt < nu)
            def _():
                for c in w_copies(be_ref[nxt], 1 - slot):
                    c.start(priority=1)

            ord_ref[0] = k + 1

        slot = lax.rem(b, X_SLOTS)
        yslot = lax.rem(b, Y_SLOTS)
        x_copy(b, slot).wait()

        @pl.when(b >= Y_SLOTS)
        def _():
            y_copy(b - Y_SLOTS, yslot).wait()

        x = _load_token_tiles(xbuf.at[slot]).astype(BF16)
        a = jnp.dot(x, wg_s[...], preferred_element_type=F32)
        u = jnp.dot(x, wu_s[...], preferred_element_type=F32)
        _store_token_tiles(ybuf.at[yslot], jnp.dot((_silu(a) * u).astype(BF16), wd_s[...],
                                                   preferred_element_type=F32))
        y_copy(b, yslot).start()
        return carry

    lax.fori_loop(0, nu, block, 0)
    for j in range(1, Y_SLOTS + 1):
        @pl.when(nu >= j)
        def _():
            y_copy(nu - j, lax.rem(nu - j, Y_SLOTS)).wait()

    ybuf[0] = jnp.zeros(ybuf.shape[1:], ybuf.dtype)

    def tail_start(b, c):
        y_copy(b, 0).start()
        return c

    def tail_wait(b, c):
        y_copy(b, 0).wait()
        return c

    lax.fori_loop(nu, nb, tail_start, 0)
    lax.fori_loop(nu, nb, tail_wait, 0)


def _experts(blk_expert, pad_end, n_used, xs, w_gate, w_up, w_down, layer):
    d, hid = w_gate.shape[2:]
    blk_rows = MOE_ROWS * SUBLANES
    hbm = pl.BlockSpec(memory_space=pl.ANY)
    return pl.pallas_call(
        functools.partial(_expert_kernel, layer=layer),
        out_shape=jax.ShapeDtypeStruct(xs.shape, F32),
        grid_spec=pltpu.PrefetchScalarGridSpec(
            num_scalar_prefetch=3,
            grid=(1,),
            in_specs=[hbm, hbm, hbm, hbm],
            out_specs=hbm,
            scratch_shapes=[pltpu.VMEM((X_SLOTS, blk_rows, LANES), F32),
                            pltpu.VMEM((Y_SLOTS, blk_rows, LANES), F32),
                            pltpu.VMEM((2, d, hid), F32), pltpu.VMEM((2, d, hid), F32),
                            pltpu.VMEM((2, hid, d), F32),
                            pltpu.VMEM((d, hid), BF16), pltpu.VMEM((d, hid), BF16),
                            pltpu.VMEM((hid, d), BF16),
                            pltpu.SemaphoreType.DMA((X_SLOTS,)), pltpu.SemaphoreType.DMA((Y_SLOTS,)),
                            pltpu.SemaphoreType.DMA((2,)), pltpu.SMEM((1,), jnp.int32)]),
        compiler_params=_cparams("arbitrary"),
        name="moe_experts",
    )(blk_expert, pad_end, n_used, xs, w_gate, w_up, w_down)


def _combined_tile(dest_ref, s_ref, wt_ref, mod_ref, ys_ref, ybuf, sem, *, ctx_tiles, tiles):
    i = pl.program_id(0)
    slot = i % 2
    rows = tiles * TM

    def gather(tile, slot):
        tile = jnp.asarray(tile, jnp.int32)
        base = lax.div(tile, tiles) * (2 * rows) + lax.rem(tile, tiles) * TM

        def start(r, c):
            _token_copy(ys_ref, dest_ref[base + r], ybuf.at[slot, 0], r,
                        sem.at[slot]).start(priority=0)
            _token_copy(ys_ref, dest_ref[base + rows + r], ybuf.at[slot, 1], r,
                        sem.at[slot]).start(priority=1)
            return c

        lax.fori_loop(0, TM, start, 0, unroll=8)

    @pl.when(i == 0)
    def _():
        gather(0, 0)

    @pl.when(i + 1 < pl.num_programs(0))
    def _():
        gather(i + 1, 1 - slot)

    for k in range(2):
        pltpu.make_async_copy(ys_ref.at[pl.ds(0, TM * SUBLANES), :], ybuf.at[slot, k],
                              sem.at[slot]).wait()
    gate = _mod_rows(mod_ref, i, ctx_tiles, 3)[2]
    wt = wt_ref[...]
    y = (wt[:, 0:1] * _load_token_tiles(ybuf.at[slot, 0])
         + wt[:, 1:2] * _load_token_tiles(ybuf.at[slot, 1]))
    return s_ref[...] + gate * y


def _combine_scratch():
    return [pltpu.VMEM((2, 2, TM * SUBLANES, LANES), F32), pltpu.SemaphoreType.DMA((2,))]


def _combine_kernel(dest_ref, s_ref, wt_ref, mod_ref, fg_ref, ys_ref, o_ref, ybuf, sem,
                    *, final_norm, **kw):
    out = _combined_tile(dest_ref, s_ref, wt_ref, mod_ref, ys_ref, ybuf, sem, **kw)
    if final_norm:
        out = _rms(out, fg_ref[...])
    o_ref[...] = out


def _combine(dest, s, wts, mods, final_g, ys, ctx_tiles, final_norm):
    n, d = s.shape
    return pl.pallas_call(
        functools.partial(_combine_kernel, ctx_tiles=ctx_tiles, final_norm=final_norm,
                          tiles=_route_tiles(n // TM)),
        out_shape=jax.ShapeDtypeStruct((n, d), F32),
        grid_spec=pltpu.PrefetchScalarGridSpec(
            num_scalar_prefetch=1,
            grid=(n // TM,),
            in_specs=[pl.BlockSpec((TM, d), lambda i, dst: (i, 0)),
                      pl.BlockSpec((TM, LANES), lambda i, dst: (i, 0)),
                      pl.BlockSpec(mods.shape, lambda i, dst: (0, 0)),
                      pl.BlockSpec((1, d), lambda i, dst: (0, 0)),
                      pl.BlockSpec(memory_space=pl.ANY)],
            out_specs=pl.BlockSpec((TM, d), lambda i, dst: (i, 0)),
            scratch_shapes=_combine_scratch()),
        compiler_params=_cparams("arbitrary"),
        name="moe_combine",
    )(dest, s, wts, mods, final_g.reshape(1, d), ys)


def _moe_layer(s, mods, norm_g, rg_w, rg_b, re_w, re_b, w_gate, w_up, w_down, layer, ctx_tiles,
               final_g, final_norm, defer=False):
    n, d = s.shape
    h, e1, e2, r1, r2, wts, cnt = _router(s, mods, norm_g, rg_w, rg_b, re_w, re_b, ctx_tiles)
    counts = cnt[MOE_GROUPS:MOE_GROUPS + MOE_EXPERTS, 0].astype(jnp.int32)
    nb = (2 * n + MOE_EXPERTS * (MOE_ROWS - 1)) // MOE_ROWS + 1
    d1, d2, blk = _finalize(counts, e1, e2, r1, r2, nb)
    dest = jnp.concatenate([d1, d2], axis=1).reshape(2 * n)
    n_used = blk[2, :1]
    pad_end = blk[1, :MOE_EXPERTS]
    xs = _dispatch(dest, pad_end, n_used, h, nb * MOE_ROWS)
    ys = _experts(blk[0, :nb], pad_end, n_used, xs, w_gate, w_up, w_down, layer)
    if defer:
        return _PendingCombine(dest, s, wts, mods, ys, ctx_tiles)
    return _combine(dest, s, wts, mods, final_g, ys, ctx_tiles, final_norm)


class _PendingCombine(NamedTuple):
    dest: jax.Array
    s: jax.Array
    wts: jax.Array
    mods: jax.Array
    ys: jax.Array
    ctx_tiles: int


def _pending_operands(p):
    n, d = p.s.shape
    kw = dict(ctx_tiles=p.ctx_tiles, tiles=_route_tiles(n // TM))
    specs = [pl.BlockSpec((TM, d), lambda i, dst: (i, 0)),
             pl.BlockSpec((TM, LANES), lambda i, dst: (i, 0)),
             pl.BlockSpec(p.mods.shape, lambda i, dst: (0, 0)),
             pl.BlockSpec(memory_space=pl.ANY)]
    return kw, p.dest, specs, (p.s, p.wts, p.mods, p.ys), _combine_scratch()


def _conv_tile(x, prev_ref, next_ref, has_prev, has_next, w_ref, b_ref):
    rows = x.shape[0]
    S = SUBLANES
    ridx = lax.broadcasted_iota(jnp.int32, (S, x.shape[1]), 0)
    pm = jnp.where(has_prev, 1.0, 0.0)
    nm = jnp.where(has_next, 1.0, 0.0)
    p2 = prev_ref[S - 2:S - 1, :] * pm
    p1 = prev_ref[S - 1:S, :] * pm
    n1 = next_ref[0:1, :] * nm

    def fix_head(rolled, head):
        return jnp.concatenate([head(rolled[:S]), rolled[S:]], axis=0)

    xm1 = fix_head(pltpu.roll(x, 1, axis=0), lambda g: jnp.where(ridx == 0, p1, g))
    xm2 = fix_head(pltpu.roll(x, 2, axis=0),
                   lambda g: jnp.where(ridx == 0, p2, jnp.where(ridx == 1, p1, g)))
    xp1 = pltpu.roll(x, rows - 1, axis=0)
    xp1 = jnp.concatenate([xp1[:rows - S], jnp.where(ridx == S - 1, n1, xp1[rows - S:])], axis=0)
    return (xm2 * w_ref[0:1, :] + xm1 * w_ref[1:2, :] + x * w_ref[2:3, :]
            + xp1 * w_ref[3:4, :] + b_ref[...])


def _ml_proj_kernel(dest_ref, s_ref, wt_ref, pmod_ref, ys_ref, mod_ref, g_ref, w_ref, wg_ref,
                    gb_ref, snew_ref, qk_ref, v_ref, o_ref, gt_ref, ybuf, sem, **pending_kw):
    i = pl.program_id(0)
    x = _combined_tile(dest_ref, s_ref, wt_ref, pmod_ref, ys_ref, ybuf, sem, **pending_kw)
    snew_ref[...] = x
    shift, scale, _ = _mod_rows(mod_ref, i, 1, 0)
    h = _rms(x, g_ref[...]) * (1.0 + scale) + shift
    z = _bdot(h, w_ref[...])
    nqk = qk_ref.shape[1]
    nv = v_ref.shape[1]
    qk_ref[...] = z[:, :nqk]
    v_ref[...] = z[:, nqk:nqk + nv].astype(BF16)
    o_ref[...] = z[:, nqk + nv:].astype(BF16)
    pre = _dot3(h, wg_ref[...], ((1,), (0,))) + gb_ref[...]
    lane = lax.broadcasted_iota(jnp.int32, pre.shape, 1)
    is_forget = ((lane >> 2) & 1) == 1
    gt_ref[...] = jnp.where(is_forget, -_softplus(-pre), pre)


def _ml_proj(pending, mods, norm_g, w_in, gate_b):
    n, d = pending.s.shape
    kw, dest, p_specs, p_args, p_scratch = _pending_operands(pending)
    nqk = 2 * ML_HEADS * ML_DK
    nv = ML_HEADS * ML_DV
    n_main = nqk + 2 * nv
    n_gate = w_in.shape[1] - n_main
    w_main = w_in[:, :n_main].astype(BF16)
    w_gate = jnp.concatenate([w_in[:, n_main:], jnp.zeros((d, LANES - n_gate), F32)], axis=1)
    gb = jnp.concatenate([gate_b.reshape(n_gate), jnp.zeros((LANES - n_gate,), F32)]).reshape(1, LANES)
    tile = lambda w: pl.BlockSpec((TM, w), lambda i, dst: (i, 0))
    whole = lambda shape: pl.BlockSpec(shape, lambda i, dst: (0,) * len(shape))
    return pl.pallas_call(
        functools.partial(_ml_proj_kernel, **kw),
        out_shape=(jax.ShapeDtypeStruct((n, d), F32),
                   jax.ShapeDtypeStruct((n, nqk), F32), jax.ShapeDtypeStruct((n, nv), BF16),
                   jax.ShapeDtypeStruct((n, nv), BF16), jax.ShapeDtypeStruct((n, LANES), F32)),
        grid_spec=pltpu.PrefetchScalarGridSpec(
            num_scalar_prefetch=1,
            grid=(n // TM,),
            in_specs=p_specs + [whole(mods.shape), whole((1, d)), whole(w_main.shape),
                                whole((d, LANES)), whole((1, LANES))],
            out_specs=(tile(d), tile(nqk), tile(nv), tile(nv), tile(LANES)),
            scratch_shapes=p_scratch),
        compiler_params=_cparams("arbitrary"),
        name="mlstm_proj",
    )(dest, *p_args, mods, norm_g.reshape(1, d), w_main, w_gate, gb)


def _ml_chunk_index(j, n_chunks, ctx_chunks, reverse):
    if not reverse:
        return j
    return jnp.where(j < ctx_chunks, ctx_chunks - 1 - j, n_chunks - 1 + ctx_chunks - j)


def _ml_rec_kernel(qk_ref, qkp_ref, qkn_ref, v_ref, gt_ref, gtt_ref, cw_ref, cb_ref, o_ref,
                   c_scr, n_scr, m_scr, *, reverse, n_chunks, ctx_chunks):
    j = pl.program_id(0)
    c = _ml_chunk_index(j, n_chunks, ctx_chunks, reverse)

    @pl.when(j == 0)
    def _():
        c_scr[...] = jnp.zeros_like(c_scr)
        n_scr[...] = jnp.zeros_like(n_scr)
        m_scr[...] = jnp.zeros_like(m_scr)

    has_prev = (c != 0) & (c != ctx_chunks)
    has_next = (c != ctx_chunks - 1) & (c != n_chunks - 1)
    qk = _silu(_conv_tile(qk_ref[...], qkp_ref, qkn_ref, has_prev, has_next, cw_ref, cb_ref))
    L = ML_CHUNK
    ri = lax.broadcasted_iota(jnp.int32, (L, L), 0)
    ci = lax.broadcasted_iota(jnp.int32, (L, L), 1)
    past = (ci >= ri) if reverse else (ci <= ri)
    pastf = past.astype(F32)
    gt = gt_ref[...]
    gtt = gtt_ref[...]
    b_col = jnp.dot(pastf, gt, precision=HI, preferred_element_type=F32)
    b_row = jnp.dot(gtt, pastf.T, precision=HI, preferred_element_type=F32)
    last = 0 if reverse else L - 1
    dbase = 8 if reverse else 0
    nq = ML_HEADS * ML_DK
    for hd in range(ML_HEADS):
        cl = dbase + hd
        cf = dbase + 4 + hd
        q = qk[:, hd * ML_DK:(hd + 1) * ML_DK] * (ML_DK ** -0.5)
        k = qk[:, nq + hd * ML_DK:nq + (hd + 1) * ML_DK]
        v = v_ref[:, hd * ML_DV:(hd + 1) * ML_DV]
        li_c = gt[:, cl:cl + 1]
        li_r = gtt[cl:cl + 1, :]
        b_c = b_col[:, cf:cf + 1]
        b_r = b_row[cf:cf + 1, :]
        g = b_r[:, last:last + 1]
        m0 = m_scr[hd:hd + 1, 0:1]
        c0 = c_scr[hd]
        n0 = n_scr[hd:hd + 1, :]
        a_c = g - b_c + li_c
        a_r = g - b_r + li_r
        m_loc = jnp.max(a_r, axis=-1, keepdims=True)
        inter = b_c + m0
        dlog = jnp.where(past, b_c - b_r + li_r, -jnp.inf)
        m = jnp.maximum(inter, jnp.max(dlog, axis=-1, keepdims=True))
        qb = q.astype(BF16)
        sc = lax.dot_general(qb, k.astype(BF16), (((1,), (1,)), ((), ())),
                             preferred_element_type=F32) * jnp.exp(dlog - m)
        w_inter = jnp.exp(inter - m)
        num = (jnp.dot(sc.astype(BF16), v, preferred_element_type=F32)
               + w_inter * jnp.dot(qb, c0.astype(BF16), preferred_element_type=F32))
        den = (jnp.sum(sc, axis=-1, keepdims=True)
               + w_inter * jnp.sum(q * n0, axis=-1, keepdims=True))
        o_ref[:, hd * ML_DV:(hd + 1) * ML_DV] = (
            num / jnp.maximum(jnp.abs(den), jnp.exp(-m))).astype(BF16)
        m_new = jnp.maximum(g + m0, m_loc)
        dec = jnp.exp(g + m0 - m_new)
        scl = jnp.exp(m_loc - m_new)
        kw = k * jnp.exp(a_c - m_loc)
        c_scr[hd] = dec * c0 + scl * jnp.dot(kw.T.astype(BF16), v, preferred_element_type=F32)
        n_scr[hd:hd + 1, :] = dec * n0 + scl * jnp.sum(kw, axis=0, keepdims=True)
        m_scr[hd:hd + 1, :] = jnp.broadcast_to(m_new, (1, LANES))


def _ml_rec(qk, v, gt, gtt, conv_w, conv_b, reverse):
    n, nqk = qk.shape
    nv = v.shape[1]
    L = ML_CHUNK
    nc = n // L
    cc = TM // L
    hb = L // SUBLANES
    idx = lambda j: _ml_chunk_index(j, nc, cc, reverse)
    last8 = n // SUBLANES - 1
    return pl.pallas_call(
        functools.partial(_ml_rec_kernel, reverse=reverse, n_chunks=nc, ctx_chunks=cc),
        out_shape=jax.ShapeDtypeStruct((n, nv), BF16),
        grid=(nc,),
        in_specs=[pl.BlockSpec((L, nqk), lambda j: (idx(j), 0)),
                  pl.BlockSpec((SUBLANES, nqk), lambda j: (jnp.maximum(idx(j) * hb - 1, 0), 0)),
                  pl.BlockSpec((SUBLANES, nqk), lambda j: (jnp.minimum((idx(j) + 1) * hb, last8), 0)),
                  pl.BlockSpec((L, nv), lambda j: (idx(j), 0)),
                  pl.BlockSpec((L, LANES), lambda j: (idx(j), 0)),
                  pl.BlockSpec((2 * SUBLANES, L), lambda j: (0, idx(j))),
                  _full((4, nqk)), _full((1, nqk))],
        out_specs=pl.BlockSpec((L, nv), lambda j: (idx(j), 0)),
        scratch_shapes=[pltpu.VMEM((ML_HEADS, ML_DK, ML_DV), F32),
                        pltpu.VMEM((SUBLANES, ML_DK), F32),
                        pltpu.VMEM((SUBLANES, LANES), F32)],
        compiler_params=_cparams("arbitrary"),
        name="mlstm_rev" if reverse else "mlstm_fwd",
    )(qk, qk, qk, v, gt, gtt, conv_w, conv_b.reshape(1, nqk))


def _ml_out_kernel(hf_ref, hr_ref, o_ref, s_ref, mod_ref, ng_ref, w_ref, out_ref, p_scr):
    i = pl.program_id(0)
    gate = _mod_rows(mod_ref, i, 1, 0)[2]
    hs = hf_ref[...].astype(F32) + hr_ref[...].astype(F32)
    sig = _sigmoid(o_ref[...].astype(F32))
    ng = ng_ref[...]
    for hd in range(ML_HEADS):
        cs = slice(hd * ML_DV, (hd + 1) * ML_DV)
        seg = hs[:, cs]
        hn = seg * lax.rsqrt(jnp.mean(seg * seg, axis=-1, keepdims=True) + EPS) * ng[:, cs]
        p_scr[:, cs] = (hn * sig[:, cs]).astype(BF16)
    y = jnp.dot(p_scr[...], w_ref[...], preferred_element_type=F32)
    out_ref[...] = s_ref[...] + gate * y


def _ml_out(hf, hr, o, s, mods, norm_g, w_out):
    n, d = s.shape
    nv = hf.shape[1]
    tile = lambda w: pl.BlockSpec((TM, w), lambda i: (i, 0))
    return pl.pallas_call(
        _ml_out_kernel,
        out_shape=jax.ShapeDtypeStruct((n, d), F32),
        grid=(n // TM,),
        in_specs=[tile(nv), tile(nv), tile(nv), tile(d), _full(mods.shape), _full((1, nv)),
                  _full(w_out.shape)],
        out_specs=tile(d),
        scratch_shapes=[pltpu.VMEM((TM, nv), BF16)],
        compiler_params=_cparams("arbitrary"),
        name="mlstm_out",
    )(hf, hr, o, s, mods, norm_g.reshape(1, nv), w_out.astype(BF16))


def _mlstm_layer(pending, mods, norm_g, w_in, conv_w, conv_b, gate_b, ml_norm_g, w_out):
    s, qk, v, o, gt = _ml_proj(pending, mods, norm_g, w_in, gate_b)
    gtt = gt[:, :2 * SUBLANES].T
    hf = _ml_rec(qk, v, gt, gtt, conv_w, conv_b, False)
    hr = _ml_rec(qk, v, gt, gtt, conv_w, conv_b, True)
    return _ml_out(hf, hr, o, s, mods, ml_norm_g, w_out)


def _lru_proj_kernel(dest_ref, s_ref, wt_ref, pmod_ref, ys_ref, mod_ref, g_ref, w_ref,
                     snew_ref, gl_ref, xr_ref, ybuf, sem, **pending_kw):
    i = pl.program_id(0)
    x = _combined_tile(dest_ref, s_ref, wt_ref, pmod_ref, ys_ref, ybuf, sem, **pending_kw)
    snew_ref[...] = x
    shift, scale, _ = _mod_rows(mod_ref, i, 1, 0)
    h = _rms(x, g_ref[...]) * (1.0 + scale) + shift
    z = _bdot(h, w_ref[...])
    w = gl_ref.shape[1]
    gl_ref[...] = _gelu(z[:, :w]).astype(BF16)
    xr_ref[...] = z[:, w:]


def _lru_proj(pending, mods, norm_g, w_in):
    n, d = pending.s.shape
    kw, dest, p_specs, p_args, p_scratch = _pending_operands(pending)
    w = w_in.shape[1] // 2
    tile = lambda c: pl.BlockSpec((TM, c), lambda i, dst: (i, 0))
    whole = lambda shape: pl.BlockSpec(shape, lambda i, dst: (0,) * len(shape))
    return pl.pallas_call(
        functools.partial(_lru_proj_kernel, **kw),
        out_shape=(jax.ShapeDtypeStruct((n, d), F32), jax.ShapeDtypeStruct((n, w), BF16),
                   jax.ShapeDtypeStruct((n, w), F32)),
        grid_spec=pltpu.PrefetchScalarGridSpec(
            num_scalar_prefetch=1,
            grid=(n // TM,),
            in_specs=p_specs + [whole(mods.shape), whole((1, d)), whole(w_in.shape)],
            out_specs=(tile(d), tile(w), tile(w)),
            scratch_shapes=p_scratch),
        compiler_params=_cparams("arbitrary"),
        name="rglru_proj",
    )(dest, *p_args, mods, norm_g.reshape(1, d), w_in.astype(BF16))


def _lru_tile_index(j, n_tiles, reverse):
    if not reverse:
        return j
    return jnp.where(j == 0, 0, n_tiles - j)


def _lru_scan_kernel(x_ref, xp_ref, xn_ref, cw_ref, cb_ref, wg_ref, ba_ref, bx_ref, lam_ref,
                     o_ref, a_scr, u_scr, carry, *, reverse, n_tiles):
    j = pl.program_id(0)
    t = _lru_tile_index(j, n_tiles, reverse)

    @pl.when(j == 0)
    def _():
        carry[...] = jnp.zeros_like(carry)

    has_prev = t > 1
    has_next = (t != 0) & (t != n_tiles - 1)
    xr = _conv_tile(x_ref[...], xp_ref, xn_ref, has_prev, has_next, cw_ref, cb_ref)
    sp = _softplus(-lam_ref[...])
    B = LRU_BLOCK
    for hd in range(LRU_HEADS):
        cs = slice(hd * B, (hd + 1) * B)
        xh = xr[:, cs]
        y = jnp.dot(xh.astype(BF16), wg_ref[hd], preferred_element_type=F32)
        r = _sigmoid(y[:, :B] + ba_ref[:, cs])
        ig = _sigmoid(y[:, B:] + bx_ref[:, cs])
        log_a = -LRU_C * r * sp[:, cs]
        a = jnp.exp(log_a)
        a_scr[:, cs] = a
        v = 1.0 - a * a
        u_scr[:, cs] = jnp.where(v > 0.0, v * lax.rsqrt(v), 0.0) * (ig * xh)

    S = SUBLANES
    w = a_scr.shape[1]
    sidx = lax.broadcasted_iota(jnp.int32, (S, w), 0)

    def group(gi, c):
        g = (TM // S - 1 - gi) if reverse else gi
        r0 = pl.multiple_of(g * S, S)
        a = a_scr[pl.ds(r0, S), :]
        u = u_scr[pl.ds(r0, S), :]
        for sft in (1, 2, 4):
            if reverse:
                ok = sidx < S - sft
                a_e = pltpu.roll(a, S - sft, axis=0)
                u_e = pltpu.roll(u, S - sft, axis=0)
            else:
                ok = sidx >= sft
                a_e = pltpu.roll(a, sft, axis=0)
                u_e = pltpu.roll(u, sft, axis=0)
            u = jnp.where(ok, a * u_e + u, u)
            a = jnp.where(ok, a * a_e, a)
        hcur = a * carry[...] + u
        u_scr[pl.ds(r0, S), :] = hcur
        edge = 0 if reverse else S - 1
        carry[...] = jnp.broadcast_to(hcur[edge:edge + 1, :], (S, w))
        return c

    lax.fori_loop(0, TM // S, group, 0)
    o_ref[...] = u_scr[...].astype(BF16)


def _lru_scan(xraw, conv_w, conv_b, w_a, b_a, w_x, b_x, lam, reverse):
    n, w = xraw.shape
    nt = n // TM
    hb = TM // SUBLANES
    idx = lambda j: _lru_tile_index(j, nt, reverse)
    last8 = n // SUBLANES - 1
    wg = jnp.concatenate([w_a, w_x], axis=-1).astype(BF16)
    return pl.pallas_call(
        functools.partial(_lru_scan_kernel, reverse=reverse, n_tiles=nt),
        out_shape=jax.ShapeDtypeStruct((n, w), BF16),
        grid=(nt,),
        in_specs=[pl.BlockSpec((TM, w), lambda j: (idx(j), 0)),
                  pl.BlockSpec((SUBLANES, w), lambda j: (jnp.maximum(idx(j) * hb - 1, 0), 0)),
                  pl.BlockSpec((SUBLANES, w), lambda j: (jnp.minimum((idx(j) + 1) * hb, last8), 0)),
                  _full((4, w)), _full((1, w)), _full(wg.shape), _full((1, w)), _full((1, w)),
                  _full((1, w))],
        out_specs=pl.BlockSpec((TM, w), lambda j: (idx(j), 0)),
        scratch_shapes=[pltpu.VMEM((TM, w), F32), pltpu.VMEM((TM, w), F32),
                        pltpu.VMEM((SUBLANES, w), F32)],
        compiler_params=_cparams("arbitrary"),
        name="rglru_rev" if reverse else "rglru_fwd",
    )(xraw, xraw, xraw, conv_w, conv_b.reshape(1, w), wg, b_a.reshape(1, w), b_x.reshape(1, w),
      lam.reshape(1, w))


def _lru_out_kernel(gl_ref, hf_ref, hr_ref, s_ref, mod_ref, w_ref, out_ref):
    gate = _mod_rows(mod_ref, 1, 0, 0)[2]
    p = gl_ref[...].astype(F32) * (hf_ref[...].astype(F32) + hr_ref[...].astype(F32))
    out_ref[...] = s_ref[...] + gate * _bdot(p, w_ref[...])


def _lru_out(gl, hf, hr, s, mods, w_out):
    n, d = s.shape
    w = gl.shape[1]
    lat = lambda c: pl.BlockSpec((TM, c), lambda i: (i + 1, 0))
    return pl.pallas_call(
        _lru_out_kernel,
        out_shape=jax.ShapeDtypeStruct((n - TM, d), F32),
        grid=(n // TM - 1,),
        in_specs=[lat(w), lat(w), lat(w), lat(d), _full(mods.shape), _full(w_out.shape)],
        out_specs=pl.BlockSpec((TM, d), lambda i: (i, 0)),
        compiler_params=_cparams("arbitrary"),
        name="rglru_out",
    )(gl, hf, hr, s, mods, w_out.astype(BF16))


def _rglru_layer(pending, mods, norm_g, w_in, conv_w, conv_b, w_a, b_a, w_x, b_x, lam, w_out):
    s, gl, xraw = _lru_proj(pending, mods, norm_g, w_in)
    hf = _lru_scan(xraw, conv_w, conv_b, w_a[0], b_a[0], w_x[0], b_x[0], lam[0], False)
    hr = _lru_scan(xraw, conv_w, conv_b, w_a[1], b_a[1], w_x[1], b_x[1], lam[1], True)
    return _lru_out(gl, hf, hr, s, mods, w_out)


def _fn_proj_kernel(s_ref, mod_ref, g_ref, wt_ref, cs_ref, yr_ref, yi_ref, ar_scr, ai_scr):
    shift, scale, _ = _mod_rows(mod_ref, 1, 0, 0)
    nm = wt_ref.shape[0]
    gw = nm // FN_GROUPS
    per = TM // FFT_N2
    nj = FN_TB // FFT_N2
    csb = cs_ref[...].astype(BF16)

    def sub(tc, c):
        r0 = pl.multiple_of(tc * TM, TM)
        h = _rms(s_ref[pl.ds(r0, TM), :], g_ref[...]) * (1.0 + scale) + shift
        zt = lax.dot_general(wt_ref[...], h.astype(BF16), (((1,), (1,)), ((), ())),
                             preferred_element_type=F32).astype(BF16)
        for g in range(FN_GROUPS):
            y = jnp.dot(csb, zt[g * gw:(g + 1) * gw, :], preferred_element_type=F32)
            for q in range(per):
                row0 = pl.multiple_of((tc * per + q) * _slab_pitch(nm) + g * gw, SUBLANES)
                ar_scr[pl.ds(row0, gw), :] = y[:gw, q * FFT_N2:(q + 1) * FFT_N2]
                ai_scr[pl.ds(row0, gw), :] = y[gw:, q * FFT_N2:(q + 1) * FFT_N2]
        return c

    lax.fori_loop(0, FN_TB // TM, sub, 0)

    def relayout(m, c):
        yr_ref[m] = ar_scr[pl.ds(m, nj, stride=_slab_pitch(nm)), :]
        yi_ref[m] = ai_scr[pl.ds(m, nj, stride=_slab_pitch(nm)), :]
        return c

    lax.fori_loop(0, nm, relayout, 0, unroll=8)


def _slab_pitch(rows):
    return rows + SUBLANES


def _dft_cos_sin(n, scale):
    k = np.arange(n, dtype=np.int64)
    ang = 2.0 * np.pi * ((k[:, None] * k[None, :]) % n).astype(np.float64) / n
    return np.cos(ang) * scale, np.sin(ang) * scale


def _fn_proj(s, mods, norm_g, w_in):
    t, d = s.shape
    nm = w_in.shape[1]
    gw = nm // FN_GROUPS
    n1 = t // FFT_N2
    nj = FN_TB // FFT_N2
    c, sn = _dft_cos_sin(gw, gw ** -0.5)
    cs = jnp.asarray(np.concatenate([c, -sn], axis=0), F32)
    yspec = pl.BlockSpec((nm, nj, FFT_N2), lambda i: (0, i, 0))
    yshape = jax.ShapeDtypeStruct((nm, n1, FFT_N2), F32)
    return pl.pallas_call(
        _fn_proj_kernel,
        out_shape=(yshape, yshape),
        grid=(t // FN_TB,),
        in_specs=[pl.BlockSpec((FN_TB, d), lambda i: (i, 0)), _full(mods.shape), _full((1, d)),
                  _full((nm, d)), _full(cs.shape)],
        out_specs=(yspec, yspec),
        scratch_shapes=[pltpu.VMEM((nj * _slab_pitch(nm), FFT_N2), F32),
                        pltpu.VMEM((nj * _slab_pitch(nm), FFT_N2), F32)],
        compiler_params=_cparams("arbitrary"),
        name="fourier_proj",
    )(s, mods, norm_g.reshape(1, d), w_in.T.astype(BF16), cs)


def _fn_fft_kernel(yr_ref, yi_ref, m_ref, tc_ref, ts_ref, d_ref, o_ref):
    n1 = yr_ref.shape[1]
    n2 = FFT_N2
    xr = jnp.concatenate([yr_ref[m].astype(BF16) for m in range(FN_CB)], axis=1)
    xi = jnp.concatenate([yi_ref[m].astype(BF16) for m in range(FN_CB)], axis=1)
    a = jnp.dot(m_ref[...].astype(BF16), jnp.concatenate([xr, xi], axis=0),
                preferred_element_type=F32)
    ar = a[:n1]
    ai = a[n1:]
    tc = jnp.concatenate([tc_ref[...]] * FN_CB, axis=1)
    ts = jnp.concatenate([ts_ref[...]] * FN_CB, axis=1)
    br = ar * tc + ai * ts
    bi = ai * tc - ar * ts
    bst = jnp.concatenate(
        [jnp.concatenate([br[:, m * n2:(m + 1) * n2], bi[:, m * n2:(m + 1) * n2]], axis=1)
         for m in range(FN_CB)], axis=0).astype(BF16)
    res = lax.dot_general(d_ref[...].astype(BF16), bst, (((1,), (1,)), ((), ())),
                          preferred_element_type=F32)
    for m in range(FN_CB):
        o_ref[m] = res[:, m * n1:(m + 1) * n1]


def _fn_fft(yr, yi):
    nm, n1, n2 = yr.shape
    t = n1 * n2
    c, sn = _dft_cos_sin(n1, n1 ** -0.5)
    m = jnp.asarray(np.block([[c, sn], [-sn, c]]), F32)
    k1 = np.arange(n1, dtype=np.int64)[:, None]
    t2 = np.arange(n2, dtype=np.int64)[None, :]
    ang = 2.0 * np.pi * ((k1 * t2) % t).astype(np.float64) / t
    tc = jnp.asarray(np.cos(ang), F32)
    ts = jnp.asarray(np.sin(ang), F32)
    c2, s2 = _dft_cos_sin(n2, n2 ** -0.5)
    dm = jnp.asarray(np.concatenate([c2, s2], axis=1), F32)
    yspec = pl.BlockSpec((FN_CB, n1, n2), lambda i: (i, 0, 0))
    return pl.pallas_call(
        _fn_fft_kernel,
        out_shape=jax.ShapeDtypeStruct((nm, n2, n1), F32),
        grid=(nm // FN_CB,),
        in_specs=[yspec, yspec, _full(m.shape), _full(tc.shape), _full(ts.shape), _full(dm.shape)],
        out_specs=pl.BlockSpec((FN_CB, n2, n1), lambda i: (i, 0, 0)),
        compiler_params=_cparams("arbitrary"),
        name="fourier_fft",
    )(yr, yi, m, tc, ts, dm)


def _fn_out_kernel(ft_ref, w_ref, s_ref, mod_ref, o_ref, a_scr):
    gate = _mod_rows(mod_ref, 1, 0, 0)[2]
    nm, nj, n1 = ft_ref.shape

    def relayout(m, c):
        a_scr[pl.ds(m, nj, stride=_slab_pitch(nm)), :] = ft_ref[m]
        return c

    lax.fori_loop(0, nm, relayout, 0, unroll=8)
    for j in range(nj):
        p0 = j * _slab_pitch(nm)
        slab = a_scr[p0:p0 + nm, :].astype(BF16)
        y = lax.dot_general(slab, w_ref[...], (((0,), (0,)), ((), ())), preferred_element_type=F32)
        rows = slice(j * n1, (j + 1) * n1)
        o_ref[rows, :] = s_ref[rows, :] + gate * y


def _fn_out(ft, s, mods, w_out):
    t, d = s.shape
    nm, n2, n1 = ft.shape
    nj = FN_TB // n1
    tok = pl.BlockSpec((FN_TB, d), lambda i: (i, 0))
    return pl.pallas_call(
        _fn_out_kernel,
        out_shape=jax.ShapeDtypeStruct((t, d), F32),
        grid=(t // FN_TB,),
        in_specs=[pl.BlockSpec((nm, nj, n1), lambda i: (0, i, 0)), _full(w_out.shape), tok,
                  _full(mods.shape)],
        out_specs=tok,
        scratch_shapes=[pltpu.VMEM((nj * _slab_pitch(nm), n1), F32)],
        compiler_params=_cparams("arbitrary"),
        name="fourier_out",
    )(ft, w_out.astype(BF16), s, mods)


def _fourier_layer(s, mods, norm_g, w_in, w_out):
    yr, yi = _fn_proj(s, mods, norm_g, w_in)
    return _fn_out(_fn_fft(yr, yi), s, mods, w_out)


def kernel(x, c, ctx, c_ctx, ada_w, ada_b, norm_mix_g, norm_ffn_g, final_norm_g, router_group_w, router_group_b, router_expert_w, router_expert_b, expert_w_gate, expert_w_up, expert_w_down, cm_w_in, cm_v_norm_g, cm_w_s, cm_b_s, cm_w_out, ml_w_in, ml_conv_w, ml_conv_b, ml_gate_b, ml_norm_g, ml_w_out, lru_w_in, lru_conv_w, lru_conv_b, lru_w_a, lru_b_a, lru_w_x, lru_b_x, lru_lambda, lru_w_out, fn_w_in, fn_w_out):
    bsz, seq, d = x.shape
    assert bsz == 1 and ada_w.shape[0] == 4 and ctx.shape[1] == TM
    c_rows = jnp.concatenate([c_ctx[None, :], c, jnp.zeros((SUBLANES - 2, d), F32)], axis=0)
    mods = _ada_table(c_rows, ada_w, ada_b)

    def moe(s, i, ctx_tiles, final_norm=False, defer=False):
        return _moe_layer(s, mods[i], norm_ffn_g[i], router_group_w[i], router_group_b[i],
                          router_expert_w[i], router_expert_b[i], expert_w_gate, expert_w_up,
                          expert_w_down, i, ctx_tiles, final_norm_g, final_norm, defer)

    s = _chunk_mlp_layer(x[0], ctx[0], mods[0], norm_mix_g[0], cm_w_in[0], cm_v_norm_g[0],
                         cm_w_s[0], cm_b_s[0], cm_w_out[0])
    s = moe(s, 0, 1, defer=True)
    s = _mlstm_layer(s, mods[1], norm_mix_g[1], ml_w_in[0], ml_conv_w[0], ml_conv_b[0],
                     ml_gate_b[0], ml_norm_g[0], ml_w_out[0])
    s = moe(s, 1, 1, defer=True)
    s = _rglru_layer(s, mods[2], norm_mix_g[2], lru_w_in[0], lru_conv_w[0], lru_conv_b[0],
                     lru_w_a[0], lru_b_a[0], lru_w_x[0], lru_b_x[0], lru_lambda[0], lru_w_out[0])
    s = moe(s, 2, 0)
    s = _fourier_layer(s, mods[3], norm_mix_g[3], fn_w_in[0], fn_w_out[0])
    s = moe(s, 3, 0, final_norm=True)
    return s[None]
```

```python
import functools
import math
from typing import NamedTuple

import jax
import jax.numpy as jnp
import numpy as np
from jax import lax
from jax.experimental import pallas as pl
from jax.experimental.pallas import tpu as pltpu

F32 = jnp.float32
BF16 = jnp.bfloat16

EPS = 1e-6
POS_BASE = 10000.0
GRID_W = 64
N_MOD = 6
TM = 256
LANES = 128
SUBLANES = 8
VMEM_LIMIT = 56 * 1024 * 1024

CM_CHUNK = 128
CM_GROUPS = 4
ML_HEADS = 4
ML_DK = 128
ML_DV = 256
ML_CHUNK = 128
LRU_HEADS = 10
LRU_BLOCK = 128
LRU_C = 8.0
FN_GROUPS = 4
FFT_N2 = 128
MOE_GROUPS = 4
MOE_EPG = 8
MOE_EXPERTS = MOE_GROUPS * MOE_EPG
MOE_ROWS_LOG2 = 9
MOE_ROWS = 1 << MOE_ROWS_LOG2
ROUTE_ROWS = 40
FN_TB = 1024
FN_CB = 16
CONV_LEFT = 2

HI = lax.Precision.HIGHEST


def _cparams(*sem):
    return pltpu.CompilerParams(dimension_semantics=sem, vmem_limit_bytes=VMEM_LIMIT)


def _full(shape):
    nd = len(shape)
    return pl.BlockSpec(shape, lambda *_: (0,) * nd)


def _rms(x, g):
    return x * lax.rsqrt(jnp.mean(x * x, axis=-1, keepdims=True) + EPS) * g


def _gelu(x):
    c = math.sqrt(2.0 / math.pi)
    return 0.5 * x * (1.0 + jnp.tanh(c * (x + 0.044715 * (x * x * x))))


def _sigmoid(x):
    return 0.5 * jnp.tanh(0.5 * x) + 0.5


def _silu(x):
    return x * _sigmoid(x)


def _softplus(x):
    return jnp.maximum(x, 0.0) + jnp.log(1.0 + jnp.exp(-jnp.abs(x)))


def _mod_rows(mod_ref, tile, ctx_tiles, first):
    row = jnp.where(tile < ctx_tiles, 0, 1)
    m = mod_ref[pl.ds(row, 1), :]
    d = m.shape[1] // N_MOD
    return tuple(m[:, (first + j) * d:(first + j + 1) * d] for j in range(3))


def _bdot(a, b):
    return jnp.dot(a.astype(BF16), b.astype(BF16), preferred_element_type=F32)


def _split_bf16(x):
    hi = x.astype(BF16)
    return hi, (x - hi.astype(F32)).astype(BF16)


def _dot3(a, b, dims):
    a_hi, a_lo = _split_bf16(a)
    b_hi, b_lo = _split_bf16(b)
    dg = functools.partial(lax.dot_general, dimension_numbers=(dims, ((), ())),
                           preferred_element_type=F32)
    return dg(a_hi, b_hi) + dg(a_hi, b_lo) + dg(a_lo, b_hi)


def _ada_kernel(c_ref, w_ref, b_ref, o_ref):
    c = c_ref[...]
    o_ref[...] = _dot3(_silu(c), w_ref[...], ((1,), (0,))) + b_ref[...]


def _ada_table(c_rows, ada_w, ada_b):
    depth, d, n = ada_w.shape
    tn = 2048
    return pl.pallas_call(
        _ada_kernel,
        out_shape=jax.ShapeDtypeStruct((depth, SUBLANES, n), F32),
        grid=(depth, n // tn),
        in_specs=[_full((SUBLANES, d)),
                  pl.BlockSpec((None, d, tn), lambda i, j: (i, 0, j)),
                  pl.BlockSpec((None, 1, tn), lambda i, j: (i, 0, j))],
        out_specs=pl.BlockSpec((None, SUBLANES, tn), lambda i, j: (i, 0, j)),
        compiler_params=_cparams("arbitrary", "arbitrary"),
        name="ada_table",
    )(c_rows, ada_w, ada_b.reshape(depth, 1, n))


def _pos_tables(seq, d):
    q = d // 4
    freq = jnp.exp(-math.log(POS_BASE) * jnp.arange(q, dtype=F32) / q)
    ar = jnp.arange(seq // GRID_W, dtype=F32)[:, None] * freq
    ac = jnp.arange(GRID_W, dtype=F32)[:, None] * freq
    return (jnp.concatenate([jnp.sin(ar), jnp.cos(ar)], axis=-1),
            jnp.concatenate([jnp.sin(ac), jnp.cos(ac)], axis=-1))


def _cm_kernel(x_ref, ctx_ref, rt_ref, ct_ref, mod_ref, g_ref, win_ref, vg_ref, ws_ref, bs_ref,
               wout_ref, o_ref, p_scr, x_scr):
    i = pl.program_id(0)

    @pl.when(i == 0)
    def _():
        x_scr[...] = ctx_ref[...]

    @pl.when(i > 0)
    def _():
        rows_per_tile = TM // GRID_W
        q2 = rt_ref.shape[1]
        r0 = (i - 1) * rows_per_tile
        rt = jnp.concatenate(
            [jnp.broadcast_to(rt_ref[pl.ds(r0 + j, 1), :], (GRID_W, q2))
             for j in range(rows_per_tile)], axis=0)
        ct = jnp.concatenate([ct_ref[...]] * rows_per_tile, axis=0)
        x_scr[...] = x_ref[...] + jnp.concatenate([rt, ct], axis=1)

    shift, scale, gate = _mod_rows(mod_ref, i, 1, 0)
    x = x_scr[...]
    h = _rms(x, g_ref[...]) * (1.0 + scale) + shift
    z = _gelu(_bdot(h, win_ref[...]))
    w = z.shape[1] // 2
    u = z[:, :w]
    v = _rms(z[:, w:], vg_ref[...]).astype(BF16)
    gw = w // CM_GROUPS
    for c in range(TM // CM_CHUNK):
        r = slice(c * CM_CHUNK, (c + 1) * CM_CHUNK)
        for g in range(CM_GROUPS):
            cs = slice(g * gw, (g + 1) * gw)
            s = jnp.dot(ws_ref[g], v[r, cs], preferred_element_type=F32) + bs_ref[:, g:g + 1]
            p_scr[r, cs] = (u[r, cs] * s).astype(BF16)
    y = jnp.dot(p_scr[...], wout_ref[...], preferred_element_type=F32)
    o_ref[...] = x + gate * y


def _chunk_mlp_layer(x2, ctx2, mods, norm_g, w_in, v_g, w_s, b_s, w_out):
    seq, d = x2.shape
    n_ctx = ctx2.shape[0]
    assert n_ctx == TM and seq % TM == 0 and TM % GRID_W == 0
    n = n_ctx + seq
    w = w_out.shape[0]
    rt, ct = _pos_tables(seq, d)
    return pl.pallas_call(
        _cm_kernel,
        out_shape=jax.ShapeDtypeStruct((n, d), F32),
        grid=(n // TM,),
        in_specs=[pl.BlockSpec((TM, d), lambda i: (jnp.maximum(i - 1, 0), 0)),
                  _full((TM, d)), _full(rt.shape), _full(ct.shape),
                  _full(mods.shape), _full((1, d)), _full(w_in.shape), _full((1, w)),
                  _full(w_s.shape), _full((CM_CHUNK, CM_GROUPS)), _full(w_out.shape)],
        out_specs=pl.BlockSpec((TM, d), lambda i: (i, 0)),
        scratch_shapes=[pltpu.VMEM((TM, w), BF16), pltpu.VMEM((TM, d), F32)],
        compiler_params=_cparams("arbitrary"),
        name="chunk_mlp",
    )(x2, ctx2, rt, ct, mods, norm_g.reshape(1, d), w_in.astype(BF16), v_g.reshape(1, w),
      w_s.astype(BF16), b_s.T, w_out.astype(BF16))


def _store_token_tiles(ref, x):
    rows, d = x.shape
    for j in range(d // LANES):
        ref[pl.ds(j, rows, stride=d // LANES), :] = x[:, j * LANES:(j + 1) * LANES]


def _load_token_tiles(ref):
    chunks = SUBLANES
    rows = ref.shape[0] // chunks
    return jnp.concatenate([ref[pl.ds(j, rows, stride=chunks), :] for j in range(chunks)], axis=1)


def _route_tiles(nt):
    return next(k for k in (5, 4, 2, 1) if nt % k == 0)


def _router_kernel(s_ref, mod_ref, g_ref, rwt_ref, rbt_ref, tri_ref, h_ref, e1_ref, e2_ref, r1_ref,
                   r2_ref, wt_ref, cnt_ref, carry, *, ctx_rows):
    i = pl.program_id(0)
    rows, d = s_ref.shape

    @pl.when(i == 0)
    def _():
        carry[...] = jnp.zeros_like(carry)

    lat = mod_ref[1:2, :]
    shift, scale = lat[:, 3 * d:4 * d], lat[:, 4 * d:5 * d]
    if ctx_rows:
        ctx = mod_ref[0:1, :]
        is_ctx = (i == 0) & (lax.broadcasted_iota(jnp.int32, (rows, 1), 0) < ctx_rows)
        shift = jnp.where(is_ctx, ctx[:, 3 * d:4 * d], shift)
        scale = jnp.where(is_ctx, ctx[:, 4 * d:5 * d], scale)
    h = _rms(s_ref[...], g_ref[...]) * (1.0 + scale) + shift
    _store_token_tiles(h_ref, h)
    logits = _dot3(rwt_ref[...], h, ((1,), (1,))) + rbt_ref[...]
    row = lax.broadcasted_iota(jnp.int32, logits.shape, 0)
    neg = jnp.float32(-jnp.inf)
    big = jnp.int32(1 << 20)
    is_g = row < MOE_GROUPS
    gl = jnp.where(is_g, logits, neg)
    gmax = jnp.max(gl, axis=0, keepdims=True)
    grp = jnp.min(jnp.where(is_g & (gl == gmax), row, big), axis=0, keepdims=True)
    p_grp = 1.0 / jnp.sum(jnp.exp(gl - gmax), axis=0, keepdims=True)
    e_row = row - MOE_GROUPS
    in_grp = (e_row >= 0) & (e_row < MOE_EXPERTS) & ((e_row >> 3) == grp)
    l1 = jnp.where(in_grp, logits, neg)
    v1 = jnp.max(l1, axis=0, keepdims=True)
    i1 = jnp.min(jnp.where(in_grp & (l1 == v1), row, big), axis=0, keepdims=True)
    rest = in_grp & (row != i1)
    l2 = jnp.where(rest, logits, neg)
    v2 = jnp.max(l2, axis=0, keepdims=True)
    i2 = jnp.min(jnp.where(rest & (l2 == v2), row, big), axis=0, keepdims=True)
    e21 = jnp.exp(v2 - v1)
    w1 = p_grp / (1.0 + e21)
    w2 = p_grp * e21 / (1.0 + e21)
    oh1 = (row == i1).astype(F32)
    oh2 = (row == i2).astype(F32)
    oh = oh1 + oh2
    before = jnp.dot(oh.astype(BF16), tri_ref[...], preferred_element_type=F32) + carry[:, 0:1]
    r1_ref[...] = jnp.sum(oh1 * before, axis=0, keepdims=True).astype(jnp.int32)
    r2_ref[...] = jnp.sum(oh2 * before, axis=0, keepdims=True).astype(jnp.int32)
    e1_ref[...] = i1 - MOE_GROUPS
    e2_ref[...] = i2 - MOE_GROUPS
    carry[...] = carry[...] + jnp.sum(oh, axis=1, keepdims=True)
    cnt_ref[...] = carry[...]
    wt_ref[...] = jnp.concatenate([w1, w2, jnp.zeros((LANES - 2, rows), F32)], axis=0).T


def _router(s, mods, norm_g, rg_w, rg_b, re_w, re_b, ctx_tiles):
    n, d = s.shape
    rows = TM * _route_tiles(n // TM)
    steps = n // rows
    pad = ROUTE_ROWS - MOE_GROUPS - MOE_EXPERTS
    rwt = jnp.concatenate([rg_w, re_w, jnp.zeros((d, pad), F32)], axis=1).T
    rbt = jnp.broadcast_to(jnp.concatenate([rg_b, re_b, jnp.zeros((pad,), F32)])[:, None],
                           (ROUTE_ROWS, rows))
    tri = jnp.asarray(np.triu(np.ones((rows, rows), np.float32), 1), BF16)
    assert d == SUBLANES * LANES
    tile = pl.BlockSpec((rows, d), lambda i: (i, 0))
    irow = pl.BlockSpec((None, 1, rows), lambda i: (i, 0, 0))
    ishape = jax.ShapeDtypeStruct((steps, 1, rows), jnp.int32)
    return pl.pallas_call(
        functools.partial(_router_kernel, ctx_rows=ctx_tiles * TM),
        out_shape=(jax.ShapeDtypeStruct((n * SUBLANES, LANES), F32), ishape, ishape, ishape, ishape,
                   jax.ShapeDtypeStruct((n, LANES), F32),
                   jax.ShapeDtypeStruct((ROUTE_ROWS, LANES), F32)),
        grid=(steps,),
        in_specs=[tile, _full(mods.shape), _full((1, d)), _full((ROUTE_ROWS, d)),
                  _full((ROUTE_ROWS, rows)), _full((rows, rows))],
        out_specs=(pl.BlockSpec((rows * SUBLANES, LANES), lambda i: (i, 0)), irow, irow, irow, irow,
                   pl.BlockSpec((rows, LANES), lambda i: (i, 0)), _full((ROUTE_ROWS, LANES))),
        scratch_shapes=[pltpu.VMEM((ROUTE_ROWS, LANES), F32)],
        compiler_params=_cparams("arbitrary"),
        name="moe_router",
    )(s, mods, norm_g.reshape(1, d), rwt, rbt, tri)


def _finalize_kernel(cnt_ref, e1_ref, e2_ref, r1_ref, r2_ref, d1_ref, d2_ref, blk_ref):
    e1 = e1_ref[...]
    e2 = e2_ref[...]
    r1 = r1_ref[...]
    r2 = r2_ref[...]
    d1 = jnp.zeros_like(e1)
    d2 = jnp.zeros_like(e2)
    lane = lax.broadcasted_iota(jnp.int32, blk_ref.shape, 1)
    brow = lane * MOE_ROWS
    sub = lax.broadcasted_iota(jnp.int32, blk_ref.shape, 0)
    be = jnp.zeros(blk_ref.shape, jnp.int32)
    pend = jnp.zeros(blk_ref.shape, jnp.int32)
    ps = jnp.int32(0)
    for e in range(MOE_EXPERTS):
        c = cnt_ref[e]
        pe = ps + lax.shift_left(lax.shift_right_logical(c + (MOE_ROWS - 1), MOE_ROWS_LOG2),
                                 MOE_ROWS_LOG2)
        d1 = jnp.where(e1 == e, ps + r1, d1)
        d2 = jnp.where(e2 == e, ps + r2, d2)
        be = be + (brow >= pe).astype(jnp.int32)
        pend = jnp.where(lane == e, pe, pend)
        ps = pe
    d1_ref[...] = d1
    d2_ref[...] = d2
    n_used = lax.shift_right_logical(ps, MOE_ROWS_LOG2)
    blk_ref[...] = jnp.where(sub == 0, jnp.minimum(be, MOE_EXPERTS - 1),
                             jnp.where(sub == 1, pend, n_used))


def _finalize(counts, e1, e2, r1, r2, nb):
    nbp = (nb + LANES - 1) // LANES * LANES
    whole = pl.BlockSpec(e1.shape, lambda i, c: (0, 0, 0))
    ishape = jax.ShapeDtypeStruct(e1.shape, jnp.int32)
    return pl.pallas_call(
        _finalize_kernel,
        out_shape=(ishape, ishape, jax.ShapeDtypeStruct((SUBLANES, nbp), jnp.int32)),
        grid_spec=pltpu.PrefetchScalarGridSpec(
            num_scalar_prefetch=1,
            grid=(1,),
            in_specs=[whole, whole, whole, whole],
            out_specs=(whole, whole, pl.BlockSpec((SUBLANES, nbp), lambda i, c: (0, 0)))),
        compiler_params=_cparams("arbitrary"),
        name="moe_finalize",
    )(counts, e1, e2, r1, r2)


def _token_copy(src, r, dst, d, sem):
    return pltpu.make_async_copy(src.at[pl.ds(pl.multiple_of(r * SUBLANES, SUBLANES), SUBLANES), :],
                                 dst.at[pl.ds(pl.multiple_of(d * SUBLANES, SUBLANES), SUBLANES), :],
                                 sem)


def _zero_fill_padding(pend_ref, nu_ref, xs_out, zbuf, zsem):
    blk_rows = MOE_ROWS * SUBLANES
    nb = xs_out.shape[0] // blk_rows
    zbuf[...] = jnp.zeros_like(zbuf)

    def block_copy(b):
        r0 = pl.multiple_of(b * blk_rows, blk_rows)
        return pltpu.make_async_copy(zbuf, xs_out.at[pl.ds(r0, blk_rows), :], zsem)

    def seg_last_block(e):
        pe = pend_ref[e]
        prev = pend_ref[e - 1] if e > 0 else 0
        return pe > prev, lax.shift_right_logical(pe, MOE_ROWS_LOG2) - 1

    for e in range(MOE_EXPERTS):
        nonempty, b = seg_last_block(e)

        @pl.when(nonempty)
        def _():
            block_copy(b).start()

    def tail_start(b, c):
        block_copy(b).start()
        return c

    lax.fori_loop(nu_ref[0], nb, tail_start, 0)
    for e in range(MOE_EXPERTS):
        nonempty, b = seg_last_block(e)

        @pl.when(nonempty)
        def _():
            block_copy(b).wait()

    def tail_wait(b, c):
        block_copy(b).wait()
        return c

    lax.fori_loop(nu_ref[0], nb, tail_wait, 0)


def _dispatch_kernel(dest_ref, pend_ref, nu_ref, h_ref, xs_out, sem, zbuf, zsem, *, tiles):
    i = pl.program_id(0)

    @pl.when(i == 0)
    def _():
        _zero_fill_padding(pend_ref, nu_ref, xs_out, zbuf, zsem)

    rows = tiles * TM
    for q in range(tiles):
        base = i * (2 * rows) + q * TM

        def start(r, c):
            _token_copy(h_ref, q * TM + r, xs_out, dest_ref[base + r], sem).start(priority=0)
            _token_copy(h_ref, q * TM + r, xs_out, dest_ref[base + rows + r],
                        sem).start(priority=1)
            return c

        lax.fori_loop(0, TM, start, 0, unroll=8)
    for _ in range(2):
        pltpu.make_async_copy(h_ref, xs_out.at[pl.ds(0, tiles * TM * SUBLANES), :], sem).wait()


def _dispatch(dest, pad_end, n_used, h, n_rows):
    n = h.shape[0] // SUBLANES
    nt = n // TM
    tiles = _route_tiles(nt)
    return pl.pallas_call(
        functools.partial(_dispatch_kernel, tiles=tiles),
        out_shape=jax.ShapeDtypeStruct((n_rows * SUBLANES, LANES), F32),
        grid_spec=pltpu.PrefetchScalarGridSpec(
            num_scalar_prefetch=3,
            grid=(nt // tiles,),
            in_specs=[pl.BlockSpec((tiles * TM * SUBLANES, LANES), lambda i, *_: (i, 0))],
            out_specs=pl.BlockSpec(memory_space=pl.ANY),
            scratch_shapes=[pltpu.SemaphoreType.DMA, pltpu.VMEM((MOE_ROWS * SUBLANES, LANES), F32),
                            pltpu.SemaphoreType.DMA]),
        compiler_params=_cparams("arbitrary"),
        name="moe_dispatch",
    )(dest, pad_end, n_used, h)


X_SLOTS = 3


Y_SLOTS = 2


def _expert_kernel(be_ref, pend_ref, nu_ref, xs_hbm, wg_hbm, wu_hbm, wd_hbm, ys_hbm,
                   xbuf, ybuf, wg_f, wu_f, wd_f, wg_s, wu_s, wd_s, xsem, ysem, wsem, ord_ref,
                   *, layer):
    nu = nu_ref[0]
    blk_rows = MOE_ROWS * SUBLANES
    nb = ys_hbm.shape[0] // blk_rows

    def rows_of(ref, blk):
        return ref.at[pl.ds(pl.multiple_of(blk * blk_rows, blk_rows), blk_rows), :]

    def x_copy(blk, slot):
        return pltpu.make_async_copy(rows_of(xs_hbm, blk), xbuf.at[slot], xsem.at[slot])

    def y_copy(blk, slot):
        return pltpu.make_async_copy(ybuf.at[slot], rows_of(ys_hbm, blk), ysem.at[slot])

    def w_copies(e, slot):
        return (pltpu.make_async_copy(wg_hbm.at[layer, e], wg_f.at[slot], wsem.at[slot]),
                pltpu.make_async_copy(wu_hbm.at[layer, e], wu_f.at[slot], wsem.at[slot]),
                pltpu.make_async_copy(wd_hbm.at[layer, e], wd_f.at[slot], wsem.at[slot]))

    ord_ref[0] = 0
    for j in range(X_SLOTS - 1):
        @pl.when(j < nu)
        def _():
            x_copy(j, j).start()

    @pl.when(nu > 0)
    def _():
        for c in w_copies(be_ref[0], 0):
            c.start()

    def block(b, carry):
        ahead = b + (X_SLOTS - 1)

        @pl.when(ahead < nu)
        def _():
            x_copy(ahead, lax.rem(ahead, X_SLOTS)).start()

        e = be_ref[b]

        @pl.when((b == 0) | (e != be_ref[jnp.maximum(b - 1, 0)]))
        def _():
            k = ord_ref[0]
            slot = lax.rem(k, 2)
            for c in w_copies(e, slot):
                c.wait()
            wg_s[...] = wg_f[slot].astype(BF16)
            wu_s[...] = wu_f[slot].astype(BF16)
            wd_s[...] = wd_f[slot].astype(BF16)
            nxt = lax.shift_right_logical(pend_ref[e], MOE_ROWS_LOG2)

            @pl.when(nxt < nu)
            def _():
                for c in w_copies(be_ref[nxt], 1 - slot):
                    c.start(priority=1)

            ord_ref[0] = k + 1

        slot = lax.rem(b, X_SLOTS)
        yslot = lax.rem(b, Y_SLOTS)
        x_copy(b, slot).wait()

        @pl.when(b >= Y_SLOTS)
        def _():
            y_copy(b - Y_SLOTS, yslot).wait()

        x = _load_token_tiles(xbuf.at[slot]).astype(BF16)
        a = jnp.dot(x, wg_s[...], preferred_element_type=F32)
        u = jnp.dot(x, wu_s[...], preferred_element_type=F32)
        _store_token_tiles(ybuf.at[yslot], jnp.dot((_silu(a) * u).astype(BF16), wd_s[...],
                                                   preferred_element_type=F32))
        y_copy(b, yslot).start()
        return carry

    lax.fori_loop(0, nu, block, 0)
    for j in range(1, Y_SLOTS + 1):
        @pl.when(nu >= j)
        def _():
            y_copy(nu - j, lax.rem(nu - j, Y_SLOTS)).wait()

    ybuf[0] = jnp.zeros(ybuf.shape[1:], ybuf.dtype)

    def tail_start(b, c):
        y_copy(b, 0).start()
        return c

    def tail_wait(b, c):
        y_copy(b, 0).wait()
        return c

    lax.fori_loop(nu, nb, tail_start, 0)
    lax.fori_loop(nu, nb, tail_wait, 0)


def _experts(blk_expert, pad_end, n_used, xs, w_gate, w_up, w_down, layer):
    d, hid = w_gate.shape[2:]
    blk_rows = MOE_ROWS * SUBLANES
    hbm = pl.BlockSpec(memory_space=pl.ANY)
    return pl.pallas_call(
        functools.partial(_expert_kernel, layer=layer),
        out_shape=jax.ShapeDtypeStruct(xs.shape, F32),
        grid_spec=pltpu.PrefetchScalarGridSpec(
            num_scalar_prefetch=3,
            grid=(1,),
            in_specs=[hbm, hbm, hbm, hbm],
            out_specs=hbm,
            scratch_shapes=[pltpu.VMEM((X_SLOTS, blk_rows, LANES), F32),
                            pltpu.VMEM((Y_SLOTS, blk_rows, LANES), F32),
                            pltpu.VMEM((2, d, hid), F32), pltpu.VMEM((2, d, hid), F32),
                            pltpu.VMEM((2, hid, d), F32),
                            pltpu.VMEM((d, hid), BF16), pltpu.VMEM((d, hid), BF16),
                            pltpu.VMEM((hid, d), BF16),
                            pltpu.SemaphoreType.DMA((X_SLOTS,)), pltpu.SemaphoreType.DMA((Y_SLOTS,)),
                            pltpu.SemaphoreType.DMA((2,)), pltpu.SMEM((1,), jnp.int32)]),
        compiler_params=_cparams("arbitrary"),
        name="moe_experts",
    )(blk_expert, pad_end, n_used, xs, w_gate, w_up, w_down)


def _combined_tile(dest_ref, s_ref, wt_ref, mod_ref, ys_ref, ybuf, sem, *, ctx_tiles, tiles):
    i = pl.program_id(0)
    slot = i % 2
    rows = tiles * TM

    def gather(tile, slot):
        tile = jnp.asarray(tile, jnp.int32)
        base = lax.div(tile, tiles) * (2 * rows) + lax.rem(tile, tiles) * TM

        def start(r, c):
            _token_copy(ys_ref, dest_ref[base + r], ybuf.at[slot, 0], r,
                        sem.at[slot]).start(priority=0)
            _token_copy(ys_ref, dest_ref[base + rows + r], ybuf.at[slot, 1], r,
                        sem.at[slot]).start(priority=1)
            return c

        lax.fori_loop(0, TM, start, 0, unroll=8)

    @pl.when(i == 0)
    def _():
        gather(0, 0)

    @pl.when(i + 1 < pl.num_programs(0))
    def _():
        gather(i + 1, 1 - slot)

    for k in range(2):
        pltpu.make_async_copy(ys_ref.at[pl.ds(0, TM * SUBLANES), :], ybuf.at[slot, k],
                              sem.at[slot]).wait()
    gate = _mod_rows(mod_ref, i, ctx_tiles, 3)[2]
    wt = wt_ref[...]
    y = (wt[:, 0:1] * _load_token_tiles(ybuf.at[slot, 0])
         + wt[:, 1:2] * _load_token_tiles(ybuf.at[slot, 1]))
    return s_ref[...] + gate * y


def _combine_scratch():
    return [pltpu.VMEM((2, 2, TM * SUBLANES, LANES), F32), pltpu.SemaphoreType.DMA((2,))]


def _combine_kernel(dest_ref, s_ref, wt_ref, mod_ref, fg_ref, ys_ref, o_ref, ybuf, sem,
                    *, final_norm, **kw):
    out = _combined_tile(dest_ref, s_ref, wt_ref, mod_ref, ys_ref, ybuf, sem, **kw)
    if final_norm:
        out = _rms(out, fg_ref[...])
    o_ref[...] = out


def _combine(dest, s, wts, mods, final_g, ys, ctx_tiles, final_norm):
    n, d = s.shape
    return pl.pallas_call(
        functools.partial(_combine_kernel, ctx_tiles=ctx_tiles, final_norm=final_norm,
                          tiles=_route_tiles(n // TM)),
        out_shape=jax.ShapeDtypeStruct((n, d), F32),
        grid_spec=pltpu.PrefetchScalarGridSpec(
            num_scalar_prefetch=1,
            grid=(n // TM,),
            in_specs=[pl.BlockSpec((TM, d), lambda i, dst: (i, 0)),
                      pl.BlockSpec((TM, LANES), lambda i, dst: (i, 0)),
                      pl.BlockSpec(mods.shape, lambda i, dst: (0, 0)),
                      pl.BlockSpec((1, d), lambda i, dst: (0, 0)),
                      pl.BlockSpec(memory_space=pl.ANY)],
            out_specs=pl.BlockSpec((TM, d), lambda i, dst: (i, 0)),
            scratch_shapes=_combine_scratch()),
        compiler_params=_cparams("arbitrary"),
        name="moe_combine",
    )(dest, s, wts, mods, final_g.reshape(1, d), ys)


def _moe_layer(s, mods, norm_g, rg_w, rg_b, re_w, re_b, w_gate, w_up, w_down, layer, ctx_tiles,
               final_g, final_norm, defer=False):
    n, d = s.shape
    h, e1, e2, r1, r2, wts, cnt = _router(s, mods, norm_g, rg_w, rg_b, re_w, re_b, ctx_tiles)
    counts = cnt[MOE_GROUPS:MOE_GROUPS + MOE_EXPERTS, 0].astype(jnp.int32)
    nb = (2 * n + MOE_EXPERTS * (MOE_ROWS - 1)) // MOE_ROWS + 1
    d1, d2, blk = _finalize(counts, e1, e2, r1, r2, nb)
    dest = jnp.concatenate([d1, d2], axis=1).reshape(2 * n)
    n_used = blk[2, :1]
    pad_end = blk[1, :MOE_EXPERTS]
    xs = _dispatch(dest, pad_end, n_used, h, nb * MOE_ROWS)
    ys = _experts(blk[0, :nb], pad_end, n_used, xs, w_gate, w_up, w_down, layer)
    if defer:
        return _PendingCombine(dest, s, wts, mods, ys, ctx_tiles)
    return _combine(dest, s, wts, mods, final_g, ys, ctx_tiles, final_norm)


class _PendingCombine(NamedTuple):
    dest: jax.Array
    s: jax.Array
    wts: jax.Array
    mods: jax.Array
    ys: jax.Array
    ctx_tiles: int


def _pending_operands(p):
    n, d = p.s.shape
    kw = dict(ctx_tiles=p.ctx_tiles, tiles=_route_tiles(n // TM))
    specs = [pl.BlockSpec((TM, d), lambda i, dst: (i, 0)),
             pl.BlockSpec((TM, LANES), lambda i, dst: (i, 0)),
             pl.BlockSpec(p.mods.shape, lambda i, dst: (0, 0)),
             pl.BlockSpec(memory_space=pl.ANY)]
    return kw, p.dest, specs, (p.s, p.wts, p.mods, p.ys), _combine_scratch()


def _conv_tile(x, prev_ref, next_ref, has_prev, has_next, w_ref, b_ref):
    rows = x.shape[0]
    S = SUBLANES
    ridx = lax.broadcasted_iota(jnp.int32, (S, x.shape[1]), 0)
    pm = jnp.where(has_prev, 1.0, 0.0)
    nm = jnp.where(has_next, 1.0, 0.0)
    p2 = prev_ref[S - 2:S - 1, :] * pm
    p1 = prev_ref[S - 1:S, :] * pm
    n1 = next_ref[0:1, :] * nm

    def fix_head(rolled, head):
        return jnp.concatenate([head(rolled[:S]), rolled[S:]], axis=0)

    xm1 = fix_head(pltpu.roll(x, 1, axis=0), lambda g: jnp.where(ridx == 0, p1, g))
    xm2 = fix_head(pltpu.roll(x, 2, axis=0),
                   lambda g: jnp.where(ridx == 0, p2, jnp.where(ridx == 1, p1, g)))
    xp1 = pltpu.roll(x, rows - 1, axis=0)
    xp1 = jnp.concatenate([xp1[:rows - S], jnp.where(ridx == S - 1, n1, xp1[rows - S:])], axis=0)
    return (xm2 * w_ref[0:1, :] + xm1 * w_ref[1:2, :] + x * w_ref[2:3, :]
            + xp1 * w_ref[3:4, :] + b_ref[...])


def _ml_proj_kernel(dest_ref, s_ref, wt_ref, pmod_ref, ys_ref, mod_ref, g_ref, w_ref, wg_ref,
                    gb_ref, snew_ref, qk_ref, v_ref, o_ref, gt_ref, ybuf, sem, **pending_kw):
    i = pl.program_id(0)
    x = _combined_tile(dest_ref, s_ref, wt_ref, pmod_ref, ys_ref, ybuf, sem, **pending_kw)
    snew_ref[...] = x
    shift, scale, _ = _mod_rows(mod_ref, i, 1, 0)
    h = _rms(x, g_ref[...]) * (1.0 + scale) + shift
    z = _bdot(h, w_ref[...])
    nqk = qk_ref.shape[1]
    nv = v_ref.shape[1]
    qk_ref[...] = z[:, :nqk]
    v_ref[...] = z[:, nqk:nqk + nv].astype(BF16)
    o_ref[...] = z[:, nqk + nv:].astype(BF16)
    pre = _dot3(h, wg_ref[...], ((1,), (0,))) + gb_ref[...]
    lane = lax.broadcasted_iota(jnp.int32, pre.shape, 1)
    is_forget = ((lane >> 2) & 1) == 1
    gt_ref[...] = jnp.where(is_forget, -_softplus(-pre), pre)


def _ml_proj(pending, mods, norm_g, w_in, gate_b):
    n, d = pending.s.shape
    kw, dest, p_specs, p_args, p_scratch = _pending_operands(pending)
    nqk = 2 * ML_HEADS * ML_DK
    nv = ML_HEADS * ML_DV
    n_main = nqk + 2 * nv
    n_gate = w_in.shape[1] - n_main
    w_main = w_in[:, :n_main].astype(BF16)
    w_gate = jnp.concatenate([w_in[:, n_main:], jnp.zeros((d, LANES - n_gate), F32)], axis=1)
    gb = jnp.concatenate([gate_b.reshape(n_gate), jnp.zeros((LANES - n_gate,), F32)]).reshape(1, LANES)
    tile = lambda w: pl.BlockSpec((TM, w), lambda i, dst: (i, 0))
    whole = lambda shape: pl.BlockSpec(shape, lambda i, dst: (0,) * len(shape))
    return pl.pallas_call(
        functools.partial(_ml_proj_kernel, **kw),
        out_shape=(jax.ShapeDtypeStruct((n, d), F32),
                   jax.ShapeDtypeStruct((n, nqk), F32), jax.ShapeDtypeStruct((n, nv), BF16),
                   jax.ShapeDtypeStruct((n, nv), BF16), jax.ShapeDtypeStruct((n, LANES), F32)),
        grid_spec=pltpu.PrefetchScalarGridSpec(
            num_scalar_prefetch=1,
            grid=(n // TM,),
            in_specs=p_specs + [whole(mods.shape), whole((1, d)), whole(w_main.shape),
                                whole((d, LANES)), whole((1, LANES))],
            out_specs=(tile(d), tile(nqk), tile(nv), tile(nv), tile(LANES)),
            scratch_shapes=p_scratch),
        compiler_params=_cparams("arbitrary"),
        name="mlstm_proj",
    )(dest, *p_args, mods, norm_g.reshape(1, d), w_main, w_gate, gb)


def _ml_chunk_index(j, n_chunks, ctx_chunks, reverse):
    if not reverse:
        return j
    return jnp.where(j < ctx_chunks, ctx_chunks - 1 - j, n_chunks - 1 + ctx_chunks - j)


def _ml_rec_kernel(qk_ref, qkp_ref, qkn_ref, v_ref, gt_ref, gtt_ref, cw_ref, cb_ref, o_ref,
                   c_scr, n_scr, m_scr, *, reverse, n_chunks, ctx_chunks):
    j = pl.program_id(0)
    c = _ml_chunk_index(j, n_chunks, ctx_chunks, reverse)

    @pl.when(j == 0)
    def _():
        c_scr[...] = jnp.zeros_like(c_scr)
        n_scr[...] = jnp.zeros_like(n_scr)
        m_scr[...] = jnp.zeros_like(m_scr)

    has_prev = (c != 0) & (c != ctx_chunks)
    has_next = (c != ctx_chunks - 1) & (c != n_chunks - 1)
    qk = _silu(_conv_tile(qk_ref[...], qkp_ref, qkn_ref, has_prev, has_next, cw_ref, cb_ref))
    L = ML_CHUNK
    ri = lax.broadcasted_iota(jnp.int32, (L, L), 0)
    ci = lax.broadcasted_iota(jnp.int32, (L, L), 1)
    past = (ci >= ri) if reverse else (ci <= ri)
    pastf = past.astype(F32)
    gt = gt_ref[...]
    gtt = gtt_ref[...]
    b_col = jnp.dot(pastf, gt, precision=HI, preferred_element_type=F32)
    b_row = jnp.dot(gtt, pastf.T, precision=HI, preferred_element_type=F32)
    last = 0 if reverse else L - 1
    dbase = 8 if reverse else 0
    nq = ML_HEADS * ML_DK
    for hd in range(ML_HEADS):
        cl = dbase + hd
        cf = dbase + 4 + hd
        q = qk[:, hd * ML_DK:(hd + 1) * ML_DK] * (ML_DK ** -0.5)
        k = qk[:, nq + hd * ML_DK:nq + (hd + 1) * ML_DK]
        v = v_ref[:, hd * ML_DV:(hd + 1) * ML_DV]
        li_c = gt[:, cl:cl + 1]
        li_r = gtt[cl:cl + 1, :]
        b_c = b_col[:, cf:cf + 1]
        b_r = b_row[cf:cf + 1, :]
        g = b_r[:, last:last + 1]
        m0 = m_scr[hd:hd + 1, 0:1]
        c0 = c_scr[hd]
        n0 = n_scr[hd:hd + 1, :]
        a_c = g - b_c + li_c
        a_r = g - b_r + li_r
        m_loc = jnp.max(a_r, axis=-1, keepdims=True)
        inter = b_c + m0
        dlog = jnp.where(past, b_c - b_r + li_r, -jnp.inf)
        m = jnp.maximum(inter, jnp.max(dlog, axis=-1, keepdims=True))
        qb = q.astype(BF16)
        sc = lax.dot_general(qb, k.astype(BF16), (((1,), (1,)), ((), ())),
                             preferred_element_type=F32) * jnp.exp(dlog - m)
        w_inter = jnp.exp(inter - m)
        num = (jnp.dot(sc.astype(BF16), v, preferred_element_type=F32)
               + w_inter * jnp.dot(qb, c0.astype(BF16), preferred_element_type=F32))
        den = (jnp.sum(sc, axis=-1, keepdims=True)
               + w_inter * jnp.sum(q * n0, axis=-1, keepdims=True))
        o_ref[:, hd * ML_DV:(hd + 1) * ML_DV] = (
            num / jnp.maximum(jnp.abs(den), jnp.exp(-m))).astype(BF16)
        m_new = jnp.maximum(g + m0, m_loc)
        dec = jnp.exp(g + m0 - m_new)
        scl = jnp.exp(m_loc - m_new)
        kw = k * jnp.exp(a_c - m_loc)
        c_scr[hd] = dec * c0 + scl * jnp.dot(kw.T.astype(BF16), v, preferred_element_type=F32)
        n_scr[hd:hd + 1, :] = dec * n0 + scl * jnp.sum(kw, axis=0, keepdims=True)
        m_scr[hd:hd + 1, :] = jnp.broadcast_to(m_new, (1, LANES))


def _ml_rec(qk, v, gt, gtt, conv_w, conv_b, reverse):
    n, nqk = qk.shape
    nv = v.shape[1]
    L = ML_CHUNK
    nc = n // L
    cc = TM // L
    hb = L // SUBLANES
    idx = lambda j: _ml_chunk_index(j, nc, cc, reverse)
    last8 = n // SUBLANES - 1
    return pl.pallas_call(
        functools.partial(_ml_rec_kernel, reverse=reverse, n_chunks=nc, ctx_chunks=cc),
        out_shape=jax.ShapeDtypeStruct((n, nv), BF16),
        grid=(nc,),
        in_specs=[pl.BlockSpec((L, nqk), lambda j: (idx(j), 0)),
                  pl.BlockSpec((SUBLANES, nqk), lambda j: (jnp.maximum(idx(j) * hb - 1, 0), 0)),
                  pl.BlockSpec((SUBLANES, nqk), lambda j: (jnp.minimum((idx(j) + 1) * hb, last8), 0)),
                  pl.BlockSpec((L, nv), lambda j: (idx(j), 0)),
                  pl.BlockSpec((L, LANES), lambda j: (idx(j), 0)),
                  pl.BlockSpec((2 * SUBLANES, L), lambda j: (0, idx(j))),
                  _full((4, nqk)), _full((1, nqk))],
        out_specs=pl.BlockSpec((L, nv), lambda j: (idx(j), 0)),
        scratch_shapes=[pltpu.VMEM((ML_HEADS, ML_DK, ML_DV), F32),
                        pltpu.VMEM((SUBLANES, ML_DK), F32),
                        pltpu.VMEM((SUBLANES, LANES), F32)],
        compiler_params=_cparams("arbitrary"),
        name="mlstm_rev" if reverse else "mlstm_fwd",
    )(qk, qk, qk, v, gt, gtt, conv_w, conv_b.reshape(1, nqk))


def _ml_out_kernel(hf_ref, hr_ref, o_ref, s_ref, mod_ref, ng_ref, w_ref, out_ref, p_scr):
    i = pl.program_id(0)
    gate = _mod_rows(mod_ref, i, 1, 0)[2]
    hs = hf_ref[...].astype(F32) + hr_ref[...].astype(F32)
    sig = _sigmoid(o_ref[...].astype(F32))
    ng = ng_ref[...]
    for hd in range(ML_HEADS):
        cs = slice(hd * ML_DV, (hd + 1) * ML_DV)
        seg = hs[:, cs]
        hn = seg * lax.rsqrt(jnp.mean(seg * seg, axis=-1, keepdims=True) + EPS) * ng[:, cs]
        p_scr[:, cs] = (hn * sig[:, cs]).astype(BF16)
    y = jnp.dot(p_scr[...], w_ref[...], preferred_element_type=F32)
    out_ref[...] = s_ref[...] + gate * y


def _ml_out(hf, hr, o, s, mods, norm_g, w_out):
    n, d = s.shape
    nv = hf.shape[1]
    tile = lambda w: pl.BlockSpec((TM, w), lambda i: (i, 0))
    return pl.pallas_call(
        _ml_out_kernel,
        out_shape=jax.ShapeDtypeStruct((n, d), F32),
        grid=(n // TM,),
        in_specs=[tile(nv), tile(nv), tile(nv), tile(d), _full(mods.shape), _full((1, nv)),
                  _full(w_out.shape)],
        out_specs=tile(d),
        scratch_shapes=[pltpu.VMEM((TM, nv), BF16)],
        compiler_params=_cparams("arbitrary"),
        name="mlstm_out",
    )(hf, hr, o, s, mods, norm_g.reshape(1, nv), w_out.astype(BF16))


def _mlstm_layer(pending, mods, norm_g, w_in, conv_w, conv_b, gate_b, ml_norm_g, w_out):
    s, qk, v, o, gt = _ml_proj(pending, mods, norm_g, w_in, gate_b)
    gtt = gt[:, :2 * SUBLANES].T
    hf = _ml_rec(qk, v, gt, gtt, conv_w, conv_b, False)
    hr = _ml_rec(qk, v, gt, gtt, conv_w, conv_b, True)
    return _ml_out(hf, hr, o, s, mods, ml_norm_g, w_out)


def _lru_proj_kernel(dest_ref, s_ref, wt_ref, pmod_ref, ys_ref, mod_ref, g_ref, w_ref,
                     snew_ref, gl_ref, xr_ref, ybuf, sem, **pending_kw):
    i = pl.program_id(0)
    x = _combined_tile(dest_ref, s_ref, wt_ref, pmod_ref, ys_ref, ybuf, sem, **pending_kw)
    snew_ref[...] = x
    shift, scale, _ = _mod_rows(mod_ref, i, 1, 0)
    h = _rms(x, g_ref[...]) * (1.0 + scale) + shift
    z = _bdot(h, w_ref[...])
    w = gl_ref.shape[1]
    gl_ref[...] = _gelu(z[:, :w]).astype(BF16)
    xr_ref[...] = z[:, w:]


def _lru_proj(pending, mods, norm_g, w_in):
    n, d = pending.s.shape
    kw, dest, p_specs, p_args, p_scratch = _pending_operands(pending)
    w = w_in.shape[1] // 2
    tile = lambda c: pl.BlockSpec((TM, c), lambda i, dst: (i, 0))
    whole = lambda shape: pl.BlockSpec(shape, lambda i, dst: (0,) * len(shape))
    return pl.pallas_call(
        functools.partial(_lru_proj_kernel, **kw),
        out_shape=(jax.ShapeDtypeStruct((n, d), F32), jax.ShapeDtypeStruct((n, w), BF16),
                   jax.ShapeDtypeStruct((n, w), F32)),
        grid_spec=pltpu.PrefetchScalarGridSpec(
            num_scalar_prefetch=1,
            grid=(n // TM,),
            in_specs=p_specs + [whole(mods.shape), whole((1, d)), whole(w_in.shape)],
            out_specs=(tile(d), tile(w), tile(w)),
            scratch_shapes=p_scratch),
        compiler_params=_cparams("arbitrary"),
        name="rglru_proj",
    )(dest, *p_args, mods, norm_g.reshape(1, d), w_in.astype(BF16))


def _lru_tile_index(j, n_tiles, reverse):
    if not reverse:
        return j
    return jnp.where(j == 0, 0, n_tiles - j)


def _lru_scan_kernel(x_ref, xp_ref, xn_ref, cw_ref, cb_ref, wg_ref, ba_ref, bx_ref, lam_ref,
                     o_ref, a_scr, u_scr, carry, *, reverse, n_tiles):
    j = pl.program_id(0)
    t = _lru_tile_index(j, n_tiles, reverse)

    @pl.when(j == 0)
    def _():
        carry[...] = jnp.zeros_like(carry)

    has_prev = t > 1
    has_next = (t != 0) & (t != n_tiles - 1)
    xr = _conv_tile(x_ref[...], xp_ref, xn_ref, has_prev, has_next, cw_ref, cb_ref)
    sp = _softplus(-lam_ref[...])
    B = LRU_BLOCK
    for hd in range(LRU_HEADS):
        cs = slice(hd * B, (hd + 1) * B)
        xh = xr[:, cs]
        y = jnp.dot(xh.astype(BF16), wg_ref[hd], preferred_element_type=F32)
        r = _sigmoid(y[:, :B] + ba_ref[:, cs])
        ig = _sigmoid(y[:, B:] + bx_ref[:, cs])
        log_a = -LRU_C * r * sp[:, cs]
        a = jnp.exp(log_a)
        a_scr[:, cs] = a
        v = 1.0 - a * a
        u_scr[:, cs] = jnp.where(v > 0.0, v * lax.rsqrt(v), 0.0) * (ig * xh)

    S = SUBLANES
    w = a_scr.shape[1]
    sidx = lax.broadcasted_iota(jnp.int32, (S, w), 0)

    def group(gi, c):
        g = (TM // S - 1 - gi) if reverse else gi
        r0 = pl.multiple_of(g * S, S)
        a = a_scr[pl.ds(r0, S), :]
        u = u_scr[pl.ds(r0, S), :]
        for sft in (1, 2, 4):
            if reverse:
                ok = sidx < S - sft
                a_e = pltpu.roll(a, S - sft, axis=0)
                u_e = pltpu.roll(u, S - sft, axis=0)
            else:
                ok = sidx >= sft
                a_e = pltpu.roll(a, sft, axis=0)
                u_e = pltpu.roll(u, sft, axis=0)
            u = jnp.where(ok, a * u_e + u, u)
            a = jnp.where(ok, a * a_e, a)
        hcur = a * carry[...] + u
        u_scr[pl.ds(r0, S), :] = hcur
        edge = 0 if reverse else S - 1
        carry[...] = jnp.broadcast_to(hcur[edge:edge + 1, :], (S, w))
        return c

    lax.fori_loop(0, TM // S, group, 0)
    o_ref[...] = u_scr[...].astype(BF16)


def _lru_scan(xraw, conv_w, conv_b, w_a, b_a, w_x, b_x, lam, reverse):
    n, w = xraw.shape
    nt = n // TM
    hb = TM // SUBLANES
    idx = lambda j: _lru_tile_index(j, nt, reverse)
    last8 = n // SUBLANES - 1
    wg = jnp.concatenate([w_a, w_x], axis=-1).astype(BF16)
    return pl.pallas_call(
        functools.partial(_lru_scan_kernel, reverse=reverse, n_tiles=nt),
        out_shape=jax.ShapeDtypeStruct((n, w), BF16),
        grid=(nt,),
        in_specs=[pl.BlockSpec((TM, w), lambda j: (idx(j), 0)),
                  pl.BlockSpec((SUBLANES, w), lambda j: (jnp.maximum(idx(j) * hb - 1, 0), 0)),
                  pl.BlockSpec((SUBLANES, w), lambda j: (jnp.minimum((idx(j) + 1) * hb, last8), 0)),
                  _full((4, w)), _full((1, w)), _full(wg.shape), _full((1, w)), _full((1, w)),
                  _full((1, w))],
        out_specs=pl.BlockSpec((TM, w), lambda j: (idx(j), 0)),
        scratch_shapes=[pltpu.VMEM((TM, w), F32), pltpu.VMEM((TM, w), F32),
                        pltpu.VMEM((SUBLANES, w), F32)],
        compiler_params=_cparams("arbitrary"),
        name="rglru_rev" if reverse else "rglru_fwd",
    )(xraw, xraw, xraw, conv_w, conv_b.reshape(1, w), wg, b_a.reshape(1, w), b_x.reshape(1, w),
      lam.reshape(1, w))


def _lru_out_kernel(gl_ref, hf_ref, hr_ref, s_ref, mod_ref, w_ref, out_ref):
    gate = _mod_rows(mod_ref, 1, 0, 0)[2]
    p = gl_ref[...].astype(F32) * (hf_ref[...].astype(F32) + hr_ref[...].astype(F32))
    out_ref[...] = s_ref[...] + gate * _bdot(p, w_ref[...])


def _lru_out(gl, hf, hr, s, mods, w_out):
    n, d = s.shape
    w = gl.shape[1]
    lat = lambda c: pl.BlockSpec((TM, c), lambda i: (i + 1, 0))
    return pl.pallas_call(
        _lru_out_kernel,
        out_shape=jax.ShapeDtypeStruct((n - TM, d), F32),
        grid=(n // TM - 1,),
        in_specs=[lat(w), lat(w), lat(w), lat(d), _full(mods.shape), _full(w_out.shape)],
        out_specs=pl.BlockSpec((TM, d), lambda i: (i, 0)),
        compiler_params=_cparams("arbitrary"),
        name="rglru_out",
    )(gl, hf, hr, s, mods, w_out.astype(BF16))


def _rglru_layer(pending, mods, norm_g, w_in, conv_w, conv_b, w_a, b_a, w_x, b_x, lam, w_out):
    s, gl, xraw = _lru_proj(pending, mods, norm_g, w_in)
    hf = _lru_scan(xraw, conv_w, conv_b, w_a[0], b_a[0], w_x[0], b_x[0], lam[0], False)
    hr = _lru_scan(xraw, conv_w, conv_b, w_a[1], b_a[1], w_x[1], b_x[1], lam[1], True)
    return _lru_out(gl, hf, hr, s, mods, w_out)


def _fn_proj_kernel(s_ref, mod_ref, g_ref, wt_ref, cs_ref, yr_ref, yi_ref, ar_scr, ai_scr):
    shift, scale, _ = _mod_rows(mod_ref, 1, 0, 0)
    nm = wt_ref.shape[0]
    gw = nm // FN_GROUPS
    per = TM // FFT_N2
    nj = FN_TB // FFT_N2
    csb = cs_ref[...].astype(BF16)

    def sub(tc, c):
        r0 = pl.multiple_of(tc * TM, TM)
        h = _rms(s_ref[pl.ds(r0, TM), :], g_ref[...]) * (1.0 + scale) + shift
        zt = lax.dot_general(wt_ref[...], h.astype(BF16), (((1,), (1,)), ((), ())),
                             preferred_element_type=F32).astype(BF16)
        for g in range(FN_GROUPS):
            y = jnp.dot(csb, zt[g * gw:(g + 1) * gw, :], preferred_element_type=F32)
            for q in range(per):
                row0 = pl.multiple_of((tc * per + q) * _slab_pitch(nm) + g * gw, SUBLANES)
                ar_scr[pl.ds(row0, gw), :] = y[:gw, q * FFT_N2:(q + 1) * FFT_N2]
                ai_scr[pl.ds(row0, gw), :] = y[gw:, q * FFT_N2:(q + 1) * FFT_N2]
        return c

    lax.fori_loop(0, FN_TB // TM, sub, 0)

    def relayout(m, c):
        yr_ref[m] = ar_scr[pl.ds(m, nj, stride=_slab_pitch(nm)), :]
        yi_ref[m] = ai_scr[pl.ds(m, nj, stride=_slab_pitch(nm)), :]
        return c

    lax.fori_loop(0, nm, relayout, 0, unroll=8)


def _slab_pitch(rows):
    return rows + SUBLANES


def _dft_cos_sin(n, scale):
    k = np.arange(n, dtype=np.int64)
    ang = 2.0 * np.pi * ((k[:, None] * k[None, :]) % n).astype(np.float64) / n
    return np.cos(ang) * scale, np.sin(ang) * scale


def _fn_proj(s, mods, norm_g, w_in):
    t, d = s.shape
    nm = w_in.shape[1]
    gw = nm // FN_GROUPS
    n1 = t // FFT_N2
    nj = FN_TB // FFT_N2
    c, sn = _dft_cos_sin(gw, gw ** -0.5)
    cs = jnp.asarray(np.concatenate([c, -sn], axis=0), F32)
    yspec = pl.BlockSpec((nm, nj, FFT_N2), lambda i: (0, i, 0))
    yshape = jax.ShapeDtypeStruct((nm, n1, FFT_N2), F32)
    return pl.pallas_call(
        _fn_proj_kernel,
        out_shape=(yshape, yshape),
        grid=(t // FN_TB,),
        in_specs=[pl.BlockSpec((FN_TB, d), lambda i: (i, 0)), _full(mods.shape), _full((1, d)),
                  _full((nm, d)), _full(cs.shape)],
        out_specs=(yspec, yspec),
        scratch_shapes=[pltpu.VMEM((nj * _slab_pitch(nm), FFT_N2), F32),
                        pltpu.VMEM((nj * _slab_pitch(nm), FFT_N2), F32)],
        compiler_params=_cparams("arbitrary"),
        name="fourier_proj",
    )(s, mods, norm_g.reshape(1, d), w_in.T.astype(BF16), cs)


def _fn_fft_kernel(yr_ref, yi_ref, m_ref, tc_ref, ts_ref, d_ref, o_ref):
    n1 = yr_ref.shape[1]
    n2 = FFT_N2
    xr = jnp.concatenate([yr_ref[m].astype(BF16) for m in range(FN_CB)], axis=1)
    xi = jnp.concatenate([yi_ref[m].astype(BF16) for m in range(FN_CB)], axis=1)
    a = jnp.dot(m_ref[...].astype(BF16), jnp.concatenate([xr, xi], axis=0),
                preferred_element_type=F32)
    ar = a[:n1]
    ai = a[n1:]
    tc = jnp.concatenate([tc_ref[...]] * FN_CB, axis=1)
    ts = jnp.concatenate([ts_ref[...]] * FN_CB, axis=1)
    br = ar * tc + ai * ts
    bi = ai * tc - ar * ts
    bst = jnp.concatenate(
        [jnp.concatenate([br[:, m * n2:(m + 1) * n2], bi[:, m * n2:(m + 1) * n2]], axis=1)
         for m in range(FN_CB)], axis=0).astype(BF16)
    res = lax.dot_general(d_ref[...].astype(BF16), bst, (((1,), (1,)), ((), ())),
                          preferred_element_type=F32)
    for m in range(FN_CB):
        o_ref[m] = res[:, m * n1:(m + 1) * n1]


def _fn_fft(yr, yi):
    nm, n1, n2 = yr.shape
    t = n1 * n2
    c, sn = _dft_cos_sin(n1, n1 ** -0.5)
    m = jnp.asarray(np.block([[c, sn], [-sn, c]]), F32)
    k1 = np.arange(n1, dtype=np.int64)[:, None]
    t2 = np.arange(n2, dtype=np.int64)[None, :]
    ang = 2.0 * np.pi * ((k1 * t2) % t).astype(np.float64) / t
    tc = jnp.asarray(np.cos(ang), F32)
    ts = jnp.asarray(np.sin(ang), F32)
    c2, s2 = _dft_cos_sin(n2, n2 ** -0.5)
    dm = jnp.asarray(np.concatenate([c2, s2], axis=1), F32)
    yspec = pl.BlockSpec((FN_CB, n1, n2), lambda i: (i, 0, 0))
    return pl.pallas_call(
        _fn_fft_kernel,
        out_shape=jax.ShapeDtypeStruct((nm, n2, n1), F32),
        grid=(nm // FN_CB,),
        in_specs=[yspec, yspec, _full(m.shape), _full(tc.shape), _full(ts.shape), _full(dm.shape)],
        out_specs=pl.BlockSpec((FN_CB, n2, n1), lambda i: (i, 0, 0)),
        compiler_params=_cparams("arbitrary"),
        name="fourier_fft",
    )(yr, yi, m, tc, ts, dm)


def _fn_out_kernel(ft_ref, w_ref, s_ref, mod_ref, o_ref, a_scr):
    gate = _mod_rows(mod_ref, 1, 0, 0)[2]
    nm, nj, n1 = ft_ref.shape

    def relayout(m, c):
        a_scr[pl.ds(m, nj, stride=_slab_pitch(nm)), :] = ft_ref[m]
        return c

    lax.fori_loop(0, nm, relayout, 0, unroll=8)
    for j in range(nj):
        p0 = j * _slab_pitch(nm)
        slab = a_scr[p0:p0 + nm, :].astype(BF16)
        y = lax.dot_general(slab, w_ref[...], (((0,), (0,)), ((), ())), preferred_element_type=F32)
        rows = slice(j * n1, (j + 1) * n1)
        o_ref[rows, :] = s_ref[rows, :] + gate * y


def _fn_out(ft, s, mods, w_out):
    t, d = s.shape
    nm, n2, n1 = ft.shape
    nj = FN_TB // n1
    tok = pl.BlockSpec((FN_TB, d), lambda i: (i, 0))
    return pl.pallas_call(
        _fn_out_kernel,
        out_shape=jax.ShapeDtypeStruct((t, d), F32),
        grid=(t // FN_TB,),
        in_specs=[pl.BlockSpec((nm, nj, n1), lambda i: (0, i, 0)), _full(w_out.shape), tok,
                  _full(mods.shape)],
        out_specs=tok,
        scratch_shapes=[pltpu.VMEM((nj * _slab_pitch(nm), n1), F32)],
        compiler_params=_cparams("arbitrary"),
        name="fourier_out",
    )(ft, w_out.astype(BF16), s, mods)


def _fourier_layer(s, mods, norm_g, w_in, w_out):
    yr, yi = _fn_proj(s, mods, norm_g, w_in)
    return _fn_out(_fn_fft(yr, yi), s, mods, w_out)


def kernel(x, c, ctx, c_ctx, ada_w, ada_b, norm_mix_g, norm_ffn_g, final_norm_g, router_group_w, router_group_b, router_expert_w, router_expert_b, expert_w_gate, expert_w_up, expert_w_down, cm_w_in, cm_v_norm_g, cm_w_s, cm_b_s, cm_w_out, ml_w_in, ml_conv_w, ml_conv_b, ml_gate_b, ml_norm_g, ml_w_out, lru_w_in, lru_conv_w, lru_conv_b, lru_w_a, lru_b_a, lru_w_x, lru_b_x, lru_lambda, lru_w_out, fn_w_in, fn_w_out):
    bsz, seq, d = x.shape
    assert bsz == 1 and ada_w.shape[0] == 4 and ctx.shape[1] == TM
    c_rows = jnp.concatenate([c_ctx[None, :], c, jnp.zeros((SUBLANES - 2, d), F32)], axis=0)
    mods = _ada_table(c_rows, ada_w, ada_b)

    def moe(s, i, ctx_tiles, final_norm=False, defer=False):
        return _moe_layer(s, mods[i], norm_ffn_g[i], router_group_w[i], router_group_b[i],
                          router_expert_w[i], router_expert_b[i], expert_w_gate, expert_w_up,
                          expert_w_down, i, ctx_tiles, final_norm_g, final_norm, defer)

    s = _chunk_mlp_layer(x[0], ctx[0], mods[0], norm_mix_g[0], cm_w_in[0], cm_v_norm_g[0],
                         cm_w_s[0], cm_b_s[0], cm_w_out[0])
    s = moe(s, 0, 1, defer=True)
    s = _mlstm_layer(s, mods[1], norm_mix_g[1], ml_w_in[0], ml_conv_w[0], ml_conv_b[0],
                     ml_gate_b[0], ml_norm_g[0], ml_w_out[0])
    s = moe(s, 1, 1, defer=True)
    s = _rglru_layer(s, mods[2], norm_mix_g[2], lru_w_in[0], lru_conv_w[0], lru_conv_b[0],
                     lru_w_a[0], lru_b_a[0], lru_w_x[0], lru_b_x[0], lru_lambda[0], lru_w_out[0])
    s = moe(s, 2, 0)
    s = _fourier_layer(s, mods[3], norm_mix_g[3], fn_w_in[0], fn_w_out[0])
    s = moe(s, 3, 0, final_norm=True)
    return s[None]
```

```python
import functools
import math
from typing import NamedTuple

import jax
import jax.numpy as jnp
import numpy as np
from jax import lax
from jax.experimental import pallas as pl
from jax.experimental.pallas import tpu as pltpu

F32 = jnp.float32
BF16 = jnp.bfloat16

EPS = 1e-6
POS_BASE = 10000.0
GRID_W = 64
N_MOD = 6
TM = 256
LANES = 128
SUBLANES = 8
VMEM_LIMIT = 56 * 1024 * 1024

CM_CHUNK = 128
CM_GROUPS = 4
ML_HEADS = 4
ML_DK = 128
ML_DV = 256
ML_CHUNK = 128
LRU_HEADS = 10
LRU_BLOCK = 128
LRU_C = 8.0
FN_GROUPS = 4
FFT_N2 = 128
MOE_GROUPS = 4
MOE_EPG = 8
MOE_EXPERTS = MOE_GROUPS * MOE_EPG
MOE_ROWS_LOG2 = 8
MOE_ROWS = 1 << MOE_ROWS_LOG2
ROUTE_ROWS = 40
FN_TB = 1024
FN_CB = 16
CONV_LEFT = 2

HI = lax.Precision.HIGHEST


def _cparams(*sem):
    return pltpu.CompilerParams(dimension_semantics=sem, vmem_limit_bytes=VMEM_LIMIT)


def _full(shape):
    nd = len(shape)
    return pl.BlockSpec(shape, lambda *_: (0,) * nd)


def _rms(x, g):
    return x * lax.rsqrt(jnp.mean(x * x, axis=-1, keepdims=True) + EPS) * g


def _gelu(x):
    c = math.sqrt(2.0 / math.pi)
    return 0.5 * x * (1.0 + jnp.tanh(c * (x + 0.044715 * (x * x * x))))


def _sigmoid(x):
    return 0.5 * jnp.tanh(0.5 * x) + 0.5


def _silu(x):
    return x * _sigmoid(x)


def _softplus(x):
    return jnp.maximum(x, 0.0) + jnp.log(1.0 + jnp.exp(-jnp.abs(x)))


def _mod_rows(mod_ref, tile, ctx_tiles, first):
    row = jnp.where(tile < ctx_tiles, 0, 1)
    m = mod_ref[pl.ds(row, 1), :]
    d = m.shape[1] // N_MOD
    return tuple(m[:, (first + j) * d:(first + j + 1) * d] for j in range(3))


def _bdot(a, b):
    return jnp.dot(a.astype(BF16), b.astype(BF16), preferred_element_type=F32)


def _split_bf16(x):
    hi = x.astype(BF16)
    return hi, (x - hi.astype(F32)).astype(BF16)


def _dot3(a, b, dims):
    a_hi, a_lo = _split_bf16(a)
    b_hi, b_lo = _split_bf16(b)
    dg = functools.partial(lax.dot_general, dimension_numbers=(dims, ((), ())),
                           preferred_element_type=F32)
    return dg(a_hi, b_hi) + dg(a_hi, b_lo) + dg(a_lo, b_hi)


def _ada_kernel(c_ref, w_ref, b_ref, o_ref):
    c = c_ref[...]
    o_ref[...] = _dot3(_silu(c), w_ref[...], ((1,), (0,))) + b_ref[...]


def _ada_table(c_rows, ada_w, ada_b):
    depth, d, n = ada_w.shape
    tn = 2048
    return pl.pallas_call(
        _ada_kernel,
        out_shape=jax.ShapeDtypeStruct((depth, SUBLANES, n), F32),
        grid=(depth, n // tn),
        in_specs=[_full((SUBLANES, d)),
                  pl.BlockSpec((None, d, tn), lambda i, j: (i, 0, j)),
                  pl.BlockSpec((None, 1, tn), lambda i, j: (i, 0, j))],
        out_specs=pl.BlockSpec((None, SUBLANES, tn), lambda i, j: (i, 0, j)),
        compiler_params=_cparams("arbitrary", "arbitrary"),
        name="ada_table",
    )(c_rows, ada_w, ada_b.reshape(depth, 1, n))


def _pos_tables(seq, d):
    q = d // 4
    freq = jnp.exp(-math.log(POS_BASE) * jnp.arange(q, dtype=F32) / q)
    ar = jnp.arange(seq // GRID_W, dtype=F32)[:, None] * freq
    ac = jnp.arange(GRID_W, dtype=F32)[:, None] * freq
    return (jnp.concatenate([jnp.sin(ar), jnp.cos(ar)], axis=-1),
            jnp.concatenate([jnp.sin(ac), jnp.cos(ac)], axis=-1))


def _cm_kernel(x_ref, ctx_ref, rt_ref, ct_ref, mod_ref, g_ref, win_ref, vg_ref, ws_ref, bs_ref,
               wout_ref, o_ref, p_scr, x_scr):
    i = pl.program_id(0)

    @pl.when(i == 0)
    def _():
        x_scr[...] = ctx_ref[...]

    @pl.when(i > 0)
    def _():
        rows_per_tile = TM // GRID_W
        q2 = rt_ref.shape[1]
        r0 = (i - 1) * rows_per_tile
        rt = jnp.concatenate(
            [jnp.broadcast_to(rt_ref[pl.ds(r0 + j, 1), :], (GRID_W, q2))
             for j in range(rows_per_tile)], axis=0)
        ct = jnp.concatenate([ct_ref[...]] * rows_per_tile, axis=0)
        x_scr[...] = x_ref[...] + jnp.concatenate([rt, ct], axis=1)

    shift, scale, gate = _mod_rows(mod_ref, i, 1, 0)
    x = x_scr[...]
    h = _rms(x, g_ref[...]) * (1.0 + scale) + shift
    z = _gelu(_bdot(h, win_ref[...]))
    w = z.shape[1] // 2
    u = z[:, :w]
    v = _rms(z[:, w:], vg_ref[...]).astype(BF16)
    gw = w // CM_GROUPS
    for c in range(TM // CM_CHUNK):
        r = slice(c * CM_CHUNK, (c + 1) * CM_CHUNK)
        for g in range(CM_GROUPS):
            cs = slice(g * gw, (g + 1) * gw)
            s = jnp.dot(ws_ref[g], v[r, cs], preferred_element_type=F32) + bs_ref[:, g:g + 1]
            p_scr[r, cs] = (u[r, cs] * s).astype(BF16)
    y = jnp.dot(p_scr[...], wout_ref[...], preferred_element_type=F32)
    o_ref[...] = x + gate * y


def _chunk_mlp_layer(x2, ctx2, mods, norm_g, w_in, v_g, w_s, b_s, w_out):
    seq, d = x2.shape
    n_ctx = ctx2.shape[0]
    assert n_ctx == TM and seq % TM == 0 and TM % GRID_W == 0
    n = n_ctx + seq
    w = w_out.shape[0]
    rt, ct = _pos_tables(seq, d)
    return pl.pallas_call(
        _cm_kernel,
        out_shape=jax.ShapeDtypeStruct((n, d), F32),
        grid=(n // TM,),
        in_specs=[pl.BlockSpec((TM, d), lambda i: (jnp.maximum(i - 1, 0), 0)),
                  _full((TM, d)), _full(rt.shape), _full(ct.shape),
                  _full(mods.shape), _full((1, d)), _full(w_in.shape), _full((1, w)),
                  _full(w_s.shape), _full((CM_CHUNK, CM_GROUPS)), _full(w_out.shape)],
        out_specs=pl.BlockSpec((TM, d), lambda i: (i, 0)),
        scratch_shapes=[pltpu.VMEM((TM, w), BF16), pltpu.VMEM((TM, d), F32)],
        compiler_params=_cparams("arbitrary"),
        name="chunk_mlp",
    )(x2, ctx2, rt, ct, mods, norm_g.reshape(1, d), w_in.astype(BF16), v_g.reshape(1, w),
      w_s.astype(BF16), b_s.T, w_out.astype(BF16))


def _store_token_tiles(ref, x):
    rows, d = x.shape
    for j in range(d // LANES):
        ref[pl.ds(j, rows, stride=d // LANES), :] = x[:, j * LANES:(j + 1) * LANES]


def _load_token_tiles(ref):
    chunks = SUBLANES
    rows = ref.shape[0] // chunks
    return jnp.concatenate([ref[pl.ds(j, rows, stride=chunks), :] for j in range(chunks)], axis=1)


def _route_tiles(nt):
    return next(k for k in (5, 4, 2, 1) if nt % k == 0)


def _router_kernel(s_ref, mod_ref, g_ref, rwt_ref, rbt_ref, tri_ref, h_ref, e1_ref, e2_ref, r1_ref,
                   r2_ref, wt_ref, cnt_ref, carry, *, ctx_rows):
    i = pl.program_id(0)
    rows, d = s_ref.shape

    @pl.when(i == 0)
    def _():
        carry[...] = jnp.zeros_like(carry)

    lat = mod_ref[1:2, :]
    shift, scale = lat[:, 3 * d:4 * d], lat[:, 4 * d:5 * d]
    if ctx_rows:
        ctx = mod_ref[0:1, :]
        is_ctx = (i == 0) & (lax.broadcasted_iota(jnp.int32, (rows, 1), 0) < ctx_rows)
        shift = jnp.where(is_ctx, ctx[:, 3 * d:4 * d], shift)
        scale = jnp.where(is_ctx, ctx[:, 4 * d:5 * d], scale)
    h = _rms(s_ref[...], g_ref[...]) * (1.0 + scale) + shift
    _store_token_tiles(h_ref, h)
    logits = _dot3(rwt_ref[...], h, ((1,), (1,))) + rbt_ref[...]
    row = lax.broadcasted_iota(jnp.int32, logits.shape, 0)
    neg = jnp.float32(-jnp.inf)
    big = jnp.int32(1 << 20)
    is_g = row < MOE_GROUPS
    gl = jnp.where(is_g, logits, neg)
    gmax = jnp.max(gl, axis=0, keepdims=True)
    grp = jnp.min(jnp.where(is_g & (gl == gmax), row, big), axis=0, keepdims=True)
    p_grp = 1.0 / jnp.sum(jnp.exp(gl - gmax), axis=0, keepdims=True)
    e_row = row - MOE_GROUPS
    in_grp = (e_row >= 0) & (e_row < MOE_EXPERTS) & ((e_row >> 3) == grp)
    l1 = jnp.where(in_grp, logits, neg)
    v1 = jnp.max(l1, axis=0, keepdims=True)
    i1 = jnp.min(jnp.where(in_grp & (l1 == v1), row, big), axis=0, keepdims=True)
    rest = in_grp & (row != i1)
    l2 = jnp.where(rest, logits, neg)
    v2 = jnp.max(l2, axis=0, keepdims=True)
    i2 = jnp.min(jnp.where(rest & (l2 == v2), row, big), axis=0, keepdims=True)
    e21 = jnp.exp(v2 - v1)
    w1 = p_grp / (1.0 + e21)
    w2 = p_grp * e21 / (1.0 + e21)
    oh1 = (row == i1).astype(F32)
    oh2 = (row == i2).astype(F32)
    oh = oh1 + oh2
    before = jnp.dot(oh.astype(BF16), tri_ref[...], preferred_element_type=F32) + carry[:, 0:1]
    r1_ref[...] = jnp.sum(oh1 * before, axis=0, keepdims=True).astype(jnp.int32)
    r2_ref[...] = jnp.sum(oh2 * before, axis=0, keepdims=True).astype(jnp.int32)
    e1_ref[...] = i1 - MOE_GROUPS
    e2_ref[...] = i2 - MOE_GROUPS
    carry[...] = carry[...] + jnp.sum(oh, axis=1, keepdims=True)
    cnt_ref[...] = carry[...]
    wt_ref[...] = jnp.concatenate([w1, w2, jnp.zeros((LANES - 2, rows), F32)], axis=0).T


def _router(s, mods, norm_g, rg_w, rg_b, re_w, re_b, ctx_tiles):
    n, d = s.shape
    rows = TM * _route_tiles(n // TM)
    steps = n // rows
    pad = ROUTE_ROWS - MOE_GROUPS - MOE_EXPERTS
    rwt = jnp.concatenate([rg_w, re_w, jnp.zeros((d, pad), F32)], axis=1).T
    rbt = jnp.broadcast_to(jnp.concatenate([rg_b, re_b, jnp.zeros((pad,), F32)])[:, None],
                           (ROUTE_ROWS, rows))
    tri = jnp.asarray(np.triu(np.ones((rows, rows), np.float32), 1), BF16)
    assert d == SUBLANES * LANES
    tile = pl.BlockSpec((rows, d), lambda i: (i, 0))
    irow = pl.BlockSpec((None, 1, rows), lambda i: (i, 0, 0))
    ishape = jax.ShapeDtypeStruct((steps, 1, rows), jnp.int32)
    return pl.pallas_call(
        functools.partial(_router_kernel, ctx_rows=ctx_tiles * TM),
        out_shape=(jax.ShapeDtypeStruct((n * SUBLANES, LANES), F32), ishape, ishape, ishape, ishape,
                   jax.ShapeDtypeStruct((n, LANES), F32),
                   jax.ShapeDtypeStruct((ROUTE_ROWS, LANES), F32)),
        grid=(steps,),
        in_specs=[tile, _full(mods.shape), _full((1, d)), _full((ROUTE_ROWS, d)),
                  _full((ROUTE_ROWS, rows)), _full((rows, rows))],
        out_specs=(pl.BlockSpec((rows * SUBLANES, LANES), lambda i: (i, 0)), irow, irow, irow, irow,
                   pl.BlockSpec((rows, LANES), lambda i: (i, 0)), _full((ROUTE_ROWS, LANES))),
        scratch_shapes=[pltpu.VMEM((ROUTE_ROWS, LANES), F32)],
        compiler_params=_cparams("arbitrary"),
        name="moe_router",
    )(s, mods, norm_g.reshape(1, d), rwt, rbt, tri)


def _finalize_kernel(cnt_ref, e1_ref, e2_ref, r1_ref, r2_ref, d1_ref, d2_ref, blk_ref):
    e1 = e1_ref[...]
    e2 = e2_ref[...]
    r1 = r1_ref[...]
    r2 = r2_ref[...]
    d1 = jnp.zeros_like(e1)
    d2 = jnp.zeros_like(e2)
    lane = lax.broadcasted_iota(jnp.int32, blk_ref.shape, 1)
    brow = lane * MOE_ROWS
    sub = lax.broadcasted_iota(jnp.int32, blk_ref.shape, 0)
    be = jnp.zeros(blk_ref.shape, jnp.int32)
    pend = jnp.zeros(blk_ref.shape, jnp.int32)
    ps = jnp.int32(0)
    for e in range(MOE_EXPERTS):
        c = cnt_ref[e]
        pe = ps + lax.shift_left(lax.shift_right_logical(c + (MOE_ROWS - 1), MOE_ROWS_LOG2),
                                 MOE_ROWS_LOG2)
        d1 = jnp.where(e1 == e, ps + r1, d1)
        d2 = jnp.where(e2 == e, ps + r2, d2)
        be = be + (brow >= pe).astype(jnp.int32)
        pend = jnp.where(lane == e, pe, pend)
        ps = pe
    d1_ref[...] = d1
    d2_ref[...] = d2
    n_used = lax.shift_right_logical(ps, MOE_ROWS_LOG2)
    blk_ref[...] = jnp.where(sub == 0, jnp.minimum(be, MOE_EXPERTS - 1),
                             jnp.where(sub == 1, pend, n_used))


def _finalize(counts, e1, e2, r1, r2, nb):
    nbp = (nb + LANES - 1) // LANES * LANES
    whole = pl.BlockSpec(e1.shape, lambda i, c: (0, 0, 0))
    ishape = jax.ShapeDtypeStruct(e1.shape, jnp.int32)
    return pl.pallas_call(
        _finalize_kernel,
        out_shape=(ishape, ishape, jax.ShapeDtypeStruct((SUBLANES, nbp), jnp.int32)),
        grid_spec=pltpu.PrefetchScalarGridSpec(
            num_scalar_prefetch=1,
            grid=(1,),
            in_specs=[whole, whole, whole, whole],
            out_specs=(whole, whole, pl.BlockSpec((SUBLANES, nbp), lambda i, c: (0, 0)))),
        compiler_params=_cparams("arbitrary"),
        name="moe_finalize",
    )(counts, e1, e2, r1, r2)


def _token_copy(src, r, dst, d, sem):
    return pltpu.make_async_copy(src.at[pl.ds(pl.multiple_of(r * SUBLANES, SUBLANES), SUBLANES), :],
                                 dst.at[pl.ds(pl.multiple_of(d * SUBLANES, SUBLANES), SUBLANES), :],
                                 sem)


def _zero_fill_padding(pend_ref, nu_ref, xs_out, zbuf, zsem):
    blk_rows = MOE_ROWS * SUBLANES
    nb = xs_out.shape[0] // blk_rows
    zbuf[...] = jnp.zeros_like(zbuf)

    def block_copy(b):
        r0 = pl.multiple_of(b * blk_rows, blk_rows)
        return pltpu.make_async_copy(zbuf, xs_out.at[pl.ds(r0, blk_rows), :], zsem)

    def seg_last_block(e):
        pe = pend_ref[e]
        prev = pend_ref[e - 1] if e > 0 else 0
        return pe > prev, lax.shift_right_logical(pe, MOE_ROWS_LOG2) - 1

    for e in range(MOE_EXPERTS):
        nonempty, b = seg_last_block(e)

        @pl.when(nonempty)
        def _():
            block_copy(b).start()

    def tail_start(b, c):
        block_copy(b).start()
        return c

    lax.fori_loop(nu_ref[0], nb, tail_start, 0)
    for e in range(MOE_EXPERTS):
        nonempty, b = seg_last_block(e)

        @pl.when(nonempty)
        def _():
            block_copy(b).wait()

    def tail_wait(b, c):
        block_copy(b).wait()
        return c

    lax.fori_loop(nu_ref[0], nb, tail_wait, 0)


def _dispatch_kernel(dest_ref, pend_ref, nu_ref, h_ref, xs_out, sem, zbuf, zsem, *, tiles):
    i = pl.program_id(0)

    @pl.when(i == 0)
    def _():
        _zero_fill_padding(pend_ref, nu_ref, xs_out, zbuf, zsem)

    rows = tiles * TM
    for q in range(tiles):
        base = i * (2 * rows) + q * TM

        def start(r, c):
            _token_copy(h_ref, q * TM + r, xs_out, dest_ref[base + r], sem).start(priority=0)
            _token_copy(h_ref, q * TM + r, xs_out, dest_ref[base + rows + r],
                        sem).start(priority=1)
            return c

        lax.fori_loop(0, TM, start, 0, unroll=8)
    for _ in range(2):
        pltpu.make_async_copy(h_ref, xs_out.at[pl.ds(0, tiles * TM * SUBLANES), :], sem).wait()


def _dispatch(dest, pad_end, n_used, h, n_rows):
    n = h.shape[0] // SUBLANES
    nt = n // TM
    tiles = _route_tiles(nt)
    return pl.pallas_call(
        functools.partial(_dispatch_kernel, tiles=tiles),
        out_shape=jax.ShapeDtypeStruct((n_rows * SUBLANES, LANES), F32),
        grid_spec=pltpu.PrefetchScalarGridSpec(
            num_scalar_prefetch=3,
            grid=(nt // tiles,),
            in_specs=[pl.BlockSpec((tiles * TM * SUBLANES, LANES), lambda i, *_: (i, 0))],
            out_specs=pl.BlockSpec(memory_space=pl.ANY),
            scratch_shapes=[pltpu.SemaphoreType.DMA, pltpu.VMEM((MOE_ROWS * SUBLANES, LANES), F32),
                            pltpu.SemaphoreType.DMA]),
        compiler_params=_cparams("arbitrary"),
        name="moe_dispatch",
    )(dest, pad_end, n_used, h)


X_SLOTS = 3


Y_SLOTS = 2


def _expert_kernel(be_ref, pend_ref, nu_ref, xs_hbm, wg_hbm, wu_hbm, wd_hbm, ys_hbm,
                   xbuf, ybuf, wg_f, wu_f, wd_f, wg_s, wu_s, wd_s, xsem, ysem, wsem, ord_ref,
                   *, layer):
    nu = nu_ref[0]
    blk_rows = MOE_ROWS * SUBLANES
    nb = ys_hbm.shape[0] // blk_rows

    def rows_of(ref, blk):
        return ref.at[pl.ds(pl.multiple_of(blk * blk_rows, blk_rows), blk_rows), :]

    def x_copy(blk, slot):
        return pltpu.make_async_copy(rows_of(xs_hbm, blk), xbuf.at[slot], xsem.at[slot])

    def y_copy(blk, slot):
        return pltpu.make_async_copy(ybuf.at[slot], rows_of(ys_hbm, blk), ysem.at[slot])

    def w_copies(e, slot):
        return (pltpu.make_async_copy(wg_hbm.at[layer, e], wg_f.at[slot], wsem.at[slot]),
                pltpu.make_async_copy(wu_hbm.at[layer, e], wu_f.at[slot], wsem.at[slot]),
                pltpu.make_async_copy(wd_hbm.at[layer, e], wd_f.at[slot], wsem.at[slot]))

    ord_ref[0] = 0
    for j in range(X_SLOTS - 1):
        @pl.when(j < nu)
        def _():
            x_copy(j, j).start()

    @pl.when(nu > 0)
    def _():
        for c in w_copies(be_ref[0], 0):
            c.start()

    def block(b, carry):
        ahead = b + (X_SLOTS - 1)

        @pl.when(ahead < nu)
        def _():
            x_copy(ahead, lax.rem(ahead, X_SLOTS)).start()

        e = be_ref[b]

        @pl.when((b == 0) | (e != be_ref[jnp.maximum(b - 1, 0)]))
        def _():
            k = ord_ref[0]
            slot = lax.rem(k, 2)
            for c in w_copies(e, slot):
                c.wait()
            wg_s[...] = wg_f[slot].astype(BF16)
            wu_s[...] = wu_f[slot].astype(BF16)
            wd_s[...] = wd_f[slot].astype(BF16)
            nxt = lax.shift_right_logical(pend_ref[e], MOE_ROWS_LOG2)

            @pl.when(nxt < nu)
            def _():
                for c in w_copies(be_ref[nxt], 1 - slot):
                    c.start(priority=1)

            ord_ref[0] = k + 1

        slot = lax.rem(b, X_SLOTS)
        yslot = lax.rem(b, Y_SLOTS)
        x_copy(b, slot).wait()

        @pl.when(b >= Y_SLOTS)
        def _():
            y_copy(b - Y_SLOTS, yslot).wait()

        x = _load_token_tiles(xbuf.at[slot]).astype(BF16)
        a = jnp.dot(x, wg_s[...], preferred_element_type=F32)
        u = jnp.dot(x, wu_s[...], preferred_element_type=F32)
        _store_token_tiles(ybuf.at[yslot], jnp.dot((_silu(a) * u).astype(BF16), wd_s[...],
                                                   preferred_element_type=F32))
        y_copy(b, yslot).start()
        return carry

    lax.fori_loop(0, nu, block, 0)
    for j in range(1, Y_SLOTS + 1):
        @pl.when(nu >= j)
        def _():
            y_copy(nu - j, lax.rem(nu - j, Y_SLOTS)).wait()

    ybuf[0] = jnp.zeros(ybuf.shape[1:], ybuf.dtype)

    def tail_start(b, c):
        y_copy(b, 0).start()
        return c

    def tail_wait(b, c):
        y_copy(b, 0).wait()
        return c

    lax.fori_loop(nu, nb, tail_start, 0)
    lax.fori_loop(nu, nb, tail_wait, 0)


def _experts(blk_expert, pad_end, n_used, xs, w_gate, w_up, w_down, layer):
    d, hid = w_gate.shape[2:]
    blk_rows = MOE_ROWS * SUBLANES
    hbm = pl.BlockSpec(memory_space=pl.ANY)
    return pl.pallas_call(
        functools.partial(_expert_kernel, layer=layer),
        out_shape=jax.ShapeDtypeStruct(xs.shape, F32),
        grid_spec=pltpu.PrefetchScalarGridSpec(
            num_scalar_prefetch=3,
            grid=(1,),
            in_specs=[hbm, hbm, hbm, hbm],
            out_specs=hbm,
            scratch_shapes=[pltpu.VMEM((X_SLOTS, blk_rows, LANES), F32),
                            pltpu.VMEM((Y_SLOTS, blk_rows, LANES), F32),
                            pltpu.VMEM((2, d, hid), F32), pltpu.VMEM((2, d, hid), F32),
                            pltpu.VMEM((2, hid, d), F32),
                            pltpu.VMEM((d, hid), BF16), pltpu.VMEM((d, hid), BF16),
                            pltpu.VMEM((hid, d), BF16),
                            pltpu.SemaphoreType.DMA((X_SLOTS,)), pltpu.SemaphoreType.DMA((Y_SLOTS,)),
                            pltpu.SemaphoreType.DMA((2,)), pltpu.SMEM((1,), jnp.int32)]),
        compiler_params=_cparams("arbitrary"),
        name="moe_experts",
    )(blk_expert, pad_end, n_used, xs, w_gate, w_up, w_down)


def _combined_tile(dest_ref, s_ref, wt_ref, mod_ref, ys_ref, ybuf, sem, *, ctx_tiles, tiles):
    i = pl.program_id(0)
    slot = i % 2
    rows = tiles * TM

    def gather(tile, slot):
        tile = jnp.asarray(tile, jnp.int32)
        base = lax.div(tile, tiles) * (2 * rows) + lax.rem(tile, tiles) * TM

        def start(r, c):
            _token_copy(ys_ref, dest_ref[base + r], ybuf.at[slot, 0], r,
                        sem.at[slot]).start(priority=0)
            _token_copy(ys_ref, dest_ref[base + rows + r], ybuf.at[slot, 1], r,
                        sem.at[slot]).start(priority=1)
            return c

        lax.fori_loop(0, TM, start, 0, unroll=8)

    @pl.when(i == 0)
    def _():
        gather(0, 0)

    @pl.when(i + 1 < pl.num_programs(0))
    def _():
        gather(i + 1, 1 - slot)

    for k in range(2):
        pltpu.make_async_copy(ys_ref.at[pl.ds(0, TM * SUBLANES), :], ybuf.at[slot, k],
                              sem.at[slot]).wait()
    gate = _mod_rows(mod_ref, i, ctx_tiles, 3)[2]
    wt = wt_ref[...]
    y = (wt[:, 0:1] * _load_token_tiles(ybuf.at[slot, 0])
         + wt[:, 1:2] * _load_token_tiles(ybuf.at[slot, 1]))
    return s_ref[...] + gate * y


def _combine_scratch():
    return [pltpu.VMEM((2, 2, TM * SUBLANES, LANES), F32), pltpu.SemaphoreType.DMA((2,))]


def _combine_kernel(dest_ref, s_ref, wt_ref, mod_ref, fg_ref, ys_ref, o_ref, ybuf, sem,
                    *, final_norm, **kw):
    out = _combined_tile(dest_ref, s_ref, wt_ref, mod_ref, ys_ref, ybuf, sem, **kw)
    if final_norm:
        out = _rms(out, fg_ref[...])
    o_ref[...] = out


def _combine(dest, s, wts, mods, final_g, ys, ctx_tiles, final_norm):
    n, d = s.shape
    return pl.pallas_call(
        functools.partial(_combine_kernel, ctx_tiles=ctx_tiles, final_norm=final_norm,
                          tiles=_route_tiles(n // TM)),
        out_shape=jax.ShapeDtypeStruct((n, d), F32),
        grid_spec=pltpu.PrefetchScalarGridSpec(
            num_scalar_prefetch=1,
            grid=(n // TM,),
            in_specs=[pl.BlockSpec((TM, d), lambda i, dst: (i, 0)),
                      pl.BlockSpec((TM, LANES), lambda i, dst: (i, 0)),
                      pl.BlockSpec(mods.shape, lambda i, dst: (0, 0)),
                      pl.BlockSpec((1, d), lambda i, dst: (0, 0)),
                      pl.BlockSpec(memory_space=pl.ANY)],
            out_specs=pl.BlockSpec((TM, d), lambda i, dst: (i, 0)),
            scratch_shapes=_combine_scratch()),
        compiler_params=_cparams("arbitrary"),
        name="moe_combine",
    )(dest, s, wts, mods, final_g.reshape(1, d), ys)


def _moe_layer(s, mods, norm_g, rg_w, rg_b, re_w, re_b, w_gate, w_up, w_down, layer, ctx_tiles,
               final_g, final_norm, defer=False):
    n, d = s.shape
    h, e1, e2, r1, r2, wts, cnt = _router(s, mods, norm_g, rg_w, rg_b, re_w, re_b, ctx_tiles)
    counts = cnt[MOE_GROUPS:MOE_GROUPS + MOE_EXPERTS, 0].astype(jnp.int32)
    nb = (2 * n + MOE_EXPERTS * (MOE_ROWS - 1)) // MOE_ROWS + 1
    d1, d2, blk = _finalize(counts, e1, e2, r1, r2, nb)
    dest = jnp.concatenate([d1, d2], axis=1).reshape(2 * n)
    n_used = blk[2, :1]
    pad_end = blk[1, :MOE_EXPERTS]
    xs = _dispatch(dest, pad_end, n_used, h, nb * MOE_ROWS)
    ys = _experts(blk[0, :nb], pad_end, n_used, xs, w_gate, w_up, w_down, layer)
    if defer:
        return _PendingCombine(dest, s, wts, mods, ys, ctx_tiles)
    return _combine(dest, s, wts, mods, final_g, ys, ctx_tiles, final_norm)


class _PendingCombine(NamedTuple):
    dest: jax.Array
    s: jax.Array
    wts: jax.Array
    mods: jax.Array
    ys: jax.Array
    ctx_tiles: int


def _pending_operands(p):
    n, d = p.s.shape
    kw = dict(ctx_tiles=p.ctx_tiles, tiles=_route_tiles(n // TM))
    specs = [pl.BlockSpec((TM, d), lambda i, dst: (i, 0)),
             pl.BlockSpec((TM, LANES), lambda i, dst: (i, 0)),
             pl.BlockSpec(p.mods.shape, lambda i, dst: (0, 0)),
             pl.BlockSpec(memory_space=pl.ANY)]
    return kw, p.dest, specs, (p.s, p.wts, p.mods, p.ys), _combine_scratch()


def _conv_tile(x, prev_ref, next_ref, has_prev, has_next, w_ref, b_ref):
    rows = x.shape[0]
    S = SUBLANES
    ridx = lax.broadcasted_iota(jnp.int32, (S, x.shape[1]), 0)
    pm = jnp.where(has_prev, 1.0, 0.0)
    nm = jnp.where(has_next, 1.0, 0.0)
    p2 = prev_ref[S - 2:S - 1, :] * pm
    p1 = prev_ref[S - 1:S, :] * pm
    n1 = next_ref[0:1, :] * nm

    def fix_head(rolled, head):
        return jnp.concatenate([head(rolled[:S]), rolled[S:]], axis=0)

    xm1 = fix_head(pltpu.roll(x, 1, axis=0), lambda g: jnp.where(ridx == 0, p1, g))
    xm2 = fix_head(pltpu.roll(x, 2, axis=0),
                   lambda g: jnp.where(ridx == 0, p2, jnp.where(ridx == 1, p1, g)))
    xp1 = pltpu.roll(x, rows - 1, axis=0)
    xp1 = jnp.concatenate([xp1[:rows - S], jnp.where(ridx == S - 1, n1, xp1[rows - S:])], axis=0)
    return (xm2 * w_ref[0:1, :] + xm1 * w_ref[1:2, :] + x * w_ref[2:3, :]
            + xp1 * w_ref[3:4, :] + b_ref[...])


def _ml_proj_kernel(dest_ref, s_ref, wt_ref, pmod_ref, ys_ref, mod_ref, g_ref, w_ref, wg_ref,
                    gb_ref, snew_ref, qk_ref, v_ref, o_ref, gt_ref, ybuf, sem, **pending_kw):
    i = pl.program_id(0)
    x = _combined_tile(dest_ref, s_ref, wt_ref, pmod_ref, ys_ref, ybuf, sem, **pending_kw)
    snew_ref[...] = x
    shift, scale, _ = _mod_rows(mod_ref, i, 1, 0)
    h = _rms(x, g_ref[...]) * (1.0 + scale) + shift
    z = _bdot(h, w_ref[...])
    nqk = qk_ref.shape[1]
    nv = v_ref.shape[1]
    qk_ref[...] = z[:, :nqk]
    v_ref[...] = z[:, nqk:nqk + nv].astype(BF16)
    o_ref[...] = z[:, nqk + nv:].astype(BF16)
    pre = _dot3(h, wg_ref[...], ((1,), (0,))) + gb_ref[...]
    lane = lax.broadcasted_iota(jnp.int32, pre.shape, 1)
    is_forget = ((lane >> 2) & 1) == 1
    gt_ref[...] = jnp.where(is_forget, -_softplus(-pre), pre)


def _ml_proj(pending, mods, norm_g, w_in, gate_b):
    n, d = pending.s.shape
    kw, dest, p_specs, p_args, p_scratch = _pending_operands(pending)
    nqk = 2 * ML_HEADS * ML_DK
    nv = ML_HEADS * ML_DV
    n_main = nqk + 2 * nv
    n_gate = w_in.shape[1] - n_main
    w_main = w_in[:, :n_main].astype(BF16)
    w_gate = jnp.concatenate([w_in[:, n_main:], jnp.zeros((d, LANES - n_gate), F32)], axis=1)
    gb = jnp.concatenate([gate_b.reshape(n_gate), jnp.zeros((LANES - n_gate,), F32)]).reshape(1, LANES)
    tile = lambda w: pl.BlockSpec((TM, w), lambda i, dst: (i, 0))
    whole = lambda shape: pl.BlockSpec(shape, lambda i, dst: (0,) * len(shape))
    return pl.pallas_call(
        functools.partial(_ml_proj_kernel, **kw),
        out_shape=(jax.ShapeDtypeStruct((n, d), F32),
                   jax.ShapeDtypeStruct((n, nqk), F32), jax.ShapeDtypeStruct((n, nv), BF16),
                   jax.ShapeDtypeStruct((n, nv), BF16), jax.ShapeDtypeStruct((n, LANES), F32)),
        grid_spec=pltpu.PrefetchScalarGridSpec(
            num_scalar_prefetch=1,
            grid=(n // TM,),
            in_specs=p_specs + [whole(mods.shape), whole((1, d)), whole(w_main.shape),
                                whole((d, LANES)), whole((1, LANES))],
            out_specs=(tile(d), tile(nqk), tile(nv), tile(nv), tile(LANES)),
            scratch_shapes=p_scratch),
        compiler_params=_cparams("arbitrary"),
        name="mlstm_proj",
    )(dest, *p_args, mods, norm_g.reshape(1, d), w_main, w_gate, gb)


def _ml_chunk_index(j, n_chunks, ctx_chunks, reverse):
    if not reverse:
        return j
    return jnp.where(j < ctx_chunks, ctx_chunks - 1 - j, n_chunks - 1 + ctx_chunks - j)


def _ml_fwd_kernel(qk_ref, qkp_ref, qkn_ref, v_ref, gt_ref, gtt_ref, cw_ref, cb_ref, o_ref,
                   qkc_ref, c_scr, n_scr, m_scr, *, n_chunks, ctx_chunks):
    c = pl.program_id(0)
    has_prev = (c != 0) & (c != ctx_chunks)
    has_next = (c != ctx_chunks - 1) & (c != n_chunks - 1)
    qk = _silu(_conv_tile(qk_ref[...], qkp_ref, qkn_ref, has_prev, has_next, cw_ref, cb_ref))
    qkc_ref[...] = qk.astype(BF16)
    _ml_chunk_step(qk, v_ref, gt_ref, gtt_ref, o_ref, c_scr, n_scr, m_scr, reverse=False)


def _ml_rev_kernel(qkc_ref, v_ref, gt_ref, gtt_ref, o_ref, c_scr, n_scr, m_scr):
    _ml_chunk_step(qkc_ref[...].astype(F32), v_ref, gt_ref, gtt_ref, o_ref, c_scr, n_scr, m_scr,
                   reverse=True)


def _ml_chunk_step(qk, v_ref, gt_ref, gtt_ref, o_ref, c_scr, n_scr, m_scr, *, reverse):
    @pl.when(pl.program_id(0) == 0)
    def _():
        c_scr[...] = jnp.zeros_like(c_scr)
        n_scr[...] = jnp.zeros_like(n_scr)
        m_scr[...] = jnp.zeros_like(m_scr)

    L = ML_CHUNK
    ri = lax.broadcasted_iota(jnp.int32, (L, L), 0)
    ci = lax.broadcasted_iota(jnp.int32, (L, L), 1)
    past = (ci >= ri) if reverse else (ci <= ri)
    pastf = past.astype(F32)
    gt = gt_ref[...]
    gtt = gtt_ref[...]
    b_col = jnp.dot(pastf, gt, precision=HI, preferred_element_type=F32)
    b_row = jnp.dot(gtt, pastf.T, precision=HI, preferred_element_type=F32)
    last = 0 if reverse else L - 1
    dbase = 8 if reverse else 0
    nq = ML_HEADS * ML_DK
    for hd in range(ML_HEADS):
        cl = dbase + hd
        cf = dbase + 4 + hd
        q = qk[:, hd * ML_DK:(hd + 1) * ML_DK] * (ML_DK ** -0.5)
        k = qk[:, nq + hd * ML_DK:nq + (hd + 1) * ML_DK]
        v = v_ref[:, hd * ML_DV:(hd + 1) * ML_DV]
        li_c = gt[:, cl:cl + 1]
        li_r = gtt[cl:cl + 1, :]
        b_c = b_col[:, cf:cf + 1]
        b_r = b_row[cf:cf + 1, :]
        g = b_r[:, last:last + 1]
        m0 = m_scr[hd:hd + 1, 0:1]
        c0 = c_scr[hd]
        n0 = n_scr[hd:hd + 1, :]
        a_c = g - b_c + li_c
        a_r = g - b_r + li_r
        m_loc = jnp.max(a_r, axis=-1, keepdims=True)
        inter = b_c + m0
        dlog = jnp.where(past, b_c - b_r + li_r, -jnp.inf)
        m = jnp.maximum(inter, jnp.max(dlog, axis=-1, keepdims=True))
        qb = q.astype(BF16)
        sc = lax.dot_general(qb, k.astype(BF16), (((1,), (1,)), ((), ())),
                             preferred_element_type=F32) * jnp.exp(dlog - m)
        w_inter = jnp.exp(inter - m)
        num = (jnp.dot(sc.astype(BF16), v, preferred_element_type=F32)
               + w_inter * jnp.dot(qb, c0.astype(BF16), preferred_element_type=F32))
        den = (jnp.sum(sc, axis=-1, keepdims=True)
               + w_inter * jnp.sum(q * n0, axis=-1, keepdims=True))
        o_ref[:, hd * ML_DV:(hd + 1) * ML_DV] = (
            num / jnp.maximum(jnp.abs(den), jnp.exp(-m))).astype(BF16)
        m_new = jnp.maximum(g + m0, m_loc)
        dec = jnp.exp(g + m0 - m_new)
        scl = jnp.exp(m_loc - m_new)
        kw = k * jnp.exp(a_c - m_loc)
        c_scr[hd] = dec * c0 + scl * jnp.dot(kw.T.astype(BF16), v, preferred_element_type=F32)
        n_scr[hd:hd + 1, :] = dec * n0 + scl * jnp.sum(kw, axis=0, keepdims=True)
        m_scr[hd:hd + 1, :] = jnp.broadcast_to(m_new, (1, LANES))


def _ml_scans(qk, v, gt, gtt, conv_w, conv_b):
    n, nqk = qk.shape
    nv = v.shape[1]
    L = ML_CHUNK
    nc = n // L
    cc = TM // L
    hb = L // SUBLANES
    last8 = n // SUBLANES - 1
    rows = lambda w, idx: pl.BlockSpec((L, w), lambda j: (idx(j), 0))
    scratch = [pltpu.VMEM((ML_HEADS, ML_DK, ML_DV), F32), pltpu.VMEM((SUBLANES, ML_DK), F32),
               pltpu.VMEM((SUBLANES, LANES), F32)]
    fwd = lambda j: j
    hf, qkc = pl.pallas_call(
        functools.partial(_ml_fwd_kernel, n_chunks=nc, ctx_chunks=cc),
        out_shape=(jax.ShapeDtypeStruct((n, nv), BF16), jax.ShapeDtypeStruct((n, nqk), BF16)),
        grid=(nc,),
        in_specs=[rows(nqk, fwd),
                  pl.BlockSpec((SUBLANES, nqk), lambda j: (jnp.maximum(j * hb - 1, 0), 0)),
                  pl.BlockSpec((SUBLANES, nqk), lambda j: (jnp.minimum((j + 1) * hb, last8), 0)),
                  rows(nv, fwd), rows(LANES, fwd),
                  pl.BlockSpec((2 * SUBLANES, L), lambda j: (0, j)),
                  _full((4, nqk)), _full((1, nqk))],
        out_specs=(rows(nv, fwd), rows(nqk, fwd)),
        scratch_shapes=scratch,
        compiler_params=_cparams("arbitrary"),
        name="mlstm_fwd",
    )(qk, qk, qk, v, gt, gtt, conv_w, conv_b.reshape(1, nqk))
    rev = lambda j: _ml_chunk_index(j, nc, cc, True)
    hr = pl.pallas_call(
        _ml_rev_kernel,
        out_shape=jax.ShapeDtypeStruct((n, nv), BF16),
        grid=(nc,),
        in_specs=[rows(nqk, rev), rows(nv, rev), rows(LANES, rev),
                  pl.BlockSpec((2 * SUBLANES, L), lambda j: (0, rev(j)))],
        out_specs=rows(nv, rev),
        scratch_shapes=scratch,
        compiler_params=_cparams("arbitrary"),
        name="mlstm_rev",
    )(qkc, v, gt, gtt)
    return hf, hr


def _ml_out_kernel(hf_ref, hr_ref, o_ref, s_ref, mod_ref, ng_ref, w_ref, out_ref, p_scr):
    i = pl.program_id(0)
    gate = _mod_rows(mod_ref, i, 1, 0)[2]
    hs = hf_ref[...].astype(F32) + hr_ref[...].astype(F32)
    sig = _sigmoid(o_ref[...].astype(F32))
    ng = ng_ref[...]
    for hd in range(ML_HEADS):
        cs = slice(hd * ML_DV, (hd + 1) * ML_DV)
        seg = hs[:, cs]
        hn = seg * lax.rsqrt(jnp.mean(seg * seg, axis=-1, keepdims=True) + EPS) * ng[:, cs]
        p_scr[:, cs] = (hn * sig[:, cs]).astype(BF16)
    y = jnp.dot(p_scr[...], w_ref[...], preferred_element_type=F32)
    out_ref[...] = s_ref[...] + gate * y


def _ml_out(hf, hr, o, s, mods, norm_g, w_out):
    n, d = s.shape
    nv = hf.shape[1]
    tile = lambda w: pl.BlockSpec((TM, w), lambda i: (i, 0))
    return pl.pallas_call(
        _ml_out_kernel,
        out_shape=jax.ShapeDtypeStruct((n, d), F32),
        grid=(n // TM,),
        in_specs=[tile(nv), tile(nv), tile(nv), tile(d), _full(mods.shape), _full((1, nv)),
                  _full(w_out.shape)],
        out_specs=tile(d),
        scratch_shapes=[pltpu.VMEM((TM, nv), BF16)],
        compiler_params=_cparams("arbitrary"),
        name="mlstm_out",
    )(hf, hr, o, s, mods, norm_g.reshape(1, nv), w_out.astype(BF16))


def _mlstm_layer(pending, mods, norm_g, w_in, conv_w, conv_b, gate_b, ml_norm_g, w_out):
    s, qk, v, o, gt = _ml_proj(pending, mods, norm_g, w_in, gate_b)
    gtt = gt[:, :2 * SUBLANES].T
    hf, hr = _ml_scans(qk, v, gt, gtt, conv_w, conv_b)
    return _ml_out(hf, hr, o, s, mods, ml_norm_g, w_out)


def _lru_proj_kernel(dest_ref, s_ref, wt_ref, pmod_ref, ys_ref, mod_ref, g_ref, w_ref,
                     snew_ref, gl_ref, xr_ref, ybuf, sem, **pending_kw):
    i = pl.program_id(0)
    x = _combined_tile(dest_ref, s_ref, wt_ref, pmod_ref, ys_ref, ybuf, sem, **pending_kw)
    snew_ref[...] = x
    shift, scale, _ = _mod_rows(mod_ref, i, 1, 0)
    h = _rms(x, g_ref[...]) * (1.0 + scale) + shift
    z = _bdot(h, w_ref[...])
    w = gl_ref.shape[1]
    gl_ref[...] = _gelu(z[:, :w]).astype(BF16)
    xr_ref[...] = z[:, w:]


def _lru_proj(pending, mods, norm_g, w_in):
    n, d = pending.s.shape
    kw, dest, p_specs, p_args, p_scratch = _pending_operands(pending)
    w = w_in.shape[1] // 2
    tile = lambda c: pl.BlockSpec((TM, c), lambda i, dst: (i, 0))
    whole = lambda shape: pl.BlockSpec(shape, lambda i, dst: (0,) * len(shape))
    return pl.pallas_call(
        functools.partial(_lru_proj_kernel, **kw),
        out_shape=(jax.ShapeDtypeStruct((n, d), F32), jax.ShapeDtypeStruct((n, w), BF16),
                   jax.ShapeDtypeStruct((n, w), F32)),
        grid_spec=pltpu.PrefetchScalarGridSpec(
            num_scalar_prefetch=1,
            grid=(n // TM,),
            in_specs=p_specs + [whole(mods.shape), whole((1, d)), whole(w_in.shape)],
            out_specs=(tile(d), tile(w), tile(w)),
            scratch_shapes=p_scratch),
        compiler_params=_cparams("arbitrary"),
        name="rglru_proj",
    )(dest, *p_args, mods, norm_g.reshape(1, d), w_in.astype(BF16))


def _lru_tile_index(j, n_tiles, reverse):
    if not reverse:
        return j
    return jnp.where(j == 0, 0, n_tiles - j)


def _lru_fwd_kernel(x_ref, xp_ref, xn_ref, cw_ref, cb_ref, wg_ref, ba_ref, bx_ref, lam_ref,
                    o_ref, xc_ref, a_scr, u_scr, carry, *, n_tiles):
    t = pl.program_id(0)
    has_prev = t > 1
    has_next = (t != 0) & (t != n_tiles - 1)
    xr = _conv_tile(x_ref[...], xp_ref, xn_ref, has_prev, has_next, cw_ref, cb_ref)
    xc_ref[...] = xr
    _lru_tile_scan(xr, wg_ref, ba_ref, bx_ref, lam_ref, o_ref, a_scr, u_scr, carry, reverse=False)


def _lru_rev_kernel(xc_ref, wg_ref, ba_ref, bx_ref, lam_ref, o_ref, a_scr, u_scr, carry):
    _lru_tile_scan(xc_ref[...], wg_ref, ba_ref, bx_ref, lam_ref, o_ref, a_scr, u_scr, carry,
                   reverse=True)


def _lru_tile_scan(xr, wg_ref, ba_ref, bx_ref, lam_ref, o_ref, a_scr, u_scr, carry, *, reverse):
    @pl.when(pl.program_id(0) == 0)
    def _():
        carry[...] = jnp.zeros_like(carry)

    sp = _softplus(-lam_ref[...])
    B = LRU_BLOCK
    for hd in range(LRU_HEADS):
        cs = slice(hd * B, (hd + 1) * B)
        xh = xr[:, cs]
        y = jnp.dot(xh.astype(BF16), wg_ref[hd], preferred_element_type=F32)
        r = _sigmoid(y[:, :B] + ba_ref[:, cs])
        ig = _sigmoid(y[:, B:] + bx_ref[:, cs])
        log_a = -LRU_C * r * sp[:, cs]
        a = jnp.exp(log_a)
        a_scr[:, cs] = a
        v = 1.0 - a * a
        u_scr[:, cs] = jnp.where(v > 0.0, v * lax.rsqrt(v), 0.0) * (ig * xh)

    S = SUBLANES
    w = a_scr.shape[1]
    sidx = lax.broadcasted_iota(jnp.int32, (S, w), 0)

    def group(gi, c):
        g = (TM // S - 1 - gi) if reverse else gi
        r0 = pl.multiple_of(g * S, S)
        a = a_scr[pl.ds(r0, S), :]
        u = u_scr[pl.ds(r0, S), :]
        for sft in (1, 2, 4):
            if reverse:
                ok = sidx < S - sft
                a_e = pltpu.roll(a, S - sft, axis=0)
                u_e = pltpu.roll(u, S - sft, axis=0)
            else:
                ok = sidx >= sft
                a_e = pltpu.roll(a, sft, axis=0)
                u_e = pltpu.roll(u, sft, axis=0)
            u = jnp.where(ok, a * u_e + u, u)
            a = jnp.where(ok, a * a_e, a)
        hcur = a * carry[...] + u
        u_scr[pl.ds(r0, S), :] = hcur
        edge = 0 if reverse else S - 1
        carry[...] = jnp.broadcast_to(hcur[edge:edge + 1, :], (S, w))
        return c

    lax.fori_loop(0, TM // S, group, 0)
    o_ref[...] = u_scr[...].astype(BF16)


def _lru_scans(xraw, conv_w, conv_b, w_a, b_a, w_x, b_x, lam):
    n, w = xraw.shape
    nt = n // TM
    hb = TM // SUBLANES
    last8 = n // SUBLANES - 1
    scratch = [pltpu.VMEM((TM, w), F32), pltpu.VMEM((TM, w), F32), pltpu.VMEM((SUBLANES, w), F32)]
    gate_args = lambda d: (jnp.concatenate([w_a[d], w_x[d]], axis=-1).astype(BF16),
                           b_a[d].reshape(1, w), b_x[d].reshape(1, w), lam[d].reshape(1, w))
    gate_specs = [_full((LRU_HEADS, LRU_BLOCK, 2 * LRU_BLOCK)), _full((1, w)), _full((1, w)),
                  _full((1, w))]
    tile = lambda idx: pl.BlockSpec((TM, w), lambda j: (idx(j), 0))
    fwd = lambda j: j
    hf, xc = pl.pallas_call(
        functools.partial(_lru_fwd_kernel, n_tiles=nt),
        out_shape=(jax.ShapeDtypeStruct((n, w), BF16), jax.ShapeDtypeStruct((n, w), F32)),
        grid=(nt,),
        in_specs=[tile(fwd),
                  pl.BlockSpec((SUBLANES, w), lambda j: (jnp.maximum(j * hb - 1, 0), 0)),
                  pl.BlockSpec((SUBLANES, w), lambda j: (jnp.minimum((j + 1) * hb, last8), 0)),
                  _full((4, w)), _full((1, w))] + gate_specs,
        out_specs=(tile(fwd), tile(fwd)),
        scratch_shapes=scratch,
        compiler_params=_cparams("arbitrary"),
        name="rglru_fwd",
    )(xraw, xraw, xraw, conv_w, conv_b.reshape(1, w), *gate_args(0))
    rev = lambda j: _lru_tile_index(j, nt, True)
    hr = pl.pallas_call(
        _lru_rev_kernel,
        out_shape=jax.ShapeDtypeStruct((n, w), BF16),
        grid=(nt,),
        in_specs=[tile(rev)] + gate_specs,
        out_specs=tile(rev),
        scratch_shapes=scratch,
        compiler_params=_cparams("arbitrary"),
        name="rglru_rev",
    )(xc, *gate_args(1))
    return hf, hr


def _lru_out_kernel(gl_ref, hf_ref, hr_ref, s_ref, mod_ref, w_ref, out_ref):
    gate = _mod_rows(mod_ref, 1, 0, 0)[2]
    p = gl_ref[...].astype(F32) * (hf_ref[...].astype(F32) + hr_ref[...].astype(F32))
    out_ref[...] = s_ref[...] + gate * _bdot(p, w_ref[...])


def _lru_out(gl, hf, hr, s, mods, w_out):
    n, d = s.shape
    w = gl.shape[1]
    lat = lambda c: pl.BlockSpec((TM, c), lambda i: (i + 1, 0))
    return pl.pallas_call(
        _lru_out_kernel,
        out_shape=jax.ShapeDtypeStruct((n - TM, d), F32),
        grid=(n // TM - 1,),
        in_specs=[lat(w), lat(w), lat(w), lat(d), _full(mods.shape), _full(w_out.shape)],
        out_specs=pl.BlockSpec((TM, d), lambda i: (i, 0)),
        compiler_params=_cparams("arbitrary"),
        name="rglru_out",
    )(gl, hf, hr, s, mods, w_out.astype(BF16))


def _rglru_layer(pending, mods, norm_g, w_in, conv_w, conv_b, w_a, b_a, w_x, b_x, lam, w_out):
    s, gl, xraw = _lru_proj(pending, mods, norm_g, w_in)
    hf, hr = _lru_scans(xraw, conv_w, conv_b, w_a, b_a, w_x, b_x, lam)
    return _lru_out(gl, hf, hr, s, mods, w_out)


def _fn_proj_kernel(s_ref, mod_ref, g_ref, wt_ref, cs_ref, yr_ref, yi_ref, ar_scr, ai_scr):
    shift, scale, _ = _mod_rows(mod_ref, 1, 0, 0)
    nm = wt_ref.shape[0]
    gw = nm // FN_GROUPS
    per = TM // FFT_N2
    nj = FN_TB // FFT_N2
    csb = cs_ref[...].astype(BF16)

    def sub(tc, c):
        r0 = pl.multiple_of(tc * TM, TM)
        h = _rms(s_ref[pl.ds(r0, TM), :], g_ref[...]) * (1.0 + scale) + shift
        zt = lax.dot_general(wt_ref[...], h.astype(BF16), (((1,), (1,)), ((), ())),
                             preferred_element_type=F32).astype(BF16)
        for g in range(FN_GROUPS):
            y = jnp.dot(csb, zt[g * gw:(g + 1) * gw, :], preferred_element_type=F32)
            for q in range(per):
                row0 = pl.multiple_of((tc * per + q) * _slab_pitch(nm) + g * gw, SUBLANES)
                ar_scr[pl.ds(row0, gw), :] = y[:gw, q * FFT_N2:(q + 1) * FFT_N2]
                ai_scr[pl.ds(row0, gw), :] = y[gw:, q * FFT_N2:(q + 1) * FFT_N2]
        return c

    lax.fori_loop(0, FN_TB // TM, sub, 0)

    def relayout(m, c):
        yr_ref[m] = ar_scr[pl.ds(m, nj, stride=_slab_pitch(nm)), :]
        yi_ref[m] = ai_scr[pl.ds(m, nj, stride=_slab_pitch(nm)), :]
        return c

    lax.fori_loop(0, nm, relayout, 0, unroll=8)


def _slab_pitch(rows):
    return rows + SUBLANES


def _dft_cos_sin(n, scale):
    k = np.arange(n, dtype=np.int64)
    ang = 2.0 * np.pi * ((k[:, None] * k[None, :]) % n).astype(np.float64) / n
    return np.cos(ang) * scale, np.sin(ang) * scale


def _fn_proj(s, mods, norm_g, w_in):
    t, d = s.shape
    nm = w_in.shape[1]
    gw = nm // FN_GROUPS
    n1 = t // FFT_N2
    nj = FN_TB // FFT_N2
    c, sn = _dft_cos_sin(gw, gw ** -0.5)
    cs = jnp.asarray(np.concatenate([c, -sn], axis=0), F32)
    yspec = pl.BlockSpec((nm, nj, FFT_N2), lambda i: (0, i, 0))
    yshape = jax.ShapeDtypeStruct((nm, n1, FFT_N2), F32)
    return pl.pallas_call(
        _fn_proj_kernel,
        out_shape=(yshape, yshape),
        grid=(t // FN_TB,),
        in_specs=[pl.BlockSpec((FN_TB, d), lambda i: (i, 0)), _full(mods.shape), _full((1, d)),
                  _full((nm, d)), _full(cs.shape)],
        out_specs=(yspec, yspec),
        scratch_shapes=[pltpu.VMEM((nj * _slab_pitch(nm), FFT_N2), F32),
                        pltpu.VMEM((nj * _slab_pitch(nm), FFT_N2), F32)],
        compiler_params=_cparams("arbitrary"),
        name="fourier_proj",
    )(s, mods, norm_g.reshape(1, d), w_in.T.astype(BF16), cs)


def _fn_fft_kernel(yr_ref, yi_ref, m_ref, tc_ref, ts_ref, d_ref, o_ref):
    n1 = yr_ref.shape[1]
    n2 = FFT_N2
    xr = jnp.concatenate([yr_ref[m].astype(BF16) for m in range(FN_CB)], axis=1)
    xi = jnp.concatenate([yi_ref[m].astype(BF16) for m in range(FN_CB)], axis=1)
    a = jnp.dot(m_ref[...].astype(BF16), jnp.concatenate([xr, xi], axis=0),
                preferred_element_type=F32)
    ar = a[:n1]
    ai = a[n1:]
    tc = jnp.concatenate([tc_ref[...]] * FN_CB, axis=1)
    ts = jnp.concatenate([ts_ref[...]] * FN_CB, axis=1)
    br = ar * tc + ai * ts
    bi = ai * tc - ar * ts
    bst = jnp.concatenate(
        [jnp.concatenate([br[:, m * n2:(m + 1) * n2], bi[:, m * n2:(m + 1) * n2]], axis=1)
         for m in range(FN_CB)], axis=0).astype(BF16)
    res = lax.dot_general(d_ref[...].astype(BF16), bst, (((1,), (1,)), ((), ())),
                          preferred_element_type=F32)
    for m in range(FN_CB):
        o_ref[m] = res[:, m * n1:(m + 1) * n1]


def _fn_fft(yr, yi):
    nm, n1, n2 = yr.shape
    t = n1 * n2
    c, sn = _dft_cos_sin(n1, n1 ** -0.5)
    m = jnp.asarray(np.block([[c, sn], [-sn, c]]), F32)
    k1 = np.arange(n1, dtype=np.int64)[:, None]
    t2 = np.arange(n2, dtype=np.int64)[None, :]
    ang = 2.0 * np.pi * ((k1 * t2) % t).astype(np.float64) / t
    tc = jnp.asarray(np.cos(ang), F32)
    ts = jnp.asarray(np.sin(ang), F32)
    c2, s2 = _dft_cos_sin(n2, n2 ** -0.5)
    dm = jnp.asarray(np.concatenate([c2, s2], axis=1), F32)
    yspec = pl.BlockSpec((FN_CB, n1, n2), lambda i: (i, 0, 0))
    return pl.pallas_call(
        _fn_fft_kernel,
        out_shape=jax.ShapeDtypeStruct((nm, n2, n1), F32),
        grid=(nm // FN_CB,),
        in_specs=[yspec, yspec, _full(m.shape), _full(tc.shape), _full(ts.shape), _full(dm.shape)],
        out_specs=pl.BlockSpec((FN_CB, n2, n1), lambda i: (i, 0, 0)),
        compiler_params=_cparams("arbitrary"),
        name="fourier_fft",
    )(yr, yi, m, tc, ts, dm)


def _fn_out_kernel(ft_ref, w_ref, s_ref, mod_ref, o_ref, a_scr):
    gate = _mod_rows(mod_ref, 1, 0, 0)[2]
    nm, nj, n1 = ft_ref.shape

    def relayout(m, c):
        a_scr[pl.ds(m, nj, stride=_slab_pitch(nm)), :] = ft_ref[m]
        return c

    lax.fori_loop(0, nm, relayout, 0, unroll=8)
    for j in range(nj):
        p0 = j * _slab_pitch(nm)
        slab = a_scr[p0:p0 + nm, :].astype(BF16)
        y = lax.dot_general(slab, w_ref[...], (((0,), (0,)), ((), ())), preferred_element_type=F32)
        rows = slice(j * n1, (j + 1) * n1)
        o_ref[rows, :] = s_ref[rows, :] + gate * y


def _fn_out(ft, s, mods, w_out):
    t, d = s.shape
    nm, n2, n1 = ft.shape
    nj = FN_TB // n1
    tok = pl.BlockSpec((FN_TB, d), lambda i: (i, 0))
    return pl.pallas_call(
        _fn_out_kernel,
        out_shape=jax.ShapeDtypeStruct((t, d), F32),
        grid=(t // FN_TB,),
        in_specs=[pl.BlockSpec((nm, nj, n1), lambda i: (0, i, 0)), _full(w_out.shape), tok,
                  _full(mods.shape)],
        out_specs=tok,
        scratch_shapes=[pltpu.VMEM((nj * _slab_pitch(nm), n1), F32)],
        compiler_params=_cparams("arbitrary"),
        name="fourier_out",
    )(ft, w_out.astype(BF16), s, mods)


def _fourier_layer(s, mods, norm_g, w_in, w_out):
    yr, yi = _fn_proj(s, mods, norm_g, w_in)
    return _fn_out(_fn_fft(yr, yi), s, mods, w_out)


def kernel(x, c, ctx, c_ctx, ada_w, ada_b, norm_mix_g, norm_ffn_g, final_norm_g, router_group_w, router_group_b, router_expert_w, router_expert_b, expert_w_gate, expert_w_up, expert_w_down, cm_w_in, cm_v_norm_g, cm_w_s, cm_b_s, cm_w_out, ml_w_in, ml_conv_w, ml_conv_b, ml_gate_b, ml_norm_g, ml_w_out, lru_w_in, lru_conv_w, lru_conv_b, lru_w_a, lru_b_a, lru_w_x, lru_b_x, lru_lambda, lru_w_out, fn_w_in, fn_w_out):
    bsz, seq, d = x.shape
    assert bsz == 1 and ada_w.shape[0] == 4 and ctx.shape[1] == TM
    c_rows = jnp.concatenate([c_ctx[None, :], c, jnp.zeros((SUBLANES - 2, d), F32)], axis=0)
    mods = _ada_table(c_rows, ada_w, ada_b)

    def moe(s, i, ctx_tiles, final_norm=False, defer=False):
        return _moe_layer(s, mods[i], norm_ffn_g[i], router_group_w[i], router_group_b[i],
                          router_expert_w[i], router_expert_b[i], expert_w_gate, expert_w_up,
                          expert_w_down, i, ctx_tiles, final_norm_g, final_norm, defer)

    s = _chunk_mlp_layer(x[0], ctx[0], mods[0], norm_mix_g[0], cm_w_in[0], cm_v_norm_g[0],
                         cm_w_s[0], cm_b_s[0], cm_w_out[0])
    s = moe(s, 0, 1, defer=True)
    s = _mlstm_layer(s, mods[1], norm_mix_g[1], ml_w_in[0], ml_conv_w[0], ml_conv_b[0],
                     ml_gate_b[0], ml_norm_g[0], ml_w_out[0])
    s = moe(s, 1, 1, defer=True)
    s = _rglru_layer(s, mods[2], norm_mix_g[2], lru_w_in[0], lru_conv_w[0], lru_conv_b[0],
                     lru_w_a[0], lru_b_a[0], lru_w_x[0], lru_b_x[0], lru_lambda[0], lru_w_out[0])
    s = moe(s, 2, 0)
    s = _fourier_layer(s, mods[3], norm_mix_g[3], fn_w_in[0], fn_w_out[0])
    s = moe(s, 3, 0, final_norm=True)
    return s[None]
```

```python
import functools
import math
from typing import NamedTuple

import jax
import jax.numpy as jnp
import numpy as np
from jax import lax
from jax.experimental import pallas as pl
from jax.experimental.pallas import tpu as pltpu

F32 = jnp.float32
BF16 = jnp.bfloat16

EPS = 1e-6
POS_BASE = 10000.0
GRID_W = 64
N_MOD = 6
TM = 256
LANES = 128
SUBLANES = 8
VMEM_LIMIT = 56 * 1024 * 1024

CM_CHUNK = 128
CM_GROUPS = 4
ML_HEADS = 4
ML_DK = 128
ML_DV = 256
ML_CHUNK = 128
LRU_HEADS = 10
LRU_BLOCK = 128
LRU_C = 8.0
FN_GROUPS = 4
FFT_N2 = 128
MOE_GROUPS = 4
MOE_EPG = 8
MOE_EXPERTS = MOE_GROUPS * MOE_EPG
MOE_ROWS_LOG2 = 8
MOE_ROWS = 1 << MOE_ROWS_LOG2
ROUTE_ROWS = 40
FN_TB = 1024
FN_CB = 32
CONV_LEFT = 2

HI = lax.Precision.HIGHEST


def _cparams(*sem):
    return pltpu.CompilerParams(dimension_semantics=sem, vmem_limit_bytes=VMEM_LIMIT)


def _full(shape):
    nd = len(shape)
    return pl.BlockSpec(shape, lambda *_: (0,) * nd)


def _rms(x, g):
    return x * lax.rsqrt(jnp.mean(x * x, axis=-1, keepdims=True) + EPS) * g


def _gelu(x):
    c = math.sqrt(2.0 / math.pi)
    return 0.5 * x * (1.0 + jnp.tanh(c * (x + 0.044715 * (x * x * x))))


def _sigmoid(x):
    return 0.5 * jnp.tanh(0.5 * x) + 0.5


def _silu(x):
    return x * _sigmoid(x)


def _softplus(x):
    return jnp.maximum(x, 0.0) + jnp.log(1.0 + jnp.exp(-jnp.abs(x)))


def _mod_rows(mod_ref, tile, ctx_tiles, first):
    row = jnp.where(tile < ctx_tiles, 0, 1)
    m = mod_ref[pl.ds(row, 1), :]
    d = m.shape[1] // N_MOD
    return tuple(m[:, (first + j) * d:(first + j + 1) * d] for j in range(3))


def _bdot(a, b):
    return jnp.dot(a.astype(BF16), b.astype(BF16), preferred_element_type=F32)


def _split_bf16(x):
    hi = x.astype(BF16)
    return hi, (x - hi.astype(F32)).astype(BF16)


def _dot3(a, b, dims):
    a_hi, a_lo = _split_bf16(a)
    b_hi, b_lo = _split_bf16(b)
    dg = functools.partial(lax.dot_general, dimension_numbers=(dims, ((), ())),
                           preferred_element_type=F32)
    return dg(a_hi, b_hi) + dg(a_hi, b_lo) + dg(a_lo, b_hi)


def _ada_kernel(c_ref, w_ref, b_ref, o_ref):
    c = c_ref[...]
    o_ref[...] = _dot3(_silu(c), w_ref[...], ((1,), (0,))) + b_ref[...]


def _ada_table(c_rows, ada_w, ada_b):
    depth, d, n = ada_w.shape
    tn = 2048
    return pl.pallas_call(
        _ada_kernel,
        out_shape=jax.ShapeDtypeStruct((depth, SUBLANES, n), F32),
        grid=(depth, n // tn),
        in_specs=[_full((SUBLANES, d)),
                  pl.BlockSpec((None, d, tn), lambda i, j: (i, 0, j)),
                  pl.BlockSpec((None, 1, tn), lambda i, j: (i, 0, j))],
        out_specs=pl.BlockSpec((None, SUBLANES, tn), lambda i, j: (i, 0, j)),
        compiler_params=_cparams("arbitrary", "arbitrary"),
        name="ada_table",
    )(c_rows, ada_w, ada_b.reshape(depth, 1, n))


def _pos_tables(seq, d):
    q = d // 4
    freq = jnp.exp(-math.log(POS_BASE) * jnp.arange(q, dtype=F32) / q)
    ar = jnp.arange(seq // GRID_W, dtype=F32)[:, None] * freq
    ac = jnp.arange(GRID_W, dtype=F32)[:, None] * freq
    return (jnp.concatenate([jnp.sin(ar), jnp.cos(ar)], axis=-1),
            jnp.concatenate([jnp.sin(ac), jnp.cos(ac)], axis=-1))


def _cm_kernel(x_ref, ctx_ref, rt_ref, ct_ref, mod_ref, g_ref, win_ref, vg_ref, ws_ref, bs_ref,
               wout_ref, o_ref, p_scr, x_scr):
    i = pl.program_id(0)

    @pl.when(i == 0)
    def _():
        x_scr[...] = ctx_ref[...]

    @pl.when(i > 0)
    def _():
        rows_per_tile = TM // GRID_W
        q2 = rt_ref.shape[1]
        r0 = (i - 1) * rows_per_tile
        rt = jnp.concatenate(
            [jnp.broadcast_to(rt_ref[pl.ds(r0 + j, 1), :], (GRID_W, q2))
             for j in range(rows_per_tile)], axis=0)
        ct = jnp.concatenate([ct_ref[...]] * rows_per_tile, axis=0)
        x_scr[...] = x_ref[...] + jnp.concatenate([rt, ct], axis=1)

    shift, scale, gate = _mod_rows(mod_ref, i, 1, 0)
    x = x_scr[...]
    h = _rms(x, g_ref[...]) * (1.0 + scale) + shift
    z = _gelu(_bdot(h, win_ref[...]))
    w = z.shape[1] // 2
    u = z[:, :w]
    v = _rms(z[:, w:], vg_ref[...]).astype(BF16)
    gw = w // CM_GROUPS
    for c in range(TM // CM_CHUNK):
        r = slice(c * CM_CHUNK, (c + 1) * CM_CHUNK)
        for g in range(CM_GROUPS):
            cs = slice(g * gw, (g + 1) * gw)
            s = jnp.dot(ws_ref[g], v[r, cs], preferred_element_type=F32) + bs_ref[:, g:g + 1]
            p_scr[r, cs] = (u[r, cs] * s).astype(BF16)
    y = jnp.dot(p_scr[...], wout_ref[...], preferred_element_type=F32)
    o_ref[...] = x + gate * y


def _chunk_mlp_layer(x2, ctx2, mods, norm_g, w_in, v_g, w_s, b_s, w_out):
    seq, d = x2.shape
    n_ctx = ctx2.shape[0]
    assert n_ctx == TM and seq % TM == 0 and TM % GRID_W == 0
    n = n_ctx + seq
    w = w_out.shape[0]
    rt, ct = _pos_tables(seq, d)
    return pl.pallas_call(
        _cm_kernel,
        out_shape=jax.ShapeDtypeStruct((n, d), F32),
        grid=(n // TM,),
        in_specs=[pl.BlockSpec((TM, d), lambda i: (jnp.maximum(i - 1, 0), 0)),
                  _full((TM, d)), _full(rt.shape), _full(ct.shape),
                  _full(mods.shape), _full((1, d)), _full(w_in.shape), _full((1, w)),
                  _full(w_s.shape), _full((CM_CHUNK, CM_GROUPS)), _full(w_out.shape)],
        out_specs=pl.BlockSpec((TM, d), lambda i: (i, 0)),
        scratch_shapes=[pltpu.VMEM((TM, w), BF16), pltpu.VMEM((TM, d), F32)],
        compiler_params=_cparams("arbitrary"),
        name="chunk_mlp",
    )(x2, ctx2, rt, ct, mods, norm_g.reshape(1, d), w_in.astype(BF16), v_g.reshape(1, w),
      w_s.astype(BF16), b_s.T, w_out.astype(BF16))


def _store_token_tiles(ref, x):
    rows, d = x.shape
    for j in range(d // LANES):
        ref[pl.ds(j, rows, stride=d // LANES), :] = x[:, j * LANES:(j + 1) * LANES]


def _load_token_tiles(ref):
    chunks = SUBLANES
    rows = ref.shape[0] // chunks
    return jnp.concatenate([ref[pl.ds(j, rows, stride=chunks), :] for j in range(chunks)], axis=1)


def _route_tiles(nt):
    return next(k for k in (5, 4, 2, 1) if nt % k == 0)


def _router_kernel(s_ref, mod_ref, g_ref, rwt_ref, rbt_ref, tri_ref, h_ref, e1_ref, e2_ref, r1_ref,
                   r2_ref, wt_ref, cnt_ref, carry, *, ctx_rows):
    i = pl.program_id(0)
    rows, d = s_ref.shape

    @pl.when(i == 0)
    def _():
        carry[...] = jnp.zeros_like(carry)

    lat = mod_ref[1:2, :]
    shift, scale = lat[:, 3 * d:4 * d], lat[:, 4 * d:5 * d]
    if ctx_rows:
        ctx = mod_ref[0:1, :]
        is_ctx = (i == 0) & (lax.broadcasted_iota(jnp.int32, (rows, 1), 0) < ctx_rows)
        shift = jnp.where(is_ctx, ctx[:, 3 * d:4 * d], shift)
        scale = jnp.where(is_ctx, ctx[:, 4 * d:5 * d], scale)
    h = _rms(s_ref[...], g_ref[...]) * (1.0 + scale) + shift
    _store_token_tiles(h_ref, h)
    logits = _dot3(rwt_ref[...], h, ((1,), (1,))) + rbt_ref[...]
    row = lax.broadcasted_iota(jnp.int32, logits.shape, 0)
    neg = jnp.float32(-jnp.inf)
    big = jnp.int32(1 << 20)
    is_g = row < MOE_GROUPS
    gl = jnp.where(is_g, logits, neg)
    gmax = jnp.max(gl, axis=0, keepdims=True)
    grp = jnp.min(jnp.where(is_g & (gl == gmax), row, big), axis=0, keepdims=True)
    p_grp = 1.0 / jnp.sum(jnp.exp(gl - gmax), axis=0, keepdims=True)
    e_row = row - MOE_GROUPS
    in_grp = (e_row >= 0) & (e_row < MOE_EXPERTS) & ((e_row >> 3) == grp)
    l1 = jnp.where(in_grp, logits, neg)
    v1 = jnp.max(l1, axis=0, keepdims=True)
    i1 = jnp.min(jnp.where(in_grp & (l1 == v1), row, big), axis=0, keepdims=True)
    rest = in_grp & (row != i1)
    l2 = jnp.where(rest, logits, neg)
    v2 = jnp.max(l2, axis=0, keepdims=True)
    i2 = jnp.min(jnp.where(rest & (l2 == v2), row, big), axis=0, keepdims=True)
    e21 = jnp.exp(v2 - v1)
    w1 = p_grp / (1.0 + e21)
    w2 = p_grp * e21 / (1.0 + e21)
    oh1 = (row == i1).astype(F32)
    oh2 = (row == i2).astype(F32)
    oh = oh1 + oh2
    before = jnp.dot(oh.astype(BF16), tri_ref[...], preferred_element_type=F32) + carry[:, 0:1]
    r1_ref[...] = jnp.sum(oh1 * before, axis=0, keepdims=True).astype(jnp.int32)
    r2_ref[...] = jnp.sum(oh2 * before, axis=0, keepdims=True).astype(jnp.int32)
    e1_ref[...] = i1 - MOE_GROUPS
    e2_ref[...] = i2 - MOE_GROUPS
    carry[...] = carry[...] + jnp.sum(oh, axis=1, keepdims=True)
    cnt_ref[...] = carry[...]
    wt_ref[...] = jnp.concatenate([w1, w2, jnp.zeros((LANES - 2, rows), F32)], axis=0).T


def _router(s, mods, norm_g, rg_w, rg_b, re_w, re_b, ctx_tiles):
    n, d = s.shape
    rows = TM * _route_tiles(n // TM)
    steps = n // rows
    pad = ROUTE_ROWS - MOE_GROUPS - MOE_EXPERTS
    rwt = jnp.concatenate([rg_w, re_w, jnp.zeros((d, pad), F32)], axis=1).T
    rbt = jnp.broadcast_to(jnp.concatenate([rg_b, re_b, jnp.zeros((pad,), F32)])[:, None],
                           (ROUTE_ROWS, rows))
    tri = jnp.asarray(np.triu(np.ones((rows, rows), np.float32), 1), BF16)
    assert d == SUBLANES * LANES
    tile = pl.BlockSpec((rows, d), lambda i: (i, 0))
    irow = pl.BlockSpec((None, 1, rows), lambda i: (i, 0, 0))
    ishape = jax.ShapeDtypeStruct((steps, 1, rows), jnp.int32)
    return pl.pallas_call(
        functools.partial(_router_kernel, ctx_rows=ctx_tiles * TM),
        out_shape=(jax.ShapeDtypeStruct((n * SUBLANES, LANES), F32), ishape, ishape, ishape, ishape,
                   jax.ShapeDtypeStruct((n, LANES), F32),
                   jax.ShapeDtypeStruct((ROUTE_ROWS, LANES), F32)),
        grid=(steps,),
        in_specs=[tile, _full(mods.shape), _full((1, d)), _full((ROUTE_ROWS, d)),
                  _full((ROUTE_ROWS, rows)), _full((rows, rows))],
        out_specs=(pl.BlockSpec((rows * SUBLANES, LANES), lambda i: (i, 0)), irow, irow, irow, irow,
                   pl.BlockSpec((rows, LANES), lambda i: (i, 0)), _full((ROUTE_ROWS, LANES))),
        scratch_shapes=[pltpu.VMEM((ROUTE_ROWS, LANES), F32)],
        compiler_params=_cparams("arbitrary"),
        name="moe_router",
    )(s, mods, norm_g.reshape(1, d), rwt, rbt, tri)


def _finalize_kernel(cnt_ref, e1_ref, e2_ref, r1_ref, r2_ref, d1_ref, d2_ref, blk_ref):
    e1 = e1_ref[...]
    e2 = e2_ref[...]
    r1 = r1_ref[...]
    r2 = r2_ref[...]
    d1 = jnp.zeros_like(e1)
    d2 = jnp.zeros_like(e2)
    lane = lax.broadcasted_iota(jnp.int32, blk_ref.shape, 1)
    brow = lane * MOE_ROWS
    sub = lax.broadcasted_iota(jnp.int32, blk_ref.shape, 0)
    be = jnp.zeros(blk_ref.shape, jnp.int32)
    pend = jnp.zeros(blk_ref.shape, jnp.int32)
    ps = jnp.int32(0)
    for e in range(MOE_EXPERTS):
        c = cnt_ref[e]
        pe = ps + lax.shift_left(lax.shift_right_logical(c + (MOE_ROWS - 1), MOE_ROWS_LOG2),
                                 MOE_ROWS_LOG2)
        d1 = jnp.where(e1 == e, ps + r1, d1)
        d2 = jnp.where(e2 == e, ps + r2, d2)
        be = be + (brow >= pe).astype(jnp.int32)
        pend = jnp.where(lane == e, pe, pend)
        ps = pe
    d1_ref[...] = d1
    d2_ref[...] = d2
    n_used = lax.shift_right_logical(ps, MOE_ROWS_LOG2)
    blk_ref[...] = jnp.where(sub == 0, jnp.minimum(be, MOE_EXPERTS - 1),
                             jnp.where(sub == 1, pend, n_used))


def _finalize(counts, e1, e2, r1, r2, nb):
    nbp = (nb + LANES - 1) // LANES * LANES
    whole = pl.BlockSpec(e1.shape, lambda i, c: (0, 0, 0))
    ishape = jax.ShapeDtypeStruct(e1.shape, jnp.int32)
    return pl.pallas_call(
        _finalize_kernel,
        out_shape=(ishape, ishape, jax.ShapeDtypeStruct((SUBLANES, nbp), jnp.int32)),
        grid_spec=pltpu.PrefetchScalarGridSpec(
            num_scalar_prefetch=1,
            grid=(1,),
            in_specs=[whole, whole, whole, whole],
            out_specs=(whole, whole, pl.BlockSpec((SUBLANES, nbp), lambda i, c: (0, 0)))),
        compiler_params=_cparams("arbitrary"),
        name="moe_finalize",
    )(counts, e1, e2, r1, r2)


def _token_copy(src, r, dst, d, sem):
    return pltpu.make_async_copy(src.at[pl.ds(pl.multiple_of(r * SUBLANES, SUBLANES), SUBLANES), :],
                                 dst.at[pl.ds(pl.multiple_of(d * SUBLANES, SUBLANES), SUBLANES), :],
                                 sem)


def _zero_fill_padding(pend_ref, nu_ref, xs_out, zbuf, zsem):
    blk_rows = MOE_ROWS * SUBLANES
    nb = xs_out.shape[0] // blk_rows
    zbuf[...] = jnp.zeros_like(zbuf)

    def block_copy(b):
        r0 = pl.multiple_of(b * blk_rows, blk_rows)
        return pltpu.make_async_copy(zbuf, xs_out.at[pl.ds(r0, blk_rows), :], zsem)

    def seg_last_block(e):
        pe = pend_ref[e]
        prev = pend_ref[e - 1] if e > 0 else 0
        return pe > prev, lax.shift_right_logical(pe, MOE_ROWS_LOG2) - 1

    for e in range(MOE_EXPERTS):
        nonempty, b = seg_last_block(e)

        @pl.when(nonempty)
        def _():
            block_copy(b).start()

    def tail_start(b, c):
        block_copy(b).start()
        return c

    lax.fori_loop(nu_ref[0], nb, tail_start, 0)
    for e in range(MOE_EXPERTS):
        nonempty, b = seg_last_block(e)

        @pl.when(nonempty)
        def _():
            block_copy(b).wait()

    def tail_wait(b, c):
        block_copy(b).wait()
        return c

    lax.fori_loop(nu_ref[0], nb, tail_wait, 0)


def _dispatch_kernel(dest_ref, pend_ref, nu_ref, h_ref, xs_out, sem, zbuf, zsem, *, tiles):
    i = pl.program_id(0)

    @pl.when(i == 0)
    def _():
        _zero_fill_padding(pend_ref, nu_ref, xs_out, zbuf, zsem)

    rows = tiles * TM
    for q in range(tiles):
        base = i * (2 * rows) + q * TM

        def start(r, c):
            _token_copy(h_ref, q * TM + r, xs_out, dest_ref[base + r], sem).start(priority=0)
            _token_copy(h_ref, q * TM + r, xs_out, dest_ref[base + rows + r],
                        sem).start(priority=1)
            return c

        lax.fori_loop(0, TM, start, 0, unroll=8)
    for _ in range(2):
        pltpu.make_async_copy(h_ref, xs_out.at[pl.ds(0, tiles * TM * SUBLANES), :], sem).wait()


def _dispatch(dest, pad_end, n_used, h, n_rows):
    n = h.shape[0] // SUBLANES
    nt = n // TM
    tiles = _route_tiles(nt)
    return pl.pallas_call(
        functools.partial(_dispatch_kernel, tiles=tiles),
        out_shape=jax.ShapeDtypeStruct((n_rows * SUBLANES, LANES), F32),
        grid_spec=pltpu.PrefetchScalarGridSpec(
            num_scalar_prefetch=3,
            grid=(nt // tiles,),
            in_specs=[pl.BlockSpec((tiles * TM * SUBLANES, LANES), lambda i, *_: (i, 0))],
            out_specs=pl.BlockSpec(memory_space=pl.ANY),
            scratch_shapes=[pltpu.SemaphoreType.DMA, pltpu.VMEM((MOE_ROWS * SUBLANES, LANES), F32),
                            pltpu.SemaphoreType.DMA]),
        compiler_params=_cparams("arbitrary"),
        name="moe_dispatch",
    )(dest, pad_end, n_used, h)


X_SLOTS = 4


Y_SLOTS = 3


def _expert_kernel(be_ref, pend_ref, nu_ref, xs_hbm, wg_hbm, wu_hbm, wd_hbm, ys_hbm,
                   xbuf, ybuf, wg_f, wu_f, wd_f, wg_s, wu_s, wd_s, xsem, ysem, wsem, ord_ref,
                   *, layer):
    nu = nu_ref[0]
    blk_rows = MOE_ROWS * SUBLANES
    nb = ys_hbm.shape[0] // blk_rows

    def rows_of(ref, blk):
        return ref.at[pl.ds(pl.multiple_of(blk * blk_rows, blk_rows), blk_rows), :]

    def x_copy(blk, slot):
        return pltpu.make_async_copy(rows_of(xs_hbm, blk), xbuf.at[slot], xsem.at[slot])

    def y_copy(blk, slot):
        return pltpu.make_async_copy(ybuf.at[slot], rows_of(ys_hbm, blk), ysem.at[slot])

    def w_copies(e, slot):
        return (pltpu.make_async_copy(wg_hbm.at[layer, e], wg_f.at[slot], wsem.at[slot]),
                pltpu.make_async_copy(wu_hbm.at[layer, e], wu_f.at[slot], wsem.at[slot]),
                pltpu.make_async_copy(wd_hbm.at[layer, e], wd_f.at[slot], wsem.at[slot]))

    ord_ref[0] = 0
    for j in range(X_SLOTS - 1):
        @pl.when(j < nu)
        def _():
            x_copy(j, j).start()

    @pl.when(nu > 0)
    def _():
        for c in w_copies(be_ref[0], 0):
            c.start()

    def block(b, carry):
        ahead = b + (X_SLOTS - 1)

        @pl.when(ahead < nu)
        def _():
            x_copy(ahead, lax.rem(ahead, X_SLOTS)).start()

        e = be_ref[b]

        @pl.when((b == 0) | (e != be_ref[jnp.maximum(b - 1, 0)]))
        def _():
            k = ord_ref[0]
            slot = lax.rem(k, 2)
            for c in w_copies(e, slot):
                c.wait()
            wg_s[...] = wg_f[slot].astype(BF16)
            wu_s[...] = wu_f[slot].astype(BF16)
            wd_s[...] = wd_f[slot].astype(BF16)
            nxt = lax.shift_right_logical(pend_ref[e], MOE_ROWS_LOG2)

            @pl.when(nxt < nu)
            def _():
                for c in w_copies(be_ref[nxt], 1 - slot):
                    c.start(priority=1)

            ord_ref[0] = k + 1

        slot = lax.rem(b, X_SLOTS)
        yslot = lax.rem(b, Y_SLOTS)
        x_copy(b, slot).wait()

        @pl.when(b >= Y_SLOTS)
        def _():
            y_copy(b - Y_SLOTS, yslot).wait()

        x = _load_token_tiles(xbuf.at[slot]).astype(BF16)
        a = jnp.dot(x, wg_s[...], preferred_element_type=F32)
        u = jnp.dot(x, wu_s[...], preferred_element_type=F32)
        _store_token_tiles(ybuf.at[yslot], jnp.dot((_silu(a) * u).astype(BF16), wd_s[...],
                                                   preferred_element_type=F32))
        y_copy(b, yslot).start()
        return carry

    lax.fori_loop(0, nu, block, 0)
    for j in range(1, Y_SLOTS + 1):
        @pl.when(nu >= j)
        def _():
            y_copy(nu - j, lax.rem(nu - j, Y_SLOTS)).wait()

    ybuf[0] = jnp.zeros(ybuf.shape[1:], ybuf.dtype)

    def tail_start(b, c):
        y_copy(b, 0).start()
        return c

    def tail_wait(b, c):
        y_copy(b, 0).wait()
        return c

    lax.fori_loop(nu, nb, tail_start, 0)
    lax.fori_loop(nu, nb, tail_wait, 0)


def _experts(blk_expert, pad_end, n_used, xs, w_gate, w_up, w_down, layer):
    d, hid = w_gate.shape[2:]
    blk_rows = MOE_ROWS * SUBLANES
    hbm = pl.BlockSpec(memory_space=pl.ANY)
    return pl.pallas_call(
        functools.partial(_expert_kernel, layer=layer),
        out_shape=jax.ShapeDtypeStruct(xs.shape, F32),
        grid_spec=pltpu.PrefetchScalarGridSpec(
            num_scalar_prefetch=3,
            grid=(1,),
            in_specs=[hbm, hbm, hbm, hbm],
            out_specs=hbm,
            scratch_shapes=[pltpu.VMEM((X_SLOTS, blk_rows, LANES), F32),
                            pltpu.VMEM((Y_SLOTS, blk_rows, LANES), F32),
                            pltpu.VMEM((2, d, hid), F32), pltpu.VMEM((2, d, hid), F32),
                            pltpu.VMEM((2, hid, d), F32),
                            pltpu.VMEM((d, hid), BF16), pltpu.VMEM((d, hid), BF16),
                            pltpu.VMEM((hid, d), BF16),
                            pltpu.SemaphoreType.DMA((X_SLOTS,)), pltpu.SemaphoreType.DMA((Y_SLOTS,)),
                            pltpu.SemaphoreType.DMA((2,)), pltpu.SMEM((1,), jnp.int32)]),
        compiler_params=_cparams("arbitrary"),
        name="moe_experts",
    )(blk_expert, pad_end, n_used, xs, w_gate, w_up, w_down)


def _combined_tile(dest_ref, s_ref, wt_ref, mod_ref, ys_ref, ybuf, sem, *, ctx_tiles, tiles):
    i = pl.program_id(0)
    slot = i % 2
    rows = tiles * TM

    def gather(tile, slot):
        tile = jnp.asarray(tile, jnp.int32)
        base = lax.div(tile, tiles) * (2 * rows) + lax.rem(tile, tiles) * TM

        def start(r, c):
            _token_copy(ys_ref, dest_ref[base + r], ybuf.at[slot, 0], r,
                        sem.at[slot]).start(priority=0)
            _token_copy(ys_ref, dest_ref[base + rows + r], ybuf.at[slot, 1], r,
                        sem.at[slot]).start(priority=1)
            return c

        lax.fori_loop(0, TM, start, 0, unroll=8)

    @pl.when(i == 0)
    def _():
        gather(0, 0)

    @pl.when(i + 1 < pl.num_programs(0))
    def _():
        gather(i + 1, 1 - slot)

    for k in range(2):
        pltpu.make_async_copy(ys_ref.at[pl.ds(0, TM * SUBLANES), :], ybuf.at[slot, k],
                              sem.at[slot]).wait()
    gate = _mod_rows(mod_ref, i, ctx_tiles, 3)[2]
    wt = wt_ref[...]
    y = (wt[:, 0:1] * _load_token_tiles(ybuf.at[slot, 0])
         + wt[:, 1:2] * _load_token_tiles(ybuf.at[slot, 1]))
    return s_ref[...] + gate * y


def _combine_scratch():
    return [pltpu.VMEM((2, 2, TM * SUBLANES, LANES), F32), pltpu.SemaphoreType.DMA((2,))]


def _combine_kernel(dest_ref, s_ref, wt_ref, mod_ref, fg_ref, ys_ref, o_ref, ybuf, sem,
                    *, final_norm, **kw):
    out = _combined_tile(dest_ref, s_ref, wt_ref, mod_ref, ys_ref, ybuf, sem, **kw)
    if final_norm:
        out = _rms(out, fg_ref[...])
    o_ref[...] = out


def _combine(dest, s, wts, mods, final_g, ys, ctx_tiles, final_norm):
    n, d = s.shape
    return pl.pallas_call(
        functools.partial(_combine_kernel, ctx_tiles=ctx_tiles, final_norm=final_norm,
                          tiles=_route_tiles(n // TM)),
        out_shape=jax.ShapeDtypeStruct((n, d), F32),
        grid_spec=pltpu.PrefetchScalarGridSpec(
            num_scalar_prefetch=1,
            grid=(n // TM,),
            in_specs=[pl.BlockSpec((TM, d), lambda i, dst: (i, 0)),
                      pl.BlockSpec((TM, LANES), lambda i, dst: (i, 0)),
                      pl.BlockSpec(mods.shape, lambda i, dst: (0, 0)),
                      pl.BlockSpec((1, d), lambda i, dst: (0, 0)),
                      pl.BlockSpec(memory_space=pl.ANY)],
            out_specs=pl.BlockSpec((TM, d), lambda i, dst: (i, 0)),
            scratch_shapes=_combine_scratch()),
        compiler_params=_cparams("arbitrary"),
        name="moe_combine",
    )(dest, s, wts, mods, final_g.reshape(1, d), ys)


def _moe_layer(s, mods, norm_g, rg_w, rg_b, re_w, re_b, w_gate, w_up, w_down, layer, ctx_tiles,
               final_g, final_norm, defer=False):
    n, d = s.shape
    h, e1, e2, r1, r2, wts, cnt = _router(s, mods, norm_g, rg_w, rg_b, re_w, re_b, ctx_tiles)
    counts = cnt[MOE_GROUPS:MOE_GROUPS + MOE_EXPERTS, 0].astype(jnp.int32)
    nb = (2 * n + MOE_EXPERTS * (MOE_ROWS - 1)) // MOE_ROWS + 1
    d1, d2, blk = _finalize(counts, e1, e2, r1, r2, nb)
    dest = jnp.concatenate([d1, d2], axis=1).reshape(2 * n)
    n_used = blk[2, :1]
    pad_end = blk[1, :MOE_EXPERTS]
    xs = _dispatch(dest, pad_end, n_used, h, nb * MOE_ROWS)
    ys = _experts(blk[0, :nb], pad_end, n_used, xs, w_gate, w_up, w_down, layer)
    if defer:
        return _PendingCombine(dest, s, wts, mods, ys, ctx_tiles)
    return _combine(dest, s, wts, mods, final_g, ys, ctx_tiles, final_norm)


class _PendingCombine(NamedTuple):
    dest: jax.Array
    s: jax.Array
    wts: jax.Array
    mods: jax.Array
    ys: jax.Array
    ctx_tiles: int


def _pending_operands(p):
    n, d = p.s.shape
    kw = dict(ctx_tiles=p.ctx_tiles, tiles=_route_tiles(n // TM))
    specs = [pl.BlockSpec((TM, d), lambda i, dst: (i, 0)),
             pl.BlockSpec((TM, LANES), lambda i, dst: (i, 0)),
             pl.BlockSpec(p.mods.shape, lambda i, dst: (0, 0)),
             pl.BlockSpec(memory_space=pl.ANY)]
    return kw, p.dest, specs, (p.s, p.wts, p.mods, p.ys), _combine_scratch()


def _conv_tile(x, prev_ref, next_ref, has_prev, has_next, w_ref, b_ref):
    rows = x.shape[0]
    S = SUBLANES
    ridx = lax.broadcasted_iota(jnp.int32, (S, x.shape[1]), 0)
    pm = jnp.where(has_prev, 1.0, 0.0)
    nm = jnp.where(has_next, 1.0, 0.0)
    p2 = prev_ref[S - 2:S - 1, :] * pm
    p1 = prev_ref[S - 1:S, :] * pm
    n1 = next_ref[0:1, :] * nm

    def fix_head(rolled, head):
        return jnp.concatenate([head(rolled[:S]), rolled[S:]], axis=0)

    xm1 = fix_head(pltpu.roll(x, 1, axis=0), lambda g: jnp.where(ridx == 0, p1, g))
    xm2 = fix_head(pltpu.roll(x, 2, axis=0),
                   lambda g: jnp.where(ridx == 0, p2, jnp.where(ridx == 1, p1, g)))
    xp1 = pltpu.roll(x, rows - 1, axis=0)
    xp1 = jnp.concatenate([xp1[:rows - S], jnp.where(ridx == S - 1, n1, xp1[rows - S:])], axis=0)
    return (xm2 * w_ref[0:1, :] + xm1 * w_ref[1:2, :] + x * w_ref[2:3, :]
            + xp1 * w_ref[3:4, :] + b_ref[...])


def _ml_proj_kernel(dest_ref, s_ref, wt_ref, pmod_ref, ys_ref, mod_ref, g_ref, w_ref, wg_ref,
                    gb_ref, snew_ref, qk_ref, v_ref, o_ref, gt_ref, ybuf, sem, **pending_kw):
    i = pl.program_id(0)
    x = _combined_tile(dest_ref, s_ref, wt_ref, pmod_ref, ys_ref, ybuf, sem, **pending_kw)
    snew_ref[...] = x
    shift, scale, _ = _mod_rows(mod_ref, i, 1, 0)
    h = _rms(x, g_ref[...]) * (1.0 + scale) + shift
    z = _bdot(h, w_ref[...])
    nqk = qk_ref.shape[1]
    nv = v_ref.shape[1]
    qk_ref[...] = z[:, :nqk]
    v_ref[...] = z[:, nqk:nqk + nv].astype(BF16)
    o_ref[...] = z[:, nqk + nv:].astype(BF16)
    pre = _dot3(h, wg_ref[...], ((1,), (0,))) + gb_ref[...]
    lane = lax.broadcasted_iota(jnp.int32, pre.shape, 1)
    is_forget = ((lane >> 2) & 1) == 1
    gt_ref[...] = jnp.where(is_forget, -_softplus(-pre), pre)


def _ml_proj(pending, mods, norm_g, w_in, gate_b):
    n, d = pending.s.shape
    kw, dest, p_specs, p_args, p_scratch = _pending_operands(pending)
    nqk = 2 * ML_HEADS * ML_DK
    nv = ML_HEADS * ML_DV
    n_main = nqk + 2 * nv
    n_gate = w_in.shape[1] - n_main
    w_main = w_in[:, :n_main].astype(BF16)
    w_gate = jnp.concatenate([w_in[:, n_main:], jnp.zeros((d, LANES - n_gate), F32)], axis=1)
    gb = jnp.concatenate([gate_b.reshape(n_gate), jnp.zeros((LANES - n_gate,), F32)]).reshape(1, LANES)
    tile = lambda w: pl.BlockSpec((TM, w), lambda i, dst: (i, 0))
    whole = lambda shape: pl.BlockSpec(shape, lambda i, dst: (0,) * len(shape))
    return pl.pallas_call(
        functools.partial(_ml_proj_kernel, **kw),
        out_shape=(jax.ShapeDtypeStruct((n, d), F32),
                   jax.ShapeDtypeStruct((n, nqk), F32), jax.ShapeDtypeStruct((n, nv), BF16),
                   jax.ShapeDtypeStruct((n, nv), BF16), jax.ShapeDtypeStruct((n, LANES), F32)),
        grid_spec=pltpu.PrefetchScalarGridSpec(
            num_scalar_prefetch=1,
            grid=(n // TM,),
            in_specs=p_specs + [whole(mods.shape), whole((1, d)), whole(w_main.shape),
                                whole((d, LANES)), whole((1, LANES))],
            out_specs=(tile(d), tile(nqk), tile(nv), tile(nv), tile(LANES)),
            scratch_shapes=p_scratch),
        compiler_params=_cparams("arbitrary"),
        name="mlstm_proj",
    )(dest, *p_args, mods, norm_g.reshape(1, d), w_main, w_gate, gb)


def _ml_chunk_index(j, n_chunks, ctx_chunks, reverse):
    if not reverse:
        return j
    return jnp.where(j < ctx_chunks, ctx_chunks - 1 - j, n_chunks - 1 + ctx_chunks - j)


def _ml_rec_kernel(qk_ref, qkp_ref, qkn_ref, v_ref, gt_ref, gtt_ref, cw_ref, cb_ref, o_ref,
                   c_scr, n_scr, m_scr, *, reverse, n_chunks, ctx_chunks):
    j = pl.program_id(0)
    c = _ml_chunk_index(j, n_chunks, ctx_chunks, reverse)

    @pl.when(j == 0)
    def _():
        c_scr[...] = jnp.zeros_like(c_scr)
        n_scr[...] = jnp.zeros_like(n_scr)
        m_scr[...] = jnp.zeros_like(m_scr)

    has_prev = (c != 0) & (c != ctx_chunks)
    has_next = (c != ctx_chunks - 1) & (c != n_chunks - 1)
    qk = _silu(_conv_tile(qk_ref[...], qkp_ref, qkn_ref, has_prev, has_next, cw_ref, cb_ref))
    L = ML_CHUNK
    ri = lax.broadcasted_iota(jnp.int32, (L, L), 0)
    ci = lax.broadcasted_iota(jnp.int32, (L, L), 1)
    past = (ci >= ri) if reverse else (ci <= ri)
    pastf = past.astype(F32)
    gt = gt_ref[...]
    gtt = gtt_ref[...]
    b_col = jnp.dot(pastf, gt, precision=HI, preferred_element_type=F32)
    b_row = jnp.dot(gtt, pastf.T, precision=HI, preferred_element_type=F32)
    last = 0 if reverse else L - 1
    dbase = 8 if reverse else 0
    nq = ML_HEADS * ML_DK
    for hd in range(ML_HEADS):
        cl = dbase + hd
        cf = dbase + 4 + hd
        q = qk[:, hd * ML_DK:(hd + 1) * ML_DK] * (ML_DK ** -0.5)
        k = qk[:, nq + hd * ML_DK:nq + (hd + 1) * ML_DK]
        v = v_ref[:, hd * ML_DV:(hd + 1) * ML_DV]
        li_c = gt[:, cl:cl + 1]
        li_r = gtt[cl:cl + 1, :]
        b_c = b_col[:, cf:cf + 1]
        b_r = b_row[cf:cf + 1, :]
        g = b_r[:, last:last + 1]
        m0 = m_scr[hd:hd + 1, 0:1]
        c0 = c_scr[hd]
        n0 = n_scr[hd:hd + 1, :]
        a_c = g - b_c + li_c
        a_r = g - b_r + li_r
        m_loc = jnp.max(a_r, axis=-1, keepdims=True)
        inter = b_c + m0
        dlog = jnp.where(past, b_c - b_r + li_r, -jnp.inf)
        m = jnp.maximum(inter, jnp.max(dlog, axis=-1, keepdims=True))
        qb = q.astype(BF16)
        sc = lax.dot_general(qb, k.astype(BF16), (((1,), (1,)), ((), ())),
                             preferred_element_type=F32) * jnp.exp(dlog - m)
        w_inter = jnp.exp(inter - m)
        num = (jnp.dot(sc.astype(BF16), v, preferred_element_type=F32)
               + w_inter * jnp.dot(qb, c0.astype(BF16), preferred_element_type=F32))
        den = (jnp.sum(sc, axis=-1, keepdims=True)
               + w_inter * jnp.sum(q * n0, axis=-1, keepdims=True))
        o_ref[:, hd * ML_DV:(hd + 1) * ML_DV] = (
            num / jnp.maximum(jnp.abs(den), jnp.exp(-m))).astype(BF16)
        m_new = jnp.maximum(g + m0, m_loc)
        dec = jnp.exp(g + m0 - m_new)
        scl = jnp.exp(m_loc - m_new)
        kw = k * jnp.exp(a_c - m_loc)
        c_scr[hd] = dec * c0 + scl * jnp.dot(kw.T.astype(BF16), v, preferred_element_type=F32)
        n_scr[hd:hd + 1, :] = dec * n0 + scl * jnp.sum(kw, axis=0, keepdims=True)
        m_scr[hd:hd + 1, :] = jnp.broadcast_to(m_new, (1, LANES))


def _ml_rec(qk, v, gt, gtt, conv_w, conv_b, reverse):
    n, nqk = qk.shape
    nv = v.shape[1]
    L = ML_CHUNK
    nc = n // L
    cc = TM // L
    hb = L // SUBLANES
    idx = lambda j: _ml_chunk_index(j, nc, cc, reverse)
    last8 = n // SUBLANES - 1
    return pl.pallas_call(
        functools.partial(_ml_rec_kernel, reverse=reverse, n_chunks=nc, ctx_chunks=cc),
        out_shape=jax.ShapeDtypeStruct((n, nv), BF16),
        grid=(nc,),
        in_specs=[pl.BlockSpec((L, nqk), lambda j: (idx(j), 0)),
                  pl.BlockSpec((SUBLANES, nqk), lambda j: (jnp.maximum(idx(j) * hb - 1, 0), 0)),
                  pl.BlockSpec((SUBLANES, nqk), lambda j: (jnp.minimum((idx(j) + 1) * hb, last8), 0)),
                  pl.BlockSpec((L, nv), lambda j: (idx(j), 0)),
                  pl.BlockSpec((L, LANES), lambda j: (idx(j), 0)),
                  pl.BlockSpec((2 * SUBLANES, L), lambda j: (0, idx(j))),
                  _full((4, nqk)), _full((1, nqk))],
        out_specs=pl.BlockSpec((L, nv), lambda j: (idx(j), 0)),
        scratch_shapes=[pltpu.VMEM((ML_HEADS, ML_DK, ML_DV), F32),
                        pltpu.VMEM((SUBLANES, ML_DK), F32),
                        pltpu.VMEM((SUBLANES, LANES), F32)],
        compiler_params=_cparams("arbitrary"),
        name="mlstm_rev" if reverse else "mlstm_fwd",
    )(qk, qk, qk, v, gt, gtt, conv_w, conv_b.reshape(1, nqk))


def _ml_out_kernel(hf_ref, hr_ref, o_ref, s_ref, mod_ref, ng_ref, w_ref, out_ref, p_scr):
    i = pl.program_id(0)
    gate = _mod_rows(mod_ref, i, 1, 0)[2]
    hs = hf_ref[...].astype(F32) + hr_ref[...].astype(F32)
    sig = _sigmoid(o_ref[...].astype(F32))
    ng = ng_ref[...]
    for hd in range(ML_HEADS):
        cs = slice(hd * ML_DV, (hd + 1) * ML_DV)
        seg = hs[:, cs]
        hn = seg * lax.rsqrt(jnp.mean(seg * seg, axis=-1, keepdims=True) + EPS) * ng[:, cs]
        p_scr[:, cs] = (hn * sig[:, cs]).astype(BF16)
    y = jnp.dot(p_scr[...], w_ref[...], preferred_element_type=F32)
    out_ref[...] = s_ref[...] + gate * y


def _ml_out(hf, hr, o, s, mods, norm_g, w_out):
    n, d = s.shape
    nv = hf.shape[1]
    tile = lambda w: pl.BlockSpec((TM, w), lambda i: (i, 0))
    return pl.pallas_call(
        _ml_out_kernel,
        out_shape=jax.ShapeDtypeStruct((n, d), F32),
        grid=(n // TM,),
        in_specs=[tile(nv), tile(nv), tile(nv), tile(d), _full(mods.shape), _full((1, nv)),
                  _full(w_out.shape)],
        out_specs=tile(d),
        scratch_shapes=[pltpu.VMEM((TM, nv), BF16)],
        compiler_params=_cparams("arbitrary"),
        name="mlstm_out",
    )(hf, hr, o, s, mods, norm_g.reshape(1, nv), w_out.astype(BF16))


def _mlstm_layer(pending, mods, norm_g, w_in, conv_w, conv_b, gate_b, ml_norm_g, w_out):
    s, qk, v, o, gt = _ml_proj(pending, mods, norm_g, w_in, gate_b)
    gtt = gt[:, :2 * SUBLANES].T
    hf = _ml_rec(qk, v, gt, gtt, conv_w, conv_b, False)
    hr = _ml_rec(qk, v, gt, gtt, conv_w, conv_b, True)
    return _ml_out(hf, hr, o, s, mods, ml_norm_g, w_out)


def _lru_proj_kernel(dest_ref, s_ref, wt_ref, pmod_ref, ys_ref, mod_ref, g_ref, w_ref,
                     snew_ref, gl_ref, xr_ref, ybuf, sem, **pending_kw):
    i = pl.program_id(0)
    x = _combined_tile(dest_ref, s_ref, wt_ref, pmod_ref, ys_ref, ybuf, sem, **pending_kw)
    snew_ref[...] = x
    shift, scale, _ = _mod_rows(mod_ref, i, 1, 0)
    h = _rms(x, g_ref[...]) * (1.0 + scale) + shift
    z = _bdot(h, w_ref[...])
    w = gl_ref.shape[1]
    gl_ref[...] = _gelu(z[:, :w]).astype(BF16)
    xr_ref[...] = z[:, w:]


def _lru_proj(pending, mods, norm_g, w_in):
    n, d = pending.s.shape
    kw, dest, p_specs, p_args, p_scratch = _pending_operands(pending)
    w = w_in.shape[1] // 2
    tile = lambda c: pl.BlockSpec((TM, c), lambda i, dst: (i, 0))
    whole = lambda shape: pl.BlockSpec(shape, lambda i, dst: (0,) * len(shape))
    return pl.pallas_call(
        functools.partial(_lru_proj_kernel, **kw),
        out_shape=(jax.ShapeDtypeStruct((n, d), F32), jax.ShapeDtypeStruct((n, w), BF16),
                   jax.ShapeDtypeStruct((n, w), F32)),
        grid_spec=pltpu.PrefetchScalarGridSpec(
            num_scalar_prefetch=1,
            grid=(n // TM,),
            in_specs=p_specs + [whole(mods.shape), whole((1, d)), whole(w_in.shape)],
            out_specs=(tile(d), tile(w), tile(w)),
            scratch_shapes=p_scratch),
        compiler_params=_cparams("arbitrary"),
        name="rglru_proj",
    )(dest, *p_args, mods, norm_g.reshape(1, d), w_in.astype(BF16))


def _lru_tile_index(j, n_tiles, reverse):
    if not reverse:
        return j
    return jnp.where(j == 0, 0, n_tiles - j)


def _lru_scan_kernel(x_ref, xp_ref, xn_ref, cw_ref, cb_ref, wg_ref, ba_ref, bx_ref, lam_ref,
                     o_ref, a_scr, u_scr, carry, *, reverse, n_tiles):
    j = pl.program_id(0)
    t = _lru_tile_index(j, n_tiles, reverse)

    @pl.when(j == 0)
    def _():
        carry[...] = jnp.zeros_like(carry)

    has_prev = t > 1
    has_next = (t != 0) & (t != n_tiles - 1)
    xr = _conv_tile(x_ref[...], xp_ref, xn_ref, has_prev, has_next, cw_ref, cb_ref)
    sp = _softplus(-lam_ref[...])
    B = LRU_BLOCK
    for hd in range(LRU_HEADS):
        cs = slice(hd * B, (hd + 1) * B)
        xh = xr[:, cs]
        y = jnp.dot(xh.astype(BF16), wg_ref[hd], preferred_element_type=F32)
        r = _sigmoid(y[:, :B] + ba_ref[:, cs])
        ig = _sigmoid(y[:, B:] + bx_ref[:, cs])
        log_a = -LRU_C * r * sp[:, cs]
        a = jnp.exp(log_a)
        a_scr[:, cs] = a
        v = 1.0 - a * a
        u_scr[:, cs] = jnp.where(v > 0.0, v * lax.rsqrt(v), 0.0) * (ig * xh)

    S = SUBLANES
    w = a_scr.shape[1]
    sidx = lax.broadcasted_iota(jnp.int32, (S, w), 0)

    def group(gi, c):
        g = (TM // S - 1 - gi) if reverse else gi
        r0 = pl.multiple_of(g * S, S)
        a = a_scr[pl.ds(r0, S), :]
        u = u_scr[pl.ds(r0, S), :]
        for sft in (1, 2, 4):
            if reverse:
                ok = sidx < S - sft
                a_e = pltpu.roll(a, S - sft, axis=0)
                u_e = pltpu.roll(u, S - sft, axis=0)
            else:
                ok = sidx >= sft
                a_e = pltpu.roll(a, sft, axis=0)
                u_e = pltpu.roll(u, sft, axis=0)
            u = jnp.where(ok, a * u_e + u, u)
            a = jnp.where(ok, a * a_e, a)
        hcur = a * carry[...] + u
        u_scr[pl.ds(r0, S), :] = hcur
        edge = 0 if reverse else S - 1
        carry[...] = jnp.broadcast_to(hcur[edge:edge + 1, :], (S, w))
        return c

    lax.fori_loop(0, TM // S, group, 0)
    o_ref[...] = u_scr[...].astype(BF16)


def _lru_scan(xraw, conv_w, conv_b, w_a, b_a, w_x, b_x, lam, reverse):
    n, w = xraw.shape
    nt = n // TM
    hb = TM // SUBLANES
    idx = lambda j: _lru_tile_index(j, nt, reverse)
    last8 = n // SUBLANES - 1
    wg = jnp.concatenate([w_a, w_x], axis=-1).astype(BF16)
    return pl.pallas_call(
        functools.partial(_lru_scan_kernel, reverse=reverse, n_tiles=nt),
        out_shape=jax.ShapeDtypeStruct((n, w), BF16),
        grid=(nt,),
        in_specs=[pl.BlockSpec((TM, w), lambda j: (idx(j), 0)),
                  pl.BlockSpec((SUBLANES, w), lambda j: (jnp.maximum(idx(j) * hb - 1, 0), 0)),
                  pl.BlockSpec((SUBLANES, w), lambda j: (jnp.minimum((idx(j) + 1) * hb, last8), 0)),
                  _full((4, w)), _full((1, w)), _full(wg.shape), _full((1, w)), _full((1, w)),
                  _full((1, w))],
        out_specs=pl.BlockSpec((TM, w), lambda j: (idx(j), 0)),
        scratch_shapes=[pltpu.VMEM((TM, w), F32), pltpu.VMEM((TM, w), F32),
                        pltpu.VMEM((SUBLANES, w), F32)],
        compiler_params=_cparams("arbitrary"),
        name="rglru_rev" if reverse else "rglru_fwd",
    )(xraw, xraw, xraw, conv_w, conv_b.reshape(1, w), wg, b_a.reshape(1, w), b_x.reshape(1, w),
      lam.reshape(1, w))


def _lru_out_kernel(gl_ref, hf_ref, hr_ref, s_ref, mod_ref, w_ref, out_ref):
    gate = _mod_rows(mod_ref, 1, 0, 0)[2]
    p = gl_ref[...].astype(F32) * (hf_ref[...].astype(F32) + hr_ref[...].astype(F32))
    out_ref[...] = s_ref[...] + gate * _bdot(p, w_ref[...])


def _lru_out(gl, hf, hr, s, mods, w_out):
    n, d = s.shape
    w = gl.shape[1]
    lat = lambda c: pl.BlockSpec((TM, c), lambda i: (i + 1, 0))
    return pl.pallas_call(
        _lru_out_kernel,
        out_shape=jax.ShapeDtypeStruct((n - TM, d), F32),
        grid=(n // TM - 1,),
        in_specs=[lat(w), lat(w), lat(w), lat(d), _full(mods.shape), _full(w_out.shape)],
        out_specs=pl.BlockSpec((TM, d), lambda i: (i, 0)),
        compiler_params=_cparams("arbitrary"),
        name="rglru_out",
    )(gl, hf, hr, s, mods, w_out.astype(BF16))


def _rglru_layer(pending, mods, norm_g, w_in, conv_w, conv_b, w_a, b_a, w_x, b_x, lam, w_out):
    s, gl, xraw = _lru_proj(pending, mods, norm_g, w_in)
    hf = _lru_scan(xraw, conv_w, conv_b, w_a[0], b_a[0], w_x[0], b_x[0], lam[0], False)
    hr = _lru_scan(xraw, conv_w, conv_b, w_a[1], b_a[1], w_x[1], b_x[1], lam[1], True)
    return _lru_out(gl, hf, hr, s, mods, w_out)


def _fn_proj_kernel(s_ref, mod_ref, g_ref, wt_ref, cs_ref, yr_ref, yi_ref, ar_scr, ai_scr):
    shift, scale, _ = _mod_rows(mod_ref, 1, 0, 0)
    nm = wt_ref.shape[0]
    gw = nm // FN_GROUPS
    per = TM // FFT_N2
    nj = FN_TB // FFT_N2
    csb = cs_ref[...].astype(BF16)

    def sub(tc, c):
        r0 = pl.multiple_of(tc * TM, TM)
        h = _rms(s_ref[pl.ds(r0, TM), :], g_ref[...]) * (1.0 + scale) + shift
        zt = lax.dot_general(wt_ref[...], h.astype(BF16), (((1,), (1,)), ((), ())),
                             preferred_element_type=F32).astype(BF16)
        for g in range(FN_GROUPS):
            y = jnp.dot(csb, zt[g * gw:(g + 1) * gw, :], preferred_element_type=F32)
            for q in range(per):
                row0 = pl.multiple_of((tc * per + q) * _slab_pitch(nm) + g * gw, SUBLANES)
                ar_scr[pl.ds(row0, gw), :] = y[:gw, q * FFT_N2:(q + 1) * FFT_N2]
                ai_scr[pl.ds(row0, gw), :] = y[gw:, q * FFT_N2:(q + 1) * FFT_N2]
        return c

    lax.fori_loop(0, FN_TB // TM, sub, 0)

    def relayout(m, c):
        yr_ref[m] = ar_scr[pl.ds(m, nj, stride=_slab_pitch(nm)), :]
        yi_ref[m] = ai_scr[pl.ds(m, nj, stride=_slab_pitch(nm)), :]
        return c

    lax.fori_loop(0, nm, relayout, 0, unroll=8)


def _slab_pitch(rows):
    return rows + SUBLANES


def _dft_cos_sin(n, scale):
    k = np.arange(n, dtype=np.int64)
    ang = 2.0 * np.pi * ((k[:, None] * k[None, :]) % n).astype(np.float64) / n
    return np.cos(ang) * scale, np.sin(ang) * scale


def _fn_proj(s, mods, norm_g, w_in):
    t, d = s.shape
    nm = w_in.shape[1]
    gw = nm // FN_GROUPS
    n1 = t // FFT_N2
    nj = FN_TB // FFT_N2
    c, sn = _dft_cos_sin(gw, gw ** -0.5)
    cs = jnp.asarray(np.concatenate([c, -sn], axis=0), F32)
    yspec = pl.BlockSpec((nm, nj, FFT_N2), lambda i: (0, i, 0))
    yshape = jax.ShapeDtypeStruct((nm, n1, FFT_N2), F32)
    return pl.pallas_call(
        _fn_proj_kernel,
        out_shape=(yshape, yshape),
        grid=(t // FN_TB,),
        in_specs=[pl.BlockSpec((FN_TB, d), lambda i: (i, 0)), _full(mods.shape), _full((1, d)),
                  _full((nm, d)), _full(cs.shape)],
        out_specs=(yspec, yspec),
        scratch_shapes=[pltpu.VMEM((nj * _slab_pitch(nm), FFT_N2), F32),
                        pltpu.VMEM((nj * _slab_pitch(nm), FFT_N2), F32)],
        compiler_params=_cparams("arbitrary"),
        name="fourier_proj",
    )(s, mods, norm_g.reshape(1, d), w_in.T.astype(BF16), cs)


def _fn_fft_kernel(yr_ref, yi_ref, m_ref, tc_ref, ts_ref, d_ref, o_ref):
    n1 = yr_ref.shape[1]
    n2 = FFT_N2
    xr = jnp.concatenate([yr_ref[m].astype(BF16) for m in range(FN_CB)], axis=1)
    xi = jnp.concatenate([yi_ref[m].astype(BF16) for m in range(FN_CB)], axis=1)
    a = jnp.dot(m_ref[...].astype(BF16), jnp.concatenate([xr, xi], axis=0),
                preferred_element_type=F32)
    ar = a[:n1]
    ai = a[n1:]
    tc = jnp.concatenate([tc_ref[...]] * FN_CB, axis=1)
    ts = jnp.concatenate([ts_ref[...]] * FN_CB, axis=1)
    br = ar * tc + ai * ts
    bi = ai * tc - ar * ts
    bst = jnp.concatenate(
        [jnp.concatenate([br[:, m * n2:(m + 1) * n2], bi[:, m * n2:(m + 1) * n2]], axis=1)
         for m in range(FN_CB)], axis=0).astype(BF16)
    res = lax.dot_general(d_ref[...].astype(BF16), bst, (((1,), (1,)), ((), ())),
                          preferred_element_type=F32)
    for m in range(FN_CB):
        o_ref[m] = res[:, m * n1:(m + 1) * n1]


def _fn_fft(yr, yi):
    nm, n1, n2 = yr.shape
    t = n1 * n2
    c, sn = _dft_cos_sin(n1, n1 ** -0.5)
    m = jnp.asarray(np.block([[c, sn], [-sn, c]]), F32)
    k1 = np.arange(n1, dtype=np.int64)[:, None]
    t2 = np.arange(n2, dtype=np.int64)[None, :]
    ang = 2.0 * np.pi * ((k1 * t2) % t).astype(np.float64) / t
    tc = jnp.asarray(np.cos(ang), F32)
    ts = jnp.asarray(np.sin(ang), F32)
    c2, s2 = _dft_cos_sin(n2, n2 ** -0.5)
    dm = jnp.asarray(np.concatenate([c2, s2], axis=1), F32)
    yspec = pl.BlockSpec((FN_CB, n1, n2), lambda i: (i, 0, 0))
    return pl.pallas_call(
        _fn_fft_kernel,
        out_shape=jax.ShapeDtypeStruct((nm, n2, n1), F32),
        grid=(nm // FN_CB,),
        in_specs=[yspec, yspec, _full(m.shape), _full(tc.shape), _full(ts.shape), _full(dm.shape)],
        out_specs=pl.BlockSpec((FN_CB, n2, n1), lambda i: (i, 0, 0)),
        compiler_params=_cparams("arbitrary"),
        name="fourier_fft",
    )(yr, yi, m, tc, ts, dm)


def _fn_out_kernel(ft_ref, w_ref, s_ref, mod_ref, o_ref, a_scr):
    gate = _mod_rows(mod_ref, 1, 0, 0)[2]
    nm, nj, n1 = ft_ref.shape

    def relayout(m, c):
        a_scr[pl.ds(m, nj, stride=_slab_pitch(nm)), :] = ft_ref[m]
        return c

    lax.fori_loop(0, nm, relayout, 0, unroll=8)
    for j in range(nj):
        p0 = j * _slab_pitch(nm)
        slab = a_scr[p0:p0 + nm, :].astype(BF16)
        y = lax.dot_general(slab, w_ref[...], (((0,), (0,)), ((), ())), preferred_element_type=F32)
        rows = slice(j * n1, (j + 1) * n1)
        o_ref[rows, :] = s_ref[rows, :] + gate * y


def _fn_out(ft, s, mods, w_out):
    t, d = s.shape
    nm, n2, n1 = ft.shape
    nj = FN_TB // n1
    tok = pl.BlockSpec((FN_TB, d), lambda i: (i, 0))
    return pl.pallas_call(
        _fn_out_kernel,
        out_shape=jax.ShapeDtypeStruct((t, d), F32),
        grid=(t // FN_TB,),
        in_specs=[pl.BlockSpec((nm, nj, n1), lambda i: (0, i, 0)), _full(w_out.shape), tok,
                  _full(mods.shape)],
        out_specs=tok,
        scratch_shapes=[pltpu.VMEM((nj * _slab_pitch(nm), n1), F32)],
        compiler_params=_cparams("arbitrary"),
        name="fourier_out",
    )(ft, w_out.astype(BF16), s, mods)


def _fourier_layer(s, mods, norm_g, w_in, w_out):
    yr, yi = _fn_proj(s, mods, norm_g, w_in)
    return _fn_out(_fn_fft(yr, yi), s, mods, w_out)


def kernel(x, c, ctx, c_ctx, ada_w, ada_b, norm_mix_g, norm_ffn_g, final_norm_g, router_group_w, router_group_b, router_expert_w, router_expert_b, expert_w_gate, expert_w_up, expert_w_down, cm_w_in, cm_v_norm_g, cm_w_s, cm_b_s, cm_w_out, ml_w_in, ml_conv_w, ml_conv_b, ml_gate_b, ml_norm_g, ml_w_out, lru_w_in, lru_conv_w, lru_conv_b, lru_w_a, lru_b_a, lru_w_x, lru_b_x, lru_lambda, lru_w_out, fn_w_in, fn_w_out):
    bsz, seq, d = x.shape
    assert bsz == 1 and ada_w.shape[0] == 4 and ctx.shape[1] == TM
    c_rows = jnp.concatenate([c_ctx[None, :], c, jnp.zeros((SUBLANES - 2, d), F32)], axis=0)
    mods = _ada_table(c_rows, ada_w, ada_b)

    def moe(s, i, ctx_tiles, final_norm=False, defer=False):
        return _moe_layer(s, mods[i], norm_ffn_g[i], router_group_w[i], router_group_b[i],
                          router_expert_w[i], router_expert_b[i], expert_w_gate, expert_w_up,
                          expert_w_down, i, ctx_tiles, final_norm_g, final_norm, defer)

    s = _chunk_mlp_layer(x[0], ctx[0], mods[0], norm_mix_g[0], cm_w_in[0], cm_v_norm_g[0],
                         cm_w_s[0], cm_b_s[0], cm_w_out[0])
    s = moe(s, 0, 1, defer=True)
    s = _mlstm_layer(s, mods[1], norm_mix_g[1], ml_w_in[0], ml_conv_w[0], ml_conv_b[0],
                     ml_gate_b[0], ml_norm_g[0], ml_w_out[0])
    s = moe(s, 1, 1, defer=True)
    s = _rglru_layer(s, mods[2], norm_mix_g[2], lru_w_in[0], lru_conv_w[0], lru_conv_b[0],
                     lru_w_a[0], lru_b_a[0], lru_w_x[0], lru_b_x[0], lru_lambda[0], lru_w_out[0])
    s = moe(s, 2, 0)
    s = _fourier_layer(s, mods[3], norm_mix_g[3], fn_w_in[0], fn_w_out[0])
    s = moe(s, 3, 0, final_norm=True)
    return s[None]
```

```python
import functools
import math
from typing import NamedTuple

import jax
import jax.numpy as jnp
import numpy as np
from jax import lax
from jax.experimental import pallas as pl
from jax.experimental.pallas import tpu as pltpu

F32 = jnp.float32
BF16 = jnp.bfloat16

EPS = 1e-6
POS_BASE = 10000.0
GRID_W = 64
N_MOD = 6
TM = 256
LANES = 128
SUBLANES = 8
VMEM_LIMIT = 56 * 1024 * 1024

CM_CHUNK = 128
CM_GROUPS = 4
ML_HEADS = 4
ML_DK = 128
ML_DV = 256
ML_CHUNK = 128
LRU_HEADS = 10
LRU_BLOCK = 128
LRU_C = 8.0
FN_GROUPS = 4
FFT_N2 = 128
MOE_GROUPS = 4
MOE_EPG = 8
MOE_EXPERTS = MOE_GROUPS * MOE_EPG
MOE_ROWS_LOG2 = 8
MOE_ROWS = 1 << MOE_ROWS_LOG2
ROUTE_ROWS = 40
FN_TB = 1024
FN_CB = 32
CONV_LEFT = 2

HI = lax.Precision.HIGHEST


def _cparams(*sem):
    return pltpu.CompilerParams(dimension_semantics=sem, vmem_limit_bytes=VMEM_LIMIT)


def _full(shape):
    nd = len(shape)
    return pl.BlockSpec(shape, lambda *_: (0,) * nd)


def _rms(x, g):
    return x * lax.rsqrt(jnp.mean(x * x, axis=-1, keepdims=True) + EPS) * g


def _gelu(x):
    c = math.sqrt(2.0 / math.pi)
    return 0.5 * x * (1.0 + jnp.tanh(c * (x + 0.044715 * (x * x * x))))


def _sigmoid(x):
    return 0.5 * jnp.tanh(0.5 * x) + 0.5


def _silu(x):
    return x * _sigmoid(x)


def _softplus(x):
    return jnp.maximum(x, 0.0) + jnp.log(1.0 + jnp.exp(-jnp.abs(x)))


def _mod_rows(mod_ref, tile, ctx_tiles, first):
    row = jnp.where(tile < ctx_tiles, 0, 1)
    m = mod_ref[pl.ds(row, 1), :]
    d = m.shape[1] // N_MOD
    return tuple(m[:, (first + j) * d:(first + j + 1) * d] for j in range(3))


def _bdot(a, b):
    return jnp.dot(a.astype(BF16), b.astype(BF16), preferred_element_type=F32)


def _split_bf16(x):
    hi = x.astype(BF16)
    return hi, (x - hi.astype(F32)).astype(BF16)


def _dot3(a, b, dims):
    a_hi, a_lo = _split_bf16(a)
    b_hi, b_lo = _split_bf16(b)
    dg = functools.partial(lax.dot_general, dimension_numbers=(dims, ((), ())),
                           preferred_element_type=F32)
    return dg(a_hi, b_hi) + dg(a_hi, b_lo) + dg(a_lo, b_hi)


def _ada_kernel(c_ref, w_ref, b_ref, o_ref):
    c = c_ref[...]
    o_ref[...] = _dot3(_silu(c), w_ref[...], ((1,), (0,))) + b_ref[...]


def _ada_table(c_rows, ada_w, ada_b):
    depth, d, n = ada_w.shape
    tn = 2048
    return pl.pallas_call(
        _ada_kernel,
        out_shape=jax.ShapeDtypeStruct((depth, SUBLANES, n), F32),
        grid=(depth, n // tn),
        in_specs=[_full((SUBLANES, d)),
                  pl.BlockSpec((None, d, tn), lambda i, j: (i, 0, j)),
                  pl.BlockSpec((None, 1, tn), lambda i, j: (i, 0, j))],
        out_specs=pl.BlockSpec((None, SUBLANES, tn), lambda i, j: (i, 0, j)),
        compiler_params=_cparams("arbitrary", "arbitrary"),
        name="ada_table",
    )(c_rows, ada_w, ada_b.reshape(depth, 1, n))


def _pos_tables(seq, d):
    q = d // 4
    freq = jnp.exp(-math.log(POS_BASE) * jnp.arange(q, dtype=F32) / q)
    ar = jnp.arange(seq // GRID_W, dtype=F32)[:, None] * freq
    ac = jnp.arange(GRID_W, dtype=F32)[:, None] * freq
    return (jnp.concatenate([jnp.sin(ar), jnp.cos(ar)], axis=-1),
            jnp.concatenate([jnp.sin(ac), jnp.cos(ac)], axis=-1))


def _cm_kernel(x_ref, ctx_ref, rt_ref, ct_ref, mod_ref, g_ref, win_ref, vg_ref, ws_ref, bs_ref,
               wout_ref, o_ref, p_scr, x_scr):
    i = pl.program_id(0)

    @pl.when(i == 0)
    def _():
        x_scr[...] = ctx_ref[...]

    @pl.when(i > 0)
    def _():
        rows_per_tile = TM // GRID_W
        q2 = rt_ref.shape[1]
        r0 = (i - 1) * rows_per_tile
        rt = jnp.concatenate(
            [jnp.broadcast_to(rt_ref[pl.ds(r0 + j, 1), :], (GRID_W, q2))
             for j in range(rows_per_tile)], axis=0)
        ct = jnp.concatenate([ct_ref[...]] * rows_per_tile, axis=0)
        x_scr[...] = x_ref[...] + jnp.concatenate([rt, ct], axis=1)

    shift, scale, gate = _mod_rows(mod_ref, i, 1, 0)
    x = x_scr[...]
    h = _rms(x, g_ref[...]) * (1.0 + scale) + shift
    z = _gelu(_bdot(h, win_ref[...]))
    w = z.shape[1] // 2
    u = z[:, :w]
    v = _rms(z[:, w:], vg_ref[...]).astype(BF16)
    gw = w // CM_GROUPS
    for c in range(TM // CM_CHUNK):
        r = slice(c * CM_CHUNK, (c + 1) * CM_CHUNK)
        for g in range(CM_GROUPS):
            cs = slice(g * gw, (g + 1) * gw)
            s = jnp.dot(ws_ref[g], v[r, cs], preferred_element_type=F32) + bs_ref[:, g:g + 1]
            p_scr[r, cs] = (u[r, cs] * s).astype(BF16)
    y = jnp.dot(p_scr[...], wout_ref[...], preferred_element_type=F32)
    o_ref[...] = x + gate * y


def _chunk_mlp_layer(x2, ctx2, mods, norm_g, w_in, v_g, w_s, b_s, w_out):
    seq, d = x2.shape
    n_ctx = ctx2.shape[0]
    assert n_ctx == TM and seq % TM == 0 and TM % GRID_W == 0
    n = n_ctx + seq
    w = w_out.shape[0]
    rt, ct = _pos_tables(seq, d)
    return pl.pallas_call(
        _cm_kernel,
        out_shape=jax.ShapeDtypeStruct((n, d), F32),
        grid=(n // TM,),
        in_specs=[pl.BlockSpec((TM, d), lambda i: (jnp.maximum(i - 1, 0), 0)),
                  _full((TM, d)), _full(rt.shape), _full(ct.shape),
                  _full(mods.shape), _full((1, d)), _full(w_in.shape), _full((1, w)),
                  _full(w_s.shape), _full((CM_CHUNK, CM_GROUPS)), _full(w_out.shape)],
        out_specs=pl.BlockSpec((TM, d), lambda i: (i, 0)),
        scratch_shapes=[pltpu.VMEM((TM, w), BF16), pltpu.VMEM((TM, d), F32)],
        compiler_params=_cparams("arbitrary"),
        name="chunk_mlp",
    )(x2, ctx2, rt, ct, mods, norm_g.reshape(1, d), w_in.astype(BF16), v_g.reshape(1, w),
      w_s.astype(BF16), b_s.T, w_out.astype(BF16))


def _store_token_tiles(ref, x):
    rows, d = x.shape
    for j in range(d // LANES):
        ref[pl.ds(j, rows, stride=d // LANES), :] = x[:, j * LANES:(j + 1) * LANES]


def _load_token_tiles(ref):
    chunks = SUBLANES
    rows = ref.shape[0] // chunks
    return jnp.concatenate([ref[pl.ds(j, rows, stride=chunks), :] for j in range(chunks)], axis=1)


def _route_tiles(nt):
    return next(k for k in (5, 4, 2, 1) if nt % k == 0)


def _moe_input(s_ref, mod_ref, g_ref, ctx_rows):
    rows, d = s_ref.shape
    lat = mod_ref[1:2, :]
    shift, scale = lat[:, 3 * d:4 * d], lat[:, 4 * d:5 * d]
    if ctx_rows:
        ctx = mod_ref[0:1, :]
        is_ctx = ((pl.program_id(0) == 0)
                  & (lax.broadcasted_iota(jnp.int32, (rows, 1), 0) < ctx_rows))
        shift = jnp.where(is_ctx, ctx[:, 3 * d:4 * d], shift)
        scale = jnp.where(is_ctx, ctx[:, 4 * d:5 * d], scale)
    return _rms(s_ref[...], g_ref[...]) * (1.0 + scale) + shift


def _router_kernel(s_ref, mod_ref, g_ref, rwt_ref, rbt_ref, tri_ref, e1_ref, e2_ref, r1_ref,
                   r2_ref, wt_ref, cnt_ref, carry, *, ctx_rows):
    i = pl.program_id(0)
    rows = s_ref.shape[0]

    @pl.when(i == 0)
    def _():
        carry[...] = jnp.zeros_like(carry)

    h = _moe_input(s_ref, mod_ref, g_ref, ctx_rows)
    logits = _dot3(rwt_ref[...], h, ((1,), (1,))) + rbt_ref[...]
    row = lax.broadcasted_iota(jnp.int32, logits.shape, 0)
    neg = jnp.float32(-jnp.inf)
    big = jnp.int32(1 << 20)
    is_g = row < MOE_GROUPS
    gl = jnp.where(is_g, logits, neg)
    gmax = jnp.max(gl, axis=0, keepdims=True)
    grp = jnp.min(jnp.where(is_g & (gl == gmax), row, big), axis=0, keepdims=True)
    p_grp = 1.0 / jnp.sum(jnp.exp(gl - gmax), axis=0, keepdims=True)
    e_row = row - MOE_GROUPS
    in_grp = (e_row >= 0) & (e_row < MOE_EXPERTS) & ((e_row >> 3) == grp)
    l1 = jnp.where(in_grp, logits, neg)
    v1 = jnp.max(l1, axis=0, keepdims=True)
    i1 = jnp.min(jnp.where(in_grp & (l1 == v1), row, big), axis=0, keepdims=True)
    rest = in_grp & (row != i1)
    l2 = jnp.where(rest, logits, neg)
    v2 = jnp.max(l2, axis=0, keepdims=True)
    i2 = jnp.min(jnp.where(rest & (l2 == v2), row, big), axis=0, keepdims=True)
    e21 = jnp.exp(v2 - v1)
    w1 = p_grp / (1.0 + e21)
    w2 = p_grp * e21 / (1.0 + e21)
    oh1 = (row == i1).astype(F32)
    oh2 = (row == i2).astype(F32)
    oh = oh1 + oh2
    before = jnp.dot(oh.astype(BF16), tri_ref[...], preferred_element_type=F32) + carry[:, 0:1]
    r1_ref[...] = jnp.sum(oh1 * before, axis=0, keepdims=True).astype(jnp.int32)
    r2_ref[...] = jnp.sum(oh2 * before, axis=0, keepdims=True).astype(jnp.int32)
    e1_ref[...] = i1 - MOE_GROUPS
    e2_ref[...] = i2 - MOE_GROUPS
    carry[...] = carry[...] + jnp.sum(oh, axis=1, keepdims=True)
    cnt_ref[...] = carry[...]
    wt_ref[...] = jnp.concatenate([w1, w2, jnp.zeros((LANES - 2, rows), F32)], axis=0).T


def _router(s, mods, norm_g, rg_w, rg_b, re_w, re_b, ctx_tiles):
    n, d = s.shape
    rows = TM * _route_tiles(n // TM)
    steps = n // rows
    pad = ROUTE_ROWS - MOE_GROUPS - MOE_EXPERTS
    rwt = jnp.concatenate([rg_w, re_w, jnp.zeros((d, pad), F32)], axis=1).T
    rbt = jnp.broadcast_to(jnp.concatenate([rg_b, re_b, jnp.zeros((pad,), F32)])[:, None],
                           (ROUTE_ROWS, rows))
    tri = jnp.asarray(np.triu(np.ones((rows, rows), np.float32), 1), BF16)
    assert d == SUBLANES * LANES
    tile = pl.BlockSpec((rows, d), lambda i: (i, 0))
    irow = pl.BlockSpec((None, 1, rows), lambda i: (i, 0, 0))
    ishape = jax.ShapeDtypeStruct((steps, 1, rows), jnp.int32)
    return pl.pallas_call(
        functools.partial(_router_kernel, ctx_rows=ctx_tiles * TM),
        out_shape=(ishape, ishape, ishape, ishape, jax.ShapeDtypeStruct((n, LANES), F32),
                   jax.ShapeDtypeStruct((ROUTE_ROWS, LANES), F32)),
        grid=(steps,),
        in_specs=[tile, _full(mods.shape), _full((1, d)), _full((ROUTE_ROWS, d)),
                  _full((ROUTE_ROWS, rows)), _full((rows, rows))],
        out_specs=(irow, irow, irow, irow, pl.BlockSpec((rows, LANES), lambda i: (i, 0)),
                   _full((ROUTE_ROWS, LANES))),
        scratch_shapes=[pltpu.VMEM((ROUTE_ROWS, LANES), F32)],
        compiler_params=_cparams("arbitrary"),
        name="moe_router",
    )(s, mods, norm_g.reshape(1, d), rwt, rbt, tri)


def _finalize_kernel(cnt_ref, e1_ref, e2_ref, r1_ref, r2_ref, d1_ref, d2_ref, blk_ref):
    e1 = e1_ref[...]
    e2 = e2_ref[...]
    r1 = r1_ref[...]
    r2 = r2_ref[...]
    d1 = jnp.zeros_like(e1)
    d2 = jnp.zeros_like(e2)
    lane = lax.broadcasted_iota(jnp.int32, blk_ref.shape, 1)
    brow = lane * MOE_ROWS
    sub = lax.broadcasted_iota(jnp.int32, blk_ref.shape, 0)
    be = jnp.zeros(blk_ref.shape, jnp.int32)
    pend = jnp.zeros(blk_ref.shape, jnp.int32)
    ps = jnp.int32(0)
    for e in range(MOE_EXPERTS):
        c = cnt_ref[e]
        pe = ps + lax.shift_left(lax.shift_right_logical(c + (MOE_ROWS - 1), MOE_ROWS_LOG2),
                                 MOE_ROWS_LOG2)
        d1 = jnp.where(e1 == e, ps + r1, d1)
        d2 = jnp.where(e2 == e, ps + r2, d2)
        be = be + (brow >= pe).astype(jnp.int32)
        pend = jnp.where(lane == e, pe, pend)
        ps = pe
    d1_ref[...] = d1
    d2_ref[...] = d2
    n_used = lax.shift_right_logical(ps, MOE_ROWS_LOG2)
    blk_ref[...] = jnp.where(sub == 0, jnp.minimum(be, MOE_EXPERTS - 1),
                             jnp.where(sub == 1, pend, n_used))


def _finalize(counts, e1, e2, r1, r2, nb):
    nbp = (nb + LANES - 1) // LANES * LANES
    whole = pl.BlockSpec(e1.shape, lambda i, c: (0, 0, 0))
    ishape = jax.ShapeDtypeStruct(e1.shape, jnp.int32)
    return pl.pallas_call(
        _finalize_kernel,
        out_shape=(ishape, ishape, jax.ShapeDtypeStruct((SUBLANES, nbp), jnp.int32)),
        grid_spec=pltpu.PrefetchScalarGridSpec(
            num_scalar_prefetch=1,
            grid=(1,),
            in_specs=[whole, whole, whole, whole],
            out_specs=(whole, whole, pl.BlockSpec((SUBLANES, nbp), lambda i, c: (0, 0)))),
        compiler_params=_cparams("arbitrary"),
        name="moe_finalize",
    )(counts, e1, e2, r1, r2)


def _token_copy(src, r, dst, d, sem):
    return pltpu.make_async_copy(src.at[pl.ds(pl.multiple_of(r * SUBLANES, SUBLANES), SUBLANES), :],
                                 dst.at[pl.ds(pl.multiple_of(d * SUBLANES, SUBLANES), SUBLANES), :],
                                 sem)


def _zero_fill_padding(pend_ref, nu_ref, xs_out, zbuf, zsem):
    blk_rows = MOE_ROWS * SUBLANES
    nb = xs_out.shape[0] // blk_rows
    zbuf[...] = jnp.zeros_like(zbuf)

    def block_copy(b):
        r0 = pl.multiple_of(b * blk_rows, blk_rows)
        return pltpu.make_async_copy(zbuf, xs_out.at[pl.ds(r0, blk_rows), :], zsem)

    def seg_last_block(e):
        pe = pend_ref[e]
        prev = pend_ref[e - 1] if e > 0 else 0
        return pe > prev, lax.shift_right_logical(pe, MOE_ROWS_LOG2) - 1

    for e in range(MOE_EXPERTS):
        nonempty, b = seg_last_block(e)

        @pl.when(nonempty)
        def _():
            block_copy(b).start()

    def tail_start(b, c):
        block_copy(b).start()
        return c

    lax.fori_loop(nu_ref[0], nb, tail_start, 0)
    for e in range(MOE_EXPERTS):
        nonempty, b = seg_last_block(e)

        @pl.when(nonempty)
        def _():
            block_copy(b).wait()

    def tail_wait(b, c):
        block_copy(b).wait()
        return c

    lax.fori_loop(nu_ref[0], nb, tail_wait, 0)


def _dispatch_kernel(dest_ref, pend_ref, nu_ref, s_ref, mod_ref, g_ref, xs_out, h_ref, sem, zbuf,
                     zsem, *, tiles, ctx_rows):
    i = pl.program_id(0)

    @pl.when(i == 0)
    def _():
        _zero_fill_padding(pend_ref, nu_ref, xs_out, zbuf, zsem)

    _store_token_tiles(h_ref, _moe_input(s_ref, mod_ref, g_ref, ctx_rows))
    rows = tiles * TM
    for q in range(tiles):
        base = i * (2 * rows) + q * TM

        def start(r, c):
            _token_copy(h_ref, q * TM + r, xs_out, dest_ref[base + r], sem).start(priority=0)
            _token_copy(h_ref, q * TM + r, xs_out, dest_ref[base + rows + r],
                        sem).start(priority=1)
            return c

        lax.fori_loop(0, TM, start, 0, unroll=8)
    for _ in range(2):
        pltpu.make_async_copy(h_ref, xs_out.at[pl.ds(0, tiles * TM * SUBLANES), :], sem).wait()


def _dispatch(dest, pad_end, n_used, s, mods, norm_g, ctx_tiles, n_rows):
    n, d = s.shape
    nt = n // TM
    tiles = _route_tiles(nt)
    rows = tiles * TM
    return pl.pallas_call(
        functools.partial(_dispatch_kernel, tiles=tiles, ctx_rows=ctx_tiles * TM),
        out_shape=jax.ShapeDtypeStruct((n_rows * SUBLANES, LANES), F32),
        grid_spec=pltpu.PrefetchScalarGridSpec(
            num_scalar_prefetch=3,
            grid=(nt // tiles,),
            in_specs=[pl.BlockSpec((rows, d), lambda i, *_: (i, 0)),
                      pl.BlockSpec(mods.shape, lambda i, *_: (0, 0)),
                      pl.BlockSpec((1, d), lambda i, *_: (0, 0))],
            out_specs=pl.BlockSpec(memory_space=pl.ANY),
            scratch_shapes=[pltpu.VMEM((rows * SUBLANES, LANES), F32), pltpu.SemaphoreType.DMA,
                            pltpu.VMEM((MOE_ROWS * SUBLANES, LANES), F32),
                            pltpu.SemaphoreType.DMA]),
        compiler_params=_cparams("arbitrary"),
        name="moe_dispatch",
    )(dest, pad_end, n_used, s, mods, norm_g.reshape(1, d))


X_SLOTS = 4


Y_SLOTS = 3


def _expert_kernel(be_ref, pend_ref, nu_ref, xs_hbm, wg_hbm, wu_hbm, wd_hbm, ys_hbm,
                   xbuf, ybuf, wg_f, wu_f, wd_f, wg_s, wu_s, wd_s, xsem, ysem, wsem, ord_ref,
                   *, layer):
    nu = nu_ref[0]
    blk_rows = MOE_ROWS * SUBLANES
    nb = ys_hbm.shape[0] // blk_rows

    def rows_of(ref, blk):
        return ref.at[pl.ds(pl.multiple_of(blk * blk_rows, blk_rows), blk_rows), :]

    def x_copy(blk, slot):
        return pltpu.make_async_copy(rows_of(xs_hbm, blk), xbuf.at[slot], xsem.at[slot])

    def y_copy(blk, slot):
        return pltpu.make_async_copy(ybuf.at[slot], rows_of(ys_hbm, blk), ysem.at[slot])

    def w_copies(e, slot):
        return (pltpu.make_async_copy(wg_hbm.at[layer, e], wg_f.at[slot], wsem.at[slot]),
                pltpu.make_async_copy(wu_hbm.at[layer, e], wu_f.at[slot], wsem.at[slot]),
                pltpu.make_async_copy(wd_hbm.at[layer, e], wd_f.at[slot], wsem.at[slot]))

    ord_ref[0] = 0
    for j in range(X_SLOTS - 1):
        @pl.when(j < nu)
        def _():
            x_copy(j, j).start()

    @pl.when(nu > 0)
    def _():
        for c in w_copies(be_ref[0], 0):
            c.start()

    def block(b, carry):
        ahead = b + (X_SLOTS - 1)

        @pl.when(ahead < nu)
        def _():
            x_copy(ahead, lax.rem(ahead, X_SLOTS)).start()

        e = be_ref[b]

        @pl.when((b == 0) | (e != be_ref[jnp.maximum(b - 1, 0)]))
        def _():
            k = ord_ref[0]
            slot = lax.rem(k, 2)
            for c in w_copies(e, slot):
                c.wait()
            wg_s[...] = wg_f[slot].astype(BF16)
            wu_s[...] = wu_f[slot].astype(BF16)
            wd_s[...] = wd_f[slot].astype(BF16)
            nxt = lax.shift_right_logical(pend_ref[e], MOE_ROWS_LOG2)

            @pl.when(nxt < nu)
            def _():
                for c in w_copies(be_ref[nxt], 1 - slot):
                    c.start(priority=1)

            ord_ref[0] = k + 1

        slot = lax.rem(b, X_SLOTS)
        yslot = lax.rem(b, Y_SLOTS)
        x_copy(b, slot).wait()

        @pl.when(b >= Y_SLOTS)
        def _():
            y_copy(b - Y_SLOTS, yslot).wait()

        x = _load_token_tiles(xbuf.at[slot]).astype(BF16)
        a = jnp.dot(x, wg_s[...], preferred_element_type=F32)
        u = jnp.dot(x, wu_s[...], preferred_element_type=F32)
        _store_token_tiles(ybuf.at[yslot], jnp.dot((_silu(a) * u).astype(BF16), wd_s[...],
                                                   preferred_element_type=F32))
        y_copy(b, yslot).start()
        return carry

    lax.fori_loop(0, nu, block, 0)
    for j in range(1, Y_SLOTS + 1):
        @pl.when(nu >= j)
        def _():
            y_copy(nu - j, lax.rem(nu - j, Y_SLOTS)).wait()

    ybuf[0] = jnp.zeros(ybuf.shape[1:], ybuf.dtype)

    def tail_start(b, c):
        y_copy(b, 0).start()
        return c

    def tail_wait(b, c):
        y_copy(b, 0).wait()
        return c

    lax.fori_loop(nu, nb, tail_start, 0)
    lax.fori_loop(nu, nb, tail_wait, 0)


def _experts(blk_expert, pad_end, n_used, xs, w_gate, w_up, w_down, layer):
    d, hid = w_gate.shape[2:]
    blk_rows = MOE_ROWS * SUBLANES
    hbm = pl.BlockSpec(memory_space=pl.ANY)
    return pl.pallas_call(
        functools.partial(_expert_kernel, layer=layer),
        out_shape=jax.ShapeDtypeStruct(xs.shape, F32),
        grid_spec=pltpu.PrefetchScalarGridSpec(
            num_scalar_prefetch=3,
            grid=(1,),
            in_specs=[hbm, hbm, hbm, hbm],
            out_specs=hbm,
            scratch_shapes=[pltpu.VMEM((X_SLOTS, blk_rows, LANES), F32),
                            pltpu.VMEM((Y_SLOTS, blk_rows, LANES), F32),
                            pltpu.VMEM((2, d, hid), F32), pltpu.VMEM((2, d, hid), F32),
                            pltpu.VMEM((2, hid, d), F32),
                            pltpu.VMEM((d, hid), BF16), pltpu.VMEM((d, hid), BF16),
                            pltpu.VMEM((hid, d), BF16),
                            pltpu.SemaphoreType.DMA((X_SLOTS,)), pltpu.SemaphoreType.DMA((Y_SLOTS,)),
                            pltpu.SemaphoreType.DMA((2,)), pltpu.SMEM((1,), jnp.int32)]),
        compiler_params=_cparams("arbitrary"),
        name="moe_experts",
    )(blk_expert, pad_end, n_used, xs, w_gate, w_up, w_down)


def _combined_tile(dest_ref, s_ref, wt_ref, mod_ref, ys_ref, ybuf, sem, *, ctx_tiles, tiles):
    i = pl.program_id(0)
    slot = i % 2
    rows = tiles * TM

    def gather(tile, slot):
        tile = jnp.asarray(tile, jnp.int32)
        base = lax.div(tile, tiles) * (2 * rows) + lax.rem(tile, tiles) * TM

        def start(r, c):
            _token_copy(ys_ref, dest_ref[base + r], ybuf.at[slot, 0], r,
                        sem.at[slot]).start(priority=0)
            _token_copy(ys_ref, dest_ref[base + rows + r], ybuf.at[slot, 1], r,
                        sem.at[slot]).start(priority=1)
            return c

        lax.fori_loop(0, TM, start, 0, unroll=8)

    @pl.when(i == 0)
    def _():
        gather(0, 0)

    @pl.when(i + 1 < pl.num_programs(0))
    def _():
        gather(i + 1, 1 - slot)

    for k in range(2):
        pltpu.make_async_copy(ys_ref.at[pl.ds(0, TM * SUBLANES), :], ybuf.at[slot, k],
                              sem.at[slot]).wait()
    gate = _mod_rows(mod_ref, i, ctx_tiles, 3)[2]
    wt = wt_ref[...]
    y = (wt[:, 0:1] * _load_token_tiles(ybuf.at[slot, 0])
         + wt[:, 1:2] * _load_token_tiles(ybuf.at[slot, 1]))
    return s_ref[...] + gate * y


def _combine_scratch():
    return [pltpu.VMEM((2, 2, TM * SUBLANES, LANES), F32), pltpu.SemaphoreType.DMA((2,))]


def _combine_kernel(dest_ref, s_ref, wt_ref, mod_ref, fg_ref, ys_ref, o_ref, ybuf, sem,
                    *, final_norm, **kw):
    out = _combined_tile(dest_ref, s_ref, wt_ref, mod_ref, ys_ref, ybuf, sem, **kw)
    if final_norm:
        out = _rms(out, fg_ref[...])
    o_ref[...] = out


def _combine(dest, s, wts, mods, final_g, ys, ctx_tiles, final_norm):
    n, d = s.shape
    return pl.pallas_call(
        functools.partial(_combine_kernel, ctx_tiles=ctx_tiles, final_norm=final_norm,
                          tiles=_route_tiles(n // TM)),
        out_shape=jax.ShapeDtypeStruct((n, d), F32),
        grid_spec=pltpu.PrefetchScalarGridSpec(
            num_scalar_prefetch=1,
            grid=(n // TM,),
            in_specs=[pl.BlockSpec((TM, d), lambda i, dst: (i, 0)),
                      pl.BlockSpec((TM, LANES), lambda i, dst: (i, 0)),
                      pl.BlockSpec(mods.shape, lambda i, dst: (0, 0)),
                      pl.BlockSpec((1, d), lambda i, dst: (0, 0)),
                      pl.BlockSpec(memory_space=pl.ANY)],
            out_specs=pl.BlockSpec((TM, d), lambda i, dst: (i, 0)),
            scratch_shapes=_combine_scratch()),
        compiler_params=_cparams("arbitrary"),
        name="moe_combine",
    )(dest, s, wts, mods, final_g.reshape(1, d), ys)


def _moe_layer(s, mods, norm_g, rg_w, rg_b, re_w, re_b, w_gate, w_up, w_down, layer, ctx_tiles,
               final_g, final_norm, defer=False):
    n, d = s.shape
    e1, e2, r1, r2, wts, cnt = _router(s, mods, norm_g, rg_w, rg_b, re_w, re_b, ctx_tiles)
    counts = cnt[MOE_GROUPS:MOE_GROUPS + MOE_EXPERTS, 0].astype(jnp.int32)
    nb = (2 * n + MOE_EXPERTS * (MOE_ROWS - 1)) // MOE_ROWS + 1
    d1, d2, blk = _finalize(counts, e1, e2, r1, r2, nb)
    dest = jnp.concatenate([d1, d2], axis=1).reshape(2 * n)
    n_used = blk[2, :1]
    pad_end = blk[1, :MOE_EXPERTS]
    xs = _dispatch(dest, pad_end, n_used, s, mods, norm_g, ctx_tiles, nb * MOE_ROWS)
    ys = _experts(blk[0, :nb], pad_end, n_used, xs, w_gate, w_up, w_down, layer)
    if defer:
        return _PendingCombine(dest, s, wts, mods, ys, ctx_tiles)
    return _combine(dest, s, wts, mods, final_g, ys, ctx_tiles, final_norm)


class _PendingCombine(NamedTuple):
    dest: jax.Array
    s: jax.Array
    wts: jax.Array
    mods: jax.Array
    ys: jax.Array
    ctx_tiles: int


def _pending_operands(p):
    n, d = p.s.shape
    kw = dict(ctx_tiles=p.ctx_tiles, tiles=_route_tiles(n // TM))
    specs = [pl.BlockSpec((TM, d), lambda i, dst: (i, 0)),
             pl.BlockSpec((TM, LANES), lambda i, dst: (i, 0)),
             pl.BlockSpec(p.mods.shape, lambda i, dst: (0, 0)),
             pl.BlockSpec(memory_space=pl.ANY)]
    return kw, p.dest, specs, (p.s, p.wts, p.mods, p.ys), _combine_scratch()


def _conv_tile(x, prev_ref, next_ref, has_prev, has_next, w_ref, b_ref):
    rows = x.shape[0]
    S = SUBLANES
    ridx = lax.broadcasted_iota(jnp.int32, (S, x.shape[1]), 0)
    pm = jnp.where(has_prev, 1.0, 0.0)
    nm = jnp.where(has_next, 1.0, 0.0)
    p2 = prev_ref[S - 2:S - 1, :] * pm
    p1 = prev_ref[S - 1:S, :] * pm
    n1 = next_ref[0:1, :] * nm

    def fix_head(rolled, head):
        return jnp.concatenate([head(rolled[:S]), rolled[S:]], axis=0)

    xm1 = fix_head(pltpu.roll(x, 1, axis=0), lambda g: jnp.where(ridx == 0, p1, g))
    xm2 = fix_head(pltpu.roll(x, 2, axis=0),
                   lambda g: jnp.where(ridx == 0, p2, jnp.where(ridx == 1, p1, g)))
    xp1 = pltpu.roll(x, rows - 1, axis=0)
    xp1 = jnp.concatenate([xp1[:rows - S], jnp.where(ridx == S - 1, n1, xp1[rows - S:])], axis=0)
    return (xm2 * w_ref[0:1, :] + xm1 * w_ref[1:2, :] + x * w_ref[2:3, :]
            + xp1 * w_ref[3:4, :] + b_ref[...])


def _ml_proj_kernel(dest_ref, s_ref, wt_ref, pmod_ref, ys_ref, mod_ref, g_ref, w_ref, wg_ref,
                    gb_ref, snew_ref, qk_ref, v_ref, o_ref, gt_ref, ybuf, sem, **pending_kw):
    i = pl.program_id(0)
    x = _combined_tile(dest_ref, s_ref, wt_ref, pmod_ref, ys_ref, ybuf, sem, **pending_kw)
    snew_ref[...] = x
    shift, scale, _ = _mod_rows(mod_ref, i, 1, 0)
    h = _rms(x, g_ref[...]) * (1.0 + scale) + shift
    z = _bdot(h, w_ref[...])
    nqk = qk_ref.shape[1]
    nv = v_ref.shape[1]
    qk_ref[...] = z[:, :nqk]
    v_ref[...] = z[:, nqk:nqk + nv].astype(BF16)
    o_ref[...] = z[:, nqk + nv:].astype(BF16)
    pre = _dot3(h, wg_ref[...], ((1,), (0,))) + gb_ref[...]
    lane = lax.broadcasted_iota(jnp.int32, pre.shape, 1)
    is_forget = ((lane >> 2) & 1) == 1
    gt_ref[...] = jnp.where(is_forget, -_softplus(-pre), pre)


def _ml_proj(pending, mods, norm_g, w_in, gate_b):
    n, d = pending.s.shape
    kw, dest, p_specs, p_args, p_scratch = _pending_operands(pending)
    nqk = 2 * ML_HEADS * ML_DK
    nv = ML_HEADS * ML_DV
    n_main = nqk + 2 * nv
    n_gate = w_in.shape[1] - n_main
    w_main = w_in[:, :n_main].astype(BF16)
    w_gate = jnp.concatenate([w_in[:, n_main:], jnp.zeros((d, LANES - n_gate), F32)], axis=1)
    gb = jnp.concatenate([gate_b.reshape(n_gate), jnp.zeros((LANES - n_gate,), F32)]).reshape(1, LANES)
    tile = lambda w: pl.BlockSpec((TM, w), lambda i, dst: (i, 0))
    whole = lambda shape: pl.BlockSpec(shape, lambda i, dst: (0,) * len(shape))
    return pl.pallas_call(
        functools.partial(_ml_proj_kernel, **kw),
        out_shape=(jax.ShapeDtypeStruct((n, d), F32),
                   jax.ShapeDtypeStruct((n, nqk), F32), jax.ShapeDtypeStruct((n, nv), BF16),
                   jax.ShapeDtypeStruct((n, nv), BF16), jax.ShapeDtypeStruct((n, LANES), F32)),
        grid_spec=pltpu.PrefetchScalarGridSpec(
            num_scalar_prefetch=1,
            grid=(n // TM,),
            in_specs=p_specs + [whole(mods.shape), whole((1, d)), whole(w_main.shape),
                                whole((d, LANES)), whole((1, LANES))],
            out_specs=(tile(d), tile(nqk), tile(nv), tile(nv), tile(LANES)),
            scratch_shapes=p_scratch),
        compiler_params=_cparams("arbitrary"),
        name="mlstm_proj",
    )(dest, *p_args, mods, norm_g.reshape(1, d), w_main, w_gate, gb)


def _ml_chunk_index(j, n_chunks, ctx_chunks, reverse):
    if not reverse:
        return j
    return jnp.where(j < ctx_chunks, ctx_chunks - 1 - j, n_chunks - 1 + ctx_chunks - j)


def _ml_rec_kernel(qk_ref, qkp_ref, qkn_ref, v_ref, gt_ref, gtt_ref, cw_ref, cb_ref, o_ref,
                   c_scr, n_scr, m_scr, *, reverse, n_chunks, ctx_chunks):
    j = pl.program_id(0)
    c = _ml_chunk_index(j, n_chunks, ctx_chunks, reverse)

    @pl.when(j == 0)
    def _():
        c_scr[...] = jnp.zeros_like(c_scr)
        n_scr[...] = jnp.zeros_like(n_scr)
        m_scr[...] = jnp.zeros_like(m_scr)

    has_prev = (c != 0) & (c != ctx_chunks)
    has_next = (c != ctx_chunks - 1) & (c != n_chunks - 1)
    qk = _silu(_conv_tile(qk_ref[...], qkp_ref, qkn_ref, has_prev, has_next, cw_ref, cb_ref))
    L = ML_CHUNK
    ri = lax.broadcasted_iota(jnp.int32, (L, L), 0)
    ci = lax.broadcasted_iota(jnp.int32, (L, L), 1)
    past = (ci >= ri) if reverse else (ci <= ri)
    pastf = past.astype(F32)
    gt = gt_ref[...]
    gtt = gtt_ref[...]
    b_col = jnp.dot(pastf, gt, precision=HI, preferred_element_type=F32)
    b_row = jnp.dot(gtt, pastf.T, precision=HI, preferred_element_type=F32)
    last = 0 if reverse else L - 1
    dbase = 8 if reverse else 0
    nq = ML_HEADS * ML_DK
    for hd in range(ML_HEADS):
        cl = dbase + hd
        cf = dbase + 4 + hd
        q = qk[:, hd * ML_DK:(hd + 1) * ML_DK] * (ML_DK ** -0.5)
        k = qk[:, nq + hd * ML_DK:nq + (hd + 1) * ML_DK]
        v = v_ref[:, hd * ML_DV:(hd + 1) * ML_DV]
        li_c = gt[:, cl:cl + 1]
        li_r = gtt[cl:cl + 1, :]
        b_c = b_col[:, cf:cf + 1]
        b_r = b_row[cf:cf + 1, :]
        g = b_r[:, last:last + 1]
        m0 = m_scr[hd:hd + 1, 0:1]
        c0 = c_scr[hd]
        n0 = n_scr[hd:hd + 1, :]
        a_c = g - b_c + li_c
        a_r = g - b_r + li_r
        m_loc = jnp.max(a_r, axis=-1, keepdims=True)
        inter = b_c + m0
        dlog = jnp.where(past, b_c - b_r + li_r, -jnp.inf)
        m = jnp.maximum(inter, jnp.max(dlog, axis=-1, keepdims=True))
        qb = q.astype(BF16)
        sc = lax.dot_general(qb, k.astype(BF16), (((1,), (1,)), ((), ())),
                             preferred_element_type=F32) * jnp.exp(dlog - m)
        w_inter = jnp.exp(inter - m)
        num = (jnp.dot(sc.astype(BF16), v, preferred_element_type=F32)
               + w_inter * jnp.dot(qb, c0.astype(BF16), preferred_element_type=F32))
        den = (jnp.sum(sc, axis=-1, keepdims=True)
               + w_inter * jnp.sum(q * n0, axis=-1, keepdims=True))
        o_ref[:, hd * ML_DV:(hd + 1) * ML_DV] = (
            num / jnp.maximum(jnp.abs(den), jnp.exp(-m))).astype(BF16)
        m_new = jnp.maximum(g + m0, m_loc)
        dec = jnp.exp(g + m0 - m_new)
        scl = jnp.exp(m_loc - m_new)
        kw = k * jnp.exp(a_c - m_loc)
        c_scr[hd] = dec * c0 + scl * jnp.dot(kw.T.astype(BF16), v, preferred_element_type=F32)
        n_scr[hd:hd + 1, :] = dec * n0 + scl * jnp.sum(kw, axis=0, keepdims=True)
        m_scr[hd:hd + 1, :] = jnp.broadcast_to(m_new, (1, LANES))


def _ml_rec(qk, v, gt, gtt, conv_w, conv_b, reverse):
    n, nqk = qk.shape
    nv = v.shape[1]
    L = ML_CHUNK
    nc = n // L
    cc = TM // L
    hb = L // SUBLANES
    idx = lambda j: _ml_chunk_index(j, nc, cc, reverse)
    last8 = n // SUBLANES - 1
    return pl.pallas_call(
        functools.partial(_ml_rec_kernel, reverse=reverse, n_chunks=nc, ctx_chunks=cc),
        out_shape=jax.ShapeDtypeStruct((n, nv), BF16),
        grid=(nc,),
        in_specs=[pl.BlockSpec((L, nqk), lambda j: (idx(j), 0)),
                  pl.BlockSpec((SUBLANES, nqk), lambda j: (jnp.maximum(idx(j) * hb - 1, 0), 0)),
                  pl.BlockSpec((SUBLANES, nqk), lambda j: (jnp.minimum((idx(j) + 1) * hb, last8), 0)),
                  pl.BlockSpec((L, nv), lambda j: (idx(j), 0)),
                  pl.BlockSpec((L, LANES), lambda j: (idx(j), 0)),
                  pl.BlockSpec((2 * SUBLANES, L), lambda j: (0, idx(j))),
                  _full((4, nqk)), _full((1, nqk))],
        out_specs=pl.BlockSpec((L, nv), lambda j: (idx(j), 0)),
        scratch_shapes=[pltpu.VMEM((ML_HEADS, ML_DK, ML_DV), F32),
                        pltpu.VMEM((SUBLANES, ML_DK), F32),
                        pltpu.VMEM((SUBLANES, LANES), F32)],
        compiler_params=_cparams("arbitrary"),
        name="mlstm_rev" if reverse else "mlstm_fwd",
    )(qk, qk, qk, v, gt, gtt, conv_w, conv_b.reshape(1, nqk))


def _ml_out_kernel(hf_ref, hr_ref, o_ref, s_ref, mod_ref, ng_ref, w_ref, out_ref, p_scr):
    i = pl.program_id(0)
    gate = _mod_rows(mod_ref, i, 1, 0)[2]
    hs = hf_ref[...].astype(F32) + hr_ref[...].astype(F32)
    sig = _sigmoid(o_ref[...].astype(F32))
    ng = ng_ref[...]
    for hd in range(ML_HEADS):
        cs = slice(hd * ML_DV, (hd + 1) * ML_DV)
        seg = hs[:, cs]
        hn = seg * lax.rsqrt(jnp.mean(seg * seg, axis=-1, keepdims=True) + EPS) * ng[:, cs]
        p_scr[:, cs] = (hn * sig[:, cs]).astype(BF16)
    y = jnp.dot(p_scr[...], w_ref[...], preferred_element_type=F32)
    out_ref[...] = s_ref[...] + gate * y


def _ml_out(hf, hr, o, s, mods, norm_g, w_out):
    n, d = s.shape
    nv = hf.shape[1]
    tile = lambda w: pl.BlockSpec((TM, w), lambda i: (i, 0))
    return pl.pallas_call(
        _ml_out_kernel,
        out_shape=jax.ShapeDtypeStruct((n, d), F32),
        grid=(n // TM,),
        in_specs=[tile(nv), tile(nv), tile(nv), tile(d), _full(mods.shape), _full((1, nv)),
                  _full(w_out.shape)],
        out_specs=tile(d),
        scratch_shapes=[pltpu.VMEM((TM, nv), BF16)],
        compiler_params=_cparams("arbitrary"),
        name="mlstm_out",
    )(hf, hr, o, s, mods, norm_g.reshape(1, nv), w_out.astype(BF16))


def _mlstm_layer(pending, mods, norm_g, w_in, conv_w, conv_b, gate_b, ml_norm_g, w_out):
    s, qk, v, o, gt = _ml_proj(pending, mods, norm_g, w_in, gate_b)
    gtt = gt[:, :2 * SUBLANES].T
    hf = _ml_rec(qk, v, gt, gtt, conv_w, conv_b, False)
    hr = _ml_rec(qk, v, gt, gtt, conv_w, conv_b, True)
    return _ml_out(hf, hr, o, s, mods, ml_norm_g, w_out)


def _lru_proj_kernel(dest_ref, s_ref, wt_ref, pmod_ref, ys_ref, mod_ref, g_ref, w_ref,
                     snew_ref, gl_ref, xr_ref, ybuf, sem, **pending_kw):
    i = pl.program_id(0)
    x = _combined_tile(dest_ref, s_ref, wt_ref, pmod_ref, ys_ref, ybuf, sem, **pending_kw)
    snew_ref[...] = x
    shift, scale, _ = _mod_rows(mod_ref, i, 1, 0)
    h = _rms(x, g_ref[...]) * (1.0 + scale) + shift
    z = _bdot(h, w_ref[...])
    w = gl_ref.shape[1]
    gl_ref[...] = _gelu(z[:, :w]).astype(BF16)
    xr_ref[...] = z[:, w:]


def _lru_proj(pending, mods, norm_g, w_in):
    n, d = pending.s.shape
    kw, dest, p_specs, p_args, p_scratch = _pending_operands(pending)
    w = w_in.shape[1] // 2
    tile = lambda c: pl.BlockSpec((TM, c), lambda i, dst: (i, 0))
    whole = lambda shape: pl.BlockSpec(shape, lambda i, dst: (0,) * len(shape))
    return pl.pallas_call(
        functools.partial(_lru_proj_kernel, **kw),
        out_shape=(jax.ShapeDtypeStruct((n, d), F32), jax.ShapeDtypeStruct((n, w), BF16),
                   jax.ShapeDtypeStruct((n, w), F32)),
        grid_spec=pltpu.PrefetchScalarGridSpec(
            num_scalar_prefetch=1,
            grid=(n // TM,),
            in_specs=p_specs + [whole(mods.shape), whole((1, d)), whole(w_in.shape)],
            out_specs=(tile(d), tile(w), tile(w)),
            scratch_shapes=p_scratch),
        compiler_params=_cparams("arbitrary"),
        name="rglru_proj",
    )(dest, *p_args, mods, norm_g.reshape(1, d), w_in.astype(BF16))


def _lru_tile_index(j, n_tiles, reverse):
    if not reverse:
        return j
    return jnp.where(j == 0, 0, n_tiles - j)


def _lru_scan_kernel(x_ref, xp_ref, xn_ref, cw_ref, cb_ref, wg_ref, ba_ref, bx_ref, lam_ref,
                     o_ref, a_scr, u_scr, carry, *, reverse, n_tiles):
    j = pl.program_id(0)
    t = _lru_tile_index(j, n_tiles, reverse)

    @pl.when(j == 0)
    def _():
        carry[...] = jnp.zeros_like(carry)

    has_prev = t > 1
    has_next = (t != 0) & (t != n_tiles - 1)
    xr = _conv_tile(x_ref[...], xp_ref, xn_ref, has_prev, has_next, cw_ref, cb_ref)
    sp = _softplus(-lam_ref[...])
    B = LRU_BLOCK
    for hd in range(LRU_HEADS):
        cs = slice(hd * B, (hd + 1) * B)
        xh = xr[:, cs]
        y = jnp.dot(xh.astype(BF16), wg_ref[hd], preferred_element_type=F32)
        r = _sigmoid(y[:, :B] + ba_ref[:, cs])
        ig = _sigmoid(y[:, B:] + bx_ref[:, cs])
        log_a = -LRU_C * r * sp[:, cs]
        a = jnp.exp(log_a)
        a_scr[:, cs] = a
        v = 1.0 - a * a
        u_scr[:, cs] = jnp.where(v > 0.0, v * lax.rsqrt(v), 0.0) * (ig * xh)

    S = SUBLANES
    w = a_scr.shape[1]
    sidx = lax.broadcasted_iota(jnp.int32, (S, w), 0)

    def group(gi, c):
        g = (TM // S - 1 - gi) if reverse else gi
        r0 = pl.multiple_of(g * S, S)
        a = a_scr[pl.ds(r0, S), :]
        u = u_scr[pl.ds(r0, S), :]
        for sft in (1, 2, 4):
            if reverse:
                ok = sidx < S - sft
                a_e = pltpu.roll(a, S - sft, axis=0)
                u_e = pltpu.roll(u, S - sft, axis=0)
            else:
                ok = sidx >= sft
                a_e = pltpu.roll(a, sft, axis=0)
                u_e = pltpu.roll(u, sft, axis=0)
            u = jnp.where(ok, a * u_e + u, u)
            a = jnp.where(ok, a * a_e, a)
        hcur = a * carry[...] + u
        u_scr[pl.ds(r0, S), :] = hcur
        edge = 0 if reverse else S - 1
        carry[...] = jnp.broadcast_to(hcur[edge:edge + 1, :], (S, w))
        return c

    lax.fori_loop(0, TM // S, group, 0)
    o_ref[...] = u_scr[...].astype(BF16)


def _lru_scan(xraw, conv_w, conv_b, w_a, b_a, w_x, b_x, lam, reverse):
    n, w = xraw.shape
    nt = n // TM
    hb = TM // SUBLANES
    idx = lambda j: _lru_tile_index(j, nt, reverse)
    last8 = n // SUBLANES - 1
    wg = jnp.concatenate([w_a, w_x], axis=-1).astype(BF16)
    return pl.pallas_call(
        functools.partial(_lru_scan_kernel, reverse=reverse, n_tiles=nt),
        out_shape=jax.ShapeDtypeStruct((n, w), BF16),
        grid=(nt,),
        in_specs=[pl.BlockSpec((TM, w), lambda j: (idx(j), 0)),
                  pl.BlockSpec((SUBLANES, w), lambda j: (jnp.maximum(idx(j) * hb - 1, 0), 0)),
                  pl.BlockSpec((SUBLANES, w), lambda j: (jnp.minimum((idx(j) + 1) * hb, last8), 0)),
                  _full((4, w)), _full((1, w)), _full(wg.shape), _full((1, w)), _full((1, w)),
                  _full((1, w))],
        out_specs=pl.BlockSpec((TM, w), lambda j: (idx(j), 0)),
        scratch_shapes=[pltpu.VMEM((TM, w), F32), pltpu.VMEM((TM, w), F32),
                        pltpu.VMEM((SUBLANES, w), F32)],
        compiler_params=_cparams("arbitrary"),
        name="rglru_rev" if reverse else "rglru_fwd",
    )(xraw, xraw, xraw, conv_w, conv_b.reshape(1, w), wg, b_a.reshape(1, w), b_x.reshape(1, w),
      lam.reshape(1, w))


def _lru_out_kernel(gl_ref, hf_ref, hr_ref, s_ref, mod_ref, w_ref, out_ref):
    gate = _mod_rows(mod_ref, 1, 0, 0)[2]
    p = gl_ref[...].astype(F32) * (hf_ref[...].astype(F32) + hr_ref[...].astype(F32))
    out_ref[...] = s_ref[...] + gate * _bdot(p, w_ref[...])


def _lru_out(gl, hf, hr, s, mods, w_out):
    n, d = s.shape
    w = gl.shape[1]
    lat = lambda c: pl.BlockSpec((TM, c), lambda i: (i + 1, 0))
    return pl.pallas_call(
        _lru_out_kernel,
        out_shape=jax.ShapeDtypeStruct((n - TM, d), F32),
        grid=(n // TM - 1,),
        in_specs=[lat(w), lat(w), lat(w), lat(d), _full(mods.shape), _full(w_out.shape)],
        out_specs=pl.BlockSpec((TM, d), lambda i: (i, 0)),
        compiler_params=_cparams("arbitrary"),
        name="rglru_out",
    )(gl, hf, hr, s, mods, w_out.astype(BF16))


def _rglru_layer(pending, mods, norm_g, w_in, conv_w, conv_b, w_a, b_a, w_x, b_x, lam, w_out):
    s, gl, xraw = _lru_proj(pending, mods, norm_g, w_in)
    hf = _lru_scan(xraw, conv_w, conv_b, w_a[0], b_a[0], w_x[0], b_x[0], lam[0], False)
    hr = _lru_scan(xraw, conv_w, conv_b, w_a[1], b_a[1], w_x[1], b_x[1], lam[1], True)
    return _lru_out(gl, hf, hr, s, mods, w_out)


def _fn_proj_kernel(s_ref, mod_ref, g_ref, wt_ref, cs_ref, yr_ref, yi_ref, ar_scr, ai_scr):
    shift, scale, _ = _mod_rows(mod_ref, 1, 0, 0)
    nm = wt_ref.shape[0]
    gw = nm // FN_GROUPS
    per = TM // FFT_N2
    nj = FN_TB // FFT_N2
    csb = cs_ref[...].astype(BF16)

    def sub(tc, c):
        r0 = pl.multiple_of(tc * TM, TM)
        h = _rms(s_ref[pl.ds(r0, TM), :], g_ref[...]) * (1.0 + scale) + shift
        zt = lax.dot_general(wt_ref[...], h.astype(BF16), (((1,), (1,)), ((), ())),
                             preferred_element_type=F32).astype(BF16)
        for g in range(FN_GROUPS):
            y = jnp.dot(csb, zt[g * gw:(g + 1) * gw, :], preferred_element_type=F32)
            for q in range(per):
                row0 = pl.multiple_of((tc * per + q) * _slab_pitch(nm) + g * gw, SUBLANES)
                ar_scr[pl.ds(row0, gw), :] = y[:gw, q * FFT_N2:(q + 1) * FFT_N2]
                ai_scr[pl.ds(row0, gw), :] = y[gw:, q * FFT_N2:(q + 1) * FFT_N2]
        return c

    lax.fori_loop(0, FN_TB // TM, sub, 0)

    def relayout(m, c):
        yr_ref[m] = ar_scr[pl.ds(m, nj, stride=_slab_pitch(nm)), :]
        yi_ref[m] = ai_scr[pl.ds(m, nj, stride=_slab_pitch(nm)), :]
        return c

    lax.fori_loop(0, nm, relayout, 0, unroll=8)


def _slab_pitch(rows):
    return rows + SUBLANES


def _dft_cos_sin(n, scale):
    k = np.arange(n, dtype=np.int64)
    ang = 2.0 * np.pi * ((k[:, None] * k[None, :]) % n).astype(np.float64) / n
    return np.cos(ang) * scale, np.sin(ang) * scale


def _fn_proj(s, mods, norm_g, w_in):
    t, d = s.shape
    nm = w_in.shape[1]
    gw = nm // FN_GROUPS
    n1 = t // FFT_N2
    nj = FN_TB // FFT_N2
    c, sn = _dft_cos_sin(gw, gw ** -0.5)
    cs = jnp.asarray(np.concatenate([c, -sn], axis=0), F32)
    yspec = pl.BlockSpec((nm, nj, FFT_N2), lambda i: (0, i, 0))
    yshape = jax.ShapeDtypeStruct((nm, n1, FFT_N2), F32)
    return pl.pallas_call(
        _fn_proj_kernel,
        out_shape=(yshape, yshape),
        grid=(t // FN_TB,),
        in_specs=[pl.BlockSpec((FN_TB, d), lambda i: (i, 0)), _full(mods.shape), _full((1, d)),
                  _full((nm, d)), _full(cs.shape)],
        out_specs=(yspec, yspec),
        scratch_shapes=[pltpu.VMEM((nj * _slab_pitch(nm), FFT_N2), F32),
                        pltpu.VMEM((nj * _slab_pitch(nm), FFT_N2), F32)],
        compiler_params=_cparams("arbitrary"),
        name="fourier_proj",
    )(s, mods, norm_g.reshape(1, d), w_in.T.astype(BF16), cs)


def _fn_fft_kernel(yr_ref, yi_ref, m_ref, tc_ref, ts_ref, d_ref, o_ref):
    n1 = yr_ref.shape[1]
    n2 = FFT_N2
    xr = jnp.concatenate([yr_ref[m].astype(BF16) for m in range(FN_CB)], axis=1)
    xi = jnp.concatenate([yi_ref[m].astype(BF16) for m in range(FN_CB)], axis=1)
    a = jnp.dot(m_ref[...].astype(BF16), jnp.concatenate([xr, xi], axis=0),
                preferred_element_type=F32)
    ar = a[:n1]
    ai = a[n1:]
    tc = jnp.concatenate([tc_ref[...]] * FN_CB, axis=1)
    ts = jnp.concatenate([ts_ref[...]] * FN_CB, axis=1)
    br = ar * tc + ai * ts
    bi = ai * tc - ar * ts
    bst = jnp.concatenate(
        [jnp.concatenate([br[:, m * n2:(m + 1) * n2], bi[:, m * n2:(m + 1) * n2]], axis=1)
         for m in range(FN_CB)], axis=0).astype(BF16)
    res = lax.dot_general(d_ref[...].astype(BF16), bst, (((1,), (1,)), ((), ())),
                          preferred_element_type=F32)
    for m in range(FN_CB):
        o_ref[m] = res[:, m * n1:(m + 1) * n1]


def _fn_fft(yr, yi):
    nm, n1, n2 = yr.shape
    t = n1 * n2
    c, sn = _dft_cos_sin(n1, n1 ** -0.5)
    m = jnp.asarray(np.block([[c, sn], [-sn, c]]), F32)
    k1 = np.arange(n1, dtype=np.int64)[:, None]
    t2 = np.arange(n2, dtype=np.int64)[None, :]
    ang = 2.0 * np.pi * ((k1 * t2) % t).astype(np.float64) / t
    tc = jnp.asarray(np.cos(ang), F32)
    ts = jnp.asarray(np.sin(ang), F32)
    c2, s2 = _dft_cos_sin(n2, n2 ** -0.5)
    dm = jnp.asarray(np.concatenate([c2, s2], axis=1), F32)
    yspec = pl.BlockSpec((FN_CB, n1, n2), lambda i: (i, 0, 0))
    return pl.pallas_call(
        _fn_fft_kernel,
        out_shape=jax.ShapeDtypeStruct((nm, n2, n1), F32),
        grid=(nm // FN_CB,),
        in_specs=[yspec, yspec, _full(m.shape), _full(tc.shape), _full(ts.shape), _full(dm.shape)],
        out_specs=pl.BlockSpec((FN_CB, n2, n1), lambda i: (i, 0, 0)),
        compiler_params=_cparams("arbitrary"),
        name="fourier_fft",
    )(yr, yi, m, tc, ts, dm)


def _fn_out_kernel(ft_ref, w_ref, s_ref, mod_ref, o_ref, a_scr):
    gate = _mod_rows(mod_ref, 1, 0, 0)[2]
    nm, nj, n1 = ft_ref.shape

    def relayout(m, c):
        a_scr[pl.ds(m, nj, stride=_slab_pitch(nm)), :] = ft_ref[m]
        return c

    lax.fori_loop(0, nm, relayout, 0, unroll=8)
    for j in range(nj):
        p0 = j * _slab_pitch(nm)
        slab = a_scr[p0:p0 + nm, :].astype(BF16)
        y = lax.dot_general(slab, w_ref[...], (((0,), (0,)), ((), ())), preferred_element_type=F32)
        rows = slice(j * n1, (j + 1) * n1)
        o_ref[rows, :] = s_ref[rows, :] + gate * y


def _fn_out(ft, s, mods, w_out):
    t, d = s.shape
    nm, n2, n1 = ft.shape
    nj = FN_TB // n1
    tok = pl.BlockSpec((FN_TB, d), lambda i: (i, 0))
    return pl.pallas_call(
        _fn_out_kernel,
        out_shape=jax.ShapeDtypeStruct((t, d), F32),
        grid=(t // FN_TB,),
        in_specs=[pl.BlockSpec((nm, nj, n1), lambda i: (0, i, 0)), _full(w_out.shape), tok,
                  _full(mods.shape)],
        out_specs=tok,
        scratch_shapes=[pltpu.VMEM((nj * _slab_pitch(nm), n1), F32)],
        compiler_params=_cparams("arbitrary"),
        name="fourier_out",
    )(ft, w_out.astype(BF16), s, mods)


def _fourier_layer(s, mods, norm_g, w_in, w_out):
    yr, yi = _fn_proj(s, mods, norm_g, w_in)
    return _fn_out(_fn_fft(yr, yi), s, mods, w_out)


def kernel(x, c, ctx, c_ctx, ada_w, ada_b, norm_mix_g, norm_ffn_g, final_norm_g, router_group_w, router_group_b, router_expert_w, router_expert_b, expert_w_gate, expert_w_up, expert_w_down, cm_w_in, cm_v_norm_g, cm_w_s, cm_b_s, cm_w_out, ml_w_in, ml_conv_w, ml_conv_b, ml_gate_b, ml_norm_g, ml_w_out, lru_w_in, lru_conv_w, lru_conv_b, lru_w_a, lru_b_a, lru_w_x, lru_b_x, lru_lambda, lru_w_out, fn_w_in, fn_w_out):
    bsz, seq, d = x.shape
    assert bsz == 1 and ada_w.shape[0] == 4 and ctx.shape[1] == TM
    c_rows = jnp.concatenate([c_ctx[None, :], c, jnp.zeros((SUBLANES - 2, d), F32)], axis=0)
    mods = _ada_table(c_rows, ada_w, ada_b)

    def moe(s, i, ctx_tiles, final_norm=False, defer=False):
        return _moe_layer(s, mods[i], norm_ffn_g[i], router_group_w[i], router_group_b[i],
                          router_expert_w[i], router_expert_b[i], expert_w_gate, expert_w_up,
                          expert_w_down, i, ctx_tiles, final_norm_g, final_norm, defer)

    s = _chunk_mlp_layer(x[0], ctx[0], mods[0], norm_mix_g[0], cm_w_in[0], cm_v_norm_g[0],
                         cm_w_s[0], cm_b_s[0], cm_w_out[0])
    s = moe(s, 0, 1, defer=True)
    s = _mlstm_layer(s, mods[1], norm_mix_g[1], ml_w_in[0], ml_conv_w[0], ml_conv_b[0],
                     ml_gate_b[0], ml_norm_g[0], ml_w_out[0])
    s = moe(s, 1, 1, defer=True)
    s = _rglru_layer(s, mods[2], norm_mix_g[2], lru_w_in[0], lru_conv_w[0], lru_conv_b[0],
                     lru_w_a[0], lru_b_a[0], lru_w_x[0], lru_b_x[0], lru_lambda[0], lru_w_out[0])
    s = moe(s, 2, 0)
    s = _fourier_layer(s, mods[3], norm_mix_g[3], fn_w_in[0], fn_w_out[0])
    s = moe(s, 3, 0, final_norm=True)
    return s[None]
```

```python
import functools
import math
from typing import NamedTuple

import jax
import jax.numpy as jnp
import numpy as np
from jax import lax
from jax.experimental import pallas as pl
from jax.experimental.pallas import tpu as pltpu

F32 = jnp.float32
BF16 = jnp.bfloat16

EPS = 1e-6
POS_BASE = 10000.0
GRID_W = 64
N_MOD = 6
TM = 256
LANES = 128
SUBLANES = 8
VMEM_LIMIT = 56 * 1024 * 1024

CM_CHUNK = 128
CM_GROUPS = 4
ML_HEADS = 4
ML_DK = 128
ML_DV = 256
ML_CHUNK = 128
LRU_HEADS = 10
LRU_BLOCK = 128
LRU_C = 8.0
FN_GROUPS = 4
FFT_N2 = 128
MOE_GROUPS = 4
MOE_EPG = 8
MOE_EXPERTS = MOE_GROUPS * MOE_EPG
MOE_ROWS_LOG2 = 8
MOE_ROWS = 1 << MOE_ROWS_LOG2
ROUTE_ROWS = 40
FN_TB = 1024
FN_CB = 32
CONV_LEFT = 2

HI = lax.Precision.HIGHEST


def _cparams(*sem):
    return pltpu.CompilerParams(dimension_semantics=sem, vmem_limit_bytes=VMEM_LIMIT)


def _full(shape):
    nd = len(shape)
    return pl.BlockSpec(shape, lambda *_: (0,) * nd)


def _rms(x, g):
    return x * lax.rsqrt(jnp.mean(x * x, axis=-1, keepdims=True) + EPS) * g


def _gelu(x):
    c = math.sqrt(2.0 / math.pi)
    return 0.5 * x * (1.0 + jnp.tanh(c * (x + 0.044715 * (x * x * x))))


def _sigmoid(x):
    return 0.5 * jnp.tanh(0.5 * x) + 0.5


def _silu(x):
    return x * _sigmoid(x)


def _softplus(x):
    return jnp.maximum(x, 0.0) + jnp.log(1.0 + jnp.exp(-jnp.abs(x)))


def _mod_rows(mod_ref, tile, ctx_tiles, first):
    row = jnp.where(tile < ctx_tiles, 0, 1)
    m = mod_ref[pl.ds(row, 1), :]
    d = m.shape[1] // N_MOD
    return tuple(m[:, (first + j) * d:(first + j + 1) * d] for j in range(3))


def _bdot(a, b):
    return jnp.dot(a.astype(BF16), b.astype(BF16), preferred_element_type=F32)


def _split_bf16(x):
    hi = x.astype(BF16)
    return hi, (x - hi.astype(F32)).astype(BF16)


def _dot3(a, b, dims):
    a_hi, a_lo = _split_bf16(a)
    b_hi, b_lo = _split_bf16(b)
    dg = functools.partial(lax.dot_general, dimension_numbers=(dims, ((), ())),
                           preferred_element_type=F32)
    return dg(a_hi, b_hi) + dg(a_hi, b_lo) + dg(a_lo, b_hi)


def _ada_kernel(c_ref, w_ref, b_ref, o_ref):
    c = c_ref[...]
    o_ref[...] = _dot3(_silu(c), w_ref[...], ((1,), (0,))) + b_ref[...]


def _ada_table(c_rows, ada_w, ada_b):
    depth, d, n = ada_w.shape
    tn = 3072
    return pl.pallas_call(
        _ada_kernel,
        out_shape=jax.ShapeDtypeStruct((depth, SUBLANES, n), F32),
        grid=(depth, n // tn),
        in_specs=[_full((SUBLANES, d)),
                  pl.BlockSpec((None, d, tn), lambda i, j: (i, 0, j)),
                  pl.BlockSpec((None, 1, tn), lambda i, j: (i, 0, j))],
        out_specs=pl.BlockSpec((None, SUBLANES, tn), lambda i, j: (i, 0, j)),
        compiler_params=_cparams("arbitrary", "arbitrary"),
        name="ada_table",
    )(c_rows, ada_w, ada_b.reshape(depth, 1, n))


def _pos_tables(seq, d):
    q = d // 4
    freq = jnp.exp(-math.log(POS_BASE) * jnp.arange(q, dtype=F32) / q)
    ar = jnp.arange(seq // GRID_W, dtype=F32)[:, None] * freq
    ac = jnp.arange(GRID_W, dtype=F32)[:, None] * freq
    return (jnp.concatenate([jnp.sin(ar), jnp.cos(ar)], axis=-1),
            jnp.concatenate([jnp.sin(ac), jnp.cos(ac)], axis=-1))


def _cm_kernel(x_ref, ctx_ref, rt_ref, ct_ref, mod_ref, g_ref, win_ref, vg_ref, ws_ref, bs_ref,
               wout_ref, o_ref, p_scr, x_scr):
    i = pl.program_id(0)

    @pl.when(i == 0)
    def _():
        x_scr[...] = ctx_ref[...]

    @pl.when(i > 0)
    def _():
        rows_per_tile = TM // GRID_W
        q2 = rt_ref.shape[1]
        r0 = (i - 1) * rows_per_tile
        rt = jnp.concatenate(
            [jnp.broadcast_to(rt_ref[pl.ds(r0 + j, 1), :], (GRID_W, q2))
             for j in range(rows_per_tile)], axis=0)
        ct = jnp.concatenate([ct_ref[...]] * rows_per_tile, axis=0)
        x_scr[...] = x_ref[...] + jnp.concatenate([rt, ct], axis=1)

    shift, scale, gate = _mod_rows(mod_ref, i, 1, 0)
    x = x_scr[...]
    h = _rms(x, g_ref[...]) * (1.0 + scale) + shift
    z = _gelu(_bdot(h, win_ref[...]))
    w = z.shape[1] // 2
    u = z[:, :w]
    v = _rms(z[:, w:], vg_ref[...]).astype(BF16)
    gw = w // CM_GROUPS
    for c in range(TM // CM_CHUNK):
        r = slice(c * CM_CHUNK, (c + 1) * CM_CHUNK)
        for g in range(CM_GROUPS):
            cs = slice(g * gw, (g + 1) * gw)
            s = jnp.dot(ws_ref[g], v[r, cs], preferred_element_type=F32) + bs_ref[:, g:g + 1]
            p_scr[r, cs] = (u[r, cs] * s).astype(BF16)
    y = jnp.dot(p_scr[...], wout_ref[...], preferred_element_type=F32)
    o_ref[...] = x + gate * y


def _chunk_mlp_layer(x2, ctx2, mods, norm_g, w_in, v_g, w_s, b_s, w_out):
    seq, d = x2.shape
    n_ctx = ctx2.shape[0]
    assert n_ctx == TM and seq % TM == 0 and TM % GRID_W == 0
    n = n_ctx + seq
    w = w_out.shape[0]
    rt, ct = _pos_tables(seq, d)
    return pl.pallas_call(
        _cm_kernel,
        out_shape=jax.ShapeDtypeStruct((n, d), F32),
        grid=(n // TM,),
        in_specs=[pl.BlockSpec((TM, d), lambda i: (jnp.maximum(i - 1, 0), 0)),
                  _full((TM, d)), _full(rt.shape), _full(ct.shape),
                  _full(mods.shape), _full((1, d)), _full(w_in.shape), _full((1, w)),
                  _full(w_s.shape), _full((CM_CHUNK, CM_GROUPS)), _full(w_out.shape)],
        out_specs=pl.BlockSpec((TM, d), lambda i: (i, 0)),
        scratch_shapes=[pltpu.VMEM((TM, w), BF16), pltpu.VMEM((TM, d), F32)],
        compiler_params=_cparams("arbitrary"),
        name="chunk_mlp",
    )(x2, ctx2, rt, ct, mods, norm_g.reshape(1, d), w_in.astype(BF16), v_g.reshape(1, w),
      w_s.astype(BF16), b_s.T, w_out.astype(BF16))


def _store_token_tiles(ref, x):
    rows, d = x.shape
    for j in range(d // LANES):
        ref[pl.ds(j, rows, stride=d // LANES), :] = x[:, j * LANES:(j + 1) * LANES]


def _load_token_tiles(ref):
    chunks = SUBLANES
    rows = ref.shape[0] // chunks
    return jnp.concatenate([ref[pl.ds(j, rows, stride=chunks), :] for j in range(chunks)], axis=1)


def _route_tiles(nt):
    return next(k for k in (5, 4, 2, 1) if nt % k == 0)


def _router_kernel(s_ref, mod_ref, g_ref, rwt_ref, rbt_ref, tri_ref, h_ref, e1_ref, e2_ref, r1_ref,
                   r2_ref, wt_ref, cnt_ref, carry, *, ctx_rows):
    i = pl.program_id(0)
    rows, d = s_ref.shape

    @pl.when(i == 0)
    def _():
        carry[...] = jnp.zeros_like(carry)

    lat = mod_ref[1:2, :]
    shift, scale = lat[:, 3 * d:4 * d], lat[:, 4 * d:5 * d]
    if ctx_rows:
        ctx = mod_ref[0:1, :]
        is_ctx = (i == 0) & (lax.broadcasted_iota(jnp.int32, (rows, 1), 0) < ctx_rows)
        shift = jnp.where(is_ctx, ctx[:, 3 * d:4 * d], shift)
        scale = jnp.where(is_ctx, ctx[:, 4 * d:5 * d], scale)
    h = _rms(s_ref[...], g_ref[...]) * (1.0 + scale) + shift
    _store_token_tiles(h_ref, h)
    logits = _dot3(rwt_ref[...], h, ((1,), (1,))) + rbt_ref[...]
    row = lax.broadcasted_iota(jnp.int32, logits.shape, 0)
    neg = jnp.float32(-jnp.inf)
    big = jnp.int32(1 << 20)
    is_g = row < MOE_GROUPS
    gl = jnp.where(is_g, logits, neg)
    gmax = jnp.max(gl, axis=0, keepdims=True)
    grp = jnp.min(jnp.where(is_g & (gl == gmax), row, big), axis=0, keepdims=True)
    p_grp = 1.0 / jnp.sum(jnp.exp(gl - gmax), axis=0, keepdims=True)
    e_row = row - MOE_GROUPS
    in_grp = (e_row >= 0) & (e_row < MOE_EXPERTS) & ((e_row >> 3) == grp)
    l1 = jnp.where(in_grp, logits, neg)
    v1 = jnp.max(l1, axis=0, keepdims=True)
    i1 = jnp.min(jnp.where(in_grp & (l1 == v1), row, big), axis=0, keepdims=True)
    rest = in_grp & (row != i1)
    l2 = jnp.where(rest, logits, neg)
    v2 = jnp.max(l2, axis=0, keepdims=True)
    i2 = jnp.min(jnp.where(rest & (l2 == v2), row, big), axis=0, keepdims=True)
    e21 = jnp.exp(v2 - v1)
    w1 = p_grp / (1.0 + e21)
    w2 = p_grp * e21 / (1.0 + e21)
    oh1 = (row == i1).astype(F32)
    oh2 = (row == i2).astype(F32)
    oh = oh1 + oh2
    before = jnp.dot(oh.astype(BF16), tri_ref[...], preferred_element_type=F32) + carry[:, 0:1]
    r1_ref[...] = jnp.sum(oh1 * before, axis=0, keepdims=True).astype(jnp.int32)
    r2_ref[...] = jnp.sum(oh2 * before, axis=0, keepdims=True).astype(jnp.int32)
    e1_ref[...] = i1 - MOE_GROUPS
    e2_ref[...] = i2 - MOE_GROUPS
    carry[...] = carry[...] + jnp.sum(oh, axis=1, keepdims=True)
    cnt_ref[...] = carry[...]
    wt_ref[...] = jnp.concatenate([w1, w2, jnp.zeros((LANES - 2, rows), F32)], axis=0).T


def _router(s, mods, norm_g, rg_w, rg_b, re_w, re_b, ctx_tiles):
    n, d = s.shape
    rows = TM * _route_tiles(n // TM)
    steps = n // rows
    pad = ROUTE_ROWS - MOE_GROUPS - MOE_EXPERTS
    rwt = jnp.concatenate([rg_w, re_w, jnp.zeros((d, pad), F32)], axis=1).T
    rbt = jnp.broadcast_to(jnp.concatenate([rg_b, re_b, jnp.zeros((pad,), F32)])[:, None],
                           (ROUTE_ROWS, rows))
    tri = jnp.asarray(np.triu(np.ones((rows, rows), np.float32), 1), BF16)
    assert d == SUBLANES * LANES
    tile = pl.BlockSpec((rows, d), lambda i: (i, 0))
    irow = pl.BlockSpec((None, 1, rows), lambda i: (i, 0, 0))
    ishape = jax.ShapeDtypeStruct((steps, 1, rows), jnp.int32)
    return pl.pallas_call(
        functools.partial(_router_kernel, ctx_rows=ctx_tiles * TM),
        out_shape=(jax.ShapeDtypeStruct((n * SUBLANES, LANES), F32), ishape, ishape, ishape, ishape,
                   jax.ShapeDtypeStruct((n, LANES), F32),
                   jax.ShapeDtypeStruct((ROUTE_ROWS, LANES), F32)),
        grid=(steps,),
        in_specs=[tile, _full(mods.shape), _full((1, d)), _full((ROUTE_ROWS, d)),
                  _full((ROUTE_ROWS, rows)), _full((rows, rows))],
        out_specs=(pl.BlockSpec((rows * SUBLANES, LANES), lambda i: (i, 0)), irow, irow, irow, irow,
                   pl.BlockSpec((rows, LANES), lambda i: (i, 0)), _full((ROUTE_ROWS, LANES))),
        scratch_shapes=[pltpu.VMEM((ROUTE_ROWS, LANES), F32)],
        compiler_params=_cparams("arbitrary"),
        name="moe_router",
    )(s, mods, norm_g.reshape(1, d), rwt, rbt, tri)


def _finalize_kernel(cnt_ref, e1_ref, e2_ref, r1_ref, r2_ref, d1_ref, d2_ref, blk_ref):
    e1 = e1_ref[...]
    e2 = e2_ref[...]
    r1 = r1_ref[...]
    r2 = r2_ref[...]
    d1 = jnp.zeros_like(e1)
    d2 = jnp.zeros_like(e2)
    lane = lax.broadcasted_iota(jnp.int32, blk_ref.shape, 1)
    brow = lane * MOE_ROWS
    sub = lax.broadcasted_iota(jnp.int32, blk_ref.shape, 0)
    be = jnp.zeros(blk_ref.shape, jnp.int32)
    pend = jnp.zeros(blk_ref.shape, jnp.int32)
    ps = jnp.int32(0)
    for e in range(MOE_EXPERTS):
        c = cnt_ref[e]
        pe = ps + lax.shift_left(lax.shift_right_logical(c + (MOE_ROWS - 1), MOE_ROWS_LOG2),
                                 MOE_ROWS_LOG2)
        d1 = jnp.where(e1 == e, ps + r1, d1)
        d2 = jnp.where(e2 == e, ps + r2, d2)
        be = be + (brow >= pe).astype(jnp.int32)
        pend = jnp.where(lane == e, pe, pend)
        ps = pe
    d1_ref[...] = d1
    d2_ref[...] = d2
    n_used = lax.shift_right_logical(ps, MOE_ROWS_LOG2)
    blk_ref[...] = jnp.where(sub == 0, jnp.minimum(be, MOE_EXPERTS - 1),
                             jnp.where(sub == 1, pend, n_used))


def _finalize(counts, e1, e2, r1, r2, nb):
    nbp = (nb + LANES - 1) // LANES * LANES
    whole = pl.BlockSpec(e1.shape, lambda i, c: (0, 0, 0))
    ishape = jax.ShapeDtypeStruct(e1.shape, jnp.int32)
    return pl.pallas_call(
        _finalize_kernel,
        out_shape=(ishape, ishape, jax.ShapeDtypeStruct((SUBLANES, nbp), jnp.int32)),
        grid_spec=pltpu.PrefetchScalarGridSpec(
            num_scalar_prefetch=1,
            grid=(1,),
            in_specs=[whole, whole, whole, whole],
            out_specs=(whole, whole, pl.BlockSpec((SUBLANES, nbp), lambda i, c: (0, 0)))),
        compiler_params=_cparams("arbitrary"),
        name="moe_finalize",
    )(counts, e1, e2, r1, r2)


def _token_copy(src, r, dst, d, sem):
    return pltpu.make_async_copy(src.at[pl.ds(pl.multiple_of(r * SUBLANES, SUBLANES), SUBLANES), :],
                                 dst.at[pl.ds(pl.multiple_of(d * SUBLANES, SUBLANES), SUBLANES), :],
                                 sem)


def _zero_fill_padding(pend_ref, nu_ref, xs_out, zbuf, zsem):
    blk_rows = MOE_ROWS * SUBLANES
    nb = xs_out.shape[0] // blk_rows
    zbuf[...] = jnp.zeros_like(zbuf)

    def block_copy(b):
        r0 = pl.multiple_of(b * blk_rows, blk_rows)
        return pltpu.make_async_copy(zbuf, xs_out.at[pl.ds(r0, blk_rows), :], zsem)

    def seg_last_block(e):
        pe = pend_ref[e]
        prev = pend_ref[e - 1] if e > 0 else 0
        return pe > prev, lax.shift_right_logical(pe, MOE_ROWS_LOG2) - 1

    for e in range(MOE_EXPERTS):
        nonempty, b = seg_last_block(e)

        @pl.when(nonempty)
        def _():
            block_copy(b).start()

    def tail_start(b, c):
        block_copy(b).start()
        return c

    lax.fori_loop(nu_ref[0], nb, tail_start, 0)
    for e in range(MOE_EXPERTS):
        nonempty, b = seg_last_block(e)

        @pl.when(nonempty)
        def _():
            block_copy(b).wait()

    def tail_wait(b, c):
        block_copy(b).wait()
        return c

    lax.fori_loop(nu_ref[0], nb, tail_wait, 0)


def _dispatch_kernel(dest_ref, pend_ref, nu_ref, h_ref, xs_out, sem, zbuf, zsem, *, tiles):
    i = pl.program_id(0)

    @pl.when(i == 0)
    def _():
        _zero_fill_padding(pend_ref, nu_ref, xs_out, zbuf, zsem)

    rows = tiles * TM
    for q in range(tiles):
        base = i * (2 * rows) + q * TM

        def start(r, c):
            _token_copy(h_ref, q * TM + r, xs_out, dest_ref[base + r], sem).start(priority=0)
            _token_copy(h_ref, q * TM + r, xs_out, dest_ref[base + rows + r],
                        sem).start(priority=1)
            return c

        lax.fori_loop(0, TM, start, 0, unroll=8)
    for _ in range(2):
        pltpu.make_async_copy(h_ref, xs_out.at[pl.ds(0, tiles * TM * SUBLANES), :], sem).wait()


def _dispatch(dest, pad_end, n_used, h, n_rows):
    n = h.shape[0] // SUBLANES
    nt = n // TM
    tiles = _route_tiles(nt)
    return pl.pallas_call(
        functools.partial(_dispatch_kernel, tiles=tiles),
        out_shape=jax.ShapeDtypeStruct((n_rows * SUBLANES, LANES), F32),
        grid_spec=pltpu.PrefetchScalarGridSpec(
            num_scalar_prefetch=3,
            grid=(nt // tiles,),
            in_specs=[pl.BlockSpec((tiles * TM * SUBLANES, LANES), lambda i, *_: (i, 0))],
            out_specs=pl.BlockSpec(memory_space=pl.ANY),
            scratch_shapes=[pltpu.SemaphoreType.DMA, pltpu.VMEM((MOE_ROWS * SUBLANES, LANES), F32),
                            pltpu.SemaphoreType.DMA]),
        compiler_params=_cparams("arbitrary"),
        name="moe_dispatch",
    )(dest, pad_end, n_used, h)


X_SLOTS = 4


Y_SLOTS = 3


def _expert_kernel(be_ref, pend_ref, nu_ref, xs_hbm, wg_hbm, wu_hbm, wd_hbm, ys_hbm,
                   xbuf, ybuf, wg_f, wu_f, wd_f, wg_s, wu_s, wd_s, xsem, ysem, wsem, ord_ref,
                   *, layer):
    nu = nu_ref[0]
    blk_rows = MOE_ROWS * SUBLANES
    nb = ys_hbm.shape[0] // blk_rows

    def rows_of(ref, blk):
        return ref.at[pl.ds(pl.multiple_of(blk * blk_rows, blk_rows), blk_rows), :]

    def x_copy(blk, slot):
        return pltpu.make_async_copy(rows_of(xs_hbm, blk), xbuf.at[slot], xsem.at[slot])

    def y_copy(blk, slot):
        return pltpu.make_async_copy(ybuf.at[slot], rows_of(ys_hbm, blk), ysem.at[slot])

    def w_copies(e, slot):
        return (pltpu.make_async_copy(wg_hbm.at[layer, e], wg_f.at[slot], wsem.at[slot]),
                pltpu.make_async_copy(wu_hbm.at[layer, e], wu_f.at[slot], wsem.at[slot]),
                pltpu.make_async_copy(wd_hbm.at[layer, e], wd_f.at[slot], wsem.at[slot]))

    ord_ref[0] = 0
    for j in range(X_SLOTS - 1):
        @pl.when(j < nu)
        def _():
            x_copy(j, j).start()

    @pl.when(nu > 0)
    def _():
        for c in w_copies(be_ref[0], 0):
            c.start()

    def block(b, carry):
        ahead = b + (X_SLOTS - 1)

        @pl.when(ahead < nu)
        def _():
            x_copy(ahead, lax.rem(ahead, X_SLOTS)).start()

        e = be_ref[b]

        @pl.when((b == 0) | (e != be_ref[jnp.maximum(b - 1, 0)]))
        def _():
            k = ord_ref[0]
            slot = lax.rem(k, 2)
            for c in w_copies(e, slot):
                c.wait()
            wg_s[...] = wg_f[slot].astype(BF16)
            wu_s[...] = wu_f[slot].astype(BF16)
            wd_s[...] = wd_f[slot].astype(BF16)
            nxt = lax.shift_right_logical(pend_ref[e], MOE_ROWS_LOG2)

            @pl.when(nxt < nu)
            def _():
                for c in w_copies(be_ref[nxt], 1 - slot):
                    c.start(priority=1)

            ord_ref[0] = k + 1

        slot = lax.rem(b, X_SLOTS)
        yslot = lax.rem(b, Y_SLOTS)
        x_copy(b, slot).wait()

        @pl.when(b >= Y_SLOTS)
        def _():
            y_copy(b - Y_SLOTS, yslot).wait()

        x = _load_token_tiles(xbuf.at[slot]).astype(BF16)
        a = jnp.dot(x, wg_s[...], preferred_element_type=F32)
        u = jnp.dot(x, wu_s[...], preferred_element_type=F32)
        _store_token_tiles(ybuf.at[yslot], jnp.dot((_silu(a) * u).astype(BF16), wd_s[...],
                                                   preferred_element_type=F32))
        y_copy(b, yslot).start()
        return carry

    lax.fori_loop(0, nu, block, 0)
    for j in range(1, Y_SLOTS + 1):
        @pl.when(nu >= j)
        def _():
            y_copy(nu - j, lax.rem(nu - j, Y_SLOTS)).wait()

    ybuf[0] = jnp.zeros(ybuf.shape[1:], ybuf.dtype)

    def tail_start(b, c):
        y_copy(b, 0).start()
        return c

    def tail_wait(b, c):
        y_copy(b, 0).wait()
        return c

    lax.fori_loop(nu, nb, tail_start, 0)
    lax.fori_loop(nu, nb, tail_wait, 0)


def _experts(blk_expert, pad_end, n_used, xs, w_gate, w_up, w_down, layer):
    d, hid = w_gate.shape[2:]
    blk_rows = MOE_ROWS * SUBLANES
    hbm = pl.BlockSpec(memory_space=pl.ANY)
    return pl.pallas_call(
        functools.partial(_expert_kernel, layer=layer),
        out_shape=jax.ShapeDtypeStruct(xs.shape, F32),
        grid_spec=pltpu.PrefetchScalarGridSpec(
            num_scalar_prefetch=3,
            grid=(1,),
            in_specs=[hbm, hbm, hbm, hbm],
            out_specs=hbm,
            scratch_shapes=[pltpu.VMEM((X_SLOTS, blk_rows, LANES), F32),
                            pltpu.VMEM((Y_SLOTS, blk_rows, LANES), F32),
                            pltpu.VMEM((2, d, hid), F32), pltpu.VMEM((2, d, hid), F32),
                            pltpu.VMEM((2, hid, d), F32),
                            pltpu.VMEM((d, hid), BF16), pltpu.VMEM((d, hid), BF16),
                            pltpu.VMEM((hid, d), BF16),
                            pltpu.SemaphoreType.DMA((X_SLOTS,)), pltpu.SemaphoreType.DMA((Y_SLOTS,)),
                            pltpu.SemaphoreType.DMA((2,)), pltpu.SMEM((1,), jnp.int32)]),
        compiler_params=_cparams("arbitrary"),
        name="moe_experts",
    )(blk_expert, pad_end, n_used, xs, w_gate, w_up, w_down)


def _combined_tile(dest_ref, s_ref, wt_ref, mod_ref, ys_ref, ybuf, sem, *, ctx_tiles, tiles):
    i = pl.program_id(0)
    slot = i % 2
    rows = tiles * TM

    def gather(tile, slot):
        tile = jnp.asarray(tile, jnp.int32)
        base = lax.div(tile, tiles) * (2 * rows) + lax.rem(tile, tiles) * TM

        def start(r, c):
            _token_copy(ys_ref, dest_ref[base + r], ybuf.at[slot, 0], r,
                        sem.at[slot]).start(priority=0)
            _token_copy(ys_ref, dest_ref[base + rows + r], ybuf.at[slot, 1], r,
                        sem.at[slot]).start(priority=1)
            return c

        lax.fori_loop(0, TM, start, 0, unroll=8)

    @pl.when(i == 0)
    def _():
        gather(0, 0)

    @pl.when(i + 1 < pl.num_programs(0))
    def _():
        gather(i + 1, 1 - slot)

    for k in range(2):
        pltpu.make_async_copy(ys_ref.at[pl.ds(0, TM * SUBLANES), :], ybuf.at[slot, k],
                              sem.at[slot]).wait()
    gate = _mod_rows(mod_ref, i, ctx_tiles, 3)[2]
    wt = wt_ref[...]
    y = (wt[:, 0:1] * _load_token_tiles(ybuf.at[slot, 0])
         + wt[:, 1:2] * _load_token_tiles(ybuf.at[slot, 1]))
    return s_ref[...] + gate * y


def _combine_scratch():
    return [pltpu.VMEM((2, 2, TM * SUBLANES, LANES), F32), pltpu.SemaphoreType.DMA((2,))]


def _combine_kernel(dest_ref, s_ref, wt_ref, mod_ref, fg_ref, ys_ref, o_ref, ybuf, sem,
                    *, final_norm, **kw):
    out = _combined_tile(dest_ref, s_ref, wt_ref, mod_ref, ys_ref, ybuf, sem, **kw)
    if final_norm:
        out = _rms(out, fg_ref[...])
    o_ref[...] = out


def _combine(dest, s, wts, mods, final_g, ys, ctx_tiles, final_norm):
    n, d = s.shape
    return pl.pallas_call(
        functools.partial(_combine_kernel, ctx_tiles=ctx_tiles, final_norm=final_norm,
                          tiles=_route_tiles(n // TM)),
        out_shape=jax.ShapeDtypeStruct((n, d), F32),
        grid_spec=pltpu.PrefetchScalarGridSpec(
            num_scalar_prefetch=1,
            grid=(n // TM,),
            in_specs=[pl.BlockSpec((TM, d), lambda i, dst: (i, 0)),
                      pl.BlockSpec((TM, LANES), lambda i, dst: (i, 0)),
                      pl.BlockSpec(mods.shape, lambda i, dst: (0, 0)),
                      pl.BlockSpec((1, d), lambda i, dst: (0, 0)),
                      pl.BlockSpec(memory_space=pl.ANY)],
            out_specs=pl.BlockSpec((TM, d), lambda i, dst: (i, 0)),
            scratch_shapes=_combine_scratch()),
        compiler_params=_cparams("arbitrary"),
        name="moe_combine",
    )(dest, s, wts, mods, final_g.reshape(1, d), ys)


def _moe_layer(s, mods, norm_g, rg_w, rg_b, re_w, re_b, w_gate, w_up, w_down, layer, ctx_tiles,
               final_g, final_norm, defer=False):
    n, d = s.shape
    h, e1, e2, r1, r2, wts, cnt = _router(s, mods, norm_g, rg_w, rg_b, re_w, re_b, ctx_tiles)
    counts = cnt[MOE_GROUPS:MOE_GROUPS + MOE_EXPERTS, 0].astype(jnp.int32)
    nb = (2 * n + MOE_EXPERTS * (MOE_ROWS - 1)) // MOE_ROWS + 1
    d1, d2, blk = _finalize(counts, e1, e2, r1, r2, nb)
    dest = jnp.concatenate([d1, d2], axis=1).reshape(2 * n)
    n_used = blk[2, :1]
    pad_end = blk[1, :MOE_EXPERTS]
    xs = _dispatch(dest, pad_end, n_used, h, nb * MOE_ROWS)
    ys = _experts(blk[0, :nb], pad_end, n_used, xs, w_gate, w_up, w_down, layer)
    if defer:
        return _PendingCombine(dest, s, wts, mods, ys, ctx_tiles)
    return _combine(dest, s, wts, mods, final_g, ys, ctx_tiles, final_norm)


class _PendingCombine(NamedTuple):
    dest: jax.Array
    s: jax.Array
    wts: jax.Array
    mods: jax.Array
    ys: jax.Array
    ctx_tiles: int


def _pending_operands(p):
    n, d = p.s.shape
    kw = dict(ctx_tiles=p.ctx_tiles, tiles=_route_tiles(n // TM))
    specs = [pl.BlockSpec((TM, d), lambda i, dst: (i, 0)),
             pl.BlockSpec((TM, LANES), lambda i, dst: (i, 0)),
             pl.BlockSpec(p.mods.shape, lambda i, dst: (0, 0)),
             pl.BlockSpec(memory_space=pl.ANY)]
    return kw, p.dest, specs, (p.s, p.wts, p.mods, p.ys), _combine_scratch()


def _conv_tile(x, prev_ref, next_ref, has_prev, has_next, w_ref, b_ref):
    rows = x.shape[0]
    S = SUBLANES
    ridx = lax.broadcasted_iota(jnp.int32, (S, x.shape[1]), 0)
    pm = jnp.where(has_prev, 1.0, 0.0)
    nm = jnp.where(has_next, 1.0, 0.0)
    p2 = prev_ref[S - 2:S - 1, :] * pm
    p1 = prev_ref[S - 1:S, :] * pm
    n1 = next_ref[0:1, :] * nm

    def fix_head(rolled, head):
        return jnp.concatenate([head(rolled[:S]), rolled[S:]], axis=0)

    xm1 = fix_head(pltpu.roll(x, 1, axis=0), lambda g: jnp.where(ridx == 0, p1, g))
    xm2 = fix_head(pltpu.roll(x, 2, axis=0),
                   lambda g: jnp.where(ridx == 0, p2, jnp.where(ridx == 1, p1, g)))
    xp1 = pltpu.roll(x, rows - 1, axis=0)
    xp1 = jnp.concatenate([xp1[:rows - S], jnp.where(ridx == S - 1, n1, xp1[rows - S:])], axis=0)
    return (xm2 * w_ref[0:1, :] + xm1 * w_ref[1:2, :] + x * w_ref[2:3, :]
            + xp1 * w_ref[3:4, :] + b_ref[...])


def _ml_proj_kernel(dest_ref, s_ref, wt_ref, pmod_ref, ys_ref, mod_ref, g_ref, w_ref, wg_ref,
                    gb_ref, snew_ref, qk_ref, v_ref, o_ref, gt_ref, ybuf, sem, **pending_kw):
    i = pl.program_id(0)
    x = _combined_tile(dest_ref, s_ref, wt_ref, pmod_ref, ys_ref, ybuf, sem, **pending_kw)
    snew_ref[...] = x
    shift, scale, _ = _mod_rows(mod_ref, i, 1, 0)
    h = _rms(x, g_ref[...]) * (1.0 + scale) + shift
    z = _bdot(h, w_ref[...])
    nqk = qk_ref.shape[1]
    nv = v_ref.shape[1]
    qk_ref[...] = z[:, :nqk]
    v_ref[...] = z[:, nqk:nqk + nv].astype(BF16)
    o_ref[...] = z[:, nqk + nv:].astype(BF16)
    pre = _dot3(h, wg_ref[...], ((1,), (0,))) + gb_ref[...]
    lane = lax.broadcasted_iota(jnp.int32, pre.shape, 1)
    is_forget = ((lane >> 2) & 1) == 1
    gt_ref[...] = jnp.where(is_forget, -_softplus(-pre), pre)


def _ml_proj(pending, mods, norm_g, w_in, gate_b):
    n, d = pending.s.shape
    kw, dest, p_specs, p_args, p_scratch = _pending_operands(pending)
    nqk = 2 * ML_HEADS * ML_DK
    nv = ML_HEADS * ML_DV
    n_main = nqk + 2 * nv
    n_gate = w_in.shape[1] - n_main
    w_main = w_in[:, :n_main].astype(BF16)
    w_gate = jnp.concatenate([w_in[:, n_main:], jnp.zeros((d, LANES - n_gate), F32)], axis=1)
    gb = jnp.concatenate([gate_b.reshape(n_gate), jnp.zeros((LANES - n_gate,), F32)]).reshape(1, LANES)
    tile = lambda w: pl.BlockSpec((TM, w), lambda i, dst: (i, 0))
    whole = lambda shape: pl.BlockSpec(shape, lambda i, dst: (0,) * len(shape))
    return pl.pallas_call(
        functools.partial(_ml_proj_kernel, **kw),
        out_shape=(jax.ShapeDtypeStruct((n, d), F32),
                   jax.ShapeDtypeStruct((n, nqk), F32), jax.ShapeDtypeStruct((n, nv), BF16),
                   jax.ShapeDtypeStruct((n, nv), BF16), jax.ShapeDtypeStruct((n, LANES), F32)),
        grid_spec=pltpu.PrefetchScalarGridSpec(
            num_scalar_prefetch=1,
            grid=(n // TM,),
            in_specs=p_specs + [whole(mods.shape), whole((1, d)), whole(w_main.shape),
                                whole((d, LANES)), whole((1, LANES))],
            out_specs=(tile(d), tile(nqk), tile(nv), tile(nv), tile(LANES)),
            scratch_shapes=p_scratch),
        compiler_params=_cparams("arbitrary"),
        name="mlstm_proj",
    )(dest, *p_args, mods, norm_g.reshape(1, d), w_main, w_gate, gb)


def _ml_chunk_index(j, n_chunks, ctx_chunks, reverse):
    if not reverse:
        return j
    return jnp.where(j < ctx_chunks, ctx_chunks - 1 - j, n_chunks - 1 + ctx_chunks - j)


def _ml_rec_kernel(qk_ref, qkp_ref, qkn_ref, v_ref, gt_ref, gtt_ref, cw_ref, cb_ref, o_ref,
                   c_scr, n_scr, m_scr, *, reverse, n_chunks, ctx_chunks):
    j = pl.program_id(0)
    c = _ml_chunk_index(j, n_chunks, ctx_chunks, reverse)

    @pl.when(j == 0)
    def _():
        c_scr[...] = jnp.zeros_like(c_scr)
        n_scr[...] = jnp.zeros_like(n_scr)
        m_scr[...] = jnp.zeros_like(m_scr)

    has_prev = (c != 0) & (c != ctx_chunks)
    has_next = (c != ctx_chunks - 1) & (c != n_chunks - 1)
    qk = _silu(_conv_tile(qk_ref[...], qkp_ref, qkn_ref, has_prev, has_next, cw_ref, cb_ref))
    L = ML_CHUNK
    ri = lax.broadcasted_iota(jnp.int32, (L, L), 0)
    ci = lax.broadcasted_iota(jnp.int32, (L, L), 1)
    past = (ci >= ri) if reverse else (ci <= ri)
    pastf = past.astype(F32)
    gt = gt_ref[...]
    gtt = gtt_ref[...]
    b_col = jnp.dot(pastf, gt, precision=HI, preferred_element_type=F32)
    b_row = jnp.dot(gtt, pastf.T, precision=HI, preferred_element_type=F32)
    last = 0 if reverse else L - 1
    dbase = 8 if reverse else 0
    nq = ML_HEADS * ML_DK
    for hd in range(ML_HEADS):
        cl = dbase + hd
        cf = dbase + 4 + hd
        q = qk[:, hd * ML_DK:(hd + 1) * ML_DK] * (ML_DK ** -0.5)
        k = qk[:, nq + hd * ML_DK:nq + (hd + 1) * ML_DK]
        v = v_ref[:, hd * ML_DV:(hd + 1) * ML_DV]
        li_c = gt[:, cl:cl + 1]
        li_r = gtt[cl:cl + 1, :]
        b_c = b_col[:, cf:cf + 1]
        b_r = b_row[cf:cf + 1, :]
        g = b_r[:, last:last + 1]
        m0 = m_scr[hd:hd + 1, 0:1]
        c0 = c_scr[hd]
        n0 = n_scr[hd:hd + 1, :]
        a_c = g - b_c + li_c
        a_r = g - b_r + li_r
        m_loc = jnp.max(a_r, axis=-1, keepdims=True)
        inter = b_c + m0
        dlog = jnp.where(past, b_c - b_r + li_r, -jnp.inf)
        m = jnp.maximum(inter, jnp.max(dlog, axis=-1, keepdims=True))
        qb = q.astype(BF16)
        sc = lax.dot_general(qb, k.astype(BF16), (((1,), (1,)), ((), ())),
                             preferred_element_type=F32) * jnp.exp(dlog - m)
        w_inter = jnp.exp(inter - m)
        num = (jnp.dot(sc.astype(BF16), v, preferred_element_type=F32)
               + w_inter * jnp.dot(qb, c0.astype(BF16), preferred_element_type=F32))
        den = (jnp.sum(sc, axis=-1, keepdims=True)
               + w_inter * jnp.sum(q * n0, axis=-1, keepdims=True))
        o_ref[:, hd * ML_DV:(hd + 1) * ML_DV] = (
            num / jnp.maximum(jnp.abs(den), jnp.exp(-m))).astype(BF16)
        m_new = jnp.maximum(g + m0, m_loc)
        dec = jnp.exp(g + m0 - m_new)
        scl = jnp.exp(m_loc - m_new)
        kw = k * jnp.exp(a_c - m_loc)
        c_scr[hd] = dec * c0 + scl * jnp.dot(kw.T.astype(BF16), v, preferred_element_type=F32)
        n_scr[hd:hd + 1, :] = dec * n0 + scl * jnp.sum(kw, axis=0, keepdims=True)
        m_scr[hd:hd + 1, :] = jnp.broadcast_to(m_new, (1, LANES))


def _ml_rec(qk, v, gt, gtt, conv_w, conv_b, reverse):
    n, nqk = qk.shape
    nv = v.shape[1]
    L = ML_CHUNK
    nc = n // L
    cc = TM // L
    hb = L // SUBLANES
    idx = lambda j: _ml_chunk_index(j, nc, cc, reverse)
    last8 = n // SUBLANES - 1
    return pl.pallas_call(
        functools.partial(_ml_rec_kernel, reverse=reverse, n_chunks=nc, ctx_chunks=cc),
        out_shape=jax.ShapeDtypeStruct((n, nv), BF16),
        grid=(nc,),
        in_specs=[pl.BlockSpec((L, nqk), lambda j: (idx(j), 0)),
                  pl.BlockSpec((SUBLANES, nqk), lambda j: (jnp.maximum(idx(j) * hb - 1, 0), 0)),
                  pl.BlockSpec((SUBLANES, nqk), lambda j: (jnp.minimum((idx(j) + 1) * hb, last8), 0)),
                  pl.BlockSpec((L, nv), lambda j: (idx(j), 0)),
                  pl.BlockSpec((L, LANES), lambda j: (idx(j), 0)),
                  pl.BlockSpec((2 * SUBLANES, L), lambda j: (0, idx(j))),
                  _full((4, nqk)), _full((1, nqk))],
        out_specs=pl.BlockSpec((L, nv), lambda j: (idx(j), 0)),
        scratch_shapes=[pltpu.VMEM((ML_HEADS, ML_DK, ML_DV), F32),
                        pltpu.VMEM((SUBLANES, ML_DK), F32),
                        pltpu.VMEM((SUBLANES, LANES), F32)],
        compiler_params=_cparams("arbitrary"),
        name="mlstm_rev" if reverse else "mlstm_fwd",
    )(qk, qk, qk, v, gt, gtt, conv_w, conv_b.reshape(1, nqk))


def _ml_out_kernel(hf_ref, hr_ref, o_ref, s_ref, mod_ref, ng_ref, w_ref, out_ref, p_scr):
    i = pl.program_id(0)
    gate = _mod_rows(mod_ref, i, 1, 0)[2]
    hs = hf_ref[...].astype(F32) + hr_ref[...].astype(F32)
    sig = _sigmoid(o_ref[...].astype(F32))
    ng = ng_ref[...]
    for hd in range(ML_HEADS):
        cs = slice(hd * ML_DV, (hd + 1) * ML_DV)
        seg = hs[:, cs]
        hn = seg * lax.rsqrt(jnp.mean(seg * seg, axis=-1, keepdims=True) + EPS) * ng[:, cs]
        p_scr[:, cs] = (hn * sig[:, cs]).astype(BF16)
    y = jnp.dot(p_scr[...], w_ref[...], preferred_element_type=F32)
    out_ref[...] = s_ref[...] + gate * y


def _ml_out(hf, hr, o, s, mods, norm_g, w_out):
    n, d = s.shape
    nv = hf.shape[1]
    tile = lambda w: pl.BlockSpec((TM, w), lambda i: (i, 0))
    return pl.pallas_call(
        _ml_out_kernel,
        out_shape=jax.ShapeDtypeStruct((n, d), F32),
        grid=(n // TM,),
        in_specs=[tile(nv), tile(nv), tile(nv), tile(d), _full(mods.shape), _full((1, nv)),
                  _full(w_out.shape)],
        out_specs=tile(d),
        scratch_shapes=[pltpu.VMEM((TM, nv), BF16)],
        compiler_params=_cparams("arbitrary"),
        name="mlstm_out",
    )(hf, hr, o, s, mods, norm_g.reshape(1, nv), w_out.astype(BF16))


def _mlstm_layer(pending, mods, norm_g, w_in, conv_w, conv_b, gate_b, ml_norm_g, w_out):
    s, qk, v, o, gt = _ml_proj(pending, mods, norm_g, w_in, gate_b)
    gtt = gt[:, :2 * SUBLANES].T
    hf = _ml_rec(qk, v, gt, gtt, conv_w, conv_b, False)
    hr = _ml_rec(qk, v, gt, gtt, conv_w, conv_b, True)
    return _ml_out(hf, hr, o, s, mods, ml_norm_g, w_out)


def _lru_proj_kernel(dest_ref, s_ref, wt_ref, pmod_ref, ys_ref, mod_ref, g_ref, w_ref,
                     snew_ref, gl_ref, xr_ref, ybuf, sem, **pending_kw):
    i = pl.program_id(0)
    x = _combined_tile(dest_ref, s_ref, wt_ref, pmod_ref, ys_ref, ybuf, sem, **pending_kw)
    snew_ref[...] = x
    shift, scale, _ = _mod_rows(mod_ref, i, 1, 0)
    h = _rms(x, g_ref[...]) * (1.0 + scale) + shift
    z = _bdot(h, w_ref[...])
    w = gl_ref.shape[1]
    gl_ref[...] = _gelu(z[:, :w]).astype(BF16)
    xr_ref[...] = z[:, w:]


def _lru_proj(pending, mods, norm_g, w_in):
    n, d = pending.s.shape
    kw, dest, p_specs, p_args, p_scratch = _pending_operands(pending)
    w = w_in.shape[1] // 2
    tile = lambda c: pl.BlockSpec((TM, c), lambda i, dst: (i, 0))
    whole = lambda shape: pl.BlockSpec(shape, lambda i, dst: (0,) * len(shape))
    return pl.pallas_call(
        functools.partial(_lru_proj_kernel, **kw),
        out_shape=(jax.ShapeDtypeStruct((n, d), F32), jax.ShapeDtypeStruct((n, w), BF16),
                   jax.ShapeDtypeStruct((n, w), F32)),
        grid_spec=pltpu.PrefetchScalarGridSpec(
            num_scalar_prefetch=1,
            grid=(n // TM,),
            in_specs=p_specs + [whole(mods.shape), whole((1, d)), whole(w_in.shape)],
            out_specs=(tile(d), tile(w), tile(w)),
            scratch_shapes=p_scratch),
        compiler_params=_cparams("arbitrary"),
        name="rglru_proj",
    )(dest, *p_args, mods, norm_g.reshape(1, d), w_in.astype(BF16))


def _lru_tile_index(j, n_tiles, reverse):
    if not reverse:
        return j
    return jnp.where(j == 0, 0, n_tiles - j)


def _lru_scan_kernel(x_ref, xp_ref, xn_ref, cw_ref, cb_ref, wg_ref, ba_ref, bx_ref, lam_ref,
                     o_ref, a_scr, u_scr, carry, *, reverse, n_tiles):
    j = pl.program_id(0)
    t = _lru_tile_index(j, n_tiles, reverse)

    @pl.when(j == 0)
    def _():
        carry[...] = jnp.zeros_like(carry)

    has_prev = t > 1
    has_next = (t != 0) & (t != n_tiles - 1)
    xr = _conv_tile(x_ref[...], xp_ref, xn_ref, has_prev, has_next, cw_ref, cb_ref)
    sp = _softplus(-lam_ref[...])
    B = LRU_BLOCK
    for hd in range(LRU_HEADS):
        cs = slice(hd * B, (hd + 1) * B)
        xh = xr[:, cs]
        y = jnp.dot(xh.astype(BF16), wg_ref[hd], preferred_element_type=F32)
        r = _sigmoid(y[:, :B] + ba_ref[:, cs])
        ig = _sigmoid(y[:, B:] + bx_ref[:, cs])
        log_a = -LRU_C * r * sp[:, cs]
        a = jnp.exp(log_a)
        a_scr[:, cs] = a
        v = 1.0 - a * a
        u_scr[:, cs] = jnp.where(v > 0.0, v * lax.rsqrt(v), 0.0) * (ig * xh)

    S = SUBLANES
    w = a_scr.shape[1]
    sidx = lax.broadcasted_iota(jnp.int32, (S, w), 0)

    def group(gi, c):
        g = (TM // S - 1 - gi) if reverse else gi
        r0 = pl.multiple_of(g * S, S)
        a = a_scr[pl.ds(r0, S), :]
        u = u_scr[pl.ds(r0, S), :]
        for sft in (1, 2, 4):
            if reverse:
                ok = sidx < S - sft
                a_e = pltpu.roll(a, S - sft, axis=0)
                u_e = pltpu.roll(u, S - sft, axis=0)
            else:
                ok = sidx >= sft
                a_e = pltpu.roll(a, sft, axis=0)
                u_e = pltpu.roll(u, sft, axis=0)
            u = jnp.where(ok, a * u_e + u, u)
            a = jnp.where(ok, a * a_e, a)
        hcur = a * carry[...] + u
        u_scr[pl.ds(r0, S), :] = hcur
        edge = 0 if reverse else S - 1
        carry[...] = jnp.broadcast_to(hcur[edge:edge + 1, :], (S, w))
        return c

    lax.fori_loop(0, TM // S, group, 0, unroll=2)
    o_ref[...] = u_scr[...].astype(BF16)


def _lru_scan(xraw, conv_w, conv_b, w_a, b_a, w_x, b_x, lam, reverse):
    n, w = xraw.shape
    nt = n // TM
    hb = TM // SUBLANES
    idx = lambda j: _lru_tile_index(j, nt, reverse)
    last8 = n // SUBLANES - 1
    wg = jnp.concatenate([w_a, w_x], axis=-1).astype(BF16)
    return pl.pallas_call(
        functools.partial(_lru_scan_kernel, reverse=reverse, n_tiles=nt),
        out_shape=jax.ShapeDtypeStruct((n, w), BF16),
        grid=(nt,),
        in_specs=[pl.BlockSpec((TM, w), lambda j: (idx(j), 0)),
                  pl.BlockSpec((SUBLANES, w), lambda j: (jnp.maximum(idx(j) * hb - 1, 0), 0)),
                  pl.BlockSpec((SUBLANES, w), lambda j: (jnp.minimum((idx(j) + 1) * hb, last8), 0)),
                  _full((4, w)), _full((1, w)), _full(wg.shape), _full((1, w)), _full((1, w)),
                  _full((1, w))],
        out_specs=pl.BlockSpec((TM, w), lambda j: (idx(j), 0)),
        scratch_shapes=[pltpu.VMEM((TM, w), F32), pltpu.VMEM((TM, w), F32),
                        pltpu.VMEM((SUBLANES, w), F32)],
        compiler_params=_cparams("arbitrary"),
        name="rglru_rev" if reverse else "rglru_fwd",
    )(xraw, xraw, xraw, conv_w, conv_b.reshape(1, w), wg, b_a.reshape(1, w), b_x.reshape(1, w),
      lam.reshape(1, w))


def _lru_out_kernel(gl_ref, hf_ref, hr_ref, s_ref, mod_ref, w_ref, out_ref):
    gate = _mod_rows(mod_ref, 1, 0, 0)[2]
    p = gl_ref[...].astype(F32) * (hf_ref[...].astype(F32) + hr_ref[...].astype(F32))
    out_ref[...] = s_ref[...] + gate * _bdot(p, w_ref[...])


def _lru_out(gl, hf, hr, s, mods, w_out):
    n, d = s.shape
    w = gl.shape[1]
    lat = lambda c: pl.BlockSpec((TM, c), lambda i: (i + 1, 0))
    return pl.pallas_call(
        _lru_out_kernel,
        out_shape=jax.ShapeDtypeStruct((n - TM, d), F32),
        grid=(n // TM - 1,),
        in_specs=[lat(w), lat(w), lat(w), lat(d), _full(mods.shape), _full(w_out.shape)],
        out_specs=pl.BlockSpec((TM, d), lambda i: (i, 0)),
        compiler_params=_cparams("arbitrary"),
        name="rglru_out",
    )(gl, hf, hr, s, mods, w_out.astype(BF16))


def _rglru_layer(pending, mods, norm_g, w_in, conv_w, conv_b, w_a, b_a, w_x, b_x, lam, w_out):
    s, gl, xraw = _lru_proj(pending, mods, norm_g, w_in)
    hf = _lru_scan(xraw, conv_w, conv_b, w_a[0], b_a[0], w_x[0], b_x[0], lam[0], False)
    hr = _lru_scan(xraw, conv_w, conv_b, w_a[1], b_a[1], w_x[1], b_x[1], lam[1], True)
    return _lru_out(gl, hf, hr, s, mods, w_out)


def _fn_proj_kernel(s_ref, mod_ref, g_ref, wt_ref, cs_ref, yr_ref, yi_ref, ar_scr, ai_scr):
    shift, scale, _ = _mod_rows(mod_ref, 1, 0, 0)
    nm = wt_ref.shape[0]
    gw = nm // FN_GROUPS
    per = TM // FFT_N2
    nj = FN_TB // FFT_N2
    csb = cs_ref[...].astype(BF16)

    def sub(tc, c):
        r0 = pl.multiple_of(tc * TM, TM)
        h = _rms(s_ref[pl.ds(r0, TM), :], g_ref[...]) * (1.0 + scale) + shift
        zt = lax.dot_general(wt_ref[...], h.astype(BF16), (((1,), (1,)), ((), ())),
                             preferred_element_type=F32).astype(BF16)
        for g in range(FN_GROUPS):
            y = jnp.dot(csb, zt[g * gw:(g + 1) * gw, :], preferred_element_type=F32)
            for q in range(per):
                row0 = pl.multiple_of((tc * per + q) * _slab_pitch(nm) + g * gw, SUBLANES)
                ar_scr[pl.ds(row0, gw), :] = y[:gw, q * FFT_N2:(q + 1) * FFT_N2]
                ai_scr[pl.ds(row0, gw), :] = y[gw:, q * FFT_N2:(q + 1) * FFT_N2]
        return c

    lax.fori_loop(0, FN_TB // TM, sub, 0)

    def relayout(m, c):
        yr_ref[m] = ar_scr[pl.ds(m, nj, stride=_slab_pitch(nm)), :]
        yi_ref[m] = ai_scr[pl.ds(m, nj, stride=_slab_pitch(nm)), :]
        return c

    lax.fori_loop(0, nm, relayout, 0, unroll=8)


def _slab_pitch(rows):
    return rows + SUBLANES


def _dft_cos_sin(n, scale):
    k = np.arange(n, dtype=np.int64)
    ang = 2.0 * np.pi * ((k[:, None] * k[None, :]) % n).astype(np.float64) / n
    return np.cos(ang) * scale, np.sin(ang) * scale


def _fn_proj(s, mods, norm_g, w_in):
    t, d = s.shape
    nm = w_in.shape[1]
    gw = nm // FN_GROUPS
    n1 = t // FFT_N2
    nj = FN_TB // FFT_N2
    c, sn = _dft_cos_sin(gw, gw ** -0.5)
    cs = jnp.asarray(np.concatenate([c, -sn], axis=0), F32)
    yspec = pl.BlockSpec((nm, nj, FFT_N2), lambda i: (0, i, 0))
    yshape = jax.ShapeDtypeStruct((nm, n1, FFT_N2), F32)
    return pl.pallas_call(
        _fn_proj_kernel,
        out_shape=(yshape, yshape),
        grid=(t // FN_TB,),
        in_specs=[pl.BlockSpec((FN_TB, d), lambda i: (i, 0)), _full(mods.shape), _full((1, d)),
                  _full((nm, d)), _full(cs.shape)],
        out_specs=(yspec, yspec),
        scratch_shapes=[pltpu.VMEM((nj * _slab_pitch(nm), FFT_N2), F32),
                        pltpu.VMEM((nj * _slab_pitch(nm), FFT_N2), F32)],
        compiler_params=_cparams("arbitrary"),
        name="fourier_proj",
    )(s, mods, norm_g.reshape(1, d), w_in.T.astype(BF16), cs)


def _fn_fft_kernel(yr_ref, yi_ref, m_ref, tc_ref, ts_ref, d_ref, o_ref):
    n1 = yr_ref.shape[1]
    n2 = FFT_N2
    xr = jnp.concatenate([yr_ref[m].astype(BF16) for m in range(FN_CB)], axis=1)
    xi = jnp.concatenate([yi_ref[m].astype(BF16) for m in range(FN_CB)], axis=1)
    a = jnp.dot(m_ref[...].astype(BF16), jnp.concatenate([xr, xi], axis=0),
                preferred_element_type=F32)
    ar = a[:n1]
    ai = a[n1:]
    tc = jnp.concatenate([tc_ref[...]] * FN_CB, axis=1)
    ts = jnp.concatenate([ts_ref[...]] * FN_CB, axis=1)
    br = ar * tc + ai * ts
    bi = ai * tc - ar * ts
    bst = jnp.concatenate(
        [jnp.concatenate([br[:, m * n2:(m + 1) * n2], bi[:, m * n2:(m + 1) * n2]], axis=1)
         for m in range(FN_CB)], axis=0).astype(BF16)
    res = lax.dot_general(d_ref[...].astype(BF16), bst, (((1,), (1,)), ((), ())),
                          preferred_element_type=F32)
    for m in range(FN_CB):
        o_ref[m] = res[:, m * n1:(m + 1) * n1]


def _fn_fft(yr, yi):
    nm, n1, n2 = yr.shape
    t = n1 * n2
    c, sn = _dft_cos_sin(n1, n1 ** -0.5)
    m = jnp.asarray(np.block([[c, sn], [-sn, c]]), F32)
    k1 = np.arange(n1, dtype=np.int64)[:, None]
    t2 = np.arange(n2, dtype=np.int64)[None, :]
    ang = 2.0 * np.pi * ((k1 * t2) % t).astype(np.float64) / t
    tc = jnp.asarray(np.cos(ang), F32)
    ts = jnp.asarray(np.sin(ang), F32)
    c2, s2 = _dft_cos_sin(n2, n2 ** -0.5)
    dm = jnp.asarray(np.concatenate([c2, s2], axis=1), F32)
    yspec = pl.BlockSpec((FN_CB, n1, n2), lambda i: (i, 0, 0))
    return pl.pallas_call(
        _fn_fft_kernel,
        out_shape=jax.ShapeDtypeStruct((nm, n2, n1), F32),
        grid=(nm // FN_CB,),
        in_specs=[yspec, yspec, _full(m.shape), _full(tc.shape), _full(ts.shape), _full(dm.shape)],
        out_specs=pl.BlockSpec((FN_CB, n2, n1), lambda i: (i, 0, 0)),
        compiler_params=_cparams("arbitrary"),
        name="fourier_fft",
    )(yr, yi, m, tc, ts, dm)


def _fn_out_kernel(ft_ref, w_ref, s_ref, mod_ref, o_ref, a_scr):
    gate = _mod_rows(mod_ref, 1, 0, 0)[2]
    nm, nj, n1 = ft_ref.shape

    def relayout(m, c):
        a_scr[pl.ds(m, nj, stride=_slab_pitch(nm)), :] = ft_ref[m]
        return c

    lax.fori_loop(0, nm, relayout, 0, unroll=8)
    for j in range(nj):
        p0 = j * _slab_pitch(nm)
        slab = a_scr[p0:p0 + nm, :].astype(BF16)
        y = lax.dot_general(slab, w_ref[...], (((0,), (0,)), ((), ())), preferred_element_type=F32)
        rows = slice(j * n1, (j + 1) * n1)
        o_ref[rows, :] = s_ref[rows, :] + gate * y


def _fn_out(ft, s, mods, w_out):
    t, d = s.shape
    nm, n2, n1 = ft.shape
    nj = FN_TB // n1
    tok = pl.BlockSpec((FN_TB, d), lambda i: (i, 0))
    return pl.pallas_call(
        _fn_out_kernel,
        out_shape=jax.ShapeDtypeStruct((t, d), F32),
        grid=(t // FN_TB,),
        in_specs=[pl.BlockSpec((nm, nj, n1), lambda i: (0, i, 0)), _full(w_out.shape), tok,
                  _full(mods.shape)],
        out_specs=tok,
        scratch_shapes=[pltpu.VMEM((nj * _slab_pitch(nm), n1), F32)],
        compiler_params=_cparams("arbitrary"),
        name="fourier_out",
    )(ft, w_out.astype(BF16), s, mods)


def _fourier_layer(s, mods, norm_g, w_in, w_out):
    yr, yi = _fn_proj(s, mods, norm_g, w_in)
    return _fn_out(_fn_fft(yr, yi), s, mods, w_out)


def kernel(x, c, ctx, c_ctx, ada_w, ada_b, norm_mix_g, norm_ffn_g, final_norm_g, router_group_w, router_group_b, router_expert_w, router_expert_b, expert_w_gate, expert_w_up, expert_w_down, cm_w_in, cm_v_norm_g, cm_w_s, cm_b_s, cm_w_out, ml_w_in, ml_conv_w, ml_conv_b, ml_gate_b, ml_norm_g, ml_w_out, lru_w_in, lru_conv_w, lru_conv_b, lru_w_a, lru_b_a, lru_w_x, lru_b_x, lru_lambda, lru_w_out, fn_w_in, fn_w_out):
    bsz, seq, d = x.shape
    assert bsz == 1 and ada_w.shape[0] == 4 and ctx.shape[1] == TM
    c_rows = jnp.concatenate([c_ctx[None, :], c, jnp.zeros((SUBLANES - 2, d), F32)], axis=0)
    mods = _ada_table(c_rows, ada_w, ada_b)

    def moe(s, i, ctx_tiles, final_norm=False, defer=False):
        return _moe_layer(s, mods[i], norm_ffn_g[i], router_group_w[i], router_group_b[i],
                          router_expert_w[i], router_expert_b[i], expert_w_gate, expert_w_up,
                          expert_w_down, i, ctx_tiles, final_norm_g, final_norm, defer)

    s = _chunk_mlp_layer(x[0], ctx[0], mods[0], norm_mix_g[0], cm_w_in[0], cm_v_norm_g[0],
                         cm_w_s[0], cm_b_s[0], cm_w_out[0])
    s = moe(s, 0, 1, defer=True)
    s = _mlstm_layer(s, mods[1], norm_mix_g[1], ml_w_in[0], ml_conv_w[0], ml_conv_b[0],
                     ml_gate_b[0], ml_norm_g[0], ml_w_out[0])
    s = moe(s, 1, 1, defer=True)
    s = _rglru_layer(s, mods[2], norm_mix_g[2], lru_w_in[0], lru_conv_w[0], lru_conv_b[0],
                     lru_w_a[0], lru_b_a[0], lru_w_x[0], lru_b_x[0], lru_lambda[0], lru_w_out[0])
    s = moe(s, 2, 0)
    s = _fourier_layer(s, mods[3], norm_mix_g[3], fn_w_in[0], fn_w_out[0])
    s = moe(s, 3, 0, final_norm=True)
    return s[None]
```

```python
import functools
import math
from typing import NamedTuple

import jax
import jax.numpy as jnp
import numpy as np
from jax import lax
from jax.experimental import pallas as pl
from jax.experimental.pallas import tpu as pltpu

F32 = jnp.float32
BF16 = jnp.bfloat16

EPS = 1e-6
POS_BASE = 10000.0
GRID_W = 64
N_MOD = 6
TM = 256
LANES = 128
SUBLANES = 8
VMEM_LIMIT = 56 * 1024 * 1024

CM_CHUNK = 128
CM_GROUPS = 4
ML_HEADS = 4
ML_DK = 128
ML_DV = 256
ML_CHUNK = 128
LRU_HEADS = 10
LRU_BLOCK = 128
LRU_C = 8.0
FN_GROUPS = 4
FFT_N2 = 128
MOE_GROUPS = 4
MOE_EPG = 8
MOE_EXPERTS = MOE_GROUPS * MOE_EPG
MOE_ROWS_LOG2 = 8
MOE_ROWS = 1 << MOE_ROWS_LOG2
ROUTE_ROWS = 40
FN_TB = 1024
FN_CB = 32
CONV_LEFT = 2

HI = lax.Precision.HIGHEST


def _cparams(*sem):
    return pltpu.CompilerParams(dimension_semantics=sem, vmem_limit_bytes=VMEM_LIMIT)


def _full(shape):
    nd = len(shape)
    return pl.BlockSpec(shape, lambda *_: (0,) * nd)


def _rms(x, g):
    return x * lax.rsqrt(jnp.mean(x * x, axis=-1, keepdims=True) + EPS) * g


def _gelu(x):
    c = math.sqrt(2.0 / math.pi)
    return 0.5 * x * (1.0 + jnp.tanh(c * (x + 0.044715 * (x * x * x))))


def _sigmoid(x):
    return 0.5 * jnp.tanh(0.5 * x) + 0.5


def _silu(x):
    return x * _sigmoid(x)


def _softplus(x):
    return jnp.maximum(x, 0.0) + jnp.log(1.0 + jnp.exp(-jnp.abs(x)))


def _mod_rows(mod_ref, tile, ctx_tiles, first):
    row = jnp.where(tile < ctx_tiles, 0, 1)
    m = mod_ref[pl.ds(row, 1), :]
    d = m.shape[1] // N_MOD
    return tuple(m[:, (first + j) * d:(first + j + 1) * d] for j in range(3))


def _bdot(a, b):
    return jnp.dot(a.astype(BF16), b.astype(BF16), preferred_element_type=F32)


def _split_bf16(x):
    hi = x.astype(BF16)
    return hi, (x - hi.astype(F32)).astype(BF16)


def _dot3(a, b, dims):
    a_hi, a_lo = _split_bf16(a)
    b_hi, b_lo = _split_bf16(b)
    dg = functools.partial(lax.dot_general, dimension_numbers=(dims, ((), ())),
                           preferred_element_type=F32)
    return dg(a_hi, b_hi) + dg(a_hi, b_lo) + dg(a_lo, b_hi)


def _ada_kernel(c_ref, w_ref, b_ref, o_ref):
    c = c_ref[...]
    o_ref[...] = _dot3(_silu(c), w_ref[...], ((1,), (0,))) + b_ref[...]


def _ada_table(c_rows, ada_w, ada_b):
    depth, d, n = ada_w.shape
    tn = 3072
    return pl.pallas_call(
        _ada_kernel,
        out_shape=jax.ShapeDtypeStruct((depth, SUBLANES, n), F32),
        grid=(depth, n // tn),
        in_specs=[_full((SUBLANES, d)),
                  pl.BlockSpec((None, d, tn), lambda i, j: (i, 0, j)),
                  pl.BlockSpec((None, 1, tn), lambda i, j: (i, 0, j))],
        out_specs=pl.BlockSpec((None, SUBLANES, tn), lambda i, j: (i, 0, j)),
        compiler_params=_cparams("arbitrary", "arbitrary"),
        name="ada_table",
    )(c_rows, ada_w, ada_b.reshape(depth, 1, n))


def _pos_tables(seq, d):
    q = d // 4
    freq = jnp.exp(-math.log(POS_BASE) * jnp.arange(q, dtype=F32) / q)
    ar = jnp.arange(seq // GRID_W, dtype=F32)[:, None] * freq
    ac = jnp.arange(GRID_W, dtype=F32)[:, None] * freq
    return (jnp.concatenate([jnp.sin(ar), jnp.cos(ar)], axis=-1),
            jnp.concatenate([jnp.sin(ac), jnp.cos(ac)], axis=-1))


def _cm_kernel(x_ref, ctx_ref, rt_ref, ct_ref, mod_ref, g_ref, win_ref, vg_ref, ws_ref, bs_ref,
               wout_ref, o_ref, p_scr, x_scr):
    i = pl.program_id(0)

    @pl.when(i == 0)
    def _():
        x_scr[...] = ctx_ref[...]

    @pl.when(i > 0)
    def _():
        rows_per_tile = TM // GRID_W
        q2 = rt_ref.shape[1]
        r0 = (i - 1) * rows_per_tile
        rt = jnp.concatenate(
            [jnp.broadcast_to(rt_ref[pl.ds(r0 + j, 1), :], (GRID_W, q2))
             for j in range(rows_per_tile)], axis=0)
        ct = jnp.concatenate([ct_ref[...]] * rows_per_tile, axis=0)
        x_scr[...] = x_ref[...] + jnp.concatenate([rt, ct], axis=1)

    shift, scale, gate = _mod_rows(mod_ref, i, 1, 0)
    x = x_scr[...]
    h = _rms(x, g_ref[...]) * (1.0 + scale) + shift
    z = _gelu(_bdot(h, win_ref[...]))
    w = z.shape[1] // 2
    u = z[:, :w]
    v = _rms(z[:, w:], vg_ref[...]).astype(BF16)
    gw = w // CM_GROUPS
    for c in range(TM // CM_CHUNK):
        r = slice(c * CM_CHUNK, (c + 1) * CM_CHUNK)
        for g in range(CM_GROUPS):
            cs = slice(g * gw, (g + 1) * gw)
            s = jnp.dot(ws_ref[g], v[r, cs], preferred_element_type=F32) + bs_ref[:, g:g + 1]
            p_scr[r, cs] = (u[r, cs] * s).astype(BF16)
    y = jnp.dot(p_scr[...], wout_ref[...], preferred_element_type=F32)
    o_ref[...] = x + gate * y


def _chunk_mlp_layer(x2, ctx2, mods, norm_g, w_in, v_g, w_s, b_s, w_out):
    seq, d = x2.shape
    n_ctx = ctx2.shape[0]
    assert n_ctx == TM and seq % TM == 0 and TM % GRID_W == 0
    n = n_ctx + seq
    w = w_out.shape[0]
    rt, ct = _pos_tables(seq, d)
    return pl.pallas_call(
        _cm_kernel,
        out_shape=jax.ShapeDtypeStruct((n, d), F32),
        grid=(n // TM,),
        in_specs=[pl.BlockSpec((TM, d), lambda i: (jnp.maximum(i - 1, 0), 0)),
                  _full((TM, d)), _full(rt.shape), _full(ct.shape),
                  _full(mods.shape), _full((1, d)), _full(w_in.shape), _full((1, w)),
                  _full(w_s.shape), _full((CM_CHUNK, CM_GROUPS)), _full(w_out.shape)],
        out_specs=pl.BlockSpec((TM, d), lambda i: (i, 0)),
        scratch_shapes=[pltpu.VMEM((TM, w), BF16), pltpu.VMEM((TM, d), F32)],
        compiler_params=_cparams("arbitrary"),
        name="chunk_mlp",
    )(x2, ctx2, rt, ct, mods, norm_g.reshape(1, d), w_in.astype(BF16), v_g.reshape(1, w),
      w_s.astype(BF16), b_s.T, w_out.astype(BF16))


def _store_token_tiles(ref, x):
    rows, d = x.shape
    for j in range(d // LANES):
        ref[pl.ds(j, rows, stride=d // LANES), :] = x[:, j * LANES:(j + 1) * LANES]


def _load_token_tiles(ref):
    chunks = SUBLANES
    rows = ref.shape[0] // chunks
    return jnp.concatenate([ref[pl.ds(j, rows, stride=chunks), :] for j in range(chunks)], axis=1)


def _route_tiles(nt):
    return next(k for k in (5, 4, 2, 1) if nt % k == 0)


def _router_kernel(s_ref, mod_ref, g_ref, rwt_ref, rbt_ref, tri_ref, h_ref, e1_ref, e2_ref, r1_ref,
                   r2_ref, wt_ref, cnt_ref, carry, *, ctx_rows):
    i = pl.program_id(0)
    rows, d = s_ref.shape

    @pl.when(i == 0)
    def _():
        carry[...] = jnp.zeros_like(carry)

    lat = mod_ref[1:2, :]
    shift, scale = lat[:, 3 * d:4 * d], lat[:, 4 * d:5 * d]
    if ctx_rows:
        ctx = mod_ref[0:1, :]
        is_ctx = (i == 0) & (lax.broadcasted_iota(jnp.int32, (rows, 1), 0) < ctx_rows)
        shift = jnp.where(is_ctx, ctx[:, 3 * d:4 * d], shift)
        scale = jnp.where(is_ctx, ctx[:, 4 * d:5 * d], scale)
    h = _rms(s_ref[...], g_ref[...]) * (1.0 + scale) + shift
    _store_token_tiles(h_ref, h)
    logits = _dot3(rwt_ref[...], h, ((1,), (1,))) + rbt_ref[...]
    row = lax.broadcasted_iota(jnp.int32, logits.shape, 0)
    neg = jnp.float32(-jnp.inf)
    big = jnp.int32(1 << 20)
    is_g = row < MOE_GROUPS
    gl = jnp.where(is_g, logits, neg)
    gmax = jnp.max(gl, axis=0, keepdims=True)
    grp = jnp.min(jnp.where(is_g & (gl == gmax), row, big), axis=0, keepdims=True)
    p_grp = 1.0 / jnp.sum(jnp.exp(gl - gmax), axis=0, keepdims=True)
    e_row = row - MOE_GROUPS
    in_grp = (e_row >= 0) & (e_row < MOE_EXPERTS) & ((e_row >> 3) == grp)
    l1 = jnp.where(in_grp, logits, neg)
    v1 = jnp.max(l1, axis=0, keepdims=True)
    i1 = jnp.min(jnp.where(in_grp & (l1 == v1), row, big), axis=0, keepdims=True)
    rest = in_grp & (row != i1)
    l2 = jnp.where(rest, logits, neg)
    v2 = jnp.max(l2, axis=0, keepdims=True)
    i2 = jnp.min(jnp.where(rest & (l2 == v2), row, big), axis=0, keepdims=True)
    e21 = jnp.exp(v2 - v1)
    w1 = p_grp / (1.0 + e21)
    w2 = p_grp * e21 / (1.0 + e21)
    oh1 = (row == i1).astype(F32)
    oh2 = (row == i2).astype(F32)
    oh = oh1 + oh2
    before = jnp.dot(oh.astype(BF16), tri_ref[...], preferred_element_type=F32) + carry[:, 0:1]
    r1_ref[...] = jnp.sum(oh1 * before, axis=0, keepdims=True).astype(jnp.int32)
    r2_ref[...] = jnp.sum(oh2 * before, axis=0, keepdims=True).astype(jnp.int32)
    e1_ref[...] = i1 - MOE_GROUPS
    e2_ref[...] = i2 - MOE_GROUPS
    carry[...] = carry[...] + jnp.sum(oh, axis=1, keepdims=True)
    cnt_ref[...] = carry[...]
    wt_ref[...] = jnp.concatenate([w1, w2, jnp.zeros((LANES - 2, rows), F32)], axis=0).T


def _router(s, mods, norm_g, rg_w, rg_b, re_w, re_b, ctx_tiles):
    n, d = s.shape
    rows = TM * _route_tiles(n // TM)
    steps = n // rows
    pad = ROUTE_ROWS - MOE_GROUPS - MOE_EXPERTS
    rwt = jnp.concatenate([rg_w, re_w, jnp.zeros((d, pad), F32)], axis=1).T
    rbt = jnp.broadcast_to(jnp.concatenate([rg_b, re_b, jnp.zeros((pad,), F32)])[:, None],
                           (ROUTE_ROWS, rows))
    tri = jnp.asarray(np.triu(np.ones((rows, rows), np.float32), 1), BF16)
    assert d == SUBLANES * LANES
    tile = pl.BlockSpec((rows, d), lambda i: (i, 0))
    irow = pl.BlockSpec((None, 1, rows), lambda i: (i, 0, 0))
    ishape = jax.ShapeDtypeStruct((steps, 1, rows), jnp.int32)
    return pl.pallas_call(
        functools.partial(_router_kernel, ctx_rows=ctx_tiles * TM),
        out_shape=(jax.ShapeDtypeStruct((n * SUBLANES, LANES), F32), ishape, ishape, ishape, ishape,
                   jax.ShapeDtypeStruct((n, LANES), F32),
                   jax.ShapeDtypeStruct((ROUTE_ROWS, LANES), F32)),
        grid=(steps,),
        in_specs=[tile, _full(mods.shape), _full((1, d)), _full((ROUTE_ROWS, d)),
                  _full((ROUTE_ROWS, rows)), _full((rows, rows))],
        out_specs=(pl.BlockSpec((rows * SUBLANES, LANES), lambda i: (i, 0)), irow, irow, irow, irow,
                   pl.BlockSpec((rows, LANES), lambda i: (i, 0)), _full((ROUTE_ROWS, LANES))),
        scratch_shapes=[pltpu.VMEM((ROUTE_ROWS, LANES), F32)],
        compiler_params=_cparams("arbitrary"),
        name="moe_router",
    )(s, mods, norm_g.reshape(1, d), rwt, rbt, tri)


def _finalize_kernel(cnt_ref, e1_ref, e2_ref, r1_ref, r2_ref, d1_ref, d2_ref, blk_ref):
    e1 = e1_ref[...]
    e2 = e2_ref[...]
    r1 = r1_ref[...]
    r2 = r2_ref[...]
    d1 = jnp.zeros_like(e1)
    d2 = jnp.zeros_like(e2)
    lane = lax.broadcasted_iota(jnp.int32, blk_ref.shape, 1)
    brow = lane * MOE_ROWS
    sub = lax.broadcasted_iota(jnp.int32, blk_ref.shape, 0)
    be = jnp.zeros(blk_ref.shape, jnp.int32)
    pend = jnp.zeros(blk_ref.shape, jnp.int32)
    ps = jnp.int32(0)
    for e in range(MOE_EXPERTS):
        c = cnt_ref[e]
        pe = ps + lax.shift_left(lax.shift_right_logical(c + (MOE_ROWS - 1), MOE_ROWS_LOG2),
                                 MOE_ROWS_LOG2)
        d1 = jnp.where(e1 == e, ps + r1, d1)
        d2 = jnp.where(e2 == e, ps + r2, d2)
        be = be + (brow >= pe).astype(jnp.int32)
        pend = jnp.where(lane == e, pe, pend)
        ps = pe
    d1_ref[...] = d1
    d2_ref[...] = d2
    n_used = lax.shift_right_logical(ps, MOE_ROWS_LOG2)
    blk_ref[...] = jnp.where(sub == 0, jnp.minimum(be, MOE_EXPERTS - 1),
                             jnp.where(sub == 1, pend, n_used))


def _finalize(counts, e1, e2, r1, r2, nb):
    nbp = (nb + LANES - 1) // LANES * LANES
    whole = pl.BlockSpec(e1.shape, lambda i, c: (0, 0, 0))
    ishape = jax.ShapeDtypeStruct(e1.shape, jnp.int32)
    return pl.pallas_call(
        _finalize_kernel,
        out_shape=(ishape, ishape, jax.ShapeDtypeStruct((SUBLANES, nbp), jnp.int32)),
        grid_spec=pltpu.PrefetchScalarGridSpec(
            num_scalar_prefetch=1,
            grid=(1,),
            in_specs=[whole, whole, whole, whole],
            out_specs=(whole, whole, pl.BlockSpec((SUBLANES, nbp), lambda i, c: (0, 0)))),
        compiler_params=_cparams("arbitrary"),
        name="moe_finalize",
    )(counts, e1, e2, r1, r2)


def _token_copy(src, r, dst, d, sem):
    return pltpu.make_async_copy(src.at[pl.ds(pl.multiple_of(r * SUBLANES, SUBLANES), SUBLANES), :],
                                 dst.at[pl.ds(pl.multiple_of(d * SUBLANES, SUBLANES), SUBLANES), :],
                                 sem)


def _zero_fill_padding(pend_ref, nu_ref, xs_out, zbuf, zsem):
    blk_rows = MOE_ROWS * SUBLANES
    nb = xs_out.shape[0] // blk_rows
    zbuf[...] = jnp.zeros_like(zbuf)

    def block_copy(b):
        r0 = pl.multiple_of(b * blk_rows, blk_rows)
        return pltpu.make_async_copy(zbuf, xs_out.at[pl.ds(r0, blk_rows), :], zsem)

    def seg_last_block(e):
        pe = pend_ref[e]
        prev = pend_ref[e - 1] if e > 0 else 0
        return pe > prev, lax.shift_right_logical(pe, MOE_ROWS_LOG2) - 1

    for e in range(MOE_EXPERTS):
        nonempty, b = seg_last_block(e)

        @pl.when(nonempty)
        def _():
            block_copy(b).start()

    def tail_start(b, c):
        block_copy(b).start()
        return c

    lax.fori_loop(nu_ref[0], nb, tail_start, 0)
    for e in range(MOE_EXPERTS):
        nonempty, b = seg_last_block(e)

        @pl.when(nonempty)
        def _():
            block_copy(b).wait()

    def tail_wait(b, c):
        block_copy(b).wait()
        return c

    lax.fori_loop(nu_ref[0], nb, tail_wait, 0)


def _dispatch_kernel(dest_ref, pend_ref, nu_ref, h_ref, xs_out, sem, zbuf, zsem, *, tiles):
    i = pl.program_id(0)

    @pl.when(i == 0)
    def _():
        _zero_fill_padding(pend_ref, nu_ref, xs_out, zbuf, zsem)

    rows = tiles * TM
    for q in range(tiles):
        base = i * (2 * rows) + q * TM

        def start(r, c):
            _token_copy(h_ref, q * TM + r, xs_out, dest_ref[base + r], sem).start(priority=0)
            _token_copy(h_ref, q * TM + r, xs_out, dest_ref[base + rows + r],
                        sem).start(priority=1)
            return c

        lax.fori_loop(0, TM, start, 0, unroll=8)
    for _ in range(2):
        pltpu.make_async_copy(h_ref, xs_out.at[pl.ds(0, tiles * TM * SUBLANES), :], sem).wait()


def _dispatch(dest, pad_end, n_used, h, n_rows):
    n = h.shape[0] // SUBLANES
    nt = n // TM
    tiles = _route_tiles(nt)
    return pl.pallas_call(
        functools.partial(_dispatch_kernel, tiles=tiles),
        out_shape=jax.ShapeDtypeStruct((n_rows * SUBLANES, LANES), F32),
        grid_spec=pltpu.PrefetchScalarGridSpec(
            num_scalar_prefetch=3,
            grid=(nt // tiles,),
            in_specs=[pl.BlockSpec((tiles * TM * SUBLANES, LANES), lambda i, *_: (i, 0))],
            out_specs=pl.BlockSpec(memory_space=pl.ANY),
            scratch_shapes=[pltpu.SemaphoreType.DMA, pltpu.VMEM((MOE_ROWS * SUBLANES, LANES), F32),
                            pltpu.SemaphoreType.DMA]),
        compiler_params=_cparams("arbitrary"),
        name="moe_dispatch",
    )(dest, pad_end, n_used, h)


X_SLOTS = 4


Y_SLOTS = 3


def _expert_kernel(be_ref, pend_ref, nu_ref, xs_hbm, wg_hbm, wu_hbm, wd_hbm, ys_hbm,
                   xbuf, ybuf, wg_f, wu_f, wd_f, wg_s, wu_s, wd_s, xsem, ysem, wsem, ord_ref,
                   *, layer):
    nu = nu_ref[0]
    blk_rows = MOE_ROWS * SUBLANES
    nb = ys_hbm.shape[0] // blk_rows

    def rows_of(ref, blk):
        return ref.at[pl.ds(pl.multiple_of(blk * blk_rows, blk_rows), blk_rows), :]

    def x_copy(blk, slot):
        return pltpu.make_async_copy(rows_of(xs_hbm, blk), xbuf.at[slot], xsem.at[slot])

    def y_copy(blk, slot):
        return pltpu.make_async_copy(ybuf.at[slot], rows_of(ys_hbm, blk), ysem.at[slot])

    def w_copies(e, slot):
        return (pltpu.make_async_copy(wg_hbm.at[layer, e], wg_f.at[slot], wsem.at[slot]),
                pltpu.make_async_copy(wu_hbm.at[layer, e], wu_f.at[slot], wsem.at[slot]),
                pltpu.make_async_copy(wd_hbm.at[layer, e], wd_f.at[slot], wsem.at[slot]))

    ord_ref[0] = 0
    for j in range(X_SLOTS - 1):
        @pl.when(j < nu)
        def _():
            x_copy(j, j).start()

    @pl.when(nu > 0)
    def _():
        for c in w_copies(be_ref[0], 0):
            c.start()

    def block(b, carry):
        ahead = b + (X_SLOTS - 1)

        @pl.when(ahead < nu)
        def _():
            x_copy(ahead, lax.rem(ahead, X_SLOTS)).start()

        e = be_ref[b]

        @pl.when((b == 0) | (e != be_ref[jnp.maximum(b - 1, 0)]))
        def _():
            k = ord_ref[0]
            slot = lax.rem(k, 2)
            for c in w_copies(e, slot):
                c.wait()
            wg_s[...] = wg_f[slot].astype(BF16)
            wu_s[...] = wu_f[slot].astype(BF16)
            wd_s[...] = wd_f[slot].astype(BF16)
            nxt = lax.shift_right_logical(pend_ref[e], MOE_ROWS_LOG2)

            @pl.when(nxt < nu)
            def _():
                for c in w_copies(be_ref[nxt], 1 - slot):
                    c.start(priority=1)

            ord_ref[0] = k + 1

        slot = lax.rem(b, X_SLOTS)
        yslot = lax.rem(b, Y_SLOTS)
        x_copy(b, slot).wait()

        @pl.when(b >= Y_SLOTS)
        def _():
            y_copy(b - Y_SLOTS, yslot).wait()

        x = _load_token_tiles(xbuf.at[slot]).astype(BF16)
        a = jnp.dot(x, wg_s[...], preferred_element_type=F32)
        u = jnp.dot(x, wu_s[...], preferred_element_type=F32)
        _store_token_tiles(ybuf.at[yslot], jnp.dot((_silu(a) * u).astype(BF16), wd_s[...],
                                                   preferred_element_type=F32))
        y_copy(b, yslot).start()
        return carry

    lax.fori_loop(0, nu, block, 0)
    for j in range(1, Y_SLOTS + 1):
        @pl.when(nu >= j)
        def _():
            y_copy(nu - j, lax.rem(nu - j, Y_SLOTS)).wait()

    ybuf[0] = jnp.zeros(ybuf.shape[1:], ybuf.dtype)

    def tail_start(b, c):
        y_copy(b, 0).start()
        return c

    def tail_wait(b, c):
        y_copy(b, 0).wait()
        return c

    lax.fori_loop(nu, nb, tail_start, 0)
    lax.fori_loop(nu, nb, tail_wait, 0)


def _experts(blk_expert, pad_end, n_used, xs, w_gate, w_up, w_down, layer):
    d, hid = w_gate.shape[2:]
    blk_rows = MOE_ROWS * SUBLANES
    hbm = pl.BlockSpec(memory_space=pl.ANY)
    return pl.pallas_call(
        functools.partial(_expert_kernel, layer=layer),
        out_shape=jax.ShapeDtypeStruct(xs.shape, F32),
        grid_spec=pltpu.PrefetchScalarGridSpec(
            num_scalar_prefetch=3,
            grid=(1,),
            in_specs=[hbm, hbm, hbm, hbm],
            out_specs=hbm,
            scratch_shapes=[pltpu.VMEM((X_SLOTS, blk_rows, LANES), F32),
                            pltpu.VMEM((Y_SLOTS, blk_rows, LANES), F32),
                            pltpu.VMEM((2, d, hid), F32), pltpu.VMEM((2, d, hid), F32),
                            pltpu.VMEM((2, hid, d), F32),
                            pltpu.VMEM((d, hid), BF16), pltpu.VMEM((d, hid), BF16),
                            pltpu.VMEM((hid, d), BF16),
                            pltpu.SemaphoreType.DMA((X_SLOTS,)), pltpu.SemaphoreType.DMA((Y_SLOTS,)),
                            pltpu.SemaphoreType.DMA((2,)), pltpu.SMEM((1,), jnp.int32)]),
        compiler_params=_cparams("arbitrary"),
        name="moe_experts",
    )(blk_expert, pad_end, n_used, xs, w_gate, w_up, w_down)


def _combined_tile(dest_ref, s_ref, wt_ref, mod_ref, ys_ref, ybuf, sem, *, ctx_tiles, tiles):
    i = pl.program_id(0)
    slot = i % 2
    rows = tiles * TM

    def gather(tile, slot):
        tile = jnp.asarray(tile, jnp.int32)
        base = lax.div(tile, tiles) * (2 * rows) + lax.rem(tile, tiles) * TM

        def start(r, c):
            _token_copy(ys_ref, dest_ref[base + r], ybuf.at[slot, 0], r,
                        sem.at[slot]).start(priority=0)
            _token_copy(ys_ref, dest_ref[base + rows + r], ybuf.at[slot, 1], r,
                        sem.at[slot]).start(priority=1)
            return c

        lax.fori_loop(0, TM, start, 0, unroll=8)

    @pl.when(i == 0)
    def _():
        gather(0, 0)

    @pl.when(i + 1 < pl.num_programs(0))
    def _():
        gather(i + 1, 1 - slot)

    for k in range(2):
        pltpu.make_async_copy(ys_ref.at[pl.ds(0, TM * SUBLANES), :], ybuf.at[slot, k],
                              sem.at[slot]).wait()
    gate = _mod_rows(mod_ref, i, ctx_tiles, 3)[2]
    wt = wt_ref[...]
    y = (wt[:, 0:1] * _load_token_tiles(ybuf.at[slot, 0])
         + wt[:, 1:2] * _load_token_tiles(ybuf.at[slot, 1]))
    return s_ref[...] + gate * y


def _combine_scratch():
    return [pltpu.VMEM((2, 2, TM * SUBLANES, LANES), F32), pltpu.SemaphoreType.DMA((2,))]


def _combine_kernel(dest_ref, s_ref, wt_ref, mod_ref, fg_ref, ys_ref, o_ref, ybuf, sem,
                    *, final_norm, **kw):
    out = _combined_tile(dest_ref, s_ref, wt_ref, mod_ref, ys_ref, ybuf, sem, **kw)
    if final_norm:
        out = _rms(out, fg_ref[...])
    o_ref[...] = out


def _combine(dest, s, wts, mods, final_g, ys, ctx_tiles, final_norm):
    n, d = s.shape
    return pl.pallas_call(
        functools.partial(_combine_kernel, ctx_tiles=ctx_tiles, final_norm=final_norm,
                          tiles=_route_tiles(n // TM)),
        out_shape=jax.ShapeDtypeStruct((n, d), F32),
        grid_spec=pltpu.PrefetchScalarGridSpec(
            num_scalar_prefetch=1,
            grid=(n // TM,),
            in_specs=[pl.BlockSpec((TM, d), lambda i, dst: (i, 0)),
                      pl.BlockSpec((TM, LANES), lambda i, dst: (i, 0)),
                      pl.BlockSpec(mods.shape, lambda i, dst: (0, 0)),
                      pl.BlockSpec((1, d), lambda i, dst: (0, 0)),
                      pl.BlockSpec(memory_space=pl.ANY)],
            out_specs=pl.BlockSpec((TM, d), lambda i, dst: (i, 0)),
            scratch_shapes=_combine_scratch()),
        compiler_params=_cparams("arbitrary"),
        name="moe_combine",
    )(dest, s, wts, mods, final_g.reshape(1, d), ys)


def _moe_layer(s, mods, norm_g, rg_w, rg_b, re_w, re_b, w_gate, w_up, w_down, layer, ctx_tiles,
               final_g, final_norm, defer=False):
    n, d = s.shape
    h, e1, e2, r1, r2, wts, cnt = _router(s, mods, norm_g, rg_w, rg_b, re_w, re_b, ctx_tiles)
    counts = cnt[MOE_GROUPS:MOE_GROUPS + MOE_EXPERTS, 0].astype(jnp.int32)
    nb = (2 * n + MOE_EXPERTS * (MOE_ROWS - 1)) // MOE_ROWS + 1
    d1, d2, blk = _finalize(counts, e1, e2, r1, r2, nb)
    dest = jnp.concatenate([d1, d2], axis=1).reshape(2 * n)
    n_used = blk[2, :1]
    pad_end = blk[1, :MOE_EXPERTS]
    xs = _dispatch(dest, pad_end, n_used, h, nb * MOE_ROWS)
    ys = _experts(blk[0, :nb], pad_end, n_used, xs, w_gate, w_up, w_down, layer)
    if defer:
        return _PendingCombine(dest, s, wts, mods, ys, ctx_tiles)
    return _combine(dest, s, wts, mods, final_g, ys, ctx_tiles, final_norm)


class _PendingCombine(NamedTuple):
    dest: jax.Array
    s: jax.Array
    wts: jax.Array
    mods: jax.Array
    ys: jax.Array
    ctx_tiles: int


def _pending_operands(p):
    n, d = p.s.shape
    kw = dict(ctx_tiles=p.ctx_tiles, tiles=_route_tiles(n // TM))
    specs = [pl.BlockSpec((TM, d), lambda i, dst: (i, 0)),
             pl.BlockSpec((TM, LANES), lambda i, dst: (i, 0)),
             pl.BlockSpec(p.mods.shape, lambda i, dst: (0, 0)),
             pl.BlockSpec(memory_space=pl.ANY)]
    return kw, p.dest, specs, (p.s, p.wts, p.mods, p.ys), _combine_scratch()


def _conv_tile(x, prev_ref, next_ref, has_prev, has_next, w_ref, b_ref):
    rows = x.shape[0]
    S = SUBLANES
    ridx = lax.broadcasted_iota(jnp.int32, (S, x.shape[1]), 0)
    pm = jnp.where(has_prev, 1.0, 0.0)
    nm = jnp.where(has_next, 1.0, 0.0)
    p2 = prev_ref[S - 2:S - 1, :] * pm
    p1 = prev_ref[S - 1:S, :] * pm
    n1 = next_ref[0:1, :] * nm

    def fix_head(rolled, head):
        return jnp.concatenate([head(rolled[:S]), rolled[S:]], axis=0)

    xm1 = fix_head(pltpu.roll(x, 1, axis=0), lambda g: jnp.where(ridx == 0, p1, g))
    xm2 = fix_head(pltpu.roll(x, 2, axis=0),
                   lambda g: jnp.where(ridx == 0, p2, jnp.where(ridx == 1, p1, g)))
    xp1 = pltpu.roll(x, rows - 1, axis=0)
    xp1 = jnp.concatenate([xp1[:rows - S], jnp.where(ridx == S - 1, n1, xp1[rows - S:])], axis=0)
    return (xm2 * w_ref[0:1, :] + xm1 * w_ref[1:2, :] + x * w_ref[2:3, :]
            + xp1 * w_ref[3:4, :] + b_ref[...])


def _ml_proj_kernel(dest_ref, s_ref, wt_ref, pmod_ref, ys_ref, mod_ref, g_ref, w_ref, wg_ref,
                    gb_ref, snew_ref, qk_ref, v_ref, o_ref, gt_ref, ybuf, sem, **pending_kw):
    i = pl.program_id(0)
    x = _combined_tile(dest_ref, s_ref, wt_ref, pmod_ref, ys_ref, ybuf, sem, **pending_kw)
    snew_ref[...] = x
    shift, scale, _ = _mod_rows(mod_ref, i, 1, 0)
    h = _rms(x, g_ref[...]) * (1.0 + scale) + shift
    z = _bdot(h, w_ref[...])
    nqk = qk_ref.shape[1]
    nv = v_ref.shape[1]
    qk_ref[...] = z[:, :nqk]
    v_ref[...] = z[:, nqk:nqk + nv].astype(BF16)
    o_ref[...] = z[:, nqk + nv:].astype(BF16)
    pre = _dot3(h, wg_ref[...], ((1,), (0,))) + gb_ref[...]
    lane = lax.broadcasted_iota(jnp.int32, pre.shape, 1)
    is_forget = ((lane >> 2) & 1) == 1
    gt_ref[...] = jnp.where(is_forget, -_softplus(-pre), pre)


def _ml_proj(pending, mods, norm_g, w_in, gate_b):
    n, d = pending.s.shape
    kw, dest, p_specs, p_args, p_scratch = _pending_operands(pending)
    nqk = 2 * ML_HEADS * ML_DK
    nv = ML_HEADS * ML_DV
    n_main = nqk + 2 * nv
    n_gate = w_in.shape[1] - n_main
    w_main = w_in[:, :n_main].astype(BF16)
    w_gate = jnp.concatenate([w_in[:, n_main:], jnp.zeros((d, LANES - n_gate), F32)], axis=1)
    gb = jnp.concatenate([gate_b.reshape(n_gate), jnp.zeros((LANES - n_gate,), F32)]).reshape(1, LANES)
    tile = lambda w: pl.BlockSpec((TM, w), lambda i, dst: (i, 0))
    whole = lambda shape: pl.BlockSpec(shape, lambda i, dst: (0,) * len(shape))
    return pl.pallas_call(
        functools.partial(_ml_proj_kernel, **kw),
        out_shape=(jax.ShapeDtypeStruct((n, d), F32),
                   jax.ShapeDtypeStruct((n, nqk), F32), jax.ShapeDtypeStruct((n, nv), BF16),
                   jax.ShapeDtypeStruct((n, nv), BF16), jax.ShapeDtypeStruct((n, LANES), F32)),
        grid_spec=pltpu.PrefetchScalarGridSpec(
            num_scalar_prefetch=1,
            grid=(n // TM,),
            in_specs=p_specs + [whole(mods.shape), whole((1, d)), whole(w_main.shape),
                                whole((d, LANES)), whole((1, LANES))],
            out_specs=(tile(d), tile(nqk), tile(nv), tile(nv), tile(LANES)),
            scratch_shapes=p_scratch),
        compiler_params=_cparams("arbitrary"),
        name="mlstm_proj",
    )(dest, *p_args, mods, norm_g.reshape(1, d), w_main, w_gate, gb)


def _ml_chunk_index(j, n_chunks, ctx_chunks, reverse):
    if not reverse:
        return j
    return jnp.where(j < ctx_chunks, ctx_chunks - 1 - j, n_chunks - 1 + ctx_chunks - j)


def _ml_rec_kernel(qk_ref, qkp_ref, qkn_ref, v_ref, gt_ref, gtt_ref, cw_ref, cb_ref, o_ref,
                   c_scr, n_scr, m_scr, *, reverse, n_chunks, ctx_chunks):
    j = pl.program_id(0)
    c = _ml_chunk_index(j, n_chunks, ctx_chunks, reverse)

    @pl.when(j == 0)
    def _():
        c_scr[...] = jnp.zeros_like(c_scr)
        n_scr[...] = jnp.zeros_like(n_scr)
        m_scr[...] = jnp.zeros_like(m_scr)

    has_prev = (c != 0) & (c != ctx_chunks)
    has_next = (c != ctx_chunks - 1) & (c != n_chunks - 1)
    qk = _silu(_conv_tile(qk_ref[...], qkp_ref, qkn_ref, has_prev, has_next, cw_ref, cb_ref))
    L = ML_CHUNK
    ri = lax.broadcasted_iota(jnp.int32, (L, L), 0)
    ci = lax.broadcasted_iota(jnp.int32, (L, L), 1)
    past = (ci >= ri) if reverse else (ci <= ri)
    pastf = past.astype(F32)
    gt = gt_ref[...]
    gtt = gtt_ref[...]
    b_col = jnp.dot(pastf, gt, precision=HI, preferred_element_type=F32)
    b_row = jnp.dot(gtt, pastf.T, precision=HI, preferred_element_type=F32)
    last = 0 if reverse else L - 1
    dbase = 8 if reverse else 0
    nq = ML_HEADS * ML_DK
    for hd in range(ML_HEADS):
        cl = dbase + hd
        cf = dbase + 4 + hd
        q = qk[:, hd * ML_DK:(hd + 1) * ML_DK] * (ML_DK ** -0.5)
        k = qk[:, nq + hd * ML_DK:nq + (hd + 1) * ML_DK]
        v = v_ref[:, hd * ML_DV:(hd + 1) * ML_DV]
        li_c = gt[:, cl:cl + 1]
        li_r = gtt[cl:cl + 1, :]
        b_c = b_col[:, cf:cf + 1]
        b_r = b_row[cf:cf + 1, :]
        g = b_r[:, last:last + 1]
        m0 = m_scr[hd:hd + 1, 0:1]
        c0 = c_scr[hd]
        n0 = n_scr[hd:hd + 1, :]
        a_c = g - b_c + li_c
        a_r = g - b_r + li_r
        m_loc = jnp.max(a_r, axis=-1, keepdims=True)
        inter = b_c + m0
        dlog = jnp.where(past, b_c - b_r + li_r, -jnp.inf)
        m = jnp.maximum(inter, jnp.max(dlog, axis=-1, keepdims=True))
        qb = q.astype(BF16)
        sc = lax.dot_general(qb, k.astype(BF16), (((1,), (1,)), ((), ())),
                             preferred_element_type=F32) * jnp.exp(dlog - m)
        w_inter = jnp.exp(inter - m)
        num = (jnp.dot(sc.astype(BF16), v, preferred_element_type=F32)
               + w_inter * jnp.dot(qb, c0.astype(BF16), preferred_element_type=F32))
        den = (jnp.sum(sc, axis=-1, keepdims=True)
               + w_inter * jnp.sum(q * n0, axis=-1, keepdims=True))
        o_ref[:, hd * ML_DV:(hd + 1) * ML_DV] = (
            num / jnp.maximum(jnp.abs(den), jnp.exp(-m))).astype(BF16)
        m_new = jnp.maximum(g + m0, m_loc)
        dec = jnp.exp(g + m0 - m_new)
        scl = jnp.exp(m_loc - m_new)
        kw = k * jnp.exp(a_c - m_loc)
        c_scr[hd] = dec * c0 + scl * jnp.dot(kw.T.astype(BF16), v, preferred_element_type=F32)
        n_scr[hd:hd + 1, :] = dec * n0 + scl * jnp.sum(kw, axis=0, keepdims=True)
        m_scr[hd:hd + 1, :] = jnp.broadcast_to(m_new, (1, LANES))


def _ml_rec(qk, v, gt, gtt, conv_w, conv_b, reverse):
    n, nqk = qk.shape
    nv = v.shape[1]
    L = ML_CHUNK
    nc = n // L
    cc = TM // L
    hb = L // SUBLANES
    idx = lambda j: _ml_chunk_index(j, nc, cc, reverse)
    last8 = n // SUBLANES - 1
    return pl.pallas_call(
        functools.partial(_ml_rec_kernel, reverse=reverse, n_chunks=nc, ctx_chunks=cc),
        out_shape=jax.ShapeDtypeStruct((n, nv), BF16),
        grid=(nc,),
        in_specs=[pl.BlockSpec((L, nqk), lambda j: (idx(j), 0)),
                  pl.BlockSpec((SUBLANES, nqk), lambda j: (jnp.maximum(idx(j) * hb - 1, 0), 0)),
                  pl.BlockSpec((SUBLANES, nqk), lambda j: (jnp.minimum((idx(j) + 1) * hb, last8), 0)),
                  pl.BlockSpec((L, nv), lambda j: (idx(j), 0)),
                  pl.BlockSpec((L, LANES), lambda j: (idx(j), 0)),
                  pl.BlockSpec((2 * SUBLANES, L), lambda j: (0, idx(j))),
                  _full((4, nqk)), _full((1, nqk))],
        out_specs=pl.BlockSpec((L, nv), lambda j: (idx(j), 0)),
        scratch_shapes=[pltpu.VMEM((ML_HEADS, ML_DK, ML_DV), F32),
                        pltpu.VMEM((SUBLANES, ML_DK), F32),
                        pltpu.VMEM((SUBLANES, LANES), F32)],
        compiler_params=_cparams("arbitrary"),
        name="mlstm_rev" if reverse else "mlstm_fwd",
    )(qk, qk, qk, v, gt, gtt, conv_w, conv_b.reshape(1, nqk))


def _ml_out_kernel(hf_ref, hr_ref, o_ref, s_ref, mod_ref, ng_ref, w_ref, out_ref, p_scr):
    i = pl.program_id(0)
    gate = _mod_rows(mod_ref, i, 1, 0)[2]
    hs = hf_ref[...].astype(F32) + hr_ref[...].astype(F32)
    sig = _sigmoid(o_ref[...].astype(F32))
    ng = ng_ref[...]
    for hd in range(ML_HEADS):
        cs = slice(hd * ML_DV, (hd + 1) * ML_DV)
        seg = hs[:, cs]
        hn = seg * lax.rsqrt(jnp.mean(seg * seg, axis=-1, keepdims=True) + EPS) * ng[:, cs]
        p_scr[:, cs] = (hn * sig[:, cs]).astype(BF16)
    y = jnp.dot(p_scr[...], w_ref[...], preferred_element_type=F32)
    out_ref[...] = s_ref[...] + gate * y


def _ml_out(hf, hr, o, s, mods, norm_g, w_out):
    n, d = s.shape
    nv = hf.shape[1]
    tile = lambda w: pl.BlockSpec((TM, w), lambda i: (i, 0))
    return pl.pallas_call(
        _ml_out_kernel,
        out_shape=jax.ShapeDtypeStruct((n, d), F32),
        grid=(n // TM,),
        in_specs=[tile(nv), tile(nv), tile(nv), tile(d), _full(mods.shape), _full((1, nv)),
                  _full(w_out.shape)],
        out_specs=tile(d),
        scratch_shapes=[pltpu.VMEM((TM, nv), BF16)],
        compiler_params=_cparams("arbitrary"),
        name="mlstm_out",
    )(hf, hr, o, s, mods, norm_g.reshape(1, nv), w_out.astype(BF16))


def _mlstm_layer(pending, mods, norm_g, w_in, conv_w, conv_b, gate_b, ml_norm_g, w_out):
    s, qk, v, o, gt = _ml_proj(pending, mods, norm_g, w_in, gate_b)
    gtt = gt[:, :2 * SUBLANES].T
    hf = _ml_rec(qk, v, gt, gtt, conv_w, conv_b, False)
    hr = _ml_rec(qk, v, gt, gtt, conv_w, conv_b, True)
    return _ml_out(hf, hr, o, s, mods, ml_norm_g, w_out)


def _lru_proj_kernel(dest_ref, s_ref, wt_ref, pmod_ref, ys_ref, mod_ref, g_ref, w_ref,
                     snew_ref, gl_ref, xr_ref, ybuf, sem, **pending_kw):
    i = pl.program_id(0)
    x = _combined_tile(dest_ref, s_ref, wt_ref, pmod_ref, ys_ref, ybuf, sem, **pending_kw)
    snew_ref[...] = x
    shift, scale, _ = _mod_rows(mod_ref, i, 1, 0)
    h = _rms(x, g_ref[...]) * (1.0 + scale) + shift
    z = _bdot(h, w_ref[...])
    w = gl_ref.shape[1]
    gl_ref[...] = _gelu(z[:, :w]).astype(BF16)
    xr_ref[...] = z[:, w:]


def _lru_proj(pending, mods, norm_g, w_in):
    n, d = pending.s.shape
    kw, dest, p_specs, p_args, p_scratch = _pending_operands(pending)
    w = w_in.shape[1] // 2
    tile = lambda c: pl.BlockSpec((TM, c), lambda i, dst: (i, 0))
    whole = lambda shape: pl.BlockSpec(shape, lambda i, dst: (0,) * len(shape))
    return pl.pallas_call(
        functools.partial(_lru_proj_kernel, **kw),
        out_shape=(jax.ShapeDtypeStruct((n, d), F32), jax.ShapeDtypeStruct((n, w), BF16),
                   jax.ShapeDtypeStruct((n, w), F32)),
        grid_spec=pltpu.PrefetchScalarGridSpec(
            num_scalar_prefetch=1,
            grid=(n // TM,),
            in_specs=p_specs + [whole(mods.shape), whole((1, d)), whole(w_in.shape)],
            out_specs=(tile(d), tile(w), tile(w)),
            scratch_shapes=p_scratch),
        compiler_params=_cparams("arbitrary"),
        name="rglru_proj",
    )(dest, *p_args, mods, norm_g.reshape(1, d), w_in.astype(BF16))


def _lru_tile_index(j, n_tiles, reverse):
    if not reverse:
        return j
    return jnp.where(j == 0, 0, n_tiles - j)


def _lru_scan_kernel(x_ref, xp_ref, xn_ref, cw_ref, cb_ref, wg_ref, ba_ref, bx_ref, lam_ref,
                     o_ref, a_scr, u_scr, carry, *, reverse, n_tiles):
    j = pl.program_id(0)
    t = _lru_tile_index(j, n_tiles, reverse)

    @pl.when(j == 0)
    def _():
        carry[...] = jnp.zeros_like(carry)

    has_prev = t > 1
    has_next = (t != 0) & (t != n_tiles - 1)
    xr = _conv_tile(x_ref[...], xp_ref, xn_ref, has_prev, has_next, cw_ref, cb_ref)
    sp = _softplus(-lam_ref[...])
    B = LRU_BLOCK
    for hd in range(LRU_HEADS):
        cs = slice(hd * B, (hd + 1) * B)
        xh = xr[:, cs]
        y = jnp.dot(xh.astype(BF16), wg_ref[hd], preferred_element_type=F32)
        r = _sigmoid(y[:, :B] + ba_ref[:, cs])
        ig = _sigmoid(y[:, B:] + bx_ref[:, cs])
        log_a = -LRU_C * r * sp[:, cs]
        a = jnp.exp(log_a)
        a_scr[:, cs] = a
        v = 1.0 - a * a
        u_scr[:, cs] = jnp.where(v > 0.0, v * lax.rsqrt(v), 0.0) * (ig * xh)

    S = SUBLANES
    w = a_scr.shape[1]
    sidx = lax.broadcasted_iota(jnp.int32, (S, w), 0)

    def group(gi, c):
        g = (TM // S - 1 - gi) if reverse else gi
        r0 = pl.multiple_of(g * S, S)
        a = a_scr[pl.ds(r0, S), :]
        u = u_scr[pl.ds(r0, S), :]
        for sft in (1, 2, 4):
            if reverse:
                ok = sidx < S - sft
                a_e = pltpu.roll(a, S - sft, axis=0)
                u_e = pltpu.roll(u, S - sft, axis=0)
            else:
                ok = sidx >= sft
                a_e = pltpu.roll(a, sft, axis=0)
                u_e = pltpu.roll(u, sft, axis=0)
            u = jnp.where(ok, a * u_e + u, u)
            a = jnp.where(ok, a * a_e, a)
        hcur = a * carry[...] + u
        u_scr[pl.ds(r0, S), :] = hcur
        edge = 0 if reverse else S - 1
        carry[...] = jnp.broadcast_to(hcur[edge:edge + 1, :], (S, w))
        return c

    lax.fori_loop(0, TM // S, group, 0, unroll=4)
    o_ref[...] = u_scr[...].astype(BF16)


def _lru_scan(xraw, conv_w, conv_b, w_a, b_a, w_x, b_x, lam, reverse):
    n, w = xraw.shape
    nt = n // TM
    hb = TM // SUBLANES
    idx = lambda j: _lru_tile_index(j, nt, reverse)
    last8 = n // SUBLANES - 1
    wg = jnp.concatenate([w_a, w_x], axis=-1).astype(BF16)
    return pl.pallas_call(
        functools.partial(_lru_scan_kernel, reverse=reverse, n_tiles=nt),
        out_shape=jax.ShapeDtypeStruct((n, w), BF16),
        grid=(nt,),
        in_specs=[pl.BlockSpec((TM, w), lambda j: (idx(j), 0)),
                  pl.BlockSpec((SUBLANES, w), lambda j: (jnp.maximum(idx(j) * hb - 1, 0), 0)),
                  pl.BlockSpec((SUBLANES, w), lambda j: (jnp.minimum((idx(j) + 1) * hb, last8), 0)),
                  _full((4, w)), _full((1, w)), _full(wg.shape), _full((1, w)), _full((1, w)),
                  _full((1, w))],
        out_specs=pl.BlockSpec((TM, w), lambda j: (idx(j), 0)),
        scratch_shapes=[pltpu.VMEM((TM, w), F32), pltpu.VMEM((TM, w), F32),
                        pltpu.VMEM((SUBLANES, w), F32)],
        compiler_params=_cparams("arbitrary"),
        name="rglru_rev" if reverse else "rglru_fwd",
    )(xraw, xraw, xraw, conv_w, conv_b.reshape(1, w), wg, b_a.reshape(1, w), b_x.reshape(1, w),
      lam.reshape(1, w))


def _lru_out_kernel(gl_ref, hf_ref, hr_ref, s_ref, mod_ref, w_ref, out_ref):
    gate = _mod_rows(mod_ref, 1, 0, 0)[2]
    p = gl_ref[...].astype(F32) * (hf_ref[...].astype(F32) + hr_ref[...].astype(F32))
    out_ref[...] = s_ref[...] + gate * _bdot(p, w_ref[...])


def _lru_out(gl, hf, hr, s, mods, w_out):
    n, d = s.shape
    w = gl.shape[1]
    lat = lambda c: pl.BlockSpec((TM, c), lambda i: (i + 1, 0))
    return pl.pallas_call(
        _lru_out_kernel,
        out_shape=jax.ShapeDtypeStruct((n - TM, d), F32),
        grid=(n // TM - 1,),
        in_specs=[lat(w), lat(w), lat(w), lat(d), _full(mods.shape), _full(w_out.shape)],
        out_specs=pl.BlockSpec((TM, d), lambda i: (i, 0)),
        compiler_params=_cparams("arbitrary"),
        name="rglru_out",
    )(gl, hf, hr, s, mods, w_out.astype(BF16))


def _rglru_layer(pending, mods, norm_g, w_in, conv_w, conv_b, w_a, b_a, w_x, b_x, lam, w_out):
    s, gl, xraw = _lru_proj(pending, mods, norm_g, w_in)
    hf = _lru_scan(xraw, conv_w, conv_b, w_a[0], b_a[0], w_x[0], b_x[0], lam[0], False)
    hr = _lru_scan(xraw, conv_w, conv_b, w_a[1], b_a[1], w_x[1], b_x[1], lam[1], True)
    return _lru_out(gl, hf, hr, s, mods, w_out)


def _fn_proj_kernel(s_ref, mod_ref, g_ref, wt_ref, cs_ref, yr_ref, yi_ref, ar_scr, ai_scr):
    shift, scale, _ = _mod_rows(mod_ref, 1, 0, 0)
    nm = wt_ref.shape[0]
    gw = nm // FN_GROUPS
    per = TM // FFT_N2
    nj = FN_TB // FFT_N2
    csb = cs_ref[...].astype(BF16)

    def sub(tc, c):
        r0 = pl.multiple_of(tc * TM, TM)
        h = _rms(s_ref[pl.ds(r0, TM), :], g_ref[...]) * (1.0 + scale) + shift
        zt = lax.dot_general(wt_ref[...], h.astype(BF16), (((1,), (1,)), ((), ())),
                             preferred_element_type=F32).astype(BF16)
        for g in range(FN_GROUPS):
            y = jnp.dot(csb, zt[g * gw:(g + 1) * gw, :], preferred_element_type=F32)
            for q in range(per):
                row0 = pl.multiple_of((tc * per + q) * _slab_pitch(nm) + g * gw, SUBLANES)
                ar_scr[pl.ds(row0, gw), :] = y[:gw, q * FFT_N2:(q + 1) * FFT_N2]
                ai_scr[pl.ds(row0, gw), :] = y[gw:, q * FFT_N2:(q + 1) * FFT_N2]
        return c

    lax.fori_loop(0, FN_TB // TM, sub, 0)

    def relayout(m, c):
        yr_ref[m] = ar_scr[pl.ds(m, nj, stride=_slab_pitch(nm)), :]
        yi_ref[m] = ai_scr[pl.ds(m, nj, stride=_slab_pitch(nm)), :]
        return c

    lax.fori_loop(0, nm, relayout, 0, unroll=8)


def _slab_pitch(rows):
    return rows + SUBLANES


def _dft_cos_sin(n, scale):
    k = np.arange(n, dtype=np.int64)
    ang = 2.0 * np.pi * ((k[:, None] * k[None, :]) % n).astype(np.float64) / n
    return np.cos(ang) * scale, np.sin(ang) * scale


def _fn_proj(s, mods, norm_g, w_in):
    t, d = s.shape
    nm = w_in.shape[1]
    gw = nm // FN_GROUPS
    n1 = t // FFT_N2
    nj = FN_TB // FFT_N2
    c, sn = _dft_cos_sin(gw, gw ** -0.5)
    cs = jnp.asarray(np.concatenate([c, -sn], axis=0), F32)
    yspec = pl.BlockSpec((nm, nj, FFT_N2), lambda i: (0, i, 0))
    yshape = jax.ShapeDtypeStruct((nm, n1, FFT_N2), F32)
    return pl.pallas_call(
        _fn_proj_kernel,
        out_shape=(yshape, yshape),
        grid=(t // FN_TB,),
        in_specs=[pl.BlockSpec((FN_TB, d), lambda i: (i, 0)), _full(mods.shape), _full((1, d)),
                  _full((nm, d)), _full(cs.shape)],
        out_specs=(yspec, yspec),
        scratch_shapes=[pltpu.VMEM((nj * _slab_pitch(nm), FFT_N2), F32),
                        pltpu.VMEM((nj * _slab_pitch(nm), FFT_N2), F32)],
        compiler_params=_cparams("arbitrary"),
        name="fourier_proj",
    )(s, mods, norm_g.reshape(1, d), w_in.T.astype(BF16), cs)


def _fn_fft_kernel(yr_ref, yi_ref, m_ref, tc_ref, ts_ref, d_ref, o_ref):
    n1 = yr_ref.shape[1]
    n2 = FFT_N2
    xr = jnp.concatenate([yr_ref[m].astype(BF16) for m in range(FN_CB)], axis=1)
    xi = jnp.concatenate([yi_ref[m].astype(BF16) for m in range(FN_CB)], axis=1)
    a = jnp.dot(m_ref[...].astype(BF16), jnp.concatenate([xr, xi], axis=0),
                preferred_element_type=F32)
    ar = a[:n1]
    ai = a[n1:]
    tc = jnp.concatenate([tc_ref[...]] * FN_CB, axis=1)
    ts = jnp.concatenate([ts_ref[...]] * FN_CB, axis=1)
    br = ar * tc + ai * ts
    bi = ai * tc - ar * ts
    bst = jnp.concatenate(
        [jnp.concatenate([br[:, m * n2:(m + 1) * n2], bi[:, m * n2:(m + 1) * n2]], axis=1)
         for m in range(FN_CB)], axis=0).astype(BF16)
    res = lax.dot_general(d_ref[...].astype(BF16), bst, (((1,), (1,)), ((), ())),
                          preferred_element_type=F32)
    for m in range(FN_CB):
        o_ref[m] = res[:, m * n1:(m + 1) * n1]


def _fn_fft(yr, yi):
    nm, n1, n2 = yr.shape
    t = n1 * n2
    c, sn = _dft_cos_sin(n1, n1 ** -0.5)
    m = jnp.asarray(np.block([[c, sn], [-sn, c]]), F32)
    k1 = np.arange(n1, dtype=np.int64)[:, None]
    t2 = np.arange(n2, dtype=np.int64)[None, :]
    ang = 2.0 * np.pi * ((k1 * t2) % t).astype(np.float64) / t
    tc = jnp.asarray(np.cos(ang), F32)
    ts = jnp.asarray(np.sin(ang), F32)
    c2, s2 = _dft_cos_sin(n2, n2 ** -0.5)
    dm = jnp.asarray(np.concatenate([c2, s2], axis=1), F32)
    yspec = pl.BlockSpec((FN_CB, n1, n2), lambda i: (i, 0, 0))
    return pl.pallas_call(
        _fn_fft_kernel,
        out_shape=jax.ShapeDtypeStruct((nm, n2, n1), F32),
        grid=(nm // FN_CB,),
        in_specs=[yspec, yspec, _full(m.shape), _full(tc.shape), _full(ts.shape), _full(dm.shape)],
        out_specs=pl.BlockSpec((FN_CB, n2, n1), lambda i: (i, 0, 0)),
        compiler_params=_cparams("arbitrary"),
        name="fourier_fft",
    )(yr, yi, m, tc, ts, dm)


def _fn_out_kernel(ft_ref, w_ref, s_ref, mod_ref, o_ref, a_scr):
    gate = _mod_rows(mod_ref, 1, 0, 0)[2]
    nm, nj, n1 = ft_ref.shape

    def relayout(m, c):
        a_scr[pl.ds(m, nj, stride=_slab_pitch(nm)), :] = ft_ref[m]
        return c

    lax.fori_loop(0, nm, relayout, 0, unroll=8)
    for j in range(nj):
        p0 = j * _slab_pitch(nm)
        slab = a_scr[p0:p0 + nm, :].astype(BF16)
        y = lax.dot_general(slab, w_ref[...], (((0,), (0,)), ((), ())), preferred_element_type=F32)
        rows = slice(j * n1, (j + 1) * n1)
        o_ref[rows, :] = s_ref[rows, :] + gate * y


def _fn_out(ft, s, mods, w_out):
    t, d = s.shape
    nm, n2, n1 = ft.shape
    nj = FN_TB // n1
    tok = pl.BlockSpec((FN_TB, d), lambda i: (i, 0))
    return pl.pallas_call(
        _fn_out_kernel,
        out_shape=jax.ShapeDtypeStruct((t, d), F32),
        grid=(t // FN_TB,),
        in_specs=[pl.BlockSpec((nm, nj, n1), lambda i: (0, i, 0)), _full(w_out.shape), tok,
                  _full(mods.shape)],
        out_specs=tok,
        scratch_shapes=[pltpu.VMEM((nj * _slab_pitch(nm), n1), F32)],
        compiler_params=_cparams("arbitrary"),
        name="fourier_out",
    )(ft, w_out.astype(BF16), s, mods)


def _fourier_layer(s, mods, norm_g, w_in, w_out):
    yr, yi = _fn_proj(s, mods, norm_g, w_in)
    return _fn_out(_fn_fft(yr, yi), s, mods, w_out)


def kernel(x, c, ctx, c_ctx, ada_w, ada_b, norm_mix_g, norm_ffn_g, final_norm_g, router_group_w, router_group_b, router_expert_w, router_expert_b, expert_w_gate, expert_w_up, expert_w_down, cm_w_in, cm_v_norm_g, cm_w_s, cm_b_s, cm_w_out, ml_w_in, ml_conv_w, ml_conv_b, ml_gate_b, ml_norm_g, ml_w_out, lru_w_in, lru_conv_w, lru_conv_b, lru_w_a, lru_b_a, lru_w_x, lru_b_x, lru_lambda, lru_w_out, fn_w_in, fn_w_out):
    bsz, seq, d = x.shape
    assert bsz == 1 and ada_w.shape[0] == 4 and ctx.shape[1] == TM
    c_rows = jnp.concatenate([c_ctx[None, :], c, jnp.zeros((SUBLANES - 2, d), F32)], axis=0)
    mods = _ada_table(c_rows, ada_w, ada_b)

    def moe(s, i, ctx_tiles, final_norm=False, defer=False):
        return _moe_layer(s, mods[i], norm_ffn_g[i], router_group_w[i], router_group_b[i],
                          router_expert_w[i], router_expert_b[i], expert_w_gate, expert_w_up,
                          expert_w_down, i, ctx_tiles, final_norm_g, final_norm, defer)

    s = _chunk_mlp_layer(x[0], ctx[0], mods[0], norm_mix_g[0], cm_w_in[0], cm_v_norm_g[0],
                         cm_w_s[0], cm_b_s[0], cm_w_out[0])
    s = moe(s, 0, 1, defer=True)
    s = _mlstm_layer(s, mods[1], norm_mix_g[1], ml_w_in[0], ml_conv_w[0], ml_conv_b[0],
                     ml_gate_b[0], ml_norm_g[0], ml_w_out[0])
    s = moe(s, 1, 1, defer=True)
    s = _rglru_layer(s, mods[2], norm_mix_g[2], lru_w_in[0], lru_conv_w[0], lru_conv_b[0],
                     lru_w_a[0], lru_b_a[0], lru_w_x[0], lru_b_x[0], lru_lambda[0], lru_w_out[0])
    s = moe(s, 2, 0)
    s = _fourier_layer(s, mods[3], norm_mix_g[3], fn_w_in[0], fn_w_out[0])
    s = moe(s, 3, 0, final_norm=True)
    return s[None]
```

```python
import functools
import math
from typing import NamedTuple

import jax
import jax.numpy as jnp
import numpy as np
from jax import lax
from jax.experimental import pallas as pl
from jax.experimental.pallas import tpu as pltpu

F32 = jnp.float32
BF16 = jnp.bfloat16

EPS = 1e-6
POS_BASE = 10000.0
GRID_W = 64
N_MOD = 6
TM = 256
LANES = 128
SUBLANES = 8
VMEM_LIMIT = 56 * 1024 * 1024

CM_CHUNK = 128
CM_GROUPS = 4
ML_HEADS = 4
ML_DK = 128
ML_DV = 256
ML_CHUNK = 128
LRU_HEADS = 10
LRU_BLOCK = 128
LRU_C = 8.0
FN_GROUPS = 4
FFT_N2 = 128
MOE_GROUPS = 4
MOE_EPG = 8
MOE_EXPERTS = MOE_GROUPS * MOE_EPG
MOE_ROWS_LOG2 = 8
MOE_ROWS = 1 << MOE_ROWS_LOG2
ROUTE_ROWS = 40
FN_TB = 1024
FN_CB = 32
CONV_LEFT = 2

HI = lax.Precision.HIGHEST


def _cparams(*sem):
    return pltpu.CompilerParams(dimension_semantics=sem, vmem_limit_bytes=VMEM_LIMIT)


def _full(shape):
    nd = len(shape)
    return pl.BlockSpec(shape, lambda *_: (0,) * nd)


def _rms(x, g):
    return x * lax.rsqrt(jnp.mean(x * x, axis=-1, keepdims=True) + EPS) * g


def _gelu(x):
    c = math.sqrt(2.0 / math.pi)
    return 0.5 * x * (1.0 + jnp.tanh(c * (x + 0.044715 * (x * x * x))))


def _sigmoid(x):
    return 0.5 * jnp.tanh(0.5 * x) + 0.5


def _silu(x):
    return x * _sigmoid(x)


def _softplus(x):
    return jnp.maximum(x, 0.0) + jnp.log(1.0 + jnp.exp(-jnp.abs(x)))


def _mod_rows(mod_ref, tile, ctx_tiles, first):
    row = jnp.where(tile < ctx_tiles, 0, 1)
    m = mod_ref[pl.ds(row, 1), :]
    d = m.shape[1] // N_MOD
    return tuple(m[:, (first + j) * d:(first + j + 1) * d] for j in range(3))


def _bdot(a, b):
    return jnp.dot(a.astype(BF16), b.astype(BF16), preferred_element_type=F32)


def _split_bf16(x):
    hi = x.astype(BF16)
    return hi, (x - hi.astype(F32)).astype(BF16)


def _dot3(a, b, dims):
    a_hi, a_lo = _split_bf16(a)
    b_hi, b_lo = _split_bf16(b)
    dg = functools.partial(lax.dot_general, dimension_numbers=(dims, ((), ())),
                           preferred_element_type=F32)
    return dg(a_hi, b_hi) + dg(a_hi, b_lo) + dg(a_lo, b_hi)


def _ada_kernel(c_ref, w_ref, b_ref, o_ref):
    c = c_ref[...]
    o_ref[...] = _dot3(_silu(c), w_ref[...], ((1,), (0,))) + b_ref[...]


def _ada_table(c_rows, ada_w, ada_b):
    depth, d, n = ada_w.shape
    tn = 3072
    return pl.pallas_call(
        _ada_kernel,
        out_shape=jax.ShapeDtypeStruct((depth, SUBLANES, n), F32),
        grid=(depth, n // tn),
        in_specs=[_full((SUBLANES, d)),
                  pl.BlockSpec((None, d, tn), lambda i, j: (i, 0, j)),
                  pl.BlockSpec((None, 1, tn), lambda i, j: (i, 0, j))],
        out_specs=pl.BlockSpec((None, SUBLANES, tn), lambda i, j: (i, 0, j)),
        compiler_params=_cparams("arbitrary", "arbitrary"),
        name="ada_table",
    )(c_rows, ada_w, ada_b.reshape(depth, 1, n))


def _pos_tables(seq, d):
    q = d // 4
    freq = jnp.exp(-math.log(POS_BASE) * jnp.arange(q, dtype=F32) / q)
    ar = jnp.arange(seq // GRID_W, dtype=F32)[:, None] * freq
    ac = jnp.arange(GRID_W, dtype=F32)[:, None] * freq
    return (jnp.concatenate([jnp.sin(ar), jnp.cos(ar)], axis=-1),
            jnp.concatenate([jnp.sin(ac), jnp.cos(ac)], axis=-1))


def _cm_kernel(x_ref, ctx_ref, rt_ref, ct_ref, mod_ref, g_ref, win_ref, vg_ref, ws_ref, bs_ref,
               wout_ref, o_ref, p_scr, x_scr):
    i = pl.program_id(0)

    @pl.when(i == 0)
    def _():
        x_scr[...] = ctx_ref[...]

    @pl.when(i > 0)
    def _():
        rows_per_tile = TM // GRID_W
        q2 = rt_ref.shape[1]
        r0 = (i - 1) * rows_per_tile
        rt = jnp.concatenate(
            [jnp.broadcast_to(rt_ref[pl.ds(r0 + j, 1), :], (GRID_W, q2))
             for j in range(rows_per_tile)], axis=0)
        ct = jnp.concatenate([ct_ref[...]] * rows_per_tile, axis=0)
        x_scr[...] = x_ref[...] + jnp.concatenate([rt, ct], axis=1)

    shift, scale, gate = _mod_rows(mod_ref, i, 1, 0)
    x = x_scr[...]
    h = _rms(x, g_ref[...]) * (1.0 + scale) + shift
    z = _gelu(_bdot(h, win_ref[...]))
    w = z.shape[1] // 2
    u = z[:, :w]
    v = _rms(z[:, w:], vg_ref[...]).astype(BF16)
    gw = w // CM_GROUPS
    for c in range(TM // CM_CHUNK):
        r = slice(c * CM_CHUNK, (c + 1) * CM_CHUNK)
        for g in range(CM_GROUPS):
            cs = slice(g * gw, (g + 1) * gw)
            s = jnp.dot(ws_ref[g], v[r, cs], preferred_element_type=F32) + bs_ref[:, g:g + 1]
            p_scr[r, cs] = (u[r, cs] * s).astype(BF16)
    y = jnp.dot(p_scr[...], wout_ref[...], preferred_element_type=F32)
    o_ref[...] = x + gate * y


def _chunk_mlp_layer(x2, ctx2, mods, norm_g, w_in, v_g, w_s, b_s, w_out):
    seq, d = x2.shape
    n_ctx = ctx2.shape[0]
    assert n_ctx == TM and seq % TM == 0 and TM % GRID_W == 0
    n = n_ctx + seq
    w = w_out.shape[0]
    rt, ct = _pos_tables(seq, d)
    return pl.pallas_call(
        _cm_kernel,
        out_shape=jax.ShapeDtypeStruct((n, d), F32),
        grid=(n // TM,),
        in_specs=[pl.BlockSpec((TM, d), lambda i: (jnp.maximum(i - 1, 0), 0)),
                  _full((TM, d)), _full(rt.shape), _full(ct.shape),
                  _full(mods.shape), _full((1, d)), _full(w_in.shape), _full((1, w)),
                  _full(w_s.shape), _full((CM_CHUNK, CM_GROUPS)), _full(w_out.shape)],
        out_specs=pl.BlockSpec((TM, d), lambda i: (i, 0)),
        scratch_shapes=[pltpu.VMEM((TM, w), BF16), pltpu.VMEM((TM, d), F32)],
        compiler_params=_cparams("arbitrary"),
        name="chunk_mlp",
    )(x2, ctx2, rt, ct, mods, norm_g.reshape(1, d), w_in.astype(BF16), v_g.reshape(1, w),
      w_s.astype(BF16), b_s.T, w_out.astype(BF16))


def _store_token_tiles(ref, x):
    rows, d = x.shape
    for j in range(d // LANES):
        ref[pl.ds(j, rows, stride=d // LANES), :] = x[:, j * LANES:(j + 1) * LANES]


def _load_token_tiles(ref):
    chunks = SUBLANES
    rows = ref.shape[0] // chunks
    return jnp.concatenate([ref[pl.ds(j, rows, stride=chunks), :] for j in range(chunks)], axis=1)


def _route_tiles(nt):
    return next(k for k in (5, 4, 2, 1) if nt % k == 0)


def _router_kernel(s_ref, mod_ref, g_ref, rwt_ref, rbt_ref, tri_ref, h_ref, e1_ref, e2_ref, r1_ref,
                   r2_ref, wt_ref, cnt_ref, carry, *, ctx_rows):
    i = pl.program_id(0)
    rows, d = s_ref.shape

    @pl.when(i == 0)
    def _():
        carry[...] = jnp.zeros_like(carry)

    lat = mod_ref[1:2, :]
    shift, scale = lat[:, 3 * d:4 * d], lat[:, 4 * d:5 * d]
    if ctx_rows:
        ctx = mod_ref[0:1, :]
        is_ctx = (i == 0) & (lax.broadcasted_iota(jnp.int32, (rows, 1), 0) < ctx_rows)
        shift = jnp.where(is_ctx, ctx[:, 3 * d:4 * d], shift)
        scale = jnp.where(is_ctx, ctx[:, 4 * d:5 * d], scale)
    h = _rms(s_ref[...], g_ref[...]) * (1.0 + scale) + shift
    _store_token_tiles(h_ref, h)
    logits = _dot3(rwt_ref[...], h, ((1,), (1,))) + rbt_ref[...]
    row = lax.broadcasted_iota(jnp.int32, logits.shape, 0)
    neg = jnp.float32(-jnp.inf)
    big = jnp.int32(1 << 20)
    is_g = row < MOE_GROUPS
    gl = jnp.where(is_g, logits, neg)
    gmax = jnp.max(gl, axis=0, keepdims=True)
    grp = jnp.min(jnp.where(is_g & (gl == gmax), row, big), axis=0, keepdims=True)
    p_grp = 1.0 / jnp.sum(jnp.exp(gl - gmax), axis=0, keepdims=True)
    e_row = row - MOE_GROUPS
    in_grp = (e_row >= 0) & (e_row < MOE_EXPERTS) & ((e_row >> 3) == grp)
    l1 = jnp.where(in_grp, logits, neg)
    v1 = jnp.max(l1, axis=0, keepdims=True)
    i1 = jnp.min(jnp.where(in_grp & (l1 == v1), row, big), axis=0, keepdims=True)
    rest = in_grp & (row != i1)
    l2 = jnp.where(rest, logits, neg)
    v2 = jnp.max(l2, axis=0, keepdims=True)
    i2 = jnp.min(jnp.where(rest & (l2 == v2), row, big), axis=0, keepdims=True)
    e21 = jnp.exp(v2 - v1)
    w1 = p_grp / (1.0 + e21)
    w2 = p_grp * e21 / (1.0 + e21)
    oh1 = (row == i1).astype(F32)
    oh2 = (row == i2).astype(F32)
    oh = oh1 + oh2
    before = jnp.dot(oh.astype(BF16), tri_ref[...], preferred_element_type=F32) + carry[:, 0:1]
    r1_ref[...] = jnp.sum(oh1 * before, axis=0, keepdims=True).astype(jnp.int32)
    r2_ref[...] = jnp.sum(oh2 * before, axis=0, keepdims=True).astype(jnp.int32)
    e1_ref[...] = i1 - MOE_GROUPS
    e2_ref[...] = i2 - MOE_GROUPS
    carry[...] = carry[...] + jnp.sum(oh, axis=1, keepdims=True)
    cnt_ref[...] = carry[...]
    wt_ref[...] = jnp.concatenate([w1, w2, jnp.zeros((LANES - 2, rows), F32)], axis=0).T


def _router(s, mods, norm_g, rg_w, rg_b, re_w, re_b, ctx_tiles):
    n, d = s.shape
    rows = TM * _route_tiles(n // TM)
    steps = n // rows
    pad = ROUTE_ROWS - MOE_GROUPS - MOE_EXPERTS
    rwt = jnp.concatenate([rg_w, re_w, jnp.zeros((d, pad), F32)], axis=1).T
    rbt = jnp.broadcast_to(jnp.concatenate([rg_b, re_b, jnp.zeros((pad,), F32)])[:, None],
                           (ROUTE_ROWS, rows))
    tri = jnp.asarray(np.triu(np.ones((rows, rows), np.float32), 1), BF16)
    assert d == SUBLANES * LANES
    tile = pl.BlockSpec((rows, d), lambda i: (i, 0))
    irow = pl.BlockSpec((None, 1, rows), lambda i: (i, 0, 0))
    ishape = jax.ShapeDtypeStruct((steps, 1, rows), jnp.int32)
    return pl.pallas_call(
        functools.partial(_router_kernel, ctx_rows=ctx_tiles * TM),
        out_shape=(jax.ShapeDtypeStruct((n * SUBLANES, LANES), F32), ishape, ishape, ishape, ishape,
                   jax.ShapeDtypeStruct((n, LANES), F32),
                   jax.ShapeDtypeStruct((ROUTE_ROWS, LANES), F32)),
        grid=(steps,),
        in_specs=[tile, _full(mods.shape), _full((1, d)), _full((ROUTE_ROWS, d)),
                  _full((ROUTE_ROWS, rows)), _full((rows, rows))],
        out_specs=(pl.BlockSpec((rows * SUBLANES, LANES), lambda i: (i, 0)), irow, irow, irow, irow,
                   pl.BlockSpec((rows, LANES), lambda i: (i, 0)), _full((ROUTE_ROWS, LANES))),
        scratch_shapes=[pltpu.VMEM((ROUTE_ROWS, LANES), F32)],
        compiler_params=_cparams("arbitrary"),
        name="moe_router",
    )(s, mods, norm_g.reshape(1, d), rwt, rbt, tri)


def _finalize_kernel(cnt_ref, e1_ref, e2_ref, r1_ref, r2_ref, d1_ref, d2_ref, blk_ref):
    e1 = e1_ref[...]
    e2 = e2_ref[...]
    r1 = r1_ref[...]
    r2 = r2_ref[...]
    d1 = jnp.zeros_like(e1)
    d2 = jnp.zeros_like(e2)
    lane = lax.broadcasted_iota(jnp.int32, blk_ref.shape, 1)
    brow = lane * MOE_ROWS
    sub = lax.broadcasted_iota(jnp.int32, blk_ref.shape, 0)
    be = jnp.zeros(blk_ref.shape, jnp.int32)
    pend = jnp.zeros(blk_ref.shape, jnp.int32)
    ps = jnp.int32(0)
    for e in range(MOE_EXPERTS):
        c = cnt_ref[e]
        pe = ps + lax.shift_left(lax.shift_right_logical(c + (MOE_ROWS - 1), MOE_ROWS_LOG2),
                                 MOE_ROWS_LOG2)
        d1 = jnp.where(e1 == e, ps + r1, d1)
        d2 = jnp.where(e2 == e, ps + r2, d2)
        be = be + (brow >= pe).astype(jnp.int32)
        pend = jnp.where(lane == e, pe, pend)
        ps = pe
    d1_ref[...] = d1
    d2_ref[...] = d2
    n_used = lax.shift_right_logical(ps, MOE_ROWS_LOG2)
    blk_ref[...] = jnp.where(sub == 0, jnp.minimum(be, MOE_EXPERTS - 1),
                             jnp.where(sub == 1, pend, n_used))


def _finalize(counts, e1, e2, r1, r2, nb):
    nbp = (nb + LANES - 1) // LANES * LANES
    whole = pl.BlockSpec(e1.shape, lambda i, c: (0, 0, 0))
    ishape = jax.ShapeDtypeStruct(e1.shape, jnp.int32)
    return pl.pallas_call(
        _finalize_kernel,
        out_shape=(ishape, ishape, jax.ShapeDtypeStruct((SUBLANES, nbp), jnp.int32)),
        grid_spec=pltpu.PrefetchScalarGridSpec(
            num_scalar_prefetch=1,
            grid=(1,),
            in_specs=[whole, whole, whole, whole],
            out_specs=(whole, whole, pl.BlockSpec((SUBLANES, nbp), lambda i, c: (0, 0)))),
        compiler_params=_cparams("arbitrary"),
        name="moe_finalize",
    )(counts, e1, e2, r1, r2)


def _token_copy(src, r, dst, d, sem):
    return pltpu.make_async_copy(src.at[pl.ds(pl.multiple_of(r * SUBLANES, SUBLANES), SUBLANES), :],
                                 dst.at[pl.ds(pl.multiple_of(d * SUBLANES, SUBLANES), SUBLANES), :],
                                 sem)


def _zero_fill_padding(pend_ref, nu_ref, xs_out, zbuf, zsem):
    blk_rows = MOE_ROWS * SUBLANES
    nb = xs_out.shape[0] // blk_rows
    zbuf[...] = jnp.zeros_like(zbuf)

    def block_copy(b):
        r0 = pl.multiple_of(b * blk_rows, blk_rows)
        return pltpu.make_async_copy(zbuf, xs_out.at[pl.ds(r0, blk_rows), :], zsem)

    def seg_last_block(e):
        pe = pend_ref[e]
        prev = pend_ref[e - 1] if e > 0 else 0
        return pe > prev, lax.shift_right_logical(pe, MOE_ROWS_LOG2) - 1

    for e in range(MOE_EXPERTS):
        nonempty, b = seg_last_block(e)

        @pl.when(nonempty)
        def _():
            block_copy(b).start()

    def tail_start(b, c):
        block_copy(b).start()
        return c

    lax.fori_loop(nu_ref[0], nb, tail_start, 0)
    for e in range(MOE_EXPERTS):
        nonempty, b = seg_last_block(e)

        @pl.when(nonempty)
        def _():
            block_copy(b).wait()

    def tail_wait(b, c):
        block_copy(b).wait()
        return c

    lax.fori_loop(nu_ref[0], nb, tail_wait, 0)


def _dispatch_kernel(dest_ref, pend_ref, nu_ref, h_ref, xs_out, sem, zbuf, zsem, *, tiles):
    i = pl.program_id(0)

    @pl.when(i == 0)
    def _():
        _zero_fill_padding(pend_ref, nu_ref, xs_out, zbuf, zsem)

    rows = tiles * TM
    for q in range(tiles):
        base = i * (2 * rows) + q * TM

        def start(r, c):
            _token_copy(h_ref, q * TM + r, xs_out, dest_ref[base + r], sem).start(priority=0)
            _token_copy(h_ref, q * TM + r, xs_out, dest_ref[base + rows + r],
                        sem).start(priority=1)
            return c

        lax.fori_loop(0, TM, start, 0, unroll=8)
    for _ in range(2):
        pltpu.make_async_copy(h_ref, xs_out.at[pl.ds(0, tiles * TM * SUBLANES), :], sem).wait()


def _dispatch(dest, pad_end, n_used, h, n_rows):
    n = h.shape[0] // SUBLANES
    nt = n // TM
    tiles = _route_tiles(nt)
    return pl.pallas_call(
        functools.partial(_dispatch_kernel, tiles=tiles),
        out_shape=jax.ShapeDtypeStruct((n_rows * SUBLANES, LANES), F32),
        grid_spec=pltpu.PrefetchScalarGridSpec(
            num_scalar_prefetch=3,
            grid=(nt // tiles,),
            in_specs=[pl.BlockSpec((tiles * TM * SUBLANES, LANES), lambda i, *_: (i, 0))],
            out_specs=pl.BlockSpec(memory_space=pl.ANY),
            scratch_shapes=[pltpu.SemaphoreType.DMA, pltpu.VMEM((MOE_ROWS * SUBLANES, LANES), F32),
                            pltpu.SemaphoreType.DMA]),
        compiler_params=_cparams("arbitrary"),
        name="moe_dispatch",
    )(dest, pad_end, n_used, h)


X_SLOTS = 4


Y_SLOTS = 3


def _expert_kernel(be_ref, pend_ref, nu_ref, xs_hbm, wg_hbm, wu_hbm, wd_hbm, ys_hbm,
                   xbuf, ybuf, wg_f, wu_f, wd_f, wg_s, wu_s, wd_s, xsem, ysem, wsem, ord_ref,
                   *, layer):
    nu = nu_ref[0]
    blk_rows = MOE_ROWS * SUBLANES
    nb = ys_hbm.shape[0] // blk_rows

    def rows_of(ref, blk):
        return ref.at[pl.ds(pl.multiple_of(blk * blk_rows, blk_rows), blk_rows), :]

    def x_copy(blk, slot):
        return pltpu.make_async_copy(rows_of(xs_hbm, blk), xbuf.at[slot], xsem.at[slot])

    def y_copy(blk, slot):
        return pltpu.make_async_copy(ybuf.at[slot], rows_of(ys_hbm, blk), ysem.at[slot])

    def w_copies(e, slot):
        return (pltpu.make_async_copy(wg_hbm.at[layer, e], wg_f.at[slot], wsem.at[slot]),
                pltpu.make_async_copy(wu_hbm.at[layer, e], wu_f.at[slot], wsem.at[slot]),
                pltpu.make_async_copy(wd_hbm.at[layer, e], wd_f.at[slot], wsem.at[slot]))

    ord_ref[0] = 0
    for j in range(X_SLOTS - 1):
        @pl.when(j < nu)
        def _():
            x_copy(j, j).start()

    @pl.when(nu > 0)
    def _():
        for c in w_copies(be_ref[0], 0):
            c.start()

    def block(b, carry):
        ahead = b + (X_SLOTS - 1)

        @pl.when(ahead < nu)
        def _():
            x_copy(ahead, lax.rem(ahead, X_SLOTS)).start()

        e = be_ref[b]

        @pl.when((b == 0) | (e != be_ref[jnp.maximum(b - 1, 0)]))
        def _():
            k = ord_ref[0]
            slot = lax.rem(k, 2)
            for c in w_copies(e, slot):
                c.wait()
            wg_s[...] = wg_f[slot].astype(BF16)
            wu_s[...] = wu_f[slot].astype(BF16)
            wd_s[...] = wd_f[slot].astype(BF16)
            nxt = lax.shift_right_logical(pend_ref[e], MOE_ROWS_LOG2)

            @pl.when(nxt < nu)
            def _():
                for c in w_copies(be_ref[nxt], 1 - slot):
                    c.start(priority=1)

            ord_ref[0] = k + 1

        slot = lax.rem(b, X_SLOTS)
        yslot = lax.rem(b, Y_SLOTS)
        x_copy(b, slot).wait()

        @pl.when(b >= Y_SLOTS)
        def _():
            y_copy(b - Y_SLOTS, yslot).wait()

        x = _load_token_tiles(xbuf.at[slot]).astype(BF16)
        a = jnp.dot(x, wg_s[...], preferred_element_type=F32)
        u = jnp.dot(x, wu_s[...], preferred_element_type=F32)
        _store_token_tiles(ybuf.at[yslot], jnp.dot((_silu(a) * u).astype(BF16), wd_s[...],
                                                   preferred_element_type=F32))
        y_copy(b, yslot).start()
        return carry

    lax.fori_loop(0, nu, block, 0)
    for j in range(1, Y_SLOTS + 1):
        @pl.when(nu >= j)
        def _():
            y_copy(nu - j, lax.rem(nu - j, Y_SLOTS)).wait()

    ybuf[0] = jnp.zeros(ybuf.shape[1:], ybuf.dtype)

    def tail_start(b, c):
        y_copy(b, 0).start()
        return c

    def tail_wait(b, c):
        y_copy(b, 0).wait()
        return c

    lax.fori_loop(nu, nb, tail_start, 0)
    lax.fori_loop(nu, nb, tail_wait, 0)


def _experts(blk_expert, pad_end, n_used, xs, w_gate, w_up, w_down, layer):
    d, hid = w_gate.shape[2:]
    blk_rows = MOE_ROWS * SUBLANES
    hbm = pl.BlockSpec(memory_space=pl.ANY)
    return pl.pallas_call(
        functools.partial(_expert_kernel, layer=layer),
        out_shape=jax.ShapeDtypeStruct(xs.shape, F32),
        grid_spec=pltpu.PrefetchScalarGridSpec(
            num_scalar_prefetch=3,
            grid=(1,),
            in_specs=[hbm, hbm, hbm, hbm],
            out_specs=hbm,
            scratch_shapes=[pltpu.VMEM((X_SLOTS, blk_rows, LANES), F32),
                            pltpu.VMEM((Y_SLOTS, blk_rows, LANES), F32),
                            pltpu.VMEM((2, d, hid), F32), pltpu.VMEM((2, d, hid), F32),
                            pltpu.VMEM((2, hid, d), F32),
                            pltpu.VMEM((d, hid), BF16), pltpu.VMEM((d, hid), BF16),
                            pltpu.VMEM((hid, d), BF16),
                            pltpu.SemaphoreType.DMA((X_SLOTS,)), pltpu.SemaphoreType.DMA((Y_SLOTS,)),
                            pltpu.SemaphoreType.DMA((2,)), pltpu.SMEM((1,), jnp.int32)]),
        compiler_params=_cparams("arbitrary"),
        name="moe_experts",
    )(blk_expert, pad_end, n_used, xs, w_gate, w_up, w_down)


def _combined_tile(dest_ref, s_ref, wt_ref, mod_ref, ys_ref, ybuf, sem, *, ctx_tiles, tiles,
                   inline_prefetch=False):
    i = pl.program_id(0)
    slot = i % 2
    rows = tiles * TM

    def gather(tile, slot, unroll=8):
        tile = jnp.asarray(tile, jnp.int32)
        base = lax.div(tile, tiles) * (2 * rows) + lax.rem(tile, tiles) * TM

        def start(r, c):
            _token_copy(ys_ref, dest_ref[base + r], ybuf.at[slot, 0], r,
                        sem.at[slot]).start(priority=0)
            _token_copy(ys_ref, dest_ref[base + rows + r], ybuf.at[slot, 1], r,
                        sem.at[slot]).start(priority=1)
            return c

        lax.fori_loop(0, TM, start, 0, unroll=unroll)

    def retire(slot):
        for k in range(2):
            pltpu.make_async_copy(ys_ref.at[pl.ds(0, TM * SUBLANES), :], ybuf.at[slot, k],
                                  sem.at[slot]).wait()

    last = pl.num_programs(0) - 1

    @pl.when(i == 0)
    def _():
        gather(0, 0)

    if inline_prefetch:
        gather(jnp.minimum(i + 1, last), 1 - slot, unroll=True)
    else:
        @pl.when(i < last)
        def _():
            gather(i + 1, 1 - slot)

    retire(slot)
    if inline_prefetch:
        @pl.when(i == last)
        def _():
            retire(1 - slot)

    gate = _mod_rows(mod_ref, i, ctx_tiles, 3)[2]
    wt = wt_ref[...]
    y = (wt[:, 0:1] * _load_token_tiles(ybuf.at[slot, 0])
         + wt[:, 1:2] * _load_token_tiles(ybuf.at[slot, 1]))
    return s_ref[...] + gate * y


def _combine_scratch():
    return [pltpu.VMEM((2, 2, TM * SUBLANES, LANES), F32), pltpu.SemaphoreType.DMA((2,))]


def _combine_kernel(dest_ref, s_ref, wt_ref, mod_ref, fg_ref, ys_ref, o_ref, ybuf, sem,
                    *, final_norm, **kw):
    out = _combined_tile(dest_ref, s_ref, wt_ref, mod_ref, ys_ref, ybuf, sem, **kw)
    if final_norm:
        out = _rms(out, fg_ref[...])
    o_ref[...] = out


def _combine(dest, s, wts, mods, final_g, ys, ctx_tiles, final_norm):
    n, d = s.shape
    return pl.pallas_call(
        functools.partial(_combine_kernel, ctx_tiles=ctx_tiles, final_norm=final_norm,
                          tiles=_route_tiles(n // TM)),
        out_shape=jax.ShapeDtypeStruct((n, d), F32),
        grid_spec=pltpu.PrefetchScalarGridSpec(
            num_scalar_prefetch=1,
            grid=(n // TM,),
            in_specs=[pl.BlockSpec((TM, d), lambda i, dst: (i, 0)),
                      pl.BlockSpec((TM, LANES), lambda i, dst: (i, 0)),
                      pl.BlockSpec(mods.shape, lambda i, dst: (0, 0)),
                      pl.BlockSpec((1, d), lambda i, dst: (0, 0)),
                      pl.BlockSpec(memory_space=pl.ANY)],
            out_specs=pl.BlockSpec((TM, d), lambda i, dst: (i, 0)),
            scratch_shapes=_combine_scratch()),
        compiler_params=_cparams("arbitrary"),
        name="moe_combine",
    )(dest, s, wts, mods, final_g.reshape(1, d), ys)


def _moe_layer(s, mods, norm_g, rg_w, rg_b, re_w, re_b, w_gate, w_up, w_down, layer, ctx_tiles,
               final_g, final_norm, defer=False):
    n, d = s.shape
    h, e1, e2, r1, r2, wts, cnt = _router(s, mods, norm_g, rg_w, rg_b, re_w, re_b, ctx_tiles)
    counts = cnt[MOE_GROUPS:MOE_GROUPS + MOE_EXPERTS, 0].astype(jnp.int32)
    nb = (2 * n + MOE_EXPERTS * (MOE_ROWS - 1)) // MOE_ROWS + 1
    d1, d2, blk = _finalize(counts, e1, e2, r1, r2, nb)
    dest = jnp.concatenate([d1, d2], axis=1).reshape(2 * n)
    n_used = blk[2, :1]
    pad_end = blk[1, :MOE_EXPERTS]
    xs = _dispatch(dest, pad_end, n_used, h, nb * MOE_ROWS)
    ys = _experts(blk[0, :nb], pad_end, n_used, xs, w_gate, w_up, w_down, layer)
    if defer:
        return _PendingCombine(dest, s, wts, mods, ys, ctx_tiles)
    return _combine(dest, s, wts, mods, final_g, ys, ctx_tiles, final_norm)


class _PendingCombine(NamedTuple):
    dest: jax.Array
    s: jax.Array
    wts: jax.Array
    mods: jax.Array
    ys: jax.Array
    ctx_tiles: int


def _pending_operands(p):
    n, d = p.s.shape
    kw = dict(ctx_tiles=p.ctx_tiles, tiles=_route_tiles(n // TM), inline_prefetch=True)
    specs = [pl.BlockSpec((TM, d), lambda i, dst: (i, 0)),
             pl.BlockSpec((TM, LANES), lambda i, dst: (i, 0)),
             pl.BlockSpec(p.mods.shape, lambda i, dst: (0, 0)),
             pl.BlockSpec(memory_space=pl.ANY)]
    return kw, p.dest, specs, (p.s, p.wts, p.mods, p.ys), _combine_scratch()


def _conv_tile(x, prev_ref, next_ref, has_prev, has_next, w_ref, b_ref):
    rows = x.shape[0]
    S = SUBLANES
    ridx = lax.broadcasted_iota(jnp.int32, (S, x.shape[1]), 0)
    pm = jnp.where(has_prev, 1.0, 0.0)
    nm = jnp.where(has_next, 1.0, 0.0)
    p2 = prev_ref[S - 2:S - 1, :] * pm
    p1 = prev_ref[S - 1:S, :] * pm
    n1 = next_ref[0:1, :] * nm

    def fix_head(rolled, head):
        return jnp.concatenate([head(rolled[:S]), rolled[S:]], axis=0)

    xm1 = fix_head(pltpu.roll(x, 1, axis=0), lambda g: jnp.where(ridx == 0, p1, g))
    xm2 = fix_head(pltpu.roll(x, 2, axis=0),
                   lambda g: jnp.where(ridx == 0, p2, jnp.where(ridx == 1, p1, g)))
    xp1 = pltpu.roll(x, rows - 1, axis=0)
    xp1 = jnp.concatenate([xp1[:rows - S], jnp.where(ridx == S - 1, n1, xp1[rows - S:])], axis=0)
    return (xm2 * w_ref[0:1, :] + xm1 * w_ref[1:2, :] + x * w_ref[2:3, :]
            + xp1 * w_ref[3:4, :] + b_ref[...])


def _ml_proj_kernel(dest_ref, s_ref, wt_ref, pmod_ref, ys_ref, mod_ref, g_ref, w_ref, wg_ref,
                    gb_ref, snew_ref, qk_ref, v_ref, o_ref, gt_ref, ybuf, sem, **pending_kw):
    i = pl.program_id(0)
    x = _combined_tile(dest_ref, s_ref, wt_ref, pmod_ref, ys_ref, ybuf, sem, **pending_kw)
    snew_ref[...] = x
    shift, scale, _ = _mod_rows(mod_ref, i, 1, 0)
    h = _rms(x, g_ref[...]) * (1.0 + scale) + shift
    z = _bdot(h, w_ref[...])
    nqk = qk_ref.shape[1]
    nv = v_ref.shape[1]
    qk_ref[...] = z[:, :nqk]
    v_ref[...] = z[:, nqk:nqk + nv].astype(BF16)
    o_ref[...] = z[:, nqk + nv:].astype(BF16)
    pre = _dot3(h, wg_ref[...], ((1,), (0,))) + gb_ref[...]
    lane = lax.broadcasted_iota(jnp.int32, pre.shape, 1)
    is_forget = ((lane >> 2) & 1) == 1
    gt_ref[...] = jnp.where(is_forget, -_softplus(-pre), pre)


def _ml_proj(pending, mods, norm_g, w_in, gate_b):
    n, d = pending.s.shape
    kw, dest, p_specs, p_args, p_scratch = _pending_operands(pending)
    nqk = 2 * ML_HEADS * ML_DK
    nv = ML_HEADS * ML_DV
    n_main = nqk + 2 * nv
    n_gate = w_in.shape[1] - n_main
    w_main = w_in[:, :n_main].astype(BF16)
    w_gate = jnp.concatenate([w_in[:, n_main:], jnp.zeros((d, LANES - n_gate), F32)], axis=1)
    gb = jnp.concatenate([gate_b.reshape(n_gate), jnp.zeros((LANES - n_gate,), F32)]).reshape(1, LANES)
    tile = lambda w: pl.BlockSpec((TM, w), lambda i, dst: (i, 0))
    whole = lambda shape: pl.BlockSpec(shape, lambda i, dst: (0,) * len(shape))
    return pl.pallas_call(
        functools.partial(_ml_proj_kernel, **kw),
        out_shape=(jax.ShapeDtypeStruct((n, d), F32),
                   jax.ShapeDtypeStruct((n, nqk), F32), jax.ShapeDtypeStruct((n, nv), BF16),
                   jax.ShapeDtypeStruct((n, nv), BF16), jax.ShapeDtypeStruct((n, LANES), F32)),
        grid_spec=pltpu.PrefetchScalarGridSpec(
            num_scalar_prefetch=1,
            grid=(n // TM,),
            in_specs=p_specs + [whole(mods.shape), whole((1, d)), whole(w_main.shape),
                                whole((d, LANES)), whole((1, LANES))],
            out_specs=(tile(d), tile(nqk), tile(nv), tile(nv), tile(LANES)),
            scratch_shapes=p_scratch),
        compiler_params=_cparams("arbitrary"),
        name="mlstm_proj",
    )(dest, *p_args, mods, norm_g.reshape(1, d), w_main, w_gate, gb)


def _ml_chunk_index(j, n_chunks, ctx_chunks, reverse):
    if not reverse:
        return j
    return jnp.where(j < ctx_chunks, ctx_chunks - 1 - j, n_chunks - 1 + ctx_chunks - j)


def _ml_rec_kernel(qk_ref, qkp_ref, qkn_ref, v_ref, gt_ref, gtt_ref, cw_ref, cb_ref, o_ref,
                   c_scr, n_scr, m_scr, *, reverse, n_chunks, ctx_chunks):
    j = pl.program_id(0)
    c = _ml_chunk_index(j, n_chunks, ctx_chunks, reverse)

    @pl.when(j == 0)
    def _():
        c_scr[...] = jnp.zeros_like(c_scr)
        n_scr[...] = jnp.zeros_like(n_scr)
        m_scr[...] = jnp.zeros_like(m_scr)

    has_prev = (c != 0) & (c != ctx_chunks)
    has_next = (c != ctx_chunks - 1) & (c != n_chunks - 1)
    qk = _silu(_conv_tile(qk_ref[...], qkp_ref, qkn_ref, has_prev, has_next, cw_ref, cb_ref))
    L = ML_CHUNK
    ri = lax.broadcasted_iota(jnp.int32, (L, L), 0)
    ci = lax.broadcasted_iota(jnp.int32, (L, L), 1)
    past = (ci >= ri) if reverse else (ci <= ri)
    pastf = past.astype(F32)
    gt = gt_ref[...]
    gtt = gtt_ref[...]
    b_col = jnp.dot(pastf, gt, precision=HI, preferred_element_type=F32)
    b_row = jnp.dot(gtt, pastf.T, precision=HI, preferred_element_type=F32)
    last = 0 if reverse else L - 1
    dbase = 8 if reverse else 0
    nq = ML_HEADS * ML_DK
    for hd in range(ML_HEADS):
        cl = dbase + hd
        cf = dbase + 4 + hd
        q = qk[:, hd * ML_DK:(hd + 1) * ML_DK] * (ML_DK ** -0.5)
        k = qk[:, nq + hd * ML_DK:nq + (hd + 1) * ML_DK]
        v = v_ref[:, hd * ML_DV:(hd + 1) * ML_DV]
        li_c = gt[:, cl:cl + 1]
        li_r = gtt[cl:cl + 1, :]
        b_c = b_col[:, cf:cf + 1]
        b_r = b_row[cf:cf + 1, :]
        g = b_r[:, last:last + 1]
        m0 = m_scr[hd:hd + 1, 0:1]
        c0 = c_scr[hd]
        n0 = n_scr[hd:hd + 1, :]
        a_c = g - b_c + li_c
        a_r = g - b_r + li_r
        m_loc = jnp.max(a_r, axis=-1, keepdims=True)
        inter = b_c + m0
        dlog = jnp.where(past, b_c - b_r + li_r, -jnp.inf)
        m = jnp.maximum(inter, jnp.max(dlog, axis=-1, keepdims=True))
        qb = q.astype(BF16)
        sc = lax.dot_general(qb, k.astype(BF16), (((1,), (1,)), ((), ())),
                             preferred_element_type=F32) * jnp.exp(dlog - m)
        w_inter = jnp.exp(inter - m)
        num = (jnp.dot(sc.astype(BF16), v, preferred_element_type=F32)
               + w_inter * jnp.dot(qb, c0.astype(BF16), preferred_element_type=F32))
        den = (jnp.sum(sc, axis=-1, keepdims=True)
               + w_inter * jnp.sum(q * n0, axis=-1, keepdims=True))
        o_ref[:, hd * ML_DV:(hd + 1) * ML_DV] = (
            num / jnp.maximum(jnp.abs(den), jnp.exp(-m))).astype(BF16)
        m_new = jnp.maximum(g + m0, m_loc)
        dec = jnp.exp(g + m0 - m_new)
        scl = jnp.exp(m_loc - m_new)
        kw = k * jnp.exp(a_c - m_loc)
        c_scr[hd] = dec * c0 + scl * jnp.dot(kw.T.astype(BF16), v, preferred_element_type=F32)
        n_scr[hd:hd + 1, :] = dec * n0 + scl * jnp.sum(kw, axis=0, keepdims=True)
        m_scr[hd:hd + 1, :] = jnp.broadcast_to(m_new, (1, LANES))


def _ml_rec(qk, v, gt, gtt, conv_w, conv_b, reverse):
    n, nqk = qk.shape
    nv = v.shape[1]
    L = ML_CHUNK
    nc = n // L
    cc = TM // L
    hb = L // SUBLANES
    idx = lambda j: _ml_chunk_index(j, nc, cc, reverse)
    last8 = n // SUBLANES - 1
    return pl.pallas_call(
        functools.partial(_ml_rec_kernel, reverse=reverse, n_chunks=nc, ctx_chunks=cc),
        out_shape=jax.ShapeDtypeStruct((n, nv), BF16),
        grid=(nc,),
        in_specs=[pl.BlockSpec((L, nqk), lambda j: (idx(j), 0)),
                  pl.BlockSpec((SUBLANES, nqk), lambda j: (jnp.maximum(idx(j) * hb - 1, 0), 0)),
                  pl.BlockSpec((SUBLANES, nqk), lambda j: (jnp.minimum((idx(j) + 1) * hb, last8), 0)),
                  pl.BlockSpec((L, nv), lambda j: (idx(j), 0)),
                  pl.BlockSpec((L, LANES), lambda j: (idx(j), 0)),
                  pl.BlockSpec((2 * SUBLANES, L), lambda j: (0, idx(j))),
                  _full((4, nqk)), _full((1, nqk))],
        out_specs=pl.BlockSpec((L, nv), lambda j: (idx(j), 0)),
        scratch_shapes=[pltpu.VMEM((ML_HEADS, ML_DK, ML_DV), F32),
                        pltpu.VMEM((SUBLANES, ML_DK), F32),
                        pltpu.VMEM((SUBLANES, LANES), F32)],
        compiler_params=_cparams("arbitrary"),
        name="mlstm_rev" if reverse else "mlstm_fwd",
    )(qk, qk, qk, v, gt, gtt, conv_w, conv_b.reshape(1, nqk))


def _ml_out_kernel(hf_ref, hr_ref, o_ref, s_ref, mod_ref, ng_ref, w_ref, out_ref, p_scr):
    i = pl.program_id(0)
    gate = _mod_rows(mod_ref, i, 1, 0)[2]
    hs = hf_ref[...].astype(F32) + hr_ref[...].astype(F32)
    sig = _sigmoid(o_ref[...].astype(F32))
    ng = ng_ref[...]
    for hd in range(ML_HEADS):
        cs = slice(hd * ML_DV, (hd + 1) * ML_DV)
        seg = hs[:, cs]
        hn = seg * lax.rsqrt(jnp.mean(seg * seg, axis=-1, keepdims=True) + EPS) * ng[:, cs]
        p_scr[:, cs] = (hn * sig[:, cs]).astype(BF16)
    y = jnp.dot(p_scr[...], w_ref[...], preferred_element_type=F32)
    out_ref[...] = s_ref[...] + gate * y


def _ml_out(hf, hr, o, s, mods, norm_g, w_out):
    n, d = s.shape
    nv = hf.shape[1]
    tile = lambda w: pl.BlockSpec((TM, w), lambda i: (i, 0))
    return pl.pallas_call(
        _ml_out_kernel,
        out_shape=jax.ShapeDtypeStruct((n, d), F32),
        grid=(n // TM,),
        in_specs=[tile(nv), tile(nv), tile(nv), tile(d), _full(mods.shape), _full((1, nv)),
                  _full(w_out.shape)],
        out_specs=tile(d),
        scratch_shapes=[pltpu.VMEM((TM, nv), BF16)],
        compiler_params=_cparams("arbitrary"),
        name="mlstm_out",
    )(hf, hr, o, s, mods, norm_g.reshape(1, nv), w_out.astype(BF16))


def _mlstm_layer(pending, mods, norm_g, w_in, conv_w, conv_b, gate_b, ml_norm_g, w_out):
    s, qk, v, o, gt = _ml_proj(pending, mods, norm_g, w_in, gate_b)
    gtt = gt[:, :2 * SUBLANES].T
    hf = _ml_rec(qk, v, gt, gtt, conv_w, conv_b, False)
    hr = _ml_rec(qk, v, gt, gtt, conv_w, conv_b, True)
    return _ml_out(hf, hr, o, s, mods, ml_norm_g, w_out)


def _lru_proj_kernel(dest_ref, s_ref, wt_ref, pmod_ref, ys_ref, mod_ref, g_ref, w_ref,
                     snew_ref, gl_ref, xr_ref, ybuf, sem, **pending_kw):
    i = pl.program_id(0)
    x = _combined_tile(dest_ref, s_ref, wt_ref, pmod_ref, ys_ref, ybuf, sem, **pending_kw)
    snew_ref[...] = x
    shift, scale, _ = _mod_rows(mod_ref, i, 1, 0)
    h = _rms(x, g_ref[...]) * (1.0 + scale) + shift
    z = _bdot(h, w_ref[...])
    w = gl_ref.shape[1]
    gl_ref[...] = _gelu(z[:, :w]).astype(BF16)
    xr_ref[...] = z[:, w:]


def _lru_proj(pending, mods, norm_g, w_in):
    n, d = pending.s.shape
    kw, dest, p_specs, p_args, p_scratch = _pending_operands(pending)
    w = w_in.shape[1] // 2
    tile = lambda c: pl.BlockSpec((TM, c), lambda i, dst: (i, 0))
    whole = lambda shape: pl.BlockSpec(shape, lambda i, dst: (0,) * len(shape))
    return pl.pallas_call(
        functools.partial(_lru_proj_kernel, **kw),
        out_shape=(jax.ShapeDtypeStruct((n, d), F32), jax.ShapeDtypeStruct((n, w), BF16),
                   jax.ShapeDtypeStruct((n, w), F32)),
        grid_spec=pltpu.PrefetchScalarGridSpec(
            num_scalar_prefetch=1,
            grid=(n // TM,),
            in_specs=p_specs + [whole(mods.shape), whole((1, d)), whole(w_in.shape)],
            out_specs=(tile(d), tile(w), tile(w)),
            scratch_shapes=p_scratch),
        compiler_params=_cparams("arbitrary"),
        name="rglru_proj",
    )(dest, *p_args, mods, norm_g.reshape(1, d), w_in.astype(BF16))


def _lru_tile_index(j, n_tiles, reverse):
    if not reverse:
        return j
    return jnp.where(j == 0, 0, n_tiles - j)


def _lru_scan_kernel(x_ref, xp_ref, xn_ref, cw_ref, cb_ref, wg_ref, ba_ref, bx_ref, lam_ref,
                     o_ref, a_scr, u_scr, carry, *, reverse, n_tiles):
    j = pl.program_id(0)
    t = _lru_tile_index(j, n_tiles, reverse)

    @pl.when(j == 0)
    def _():
        carry[...] = jnp.zeros_like(carry)

    has_prev = t > 1
    has_next = (t != 0) & (t != n_tiles - 1)
    xr = _conv_tile(x_ref[...], xp_ref, xn_ref, has_prev, has_next, cw_ref, cb_ref)
    sp = _softplus(-lam_ref[...])
    B = LRU_BLOCK
    for hd in range(LRU_HEADS):
        cs = slice(hd * B, (hd + 1) * B)
        xh = xr[:, cs]
        y = jnp.dot(xh.astype(BF16), wg_ref[hd], preferred_element_type=F32)
        r = _sigmoid(y[:, :B] + ba_ref[:, cs])
        ig = _sigmoid(y[:, B:] + bx_ref[:, cs])
        log_a = -LRU_C * r * sp[:, cs]
        a = jnp.exp(log_a)
        a_scr[:, cs] = a
        v = 1.0 - a * a
        u_scr[:, cs] = jnp.where(v > 0.0, v * lax.rsqrt(v), 0.0) * (ig * xh)

    S = SUBLANES
    w = a_scr.shape[1]
    sidx = lax.broadcasted_iota(jnp.int32, (S, w), 0)

    def group(gi, c):
        g = (TM // S - 1 - gi) if reverse else gi
        r0 = pl.multiple_of(g * S, S)
        a = a_scr[pl.ds(r0, S), :]
        u = u_scr[pl.ds(r0, S), :]
        for sft in (1, 2, 4):
            if reverse:
                ok = sidx < S - sft
                a_e = pltpu.roll(a, S - sft, axis=0)
                u_e = pltpu.roll(u, S - sft, axis=0)
            else:
                ok = sidx >= sft
                a_e = pltpu.roll(a, sft, axis=0)
                u_e = pltpu.roll(u, sft, axis=0)
            u = jnp.where(ok, a * u_e + u, u)
            a = jnp.where(ok, a * a_e, a)
        hcur = a * carry[...] + u
        u_scr[pl.ds(r0, S), :] = hcur
        edge = 0 if reverse else S - 1
        carry[...] = jnp.broadcast_to(hcur[edge:edge + 1, :], (S, w))
        return c

    lax.fori_loop(0, TM // S, group, 0, unroll=4)
    o_ref[...] = u_scr[...].astype(BF16)


def _lru_scan(xraw, conv_w, conv_b, w_a, b_a, w_x, b_x, lam, reverse):
    n, w = xraw.shape
    nt = n // TM
    hb = TM // SUBLANES
    idx = lambda j: _lru_tile_index(j, nt, reverse)
    last8 = n // SUBLANES - 1
    wg = jnp.concatenate([w_a, w_x], axis=-1).astype(BF16)
    return pl.pallas_call(
        functools.partial(_lru_scan_kernel, reverse=reverse, n_tiles=nt),
        out_shape=jax.ShapeDtypeStruct((n, w), BF16),
        grid=(nt,),
        in_specs=[pl.BlockSpec((TM, w), lambda j: (idx(j), 0)),
                  pl.BlockSpec((SUBLANES, w), lambda j: (jnp.maximum(idx(j) * hb - 1, 0), 0)),
                  pl.BlockSpec((SUBLANES, w), lambda j: (jnp.minimum((idx(j) + 1) * hb, last8), 0)),
                  _full((4, w)), _full((1, w)), _full(wg.shape), _full((1, w)), _full((1, w)),
                  _full((1, w))],
        out_specs=pl.BlockSpec((TM, w), lambda j: (idx(j), 0)),
        scratch_shapes=[pltpu.VMEM((TM, w), F32), pltpu.VMEM((TM, w), F32),
                        pltpu.VMEM((SUBLANES, w), F32)],
        compiler_params=_cparams("arbitrary"),
        name="rglru_rev" if reverse else "rglru_fwd",
    )(xraw, xraw, xraw, conv_w, conv_b.reshape(1, w), wg, b_a.reshape(1, w), b_x.reshape(1, w),
      lam.reshape(1, w))


def _lru_out_kernel(gl_ref, hf_ref, hr_ref, s_ref, mod_ref, w_ref, out_ref):
    gate = _mod_rows(mod_ref, 1, 0, 0)[2]
    p = gl_ref[...].astype(F32) * (hf_ref[...].astype(F32) + hr_ref[...].astype(F32))
    out_ref[...] = s_ref[...] + gate * _bdot(p, w_ref[...])


def _lru_out(gl, hf, hr, s, mods, w_out):
    n, d = s.shape
    w = gl.shape[1]
    lat = lambda c: pl.BlockSpec((TM, c), lambda i: (i + 1, 0))
    return pl.pallas_call(
        _lru_out_kernel,
        out_shape=jax.ShapeDtypeStruct((n - TM, d), F32),
        grid=(n // TM - 1,),
        in_specs=[lat(w), lat(w), lat(w), lat(d), _full(mods.shape), _full(w_out.shape)],
        out_specs=pl.BlockSpec((TM, d), lambda i: (i, 0)),
        compiler_params=_cparams("arbitrary"),
        name="rglru_out",
    )(gl, hf, hr, s, mods, w_out.astype(BF16))


def _rglru_layer(pending, mods, norm_g, w_in, conv_w, conv_b, w_a, b_a, w_x, b_x, lam, w_out):
    s, gl, xraw = _lru_proj(pending, mods, norm_g, w_in)
    hf = _lru_scan(xraw, conv_w, conv_b, w_a[0], b_a[0], w_x[0], b_x[0], lam[0], False)
    hr = _lru_scan(xraw, conv_w, conv_b, w_a[1], b_a[1], w_x[1], b_x[1], lam[1], True)
    return _lru_out(gl, hf, hr, s, mods, w_out)


def _fn_proj_kernel(s_ref, mod_ref, g_ref, wt_ref, cs_ref, yr_ref, yi_ref, ar_scr, ai_scr):
    shift, scale, _ = _mod_rows(mod_ref, 1, 0, 0)
    nm = wt_ref.shape[0]
    gw = nm // FN_GROUPS
    per = TM // FFT_N2
    nj = FN_TB // FFT_N2
    csb = cs_ref[...].astype(BF16)

    def sub(tc, c):
        r0 = pl.multiple_of(tc * TM, TM)
        h = _rms(s_ref[pl.ds(r0, TM), :], g_ref[...]) * (1.0 + scale) + shift
        zt = lax.dot_general(wt_ref[...], h.astype(BF16), (((1,), (1,)), ((), ())),
                             preferred_element_type=F32).astype(BF16)
        for g in range(FN_GROUPS):
            y = jnp.dot(csb, zt[g * gw:(g + 1) * gw, :], preferred_element_type=F32)
            for q in range(per):
                row0 = pl.multiple_of((tc * per + q) * _slab_pitch(nm) + g * gw, SUBLANES)
                ar_scr[pl.ds(row0, gw), :] = y[:gw, q * FFT_N2:(q + 1) * FFT_N2]
                ai_scr[pl.ds(row0, gw), :] = y[gw:, q * FFT_N2:(q + 1) * FFT_N2]
        return c

    lax.fori_loop(0, FN_TB // TM, sub, 0)

    def relayout(m, c):
        yr_ref[m] = ar_scr[pl.ds(m, nj, stride=_slab_pitch(nm)), :]
        yi_ref[m] = ai_scr[pl.ds(m, nj, stride=_slab_pitch(nm)), :]
        return c

    lax.fori_loop(0, nm, relayout, 0, unroll=8)


def _slab_pitch(rows):
    return rows + SUBLANES


def _dft_cos_sin(n, scale):
    k = np.arange(n, dtype=np.int64)
    ang = 2.0 * np.pi * ((k[:, None] * k[None, :]) % n).astype(np.float64) / n
    return np.cos(ang) * scale, np.sin(ang) * scale


def _fn_proj(s, mods, norm_g, w_in):
    t, d = s.shape
    nm = w_in.shape[1]
    gw = nm // FN_GROUPS
    n1 = t // FFT_N2
    nj = FN_TB // FFT_N2
    c, sn = _dft_cos_sin(gw, gw ** -0.5)
    cs = jnp.asarray(np.concatenate([c, -sn], axis=0), F32)
    yspec = pl.BlockSpec((nm, nj, FFT_N2), lambda i: (0, i, 0))
    yshape = jax.ShapeDtypeStruct((nm, n1, FFT_N2), F32)
    return pl.pallas_call(
        _fn_proj_kernel,
        out_shape=(yshape, yshape),
        grid=(t // FN_TB,),
        in_specs=[pl.BlockSpec((FN_TB, d), lambda i: (i, 0)), _full(mods.shape), _full((1, d)),
                  _full((nm, d)), _full(cs.shape)],
        out_specs=(yspec, yspec),
        scratch_shapes=[pltpu.VMEM((nj * _slab_pitch(nm), FFT_N2), F32),
                        pltpu.VMEM((nj * _slab_pitch(nm), FFT_N2), F32)],
        compiler_params=_cparams("arbitrary"),
        name="fourier_proj",
    )(s, mods, norm_g.reshape(1, d), w_in.T.astype(BF16), cs)


def _fn_fft_kernel(yr_ref, yi_ref, m_ref, tc_ref, ts_ref, d_ref, o_ref):
    n1 = yr_ref.shape[1]
    n2 = FFT_N2
    xr = jnp.concatenate([yr_ref[m].astype(BF16) for m in range(FN_CB)], axis=1)
    xi = jnp.concatenate([yi_ref[m].astype(BF16) for m in range(FN_CB)], axis=1)
    a = jnp.dot(m_ref[...].astype(BF16), jnp.concatenate([xr, xi], axis=0),
                preferred_element_type=F32)
    ar = a[:n1]
    ai = a[n1:]
    tc = jnp.concatenate([tc_ref[...]] * FN_CB, axis=1)
    ts = jnp.concatenate([ts_ref[...]] * FN_CB, axis=1)
    br = ar * tc + ai * ts
    bi = ai * tc - ar * ts
    bst = jnp.concatenate(
        [jnp.concatenate([br[:, m * n2:(m + 1) * n2], bi[:, m * n2:(m + 1) * n2]], axis=1)
         for m in range(FN_CB)], axis=0).astype(BF16)
    res = lax.dot_general(d_ref[...].astype(BF16), bst, (((1,), (1,)), ((), ())),
                          preferred_element_type=F32)
    for m in range(FN_CB):
        o_ref[m] = res[:, m * n1:(m + 1) * n1]


def _fn_fft(yr, yi):
    nm, n1, n2 = yr.shape
    t = n1 * n2
    c, sn = _dft_cos_sin(n1, n1 ** -0.5)
    m = jnp.asarray(np.block([[c, sn], [-sn, c]]), F32)
    k1 = np.arange(n1, dtype=np.int64)[:, None]
    t2 = np.arange(n2, dtype=np.int64)[None, :]
    ang = 2.0 * np.pi * ((k1 * t2) % t).astype(np.float64) / t
    tc = jnp.asarray(np.cos(ang), F32)
    ts = jnp.asarray(np.sin(ang), F32)
    c2, s2 = _dft_cos_sin(n2, n2 ** -0.5)
    dm = jnp.asarray(np.concatenate([c2, s2], axis=1), F32)
    yspec = pl.BlockSpec((FN_CB, n1, n2), lambda i: (i, 0, 0))
    return pl.pallas_call(
        _fn_fft_kernel,
        out_shape=jax.ShapeDtypeStruct((nm, n2, n1), F32),
        grid=(nm // FN_CB,),
        in_specs=[yspec, yspec, _full(m.shape), _full(tc.shape), _full(ts.shape), _full(dm.shape)],
        out_specs=pl.BlockSpec((FN_CB, n2, n1), lambda i: (i, 0, 0)),
        compiler_params=_cparams("arbitrary"),
        name="fourier_fft",
    )(yr, yi, m, tc, ts, dm)


def _fn_out_kernel(ft_ref, w_ref, s_ref, mod_ref, o_ref, a_scr):
    gate = _mod_rows(mod_ref, 1, 0, 0)[2]
    nm, nj, n1 = ft_ref.shape

    def relayout(m, c):
        a_scr[pl.ds(m, nj, stride=_slab_pitch(nm)), :] = ft_ref[m]
        return c

    lax.fori_loop(0, nm, relayout, 0, unroll=8)
    for j in range(nj):
        p0 = j * _slab_pitch(nm)
        slab = a_scr[p0:p0 + nm, :].astype(BF16)
        y = lax.dot_general(slab, w_ref[...], (((0,), (0,)), ((), ())), preferred_element_type=F32)
        rows = slice(j * n1, (j + 1) * n1)
        o_ref[rows, :] = s_ref[rows, :] + gate * y


def _fn_out(ft, s, mods, w_out):
    t, d = s.shape
    nm, n2, n1 = ft.shape
    nj = FN_TB // n1
    tok = pl.BlockSpec((FN_TB, d), lambda i: (i, 0))
    return pl.pallas_call(
        _fn_out_kernel,
        out_shape=jax.ShapeDtypeStruct((t, d), F32),
        grid=(t // FN_TB,),
        in_specs=[pl.BlockSpec((nm, nj, n1), lambda i: (0, i, 0)), _full(w_out.shape), tok,
                  _full(mods.shape)],
        out_specs=tok,
        scratch_shapes=[pltpu.VMEM((nj * _slab_pitch(nm), n1), F32)],
        compiler_params=_cparams("arbitrary"),
        name="fourier_out",
    )(ft, w_out.astype(BF16), s, mods)


def _fourier_layer(s, mods, norm_g, w_in, w_out):
    yr, yi = _fn_proj(s, mods, norm_g, w_in)
    return _fn_out(_fn_fft(yr, yi), s, mods, w_out)


def kernel(x, c, ctx, c_ctx, ada_w, ada_b, norm_mix_g, norm_ffn_g, final_norm_g, router_group_w, router_group_b, router_expert_w, router_expert_b, expert_w_gate, expert_w_up, expert_w_down, cm_w_in, cm_v_norm_g, cm_w_s, cm_b_s, cm_w_out, ml_w_in, ml_conv_w, ml_conv_b, ml_gate_b, ml_norm_g, ml_w_out, lru_w_in, lru_conv_w, lru_conv_b, lru_w_a, lru_b_a, lru_w_x, lru_b_x, lru_lambda, lru_w_out, fn_w_in, fn_w_out):
    bsz, seq, d = x.shape
    assert bsz == 1 and ada_w.shape[0] == 4 and ctx.shape[1] == TM
    c_rows = jnp.concatenate([c_ctx[None, :], c, jnp.zeros((SUBLANES - 2, d), F32)], axis=0)
    mods = _ada_table(c_rows, ada_w, ada_b)

    def moe(s, i, ctx_tiles, final_norm=False, defer=False):
        return _moe_layer(s, mods[i], norm_ffn_g[i], router_group_w[i], router_group_b[i],
                          router_expert_w[i], router_expert_b[i], expert_w_gate, expert_w_up,
                          expert_w_down, i, ctx_tiles, final_norm_g, final_norm, defer)

    s = _chunk_mlp_layer(x[0], ctx[0], mods[0], norm_mix_g[0], cm_w_in[0], cm_v_norm_g[0],
                         cm_w_s[0], cm_b_s[0], cm_w_out[0])
    s = moe(s, 0, 1, defer=True)
    s = _mlstm_layer(s, mods[1], norm_mix_g[1], ml_w_in[0], ml_conv_w[0], ml_conv_b[0],
                     ml_gate_b[0], ml_norm_g[0], ml_w_out[0])
    s = moe(s, 1, 1, defer=True)
    s = _rglru_layer(s, mods[2], norm_mix_g[2], lru_w_in[0], lru_conv_w[0], lru_conv_b[0],
                     lru_w_a[0], lru_b_a[0], lru_w_x[0], lru_b_x[0], lru_lambda[0], lru_w_out[0])
    s = moe(s, 2, 0)
    s = _fourier_layer(s, mods[3], norm_mix_g[3], fn_w_in[0], fn_w_out[0])
    s = moe(s, 3, 0, final_norm=True)
    return s[None]
```
